```python
import math
import jax, jax.numpy as jnp
from jax import lax
import numpy as np

D_MODEL = 1024
BATCH = 8
SEQ = 4096
DEPTH = 2

N_MEM = 256
RMS_EPS = 1e-6
MAX_POS_OFFSET = 1024
MASK_VALUE = -1e30

SSM_GROUPS = 32
SSM_GROUP_CH = 16
SSM_WIDTH = SSM_GROUPS * SSM_GROUP_CH
SSM_STATE = 64

MLA_HEADS = 8
MLA_Q_RANK = 512
MLA_KV_RANK = 256
MLA_NOPE = 64
MLA_ROPE = 32
MLA_V = 64
MLA_WIDTH = MLA_HEADS * MLA_V
ROPE_THETA = 10000.0
Q_BLOCK = 128

HG_HEADS = 4
HG_DK = 128
HG_DV = 128
HG_WIDTH = HG_HEADS * HG_DK
HG_CHUNK = 64

X_HEADS = 4
X_HEAD_DIM = 128
X_WIDTH = X_HEADS * X_HEAD_DIM

D_FF = -(-(8 * D_MODEL) // (3 * 256)) * 256

N_BRANCH = 3
IN_SPLITS = [SSM_WIDTH, MLA_Q_RANK, MLA_KV_RANK, MLA_ROPE,
             HG_HEADS * HG_DK, HG_HEADS * HG_DK, HG_HEADS * HG_DV, HG_HEADS * HG_DV,
             N_BRANCH * D_MODEL]
D_IN = sum(IN_SPLITS)

kernel_name = "hybrid_s5_mla_hgrn2_gated_block"


def rmsnorm(x, g):
    xf = x.astype(jnp.float32)
    y = xf * lax.rsqrt(jnp.mean(xf * xf, axis=-1, keepdims=True) + RMS_EPS)
    return (y * g.astype(jnp.float32)).astype(x.dtype)


def rope_tables(positions):
    half = MLA_ROPE // 2
    inv_freq = ROPE_THETA ** (-jnp.arange(half, dtype=jnp.float32) / half)
    ang = positions.astype(jnp.float32)[..., None] * inv_freq
    return jnp.cos(ang), jnp.sin(ang)


def apply_rope(x, cos, sin):
    xf = x.astype(jnp.float32)
    x1, x2 = jnp.split(xf, 2, axis=-1)
    return jnp.concatenate([x1 * cos - x2 * sin, x2 * cos + x1 * sin], axis=-1).astype(x.dtype)


def s5_mixer(u, lam_re, lam_im, b_re, b_im, c_re, c_im, d_skip, log_step, w_glu):
    bsz, s, _ = u.shape
    uf = u.astype(jnp.float32).reshape(bsz, s, SSM_GROUPS, SSM_GROUP_CH)
    lam = lax.complex(lam_re.astype(jnp.float32), lam_im.astype(jnp.float32))
    step = jnp.exp(log_step.astype(jnp.float32))[:, None]
    lam_bar = jnp.exp(lam * step)
    b_mat = lax.complex(b_re.astype(jnp.float32), b_im.astype(jnp.float32))
    b_bar = ((lam_bar - 1.0) / lam)[..., None] * b_mat
    bu = jnp.einsum('bsgh,gph->bsgp', uf.astype(jnp.complex64), b_bar)
    a = jnp.broadcast_to(lam_bar, bu.shape)

    def combine(left, right):
        a_l, b_l = left
        a_r, b_r = right
        return a_r * a_l, a_r * b_l + b_r

    _, states = lax.associative_scan(combine, (a, bu), axis=1)
    c_mat = lax.complex(c_re.astype(jnp.float32), c_im.astype(jnp.float32))
    y = jnp.real(jnp.einsum('bsgp,ghp->bsgh', states, c_mat)) + d_skip.astype(jnp.float32) * uf
    y = jax.nn.gelu(y.reshape(bsz, s, SSM_WIDTH)).astype(u.dtype)
    z_out, z_gate = jnp.split(y @ w_glu, 2, axis=-1)
    return z_out * jax.nn.sigmoid(z_gate)


def blocked_causal_attention(q, k, v, scale):
    bsz, s, h, dqk = q.shape
    nb = s // Q_BLOCK
    qb = q.reshape(bsz, nb, Q_BLOCK, h, dqk).transpose(1, 0, 2, 3, 4)
    kpos = jnp.arange(s)

    def one_block(args):
        q_blk, start = args
        sc = jnp.einsum('bqhd,bkhd->bhqk', q_blk, k, preferred_element_type=jnp.float32) * scale
        qpos = start + jnp.arange(Q_BLOCK)
        sc = jnp.where(kpos[None, :] <= qpos[:, None], sc, MASK_VALUE)
        p = jax.nn.softmax(sc, axis=-1).astype(v.dtype)
        return jnp.einsum('bhqk,bkhd->bqhd', p, v)

    ob = lax.map(one_block, (qb, jnp.arange(nb) * Q_BLOCK))
    return ob.transpose(1, 0, 2, 3, 4).reshape(bsz, s, h, v.shape[-1])


def mla_mixer(q_lat, kv_lat, k_rope, cos, sin, q_norm, kv_norm, w_uq, w_ukv, w_o):
    bsz, s, _ = q_lat.shape
    q = (rmsnorm(q_lat, q_norm) @ w_uq).reshape(bsz, s, MLA_HEADS, MLA_NOPE + MLA_ROPE)
    q_nope, q_pe = q[..., :MLA_NOPE], q[..., MLA_NOPE:]
    q_pe = apply_rope(q_pe, cos[:, :, None, :], sin[:, :, None, :])
    kv = (rmsnorm(kv_lat, kv_norm) @ w_ukv).reshape(bsz, s, MLA_HEADS, MLA_NOPE + MLA_V)
    k_nope, v = kv[..., :MLA_NOPE], kv[..., MLA_NOPE:]
    k_pe = apply_rope(k_rope, cos, sin)
    k = jnp.concatenate([k_nope, jnp.broadcast_to(k_pe[:, :, None, :], (bsz, s, MLA_HEADS, MLA_ROPE))], axis=-1)
    q = jnp.concatenate([q_nope, q_pe], axis=-1)
    o = blocked_causal_attention(q, k, v, 1.0 / math.sqrt(MLA_NOPE + MLA_ROPE))
    return o.reshape(bsz, s, MLA_WIDTH) @ w_o


def hgrn2_mixer(q, f_logit, i_in, g, lb, g_norm, w_o):
    bsz, s, _ = q.shape
    n_chunks = s // HG_CHUNK

    def to_chunks(t):
        return t.reshape(bsz, n_chunks, HG_CHUNK, HG_HEADS, -1).transpose(1, 0, 3, 2, 4)

    lbf = lb.astype(jnp.float32)
    sig = jax.nn.sigmoid(f_logit.astype(jnp.float32))
    f = lbf + (1.0 - lbf) * sig
    log_f = jnp.log(f)
    k = 1.0 - f
    xs = (to_chunks(jax.nn.silu(q.astype(jnp.float32))), to_chunks(k),
          to_chunks(i_in.astype(jnp.float32)), to_chunks(log_f))
    causal = jnp.tril(jnp.ones((HG_CHUNK, HG_CHUNK), dtype=bool))[:, :, None]

    def chunk_step(state, chunk):
        q_n, k_n, v_n, lf_n = chunk
        b = jnp.cumsum(lf_n, axis=2)
        diff = b[:, :, :, None, :] - b[:, :, None, :, :]
        decay = jnp.where(causal, jnp.exp(jnp.where(causal, diff, 0.0)), 0.0)
        attn = jnp.einsum('bhtd,bhsd,bhtsd->bhts', q_n, k_n, decay)
        o_n = jnp.einsum('bhts,bhse->bhte', attn, v_n) + jnp.einsum('bhtd,bhde->bhte', q_n * jnp.exp(b), state)
        b_last = b[:, :, -1, :]
        k_dec = k_n * jnp.exp(b_last[:, :, None, :] - b)
        state = jnp.exp(b_last)[..., None] * state + jnp.einsum('bhsd,bhse->bhde', k_dec, v_n)
        return state, o_n

    s0 = jnp.zeros((bsz, HG_HEADS, HG_DK, HG_DV), jnp.float32)
    _, o = lax.scan(chunk_step, s0, xs)
    o = o.transpose(1, 0, 3, 2, 4).reshape(bsz, s, HG_HEADS, HG_DV)
    gate = g.astype(jnp.float32).reshape(bsz, s, HG_HEADS, HG_DV)
    o = rmsnorm(o, g_norm) * jax.nn.silu(gate)
    return o.reshape(bsz, s, HG_HEADS * HG_DV).astype(g.dtype) @ w_o


def memory_cross_attention(h, mem_n, w_q, w_kv, w_o):
    bsz, s, _ = h.shape
    m = mem_n.shape[1]
    q = (h @ w_q).reshape(bsz, s, X_HEADS, X_HEAD_DIM)
    k, v = jnp.split(mem_n @ w_kv, 2, axis=-1)
    k = k.reshape(bsz, m, X_HEADS, X_HEAD_DIM)
    v = v.reshape(bsz, m, X_HEADS, X_HEAD_DIM)
    sc = jnp.einsum('bqhd,bkhd->bhqk', q, k, preferred_element_type=jnp.float32) / math.sqrt(X_HEAD_DIM)
    p = jax.nn.softmax(sc, axis=-1).astype(v.dtype)
    o = jnp.einsum('bhqk,bkhd->bqhd', p, v)
    return o.reshape(bsz, s, X_WIDTH) @ w_o


def swiglu_ffn(h, w_gate_up, w_down):
    gt, up = jnp.split(h @ w_gate_up, 2, axis=-1)
    return (jax.nn.silu(gt) * up) @ w_down


def _fwd_setup_inputs(seed: int = 0) -> dict:
    key = jax.random.key(seed)
    keys = list(jax.random.split(key, 48))
    L = DEPTH

    def nrm(shape, scale):
        return scale * jax.random.normal(keys.pop(), shape, jnp.float32)

    def gain(shape):
        return 1.0 + 0.05 * jax.random.normal(keys.pop(), shape, jnp.float32)

    x = nrm((BATCH, SEQ, D_MODEL), 1.0)
    mem = nrm((BATCH, N_MEM, D_MODEL), 1.0)
    offsets = jax.random.randint(keys.pop(), (BATCH, 1), 0, MAX_POS_OFFSET, dtype=jnp.int32)
    positions = (offsets + jnp.arange(SEQ, dtype=jnp.int32)[None, :]).astype(jnp.int32)
    state_idx = jnp.arange(SSM_STATE, dtype=jnp.float32)
    return {
        "x": x,
        "mem": mem,
        "positions": positions,
        "norm_mix": gain((L, D_MODEL)),
        "w_in": nrm((L, D_MODEL, D_IN), D_MODEL ** -0.5),
        "ssm_lam_re": -0.5 + nrm((L, SSM_GROUPS, SSM_STATE), 0.01),
        "ssm_lam_im": math.pi * state_idx + nrm((L, SSM_GROUPS, SSM_STATE), 0.01),
        "ssm_b_re": nrm((L, SSM_GROUPS, SSM_STATE, SSM_GROUP_CH), (2.0 * SSM_GROUP_CH) ** -0.5),
        "ssm_b_im": nrm((L, SSM_GROUPS, SSM_STATE, SSM_GROUP_CH), (2.0 * SSM_GROUP_CH) ** -0.5),
        "ssm_c_re": nrm((L, SSM_GROUPS, SSM_GROUP_CH, SSM_STATE), (2.0 * SSM_STATE) ** -0.5),
        "ssm_c_im": nrm((L, SSM_GROUPS, SSM_GROUP_CH, SSM_STATE), (2.0 * SSM_STATE) ** -0.5),
        "ssm_d": nrm((L, SSM_GROUPS, SSM_GROUP_CH), 1.0),
        "ssm_log_step": jax.random.uniform(keys.pop(), (L, SSM_GROUPS), jnp.float32, math.log(1e-3), math.log(1e-1)),
        "ssm_w_glu": nrm((L, SSM_WIDTH, 2 * D_MODEL), SSM_WIDTH ** -0.5),
        "mla_q_norm": gain((L, MLA_Q_RANK)),
        "mla_kv_norm": gain((L, MLA_KV_RANK)),
        "mla_w_uq": nrm((L, MLA_Q_RANK, MLA_HEADS * (MLA_NOPE + MLA_ROPE)), MLA_Q_RANK ** -0.5),
        "mla_w_ukv": nrm((L, MLA_KV_RANK, MLA_HEADS * (MLA_NOPE + MLA_V)), MLA_KV_RANK ** -0.5),
        "mla_w_o": nrm((L, MLA_WIDTH, D_MODEL), MLA_WIDTH ** -0.5),
        "hg_lb": nrm((L, HG_HEADS * HG_DK), 1.0),
        "hg_g_norm": gain((L, HG_DV)),
        "hg_w_o": nrm((L, HG_HEADS * HG_DV, D_MODEL), (HG_HEADS * HG_DV) ** -0.5),
        "w_out": nrm((L, D_MODEL, D_MODEL), D_MODEL ** -0.5),
        "norm_cross": gain((L, D_MODEL)),
        "norm_mem": gain((L, D_MODEL)),
        "x_w_q": nrm((L, D_MODEL, X_WIDTH), D_MODEL ** -0.5),
        "x_w_kv": nrm((L, D_MODEL, 2 * X_WIDTH), D_MODEL ** -0.5),
        "x_w_o": nrm((L, X_WIDTH, D_MODEL), X_WIDTH ** -0.5),
        "norm_ffn": gain((L, D_MODEL)),
        "ffn_w_gate_up": nrm((L, D_MODEL, 2 * D_FF), D_MODEL ** -0.5),
        "ffn_w_down": nrm((L, D_FF, D_MODEL), D_FF ** -0.5),
        "norm_final": gain((D_MODEL,)),
    }


def _fwd_reference(x, mem, positions, norm_mix, w_in, ssm_lam_re, ssm_lam_im, ssm_b_re, ssm_b_im,
              ssm_c_re, ssm_c_im, ssm_d, ssm_log_step, ssm_w_glu, mla_q_norm, mla_kv_norm,
              mla_w_uq, mla_w_ukv, mla_w_o, hg_lb, hg_g_norm, hg_w_o, w_out, norm_cross, norm_mem,
              x_w_q, x_w_kv, x_w_o, norm_ffn, ffn_w_gate_up, ffn_w_down, norm_final):
    bsz, s, _ = x.shape
    cos, sin = rope_tables(positions)
    lb_p = jax.nn.softmax(hg_lb.astype(jnp.float32), axis=0)
    lower_bounds = jnp.cumsum(lb_p, axis=0) - lb_p[0:1]
    split_at = np.cumsum(IN_SPLITS)[:-1].tolist()

    for l in range(DEPTH):
        h = rmsnorm(x, norm_mix[l])
        (u_ssm, q_lat, kv_lat, k_rope, hg_q, hg_f, hg_i, hg_g, gate_logits) = jnp.split(h @ w_in[l], split_at, axis=-1)
        y_ssm = s5_mixer(u_ssm, ssm_lam_re[l], ssm_lam_im[l], ssm_b_re[l], ssm_b_im[l],
                         ssm_c_re[l], ssm_c_im[l], ssm_d[l], ssm_log_step[l], ssm_w_glu[l])
        y_mla = mla_mixer(q_lat, kv_lat, k_rope, cos, sin, mla_q_norm[l], mla_kv_norm[l],
                          mla_w_uq[l], mla_w_ukv[l], mla_w_o[l])
        y_hg = hgrn2_mixer(hg_q, hg_f, hg_i, hg_g, lower_bounds[l], hg_g_norm[l], hg_w_o[l])
        gates = jax.nn.sigmoid(gate_logits.astype(jnp.float32)).reshape(bsz, s, N_BRANCH, D_MODEL)
        merged = (gates[:, :, 0] * y_ssm.astype(jnp.float32)
                  + gates[:, :, 1] * y_mla.astype(jnp.float32)
                  + gates[:, :, 2] * y_hg.astype(jnp.float32)).astype(x.dtype)
        x = x + merged @ w_out[l]
        x = x + memory_cross_attention(rmsnorm(x, norm_cross[l]), rmsnorm(mem, norm_mem[l]),
                                       x_w_q[l], x_w_kv[l], x_w_o[l])
        x = x + swiglu_ffn(rmsnorm(x, norm_ffn[l]), ffn_w_gate_up[l], ffn_w_down[l])
    return rmsnorm(x, norm_final)


import jax as _jax
import jax.numpy as _jnp

TWIN_FORMAT = 'train_step'
FWD_PARAMS = ['x', 'mem', 'positions', 'norm_mix', 'w_in', 'ssm_lam_re', 'ssm_lam_im', 'ssm_b_re', 'ssm_b_im', 'ssm_c_re', 'ssm_c_im', 'ssm_d', 'ssm_log_step', 'ssm_w_glu', 'mla_q_norm', 'mla_kv_norm', 'mla_w_uq', 'mla_w_ukv', 'mla_w_o', 'hg_lb', 'hg_g_norm', 'hg_w_o', 'w_out', 'norm_cross', 'norm_mem', 'x_w_q', 'x_w_kv', 'x_w_o', 'norm_ffn', 'ffn_w_gate_up', 'ffn_w_down', 'norm_final']
TWIN_WEIGHTS = ['norm_mix', 'w_in', 'ssm_lam_re', 'ssm_lam_im', 'ssm_b_re', 'ssm_b_im', 'ssm_c_re', 'ssm_c_im', 'ssm_d', 'ssm_log_step', 'ssm_w_glu', 'mla_q_norm', 'mla_kv_norm', 'mla_w_uq', 'mla_w_ukv', 'mla_w_o', 'hg_lb', 'hg_g_norm', 'hg_w_o', 'w_out', 'norm_cross', 'norm_mem', 'x_w_q', 'x_w_kv', 'x_w_o', 'norm_ffn', 'ffn_w_gate_up', 'ffn_w_down', 'norm_final']
TWIN_DIFF_INPUT = 'x'
TWIN_INPUTS = ['x', 'mem', 'positions', 'norm_mix', 'w_in', 'ssm_lam_re', 'ssm_lam_im', 'ssm_b_re', 'ssm_b_im', 'ssm_c_re', 'ssm_c_im', 'ssm_d', 'ssm_log_step', 'ssm_w_glu', 'mla_q_norm', 'mla_kv_norm', 'mla_w_uq', 'mla_w_ukv', 'mla_w_o', 'hg_lb', 'hg_g_norm', 'hg_w_o', 'w_out', 'norm_cross', 'norm_mem', 'x_w_q', 'x_w_kv', 'x_w_o', 'norm_ffn', 'ffn_w_gate_up', 'ffn_w_down', 'norm_final', 'loss_target', 'm_norm_mix', 'm_w_in', 'm_ssm_lam_re', 'm_ssm_lam_im', 'm_ssm_b_re', 'm_ssm_b_im', 'm_ssm_c_re', 'm_ssm_c_im', 'm_ssm_d', 'm_ssm_log_step', 'm_ssm_w_glu', 'm_mla_q_norm', 'm_mla_kv_norm', 'm_mla_w_uq', 'm_mla_w_ukv', 'm_mla_w_o', 'm_hg_lb', 'm_hg_g_norm', 'm_hg_w_o', 'm_w_out', 'm_norm_cross', 'm_norm_mem', 'm_x_w_q', 'm_x_w_kv', 'm_x_w_o', 'm_norm_ffn', 'm_ffn_w_gate_up', 'm_ffn_w_down', 'm_norm_final', 'v_norm_mix', 'v_w_in', 'v_ssm_lam_re', 'v_ssm_lam_im', 'v_ssm_b_re', 'v_ssm_b_im', 'v_ssm_c_re', 'v_ssm_c_im', 'v_ssm_d', 'v_ssm_log_step', 'v_ssm_w_glu', 'v_mla_q_norm', 'v_mla_kv_norm', 'v_mla_w_uq', 'v_mla_w_ukv', 'v_mla_w_o', 'v_hg_lb', 'v_hg_g_norm', 'v_hg_w_o', 'v_w_out', 'v_norm_cross', 'v_norm_mem', 'v_x_w_q', 'v_x_w_kv', 'v_x_w_o', 'v_norm_ffn', 'v_ffn_w_gate_up', 'v_ffn_w_down', 'v_norm_final']
TWIN_OUTPUTS = ['loss', 'grad_x', 'grad_norm_mix', 'grad_w_in', 'grad_ssm_lam_re', 'grad_ssm_lam_im', 'grad_ssm_b_re', 'grad_ssm_b_im', 'grad_ssm_c_re', 'grad_ssm_c_im', 'grad_ssm_d', 'grad_ssm_log_step', 'grad_ssm_w_glu', 'grad_mla_q_norm', 'grad_mla_kv_norm', 'grad_mla_w_uq', 'grad_mla_w_ukv', 'grad_mla_w_o', 'grad_hg_lb', 'grad_hg_g_norm', 'grad_hg_w_o', 'grad_w_out', 'grad_norm_cross', 'grad_norm_mem', 'grad_x_w_q', 'grad_x_w_kv', 'grad_x_w_o', 'grad_norm_ffn', 'grad_ffn_w_gate_up', 'grad_ffn_w_down', 'grad_norm_final', 'delta_norm_mix', 'delta_w_in', 'delta_ssm_lam_re', 'delta_ssm_lam_im', 'delta_ssm_b_re', 'delta_ssm_b_im', 'delta_ssm_c_re', 'delta_ssm_c_im', 'delta_ssm_d', 'delta_ssm_log_step', 'delta_ssm_w_glu', 'delta_mla_q_norm', 'delta_mla_kv_norm', 'delta_mla_w_uq', 'delta_mla_w_ukv', 'delta_mla_w_o', 'delta_hg_lb', 'delta_hg_g_norm', 'delta_hg_w_o', 'delta_w_out', 'delta_norm_cross', 'delta_norm_mem', 'delta_x_w_q', 'delta_x_w_kv', 'delta_x_w_o', 'delta_norm_ffn', 'delta_ffn_w_gate_up', 'delta_ffn_w_down', 'delta_norm_final', 'new_m_norm_mix', 'new_m_w_in', 'new_m_ssm_lam_re', 'new_m_ssm_lam_im', 'new_m_ssm_b_re', 'new_m_ssm_b_im', 'new_m_ssm_c_re', 'new_m_ssm_c_im', 'new_m_ssm_d', 'new_m_ssm_log_step', 'new_m_ssm_w_glu', 'new_m_mla_q_norm', 'new_m_mla_kv_norm', 'new_m_mla_w_uq', 'new_m_mla_w_ukv', 'new_m_mla_w_o', 'new_m_hg_lb', 'new_m_hg_g_norm', 'new_m_hg_w_o', 'new_m_w_out', 'new_m_norm_cross', 'new_m_norm_mem', 'new_m_x_w_q', 'new_m_x_w_kv', 'new_m_x_w_o', 'new_m_norm_ffn', 'new_m_ffn_w_gate_up', 'new_m_ffn_w_down', 'new_m_norm_final', 'new_v_norm_mix', 'new_v_w_in', 'new_v_ssm_lam_re', 'new_v_ssm_lam_im', 'new_v_ssm_b_re', 'new_v_ssm_b_im', 'new_v_ssm_c_re', 'new_v_ssm_c_im', 'new_v_ssm_d', 'new_v_ssm_log_step', 'new_v_ssm_w_glu', 'new_v_mla_q_norm', 'new_v_mla_kv_norm', 'new_v_mla_w_uq', 'new_v_mla_w_ukv', 'new_v_mla_w_o', 'new_v_hg_lb', 'new_v_hg_g_norm', 'new_v_hg_w_o', 'new_v_w_out', 'new_v_norm_cross', 'new_v_norm_mem', 'new_v_x_w_q', 'new_v_x_w_kv', 'new_v_x_w_o', 'new_v_norm_ffn', 'new_v_ffn_w_gate_up', 'new_v_ffn_w_down', 'new_v_norm_final']
TWIN_LEAF_KINDS = {'loss': 'loss', 'grad_x': 'grad_x', 'grad_norm_mix': 'grad_w', 'grad_w_in': 'grad_w', 'grad_ssm_lam_re': 'grad_w', 'grad_ssm_lam_im': 'grad_w', 'grad_ssm_b_re': 'grad_w', 'grad_ssm_b_im': 'grad_w', 'grad_ssm_c_re': 'grad_w', 'grad_ssm_c_im': 'grad_w', 'grad_ssm_d': 'grad_w', 'grad_ssm_log_step': 'grad_w', 'grad_ssm_w_glu': 'grad_w', 'grad_mla_q_norm': 'grad_w', 'grad_mla_kv_norm': 'grad_w', 'grad_mla_w_uq': 'grad_w', 'grad_mla_w_ukv': 'grad_w', 'grad_mla_w_o': 'grad_w', 'grad_hg_lb': 'grad_w', 'grad_hg_g_norm': 'grad_w', 'grad_hg_w_o': 'grad_w', 'grad_w_out': 'grad_w', 'grad_norm_cross': 'grad_w', 'grad_norm_mem': 'grad_w', 'grad_x_w_q': 'grad_w', 'grad_x_w_kv': 'grad_w', 'grad_x_w_o': 'grad_w', 'grad_norm_ffn': 'grad_w', 'grad_ffn_w_gate_up': 'grad_w', 'grad_ffn_w_down': 'grad_w', 'grad_norm_final': 'grad_w', 'delta_norm_mix': 'delta_w', 'delta_w_in': 'delta_w', 'delta_ssm_lam_re': 'delta_w', 'delta_ssm_lam_im': 'delta_w', 'delta_ssm_b_re': 'delta_w', 'delta_ssm_b_im': 'delta_w', 'delta_ssm_c_re': 'delta_w', 'delta_ssm_c_im': 'delta_w', 'delta_ssm_d': 'delta_w', 'delta_ssm_log_step': 'delta_w', 'delta_ssm_w_glu': 'delta_w', 'delta_mla_q_norm': 'delta_w', 'delta_mla_kv_norm': 'delta_w', 'delta_mla_w_uq': 'delta_w', 'delta_mla_w_ukv': 'delta_w', 'delta_mla_w_o': 'delta_w', 'delta_hg_lb': 'delta_w', 'delta_hg_g_norm': 'delta_w', 'delta_hg_w_o': 'delta_w', 'delta_w_out': 'delta_w', 'delta_norm_cross': 'delta_w', 'delta_norm_mem': 'delta_w', 'delta_x_w_q': 'delta_w', 'delta_x_w_kv': 'delta_w', 'delta_x_w_o': 'delta_w', 'delta_norm_ffn': 'delta_w', 'delta_ffn_w_gate_up': 'delta_w', 'delta_ffn_w_down': 'delta_w', 'delta_norm_final': 'delta_w', 'new_m_norm_mix': 'new_m', 'new_m_w_in': 'new_m', 'new_m_ssm_lam_re': 'new_m', 'new_m_ssm_lam_im': 'new_m', 'new_m_ssm_b_re': 'new_m', 'new_m_ssm_b_im': 'new_m', 'new_m_ssm_c_re': 'new_m', 'new_m_ssm_c_im': 'new_m', 'new_m_ssm_d': 'new_m', 'new_m_ssm_log_step': 'new_m', 'new_m_ssm_w_glu': 'new_m', 'new_m_mla_q_norm': 'new_m', 'new_m_mla_kv_norm': 'new_m', 'new_m_mla_w_uq': 'new_m', 'new_m_mla_w_ukv': 'new_m', 'new_m_mla_w_o': 'new_m', 'new_m_hg_lb': 'new_m', 'new_m_hg_g_norm': 'new_m', 'new_m_hg_w_o': 'new_m', 'new_m_w_out': 'new_m', 'new_m_norm_cross': 'new_m', 'new_m_norm_mem': 'new_m', 'new_m_x_w_q': 'new_m', 'new_m_x_w_kv': 'new_m', 'new_m_x_w_o': 'new_m', 'new_m_norm_ffn': 'new_m', 'new_m_ffn_w_gate_up': 'new_m', 'new_m_ffn_w_down': 'new_m', 'new_m_norm_final': 'new_m', 'new_v_norm_mix': 'new_v', 'new_v_w_in': 'new_v', 'new_v_ssm_lam_re': 'new_v', 'new_v_ssm_lam_im': 'new_v', 'new_v_ssm_b_re': 'new_v', 'new_v_ssm_b_im': 'new_v', 'new_v_ssm_c_re': 'new_v', 'new_v_ssm_c_im': 'new_v', 'new_v_ssm_d': 'new_v', 'new_v_ssm_log_step': 'new_v', 'new_v_ssm_w_glu': 'new_v', 'new_v_mla_q_norm': 'new_v', 'new_v_mla_kv_norm': 'new_v', 'new_v_mla_w_uq': 'new_v', 'new_v_mla_w_ukv': 'new_v', 'new_v_mla_w_o': 'new_v', 'new_v_hg_lb': 'new_v', 'new_v_hg_g_norm': 'new_v', 'new_v_hg_w_o': 'new_v', 'new_v_w_out': 'new_v', 'new_v_norm_cross': 'new_v', 'new_v_norm_mem': 'new_v', 'new_v_x_w_q': 'new_v', 'new_v_x_w_kv': 'new_v', 'new_v_x_w_o': 'new_v', 'new_v_norm_ffn': 'new_v', 'new_v_ffn_w_gate_up': 'new_v', 'new_v_ffn_w_down': 'new_v', 'new_v_norm_final': 'new_v'}


def _forward(args):
    return _fwd_reference(*[args[k] for k in FWD_PARAMS])


def _output_shape():
    def fwd():
        inp = _fwd_setup_inputs(0)
        return _fwd_reference(*[inp[k] for k in FWD_PARAMS])
    out = _jax.eval_shape(fwd)
    return out.shape, out.dtype

N_MICROBATCH = 1
ADAM_LR = 0.001
ADAM_B1 = 0.9
ADAM_B2 = 0.999
ADAM_EPS = 1e-08
ADAM_WD = 0.01
ADAM_STEP = 10
PER_EXAMPLE_BATCH_AXIS = {'x': 0, 'mem': 0, 'positions': 0, 'loss_target': 0}
SHARED_INPUTS = []
_WEIGHT_DTYPES = {'norm_mix': _jnp.float32, 'w_in': _jnp.float32, 'ssm_lam_re': _jnp.float32, 'ssm_lam_im': _jnp.float32, 'ssm_b_re': _jnp.float32, 'ssm_b_im': _jnp.float32, 'ssm_c_re': _jnp.float32, 'ssm_c_im': _jnp.float32, 'ssm_d': _jnp.float32, 'ssm_log_step': _jnp.float32, 'ssm_w_glu': _jnp.float32, 'mla_q_norm': _jnp.float32, 'mla_kv_norm': _jnp.float32, 'mla_w_uq': _jnp.float32, 'mla_w_ukv': _jnp.float32, 'mla_w_o': _jnp.float32, 'hg_lb': _jnp.float32, 'hg_g_norm': _jnp.float32, 'hg_w_o': _jnp.float32, 'w_out': _jnp.float32, 'norm_cross': _jnp.float32, 'norm_mem': _jnp.float32, 'x_w_q': _jnp.float32, 'x_w_kv': _jnp.float32, 'x_w_o': _jnp.float32, 'norm_ffn': _jnp.float32, 'ffn_w_gate_up': _jnp.float32, 'ffn_w_down': _jnp.float32, 'norm_final': _jnp.float32}
MOMENT_SCALE = {'norm_mix': 1.032730e-01, 'w_in': 4.129989e-02, 'ssm_lam_re': 2.553106e-03, 'ssm_lam_im': 2.688267e-03, 'ssm_b_re': 1.691992e-03, 'ssm_b_im': 1.700635e-03, 'ssm_c_re': 3.376447e-03, 'ssm_c_im': 3.383515e-03, 'ssm_d': 6.006047e-02, 'ssm_log_step': 1.871774e+00, 'ssm_w_glu': 2.940594e-02, 'mla_q_norm': 2.483696e-02, 'mla_kv_norm': 5.669900e-02, 'mla_w_uq': 2.080075e-02, 'mla_w_ukv': 2.763124e-02, 'mla_w_o': 2.311038e-02, 'hg_lb': 6.011792e-03, 'hg_g_norm': 1.867251e-01, 'hg_w_o': 5.990772e-02, 'w_out': 7.315590e-02, 'norm_cross': 1.939017e-02, 'norm_mem': 2.978209e-02, 'x_w_q': 2.728112e-02, 'x_w_kv': 2.842996e-02, 'x_w_o': 2.057609e-02, 'norm_ffn': 1.262291e-01, 'ffn_w_gate_up': 5.399085e-02, 'ffn_w_down': 8.856624e-02, 'norm_final': 3.208380e+01}


def _to_microbatches(a, axis):
    t = _jnp.moveaxis(a, axis, 0)
    t = t.reshape((N_MICROBATCH, t.shape[0] // N_MICROBATCH) + t.shape[1:])
    return _jnp.moveaxis(t, 1, axis + 1)


def setup_inputs(seed: int = 0) -> dict:
    inp = _fwd_setup_inputs(seed)
    key = _jax.random.fold_in(_jax.random.key(seed), 7919)
    shape, _ = _output_shape()
    out = dict(inp)
    out["loss_target"] = _jax.random.normal(_jax.random.fold_in(key, 0), shape, _jnp.float32)
    for i, name in enumerate(TWIN_WEIGHTS):
        w = inp[name].astype(_jnp.float32)
        if MOMENT_SCALE is None:
            s = _jnp.sqrt(_jnp.mean(_jnp.square(w)) + 1e-30)
        else:
            s = MOMENT_SCALE[name]
        km, kv = _jax.random.split(_jax.random.fold_in(key, i + 1))
        out[name] = w
        out["m_" + name] = s * _jax.random.normal(km, w.shape, _jnp.float32)
        out["v_" + name] = (s * s) * _jax.random.uniform(kv, w.shape, _jnp.float32, 0.5, 1.5)
    if N_MICROBATCH > 1:
        for name, axis in PER_EXAMPLE_BATCH_AXIS.items():
            out[name] = _to_microbatches(out[name], axis)
    return {'x': out['x'], 'mem': out['mem'], 'positions': out['positions'], 'norm_mix': out['norm_mix'], 'w_in': out['w_in'], 'ssm_lam_re': out['ssm_lam_re'], 'ssm_lam_im': out['ssm_lam_im'], 'ssm_b_re': out['ssm_b_re'], 'ssm_b_im': out['ssm_b_im'], 'ssm_c_re': out['ssm_c_re'], 'ssm_c_im': out['ssm_c_im'], 'ssm_d': out['ssm_d'], 'ssm_log_step': out['ssm_log_step'], 'ssm_w_glu': out['ssm_w_glu'], 'mla_q_norm': out['mla_q_norm'], 'mla_kv_norm': out['mla_kv_norm'], 'mla_w_uq': out['mla_w_uq'], 'mla_w_ukv': out['mla_w_ukv'], 'mla_w_o': out['mla_w_o'], 'hg_lb': out['hg_lb'], 'hg_g_norm': out['hg_g_norm'], 'hg_w_o': out['hg_w_o'], 'w_out': out['w_out'], 'norm_cross': out['norm_cross'], 'norm_mem': out['norm_mem'], 'x_w_q': out['x_w_q'], 'x_w_kv': out['x_w_kv'], 'x_w_o': out['x_w_o'], 'norm_ffn': out['norm_ffn'], 'ffn_w_gate_up': out['ffn_w_gate_up'], 'ffn_w_down': out['ffn_w_down'], 'norm_final': out['norm_final'], 'loss_target': out['loss_target'], 'm_norm_mix': out['m_norm_mix'], 'm_w_in': out['m_w_in'], 'm_ssm_lam_re': out['m_ssm_lam_re'], 'm_ssm_lam_im': out['m_ssm_lam_im'], 'm_ssm_b_re': out['m_ssm_b_re'], 'm_ssm_b_im': out['m_ssm_b_im'], 'm_ssm_c_re': out['m_ssm_c_re'], 'm_ssm_c_im': out['m_ssm_c_im'], 'm_ssm_d': out['m_ssm_d'], 'm_ssm_log_step': out['m_ssm_log_step'], 'm_ssm_w_glu': out['m_ssm_w_glu'], 'm_mla_q_norm': out['m_mla_q_norm'], 'm_mla_kv_norm': out['m_mla_kv_norm'], 'm_mla_w_uq': out['m_mla_w_uq'], 'm_mla_w_ukv': out['m_mla_w_ukv'], 'm_mla_w_o': out['m_mla_w_o'], 'm_hg_lb': out['m_hg_lb'], 'm_hg_g_norm': out['m_hg_g_norm'], 'm_hg_w_o': out['m_hg_w_o'], 'm_w_out': out['m_w_out'], 'm_norm_cross': out['m_norm_cross'], 'm_norm_mem': out['m_norm_mem'], 'm_x_w_q': out['m_x_w_q'], 'm_x_w_kv': out['m_x_w_kv'], 'm_x_w_o': out['m_x_w_o'], 'm_norm_ffn': out['m_norm_ffn'], 'm_ffn_w_gate_up': out['m_ffn_w_gate_up'], 'm_ffn_w_down': out['m_ffn_w_down'], 'm_norm_final': out['m_norm_final'], 'v_norm_mix': out['v_norm_mix'], 'v_w_in': out['v_w_in'], 'v_ssm_lam_re': out['v_ssm_lam_re'], 'v_ssm_lam_im': out['v_ssm_lam_im'], 'v_ssm_b_re': out['v_ssm_b_re'], 'v_ssm_b_im': out['v_ssm_b_im'], 'v_ssm_c_re': out['v_ssm_c_re'], 'v_ssm_c_im': out['v_ssm_c_im'], 'v_ssm_d': out['v_ssm_d'], 'v_ssm_log_step': out['v_ssm_log_step'], 'v_ssm_w_glu': out['v_ssm_w_glu'], 'v_mla_q_norm': out['v_mla_q_norm'], 'v_mla_kv_norm': out['v_mla_kv_norm'], 'v_mla_w_uq': out['v_mla_w_uq'], 'v_mla_w_ukv': out['v_mla_w_ukv'], 'v_mla_w_o': out['v_mla_w_o'], 'v_hg_lb': out['v_hg_lb'], 'v_hg_g_norm': out['v_hg_g_norm'], 'v_hg_w_o': out['v_hg_w_o'], 'v_w_out': out['v_w_out'], 'v_norm_cross': out['v_norm_cross'], 'v_norm_mem': out['v_norm_mem'], 'v_x_w_q': out['v_x_w_q'], 'v_x_w_kv': out['v_x_w_kv'], 'v_x_w_o': out['v_x_w_o'], 'v_norm_ffn': out['v_norm_ffn'], 'v_ffn_w_gate_up': out['v_ffn_w_gate_up'], 'v_ffn_w_down': out['v_ffn_w_down'], 'v_norm_final': out['v_norm_final']}


def _loss(weights, diff, rest, loss_target):
    with _jax.named_scope("forward"):
        args = {**rest, TWIN_DIFF_INPUT: diff, **{k: w.astype(_WEIGHT_DTYPES[k]) for k, w in weights.items()}}
        y = _forward(args)
    with _jax.named_scope("loss_head"):
        err = _jnp.square(y.astype(_jnp.float32) - loss_target)
        return 0.5 * _jnp.sum(_jnp.mean(err, axis=-1)) if err.ndim else 0.5 * err


def _adamw(w, g, m, v):
    m = ADAM_B1 * m + (1.0 - ADAM_B1) * g
    v = ADAM_B2 * v + (1.0 - ADAM_B2) * _jnp.square(g)
    m_hat = m / (1.0 - ADAM_B1 ** ADAM_STEP)
    v_hat = v / (1.0 - ADAM_B2 ** ADAM_STEP)
    delta = -ADAM_LR * (m_hat / (_jnp.sqrt(v_hat) + ADAM_EPS) + ADAM_WD * w)
    return delta, m, v


def reference(x, mem, positions, norm_mix, w_in, ssm_lam_re, ssm_lam_im, ssm_b_re, ssm_b_im, ssm_c_re, ssm_c_im, ssm_d, ssm_log_step, ssm_w_glu, mla_q_norm, mla_kv_norm, mla_w_uq, mla_w_ukv, mla_w_o, hg_lb, hg_g_norm, hg_w_o, w_out, norm_cross, norm_mem, x_w_q, x_w_kv, x_w_o, norm_ffn, ffn_w_gate_up, ffn_w_down, norm_final, loss_target, m_norm_mix, m_w_in, m_ssm_lam_re, m_ssm_lam_im, m_ssm_b_re, m_ssm_b_im, m_ssm_c_re, m_ssm_c_im, m_ssm_d, m_ssm_log_step, m_ssm_w_glu, m_mla_q_norm, m_mla_kv_norm, m_mla_w_uq, m_mla_w_ukv, m_mla_w_o, m_hg_lb, m_hg_g_norm, m_hg_w_o, m_w_out, m_norm_cross, m_norm_mem, m_x_w_q, m_x_w_kv, m_x_w_o, m_norm_ffn, m_ffn_w_gate_up, m_ffn_w_down, m_norm_final, v_norm_mix, v_w_in, v_ssm_lam_re, v_ssm_lam_im, v_ssm_b_re, v_ssm_b_im, v_ssm_c_re, v_ssm_c_im, v_ssm_d, v_ssm_log_step, v_ssm_w_glu, v_mla_q_norm, v_mla_kv_norm, v_mla_w_uq, v_mla_w_ukv, v_mla_w_o, v_hg_lb, v_hg_g_norm, v_hg_w_o, v_w_out, v_norm_cross, v_norm_mem, v_x_w_q, v_x_w_kv, v_x_w_o, v_norm_ffn, v_ffn_w_gate_up, v_ffn_w_down, v_norm_final):
    given = dict(x=x, mem=mem, positions=positions, norm_mix=norm_mix, w_in=w_in, ssm_lam_re=ssm_lam_re, ssm_lam_im=ssm_lam_im, ssm_b_re=ssm_b_re, ssm_b_im=ssm_b_im, ssm_c_re=ssm_c_re, ssm_c_im=ssm_c_im, ssm_d=ssm_d, ssm_log_step=ssm_log_step, ssm_w_glu=ssm_w_glu, mla_q_norm=mla_q_norm, mla_kv_norm=mla_kv_norm, mla_w_uq=mla_w_uq, mla_w_ukv=mla_w_ukv, mla_w_o=mla_w_o, hg_lb=hg_lb, hg_g_norm=hg_g_norm, hg_w_o=hg_w_o, w_out=w_out, norm_cross=norm_cross, norm_mem=norm_mem, x_w_q=x_w_q, x_w_kv=x_w_kv, x_w_o=x_w_o, norm_ffn=norm_ffn, ffn_w_gate_up=ffn_w_gate_up, ffn_w_down=ffn_w_down, norm_final=norm_final, loss_target=loss_target, m_norm_mix=m_norm_mix, m_w_in=m_w_in, m_ssm_lam_re=m_ssm_lam_re, m_ssm_lam_im=m_ssm_lam_im, m_ssm_b_re=m_ssm_b_re, m_ssm_b_im=m_ssm_b_im, m_ssm_c_re=m_ssm_c_re, m_ssm_c_im=m_ssm_c_im, m_ssm_d=m_ssm_d, m_ssm_log_step=m_ssm_log_step, m_ssm_w_glu=m_ssm_w_glu, m_mla_q_norm=m_mla_q_norm, m_mla_kv_norm=m_mla_kv_norm, m_mla_w_uq=m_mla_w_uq, m_mla_w_ukv=m_mla_w_ukv, m_mla_w_o=m_mla_w_o, m_hg_lb=m_hg_lb, m_hg_g_norm=m_hg_g_norm, m_hg_w_o=m_hg_w_o, m_w_out=m_w_out, m_norm_cross=m_norm_cross, m_norm_mem=m_norm_mem, m_x_w_q=m_x_w_q, m_x_w_kv=m_x_w_kv, m_x_w_o=m_x_w_o, m_norm_ffn=m_norm_ffn, m_ffn_w_gate_up=m_ffn_w_gate_up, m_ffn_w_down=m_ffn_w_down, m_norm_final=m_norm_final, v_norm_mix=v_norm_mix, v_w_in=v_w_in, v_ssm_lam_re=v_ssm_lam_re, v_ssm_lam_im=v_ssm_lam_im, v_ssm_b_re=v_ssm_b_re, v_ssm_b_im=v_ssm_b_im, v_ssm_c_re=v_ssm_c_re, v_ssm_c_im=v_ssm_c_im, v_ssm_d=v_ssm_d, v_ssm_log_step=v_ssm_log_step, v_ssm_w_glu=v_ssm_w_glu, v_mla_q_norm=v_mla_q_norm, v_mla_kv_norm=v_mla_kv_norm, v_mla_w_uq=v_mla_w_uq, v_mla_w_ukv=v_mla_w_ukv, v_mla_w_o=v_mla_w_o, v_hg_lb=v_hg_lb, v_hg_g_norm=v_hg_g_norm, v_hg_w_o=v_hg_w_o, v_w_out=v_w_out, v_norm_cross=v_norm_cross, v_norm_mem=v_norm_mem, v_x_w_q=v_x_w_q, v_x_w_kv=v_x_w_kv, v_x_w_o=v_x_w_o, v_norm_ffn=v_norm_ffn, v_ffn_w_gate_up=v_ffn_w_gate_up, v_ffn_w_down=v_ffn_w_down, v_norm_final=v_norm_final)
    weights = {n: given[n] for n in TWIN_WEIGHTS}
    shared = {n: given[n] for n in SHARED_INPUTS}
    per_example = {n: given[n] for n in ['x', 'mem', 'positions']}
    grad_fn = _jax.value_and_grad(_loss, argnums=(0, 1))

    def one_microbatch(ex, loss_target):
        ex = dict(ex)
        diff = ex.pop(TWIN_DIFF_INPUT)
        return grad_fn(weights, diff, {**shared, **ex}, loss_target)

    if N_MICROBATCH == 1:
        loss, (grad_w, grad_x) = one_microbatch(per_example, given["loss_target"])
    else:
        def body(carry, xs):
            loss_sum, grad_sum = carry
            l_k, (gw_k, gx_k) = one_microbatch(xs[0], xs[1])
            with _jax.named_scope("update"):
                return (loss_sum + l_k, _jax.tree.map(_jnp.add, grad_sum, gw_k)), gx_k

        init = (_jnp.zeros((), _jnp.float32), _jax.tree.map(_jnp.zeros_like, weights))
        (loss, grad_w), grad_x = _jax.lax.scan(body, init, (per_example, given["loss_target"]))
    with _jax.named_scope("update"):
        delta_w, new_m, new_v = {}, {}, {}
        for n in TWIN_WEIGHTS:
            delta_w[n], new_m[n], new_v[n] = _adamw(weights[n], grad_w[n], given["m_" + n], given["v_" + n])
    return (loss, grad_x, *[grad_w[n] for n in TWIN_WEIGHTS], *[delta_w[n] for n in TWIN_WEIGHTS],
            *[new_m[n] for n in TWIN_WEIGHTS], *[new_v[n] for n in TWIN_WEIGHTS])
```

```python
import functools
import math

import jax
import jax.numpy as jnp
from jax import lax
from jax.experimental import pallas as pl
from jax.experimental.pallas import tpu as pltpu

F32 = jnp.float32
BF16 = jnp.bfloat16

VMEM_LIMIT_BYTES = 48 * 1024 * 1024
LANES = 128
SUBLANES = 8


def _cparams(*sem):
    return pltpu.CompilerParams(dimension_semantics=sem, vmem_limit_bytes=VMEM_LIMIT_BYTES)


def _pick_tile(n, cands):
    for c in cands:
        if n % c == 0:
            return c
    return n


def _mm_call(a, b, ta, tb, add=None, out_dtype=F32, name="mm"):
    m, k = (a.shape[1], a.shape[0]) if ta else a.shape
    k2, n = (b.shape[1], b.shape[0]) if tb else b.shape
    assert k == k2, (a.shape, b.shape, ta, tb)
    tm = _pick_tile(m, (1024, 512, 256, 128))
    tn = _pick_tile(n, (512, 384, 256, 128))
    tk = _pick_tile(k, (1024, 1408, 512, 384, 256, 128))
    nk = k // tk
    a_spec = pl.BlockSpec((tk, tm), lambda i, j, kk: (kk, i)) if ta else pl.BlockSpec((tm, tk), lambda i, j, kk: (i, kk))
    b_spec = pl.BlockSpec((tn, tk), lambda i, j, kk: (j, kk)) if tb else pl.BlockSpec((tk, tn), lambda i, j, kk: (kk, j))
    o_spec = pl.BlockSpec((tm, tn), lambda i, j, kk: (i, j))
    dn = (((0 if ta else 1,), (1 if tb else 0,)), ((), ()))
    has_add = add is not None

    def body(*refs):
        a_ref, b_ref = refs[0], refs[1]
        c_ref = refs[2] if has_add else None
        o_ref = refs[3] if has_add else refs[2]
        p = lax.dot_general(a_ref[...].astype(BF16), b_ref[...].astype(BF16), dn, preferred_element_type=F32)

        def finish(r):
            if has_add:
                r = r + c_ref[...].astype(F32)
            o_ref[...] = r.astype(out_dtype)

        if nk == 1:
            finish(p)
        else:
            acc_ref = refs[-1]
            kk = pl.program_id(2)

            @pl.when(kk == 0)
            def _():
                acc_ref[...] = p

            @pl.when(kk > 0)
            def _():
                acc_ref[...] += p

            @pl.when(kk == nk - 1)
            def _():
                finish(acc_ref[...])

    in_specs = [a_spec, b_spec] + ([o_spec] if has_add else [])
    args = (a, b) + ((add,) if has_add else ())
    return pl.pallas_call(
        body,
        name=name,
        grid=(m // tm, n // tn, nk),
        in_specs=in_specs,
        out_specs=o_spec,
        out_shape=jax.ShapeDtypeStruct((m, n), out_dtype),
        scratch_shapes=[] if nk == 1 else [pltpu.VMEM((tm, tn), F32)],
        compiler_params=_cparams("parallel", "parallel", "arbitrary"),
    )(*args)


@jax.custom_vjp
def matmul(a, b):
    return _mm_call(a, b, False, False, name="mm_fwd")


def _matmul_fwd(a, b):
    return _mm_call(a, b, False, False, name="mm_fwd"), (a, b)


def _matmul_bwd(res, g):
    a, b = res
    da = _mm_call(g, b, False, True, out_dtype=a.dtype, name="mm_da")
    db = _mm_call(a, g, True, False, out_dtype=b.dtype, name="mm_db")
    return da, db


matmul.defvjp(_matmul_fwd, _matmul_bwd)


@jax.custom_vjp
def matmul_add(a, b, c):
    return _mm_call(a, b, False, False, add=c, name="mm_add_fwd")


def _matmul_add_fwd(a, b, c):
    return _mm_call(a, b, False, False, add=c, name="mm_add_fwd"), (a, b)


def _matmul_add_bwd(res, g):
    a, b = res
    da = _mm_call(g, b, False, True, out_dtype=a.dtype, name="mm_da")
    db = _mm_call(a, g, True, False, out_dtype=b.dtype, name="mm_db")
    return da, db, g


matmul_add.defvjp(_matmul_add_fwd, _matmul_add_bwd)


def rowwise(f, n_rows, n_aux, tile, name):
    def specs(arrs, tiled):
        out = []
        for x in arrs:
            if tiled:
                out.append(pl.BlockSpec((tile, x.shape[1]), lambda i: (i, 0)))
            else:
                out.append(pl.BlockSpec(x.shape, lambda i: (0, 0)))
        return out

    def tile_structs(args):
        rows_aux, params = args[: n_rows + n_aux], args[n_rows + n_aux:]
        return [jax.ShapeDtypeStruct((tile, x.shape[1]), x.dtype) for x in rows_aux] + [
            jax.ShapeDtypeStruct(p.shape, p.dtype) for p in params]

    def fwd_call(*args):
        s = args[0].shape[0]
        outs = jax.eval_shape(f, *tile_structs(args))
        n_in = len(args)

        def body(*refs):
            vals = [r[...] for r in refs[:n_in]]
            res = f(*vals)
            for o_ref, r in zip(refs[n_in:], res):
                o_ref[...] = r.astype(o_ref.dtype)

        return pl.pallas_call(
            body,
            name=name + "_fwd",
            grid=(s // tile,),
            in_specs=specs(args[: n_rows + n_aux], True) + specs(args[n_rows + n_aux:], False),
            out_specs=[pl.BlockSpec((tile, o.shape[1]), lambda i: (i, 0)) for o in outs],
            out_shape=[jax.ShapeDtypeStruct((s, o.shape[1]), o.dtype) for o in outs],
            compiler_params=_cparams("parallel"),
        )(*args)

    def bwd_call(args, gs):
        s = args[0].shape[0]
        rows, aux, params = args[:n_rows], args[n_rows:n_rows + n_aux], args[n_rows + n_aux:]
        n_in, n_g, n_p = len(args), len(gs), len(params)

        def body(*refs):
            vals = [r[...] for r in refs[:n_in]]
            gvals = tuple(r[...] for r in refs[n_in:n_in + n_g])
            out_refs = refs[n_in + n_g:]
            auxv = vals[n_rows:n_rows + n_aux]

            def g_(*rp):
                return tuple(f(*rp[:n_rows], *auxv, *rp[n_rows:]))

            _, vjp = jax.vjp(g_, *vals[:n_rows], *vals[n_rows + n_aux:])
            cts = vjp(gvals)
            for o_ref, ct in zip(out_refs[:n_rows], cts[:n_rows]):
                o_ref[...] = ct.astype(o_ref.dtype)
            if n_p:
                @pl.when(pl.program_id(0) == 0)
                def _():
                    for o_ref in out_refs[n_rows:]:
                        o_ref[...] = jnp.zeros_like(o_ref)

                for o_ref, ct in zip(out_refs[n_rows:], cts[n_rows:]):
                    o_ref[...] += ct.astype(o_ref.dtype)

        return pl.pallas_call(
            body,
            name=name + "_bwd",
            grid=(s // tile,),
            in_specs=specs(rows + aux, True) + specs(params, False) + specs(gs, True),
            out_specs=specs(rows, True) + specs(params, False),
            out_shape=[jax.ShapeDtypeStruct(x.shape, x.dtype) for x in rows + params],
            compiler_params=_cparams("arbitrary" if n_p else "parallel"),
        )(*args, *gs)

    @jax.custom_vjp
    def op(*args):
        return tuple(fwd_call(*args))

    def op_fwd(*args):
        return tuple(fwd_call(*args)), args

    def op_bwd(args, gs):
        cts = bwd_call(tuple(args), tuple(gs))
        rows_ct, par_ct = cts[:n_rows], cts[n_rows:]
        aux_ct = [jnp.zeros_like(a) for a in args[n_rows:n_rows + n_aux]]
        return tuple(rows_ct) + tuple(aux_ct) + tuple(par_ct)

    op.defvjp(op_fwd, op_bwd)
    return op


SCAN_SEGMENTS = SUBLANES
SCAN_TILE_ROWS = 512
SCAN_TILE_LANES = 512


def _scan_specs(s, n):
    tr = min(SCAN_TILE_ROWS, s)
    tn = min(SCAN_TILE_LANES, n)
    return tr, tn, s // tr, n // tn


def _scan_step(ar, ai, xr, xi, br, bi):
    return ar * xr - ai * xi + br, ar * xi + ai * xr + bi


def _scan_finals(b_r, b_i, a, reverse, name):
    s, n = b_r.shape
    tr, tn, nt, nc = _scan_specs(s, n)
    ti = tr // SUBLANES
    tmap = (lambda c, j: (nt - 1 - j, c)) if reverse else (lambda c, j: (j, c))

    def body(br_ref, bi_ref, a_ref, fr_ref, fi_ref, sr, si):
        j = pl.program_id(1)

        @pl.when(j == 0)
        def _():
            sr[...] = jnp.zeros_like(sr)
            si[...] = jnp.zeros_like(si)

        ar = jnp.broadcast_to(a_ref[0:1, :], (SUBLANES, tn))
        ai = jnp.broadcast_to(a_ref[1:2, :], (SUBLANES, tn))

        def step(ii, carry):
            i = (ti - 1 - ii) if reverse else ii
            off = pl.multiple_of(i * SUBLANES, SUBLANES)
            return _scan_step(ar, ai, carry[0], carry[1], br_ref[pl.ds(off, SUBLANES), :], bi_ref[pl.ds(off, SUBLANES), :])

        xr, xi = lax.fori_loop(0, ti, step, (sr[...], si[...]), unroll=4)
        sr[...] = xr
        si[...] = xi

        @pl.when(j == nt - 1)
        def _():
            fr_ref[...] = xr
            fi_ref[...] = xi

    bspec = pl.BlockSpec((tr, tn), tmap)
    fspec = pl.BlockSpec((SUBLANES, tn), lambda c, j: (0, c))
    return pl.pallas_call(
        body,
        name=name,
        grid=(nc, nt),
        in_specs=[bspec, bspec, pl.BlockSpec((2, tn), lambda c, j: (0, c))],
        out_specs=[fspec, fspec],
        out_shape=[jax.ShapeDtypeStruct((SUBLANES, n), F32)] * 2,
        scratch_shapes=[pltpu.VMEM((SUBLANES, tn), F32)] * 2,
        compiler_params=_cparams("parallel", "arbitrary"),
    )(b_r, b_i, a)


def _scan_states(b_r, b_i, a, f_r, f_i, reverse, xs, name):
    s, n = b_r.shape
    tr, tn, nt, nc = _scan_specs(s, n)
    ti = tr // SUBLANES
    seg_len = s // SCAN_SEGMENTS
    assert seg_len & (seg_len - 1) == 0
    with_acc = xs is not None
    tmap = (lambda c, j: (nt - 1 - j, c)) if reverse else (lambda c, j: (j, c))
    order = list(range(SCAN_SEGMENTS))[::-1] if reverse else list(range(SCAN_SEGMENTS))

    def body(*refs):
        br_ref, bi_ref, a_ref, fr_ref, fi_ref = refs[:5]
        pos = 5
        if with_acc:
            xr_ref, xi_ref = refs[5:7]
            pos = 7
        or_ref, oi_ref = refs[pos:pos + 2]
        pos += 2
        if with_acc:
            dr_ref, di_ref = refs[pos:pos + 2]
            pos += 2
        sr, si = refs[pos:pos + 2]
        if with_acc:
            accr, acci = refs[pos + 2:pos + 4]
        j = pl.program_id(1)
        a_r1, a_i1 = a_ref[0:1, :], a_ref[1:2, :]

        @pl.when(j == 0)
        def _():
            pr, pi = a_r1, a_i1
            for _ in range(seg_len.bit_length() - 1):
                pr, pi = pr * pr - pi * pi, 2.0 * pr * pi
            cr = jnp.zeros((1, tn), F32)
            ci = jnp.zeros((1, tn), F32)
            for idx, k in enumerate(order):
                if idx > 0:
                    kp = order[idx - 1]
                    cr, ci = (fr_ref[kp:kp + 1, :] + pr * cr - pi * ci, fi_ref[kp:kp + 1, :] + pr * ci + pi * cr)
                sr[k:k + 1, :] = cr
                si[k:k + 1, :] = ci
            if with_acc:
                accr[...] = jnp.zeros_like(accr)
                acci[...] = jnp.zeros_like(acci)

        ar = jnp.broadcast_to(a_r1, (SUBLANES, tn))
        ai = jnp.broadcast_to(a_i1, (SUBLANES, tn))

        def step(ii, carry):
            i = (ti - 1 - ii) if reverse else ii
            off = pl.multiple_of(i * SUBLANES, SUBLANES)
            rows = pl.ds(off, SUBLANES)
            xr, xi = carry[0], carry[1]
            if with_acc:
                zr, zi = xr_ref[rows, :], xi_ref[rows, :]
                acc = (carry[2] + xr * zr + xi * zi, carry[3] + xi * zr - xr * zi)
            nr, ni = _scan_step(ar, ai, xr, xi, br_ref[rows, :], bi_ref[rows, :])
            or_ref[rows, :] = nr
            oi_ref[rows, :] = ni
            return (nr, ni) + (acc if with_acc else ())

        init = (sr[...], si[...]) + ((accr[...], acci[...]) if with_acc else ())
        out = lax.fori_loop(0, ti, step, init, unroll=4)
        sr[...] = out[0]
        si[...] = out[1]
        if with_acc:
            accr[...] = out[2]
            acci[...] = out[3]

            @pl.when(j == nt - 1)
            def _():
                dr_ref[...] = jnp.sum(out[2], axis=0, keepdims=True)
                di_ref[...] = jnp.sum(out[3], axis=0, keepdims=True)

    bspec = pl.BlockSpec((tr, tn), tmap)
    fspec = pl.BlockSpec((SUBLANES, tn), lambda c, j: (0, c))
    dspec = pl.BlockSpec((1, tn), lambda c, j: (0, c))
    in_specs = [bspec, bspec, pl.BlockSpec((2, tn), lambda c, j: (0, c)), fspec, fspec] + ([bspec, bspec] if with_acc else [])
    out_specs = [bspec, bspec] + ([dspec, dspec] if with_acc else [])
    out_shape = [jax.ShapeDtypeStruct((s, n), F32)] * 2 + ([jax.ShapeDtypeStruct((1, n), F32)] * 2 if with_acc else [])
    scratch = [pltpu.VMEM((SUBLANES, tn), F32)] * (4 if with_acc else 2)
    args = (b_r, b_i, a, f_r, f_i) + (tuple(xs) if with_acc else ())
    return pl.pallas_call(
        body,
        name=name,
        grid=(nc, nt),
        in_specs=in_specs,
        out_specs=out_specs,
        out_shape=out_shape,
        scratch_shapes=scratch,
        compiler_params=_cparams("parallel", "arbitrary"),
    )(*args)


@jax.custom_vjp
def s5_scan(b_r, b_i, a):
    f_r, f_i = _scan_finals(b_r, b_i, a, False, "s5_scan_fin")
    return tuple(_scan_states(b_r, b_i, a, f_r, f_i, False, None, "s5_scan"))


def _s5_scan_fwd(b_r, b_i, a):
    xs = s5_scan(b_r, b_i, a)
    return xs, (a, xs)


def _s5_scan_bwd(res, g):
    a, xs = res
    a_conj = a * jnp.array([[1.0], [-1.0]], F32)
    f_r, f_i = _scan_finals(g[0], g[1], a_conj, True, "s5_rscan_fin")
    g_r, g_i, da_r, da_i = _scan_states(g[0], g[1], a_conj, f_r, f_i, True, xs, "s5_rscan")
    return g_r, g_i, jnp.concatenate([da_r, da_i], axis=0)


s5_scan.defvjp(_s5_scan_fwd, _s5_scan_bwd)


def _to_segments(x):
    s, f = x.shape
    return x.reshape(SCAN_SEGMENTS, s // SCAN_SEGMENTS, f).transpose(1, 0, 2).reshape(s, f)


def _from_segments(x):
    s, f = x.shape
    return x.reshape(s // SCAN_SEGMENTS, SCAN_SEGMENTS, f).transpose(1, 0, 2).reshape(s, f)


_NN = (((1,), (0,)), ((), ()))
_NT = (((1,), (1,)), ((), ()))
_TN = (((0,), (0,)), ((), ()))


def _dot(a, b, dn):
    return lax.dot_general(a.astype(BF16), b.astype(BF16), dn, preferred_element_type=F32)


@jax.custom_vjp
def bdot_nn(a, b):
    return _dot(a, b, _NN)


bdot_nn.defvjp(lambda a, b: (_dot(a, b, _NN), (a, b)),
               lambda r, g: (_dot(g, r[1], _NT), _dot(r[0], g, _TN)))


@jax.custom_vjp
def bdot_nt(a, b):
    return _dot(a, b, _NT)


bdot_nt.defvjp(lambda a, b: (_dot(a, b, _NT), (a, b)),
               lambda r, g: (_dot(g, r[1], _NN), _dot(g, r[0], _TN)))


@jax.custom_vjp
def bdot_tn(a, b):
    return _dot(a, b, _TN)


bdot_tn.defvjp(lambda a, b: (_dot(a, b, _TN), (a, b)),
               lambda r, g: (_dot(r[1], g, _NT), _dot(r[0], g, _NN)))


def _split3(x):
    h = x.astype(BF16)
    r = x - h.astype(F32)
    m = r.astype(BF16)
    l = (r - m.astype(F32)).astype(BF16)
    return h, m, l


def _exact_dot(t, x, dn):
    h, m, l = _split3(x)
    d = lambda p: lax.dot_general(t, p, dn, preferred_element_type=F32)
    return d(h) + d(m) + d(l)


@jax.custom_vjp
def tri_cumsum(t, x):
    return _exact_dot(t, x, _NN)


tri_cumsum.defvjp(lambda t, x: (_exact_dot(t, x, _NN), t),
                  lambda t, g: (jnp.zeros_like(t), _exact_dot(t, g, _TN)))


def _split_lanes_impl(x, w):
    return tuple(x[:, i * w:(i + 1) * w] for i in range(x.shape[1] // w))


@functools.partial(jax.custom_vjp, nondiff_argnums=(1,))
def split_lanes(x, w):
    return _split_lanes_impl(x, w)


split_lanes.defvjp(lambda x, w: (_split_lanes_impl(x, w), None),
                   lambda w, r, g: (jnp.concatenate(g, axis=1),))


def _join_impl(parts):
    return jnp.concatenate(parts, axis=1)


@jax.custom_vjp
def join_lanes(parts):
    return _join_impl(parts)


def _join_bwd(ws, g):
    out, off = [], 0
    for w in ws:
        out.append(g[:, off:off + w])
        off += w
    return (tuple(out),)


join_lanes.defvjp(lambda parts: (_join_impl(parts), tuple(p.shape[1] for p in parts)), _join_bwd)


def _rope_impl(x, c, sa, sb, shift):
    w = x.shape[1]
    return x * c + pltpu.roll(x, w - shift, 1) * sa + pltpu.roll(x, shift, 1) * sb


@functools.partial(jax.custom_vjp, nondiff_argnums=(4,))
def rope_lanes(x, c, sa, sb, shift):
    return _rope_impl(x, c, sa, sb, shift)


def _rope_bwd(shift, r, g):
    c, sa, sb = r
    w = g.shape[1]
    dx = g * c + pltpu.roll(g * sa, shift, 1) + pltpu.roll(g * sb, w - shift, 1)
    return dx, jnp.zeros_like(c), jnp.zeros_like(sa), jnp.zeros_like(sb)


rope_lanes.defvjp(lambda x, c, sa, sb, shift: (_rope_impl(x, c, sa, sb, shift), (c, sa, sb)), _rope_bwd)


RMS_EPS = 1e-6


def _rms(x, g):
    return x * lax.rsqrt(jnp.mean(x * x, axis=-1, keepdims=True) + RMS_EPS) * g


ATTN_BLOCK = 512
MASK_VALUE = -1e30


def _causal_mask(t):
    r = lax.broadcasted_iota(jnp.int32, (t, t), 0)
    c = lax.broadcasted_iota(jnp.int32, (t, t), 1)
    return c <= r


def _attn_fwd_call(q, k, v, scale):
    s, width = q.shape
    n_heads = width // LANES
    tq = min(ATTN_BLOCK, s)
    nq = s // tq

    def body(q_ref, k_ref, v_ref, o_ref, lse_ref):
        i = pl.program_id(1)
        qb = q_ref[...].astype(BF16)

        def block(kb, carry, masked):
            m, l, acc = carry
            rows = pl.ds(pl.multiple_of(kb * tq, tq), tq)
            sc = lax.dot_general(qb, k_ref[rows, :].astype(BF16), _NT, preferred_element_type=F32) * scale
            if masked:
                sc = jnp.where(_causal_mask(tq), sc, MASK_VALUE)
            m_new = jnp.maximum(m, jnp.max(sc, axis=-1, keepdims=True))
            alpha = jnp.exp(m - m_new)
            p = jnp.exp(sc - m_new)
            l = alpha * l + jnp.sum(p, axis=-1, keepdims=True)
            acc = alpha * acc + lax.dot_general(p.astype(BF16), v_ref[rows, :].astype(BF16), _NN, preferred_element_type=F32)
            return m_new, l, acc

        init = (jnp.full((tq, 1), MASK_VALUE, F32), jnp.zeros((tq, 1), F32), jnp.zeros((tq, LANES), F32))
        carry = lax.fori_loop(0, i, lambda kb, c: block(kb, c, False), init)
        m, l, acc = block(i, carry, True)
        o_ref[...] = acc / l
        lse_ref[...] = jnp.broadcast_to(m + jnp.log(l), (tq, LANES))

    qspec = pl.BlockSpec((tq, LANES), lambda h, i: (i, h))
    kspec = pl.BlockSpec((s, LANES), lambda h, i: (0, h))
    return pl.pallas_call(
        body,
        name="mla_attn_fwd",
        grid=(n_heads, nq),
        in_specs=[qspec, kspec, kspec],
        out_specs=[qspec, qspec],
        out_shape=[jax.ShapeDtypeStruct((s, width), F32)] * 2,
        compiler_params=_cparams("parallel", "parallel"),
    )(q, k, v)


def _attn_bwd_call(q, k, v, o, lse, do, scale):
    s, width = q.shape
    n_heads = width // LANES
    tq = min(ATTN_BLOCK, s)
    nq = s // tq

    def body(q_ref, k_ref, v_ref, o_ref, lse_ref, do_ref, dq_ref, dk_ref, dv_ref):
        j = pl.program_id(1)

        @pl.when(j == 0)
        def _():
            dq_ref[...] = jnp.zeros_like(dq_ref)

        kb = k_ref[...].astype(BF16)
        vb = v_ref[...].astype(BF16)

        def block(i, carry, masked):
            dk, dv = carry
            rows = pl.ds(pl.multiple_of(i * tq, tq), tq)
            qi = q_ref[rows, :].astype(BF16)
            doi = do_ref[rows, :]
            delta = jnp.sum(doi * o_ref[rows, :], axis=-1, keepdims=True)
            sc = lax.dot_general(qi, kb, _NT, preferred_element_type=F32) * scale
            if masked:
                sc = jnp.where(_causal_mask(tq), sc, MASK_VALUE)
            p = jnp.exp(sc - lse_ref[rows, 0:1])
            dob = doi.astype(BF16)
            dv = dv + lax.dot_general(p.astype(BF16), dob, _TN, preferred_element_type=F32)
            dp = lax.dot_general(dob, vb, _NT, preferred_element_type=F32)
            ds = (p * (dp - delta) * scale).astype(BF16)
            dq_ref[rows, :] += lax.dot_general(ds, kb, _NN, preferred_element_type=F32)
            dk = dk + lax.dot_general(ds, qi, _TN, preferred_element_type=F32)
            return dk, dv

        zero = jnp.zeros((tq, LANES), F32)
        carry = block(j, (zero, zero), True)
        dk, dv = lax.fori_loop(j + 1, nq, lambda i, c: block(i, c, False), carry)
        dk_ref[...] = dk
        dv_ref[...] = dv

    full = pl.BlockSpec((s, LANES), lambda h, j: (0, h))
    blk = pl.BlockSpec((tq, LANES), lambda h, j: (j, h))
    return pl.pallas_call(
        body,
        name="mla_attn_bwd",
        grid=(n_heads, nq),
        in_specs=[full, blk, blk, full, full, full],
        out_specs=[full, blk, blk],
        out_shape=[jax.ShapeDtypeStruct((s, width), F32)] * 3,
        compiler_params=_cparams("parallel", "arbitrary"),
    )(q, k, v, o, lse, do)


@functools.partial(jax.custom_vjp, nondiff_argnums=(3,))
def causal_attention(q, k, v, scale):
    return _attn_fwd_call(q, k, v, scale)[0]


def _causal_attention_fwd(q, k, v, scale):
    o, lse = _attn_fwd_call(q, k, v, scale)
    return o, (q, k, v, o, lse)


def _causal_attention_bwd(scale, res, do):
    return tuple(_attn_bwd_call(*res, do, scale))


causal_attention.defvjp(_causal_attention_fwd, _causal_attention_bwd)


HG_HEADS = 4
HG_CHUNK = 32
HG_REF_ROW = HG_CHUNK // 2 - 1
HG_TILE_ROWS = 256
HG_EXP_CLAMP = 80.0


def _hg_tri():
    r = lax.broadcasted_iota(jnp.int32, (HG_CHUNK, HG_CHUNK), 0)
    c = lax.broadcasted_iota(jnp.int32, (HG_CHUNK, HG_CHUNK), 1)
    return c <= r


def _hg_chunk(q, fl, v, lb, st):
    causal = _hg_tri()
    f = lb + (1.0 - lb) * jax.nn.sigmoid(fl)
    kk = 1.0 - f
    qs = q * jax.nn.sigmoid(q)
    b = tri_cumsum(causal.astype(BF16), jnp.log(f))
    rid = lax.broadcasted_iota(jnp.int32, b.shape, 0)
    b_ref = jnp.sum(jnp.where(rid == HG_REF_ROW, b, 0.0), axis=0, keepdims=True)
    b_last = jnp.sum(jnp.where(rid == HG_CHUNK - 1, b, 0.0), axis=0, keepdims=True)
    q_in = qs * jnp.exp(jnp.minimum(b - b_ref, HG_EXP_CLAMP))
    k_in = kk * jnp.exp(jnp.minimum(b_ref - b, HG_EXP_CLAMP))
    attn = jnp.where(causal, bdot_nt(q_in, k_in), 0.0)
    o = bdot_nn(attn, v) + bdot_nt(qs * jnp.exp(b), st)
    st_new = st * jnp.exp(b_last) + bdot_tn(v, kk * jnp.exp(b_last - b))
    return o, st_new


def _hg_fwd_call(q, fl, v, lb):
    s, width = q.shape
    tr = min(HG_TILE_ROWS, s)
    ncl = tr // HG_CHUNK
    nt = s // tr

    def body(q_ref, f_ref, v_ref, lb_ref, o_ref, sts_ref, st_ref):
        @pl.when(pl.program_id(0) == 0)
        def _():
            st_ref[...] = jnp.zeros_like(st_ref)

        def chunk(c, carry):
            rows = pl.ds(pl.multiple_of(c * HG_CHUNK, HG_CHUNK), HG_CHUNK)
            for h in range(HG_HEADS):
                ln = slice(h * LANES, (h + 1) * LANES)
                st = st_ref[h]
                sts_ref[c, h] = st
                o, st_new = _hg_chunk(q_ref[rows, ln], f_ref[rows, ln], v_ref[rows, ln], lb_ref[:, ln], st)
                o_ref[rows, ln] = o
                st_ref[h] = st_new
            return carry

        lax.fori_loop(0, ncl, chunk, 0)

    rspec = pl.BlockSpec((tr, width), lambda j: (j, 0))
    return pl.pallas_call(
        body,
        name="hgrn2_fwd",
        grid=(nt,),
        in_specs=[rspec, rspec, rspec, pl.BlockSpec((1, width), lambda j: (0, 0))],
        out_specs=[rspec, pl.BlockSpec((ncl, HG_HEADS, LANES, LANES), lambda j: (j, 0, 0, 0))],
        out_shape=[jax.ShapeDtypeStruct((s, width), F32),
                   jax.ShapeDtypeStruct((s // HG_CHUNK, HG_HEADS, LANES, LANES), F32)],
        scratch_shapes=[pltpu.VMEM((HG_HEADS, LANES, LANES), F32)],
        compiler_params=_cparams("arbitrary"),
    )(q, fl, v, lb)


def _hg_bwd_call(q, fl, v, lb, sts, do):
    s, width = q.shape
    tr = min(HG_TILE_ROWS, s)
    ncl = tr // HG_CHUNK
    nt = s // tr

    def body(q_ref, f_ref, v_ref, lb_ref, sts_ref, do_ref, dq_ref, df_ref, dv_ref, dlb_ref, dst_ref):
        @pl.when(pl.program_id(0) == 0)
        def _():
            dst_ref[...] = jnp.zeros_like(dst_ref)
            dlb_ref[...] = jnp.zeros_like(dlb_ref)

        def chunk(cc, carry):
            c = ncl - 1 - cc
            rows = pl.ds(pl.multiple_of(c * HG_CHUNK, HG_CHUNK), HG_CHUNK)
            for h in range(HG_HEADS):
                ln = slice(h * LANES, (h + 1) * LANES)
                _, vjp = jax.vjp(_hg_chunk, q_ref[rows, ln], f_ref[rows, ln], v_ref[rows, ln], lb_ref[:, ln], sts_ref[c, h])
                dq, df, dv, dlb, dst = vjp((do_ref[rows, ln], dst_ref[h]))
                dq_ref[rows, ln] = dq
                df_ref[rows, ln] = df
                dv_ref[rows, ln] = dv
                dlb_ref[:, ln] += dlb
                dst_ref[h] = dst
            return carry

        lax.fori_loop(0, ncl, chunk, 0)

    rspec = pl.BlockSpec((tr, width), lambda j: (nt - 1 - j, 0))
    pspec = pl.BlockSpec((1, width), lambda j: (0, 0))
    return pl.pallas_call(
        body,
        name="hgrn2_bwd",
        grid=(nt,),
        in_specs=[rspec, rspec, rspec, pspec,
                  pl.BlockSpec((ncl, HG_HEADS, LANES, LANES), lambda j: (nt - 1 - j, 0, 0, 0)), rspec],
        out_specs=[rspec, rspec, rspec, pspec],
        out_shape=[jax.ShapeDtypeStruct((s, width), F32)] * 3 + [jax.ShapeDtypeStruct((1, width), F32)],
        scratch_shapes=[pltpu.VMEM((HG_HEADS, LANES, LANES), F32)],
        compiler_params=_cparams("arbitrary"),
    )(q, fl, v, lb, sts, do)


@jax.custom_vjp
def hgrn2_core(q, fl, v, lb):
    return _hg_fwd_call(q, fl, v, lb)[0]


def _hgrn2_core_fwd(q, fl, v, lb):
    o, sts = _hg_fwd_call(q, fl, v, lb)
    return o, (q, fl, v, lb, sts)


def _hgrn2_core_bwd(res, do):
    return tuple(_hg_bwd_call(*res, do))


hgrn2_core.defvjp(_hgrn2_core_fwd, _hgrn2_core_bwd)


D_MODEL = 1024
DEPTH = 2
SSM_GROUPS, SSM_GROUP_CH, SSM_STATE = 32, 16, 64
SSM_WIDTH = SSM_GROUPS * SSM_GROUP_CH
MLA_HEADS, MLA_NOPE, MLA_ROPE, MLA_V = 8, 64, 32, 64
MLA_Q_RANK, MLA_KV_RANK = 512, 256
HG_WIDTH = HG_HEADS * LANES
X_HEADS, X_HEAD_DIM = 4, 128
X_WIDTH = X_HEADS * X_HEAD_DIM
D_FF = 2816
ROPE_THETA = 10000.0
IN_SPLITS = (SSM_WIDTH, MLA_Q_RANK, MLA_KV_RANK, MLA_ROPE, HG_WIDTH, HG_WIDTH, HG_WIDTH, HG_WIDTH, 3 * D_MODEL)
ROPE_LANE0 = MLA_NOPE
ROW_TILE = 256


def _t_rms(x, g):
    return (_rms(x, g),)


def _t_s5_act(y, u, d):
    return (jax.nn.gelu(y + d * u),)


def _t_glu(zo, zg):
    return (zo * jax.nn.sigmoid(zg),)


def _t_mla_rope(q, k, kr, c, sa, sb):
    rep = lambda t: jnp.concatenate([t] * MLA_HEADS, axis=1)
    half = MLA_ROPE // 2
    q_out = rope_lanes(q, rep(c), rep(sa), rep(sb), half)
    kr_out = rope_lanes(kr, c, sa, sb, half)
    return q_out, k + join_lanes((kr_out,) * MLA_HEADS)


def _t_hg_post(o, gate, gn):
    os_, gs = split_lanes(o, LANES), split_lanes(gate, LANES)
    return (join_lanes(tuple(_rms(a, gn) * (b * jax.nn.sigmoid(b)) for a, b in zip(os_, gs))),)


def _t_merge(y_ssm, y_mla, y_hg, g0, g1, g2):
    return (jax.nn.sigmoid(g0) * y_ssm + jax.nn.sigmoid(g1) * y_mla + jax.nn.sigmoid(g2) * y_hg,)


def _t_xattn(q, k, v):
    scale = 1.0 / math.sqrt(X_HEAD_DIM)
    outs = []
    for qh, kh, vh in zip(split_lanes(q, X_HEAD_DIM), split_lanes(k, X_HEAD_DIM), split_lanes(v, X_HEAD_DIM)):
        sc = bdot_nt(qh, kh) * scale
        p = jnp.exp(sc - jnp.max(sc, axis=-1, keepdims=True))
        p = p / jnp.sum(p, axis=-1, keepdims=True)
        outs.append(bdot_nn(p, vh))
    return (join_lanes(tuple(outs)),)


def _t_swiglu(gt, up):
    return (gt * jax.nn.sigmoid(gt) * up,)


def _t_loss(x, tgt, g):
    e = _rms(x, g) - tgt
    return (jnp.broadcast_to(jnp.mean(e * e, axis=-1, keepdims=True), (x.shape[0], LANES)),)


rms_op = rowwise(_t_rms, 1, 0, ROW_TILE, "rmsnorm")
s5_act_op = rowwise(_t_s5_act, 2, 0, ROW_TILE, "s5_act")
glu_op = rowwise(_t_glu, 2, 0, ROW_TILE, "glu")
mla_rope_op = rowwise(_t_mla_rope, 3, 3, ROW_TILE, "mla_rope")
hg_post_op = rowwise(_t_hg_post, 2, 0, ROW_TILE, "hg_post")
merge_op = rowwise(_t_merge, 6, 0, ROW_TILE, "merge")
xattn_op = rowwise(_t_xattn, 1, 0, ROW_TILE, "xattn")
swiglu_op = rowwise(_t_swiglu, 2, 0, ROW_TILE, "swiglu")
loss_op = rowwise(_t_loss, 1, 1, ROW_TILE, "loss")


def _rope_tables(positions):
    half = MLA_ROPE // 2
    inv_freq = ROPE_THETA ** (-jnp.arange(half, dtype=F32) / half)
    ang = positions.astype(F32)[:, None] * inv_freq
    cos, sin = jnp.cos(ang), jnp.sin(ang)
    s = positions.shape[0]
    z = lambda w: jnp.zeros((s, w), F32)
    tail = LANES - ROPE_LANE0 - MLA_ROPE
    c = jnp.concatenate([jnp.ones((s, ROPE_LANE0), F32), cos, cos, z(tail)], axis=1)
    sa = jnp.concatenate([z(ROPE_LANE0), -sin, z(half), z(tail)], axis=1)
    sb = jnp.concatenate([z(ROPE_LANE0), z(half), sin, z(tail)], axis=1)
    return c, sa, sb


def _s5_operators(lam_re, lam_im, b_re, b_im, c_re, c_im, log_step):
    g, p, h = SSM_GROUPS, SSM_STATE, SSM_GROUP_CH
    lam = lax.complex(lam_re, lam_im)
    lam_bar = jnp.exp(lam * jnp.exp(log_step)[:, None])
    b_bar = ((lam_bar - 1.0) / lam)[..., None] * lax.complex(b_re, b_im)
    eye = jnp.eye(g, dtype=F32)
    bd = lambda t: jnp.einsum("gph,gk->ghkp", t, eye).reshape(g * h, g * p)
    cd = lambda t: jnp.einsum("ghp,gk->gpkh", t, eye).reshape(g * p, g * h)
    a = jnp.stack([jnp.real(lam_bar).reshape(-1), jnp.imag(lam_bar).reshape(-1)])
    return a, bd(jnp.real(b_bar)), bd(jnp.imag(b_bar)), cd(c_re), cd(-c_im)


IN_PAD = 6656
_SEG = {}
_off = 0
for _name, _w in (("u", 512), ("q_lat", 512), ("kv_lat", 256), ("k_rope", 128), ("hg_q", 512), ("hg_f", 512),
                  ("hg_i", 512), ("hg_g", 512), ("g0", 1024), ("g1", 1024), ("g2", 1024)):
    _SEG[_name] = (_off, _off + _w)
    _off += _w


def _layer_matrices(w, l):
    w_in = w["w_in"][l]
    d, dt = w_in.shape[0], w_in.dtype
    z = lambda n: jnp.zeros((d, n), dt)
    r0 = SSM_WIDTH + MLA_Q_RANK + MLA_KV_RANK
    w_proj = jnp.concatenate([w_in[:, :r0], z(ROPE_LANE0), w_in[:, r0:r0 + MLA_ROPE], z(LANES - ROPE_LANE0 - MLA_ROPE),
                              w_in[:, r0 + MLA_ROPE:], z(IN_PAD - _off)], axis=1)
    pad_heads = lambda t: jnp.pad(t, ((0, 0), (0, 0), (0, LANES - t.shape[2]))).reshape(t.shape[0], -1)
    uq = w["mla_w_uq"][l].reshape(MLA_Q_RANK, MLA_HEADS, MLA_NOPE + MLA_ROPE)
    ukv = w["mla_w_ukv"][l].reshape(MLA_KV_RANK, MLA_HEADS, MLA_NOPE + MLA_V)
    wo = w["mla_w_o"][l].reshape(MLA_HEADS, MLA_V, D_MODEL)
    glu, xkv, gu = w["ssm_w_glu"][l], w["x_w_kv"][l], w["ffn_w_gate_up"][l]
    return dict(
        w_proj=w_proj, glu_o=glu[:, :D_MODEL], glu_g=glu[:, D_MODEL:],
        uq=pad_heads(uq), uk=pad_heads(ukv[:, :, :MLA_NOPE]), uv=pad_heads(ukv[:, :, MLA_NOPE:]),
        mla_o=jnp.pad(wo, ((0, 0), (0, LANES - MLA_V), (0, 0))).reshape(MLA_HEADS * LANES, D_MODEL),
        hg_o=w["hg_w_o"][l], w_out=w["w_out"][l], x_q=w["x_w_q"][l], x_k=xkv[:, :X_WIDTH], x_v=xkv[:, X_WIDTH:],
        x_o=w["x_w_o"][l], ffn_g=gu[:, :D_FF], ffn_u=gu[:, D_FF:], ffn_d=w["ffn_w_down"][l])


def _layer(x, mem, tabs, m, sp, l, lower_bound):
    row = lambda name: sp[name][l].reshape(1, -1)
    (h,) = rms_op(x, row("norm_mix"))
    proj = matmul(h, m["w_proj"])
    seg = lambda name: proj[:, _SEG[name][0]:_SEG[name][1]]
    a, bd_r, bd_i, cd_r, cd_i = _s5_operators(*(sp[n][l] for n in (
        "ssm_lam_re", "ssm_lam_im", "ssm_b_re", "ssm_b_im", "ssm_c_re", "ssm_c_im", "ssm_log_step")))
    u = seg("u")
    us = _to_segments(u)
    x_r, x_i = s5_scan(matmul(us, bd_r), matmul(us, bd_i), a)
    y = _from_segments(matmul_add(x_i, cd_i, matmul(x_r, cd_r)))
    (ya,) = s5_act_op(y, u, row("ssm_d"))
    (y_ssm,) = glu_op(matmul(ya, m["glu_o"]), matmul(ya, m["glu_g"]))
    (qn,) = rms_op(seg("q_lat"), row("mla_q_norm"))
    (kvn,) = rms_op(seg("kv_lat"), row("mla_kv_norm"))
    q, k = mla_rope_op(matmul(qn, m["uq"]), matmul(kvn, m["uk"]), seg("k_rope"), *tabs)
    o = causal_attention(q, k, matmul(kvn, m["uv"]), 1.0 / math.sqrt(MLA_NOPE + MLA_ROPE))
    y_mla = matmul(o, m["mla_o"])
    o = hgrn2_core(seg("hg_q"), seg("hg_f"), seg("hg_i"), lower_bound)
    (og,) = hg_post_op(o, seg("hg_g"), row("hg_g_norm"))
    y_hg = matmul(og, m["hg_o"])
    (merged,) = merge_op(y_ssm, y_mla, y_hg, seg("g0"), seg("g1"), seg("g2"))
    x = matmul_add(merged, m["w_out"], x)
    (hc,) = rms_op(x, row("norm_cross"))
    (mn,) = rms_op(mem, row("norm_mem"))
    (ox,) = xattn_op(matmul(hc, m["x_q"]), matmul(mn, m["x_k"]), matmul(mn, m["x_v"]))
    x = matmul_add(ox, m["x_o"], x)
    (hf,) = rms_op(x, row("norm_ffn"))
    (act,) = swiglu_op(matmul(hf, m["ffn_g"]), matmul(hf, m["ffn_u"]))
    return matmul_add(act, m["ffn_d"], x)


def _local_loss(x, mem, positions, target, w, sp):
    tabs = _rope_tables(positions)
    lb_p = jax.nn.softmax(sp["hg_lb"], axis=0)
    lower = jnp.cumsum(lb_p, axis=0) - lb_p[0:1]
    for l in range(DEPTH):
        x = _layer(x, mem, tabs, _layer_matrices(w, l), sp, l, lower[l].reshape(1, -1))
    (row_loss,) = loss_op(x, target, sp["norm_final"].reshape(1, -1))
    return 0.5 * jnp.sum(row_loss[:, 0])


N_DEV = 8
N_CHIPS = 4
COMM_LANES = 512
MESH_ID = pl.DeviceIdType.MESH
_ANY = pl.BlockSpec(memory_space=pl.ANY)
_OTHER_CHIPS = ((1, 0), (0, 1), (1, 1))


def _place():
    return lax.axis_index("x"), lax.axis_index("y"), lax.axis_index("c")


def _all_gather_call(block, name):
    r, w = block.shape

    def body(x_ref, out_ref, send_sems, recv_sems, local_sem):
        x, y, c = _place()
        me, sibling = (x, y, c), (x, y, 1 - c)
        chips = [(x ^ fx, y ^ fy) for fx, fy in _OTHER_CHIPS]

        def slot(px, py, pc):
            return out_ref.at[4 * px + 2 * py + pc]

        def copy(k, blk, to, src=None):
            return pltpu.make_async_remote_copy(
                src_ref=slot(*blk) if src is None else src, dst_ref=slot(*blk),
                send_sem=send_sems.at[k], recv_sem=recv_sems.at[k], device_id=to, device_id_type=MESH_ID)

        mine = pltpu.make_async_copy(x_ref, slot(*me), local_sem)
        mine.start()
        first = [copy(0, me, sibling, src=x_ref)] + [copy(1 + j, me, (*chip, c), src=x_ref) for j, chip in enumerate(chips)]
        for cp in first:
            cp.start()
        passed = [copy(4 + j, (*chip, c), sibling) for j, chip in enumerate(chips)]
        for j, chip in enumerate(chips):
            copy(1 + j, (*chip, c), me).wait_recv()
            passed[j].start()
        copy(0, sibling, me).wait_recv()
        for j, chip in enumerate(chips):
            copy(4 + j, (*chip, 1 - c), me).wait_recv()
        for cp in first + passed:
            cp.wait_send()
        mine.wait()

    return pl.pallas_call(
        body,
        name=name,
        out_shape=jax.ShapeDtypeStruct((N_DEV, r, w), block.dtype),
        in_specs=[_ANY],
        out_specs=_ANY,
        scratch_shapes=[pltpu.SemaphoreType.DMA((7,)), pltpu.SemaphoreType.DMA((7,)), pltpu.SemaphoreType.DMA],
    )(block)


def _pair_exchange_call(g, name):
    _, r, w = g.shape

    def body(g_ref, own_ref, got_ref, send_sems, recv_sems, local_sems):
        x, y, c = _place()
        sibling = (x, y, 1 - c)
        sends, keeps = [], []
        for p in range(N_CHIPS):
            px, py = p // 2, p % 2
            sends.append(pltpu.make_async_remote_copy(
                src_ref=g_ref.at[4 * px + 2 * py + (1 - c)], dst_ref=got_ref.at[p],
                send_sem=send_sems.at[p], recv_sem=recv_sems.at[p], device_id=sibling, device_id_type=MESH_ID))
            keeps.append(pltpu.make_async_copy(g_ref.at[4 * px + 2 * py + c], own_ref.at[p], local_sems.at[p]))
        for cp in sends + keeps:
            cp.start()
        for cp in sends:
            cp.wait_recv()
        for cp in sends:
            cp.wait_send()
        for cp in keeps:
            cp.wait()

    shp = jax.ShapeDtypeStruct((N_CHIPS, r, w), g.dtype)
    return pl.pallas_call(
        body,
        name=name,
        out_shape=[shp, shp],
        in_specs=[_ANY],
        out_specs=[_ANY, _ANY],
        scratch_shapes=[pltpu.SemaphoreType.DMA((N_CHIPS,))] * 3,
    )(g)


def _chip_exchange_call(part, name):
    _, r, w = part.shape

    def body(p_ref, own_ref, got_ref, send_sems, recv_sems, local_sem):
        x, y, c = _place()
        keep = pltpu.make_async_copy(p_ref.at[2 * x + y], own_ref, local_sem)
        keep.start()
        sends = []
        for k, (fx, fy) in enumerate(_OTHER_CHIPS):
            px, py = x ^ fx, y ^ fy
            sends.append(pltpu.make_async_remote_copy(
                src_ref=p_ref.at[2 * px + py], dst_ref=got_ref.at[k],
                send_sem=send_sems.at[k], recv_sem=recv_sems.at[k], device_id=(px, py, c), device_id_type=MESH_ID))
        for cp in sends:
            cp.start()
        for cp in sends:
            cp.wait_recv()
        for cp in sends:
            cp.wait_send()
        keep.wait()

    return pl.pallas_call(
        body,
        name=name,
        out_shape=[jax.ShapeDtypeStruct((r, w), part.dtype), jax.ShapeDtypeStruct((3, r, w), part.dtype)],
        in_specs=[_ANY],
        out_specs=[_ANY, _ANY],
        scratch_shapes=[pltpu.SemaphoreType.DMA((3,)), pltpu.SemaphoreType.DMA((3,)), pltpu.SemaphoreType.DMA],
    )(part)


def _comm_tile(r):
    return _pick_tile(r, (256, 128, 64, 32, 16))


def _pair_sum_call(own, got, name):
    n, r, w = own.shape
    tr = _comm_tile(r)

    def body(a_ref, b_ref, o_ref):
        o_ref[...] = (a_ref[...].astype(F32) + b_ref[...].astype(F32)).astype(o_ref.dtype)

    spec = pl.BlockSpec((1, tr, w), lambda p, i: (p, i, 0))
    return pl.pallas_call(
        body, name=name, grid=(n, r // tr), in_specs=[spec, spec], out_specs=spec,
        out_shape=jax.ShapeDtypeStruct(own.shape, own.dtype), compiler_params=_cparams("parallel", "parallel"),
    )(own, got)


def _chip_sum_call(own, got, name):
    r, w = own.shape
    tr = _comm_tile(r)

    def body(a_ref, b_ref, o_ref):
        acc = a_ref[...].astype(F32)
        for k in range(3):
            acc = acc + b_ref[k].astype(F32)
        o_ref[...] = acc

    return pl.pallas_call(
        body, name=name, grid=(r // tr,),
        in_specs=[pl.BlockSpec((tr, w), lambda i: (i, 0)), pl.BlockSpec((3, tr, w), lambda i: (0, i, 0))],
        out_specs=pl.BlockSpec((tr, w), lambda i: (i, 0)),
        out_shape=jax.ShapeDtypeStruct((r, w), F32), compiler_params=_cparams("parallel"),
    )(own, got)


def _reduce_scatter(g, name):
    own, got = _pair_exchange_call(g, name + "_pair")
    part = _pair_sum_call(own, got, name + "_pair_sum")
    own, got = _chip_exchange_call(part, name + "_chip")
    return _chip_sum_call(own, got, name + "_chip_sum")


def _flat_rows(n):
    r = -(-n // COMM_LANES)
    return -(-r // 256) * 256 if r > 256 else -(-r // 16) * 16


def _pack(parts, lead):
    flat = jnp.concatenate([p.reshape(p.shape[:lead] + (-1,)) for p in parts], axis=-1)
    n = flat.shape[-1]
    r = _flat_rows(n)
    flat = jnp.pad(flat, [(0, 0)] * lead + [(0, r * COMM_LANES - n)])
    return flat.reshape(flat.shape[:lead] + (r, COMM_LANES))


def _unpack(buf, shapes, lead):
    flat = buf.reshape(buf.shape[:lead] + (-1,))
    out, off = [], 0
    for shp in shapes:
        n = math.prod(shp)
        out.append(flat[..., off:off + n].reshape(buf.shape[:lead] + tuple(shp)))
        off += n
    return out


SHARDED = dict(w_in=2, ssm_w_glu=2, mla_w_uq=2, mla_w_ukv=2, mla_w_o=2, hg_w_o=2, w_out=1, x_w_q=1, x_w_kv=1,
               x_w_o=2, ffn_w_gate_up=2, ffn_w_down=1)
REPLICATED = ("norm_mix", "ssm_lam_re", "ssm_lam_im", "ssm_b_re", "ssm_b_im", "ssm_c_re", "ssm_c_im", "ssm_d",
              "ssm_log_step", "mla_q_norm", "mla_kv_norm", "hg_lb", "hg_g_norm", "norm_cross", "norm_mem", "norm_ffn",
              "norm_final")


def _join_shards(stacked, axis):
    n, l, a, b = stacked.shape
    if axis == 1:
        return stacked.transpose(1, 0, 2, 3).reshape(l, n * a, b)
    return stacked.transpose(1, 2, 0, 3).reshape(l, a, n * b)


def _split_shards(full, axis):
    l, a, b = full.shape
    if axis == 1:
        return full.reshape(l, N_DEV, a // N_DEV, b).transpose(1, 0, 2, 3)
    return full.reshape(l, a, N_DEV, b // N_DEV).transpose(2, 0, 1, 3)


@jax.custom_vjp
def gather_weights(shards):
    shapes = [s.shape for s in shards]
    got = _all_gather_call(_pack([s.astype(BF16) for s in shards], 0), "weights_all_gather")
    return tuple(_join_shards(p, ax) for p, ax in zip(_unpack(got, shapes, 1), SHARDED.values()))


def _gather_weights_fwd(shards):
    return gather_weights(shards), tuple(s.shape for s in shards)


def _gather_weights_bwd(shapes, cts):
    g = _pack([_split_shards(ct, ax) for ct, ax in zip(cts, SHARDED.values())], 1)
    return (tuple(_unpack(_reduce_scatter(g, "grads_reduce_scatter"), shapes, 0)),)


gather_weights.defvjp(_gather_weights_fwd, _gather_weights_bwd)


@jax.custom_vjp
def sync_replicated(params):
    return params


def _sync_replicated_bwd(shapes, cts):
    flat = _pack(list(cts), 0)
    r = flat.shape[0]
    flat = jnp.pad(flat, ((0, (-r) % (N_DEV * 16)), (0, 0)))
    per = flat.shape[0] // N_DEV
    mine = _reduce_scatter(flat.reshape(N_DEV, per, COMM_LANES), "small_reduce_scatter")
    total = _all_gather_call(mine, "small_all_gather").reshape(-1, COMM_LANES)[:r]
    return (tuple(_unpack(total, shapes, 0)),)


sync_replicated.defvjp(lambda params: (params, tuple(p.shape for p in params)), _sync_replicated_bwd)


ADAM_LR, ADAM_B1, ADAM_B2, ADAM_EPS, ADAM_WD, ADAM_STEP = 0.001, 0.9, 0.999, 1e-08, 0.01, 10


def _adamw_call(w, g, m, v, name):
    shape = w.shape
    cols = shape[-1]
    rows = math.prod(shape[:-1]) if len(shape) > 1 else 1
    tr = _pick_tile(rows, (512, 256, 128, 64, 32, 16, 8))

    def body(w_ref, g_ref, m_ref, v_ref, d_ref, nm_ref, nv_ref):
        gg = g_ref[...]
        m_new = ADAM_B1 * m_ref[...] + (1.0 - ADAM_B1) * gg
        v_new = ADAM_B2 * v_ref[...] + (1.0 - ADAM_B2) * jnp.square(gg)
        m_hat = m_new / (1.0 - ADAM_B1 ** ADAM_STEP)
        v_hat = v_new / (1.0 - ADAM_B2 ** ADAM_STEP)
        d_ref[...] = -ADAM_LR * (m_hat / (jnp.sqrt(v_hat) + ADAM_EPS) + ADAM_WD * w_ref[...])
        nm_ref[...] = m_new
        nv_ref[...] = v_new

    spec = pl.BlockSpec((tr, cols), lambda i: (i, 0))
    outs = pl.pallas_call(
        body, name=name, grid=(rows // tr,), in_specs=[spec] * 4, out_specs=[spec] * 3,
        out_shape=[jax.ShapeDtypeStruct((rows, cols), F32)] * 3, compiler_params=_cparams("parallel"),
    )(*(t.reshape(rows, cols) for t in (w, g, m, v)))
    return tuple(o.reshape(shape) for o in outs)


WEIGHTS = ("norm_mix", "w_in", "ssm_lam_re", "ssm_lam_im", "ssm_b_re", "ssm_b_im", "ssm_c_re", "ssm_c_im", "ssm_d",
           "ssm_log_step", "ssm_w_glu", "mla_q_norm", "mla_kv_norm", "mla_w_uq", "mla_w_ukv", "mla_w_o", "hg_lb",
           "hg_g_norm", "hg_w_o", "w_out", "norm_cross", "norm_mem", "x_w_q", "x_w_kv", "x_w_o", "norm_ffn",
           "ffn_w_gate_up", "ffn_w_down", "norm_final")


def kernel(x, mem, positions, norm_mix, w_in, ssm_lam_re, ssm_lam_im, ssm_b_re, ssm_b_im, ssm_c_re, ssm_c_im, ssm_d, ssm_log_step, ssm_w_glu, mla_q_norm, mla_kv_norm, mla_w_uq, mla_w_ukv, mla_w_o, hg_lb, hg_g_norm, hg_w_o, w_out, norm_cross, norm_mem, x_w_q, x_w_kv, x_w_o, norm_ffn, ffn_w_gate_up, ffn_w_down, norm_final, loss_target, m_norm_mix, m_w_in, m_ssm_lam_re, m_ssm_lam_im, m_ssm_b_re, m_ssm_b_im, m_ssm_c_re, m_ssm_c_im, m_ssm_d, m_ssm_log_step, m_ssm_w_glu, m_mla_q_norm, m_mla_kv_norm, m_mla_w_uq, m_mla_w_ukv, m_mla_w_o, m_hg_lb, m_hg_g_norm, m_hg_w_o, m_w_out, m_norm_cross, m_norm_mem, m_x_w_q, m_x_w_kv, m_x_w_o, m_norm_ffn, m_ffn_w_gate_up, m_ffn_w_down, m_norm_final, v_norm_mix, v_w_in, v_ssm_lam_re, v_ssm_lam_im, v_ssm_b_re, v_ssm_b_im, v_ssm_c_re, v_ssm_c_im, v_ssm_d, v_ssm_log_step, v_ssm_w_glu, v_mla_q_norm, v_mla_kv_norm, v_mla_w_uq, v_mla_w_ukv, v_mla_w_o, v_hg_lb, v_hg_g_norm, v_hg_w_o, v_w_out, v_norm_cross, v_norm_mem, v_x_w_q, v_x_w_kv, v_x_w_o, v_norm_ffn, v_ffn_w_gate_up, v_ffn_w_down, v_norm_final):
    given = dict(locals())
    weights = {n: given[n] for n in WEIGHTS}

    def loss_fn(xs, shards, small):
        full = dict(zip(SHARDED, gather_weights(shards)))
        sp = dict(zip(REPLICATED, sync_replicated(small)))
        return _local_loss(xs, mem[0], positions[0], loss_target[0], full, sp)

    shards = tuple(weights[n] for n in SHARDED)
    small = tuple(weights[n] for n in REPLICATED)
    loss_local, (gx, g_shards, g_small) = jax.value_and_grad(loss_fn, argnums=(0, 1, 2))(x[0], shards, small)
    loss = lax.psum(loss_local, ("x", "y", "c"))
    grads = {**dict(zip(SHARDED, g_shards)), **dict(zip(REPLICATED, g_small))}
    steps = {n: _adamw_call(weights[n], grads[n], given["m_" + n], given["v_" + n], "adamw_" + n) for n in WEIGHTS}
    return (loss, gx[None], *[grads[n] for n in WEIGHTS], *[steps[n][0] for n in WEIGHTS],
            *[steps[n][1] for n in WEIGHTS], *[steps[n][2] for n in WEIGHTS])
```

```python
import functools
import math

import jax
import jax.numpy as jnp
from jax import lax
from jax.experimental import pallas as pl
from jax.experimental.pallas import tpu as pltpu

F32 = jnp.float32
BF16 = jnp.bfloat16

VMEM_LIMIT_BYTES = 48 * 1024 * 1024
LANES = 128
SUBLANES = 8


def _cparams(*sem):
    return pltpu.CompilerParams(dimension_semantics=sem, vmem_limit_bytes=VMEM_LIMIT_BYTES)


def _pick_tile(n, cands):
    for c in cands:
        if n % c == 0:
            return c
    return n


MM_VMEM_BUDGET = 38 * 1024 * 1024
MM_STEP_US = 0.35
HBM_BYTES_PER_US = 3.0e6
VREG_RMW_PER_US = 1.5e3


def _divisor_tiles(dim, cands):
    out = [t for t in cands if dim % t == 0]
    return out or [dim]


def _mm_tiles(m, n, k, sa, sb, so):
    tms = _divisor_tiles(m, (1024, 512, 256, 128, 64, 32, 16, 8))[:2]
    tns = _divisor_tiles(n, (1024, 512, 384, 256, 128))
    tks = [k // d for d in (1, 2, 4, 8, 13, 16, 26, 32, 52) if k % d == 0 and (k // d) % LANES == 0] or [k]
    best = None
    for tk in tks:
        nk = k // tk
        for tm in tms:
            for tn in tns:
                vmem = 2 * (tm * tk * sa + tk * tn * sb + tm * tn * so) + (tm * tn * 4 if nk > 1 else 0)
                if vmem > MM_VMEM_BUDGET:
                    continue
                steps = (m // tm) * (n // tn) * nk
                a_reads = m * k * sa * (n // tn if nk > 1 else 1)
                b_reads = k * n * sb * (m // tm if (nk > 1 or n // tn > 1) else 1)
                cost = (steps * MM_STEP_US + (a_reads + b_reads) / HBM_BYTES_PER_US
                        + (m * n * nk / 1024 / VREG_RMW_PER_US if nk > 1 else 0.0))
                if best is None or cost < best[0]:
                    best = (cost, tm, tn, tk)
    assert best is not None, (m, n, k)
    return best[1:]


def _mm_tiles_cached_t(m, n, k, sa, sb, so):
    for tm in _divisor_tiles(m, (1024, 512, 256, 128)):
        for tn in _divisor_tiles(n, (1024, 512, 384, 256, 128)):
            if 2 * (k * tm * sa + k * tn * sb + tm * tn * so) + tm * k * 2 <= MM_VMEM_BUDGET:
                return tm, tn
    return None


def _mm_tn_cached_call(a, b, tiles, out_dtype, name):
    k, m = a.shape
    n = b.shape[1]
    tm, tn = tiles

    def body(a_ref, b_ref, o_ref, at_ref):
        @pl.when(pl.program_id(1) == 0)
        def _():
            at_ref[...] = a_ref[...].astype(BF16).T

        o_ref[...] = lax.dot_general(at_ref[...], b_ref[...].astype(BF16), _NN_DIMS,
                                     preferred_element_type=F32).astype(out_dtype)

    return pl.pallas_call(
        body,
        name=name,
        grid=(m // tm, n // tn),
        in_specs=[pl.BlockSpec((k, tm), lambda i, j: (0, i)), pl.BlockSpec((k, tn), lambda i, j: (0, j))],
        out_specs=pl.BlockSpec((tm, tn), lambda i, j: (i, j)),
        out_shape=jax.ShapeDtypeStruct((m, n), out_dtype),
        scratch_shapes=[pltpu.VMEM((tm, k), BF16)],
        compiler_params=_cparams("parallel", "arbitrary"),
    )(a, b)


_NN_DIMS = (((1,), (0,)), ((), ()))


def _mm_call(a, b, ta, tb, add=None, out_dtype=F32, name="mm"):
    m, k = (a.shape[1], a.shape[0]) if ta else a.shape
    k2, n = (b.shape[1], b.shape[0]) if tb else b.shape
    assert k == k2, (a.shape, b.shape, ta, tb)
    sizes = (a.dtype.itemsize, b.dtype.itemsize, jnp.dtype(out_dtype).itemsize + (add.dtype.itemsize if add is not None else 0))
    if ta:
        tiles = _mm_tiles_cached_t(m, n, k, *sizes)
        if tiles is not None:
            return _mm_tn_cached_call(a, b, tiles, out_dtype, name)
    tm, tn, tk = _mm_tiles(m, n, k, *sizes)
    nk = k // tk
    a_spec = pl.BlockSpec((tk, tm), lambda i, j, kk: (kk, i)) if ta else pl.BlockSpec((tm, tk), lambda i, j, kk: (i, kk))
    b_spec = pl.BlockSpec((tn, tk), lambda i, j, kk: (j, kk)) if tb else pl.BlockSpec((tk, tn), lambda i, j, kk: (kk, j))
    o_spec = pl.BlockSpec((tm, tn), lambda i, j, kk: (i, j))
    dn = (((0 if ta else 1,), (1 if tb else 0,)), ((), ()))
    has_add = add is not None

    def body(*refs):
        a_ref, b_ref = refs[0], refs[1]
        c_ref = refs[2] if has_add else None
        o_ref = refs[3] if has_add else refs[2]
        p = lax.dot_general(a_ref[...].astype(BF16), b_ref[...].astype(BF16), dn, preferred_element_type=F32)

        def finish(r):
            if has_add:
                r = r + c_ref[...].astype(F32)
            o_ref[...] = r.astype(out_dtype)

        if nk == 1:
            finish(p)
        else:
            acc_ref = refs[-1]
            kk = pl.program_id(2)

            @pl.when(kk == 0)
            def _():
                acc_ref[...] = p

            @pl.when(kk > 0)
            def _():
                acc_ref[...] += p

            @pl.when(kk == nk - 1)
            def _():
                finish(acc_ref[...])

    in_specs = [a_spec, b_spec] + ([o_spec] if has_add else [])
    args = (a, b) + ((add,) if has_add else ())
    return pl.pallas_call(
        body,
        name=name,
        grid=(m // tm, n // tn, nk),
        in_specs=in_specs,
        out_specs=o_spec,
        out_shape=jax.ShapeDtypeStruct((m, n), out_dtype),
        scratch_shapes=[] if nk == 1 else [pltpu.VMEM((tm, tn), F32)],
        compiler_params=_cparams("parallel", "parallel", "arbitrary"),
    )(*args)


@jax.custom_vjp
def matmul(a, b):
    return _mm_call(a, b, False, False, name="mm_fwd")


def _matmul_fwd(a, b):
    return _mm_call(a, b, False, False, name="mm_fwd"), (a, b)


def _matmul_bwd(res, g):
    a, b = res
    da = _mm_call(g, b, False, True, out_dtype=a.dtype, name="mm_da")
    db = _mm_call(a, g, True, False, out_dtype=b.dtype, name="mm_db")
    return da, db


matmul.defvjp(_matmul_fwd, _matmul_bwd)


@jax.custom_vjp
def matmul_add(a, b, c):
    return _mm_call(a, b, False, False, add=c, name="mm_add_fwd")


def _matmul_add_fwd(a, b, c):
    return _mm_call(a, b, False, False, add=c, name="mm_add_fwd"), (a, b)


def _matmul_add_bwd(res, g):
    a, b = res
    da = _mm_call(g, b, False, True, out_dtype=a.dtype, name="mm_da")
    db = _mm_call(a, g, True, False, out_dtype=b.dtype, name="mm_db")
    return da, db, g


matmul_add.defvjp(_matmul_add_fwd, _matmul_add_bwd)


def rowwise(f, n_rows, n_aux, tile, name):
    def specs(arrs, tiled):
        out = []
        for x in arrs:
            if tiled:
                out.append(pl.BlockSpec((tile, x.shape[1]), lambda i: (i, 0)))
            else:
                out.append(pl.BlockSpec(x.shape, lambda i: (0, 0)))
        return out

    def tile_structs(args):
        rows_aux, params = args[: n_rows + n_aux], args[n_rows + n_aux:]
        return [jax.ShapeDtypeStruct((tile, x.shape[1]), x.dtype) for x in rows_aux] + [
            jax.ShapeDtypeStruct(p.shape, p.dtype) for p in params]

    def fwd_call(*args):
        s = args[0].shape[0]
        outs = jax.eval_shape(f, *tile_structs(args))
        n_in = len(args)

        def body(*refs):
            vals = [r[...] for r in refs[:n_in]]
            res = f(*vals)
            for o_ref, r in zip(refs[n_in:], res):
                o_ref[...] = r.astype(o_ref.dtype)

        return pl.pallas_call(
            body,
            name=name + "_fwd",
            grid=(s // tile,),
            in_specs=specs(args[: n_rows + n_aux], True) + specs(args[n_rows + n_aux:], False),
            out_specs=[pl.BlockSpec((tile, o.shape[1]), lambda i: (i, 0)) for o in outs],
            out_shape=[jax.ShapeDtypeStruct((s, o.shape[1]), o.dtype) for o in outs],
            compiler_params=_cparams("parallel"),
        )(*args)

    def bwd_call(args, gs):
        s = args[0].shape[0]
        rows, aux, params = args[:n_rows], args[n_rows:n_rows + n_aux], args[n_rows + n_aux:]
        n_in, n_g, n_p = len(args), len(gs), len(params)

        def body(*refs):
            vals = [r[...] for r in refs[:n_in]]
            gvals = tuple(r[...] for r in refs[n_in:n_in + n_g])
            out_refs = refs[n_in + n_g:]
            auxv = vals[n_rows:n_rows + n_aux]

            def g_(*rp):
                return tuple(f(*rp[:n_rows], *auxv, *rp[n_rows:]))

            _, vjp = jax.vjp(g_, *vals[:n_rows], *vals[n_rows + n_aux:])
            cts = vjp(gvals)
            for o_ref, ct in zip(out_refs[:n_rows], cts[:n_rows]):
                o_ref[...] = ct.astype(o_ref.dtype)
            if n_p:
                @pl.when(pl.program_id(0) == 0)
                def _():
                    for o_ref in out_refs[n_rows:]:
                        o_ref[...] = jnp.zeros_like(o_ref)

                for o_ref, ct in zip(out_refs[n_rows:], cts[n_rows:]):
                    o_ref[...] += ct.astype(o_ref.dtype)

        return pl.pallas_call(
            body,
            name=name + "_bwd",
            grid=(s // tile,),
            in_specs=specs(rows + aux, True) + specs(params, False) + specs(gs, True),
            out_specs=specs(rows, True) + specs(params, False),
            out_shape=[jax.ShapeDtypeStruct(x.shape, x.dtype) for x in rows + params],
            compiler_params=_cparams("arbitrary" if n_p else "parallel"),
        )(*args, *gs)

    @jax.custom_vjp
    def op(*args):
        return tuple(fwd_call(*args))

    def op_fwd(*args):
        return tuple(fwd_call(*args)), args

    def op_bwd(args, gs):
        cts = bwd_call(tuple(args), tuple(gs))
        rows_ct, par_ct = cts[:n_rows], cts[n_rows:]
        aux_ct = [jnp.zeros_like(a) for a in args[n_rows:n_rows + n_aux]]
        return tuple(rows_ct) + tuple(aux_ct) + tuple(par_ct)

    op.defvjp(op_fwd, op_bwd)
    return op


SCAN_SEGMENTS = SUBLANES
SCAN_TILE_ROWS = 512
SCAN_TILE_LANES = 512


def _scan_specs(s, n):
    tr = min(SCAN_TILE_ROWS, s)
    tn = min(SCAN_TILE_LANES, n)
    return tr, tn, s // tr, n // tn


def _scan_step(ar, ai, xr, xi, br, bi):
    return ar * xr - ai * xi + br, ar * xi + ai * xr + bi


def _scan_finals(b_r, b_i, a, reverse, name):
    s, n = b_r.shape
    tr, tn, nt, nc = _scan_specs(s, n)
    ti = tr // SUBLANES
    tmap = (lambda c, j: (nt - 1 - j, c)) if reverse else (lambda c, j: (j, c))

    def body(br_ref, bi_ref, a_ref, fr_ref, fi_ref, sr, si):
        j = pl.program_id(1)

        @pl.when(j == 0)
        def _():
            sr[...] = jnp.zeros_like(sr)
            si[...] = jnp.zeros_like(si)

        ar = jnp.broadcast_to(a_ref[0:1, :], (SUBLANES, tn))
        ai = jnp.broadcast_to(a_ref[1:2, :], (SUBLANES, tn))

        def step(ii, carry):
            i = (ti - 1 - ii) if reverse else ii
            off = pl.multiple_of(i * SUBLANES, SUBLANES)
            return _scan_step(ar, ai, carry[0], carry[1], br_ref[pl.ds(off, SUBLANES), :], bi_ref[pl.ds(off, SUBLANES), :])

        xr, xi = lax.fori_loop(0, ti, step, (sr[...], si[...]), unroll=4)
        sr[...] = xr
        si[...] = xi

        @pl.when(j == nt - 1)
        def _():
            fr_ref[...] = xr
            fi_ref[...] = xi

    bspec = pl.BlockSpec((tr, tn), tmap)
    fspec = pl.BlockSpec((SUBLANES, tn), lambda c, j: (0, c))
    return pl.pallas_call(
        body,
        name=name,
        grid=(nc, nt),
        in_specs=[bspec, bspec, pl.BlockSpec((2, tn), lambda c, j: (0, c))],
        out_specs=[fspec, fspec],
        out_shape=[jax.ShapeDtypeStruct((SUBLANES, n), F32)] * 2,
        scratch_shapes=[pltpu.VMEM((SUBLANES, tn), F32)] * 2,
        compiler_params=_cparams("parallel", "arbitrary"),
    )(b_r, b_i, a)


def _scan_states(b_r, b_i, a, f_r, f_i, reverse, xs, name):
    s, n = b_r.shape
    tr, tn, nt, nc = _scan_specs(s, n)
    ti = tr // SUBLANES
    seg_len = s // SCAN_SEGMENTS
    assert seg_len & (seg_len - 1) == 0
    with_acc = xs is not None
    tmap = (lambda c, j: (nt - 1 - j, c)) if reverse else (lambda c, j: (j, c))
    order = list(range(SCAN_SEGMENTS))[::-1] if reverse else list(range(SCAN_SEGMENTS))

    def body(*refs):
        br_ref, bi_ref, a_ref, fr_ref, fi_ref = refs[:5]
        pos = 5
        if with_acc:
            xr_ref, xi_ref = refs[5:7]
            pos = 7
        or_ref, oi_ref = refs[pos:pos + 2]
        pos += 2
        if with_acc:
            dr_ref, di_ref = refs[pos:pos + 2]
            pos += 2
        sr, si = refs[pos:pos + 2]
        if with_acc:
            accr, acci = refs[pos + 2:pos + 4]
        j = pl.program_id(1)
        a_r1, a_i1 = a_ref[0:1, :], a_ref[1:2, :]

        @pl.when(j == 0)
        def _():
            pr, pi = a_r1, a_i1
            for _ in range(seg_len.bit_length() - 1):
                pr, pi = pr * pr - pi * pi, 2.0 * pr * pi
            cr = jnp.zeros((1, tn), F32)
            ci = jnp.zeros((1, tn), F32)
            for idx, k in enumerate(order):
                if idx > 0:
                    kp = order[idx - 1]
                    cr, ci = (fr_ref[kp:kp + 1, :] + pr * cr - pi * ci, fi_ref[kp:kp + 1, :] + pr * ci + pi * cr)
                sr[k:k + 1, :] = cr
                si[k:k + 1, :] = ci
            if with_acc:
                accr[...] = jnp.zeros_like(accr)
                acci[...] = jnp.zeros_like(acci)

        ar = jnp.broadcast_to(a_r1, (SUBLANES, tn))
        ai = jnp.broadcast_to(a_i1, (SUBLANES, tn))

        def step(ii, carry):
            i = (ti - 1 - ii) if reverse else ii
            off = pl.multiple_of(i * SUBLANES, SUBLANES)
            rows = pl.ds(off, SUBLANES)
            xr, xi = carry[0], carry[1]
            if with_acc:
                zr, zi = xr_ref[rows, :], xi_ref[rows, :]
                acc = (carry[2] + xr * zr + xi * zi, carry[3] + xi * zr - xr * zi)
            nr, ni = _scan_step(ar, ai, xr, xi, br_ref[rows, :], bi_ref[rows, :])
            or_ref[rows, :] = nr
            oi_ref[rows, :] = ni
            return (nr, ni) + (acc if with_acc else ())

        init = (sr[...], si[...]) + ((accr[...], acci[...]) if with_acc else ())
        out = lax.fori_loop(0, ti, step, init, unroll=4)
        sr[...] = out[0]
        si[...] = out[1]
        if with_acc:
            accr[...] = out[2]
            acci[...] = out[3]

            @pl.when(j == nt - 1)
            def _():
                dr_ref[...] = jnp.sum(out[2], axis=0, keepdims=True)
                di_ref[...] = jnp.sum(out[3], axis=0, keepdims=True)

    bspec = pl.BlockSpec((tr, tn), tmap)
    fspec = pl.BlockSpec((SUBLANES, tn), lambda c, j: (0, c))
    dspec = pl.BlockSpec((1, tn), lambda c, j: (0, c))
    in_specs = [bspec, bspec, pl.BlockSpec((2, tn), lambda c, j: (0, c)), fspec, fspec] + ([bspec, bspec] if with_acc else [])
    out_specs = [bspec, bspec] + ([dspec, dspec] if with_acc else [])
    out_shape = [jax.ShapeDtypeStruct((s, n), F32)] * 2 + ([jax.ShapeDtypeStruct((1, n), F32)] * 2 if with_acc else [])
    scratch = [pltpu.VMEM((SUBLANES, tn), F32)] * (4 if with_acc else 2)
    args = (b_r, b_i, a, f_r, f_i) + (tuple(xs) if with_acc else ())
    return pl.pallas_call(
        body,
        name=name,
        grid=(nc, nt),
        in_specs=in_specs,
        out_specs=out_specs,
        out_shape=out_shape,
        scratch_shapes=scratch,
        compiler_params=_cparams("parallel", "arbitrary"),
    )(*args)


@jax.custom_vjp
def s5_scan(b_r, b_i, a):
    f_r, f_i = _scan_finals(b_r, b_i, a, False, "s5_scan_fin")
    return tuple(_scan_states(b_r, b_i, a, f_r, f_i, False, None, "s5_scan"))


def _s5_scan_fwd(b_r, b_i, a):
    xs = s5_scan(b_r, b_i, a)
    return xs, (a, xs)


def _s5_scan_bwd(res, g):
    a, xs = res
    a_conj = a * jnp.array([[1.0], [-1.0]], F32)
    f_r, f_i = _scan_finals(g[0], g[1], a_conj, True, "s5_rscan_fin")
    g_r, g_i, da_r, da_i = _scan_states(g[0], g[1], a_conj, f_r, f_i, True, xs, "s5_rscan")
    return g_r, g_i, jnp.concatenate([da_r, da_i], axis=0)


s5_scan.defvjp(_s5_scan_fwd, _s5_scan_bwd)


def _to_segments(x):
    s, f = x.shape
    return x.reshape(SCAN_SEGMENTS, s // SCAN_SEGMENTS, f).transpose(1, 0, 2).reshape(s, f)


def _from_segments(x):
    s, f = x.shape
    return x.reshape(s // SCAN_SEGMENTS, SCAN_SEGMENTS, f).transpose(1, 0, 2).reshape(s, f)


_NN = (((1,), (0,)), ((), ()))
_NT = (((1,), (1,)), ((), ()))
_TN = (((0,), (0,)), ((), ()))


def _dot(a, b, dn):
    return lax.dot_general(a.astype(BF16), b.astype(BF16), dn, preferred_element_type=F32)


@jax.custom_vjp
def bdot_nn(a, b):
    return _dot(a, b, _NN)


bdot_nn.defvjp(lambda a, b: (_dot(a, b, _NN), (a, b)),
               lambda r, g: (_dot(g, r[1], _NT), _dot(r[0], g, _TN)))


@jax.custom_vjp
def bdot_nt(a, b):
    return _dot(a, b, _NT)


bdot_nt.defvjp(lambda a, b: (_dot(a, b, _NT), (a, b)),
               lambda r, g: (_dot(g, r[1], _NN), _dot(g, r[0], _TN)))


@jax.custom_vjp
def bdot_tn(a, b):
    return _dot(a, b, _TN)


bdot_tn.defvjp(lambda a, b: (_dot(a, b, _TN), (a, b)),
               lambda r, g: (_dot(r[1], g, _NT), _dot(r[0], g, _NN)))


def _split3(x):
    h = x.astype(BF16)
    r = x - h.astype(F32)
    m = r.astype(BF16)
    l = (r - m.astype(F32)).astype(BF16)
    return h, m, l


def _exact_dot(t, x, dn):
    h, m, l = _split3(x)
    d = lambda p: lax.dot_general(t, p, dn, preferred_element_type=F32)
    return d(h) + d(m) + d(l)


@jax.custom_vjp
def tri_cumsum(t, x):
    return _exact_dot(t, x, _NN)


tri_cumsum.defvjp(lambda t, x: (_exact_dot(t, x, _NN), t),
                  lambda t, g: (jnp.zeros_like(t), _exact_dot(t, g, _TN)))


def _split_lanes_impl(x, w):
    return tuple(x[:, i * w:(i + 1) * w] for i in range(x.shape[1] // w))


@functools.partial(jax.custom_vjp, nondiff_argnums=(1,))
def split_lanes(x, w):
    return _split_lanes_impl(x, w)


split_lanes.defvjp(lambda x, w: (_split_lanes_impl(x, w), None),
                   lambda w, r, g: (jnp.concatenate(g, axis=1),))


def _join_impl(parts):
    return jnp.concatenate(parts, axis=1)


@jax.custom_vjp
def join_lanes(parts):
    return _join_impl(parts)


def _join_bwd(ws, g):
    out, off = [], 0
    for w in ws:
        out.append(g[:, off:off + w])
        off += w
    return (tuple(out),)


join_lanes.defvjp(lambda parts: (_join_impl(parts), tuple(p.shape[1] for p in parts)), _join_bwd)


def _rope_impl(x, c, sa, sb, shift):
    w = x.shape[1]
    return x * c + pltpu.roll(x, w - shift, 1) * sa + pltpu.roll(x, shift, 1) * sb


@functools.partial(jax.custom_vjp, nondiff_argnums=(4,))
def rope_lanes(x, c, sa, sb, shift):
    return _rope_impl(x, c, sa, sb, shift)


def _rope_bwd(shift, r, g):
    c, sa, sb = r
    w = g.shape[1]
    dx = g * c + pltpu.roll(g * sa, shift, 1) + pltpu.roll(g * sb, w - shift, 1)
    return dx, jnp.zeros_like(c), jnp.zeros_like(sa), jnp.zeros_like(sb)


rope_lanes.defvjp(lambda x, c, sa, sb, shift: (_rope_impl(x, c, sa, sb, shift), (c, sa, sb)), _rope_bwd)


RMS_EPS = 1e-6


def _rms(x, g):
    return x * lax.rsqrt(jnp.mean(x * x, axis=-1, keepdims=True) + RMS_EPS) * g


ATTN_BLOCK = 512
MASK_VALUE = -1e30


def _causal_mask(t):
    r = lax.broadcasted_iota(jnp.int32, (t, t), 0)
    c = lax.broadcasted_iota(jnp.int32, (t, t), 1)
    return c <= r


def _attn_fwd_call(q, k, v, scale):
    s, width = q.shape
    n_heads = width // LANES
    tq = min(ATTN_BLOCK, s)
    nq = s // tq

    def body(q_ref, k_ref, v_ref, o_ref, lse_ref):
        i = pl.program_id(1)
        qb = q_ref[...].astype(BF16)

        def block(kb, carry, masked):
            m, l, acc = carry
            rows = pl.ds(pl.multiple_of(kb * tq, tq), tq)
            sc = lax.dot_general(qb, k_ref[rows, :].astype(BF16), _NT, preferred_element_type=F32) * scale
            if masked:
                sc = jnp.where(_causal_mask(tq), sc, MASK_VALUE)
            m_new = jnp.maximum(m, jnp.max(sc, axis=-1, keepdims=True))
            alpha = jnp.exp(m - m_new)
            p = jnp.exp(sc - m_new)
            l = alpha * l + jnp.sum(p, axis=-1, keepdims=True)
            acc = alpha * acc + lax.dot_general(p.astype(BF16), v_ref[rows, :].astype(BF16), _NN, preferred_element_type=F32)
            return m_new, l, acc

        init = (jnp.full((tq, 1), MASK_VALUE, F32), jnp.zeros((tq, 1), F32), jnp.zeros((tq, LANES), F32))
        carry = lax.fori_loop(0, i, lambda kb, c: block(kb, c, False), init)
        m, l, acc = block(i, carry, True)
        o_ref[...] = (acc / l).astype(o_ref.dtype)
        lse_ref[...] = jnp.broadcast_to(m + jnp.log(l), (tq, LANES))

    qspec = pl.BlockSpec((tq, LANES), lambda h, i: (i, h))
    kspec = pl.BlockSpec((s, LANES), lambda h, i: (0, h))
    return pl.pallas_call(
        body,
        name="mla_attn_fwd",
        grid=(n_heads, nq),
        in_specs=[qspec, kspec, kspec],
        out_specs=[qspec, qspec],
        out_shape=[jax.ShapeDtypeStruct((s, width), BF16), jax.ShapeDtypeStruct((s, width), F32)],
        compiler_params=_cparams("parallel", "parallel"),
    )(q, k, v)


def _attn_bwd_call(q, k, v, o, lse, do, scale):
    s, width = q.shape
    n_heads = width // LANES
    tq = min(ATTN_BLOCK, s)
    nq = s // tq

    def body(q_ref, k_ref, v_ref, o_ref, lse_ref, do_ref, dq_ref, dk_ref, dv_ref, dq_acc):
        j = pl.program_id(1)

        @pl.when(j == 0)
        def _():
            dq_acc[...] = jnp.zeros_like(dq_acc)

        kb = k_ref[...].astype(BF16)
        vb = v_ref[...].astype(BF16)

        def block(i, carry, masked):
            dk, dv = carry
            rows = pl.ds(pl.multiple_of(i * tq, tq), tq)
            qi = q_ref[rows, :].astype(BF16)
            doi = do_ref[rows, :].astype(F32)
            delta = jnp.sum(doi * o_ref[rows, :].astype(F32), axis=-1, keepdims=True)
            sc = lax.dot_general(qi, kb, _NT, preferred_element_type=F32) * scale
            if masked:
                sc = jnp.where(_causal_mask(tq), sc, MASK_VALUE)
            p = jnp.exp(sc - lse_ref[rows, 0:1])
            dob = doi.astype(BF16)
            dv = dv + lax.dot_general(p.astype(BF16), dob, _TN, preferred_element_type=F32)
            dp = lax.dot_general(dob, vb, _NT, preferred_element_type=F32)
            ds = (p * (dp - delta) * scale).astype(BF16)
            dq_acc[rows, :] += lax.dot_general(ds, kb, _NN, preferred_element_type=F32)
            dk = dk + lax.dot_general(ds, qi, _TN, preferred_element_type=F32)
            return dk, dv

        zero = jnp.zeros((tq, LANES), F32)
        carry = block(j, (zero, zero), True)
        dk, dv = lax.fori_loop(j + 1, nq, lambda i, c: block(i, c, False), carry)
        dk_ref[...] = dk.astype(dk_ref.dtype)
        dv_ref[...] = dv.astype(dv_ref.dtype)

        @pl.when(j == nq - 1)
        def _():
            dq_ref[...] = dq_acc[...].astype(dq_ref.dtype)

    full = pl.BlockSpec((s, LANES), lambda h, j: (0, h))
    blk = pl.BlockSpec((tq, LANES), lambda h, j: (j, h))
    return pl.pallas_call(
        body,
        name="mla_attn_bwd",
        grid=(n_heads, nq),
        in_specs=[full, blk, blk, full, full, full],
        out_specs=[full, blk, blk],
        out_shape=[jax.ShapeDtypeStruct((s, width), t.dtype) for t in (q, k, v)],
        scratch_shapes=[pltpu.VMEM((s, LANES), F32)],
        compiler_params=_cparams("parallel", "arbitrary"),
    )(q, k, v, o, lse, do)


@functools.partial(jax.custom_vjp, nondiff_argnums=(3,))
def causal_attention(q, k, v, scale):
    return _attn_fwd_call(q, k, v, scale)[0]


def _causal_attention_fwd(q, k, v, scale):
    o, lse = _attn_fwd_call(q, k, v, scale)
    return o, (q, k, v, o, lse)


def _causal_attention_bwd(scale, res, do):
    return tuple(_attn_bwd_call(*res, do, scale))


causal_attention.defvjp(_causal_attention_fwd, _causal_attention_bwd)


HG_HEADS = 4
HG_CHUNK = 32
HG_REF_ROW = HG_CHUNK // 2 - 1
HG_TILE_ROWS = 256
HG_EXP_CLAMP = 80.0


def _hg_tri():
    r = lax.broadcasted_iota(jnp.int32, (HG_CHUNK, HG_CHUNK), 0)
    c = lax.broadcasted_iota(jnp.int32, (HG_CHUNK, HG_CHUNK), 1)
    return c <= r


def _hg_chunk(q, fl, v, lb, st):
    causal = _hg_tri()
    f = lb + (1.0 - lb) * jax.nn.sigmoid(fl)
    kk = 1.0 - f
    qs = q * jax.nn.sigmoid(q)
    b = tri_cumsum(causal.astype(BF16), jnp.log(f))
    rid = lax.broadcasted_iota(jnp.int32, b.shape, 0)
    b_ref = jnp.sum(jnp.where(rid == HG_REF_ROW, b, 0.0), axis=0, keepdims=True)
    b_last = jnp.sum(jnp.where(rid == HG_CHUNK - 1, b, 0.0), axis=0, keepdims=True)
    q_in = qs * jnp.exp(jnp.minimum(b - b_ref, HG_EXP_CLAMP))
    k_in = kk * jnp.exp(jnp.minimum(b_ref - b, HG_EXP_CLAMP))
    attn = jnp.where(causal, bdot_nt(q_in, k_in), 0.0)
    o = bdot_nn(attn, v) + bdot_nt(qs * jnp.exp(b), st)
    st_new = st * jnp.exp(b_last) + bdot_tn(v, kk * jnp.exp(b_last - b))
    return o, st_new


def _hg_fwd_call(q, fl, v, lb):
    s, width = q.shape
    tr = min(HG_TILE_ROWS, s)
    ncl = tr // HG_CHUNK
    nt = s // tr

    def body(q_ref, f_ref, v_ref, lb_ref, o_ref, sts_ref, st_ref):
        @pl.when(pl.program_id(0) == 0)
        def _():
            st_ref[...] = jnp.zeros_like(st_ref)

        def chunk(c, carry):
            rows = pl.ds(pl.multiple_of(c * HG_CHUNK, HG_CHUNK), HG_CHUNK)
            for h in range(HG_HEADS):
                ln = slice(h * LANES, (h + 1) * LANES)
                st = st_ref[h]
                sts_ref[c, h] = st
                o, st_new = _hg_chunk(q_ref[rows, ln], f_ref[rows, ln], v_ref[rows, ln], lb_ref[:, ln], st)
                o_ref[rows, ln] = o
                st_ref[h] = st_new
            return carry

        lax.fori_loop(0, ncl, chunk, 0)

    rspec = pl.BlockSpec((tr, width), lambda j: (j, 0))
    return pl.pallas_call(
        body,
        name="hgrn2_fwd",
        grid=(nt,),
        in_specs=[rspec, rspec, rspec, pl.BlockSpec((1, width), lambda j: (0, 0))],
        out_specs=[rspec, pl.BlockSpec((ncl, HG_HEADS, LANES, LANES), lambda j: (j, 0, 0, 0))],
        out_shape=[jax.ShapeDtypeStruct((s, width), F32),
                   jax.ShapeDtypeStruct((s // HG_CHUNK, HG_HEADS, LANES, LANES), F32)],
        scratch_shapes=[pltpu.VMEM((HG_HEADS, LANES, LANES), F32)],
        compiler_params=_cparams("arbitrary"),
    )(q, fl, v, lb)


def _hg_bwd_call(q, fl, v, lb, sts, do):
    s, width = q.shape
    tr = min(HG_TILE_ROWS, s)
    ncl = tr // HG_CHUNK
    nt = s // tr

    def body(q_ref, f_ref, v_ref, lb_ref, sts_ref, do_ref, dq_ref, df_ref, dv_ref, dlb_ref, dst_ref):
        @pl.when(pl.program_id(0) == 0)
        def _():
            dst_ref[...] = jnp.zeros_like(dst_ref)
            dlb_ref[...] = jnp.zeros_like(dlb_ref)

        def chunk(cc, carry):
            c = ncl - 1 - cc
            rows = pl.ds(pl.multiple_of(c * HG_CHUNK, HG_CHUNK), HG_CHUNK)
            for h in range(HG_HEADS):
                ln = slice(h * LANES, (h + 1) * LANES)
                _, vjp = jax.vjp(_hg_chunk, q_ref[rows, ln], f_ref[rows, ln], v_ref[rows, ln], lb_ref[:, ln], sts_ref[c, h])
                dq, df, dv, dlb, dst = vjp((do_ref[rows, ln], dst_ref[h]))
                dq_ref[rows, ln] = dq
                df_ref[rows, ln] = df
                dv_ref[rows, ln] = dv
                dlb_ref[:, ln] += dlb
                dst_ref[h] = dst
            return carry

        lax.fori_loop(0, ncl, chunk, 0)

    rspec = pl.BlockSpec((tr, width), lambda j: (nt - 1 - j, 0))
    pspec = pl.BlockSpec((1, width), lambda j: (0, 0))
    return pl.pallas_call(
        body,
        name="hgrn2_bwd",
        grid=(nt,),
        in_specs=[rspec, rspec, rspec, pspec,
                  pl.BlockSpec((ncl, HG_HEADS, LANES, LANES), lambda j: (nt - 1 - j, 0, 0, 0)), rspec],
        out_specs=[rspec, rspec, rspec, pspec],
        out_shape=[jax.ShapeDtypeStruct((s, width), F32)] * 3 + [jax.ShapeDtypeStruct((1, width), F32)],
        scratch_shapes=[pltpu.VMEM((HG_HEADS, LANES, LANES), F32)],
        compiler_params=_cparams("arbitrary"),
    )(q, fl, v, lb, sts, do)


@jax.custom_vjp
def hgrn2_core(q, fl, v, lb):
    return _hg_fwd_call(q, fl, v, lb)[0]


def _hgrn2_core_fwd(q, fl, v, lb):
    o, sts = _hg_fwd_call(q, fl, v, lb)
    return o, (q, fl, v, lb, sts)


def _hgrn2_core_bwd(res, do):
    return tuple(_hg_bwd_call(*res, do))


hgrn2_core.defvjp(_hgrn2_core_fwd, _hgrn2_core_bwd)


D_MODEL = 1024
DEPTH = 2
SSM_GROUPS, SSM_GROUP_CH, SSM_STATE = 32, 16, 64
SSM_WIDTH = SSM_GROUPS * SSM_GROUP_CH
MLA_HEADS, MLA_NOPE, MLA_ROPE, MLA_V = 8, 64, 32, 64
MLA_Q_RANK, MLA_KV_RANK = 512, 256
HG_WIDTH = HG_HEADS * LANES
X_HEADS, X_HEAD_DIM = 4, 128
X_WIDTH = X_HEADS * X_HEAD_DIM
D_FF = 2816
ROPE_THETA = 10000.0
IN_SPLITS = (SSM_WIDTH, MLA_Q_RANK, MLA_KV_RANK, MLA_ROPE, HG_WIDTH, HG_WIDTH, HG_WIDTH, HG_WIDTH, 3 * D_MODEL)
ROPE_LANE0 = MLA_NOPE
ROW_TILE = 256


def _t_rms(x, g):
    return (_rms(x, g).astype(BF16),)


def _t_s5_act(y, u, d):
    return (jax.nn.gelu(y + d * u).astype(BF16),)


def _t_glu(zo, zg):
    return (zo * jax.nn.sigmoid(zg),)


def _t_mla_rope(q, k, kr, c, sa, sb):
    rep = lambda t: jnp.concatenate([t] * MLA_HEADS, axis=1)
    half = MLA_ROPE // 2
    q_out = rope_lanes(q, rep(c), rep(sa), rep(sb), half)
    kr_out = rope_lanes(kr, c, sa, sb, half)
    return q_out.astype(BF16), (k + join_lanes((kr_out,) * MLA_HEADS)).astype(BF16)


def _t_hg_post(o, gate, gn):
    os_, gs = split_lanes(o, LANES), split_lanes(gate, LANES)
    return (join_lanes(tuple(_rms(a, gn) * (b * jax.nn.sigmoid(b)) for a, b in zip(os_, gs))).astype(BF16),)


def _t_merge(y_ssm, y_mla, y_hg, g0, g1, g2):
    return ((jax.nn.sigmoid(g0) * y_ssm + jax.nn.sigmoid(g1) * y_mla + jax.nn.sigmoid(g2) * y_hg).astype(BF16),)


def _t_xattn(q, k, v):
    scale = 1.0 / math.sqrt(X_HEAD_DIM)
    outs = []
    for qh, kh, vh in zip(split_lanes(q, X_HEAD_DIM), split_lanes(k, X_HEAD_DIM), split_lanes(v, X_HEAD_DIM)):
        sc = bdot_nt(qh, kh) * scale
        p = jnp.exp(sc - jnp.max(sc, axis=-1, keepdims=True))
        p = p / jnp.sum(p, axis=-1, keepdims=True)
        outs.append(bdot_nn(p, vh))
    return (join_lanes(tuple(outs)).astype(BF16),)


def _t_swiglu(gt, up):
    return ((gt * jax.nn.sigmoid(gt) * up).astype(BF16),)


def _t_loss(x, tgt, g):
    e = _rms(x, g) - tgt
    return (jnp.broadcast_to(jnp.mean(e * e, axis=-1, keepdims=True), (x.shape[0], LANES)),)


rms_op = rowwise(_t_rms, 1, 0, ROW_TILE, "rmsnorm")
s5_act_op = rowwise(_t_s5_act, 2, 0, ROW_TILE, "s5_act")
glu_op = rowwise(_t_glu, 2, 0, ROW_TILE, "glu")
mla_rope_op = rowwise(_t_mla_rope, 3, 3, ROW_TILE, "mla_rope")
hg_post_op = rowwise(_t_hg_post, 2, 0, ROW_TILE, "hg_post")
merge_op = rowwise(_t_merge, 6, 0, ROW_TILE, "merge")
xattn_op = rowwise(_t_xattn, 1, 0, ROW_TILE, "xattn")
swiglu_op = rowwise(_t_swiglu, 2, 0, ROW_TILE, "swiglu")
loss_op = rowwise(_t_loss, 1, 1, ROW_TILE, "loss")


def _rope_tables(positions):
    half = MLA_ROPE // 2
    inv_freq = ROPE_THETA ** (-jnp.arange(half, dtype=F32) / half)
    ang = positions.astype(F32)[:, None] * inv_freq
    cos, sin = jnp.cos(ang), jnp.sin(ang)
    s = positions.shape[0]
    z = lambda w: jnp.zeros((s, w), F32)
    tail = LANES - ROPE_LANE0 - MLA_ROPE
    c = jnp.concatenate([jnp.ones((s, ROPE_LANE0), F32), cos, cos, z(tail)], axis=1)
    sa = jnp.concatenate([z(ROPE_LANE0), -sin, z(half), z(tail)], axis=1)
    sb = jnp.concatenate([z(ROPE_LANE0), z(half), sin, z(tail)], axis=1)
    return c, sa, sb


def _s5_operators(lam_re, lam_im, b_re, b_im, c_re, c_im, log_step):
    g, p, h = SSM_GROUPS, SSM_STATE, SSM_GROUP_CH
    lam = lax.complex(lam_re, lam_im)
    lam_bar = jnp.exp(lam * jnp.exp(log_step)[:, None])
    b_bar = ((lam_bar - 1.0) / lam)[..., None] * lax.complex(b_re, b_im)
    eye = jnp.eye(g, dtype=F32)
    bd = lambda t: jnp.einsum("gph,gk->ghkp", t, eye).reshape(g * h, g * p)
    cd = lambda t: jnp.einsum("ghp,gk->gpkh", t, eye).reshape(g * p, g * h)
    a = jnp.stack([jnp.real(lam_bar).reshape(-1), jnp.imag(lam_bar).reshape(-1)])
    return a, bd(jnp.real(b_bar)), bd(jnp.imag(b_bar)), cd(c_re), cd(-c_im)


IN_PAD = 6656
_SEG = {}
_off = 0
for _name, _w in (("u", 512), ("q_lat", 512), ("kv_lat", 256), ("k_rope", 128), ("hg_q", 512), ("hg_f", 512),
                  ("hg_i", 512), ("hg_g", 512), ("g0", 1024), ("g1", 1024), ("g2", 1024)):
    _SEG[_name] = (_off, _off + _w)
    _off += _w


def _layer_matrices(w, l):
    w_in = w["w_in"][l]
    d, dt = w_in.shape[0], w_in.dtype
    z = lambda n: jnp.zeros((d, n), dt)
    r0 = SSM_WIDTH + MLA_Q_RANK + MLA_KV_RANK
    w_proj = jnp.concatenate([w_in[:, :r0], z(ROPE_LANE0), w_in[:, r0:r0 + MLA_ROPE], z(LANES - ROPE_LANE0 - MLA_ROPE),
                              w_in[:, r0 + MLA_ROPE:], z(IN_PAD - _off)], axis=1)
    pad_heads = lambda t: jnp.pad(t, ((0, 0), (0, 0), (0, LANES - t.shape[2]))).reshape(t.shape[0], -1)
    uq = w["mla_w_uq"][l].reshape(MLA_Q_RANK, MLA_HEADS, MLA_NOPE + MLA_ROPE)
    ukv = w["mla_w_ukv"][l].reshape(MLA_KV_RANK, MLA_HEADS, MLA_NOPE + MLA_V)
    wo = w["mla_w_o"][l].reshape(MLA_HEADS, MLA_V, D_MODEL)
    glu, xkv, gu = w["ssm_w_glu"][l], w["x_w_kv"][l], w["ffn_w_gate_up"][l]
    return dict(
        w_proj=w_proj, glu_o=glu[:, :D_MODEL], glu_g=glu[:, D_MODEL:],
        uq=pad_heads(uq), uk=pad_heads(ukv[:, :, :MLA_NOPE]), uv=pad_heads(ukv[:, :, MLA_NOPE:]),
        mla_o=jnp.pad(wo, ((0, 0), (0, LANES - MLA_V), (0, 0))).reshape(MLA_HEADS * LANES, D_MODEL),
        hg_o=w["hg_w_o"][l], w_out=w["w_out"][l], x_q=w["x_w_q"][l], x_k=xkv[:, :X_WIDTH], x_v=xkv[:, X_WIDTH:],
        x_o=w["x_w_o"][l], ffn_g=gu[:, :D_FF], ffn_u=gu[:, D_FF:], ffn_d=w["ffn_w_down"][l])


def _layer(x, mem, tabs, m, sp, l, lower_bound):
    row = lambda name: sp[name][l].reshape(1, -1)
    (h,) = rms_op(x, row("norm_mix"))
    proj = matmul(h, m["w_proj"])
    seg = lambda name: proj[:, _SEG[name][0]:_SEG[name][1]]
    a, bd_r, bd_i, cd_r, cd_i = _s5_operators(*(sp[n][l] for n in (
        "ssm_lam_re", "ssm_lam_im", "ssm_b_re", "ssm_b_im", "ssm_c_re", "ssm_c_im", "ssm_log_step")))
    u = seg("u")
    us = _to_segments(u)
    x_r, x_i = s5_scan(matmul(us, bd_r), matmul(us, bd_i), a)
    y = _from_segments(matmul_add(x_i, cd_i, matmul(x_r, cd_r)))
    (ya,) = s5_act_op(y, u, row("ssm_d"))
    (y_ssm,) = glu_op(matmul(ya, m["glu_o"]), matmul(ya, m["glu_g"]))
    (qn,) = rms_op(seg("q_lat"), row("mla_q_norm"))
    (kvn,) = rms_op(seg("kv_lat"), row("mla_kv_norm"))
    q, k = mla_rope_op(matmul(qn, m["uq"]), matmul(kvn, m["uk"]), seg("k_rope"), *tabs)
    o = causal_attention(q, k, matmul(kvn, m["uv"]), 1.0 / math.sqrt(MLA_NOPE + MLA_ROPE))
    y_mla = matmul(o, m["mla_o"])
    o = hgrn2_core(seg("hg_q"), seg("hg_f"), seg("hg_i"), lower_bound)
    (og,) = hg_post_op(o, seg("hg_g"), row("hg_g_norm"))
    y_hg = matmul(og, m["hg_o"])
    (merged,) = merge_op(y_ssm, y_mla, y_hg, seg("g0"), seg("g1"), seg("g2"))
    x = matmul_add(merged, m["w_out"], x)
    (hc,) = rms_op(x, row("norm_cross"))
    (mn,) = rms_op(mem, row("norm_mem"))
    (ox,) = xattn_op(matmul(hc, m["x_q"]), matmul(mn, m["x_k"]), matmul(mn, m["x_v"]))
    x = matmul_add(ox, m["x_o"], x)
    (hf,) = rms_op(x, row("norm_ffn"))
    (act,) = swiglu_op(matmul(hf, m["ffn_g"]), matmul(hf, m["ffn_u"]))
    return matmul_add(act, m["ffn_d"], x)


def _local_loss(x, mem, positions, target, w, sp):
    tabs = _rope_tables(positions)
    lb_p = jax.nn.softmax(sp["hg_lb"], axis=0)
    lower = jnp.cumsum(lb_p, axis=0) - lb_p[0:1]
    for l in range(DEPTH):
        x = _layer(x, mem, tabs, _layer_matrices(w, l), sp, l, lower[l].reshape(1, -1))
    (row_loss,) = loss_op(x, target, sp["norm_final"].reshape(1, -1))
    return 0.5 * jnp.sum(row_loss[:, 0])


N_DEV = 8
N_CHIPS = 4
COMM_LANES = 512
MESH_ID = pl.DeviceIdType.MESH
_ANY = pl.BlockSpec(memory_space=pl.ANY)
_OTHER_CHIPS = ((1, 0), (0, 1), (1, 1))


def _place():
    return lax.axis_index("x"), lax.axis_index("y"), lax.axis_index("c")


def _all_gather_call(blocks, name):
    n = len(blocks)

    def body(*refs):
        x_refs, out_refs = refs[:n], refs[n:2 * n]
        send_sems, recv_sems, local_sems = refs[2 * n:]
        x, y, c = _place()
        me, sibling = (x, y, c), (x, y, 1 - c)
        chips = [(x ^ fx, y ^ fy) for fx, fy in _OTHER_CHIPS]

        def slot(i, px, py, pc):
            return out_refs[i].at[4 * px + 2 * py + pc]

        def copy(i, k, blk, to, src=None):
            return pltpu.make_async_remote_copy(
                src_ref=slot(i, *blk) if src is None else src, dst_ref=slot(i, *blk),
                send_sem=send_sems.at[i, k], recv_sem=recv_sems.at[i, k], device_id=to, device_id_type=MESH_ID)

        mine = [pltpu.make_async_copy(x_refs[i], slot(i, *me), local_sems.at[i]) for i in range(n)]
        first = []
        for i in range(n):
            first.append(copy(i, 0, me, sibling, src=x_refs[i]))
            first += [copy(i, 1 + j, me, (*chip, c), src=x_refs[i]) for j, chip in enumerate(chips)]
        for cp in mine + first:
            cp.start()
        passed = []
        for j, chip in enumerate(chips):
            for i in range(n):
                copy(i, 1 + j, (*chip, c), me).wait_recv()
                passed.append(copy(i, 4 + j, (*chip, c), sibling))
                passed[-1].start()
        for i in range(n):
            copy(i, 0, sibling, me).wait_recv()
            for j, chip in enumerate(chips):
                copy(i, 4 + j, (*chip, 1 - c), me).wait_recv()
        for cp in first + passed:
            cp.wait_send()
        for cp in mine:
            cp.wait()

    return pl.pallas_call(
        body,
        name=name,
        out_shape=[jax.ShapeDtypeStruct((N_DEV,) + b.shape, b.dtype) for b in blocks],
        in_specs=[_ANY] * n,
        out_specs=[_ANY] * n,
        scratch_shapes=[pltpu.SemaphoreType.DMA((n, 7)), pltpu.SemaphoreType.DMA((n, 7)), pltpu.SemaphoreType.DMA((n,))],
    )(*blocks)


def _pair_exchange_call(gs, name):
    n = len(gs)

    def body(*refs):
        g_refs, got_refs = refs[:n], refs[n:2 * n]
        send_sems, recv_sems = refs[2 * n:]
        x, y, c = _place()
        sends = [pltpu.make_async_remote_copy(
            src_ref=g_refs[i].at[2 * p + (1 - c)], dst_ref=got_refs[i].at[p],
            send_sem=send_sems.at[i, p], recv_sem=recv_sems.at[i, p], device_id=(x, y, 1 - c), device_id_type=MESH_ID)
            for i in range(n) for p in range(N_CHIPS)]
        for cp in sends:
            cp.start()
        for cp in sends:
            cp.wait_recv()
        for cp in sends:
            cp.wait_send()

    return pl.pallas_call(
        body,
        name=name,
        out_shape=[jax.ShapeDtypeStruct((N_CHIPS,) + g.shape[1:], g.dtype) for g in gs],
        in_specs=[_ANY] * n,
        out_specs=[_ANY] * n,
        scratch_shapes=[pltpu.SemaphoreType.DMA((n, N_CHIPS))] * 2,
    )(*gs)


def _chip_exchange_call(parts, name):
    n = len(parts)

    def body(*refs):
        p_refs, got_refs = refs[:n], refs[n:2 * n]
        send_sems, recv_sems = refs[2 * n:]
        x, y, c = _place()
        sends = []
        for i in range(n):
            for k, (fx, fy) in enumerate(_OTHER_CHIPS):
                px, py = x ^ fx, y ^ fy
                sends.append(pltpu.make_async_remote_copy(
                    src_ref=p_refs[i].at[2 * px + py], dst_ref=got_refs[i].at[k],
                    send_sem=send_sems.at[i, k], recv_sem=recv_sems.at[i, k], device_id=(px, py, c), device_id_type=MESH_ID))
        for cp in sends:
            cp.start()
        for cp in sends:
            cp.wait_recv()
        for cp in sends:
            cp.wait_send()

    return pl.pallas_call(
        body,
        name=name,
        out_shape=[jax.ShapeDtypeStruct((3,) + p.shape[1:], p.dtype) for p in parts],
        in_specs=[_ANY] * n,
        out_specs=[_ANY] * n,
        scratch_shapes=[pltpu.SemaphoreType.DMA((n, 3))] * 2,
    )(*parts)


def _rows_cols(shape):
    return math.prod(shape[:-1]), shape[-1]


def _pair_sum_call(g, got, c_idx, name):
    rows, cols = _rows_cols(got.shape[1:])
    tr = _pick_tile(rows, (512, 256, 128, 64, 32, 16))

    def body(c_ref, a_ref, b_ref, o_ref):
        o_ref[...] = (a_ref[...].astype(F32) + b_ref[...].astype(F32)).astype(o_ref.dtype)

    spec = pl.BlockSpec((1, tr, cols), lambda p, i, c_ref: (p, i, 0))
    out = pl.pallas_call(
        body,
        name=name,
        grid_spec=pltpu.PrefetchScalarGridSpec(
            num_scalar_prefetch=1, grid=(N_CHIPS, rows // tr),
            in_specs=[pl.BlockSpec((1, tr, cols), lambda p, i, c_ref: (2 * p + c_ref[0], i, 0)), spec],
            out_specs=spec),
        out_shape=jax.ShapeDtypeStruct((N_CHIPS, rows, cols), got.dtype),
        compiler_params=_cparams("parallel", "parallel"),
    )(c_idx, g.reshape(N_DEV, rows, cols), got.reshape(N_CHIPS, rows, cols))
    return out.reshape(got.shape)


def _chip_sum_call(part, got, chip_idx, name):
    rows, cols = _rows_cols(got.shape[1:])
    tr = _pick_tile(rows, (512, 256, 128, 64, 32, 16))

    def body(p_ref, a_ref, b_ref, o_ref):
        acc = a_ref[0].astype(F32)
        for k in range(3):
            acc = acc + b_ref[k].astype(F32)
        o_ref[...] = acc

    out = pl.pallas_call(
        body,
        name=name,
        grid_spec=pltpu.PrefetchScalarGridSpec(
            num_scalar_prefetch=1, grid=(rows // tr,),
            in_specs=[pl.BlockSpec((1, tr, cols), lambda i, p_ref: (p_ref[0], i, 0)),
                      pl.BlockSpec((3, tr, cols), lambda i, p_ref: (0, i, 0))],
            out_specs=pl.BlockSpec((tr, cols), lambda i, p_ref: (i, 0))),
        out_shape=jax.ShapeDtypeStruct((rows, cols), F32),
        compiler_params=_cparams("parallel"),
    )(chip_idx, part.reshape(N_CHIPS, rows, cols), got.reshape(3, rows, cols))
    return out.reshape(got.shape[1:])


def _reduce_scatter(gs, name):
    x, y, c = _place()
    c_idx = c.astype(jnp.int32).reshape(1)
    chip_idx = (2 * x + y).astype(jnp.int32).reshape(1)
    gots = _pair_exchange_call(gs, name + "_pair")
    parts = [_pair_sum_call(g, got, c_idx, name + "_pair_sum") for g, got in zip(gs, gots)]
    gots = _chip_exchange_call(parts, name + "_chip")
    return [_chip_sum_call(p, got, chip_idx, name + "_chip_sum") for p, got in zip(parts, gots)]


SMALL_BLOCK_ROWS = 16


def _pack_small(parts):
    flat = jnp.concatenate([p.reshape(-1) for p in parts])
    chunk = N_DEV * SMALL_BLOCK_ROWS * COMM_LANES
    flat = jnp.pad(flat, (0, (-flat.shape[0]) % chunk))
    return flat.reshape(N_DEV, -1, COMM_LANES)


def _unpack_small(buf, shapes):
    flat = buf.reshape(-1)
    out, off = [], 0
    for shp in shapes:
        n = math.prod(shp)
        out.append(flat[off:off + n].reshape(shp))
        off += n
    return out


SHARDED = dict(w_in=2, ssm_w_glu=2, mla_w_uq=2, mla_w_ukv=2, mla_w_o=2, hg_w_o=2, w_out=1, x_w_q=1, x_w_kv=1,
               x_w_o=2, ffn_w_gate_up=2, ffn_w_down=1)
REPLICATED = ("norm_mix", "ssm_lam_re", "ssm_lam_im", "ssm_b_re", "ssm_b_im", "ssm_c_re", "ssm_c_im", "ssm_d",
              "ssm_log_step", "mla_q_norm", "mla_kv_norm", "hg_lb", "hg_g_norm", "norm_cross", "norm_mem", "norm_ffn",
              "norm_final")


def _join_shards(stacked, axis):
    n, l, a, b = stacked.shape
    if axis == 1:
        return stacked.transpose(1, 0, 2, 3).reshape(l, n * a, b)
    return stacked.transpose(1, 2, 0, 3).reshape(l, a, n * b)


def _split_shards(full, axis):
    l, a, b = full.shape
    if axis == 1:
        return full.reshape(l, N_DEV, a // N_DEV, b).transpose(1, 0, 2, 3)
    return full.reshape(l, a, N_DEV, b // N_DEV).transpose(2, 0, 1, 3)


@jax.custom_vjp
def gather_weights(shards):
    got = _all_gather_call([s.astype(BF16) for s in shards], "weights_all_gather")
    return tuple(_join_shards(p, ax) for p, ax in zip(got, SHARDED.values()))


def _gather_weights_fwd(shards):
    return gather_weights(shards), None


def _gather_weights_bwd(_, cts):
    gs = [_split_shards(ct, ax) for ct, ax in zip(cts, SHARDED.values())]
    return (tuple(_reduce_scatter(gs, "grads_reduce_scatter")),)


gather_weights.defvjp(_gather_weights_fwd, _gather_weights_bwd)


@jax.custom_vjp
def sync_replicated(params):
    return params


def _sync_replicated_bwd(shapes, cts):
    (mine,) = _reduce_scatter([_pack_small(cts)], "small_reduce_scatter")
    (total,) = _all_gather_call([mine], "small_all_gather")
    return (tuple(_unpack_small(total, shapes)),)


sync_replicated.defvjp(lambda params: (params, tuple(p.shape for p in params)), _sync_replicated_bwd)


ADAM_LR, ADAM_B1, ADAM_B2, ADAM_EPS, ADAM_WD, ADAM_STEP = 0.001, 0.9, 0.999, 1e-08, 0.01, 10


def _adamw_call(w, g, m, v, name):
    shape = w.shape
    cols = shape[-1]
    rows = math.prod(shape[:-1]) if len(shape) > 1 else 1
    tr = _pick_tile(rows, (512, 256, 128, 64, 32, 16, 8))

    def body(w_ref, g_ref, m_ref, v_ref, d_ref, nm_ref, nv_ref):
        gg = g_ref[...]
        m_new = ADAM_B1 * m_ref[...] + (1.0 - ADAM_B1) * gg
        v_new = ADAM_B2 * v_ref[...] + (1.0 - ADAM_B2) * jnp.square(gg)
        m_hat = m_new / (1.0 - ADAM_B1 ** ADAM_STEP)
        v_hat = v_new / (1.0 - ADAM_B2 ** ADAM_STEP)
        d_ref[...] = -ADAM_LR * (m_hat / (jnp.sqrt(v_hat) + ADAM_EPS) + ADAM_WD * w_ref[...])
        nm_ref[...] = m_new
        nv_ref[...] = v_new

    spec = pl.BlockSpec((tr, cols), lambda i: (i, 0))
    outs = pl.pallas_call(
        body, name=name, grid=(rows // tr,), in_specs=[spec] * 4, out_specs=[spec] * 3,
        out_shape=[jax.ShapeDtypeStruct((rows, cols), F32)] * 3, compiler_params=_cparams("parallel"),
    )(*(t.reshape(rows, cols) for t in (w, g, m, v)))
    return tuple(o.reshape(shape) for o in outs)


WEIGHTS = ("norm_mix", "w_in", "ssm_lam_re", "ssm_lam_im", "ssm_b_re", "ssm_b_im", "ssm_c_re", "ssm_c_im", "ssm_d",
           "ssm_log_step", "ssm_w_glu", "mla_q_norm", "mla_kv_norm", "mla_w_uq", "mla_w_ukv", "mla_w_o", "hg_lb",
           "hg_g_norm", "hg_w_o", "w_out", "norm_cross", "norm_mem", "x_w_q", "x_w_kv", "x_w_o", "norm_ffn",
           "ffn_w_gate_up", "ffn_w_down", "norm_final")


def kernel(x, mem, positions, norm_mix, w_in, ssm_lam_re, ssm_lam_im, ssm_b_re, ssm_b_im, ssm_c_re, ssm_c_im, ssm_d, ssm_log_step, ssm_w_glu, mla_q_norm, mla_kv_norm, mla_w_uq, mla_w_ukv, mla_w_o, hg_lb, hg_g_norm, hg_w_o, w_out, norm_cross, norm_mem, x_w_q, x_w_kv, x_w_o, norm_ffn, ffn_w_gate_up, ffn_w_down, norm_final, loss_target, m_norm_mix, m_w_in, m_ssm_lam_re, m_ssm_lam_im, m_ssm_b_re, m_ssm_b_im, m_ssm_c_re, m_ssm_c_im, m_ssm_d, m_ssm_log_step, m_ssm_w_glu, m_mla_q_norm, m_mla_kv_norm, m_mla_w_uq, m_mla_w_ukv, m_mla_w_o, m_hg_lb, m_hg_g_norm, m_hg_w_o, m_w_out, m_norm_cross, m_norm_mem, m_x_w_q, m_x_w_kv, m_x_w_o, m_norm_ffn, m_ffn_w_gate_up, m_ffn_w_down, m_norm_final, v_norm_mix, v_w_in, v_ssm_lam_re, v_ssm_lam_im, v_ssm_b_re, v_ssm_b_im, v_ssm_c_re, v_ssm_c_im, v_ssm_d, v_ssm_log_step, v_ssm_w_glu, v_mla_q_norm, v_mla_kv_norm, v_mla_w_uq, v_mla_w_ukv, v_mla_w_o, v_hg_lb, v_hg_g_norm, v_hg_w_o, v_w_out, v_norm_cross, v_norm_mem, v_x_w_q, v_x_w_kv, v_x_w_o, v_norm_ffn, v_ffn_w_gate_up, v_ffn_w_down, v_norm_final):
    given = dict(locals())
    weights = {n: given[n] for n in WEIGHTS}

    def loss_fn(xs, shards, small):
        full = dict(zip(SHARDED, gather_weights(shards)))
        sp = dict(zip(REPLICATED, sync_replicated(small)))
        return _local_loss(xs, mem[0], positions[0], loss_target[0], full, sp)

    shards = tuple(weights[n] for n in SHARDED)
    small = tuple(weights[n] for n in REPLICATED)
    loss_local, (gx, g_shards, g_small) = jax.value_and_grad(loss_fn, argnums=(0, 1, 2))(x[0], shards, small)
    loss = lax.psum(loss_local, ("x", "y", "c"))
    grads = {**dict(zip(SHARDED, g_shards)), **dict(zip(REPLICATED, g_small))}
    steps = {n: _adamw_call(weights[n], grads[n], given["m_" + n], given["v_" + n], "adamw_" + n) for n in WEIGHTS}
    return (loss, gx[None], *[grads[n] for n in WEIGHTS], *[steps[n][0] for n in WEIGHTS],
            *[steps[n][1] for n in WEIGHTS], *[steps[n][2] for n in WEIGHTS])
```

```python
import functools
import math

import jax
import jax.numpy as jnp
from jax import lax
from jax.experimental import pallas as pl
from jax.experimental.pallas import tpu as pltpu

F32 = jnp.float32
BF16 = jnp.bfloat16

VMEM_LIMIT_BYTES = 48 * 1024 * 1024
LANES = 128
SUBLANES = 8


def _cparams(*sem):
    return pltpu.CompilerParams(dimension_semantics=sem, vmem_limit_bytes=VMEM_LIMIT_BYTES)


def _pick_tile(n, cands):
    for c in cands:
        if n % c == 0:
            return c
    return n


MM_VMEM_BUDGET = 38 * 1024 * 1024
MM_STEP_US = 0.35
HBM_BYTES_PER_US = 3.0e6
VREG_RMW_PER_US = 1.5e3


def _divisor_tiles(dim, cands):
    out = [t for t in cands if dim % t == 0]
    return out or [dim]


def _mm_tiles(m, n, k, sa, sb, so):
    tms = _divisor_tiles(m, (1024, 512, 256, 128, 64, 32, 16, 8))[:2]
    tns = _divisor_tiles(n, (1024, 512, 384, 256, 128))
    tks = [k // d for d in (1, 2, 4, 8, 13, 16, 26, 32, 52) if k % d == 0 and (k // d) % LANES == 0] or [k]
    best = None
    for tk in tks:
        nk = k // tk
        for tm in tms:
            for tn in tns:
                vmem = 2 * (tm * tk * sa + tk * tn * sb + tm * tn * so) + (tm * tn * 4 if nk > 1 else 0)
                if vmem > MM_VMEM_BUDGET:
                    continue
                steps = (m // tm) * (n // tn) * nk
                a_reads = m * k * sa * (n // tn if nk > 1 else 1)
                b_reads = k * n * sb * (m // tm if (nk > 1 or n // tn > 1) else 1)
                cost = (steps * MM_STEP_US + (a_reads + b_reads) / HBM_BYTES_PER_US
                        + (m * n * nk / 1024 / VREG_RMW_PER_US if nk > 1 else 0.0))
                if best is None or cost < best[0]:
                    best = (cost, tm, tn, tk)
    assert best is not None, (m, n, k)
    return best[1:]


def _mm_tiles_cached_t(m, n, k, sa, sb, so):
    for tm in _divisor_tiles(m, (1024, 512, 256, 128)):
        for tn in _divisor_tiles(n, (1024, 512, 384, 256, 128)):
            if 2 * (k * tm * sa + k * tn * sb + tm * tn * so) + tm * k * 2 <= MM_VMEM_BUDGET:
                return tm, tn
    return None


def _mm_tn_cached_call(a, b, tiles, out_dtype, name):
    k, m = a.shape
    n = b.shape[1]
    tm, tn = tiles

    def body(a_ref, b_ref, o_ref, at_ref):
        @pl.when(pl.program_id(1) == 0)
        def _():
            at_ref[...] = a_ref[...].astype(BF16).T

        o_ref[...] = lax.dot_general(at_ref[...], b_ref[...].astype(BF16), _NN_DIMS,
                                     preferred_element_type=F32).astype(out_dtype)

    return pl.pallas_call(
        body,
        name=name,
        grid=(m // tm, n // tn),
        in_specs=[pl.BlockSpec((k, tm), lambda i, j: (0, i)), pl.BlockSpec((k, tn), lambda i, j: (0, j))],
        out_specs=pl.BlockSpec((tm, tn), lambda i, j: (i, j)),
        out_shape=jax.ShapeDtypeStruct((m, n), out_dtype),
        scratch_shapes=[pltpu.VMEM((tm, k), BF16)],
        compiler_params=_cparams("parallel", "arbitrary"),
    )(a, b)


_NN_DIMS = (((1,), (0,)), ((), ()))


def _mm_call(a, b, ta, tb, add=None, out_dtype=F32, name="mm"):
    m, k = (a.shape[1], a.shape[0]) if ta else a.shape
    k2, n = (b.shape[1], b.shape[0]) if tb else b.shape
    assert k == k2, (a.shape, b.shape, ta, tb)
    sizes = (a.dtype.itemsize, b.dtype.itemsize, jnp.dtype(out_dtype).itemsize + (add.dtype.itemsize if add is not None else 0))
    if ta:
        tiles = _mm_tiles_cached_t(m, n, k, *sizes)
        if tiles is not None:
            return _mm_tn_cached_call(a, b, tiles, out_dtype, name)
    tm, tn, tk = _mm_tiles(m, n, k, *sizes)
    nk = k // tk
    a_spec = pl.BlockSpec((tk, tm), lambda i, j, kk: (kk, i)) if ta else pl.BlockSpec((tm, tk), lambda i, j, kk: (i, kk))
    b_spec = pl.BlockSpec((tn, tk), lambda i, j, kk: (j, kk)) if tb else pl.BlockSpec((tk, tn), lambda i, j, kk: (kk, j))
    o_spec = pl.BlockSpec((tm, tn), lambda i, j, kk: (i, j))
    dn = (((0 if ta else 1,), (1 if tb else 0,)), ((), ()))
    has_add = add is not None

    def body(*refs):
        a_ref, b_ref = refs[0], refs[1]
        c_ref = refs[2] if has_add else None
        o_ref = refs[3] if has_add else refs[2]
        p = lax.dot_general(a_ref[...].astype(BF16), b_ref[...].astype(BF16), dn, preferred_element_type=F32)

        def finish(r):
            if has_add:
                r = r + c_ref[...].astype(F32)
            o_ref[...] = r.astype(out_dtype)

        if nk == 1:
            finish(p)
        else:
            acc_ref = refs[-1]
            kk = pl.program_id(2)

            @pl.when(kk == 0)
            def _():
                acc_ref[...] = p

            @pl.when(kk > 0)
            def _():
                acc_ref[...] += p

            @pl.when(kk == nk - 1)
            def _():
                finish(acc_ref[...])

    in_specs = [a_spec, b_spec] + ([o_spec] if has_add else [])
    args = (a, b) + ((add,) if has_add else ())
    return pl.pallas_call(
        body,
        name=name,
        grid=(m // tm, n // tn, nk),
        in_specs=in_specs,
        out_specs=o_spec,
        out_shape=jax.ShapeDtypeStruct((m, n), out_dtype),
        scratch_shapes=[] if nk == 1 else [pltpu.VMEM((tm, tn), F32)],
        compiler_params=_cparams("parallel", "parallel", "arbitrary"),
    )(*args)


@jax.custom_vjp
def matmul(a, b):
    return _mm_call(a, b, False, False, name="mm_fwd")


def _matmul_fwd(a, b):
    return _mm_call(a, b, False, False, name="mm_fwd"), (a, b)


def _matmul_bwd(res, g):
    a, b = res
    da = _mm_call(g, b, False, True, out_dtype=a.dtype, name="mm_da")
    db = _mm_call(a, g, True, False, out_dtype=b.dtype, name="mm_db")
    return da, db


matmul.defvjp(_matmul_fwd, _matmul_bwd)


@jax.custom_vjp
def matmul_add(a, b, c):
    return _mm_call(a, b, False, False, add=c, name="mm_add_fwd")


def _matmul_add_fwd(a, b, c):
    return _mm_call(a, b, False, False, add=c, name="mm_add_fwd"), (a, b)


def _matmul_add_bwd(res, g):
    a, b = res
    da = _mm_call(g, b, False, True, out_dtype=a.dtype, name="mm_da")
    db = _mm_call(a, g, True, False, out_dtype=b.dtype, name="mm_db")
    return da, db, g


matmul_add.defvjp(_matmul_add_fwd, _matmul_add_bwd)


def rowwise(f, n_rows, n_aux, tile, name, passthrough=False):
    def specs(arrs, tiled):
        out = []
        for x in arrs:
            if tiled:
                out.append(pl.BlockSpec((tile, x.shape[1]), lambda i: (i, 0)))
            else:
                out.append(pl.BlockSpec(x.shape, lambda i: (0, 0)))
        return out

    def tile_structs(args):
        rows_aux, params = args[: n_rows + n_aux], args[n_rows + n_aux:]
        return [jax.ShapeDtypeStruct((tile, x.shape[1]), x.dtype) for x in rows_aux] + [
            jax.ShapeDtypeStruct(p.shape, p.dtype) for p in params]

    def fwd_call(*args):
        s = args[0].shape[0]
        outs = jax.eval_shape(f, *tile_structs(args))
        n_in = len(args)

        def body(*refs):
            vals = [r[...] for r in refs[:n_in]]
            res = f(*vals)
            for o_ref, r in zip(refs[n_in:], res):
                o_ref[...] = r.astype(o_ref.dtype)

        return pl.pallas_call(
            body,
            name=name + "_fwd",
            grid=(s // tile,),
            in_specs=specs(args[: n_rows + n_aux], True) + specs(args[n_rows + n_aux:], False),
            out_specs=[pl.BlockSpec((tile, o.shape[1]), lambda i: (i, 0)) for o in outs],
            out_shape=[jax.ShapeDtypeStruct((s, o.shape[1]), o.dtype) for o in outs],
            compiler_params=_cparams("parallel"),
        )(*args)

    def bwd_call(args, gs):
        s = args[0].shape[0]
        rows, aux, params = args[:n_rows], args[n_rows:n_rows + n_aux], args[n_rows + n_aux:]
        n_in, n_g, n_p = len(args), len(gs), len(params)
        n_gf = n_g - 1 if passthrough else n_g

        def body(*refs):
            vals = [r[...] for r in refs[:n_in]]
            gvals = tuple(r[...] for r in refs[n_in:n_in + n_gf])
            out_refs = refs[n_in + n_g:]
            auxv = vals[n_rows:n_rows + n_aux]

            def g_(*rp):
                return tuple(f(*rp[:n_rows], *auxv, *rp[n_rows:]))

            _, vjp = jax.vjp(g_, *vals[:n_rows], *vals[n_rows + n_aux:])
            cts = list(vjp(gvals))
            if passthrough:
                cts[0] = cts[0] + refs[n_in + n_gf][...]
            for o_ref, ct in zip(out_refs[:n_rows], cts[:n_rows]):
                o_ref[...] = ct.astype(o_ref.dtype)
            if n_p:
                @pl.when(pl.program_id(0) == 0)
                def _():
                    for o_ref in out_refs[n_rows:]:
                        o_ref[...] = jnp.zeros_like(o_ref)

                for o_ref, ct in zip(out_refs[n_rows:], cts[n_rows:]):
                    o_ref[...] += ct.astype(o_ref.dtype)

        return pl.pallas_call(
            body,
            name=name + "_bwd",
            grid=(s // tile,),
            in_specs=specs(rows + aux, True) + specs(params, False) + specs(gs, True),
            out_specs=specs(rows, True) + specs(params, False),
            out_shape=[jax.ShapeDtypeStruct(x.shape, x.dtype) for x in rows + params],
            compiler_params=_cparams("arbitrary" if n_p else "parallel"),
        )(*args, *gs)

    @jax.custom_vjp
    def op(*args):
        return tuple(fwd_call(*args)) + ((args[0],) if passthrough else ())

    def op_fwd(*args):
        return op(*args), args

    def op_bwd(args, gs):
        cts = bwd_call(tuple(args), tuple(gs))
        rows_ct, par_ct = cts[:n_rows], cts[n_rows:]
        aux_ct = [jnp.zeros_like(a) for a in args[n_rows:n_rows + n_aux]]
        return tuple(rows_ct) + tuple(aux_ct) + tuple(par_ct)

    op.defvjp(op_fwd, op_bwd)
    return op


SCAN_SEGMENTS = SUBLANES
SCAN_TILE_ROWS = 512
SCAN_TILE_LANES = 512


def _scan_specs(s, n):
    tr = min(SCAN_TILE_ROWS, s)
    tn = min(SCAN_TILE_LANES, n)
    return tr, tn, s // tr, n // tn


def _scan_step(ar, ai, xr, xi, br, bi):
    return ar * xr - ai * xi + br, ar * xi + ai * xr + bi


def _scan_finals(b_r, b_i, a, reverse, name):
    s, n = b_r.shape
    tr, tn, nt, nc = _scan_specs(s, n)
    ti = tr // SUBLANES
    tmap = (lambda c, j: (nt - 1 - j, c)) if reverse else (lambda c, j: (j, c))

    def body(br_ref, bi_ref, a_ref, fr_ref, fi_ref, sr, si):
        j = pl.program_id(1)

        @pl.when(j == 0)
        def _():
            sr[...] = jnp.zeros_like(sr)
            si[...] = jnp.zeros_like(si)

        ar = jnp.broadcast_to(a_ref[0:1, :], (SUBLANES, tn))
        ai = jnp.broadcast_to(a_ref[1:2, :], (SUBLANES, tn))

        def step(ii, carry):
            i = (ti - 1 - ii) if reverse else ii
            off = pl.multiple_of(i * SUBLANES, SUBLANES)
            return _scan_step(ar, ai, carry[0], carry[1], br_ref[pl.ds(off, SUBLANES), :], bi_ref[pl.ds(off, SUBLANES), :])

        xr, xi = lax.fori_loop(0, ti, step, (sr[...], si[...]), unroll=4)
        sr[...] = xr
        si[...] = xi

        @pl.when(j == nt - 1)
        def _():
            fr_ref[...] = xr
            fi_ref[...] = xi

    bspec = pl.BlockSpec((tr, tn), tmap)
    fspec = pl.BlockSpec((SUBLANES, tn), lambda c, j: (0, c))
    return pl.pallas_call(
        body,
        name=name,
        grid=(nc, nt),
        in_specs=[bspec, bspec, pl.BlockSpec((2, tn), lambda c, j: (0, c))],
        out_specs=[fspec, fspec],
        out_shape=[jax.ShapeDtypeStruct((SUBLANES, n), F32)] * 2,
        scratch_shapes=[pltpu.VMEM((SUBLANES, tn), F32)] * 2,
        compiler_params=_cparams("parallel", "arbitrary"),
    )(b_r, b_i, a)


def _scan_states(b_r, b_i, a, f_r, f_i, reverse, xs, name):
    s, n = b_r.shape
    tr, tn, nt, nc = _scan_specs(s, n)
    ti = tr // SUBLANES
    seg_len = s // SCAN_SEGMENTS
    assert seg_len & (seg_len - 1) == 0
    with_acc = xs is not None
    tmap = (lambda c, j: (nt - 1 - j, c)) if reverse else (lambda c, j: (j, c))
    order = list(range(SCAN_SEGMENTS))[::-1] if reverse else list(range(SCAN_SEGMENTS))

    def body(*refs):
        br_ref, bi_ref, a_ref, fr_ref, fi_ref = refs[:5]
        pos = 5
        if with_acc:
            xr_ref, xi_ref = refs[5:7]
            pos = 7
        or_ref, oi_ref = refs[pos:pos + 2]
        pos += 2
        if with_acc:
            dr_ref, di_ref = refs[pos:pos + 2]
            pos += 2
        sr, si = refs[pos:pos + 2]
        if with_acc:
            accr, acci = refs[pos + 2:pos + 4]
        j = pl.program_id(1)
        a_r1, a_i1 = a_ref[0:1, :], a_ref[1:2, :]

        @pl.when(j == 0)
        def _():
            pr, pi = a_r1, a_i1
            for _ in range(seg_len.bit_length() - 1):
                pr, pi = pr * pr - pi * pi, 2.0 * pr * pi
            cr = jnp.zeros((1, tn), F32)
            ci = jnp.zeros((1, tn), F32)
            for idx, k in enumerate(order):
                if idx > 0:
                    kp = order[idx - 1]
                    cr, ci = (fr_ref[kp:kp + 1, :] + pr * cr - pi * ci, fi_ref[kp:kp + 1, :] + pr * ci + pi * cr)
                sr[k:k + 1, :] = cr
                si[k:k + 1, :] = ci
            if with_acc:
                accr[...] = jnp.zeros_like(accr)
                acci[...] = jnp.zeros_like(acci)

        ar = jnp.broadcast_to(a_r1, (SUBLANES, tn))
        ai = jnp.broadcast_to(a_i1, (SUBLANES, tn))

        def step(ii, carry):
            i = (ti - 1 - ii) if reverse else ii
            off = pl.multiple_of(i * SUBLANES, SUBLANES)
            rows = pl.ds(off, SUBLANES)
            xr, xi = carry[0], carry[1]
            if with_acc:
                zr, zi = xr_ref[rows, :], xi_ref[rows, :]
                acc = (carry[2] + xr * zr + xi * zi, carry[3] + xi * zr - xr * zi)
            nr, ni = _scan_step(ar, ai, xr, xi, br_ref[rows, :], bi_ref[rows, :])
            or_ref[rows, :] = nr
            oi_ref[rows, :] = ni
            return (nr, ni) + (acc if with_acc else ())

        init = (sr[...], si[...]) + ((accr[...], acci[...]) if with_acc else ())
        out = lax.fori_loop(0, ti, step, init, unroll=4)
        sr[...] = out[0]
        si[...] = out[1]
        if with_acc:
            accr[...] = out[2]
            acci[...] = out[3]

            @pl.when(j == nt - 1)
            def _():
                dr_ref[...] = jnp.sum(out[2], axis=0, keepdims=True)
                di_ref[...] = jnp.sum(out[3], axis=0, keepdims=True)

    bspec = pl.BlockSpec((tr, tn), tmap)
    fspec = pl.BlockSpec((SUBLANES, tn), lambda c, j: (0, c))
    dspec = pl.BlockSpec((1, tn), lambda c, j: (0, c))
    in_specs = [bspec, bspec, pl.BlockSpec((2, tn), lambda c, j: (0, c)), fspec, fspec] + ([bspec, bspec] if with_acc else [])
    out_specs = [bspec, bspec] + ([dspec, dspec] if with_acc else [])
    out_shape = [jax.ShapeDtypeStruct((s, n), F32)] * 2 + ([jax.ShapeDtypeStruct((1, n), F32)] * 2 if with_acc else [])
    scratch = [pltpu.VMEM((SUBLANES, tn), F32)] * (4 if with_acc else 2)
    args = (b_r, b_i, a, f_r, f_i) + (tuple(xs) if with_acc else ())
    return pl.pallas_call(
        body,
        name=name,
        grid=(nc, nt),
        in_specs=in_specs,
        out_specs=out_specs,
        out_shape=out_shape,
        scratch_shapes=scratch,
        compiler_params=_cparams("parallel", "arbitrary"),
    )(*args)


@jax.custom_vjp
def s5_scan(b_r, b_i, a):
    f_r, f_i = _scan_finals(b_r, b_i, a, False, "s5_scan_fin")
    return tuple(_scan_states(b_r, b_i, a, f_r, f_i, False, None, "s5_scan"))


def _s5_scan_fwd(b_r, b_i, a):
    xs = s5_scan(b_r, b_i, a)
    return xs, (a, xs)


def _s5_scan_bwd(res, g):
    a, xs = res
    a_conj = a * jnp.array([[1.0], [-1.0]], F32)
    f_r, f_i = _scan_finals(g[0], g[1], a_conj, True, "s5_rscan_fin")
    g_r, g_i, da_r, da_i = _scan_states(g[0], g[1], a_conj, f_r, f_i, True, xs, "s5_rscan")
    return g_r, g_i, jnp.concatenate([da_r, da_i], axis=0)


s5_scan.defvjp(_s5_scan_fwd, _s5_scan_bwd)


BD_SEG_ROWS = 128


def _bd_view(x, seg):
    return x if seg else x.reshape(SCAN_SEGMENTS, x.shape[0] // SCAN_SEGMENTS, x.shape[1])


def _bd_spec(seg, w, rows):
    if seg:
        return pl.BlockSpec((SCAN_SEGMENTS * rows, w), lambda i, j: (i, j))
    return pl.BlockSpec((SCAN_SEGMENTS, rows, w), lambda i, j: (0, i, j))


def _bd_apply_call(a, w, transpose_w, a_seg, out_seg, add, name):
    s = a.shape[0]
    nb, ka, kn = w.shape
    wi, wo = (kn, ka) if transpose_w else (ka, kn)
    convert = a_seg != out_seg
    a_seg_k, out_seg_k = (a_seg, out_seg) if convert else (True, True)
    ov_shape = (s, nb * wo) if out_seg_k else (SCAN_SEGMENTS, s // SCAN_SEGMENTS, nb * wo)
    dn = _NT_DIMS if transpose_w else _NN_DIMS
    has_add = add is not None
    rows = min(BD_SEG_ROWS, s // SCAN_SEGMENTS)

    assert not convert or (wo if a_seg_k else wi) == LANES

    def body(*refs):
        a_ref, w_ref = refs[0], refs[1]
        c_ref = refs[2] if has_add else None
        o_ref = refs[3] if has_add else refs[2]
        wb = w_ref[0].astype(BF16)
        mm = lambda a_val: lax.dot_general(a_val.astype(BF16), wb, dn, preferred_element_type=F32)
        if not convert:
            r = mm(a_ref[...])
            o_ref[...] = r + c_ref[...] if has_add else r
        elif out_seg_k:
            scr = refs[-1]
            for k in range(SCAN_SEGMENTS):
                scr[pl.ds(k, rows, stride=SCAN_SEGMENTS), :] = a_ref[k]
            r = mm(scr[...])
            o_ref[...] = r + c_ref[...] if has_add else r
        else:
            scr = refs[-1]
            scr[...] = mm(a_ref[...])
            for k in range(SCAN_SEGMENTS):
                r = scr[pl.ds(k, rows, stride=SCAN_SEGMENTS), :]
                o_ref[k] = r + c_ref[k] if has_add else r

    args = [_bd_view(a, a_seg_k), w] + ([_bd_view(add, out_seg_k)] if has_add else [])
    in_specs = ([_bd_spec(a_seg_k, wi, rows), pl.BlockSpec((1, ka, kn), lambda i, j: (j, 0, 0))]
                + ([_bd_spec(out_seg_k, wo, rows)] if has_add else []))
    out = pl.pallas_call(
        body,
        name=name,
        grid=(s // (SCAN_SEGMENTS * rows), nb),
        in_specs=in_specs,
        out_specs=_bd_spec(out_seg_k, wo, rows),
        out_shape=jax.ShapeDtypeStruct(ov_shape, F32),
        scratch_shapes=[pltpu.VMEM((SCAN_SEGMENTS * rows, LANES), F32)] if convert else [],
        compiler_params=_cparams("parallel", "parallel"),
    )(*args)
    return out.reshape(s, nb * wo)


def _bd_weight_grad_call(a, g, a_seg, g_seg, nb, name):
    s = a.shape[0]
    ka, kn = a.shape[1] // nb, g.shape[1] // nb
    seg_len = s // SCAN_SEGMENTS
    convert = a_seg != g_seg

    def view(x, seg, w):
        if not convert:
            return x, pl.BlockSpec((s, w), lambda j: (0, j))
        if seg:
            return x, pl.BlockSpec((s, w), lambda j: (0, j))
        return x.reshape(SCAN_SEGMENTS, seg_len, x.shape[1]), pl.BlockSpec((SCAN_SEGMENTS, seg_len, w), lambda j: (0, 0, j))

    av, a_spec = view(a, a_seg, ka)
    gv, g_spec = view(g, g_seg, kn)

    assert not convert or (kn if a_seg else ka) == LANES

    def body(a_ref, g_ref, o_ref, *scratch):
        tn = lambda x, y: lax.dot_general(x.astype(BF16), y.astype(BF16), _TN_DIMS, preferred_element_type=F32)
        if not convert:
            o_ref[0] = tn(a_ref[...], g_ref[...])
        else:
            scr = scratch[0]
            t_ref = g_ref if a_seg else a_ref
            for k in range(SCAN_SEGMENTS):
                scr[pl.ds(k, seg_len, stride=SCAN_SEGMENTS), :] = t_ref[k]
            o_ref[0] = tn(a_ref[...], scr[...]) if a_seg else tn(scr[...], g_ref[...])

    return pl.pallas_call(
        body,
        name=name,
        grid=(nb,),
        in_specs=[a_spec, g_spec],
        out_specs=pl.BlockSpec((1, ka, kn), lambda j: (j, 0, 0)),
        out_shape=jax.ShapeDtypeStruct((nb, ka, kn), F32),
        scratch_shapes=[pltpu.VMEM((s, LANES), F32)] if convert else [],
        compiler_params=_cparams("parallel"),
    )(av, gv)


_NT_DIMS = (((1,), (1,)), ((), ()))
_TN_DIMS = (((0,), (0,)), ((), ()))


@functools.partial(jax.custom_vjp, nondiff_argnums=(2, 3))
def bd_matmul(a, w, a_seg, out_seg):
    return _bd_apply_call(a, w, False, a_seg, out_seg, None, "bd_mm_fwd")


def _bd_matmul_fwd(a, w, a_seg, out_seg):
    return bd_matmul(a, w, a_seg, out_seg), (a, w)


def _bd_matmul_bwd(a_seg, out_seg, res, g):
    a, w = res
    da = _bd_apply_call(g, w, True, out_seg, a_seg, None, "bd_mm_da")
    dw = _bd_weight_grad_call(a, g, a_seg, out_seg, w.shape[0], "bd_mm_dw")
    return da, dw


bd_matmul.defvjp(_bd_matmul_fwd, _bd_matmul_bwd)


@functools.partial(jax.custom_vjp, nondiff_argnums=(3, 4))
def bd_matmul_add(a, w, c, a_seg, out_seg):
    return _bd_apply_call(a, w, False, a_seg, out_seg, c, "bd_mm_add_fwd")


def _bd_matmul_add_fwd(a, w, c, a_seg, out_seg):
    return bd_matmul_add(a, w, c, a_seg, out_seg), (a, w)


def _bd_matmul_add_bwd(a_seg, out_seg, res, g):
    return _bd_matmul_bwd(a_seg, out_seg, res, g) + (g,)


bd_matmul_add.defvjp(_bd_matmul_add_fwd, _bd_matmul_add_bwd)


_NN = (((1,), (0,)), ((), ()))
_NT = (((1,), (1,)), ((), ()))
_TN = (((0,), (0,)), ((), ()))


def _dot(a, b, dn):
    return lax.dot_general(a.astype(BF16), b.astype(BF16), dn, preferred_element_type=F32)


@jax.custom_vjp
def bdot_nn(a, b):
    return _dot(a, b, _NN)


bdot_nn.defvjp(lambda a, b: (_dot(a, b, _NN), (a, b)),
               lambda r, g: (_dot(g, r[1], _NT), _dot(r[0], g, _TN)))


@jax.custom_vjp
def bdot_nt(a, b):
    return _dot(a, b, _NT)


bdot_nt.defvjp(lambda a, b: (_dot(a, b, _NT), (a, b)),
               lambda r, g: (_dot(g, r[1], _NN), _dot(g, r[0], _TN)))


@jax.custom_vjp
def bdot_tn(a, b):
    return _dot(a, b, _TN)


bdot_tn.defvjp(lambda a, b: (_dot(a, b, _TN), (a, b)),
               lambda r, g: (_dot(r[1], g, _NT), _dot(r[0], g, _NN)))


def _split3(x):
    h = x.astype(BF16)
    r = x - h.astype(F32)
    m = r.astype(BF16)
    l = (r - m.astype(F32)).astype(BF16)
    return h, m, l


def _exact_dot(t, x, dn):
    h, m, l = _split3(x)
    d = lambda p: lax.dot_general(t, p, dn, preferred_element_type=F32)
    return d(h) + d(m) + d(l)


@jax.custom_vjp
def select_dot(t, x):
    return _exact_dot(t, x, _NN)


select_dot.defvjp(lambda t, x: (_exact_dot(t, x, _NN), t),
                  lambda t, g: (jnp.zeros_like(t), _exact_dot(t, g, _TN)))


def _split_rows_impl(x, h):
    return tuple(x[i * h:(i + 1) * h] for i in range(x.shape[0] // h))


@functools.partial(jax.custom_vjp, nondiff_argnums=(1,))
def split_rows(x, h):
    return _split_rows_impl(x, h)


split_rows.defvjp(lambda x, h: (_split_rows_impl(x, h), None),
                  lambda h, r, g: (jnp.concatenate(g, axis=0),))


@jax.custom_vjp
def join_rows(parts):
    return jnp.concatenate(parts, axis=0)


def _join_rows_bwd(hs, g):
    out, off = [], 0
    for h in hs:
        out.append(g[off:off + h])
        off += h
    return (tuple(out),)


join_rows.defvjp(lambda parts: (jnp.concatenate(parts, axis=0), tuple(p.shape[0] for p in parts)), _join_rows_bwd)


def _split_lanes_impl(x, w):
    return tuple(x[:, i * w:(i + 1) * w] for i in range(x.shape[1] // w))


@functools.partial(jax.custom_vjp, nondiff_argnums=(1,))
def split_lanes(x, w):
    return _split_lanes_impl(x, w)


split_lanes.defvjp(lambda x, w: (_split_lanes_impl(x, w), None),
                   lambda w, r, g: (jnp.concatenate(g, axis=1),))


def _join_impl(parts):
    return jnp.concatenate(parts, axis=1)


@jax.custom_vjp
def join_lanes(parts):
    return _join_impl(parts)


def _join_bwd(ws, g):
    out, off = [], 0
    for w in ws:
        out.append(g[:, off:off + w])
        off += w
    return (tuple(out),)


join_lanes.defvjp(lambda parts: (_join_impl(parts), tuple(p.shape[1] for p in parts)), _join_bwd)


def _rope_impl(x, c, sa, sb, shift):
    w = x.shape[1]
    return x * c + pltpu.roll(x, w - shift, 1) * sa + pltpu.roll(x, shift, 1) * sb


@functools.partial(jax.custom_vjp, nondiff_argnums=(4,))
def rope_lanes(x, c, sa, sb, shift):
    return _rope_impl(x, c, sa, sb, shift)


def _rope_bwd(shift, r, g):
    c, sa, sb = r
    w = g.shape[1]
    dx = g * c + pltpu.roll(g * sa, shift, 1) + pltpu.roll(g * sb, w - shift, 1)
    return dx, jnp.zeros_like(c), jnp.zeros_like(sa), jnp.zeros_like(sb)


rope_lanes.defvjp(lambda x, c, sa, sb, shift: (_rope_impl(x, c, sa, sb, shift), (c, sa, sb)), _rope_bwd)


RMS_EPS = 1e-6


def _rms(x, g):
    return x * lax.rsqrt(jnp.mean(x * x, axis=-1, keepdims=True) + RMS_EPS) * g


ATTN_BLOCK = 512
MASK_VALUE = -1e30
LOG2E = math.log2(math.e)
LN2 = math.log(2.0)
V_ONES_LANE = 64


def _causal_mask(t):
    r = lax.broadcasted_iota(jnp.int32, (t, t), 0)
    c = lax.broadcasted_iota(jnp.int32, (t, t), 1)
    return c <= r


def _attn_fwd_call(q, k, v):
    s, width = q.shape
    n_heads = width // LANES
    tq = min(ATTN_BLOCK, s)
    nq = s // tq

    def body(q_ref, k_ref, v_ref, o_ref, lse_ref):
        i = pl.program_id(1)
        qb = q_ref[...].astype(BF16)
        ones_lane = lax.broadcasted_iota(jnp.int32, (tq, LANES), 1) == V_ONES_LANE

        def block(kb, carry, masked):
            m, acc = carry
            rows = pl.ds(pl.multiple_of(kb * tq, tq), tq)
            sc = lax.dot_general(qb, k_ref[rows, :].astype(BF16), _NT, preferred_element_type=F32)
            if masked:
                sc = jnp.where(_causal_mask(tq), sc, MASK_VALUE)
            m_new = jnp.maximum(m, jnp.max(sc, axis=-1, keepdims=True))
            p = jnp.exp2(sc - m_new).astype(BF16)
            vb = jnp.where(ones_lane, 1.0, v_ref[rows, :]).astype(BF16)
            acc = jnp.exp2(m - m_new) * acc + lax.dot_general(p, vb, _NN, preferred_element_type=F32)
            return m_new, acc

        init = (jnp.full((tq, 1), MASK_VALUE, F32), jnp.zeros((tq, LANES), F32))
        carry = lax.fori_loop(0, i, lambda kb, c: block(kb, c, False), init)
        m, acc = block(i, carry, True)
        l = jnp.sum(jnp.where(ones_lane, acc, 0.0), axis=-1, keepdims=True)
        o_ref[...] = jnp.where(ones_lane, 0.0, acc / l).astype(o_ref.dtype)
        lse_ref[...] = jnp.broadcast_to(m + jnp.log2(l), (tq, LANES))

    qspec = pl.BlockSpec((tq, LANES), lambda h, i: (i, h))
    kspec = pl.BlockSpec((s, LANES), lambda h, i: (0, h))
    return pl.pallas_call(
        body,
        name="mla_attn_fwd",
        grid=(n_heads, nq),
        in_specs=[qspec, kspec, kspec],
        out_specs=[qspec, qspec],
        out_shape=[jax.ShapeDtypeStruct((s, width), BF16), jax.ShapeDtypeStruct((s, width), F32)],
        compiler_params=_cparams("parallel", "parallel"),
    )(q, k, v)


def _attn_bwd_call(q, k, v, o, lse, do):
    s, width = q.shape
    n_heads = width // LANES
    tq = min(ATTN_BLOCK, s)
    nq = s // tq

    def body(q_ref, k_ref, v_ref, o_ref, lse_ref, do_ref, dq_ref, dk_ref, dv_ref, dq_acc):
        j = pl.program_id(1)

        @pl.when(j == 0)
        def _():
            dq_acc[...] = jnp.zeros_like(dq_acc)

        kb = k_ref[...].astype(BF16)
        vb = v_ref[...].astype(BF16)

        def block(i, carry, masked):
            dk, dv = carry
            rows = pl.ds(pl.multiple_of(i * tq, tq), tq)
            qi = q_ref[rows, :].astype(BF16)
            doi = do_ref[rows, :].astype(F32)
            delta = jnp.sum(doi * o_ref[rows, :].astype(F32), axis=-1, keepdims=True)
            sc = lax.dot_general(qi, kb, _NT, preferred_element_type=F32)
            if masked:
                sc = jnp.where(_causal_mask(tq), sc, MASK_VALUE)
            p = jnp.exp2(sc - lse_ref[rows, 0:1])
            dob = doi.astype(BF16)
            dv = dv + lax.dot_general(p.astype(BF16), dob, _TN, preferred_element_type=F32)
            dp = lax.dot_general(dob, vb, _NT, preferred_element_type=F32)
            ds = (p * (dp - delta)).astype(BF16)
            dq_acc[rows, :] += lax.dot_general(ds, kb, _NN, preferred_element_type=F32)
            dk = dk + lax.dot_general(ds, qi, _TN, preferred_element_type=F32)
            return dk, dv

        zero = jnp.zeros((tq, LANES), F32)
        carry = block(j, (zero, zero), True)
        dk, dv = lax.fori_loop(j + 1, nq, lambda i, c: block(i, c, False), carry)
        dk_ref[...] = (dk * LN2).astype(dk_ref.dtype)
        dv_ref[...] = dv.astype(dv_ref.dtype)

        @pl.when(j == nq - 1)
        def _():
            dq_ref[...] = (dq_acc[...] * LN2).astype(dq_ref.dtype)

    full = pl.BlockSpec((s, LANES), lambda h, j: (0, h))
    blk = pl.BlockSpec((tq, LANES), lambda h, j: (j, h))
    return pl.pallas_call(
        body,
        name="mla_attn_bwd",
        grid=(n_heads, nq),
        in_specs=[full, blk, blk, full, full, full],
        out_specs=[full, blk, blk],
        out_shape=[jax.ShapeDtypeStruct((s, width), t.dtype) for t in (q, k, v)],
        scratch_shapes=[pltpu.VMEM((s, LANES), F32)],
        compiler_params=_cparams("parallel", "arbitrary"),
    )(q, k, v, o, lse, do)


@jax.custom_vjp
def causal_attention(q, k, v):
    return _attn_fwd_call(q, k, v)[0]


def _causal_attention_fwd(q, k, v):
    o, lse = _attn_fwd_call(q, k, v)
    return o, (q, k, v, o, lse)


def _causal_attention_bwd(res, do):
    return tuple(_attn_bwd_call(*res, do))


causal_attention.defvjp(_causal_attention_fwd, _causal_attention_bwd)


HG_HEADS = 4
HG_CHUNK = 32
HG_REF_ROW = HG_CHUNK // 2 - 1
HG_TILE_ROWS = 256
HG_EXP_CLAMP = 80.0


def _hg_tile_masks(t):
    shift = HG_CHUNK.bit_length() - 1
    r = lax.broadcasted_iota(jnp.int32, (t, t), 0)
    c = lax.broadcasted_iota(jnp.int32, (t, t), 1)
    start = lax.shift_left(lax.shift_right_logical(r, shift), shift)
    causal = (c >= start) & (c <= r)
    return causal, c == start + HG_REF_ROW, c == start + (HG_CHUNK - 1)


def _hg_tile(q, fl, v, lb, st):
    t = q.shape[0]
    causal, ref_sel, last_sel = _hg_tile_masks(t)
    f = lb + (1.0 - lb) * jax.nn.sigmoid(fl)
    kk = 1.0 - f
    qs = q * jax.nn.sigmoid(q)
    b = select_dot(causal.astype(BF16), jnp.log(f))
    b_ref = select_dot(ref_sel.astype(BF16), b)
    b_last = select_dot(last_sel.astype(BF16), b)
    q_in = qs * jnp.exp(jnp.minimum(b - b_ref, HG_EXP_CLAMP))
    k_in = kk * jnp.exp(jnp.minimum(b_ref - b, HG_EXP_CLAMP))
    o = bdot_nn(jnp.where(causal, bdot_nt(q_in, k_in), 0.0), v)
    q_hat = split_rows(qs * jnp.exp(b), HG_CHUNK)
    k_hat = split_rows(kk * jnp.exp(b_last - b), HG_CHUNK)
    decay = split_rows(jnp.exp(b_last), HG_CHUNK)
    vs = split_rows(v, HG_CHUNK)
    first_row = lax.broadcasted_iota(jnp.int32, (HG_CHUNK, LANES), 0) == 0
    inter = []
    for c in range(t // HG_CHUNK):
        inter.append(bdot_nt(q_hat[c], st))
        st = st * jnp.sum(jnp.where(first_row, decay[c], 0.0), axis=0, keepdims=True) + bdot_tn(vs[c], k_hat[c])
    return o + join_rows(tuple(inter)), st


def _hg_fwd_call(q, fl, v, lb):
    s, width = q.shape
    tr = min(HG_TILE_ROWS, s)
    nt = s // tr

    def body(q_ref, f_ref, v_ref, lb_ref, o_ref, sts_ref, st_ref):
        @pl.when(pl.program_id(0) == 0)
        def _():
            st_ref[...] = jnp.zeros_like(st_ref)

        for h in range(HG_HEADS):
            ln = slice(h * LANES, (h + 1) * LANES)
            st = st_ref[h]
            sts_ref[0, h] = st
            o, st_new = _hg_tile(q_ref[:, ln], f_ref[:, ln], v_ref[:, ln], lb_ref[:, ln], st)
            o_ref[:, ln] = o
            st_ref[h] = st_new

    rspec = pl.BlockSpec((tr, width), lambda j: (j, 0))
    return pl.pallas_call(
        body,
        name="hgrn2_fwd",
        grid=(nt,),
        in_specs=[rspec, rspec, rspec, pl.BlockSpec((1, width), lambda j: (0, 0))],
        out_specs=[rspec, pl.BlockSpec((1, HG_HEADS, LANES, LANES), lambda j: (j, 0, 0, 0))],
        out_shape=[jax.ShapeDtypeStruct((s, width), F32),
                   jax.ShapeDtypeStruct((nt, HG_HEADS, LANES, LANES), F32)],
        scratch_shapes=[pltpu.VMEM((HG_HEADS, LANES, LANES), F32)],
        compiler_params=_cparams("arbitrary"),
    )(q, fl, v, lb)


def _hg_bwd_call(q, fl, v, lb, sts, do):
    s, width = q.shape
    tr = min(HG_TILE_ROWS, s)
    nt = s // tr

    def body(q_ref, f_ref, v_ref, lb_ref, sts_ref, do_ref, dq_ref, df_ref, dv_ref, dlb_ref, dst_ref):
        @pl.when(pl.program_id(0) == 0)
        def _():
            dst_ref[...] = jnp.zeros_like(dst_ref)
            dlb_ref[...] = jnp.zeros_like(dlb_ref)

        for h in range(HG_HEADS):
            ln = slice(h * LANES, (h + 1) * LANES)
            _, vjp = jax.vjp(_hg_tile, q_ref[:, ln], f_ref[:, ln], v_ref[:, ln], lb_ref[:, ln], sts_ref[0, h])
            dq, df, dv, dlb, dst = vjp((do_ref[:, ln], dst_ref[h]))
            dq_ref[:, ln] = dq
            df_ref[:, ln] = df
            dv_ref[:, ln] = dv
            dlb_ref[:, ln] += dlb
            dst_ref[h] = dst

    rspec = pl.BlockSpec((tr, width), lambda j: (nt - 1 - j, 0))
    pspec = pl.BlockSpec((1, width), lambda j: (0, 0))
    return pl.pallas_call(
        body,
        name="hgrn2_bwd",
        grid=(nt,),
        in_specs=[rspec, rspec, rspec, pspec,
                  pl.BlockSpec((1, HG_HEADS, LANES, LANES), lambda j: (nt - 1 - j, 0, 0, 0)), rspec],
        out_specs=[rspec, rspec, rspec, pspec],
        out_shape=[jax.ShapeDtypeStruct((s, width), F32)] * 3 + [jax.ShapeDtypeStruct((1, width), F32)],
        scratch_shapes=[pltpu.VMEM((HG_HEADS, LANES, LANES), F32)],
        compiler_params=_cparams("arbitrary"),
    )(q, fl, v, lb, sts, do)


@jax.custom_vjp
def hgrn2_core(q, fl, v, lb):
    return _hg_fwd_call(q, fl, v, lb)[0]


def _hgrn2_core_fwd(q, fl, v, lb):
    o, sts = _hg_fwd_call(q, fl, v, lb)
    return o, (q, fl, v, lb, sts)


def _hgrn2_core_bwd(res, do):
    return tuple(_hg_bwd_call(*res, do))


hgrn2_core.defvjp(_hgrn2_core_fwd, _hgrn2_core_bwd)


D_MODEL = 1024
DEPTH = 2
SSM_GROUPS, SSM_GROUP_CH, SSM_STATE = 32, 16, 64
SSM_WIDTH = SSM_GROUPS * SSM_GROUP_CH
MLA_HEADS, MLA_NOPE, MLA_ROPE, MLA_V = 8, 64, 32, 64
MLA_Q_RANK, MLA_KV_RANK = 512, 256
HG_WIDTH = HG_HEADS * LANES
X_HEADS, X_HEAD_DIM = 4, 128
X_WIDTH = X_HEADS * X_HEAD_DIM
D_FF = 2816
ROPE_THETA = 10000.0
IN_SPLITS = (SSM_WIDTH, MLA_Q_RANK, MLA_KV_RANK, MLA_ROPE, HG_WIDTH, HG_WIDTH, HG_WIDTH, HG_WIDTH, 3 * D_MODEL)
ROPE_LANE0 = MLA_NOPE
MLA_Q_SCALE = LOG2E / math.sqrt(MLA_NOPE + MLA_ROPE)
ROW_TILE = 256


def _t_rms(x, g):
    return (_rms(x, g).astype(BF16),)


def _t_s5_act(y, u, d):
    return (jax.nn.gelu(y + d * u).astype(BF16),)


def _t_glu(zo, zg):
    return (zo * jax.nn.sigmoid(zg),)


def _t_mla_rope(q, k, kr, c, sa, sb):
    rep = lambda t: jnp.concatenate([t] * MLA_HEADS, axis=1)
    half = MLA_ROPE // 2
    q_out = rope_lanes(q, rep(c), rep(sa), rep(sb), half) * MLA_Q_SCALE
    kr_out = rope_lanes(kr, c, sa, sb, half)
    return q_out.astype(BF16), (k + join_lanes((kr_out,) * MLA_HEADS)).astype(BF16)


def _t_hg_post(o, gate, gn):
    os_, gs = split_lanes(o, LANES), split_lanes(gate, LANES)
    return (join_lanes(tuple(_rms(a, gn) * (b * jax.nn.sigmoid(b)) for a, b in zip(os_, gs))).astype(BF16),)


def _t_merge(y_ssm, y_mla, y_hg, g0, g1, g2):
    return ((jax.nn.sigmoid(g0) * y_ssm + jax.nn.sigmoid(g1) * y_mla + jax.nn.sigmoid(g2) * y_hg).astype(BF16),)


def _t_xattn(q, k, v):
    scale = 1.0 / math.sqrt(X_HEAD_DIM)
    outs = []
    for qh, kh, vh in zip(split_lanes(q, X_HEAD_DIM), split_lanes(k, X_HEAD_DIM), split_lanes(v, X_HEAD_DIM)):
        sc = bdot_nt(qh, kh) * scale
        p = jnp.exp(sc - jnp.max(sc, axis=-1, keepdims=True))
        p = p / jnp.sum(p, axis=-1, keepdims=True)
        outs.append(bdot_nn(p, vh))
    return (join_lanes(tuple(outs)).astype(BF16),)


def _t_swiglu(gt, up):
    return ((gt * jax.nn.sigmoid(gt) * up).astype(BF16),)


def _t_loss(x, tgt, g):
    e = _rms(x, g) - tgt
    return (jnp.broadcast_to(jnp.mean(e * e, axis=-1, keepdims=True), (x.shape[0], LANES)),)


rms_op = rowwise(_t_rms, 1, 0, ROW_TILE, "rmsnorm")
rms_res_op = rowwise(_t_rms, 1, 0, ROW_TILE, "rmsnorm_res", passthrough=True)
s5_act_op = rowwise(_t_s5_act, 2, 0, ROW_TILE, "s5_act")
glu_op = rowwise(_t_glu, 2, 0, ROW_TILE, "glu")
mla_rope_op = rowwise(_t_mla_rope, 3, 3, ROW_TILE, "mla_rope")
hg_post_op = rowwise(_t_hg_post, 2, 0, ROW_TILE, "hg_post")
merge_op = rowwise(_t_merge, 6, 0, ROW_TILE, "merge")
xattn_op = rowwise(_t_xattn, 1, 0, ROW_TILE, "xattn")
swiglu_op = rowwise(_t_swiglu, 2, 0, ROW_TILE, "swiglu")
loss_op = rowwise(_t_loss, 1, 1, ROW_TILE, "loss")


def _rope_tables(positions):
    half = MLA_ROPE // 2
    inv_freq = ROPE_THETA ** (-jnp.arange(half, dtype=F32) / half)
    ang = positions.astype(F32)[:, None] * inv_freq
    cos, sin = jnp.cos(ang), jnp.sin(ang)
    s = positions.shape[0]
    z = lambda w: jnp.zeros((s, w), F32)
    tail = LANES - ROPE_LANE0 - MLA_ROPE
    c = jnp.concatenate([jnp.ones((s, ROPE_LANE0), F32), cos, cos, z(tail)], axis=1)
    sa = jnp.concatenate([z(ROPE_LANE0), -sin, z(half), z(tail)], axis=1)
    sb = jnp.concatenate([z(ROPE_LANE0), z(half), sin, z(tail)], axis=1)
    return c, sa, sb


def _s5_operators(lam_re, lam_im, b_re, b_im, c_re, c_im, log_step):
    g, p, h = SSM_GROUPS, SSM_STATE, SSM_GROUP_CH
    lam = lax.complex(lam_re, lam_im)
    lam_bar = jnp.exp(lam * jnp.exp(log_step)[:, None])
    b_bar = ((lam_bar - 1.0) / lam)[..., None] * lax.complex(b_re, b_im)
    per = LANES // h
    nb = g // per
    eye = jnp.eye(per, dtype=F32)
    bd = lambda t: jnp.einsum("jgph,gk->jghkp", t.reshape(nb, per, p, h), eye).reshape(nb, per * h, per * p)
    cd = lambda t: jnp.einsum("jghp,gk->jgpkh", t.reshape(nb, per, h, p), eye).reshape(nb, per * p, per * h)
    a = jnp.stack([jnp.real(lam_bar).reshape(-1), jnp.imag(lam_bar).reshape(-1)])
    return a, bd(jnp.real(b_bar)), bd(jnp.imag(b_bar)), cd(c_re), cd(-c_im)


IN_PAD = 6656
_SEG = {}
_off = 0
for _name, _w in (("u", 512), ("q_lat", 512), ("kv_lat", 256), ("k_rope", 128), ("hg_q", 512), ("hg_f", 512),
                  ("hg_i", 512), ("hg_g", 512), ("g0", 1024), ("g1", 1024), ("g2", 1024)):
    _SEG[_name] = (_off, _off + _w)
    _off += _w


def _layer_matrices(w, l):
    w_in = w["w_in"][l]
    d, dt = w_in.shape[0], w_in.dtype
    z = lambda n: jnp.zeros((d, n), dt)
    r0 = SSM_WIDTH + MLA_Q_RANK + MLA_KV_RANK
    w_proj = jnp.concatenate([w_in[:, :r0], z(ROPE_LANE0), w_in[:, r0:r0 + MLA_ROPE], z(LANES - ROPE_LANE0 - MLA_ROPE),
                              w_in[:, r0 + MLA_ROPE:], z(IN_PAD - _off)], axis=1)
    pad_heads = lambda t: jnp.pad(t, ((0, 0), (0, 0), (0, LANES - t.shape[2]))).reshape(t.shape[0], -1)
    uq = w["mla_w_uq"][l].reshape(MLA_Q_RANK, MLA_HEADS, MLA_NOPE + MLA_ROPE)
    ukv = w["mla_w_ukv"][l].reshape(MLA_KV_RANK, MLA_HEADS, MLA_NOPE + MLA_V)
    wo = w["mla_w_o"][l].reshape(MLA_HEADS, MLA_V, D_MODEL)
    glu, xkv, gu = w["ssm_w_glu"][l], w["x_w_kv"][l], w["ffn_w_gate_up"][l]
    return dict(
        w_proj=w_proj, glu_o=glu[:, :D_MODEL], glu_g=glu[:, D_MODEL:],
        uq=pad_heads(uq), uk=pad_heads(ukv[:, :, :MLA_NOPE]), uv=pad_heads(ukv[:, :, MLA_NOPE:]),
        mla_o=jnp.pad(wo, ((0, 0), (0, LANES - MLA_V), (0, 0))).reshape(MLA_HEADS * LANES, D_MODEL),
        hg_o=w["hg_w_o"][l], w_out=w["w_out"][l], x_q=w["x_w_q"][l], x_k=xkv[:, :X_WIDTH], x_v=xkv[:, X_WIDTH:],
        x_o=w["x_w_o"][l], ffn_g=gu[:, :D_FF], ffn_u=gu[:, D_FF:], ffn_d=w["ffn_w_down"][l])


def _layer(x, mem, tabs, m, sp, l, lower_bound):
    row = lambda name: sp[name][l].reshape(1, -1)
    h, x = rms_res_op(x, row("norm_mix"))
    proj = matmul(h, m["w_proj"])
    seg = lambda name: proj[:, _SEG[name][0]:_SEG[name][1]]
    a, bd_r, bd_i, cd_r, cd_i = _s5_operators(*(sp[n][l] for n in (
        "ssm_lam_re", "ssm_lam_im", "ssm_b_re", "ssm_b_im", "ssm_c_re", "ssm_c_im", "ssm_log_step")))
    u = seg("u")
    x_r, x_i = s5_scan(bd_matmul(u, bd_r, False, True), bd_matmul(u, bd_i, False, True), a)
    y = bd_matmul_add(x_i, cd_i, bd_matmul(x_r, cd_r, True, False), True, False)
    (ya,) = s5_act_op(y, u, row("ssm_d"))
    (y_ssm,) = glu_op(matmul(ya, m["glu_o"]), matmul(ya, m["glu_g"]))
    (qn,) = rms_op(seg("q_lat"), row("mla_q_norm"))
    (kvn,) = rms_op(seg("kv_lat"), row("mla_kv_norm"))
    q, k = mla_rope_op(matmul(qn, m["uq"]), matmul(kvn, m["uk"]), seg("k_rope"), *tabs)
    o = causal_attention(q, k, matmul(kvn, m["uv"]))
    y_mla = matmul(o, m["mla_o"])
    o = hgrn2_core(seg("hg_q"), seg("hg_f"), seg("hg_i"), lower_bound)
    (og,) = hg_post_op(o, seg("hg_g"), row("hg_g_norm"))
    y_hg = matmul(og, m["hg_o"])
    (merged,) = merge_op(y_ssm, y_mla, y_hg, seg("g0"), seg("g1"), seg("g2"))
    x = matmul_add(merged, m["w_out"], x)
    hc, x = rms_res_op(x, row("norm_cross"))
    (mn,) = rms_op(mem, row("norm_mem"))
    (ox,) = xattn_op(matmul(hc, m["x_q"]), matmul(mn, m["x_k"]), matmul(mn, m["x_v"]))
    x = matmul_add(ox, m["x_o"], x)
    hf, x = rms_res_op(x, row("norm_ffn"))
    (act,) = swiglu_op(matmul(hf, m["ffn_g"]), matmul(hf, m["ffn_u"]))
    return matmul_add(act, m["ffn_d"], x)


def _local_loss(x, mem, positions, target, w, sp):
    tabs = _rope_tables(positions)
    lb_p = jax.nn.softmax(sp["hg_lb"], axis=0)
    lower = jnp.cumsum(lb_p, axis=0) - lb_p[0:1]
    for l in range(DEPTH):
        x = _layer(x, mem, tabs, _layer_matrices(w, l), sp, l, lower[l].reshape(1, -1))
    (row_loss,) = loss_op(x, target, sp["norm_final"].reshape(1, -1))
    return 0.5 * jnp.sum(row_loss[:, 0])


N_DEV = 8
N_CHIPS = 4
COMM_LANES = 512
MESH_ID = pl.DeviceIdType.MESH
_ANY = pl.BlockSpec(memory_space=pl.ANY)
_OTHER_CHIPS = ((1, 0), (0, 1), (1, 1))


def _place():
    return lax.axis_index("x"), lax.axis_index("y"), lax.axis_index("c")


def _all_gather_call(blocks, name):
    n = len(blocks)

    def body(*refs):
        x_refs, out_refs = refs[:n], refs[n:2 * n]
        send_sems, recv_sems, local_sems = refs[2 * n:]
        x, y, c = _place()
        me, sibling = (x, y, c), (x, y, 1 - c)
        chips = [(x ^ fx, y ^ fy) for fx, fy in _OTHER_CHIPS]

        def slot(i, px, py, pc):
            return out_refs[i].at[4 * px + 2 * py + pc]

        def copy(i, k, blk, to, src=None):
            return pltpu.make_async_remote_copy(
                src_ref=slot(i, *blk) if src is None else src, dst_ref=slot(i, *blk),
                send_sem=send_sems.at[i, k], recv_sem=recv_sems.at[i, k], device_id=to, device_id_type=MESH_ID)

        mine = [pltpu.make_async_copy(x_refs[i], slot(i, *me), local_sems.at[i]) for i in range(n)]
        first = []
        for i in range(n):
            first.append(copy(i, 0, me, sibling, src=x_refs[i]))
            first += [copy(i, 1 + j, me, (*chip, c), src=x_refs[i]) for j, chip in enumerate(chips)]
        for cp in mine + first:
            cp.start()
        passed = []
        for j, chip in enumerate(chips):
            for i in range(n):
                copy(i, 1 + j, (*chip, c), me).wait_recv()
                passed.append(copy(i, 4 + j, (*chip, c), sibling))
                passed[-1].start()
        for i in range(n):
            copy(i, 0, sibling, me).wait_recv()
            for j, chip in enumerate(chips):
                copy(i, 4 + j, (*chip, 1 - c), me).wait_recv()
        for cp in first + passed:
            cp.wait_send()
        for cp in mine:
            cp.wait()

    return pl.pallas_call(
        body,
        name=name,
        out_shape=[jax.ShapeDtypeStruct((N_DEV,) + b.shape, b.dtype) for b in blocks],
        in_specs=[_ANY] * n,
        out_specs=[_ANY] * n,
        scratch_shapes=[pltpu.SemaphoreType.DMA((n, 7)), pltpu.SemaphoreType.DMA((n, 7)), pltpu.SemaphoreType.DMA((n,))],
    )(*blocks)


def _pair_exchange_call(gs, name):
    n = len(gs)

    def body(*refs):
        g_refs, got_refs = refs[:n], refs[n:2 * n]
        send_sems, recv_sems = refs[2 * n:]
        x, y, c = _place()
        sends = [pltpu.make_async_remote_copy(
            src_ref=g_refs[i].at[2 * p + (1 - c)], dst_ref=got_refs[i].at[p],
            send_sem=send_sems.at[i, p], recv_sem=recv_sems.at[i, p], device_id=(x, y, 1 - c), device_id_type=MESH_ID)
            for i in range(n) for p in range(N_CHIPS)]
        for cp in sends:
            cp.start()
        for cp in sends:
            cp.wait_recv()
        for cp in sends:
            cp.wait_send()

    return pl.pallas_call(
        body,
        name=name,
        out_shape=[jax.ShapeDtypeStruct((N_CHIPS,) + g.shape[1:], g.dtype) for g in gs],
        in_specs=[_ANY] * n,
        out_specs=[_ANY] * n,
        scratch_shapes=[pltpu.SemaphoreType.DMA((n, N_CHIPS))] * 2,
    )(*gs)


def _chip_exchange_call(parts, name):
    n = len(parts)

    def body(*refs):
        p_refs, got_refs = refs[:n], refs[n:2 * n]
        send_sems, recv_sems = refs[2 * n:]
        x, y, c = _place()
        sends = []
        for i in range(n):
            for k, (fx, fy) in enumerate(_OTHER_CHIPS):
                px, py = x ^ fx, y ^ fy
                sends.append(pltpu.make_async_remote_copy(
                    src_ref=p_refs[i].at[2 * px + py], dst_ref=got_refs[i].at[k],
                    send_sem=send_sems.at[i, k], recv_sem=recv_sems.at[i, k], device_id=(px, py, c), device_id_type=MESH_ID))
        for cp in sends:
            cp.start()
        for cp in sends:
            cp.wait_recv()
        for cp in sends:
            cp.wait_send()

    return pl.pallas_call(
        body,
        name=name,
        out_shape=[jax.ShapeDtypeStruct((3,) + p.shape[1:], p.dtype) for p in parts],
        in_specs=[_ANY] * n,
        out_specs=[_ANY] * n,
        scratch_shapes=[pltpu.SemaphoreType.DMA((n, 3))] * 2,
    )(*parts)


def _rows_cols(shape):
    return math.prod(shape[:-1]), shape[-1]


def _pair_sum_call(g, got, c_idx, name):
    rows, cols = _rows_cols(got.shape[1:])
    tr = _pick_tile(rows, (512, 256, 128, 64, 32, 16))

    def body(c_ref, a_ref, b_ref, o_ref):
        o_ref[...] = (a_ref[...].astype(F32) + b_ref[...].astype(F32)).astype(o_ref.dtype)

    spec = pl.BlockSpec((1, tr, cols), lambda p, i, c_ref: (p, i, 0))
    out = pl.pallas_call(
        body,
        name=name,
        grid_spec=pltpu.PrefetchScalarGridSpec(
            num_scalar_prefetch=1, grid=(N_CHIPS, rows // tr),
            in_specs=[pl.BlockSpec((1, tr, cols), lambda p, i, c_ref: (2 * p + c_ref[0], i, 0)), spec],
            out_specs=spec),
        out_shape=jax.ShapeDtypeStruct((N_CHIPS, rows, cols), got.dtype),
        compiler_params=_cparams("parallel", "parallel"),
    )(c_idx, g.reshape(N_DEV, rows, cols), got.reshape(N_CHIPS, rows, cols))
    return out.reshape(got.shape)


def _chip_sum_call(part, got, chip_idx, name):
    rows, cols = _rows_cols(got.shape[1:])
    tr = _pick_tile(rows, (512, 256, 128, 64, 32, 16))

    def body(p_ref, a_ref, b_ref, o_ref):
        acc = a_ref[0].astype(F32)
        for k in range(3):
            acc = acc + b_ref[k].astype(F32)
        o_ref[...] = acc

    out = pl.pallas_call(
        body,
        name=name,
        grid_spec=pltpu.PrefetchScalarGridSpec(
            num_scalar_prefetch=1, grid=(rows // tr,),
            in_specs=[pl.BlockSpec((1, tr, cols), lambda i, p_ref: (p_ref[0], i, 0)),
                      pl.BlockSpec((3, tr, cols), lambda i, p_ref: (0, i, 0))],
            out_specs=pl.BlockSpec((tr, cols), lambda i, p_ref: (i, 0))),
        out_shape=jax.ShapeDtypeStruct((rows, cols), F32),
        compiler_params=_cparams("parallel"),
    )(chip_idx, part.reshape(N_CHIPS, rows, cols), got.reshape(3, rows, cols))
    return out.reshape(got.shape[1:])


def _reduce_scatter(gs, name):
    x, y, c = _place()
    c_idx = c.astype(jnp.int32).reshape(1)
    chip_idx = (2 * x + y).astype(jnp.int32).reshape(1)
    gots = _pair_exchange_call(gs, name + "_pair")
    parts = [_pair_sum_call(g, got, c_idx, name + "_pair_sum") for g, got in zip(gs, gots)]
    gots = _chip_exchange_call(parts, name + "_chip")
    return [_chip_sum_call(p, got, chip_idx, name + "_chip_sum") for p, got in zip(parts, gots)]


SMALL_BLOCK_ROWS = 16


def _pack_small(parts):
    flat = jnp.concatenate([p.reshape(-1) for p in parts])
    chunk = N_DEV * SMALL_BLOCK_ROWS * COMM_LANES
    flat = jnp.pad(flat, (0, (-flat.shape[0]) % chunk))
    return flat.reshape(N_DEV, -1, COMM_LANES)


def _unpack_small(buf, shapes):
    flat = buf.reshape(-1)
    out, off = [], 0
    for shp in shapes:
        n = math.prod(shp)
        out.append(flat[off:off + n].reshape(shp))
        off += n
    return out


SHARDED = dict(w_in=2, ssm_w_glu=2, mla_w_uq=2, mla_w_ukv=2, mla_w_o=2, hg_w_o=2, w_out=1, x_w_q=1, x_w_kv=1,
               x_w_o=2, ffn_w_gate_up=2, ffn_w_down=1)
REPLICATED = ("norm_mix", "ssm_lam_re", "ssm_lam_im", "ssm_b_re", "ssm_b_im", "ssm_c_re", "ssm_c_im", "ssm_d",
              "ssm_log_step", "mla_q_norm", "mla_kv_norm", "hg_lb", "hg_g_norm", "norm_cross", "norm_mem", "norm_ffn",
              "norm_final")


def _join_shards(stacked, axis):
    n, l, a, b = stacked.shape
    if axis == 1:
        return stacked.transpose(1, 0, 2, 3).reshape(l, n * a, b)
    return stacked.transpose(1, 2, 0, 3).reshape(l, a, n * b)


def _split_shards(full, axis):
    l, a, b = full.shape
    if axis == 1:
        return full.reshape(l, N_DEV, a // N_DEV, b).transpose(1, 0, 2, 3)
    return full.reshape(l, a, N_DEV, b // N_DEV).transpose(2, 0, 1, 3)


@jax.custom_vjp
def gather_weights(shards):
    got = _all_gather_call([s.astype(BF16) for s in shards], "weights_all_gather")
    return tuple(_join_shards(p, ax) for p, ax in zip(got, SHARDED.values()))


def _gather_weights_fwd(shards):
    return gather_weights(shards), None


def _gather_weights_bwd(_, cts):
    gs = [_split_shards(ct, ax) for ct, ax in zip(cts, SHARDED.values())]
    return (tuple(_reduce_scatter(gs, "grads_reduce_scatter")),)


gather_weights.defvjp(_gather_weights_fwd, _gather_weights_bwd)


@jax.custom_vjp
def sync_replicated(params):
    return params


def _sync_replicated_bwd(shapes, cts):
    (mine,) = _reduce_scatter([_pack_small(cts)], "small_reduce_scatter")
    (total,) = _all_gather_call([mine], "small_all_gather")
    return (tuple(_unpack_small(total, shapes)),)


sync_replicated.defvjp(lambda params: (params, tuple(p.shape for p in params)), _sync_replicated_bwd)


ADAM_LR, ADAM_B1, ADAM_B2, ADAM_EPS, ADAM_WD, ADAM_STEP = 0.001, 0.9, 0.999, 1e-08, 0.01, 10


def _adamw_call(w, g, m, v, name):
    shape = w.shape
    cols = shape[-1]
    rows = math.prod(shape[:-1]) if len(shape) > 1 else 1
    tr = _pick_tile(rows, (512, 256, 128, 64, 32, 16, 8))

    def body(w_ref, g_ref, m_ref, v_ref, d_ref, nm_ref, nv_ref):
        gg = g_ref[...]
        m_new = ADAM_B1 * m_ref[...] + (1.0 - ADAM_B1) * gg
        v_new = ADAM_B2 * v_ref[...] + (1.0 - ADAM_B2) * jnp.square(gg)
        m_hat = m_new / (1.0 - ADAM_B1 ** ADAM_STEP)
        v_hat = v_new / (1.0 - ADAM_B2 ** ADAM_STEP)
        d_ref[...] = -ADAM_LR * (m_hat / (jnp.sqrt(v_hat) + ADAM_EPS) + ADAM_WD * w_ref[...])
        nm_ref[...] = m_new
        nv_ref[...] = v_new

    spec = pl.BlockSpec((tr, cols), lambda i: (i, 0))
    outs = pl.pallas_call(
        body, name=name, grid=(rows // tr,), in_specs=[spec] * 4, out_specs=[spec] * 3,
        out_shape=[jax.ShapeDtypeStruct((rows, cols), F32)] * 3, compiler_params=_cparams("parallel"),
    )(*(t.reshape(rows, cols) for t in (w, g, m, v)))
    return tuple(o.reshape(shape) for o in outs)


WEIGHTS = ("norm_mix", "w_in", "ssm_lam_re", "ssm_lam_im", "ssm_b_re", "ssm_b_im", "ssm_c_re", "ssm_c_im", "ssm_d",
           "ssm_log_step", "ssm_w_glu", "mla_q_norm", "mla_kv_norm", "mla_w_uq", "mla_w_ukv", "mla_w_o", "hg_lb",
           "hg_g_norm", "hg_w_o", "w_out", "norm_cross", "norm_mem", "x_w_q", "x_w_kv", "x_w_o", "norm_ffn",
           "ffn_w_gate_up", "ffn_w_down", "norm_final")


def kernel(x, mem, positions, norm_mix, w_in, ssm_lam_re, ssm_lam_im, ssm_b_re, ssm_b_im, ssm_c_re, ssm_c_im, ssm_d, ssm_log_step, ssm_w_glu, mla_q_norm, mla_kv_norm, mla_w_uq, mla_w_ukv, mla_w_o, hg_lb, hg_g_norm, hg_w_o, w_out, norm_cross, norm_mem, x_w_q, x_w_kv, x_w_o, norm_ffn, ffn_w_gate_up, ffn_w_down, norm_final, loss_target, m_norm_mix, m_w_in, m_ssm_lam_re, m_ssm_lam_im, m_ssm_b_re, m_ssm_b_im, m_ssm_c_re, m_ssm_c_im, m_ssm_d, m_ssm_log_step, m_ssm_w_glu, m_mla_q_norm, m_mla_kv_norm, m_mla_w_uq, m_mla_w_ukv, m_mla_w_o, m_hg_lb, m_hg_g_norm, m_hg_w_o, m_w_out, m_norm_cross, m_norm_mem, m_x_w_q, m_x_w_kv, m_x_w_o, m_norm_ffn, m_ffn_w_gate_up, m_ffn_w_down, m_norm_final, v_norm_mix, v_w_in, v_ssm_lam_re, v_ssm_lam_im, v_ssm_b_re, v_ssm_b_im, v_ssm_c_re, v_ssm_c_im, v_ssm_d, v_ssm_log_step, v_ssm_w_glu, v_mla_q_norm, v_mla_kv_norm, v_mla_w_uq, v_mla_w_ukv, v_mla_w_o, v_hg_lb, v_hg_g_norm, v_hg_w_o, v_w_out, v_norm_cross, v_norm_mem, v_x_w_q, v_x_w_kv, v_x_w_o, v_norm_ffn, v_ffn_w_gate_up, v_ffn_w_down, v_norm_final):
    given = dict(locals())
    weights = {n: given[n] for n in WEIGHTS}

    def loss_fn(xs, shards, small):
        full = dict(zip(SHARDED, gather_weights(shards)))
        sp = dict(zip(REPLICATED, sync_replicated(small)))
        return _local_loss(xs, mem[0], positions[0], loss_target[0], full, sp)

    shards = tuple(weights[n] for n in SHARDED)
    small = tuple(weights[n] for n in REPLICATED)
    loss_local, (gx, g_shards, g_small) = jax.value_and_grad(loss_fn, argnums=(0, 1, 2))(x[0], shards, small)
    loss = lax.psum(loss_local, ("x", "y", "c"))
    grads = {**dict(zip(SHARDED, g_shards)), **dict(zip(REPLICATED, g_small))}
    steps = {n: _adamw_call(weights[n], grads[n], given["m_" + n], given["v_" + n], "adamw_" + n) for n in WEIGHTS}
    return (loss, gx[None], *[grads[n] for n in WEIGHTS], *[steps[n][0] for n in WEIGHTS],
            *[steps[n][1] for n in WEIGHTS], *[steps[n][2] for n in WEIGHTS])
```

```python
import functools
import math

import jax
import jax.numpy as jnp
from jax import lax
from jax.experimental import pallas as pl
from jax.experimental.pallas import tpu as pltpu

F32 = jnp.float32
BF16 = jnp.bfloat16

VMEM_LIMIT_BYTES = 48 * 1024 * 1024
LANES = 128
SUBLANES = 8


def _cparams(*sem):
    return pltpu.CompilerParams(dimension_semantics=sem, vmem_limit_bytes=VMEM_LIMIT_BYTES)


def _pick_tile(n, cands):
    for c in cands:
        if n % c == 0:
            return c
    return n


MM_VMEM_BUDGET = 38 * 1024 * 1024
MM_STEP_US = 0.35
HBM_BYTES_PER_US = 3.0e6
VREG_RMW_PER_US = 1.5e3


def _divisor_tiles(dim, cands):
    out = [t for t in cands if dim % t == 0]
    return out or [dim]


def _mm_tiles(m, n, k, sa, sb, so):
    tms = _divisor_tiles(m, (1024, 512, 256, 128, 64, 32, 16, 8))[:2]
    tns = _divisor_tiles(n, (1024, 512, 384, 256, 128))
    tks = [k // d for d in (1, 2, 4, 8, 13, 16, 26, 32, 52) if k % d == 0 and (k // d) % LANES == 0] or [k]
    best = None
    for tk in tks:
        nk = k // tk
        for tm in tms:
            for tn in tns:
                vmem = 2 * (tm * tk * sa + tk * tn * sb + tm * tn * so) + (tm * tn * 4 if nk > 1 else 0)
                if vmem > MM_VMEM_BUDGET:
                    continue
                steps = (m // tm) * (n // tn) * nk
                a_reads = m * k * sa * (n // tn if nk > 1 else 1)
                b_reads = k * n * sb * (m // tm if (nk > 1 or n // tn > 1) else 1)
                cost = (steps * MM_STEP_US + (a_reads + b_reads) / HBM_BYTES_PER_US
                        + (m * n * nk / 1024 / VREG_RMW_PER_US if nk > 1 else 0.0))
                if best is None or cost < best[0]:
                    best = (cost, tm, tn, tk)
    assert best is not None, (m, n, k)
    return best[1:]


def _mm_tiles_cached_t(m, n, k, sa, sb, so):
    for tm in _divisor_tiles(m, (1024, 512, 256, 128)):
        for tn in _divisor_tiles(n, (1024, 512, 384, 256, 128)):
            if 2 * (k * tm * sa + k * tn * sb + tm * tn * so) + tm * k * 2 <= MM_VMEM_BUDGET:
                return tm, tn
    return None


def _mm_tn_cached_call(a, b, tiles, out_dtype, name):
    k, m = a.shape
    n = b.shape[1]
    tm, tn = tiles

    def body(a_ref, b_ref, o_ref, at_ref):
        @pl.when(pl.program_id(1) == 0)
        def _():
            at_ref[...] = a_ref[...].astype(BF16).T

        o_ref[...] = lax.dot_general(at_ref[...], b_ref[...].astype(BF16), _NN_DIMS,
                                     preferred_element_type=F32).astype(out_dtype)

    return pl.pallas_call(
        body,
        name=name,
        grid=(m // tm, n // tn),
        in_specs=[pl.BlockSpec((k, tm), lambda i, j: (0, i)), pl.BlockSpec((k, tn), lambda i, j: (0, j))],
        out_specs=pl.BlockSpec((tm, tn), lambda i, j: (i, j)),
        out_shape=jax.ShapeDtypeStruct((m, n), out_dtype),
        scratch_shapes=[pltpu.VMEM((tm, k), BF16)],
        compiler_params=_cparams("parallel", "arbitrary"),
    )(a, b)


_NN_DIMS = (((1,), (0,)), ((), ()))


def _mm_call(a, b, ta, tb, add=None, out_dtype=F32, name="mm"):
    m, k = (a.shape[1], a.shape[0]) if ta else a.shape
    k2, n = (b.shape[1], b.shape[0]) if tb else b.shape
    assert k == k2, (a.shape, b.shape, ta, tb)
    sizes = (a.dtype.itemsize, b.dtype.itemsize, jnp.dtype(out_dtype).itemsize + (add.dtype.itemsize if add is not None else 0))
    if ta:
        tiles = _mm_tiles_cached_t(m, n, k, *sizes)
        if tiles is not None:
            return _mm_tn_cached_call(a, b, tiles, out_dtype, name)
    tm, tn, tk = _mm_tiles(m, n, k, *sizes)
    nk = k // tk
    a_spec = pl.BlockSpec((tk, tm), lambda i, j, kk: (kk, i)) if ta else pl.BlockSpec((tm, tk), lambda i, j, kk: (i, kk))
    b_spec = pl.BlockSpec((tn, tk), lambda i, j, kk: (j, kk)) if tb else pl.BlockSpec((tk, tn), lambda i, j, kk: (kk, j))
    o_spec = pl.BlockSpec((tm, tn), lambda i, j, kk: (i, j))
    dn = (((0 if ta else 1,), (1 if tb else 0,)), ((), ()))
    has_add = add is not None

    def body(*refs):
        a_ref, b_ref = refs[0], refs[1]
        c_ref = refs[2] if has_add else None
        o_ref = refs[3] if has_add else refs[2]
        p = lax.dot_general(a_ref[...].astype(BF16), b_ref[...].astype(BF16), dn, preferred_element_type=F32)

        def finish(r):
            if has_add:
                r = r + c_ref[...].astype(F32)
            o_ref[...] = r.astype(out_dtype)

        if nk == 1:
            finish(p)
        else:
            acc_ref = refs[-1]
            kk = pl.program_id(2)

            @pl.when(kk == 0)
            def _():
                acc_ref[...] = p

            @pl.when(kk > 0)
            def _():
                acc_ref[...] += p

            @pl.when(kk == nk - 1)
            def _():
                finish(acc_ref[...])

    in_specs = [a_spec, b_spec] + ([o_spec] if has_add else [])
    args = (a, b) + ((add,) if has_add else ())
    return pl.pallas_call(
        body,
        name=name,
        grid=(m // tm, n // tn, nk),
        in_specs=in_specs,
        out_specs=o_spec,
        out_shape=jax.ShapeDtypeStruct((m, n), out_dtype),
        scratch_shapes=[] if nk == 1 else [pltpu.VMEM((tm, tn), F32)],
        compiler_params=_cparams("parallel", "parallel", "arbitrary"),
    )(*args)


@jax.custom_vjp
def matmul(a, b):
    return _mm_call(a, b, False, False, name="mm_fwd")


def _matmul_fwd(a, b):
    return _mm_call(a, b, False, False, name="mm_fwd"), (a, b)


def _matmul_bwd(res, g):
    a, b = res
    da = _mm_call(g, b, False, True, out_dtype=a.dtype, name="mm_da")
    db = _mm_call(a, g, True, False, out_dtype=b.dtype, name="mm_db")
    return da, db


matmul.defvjp(_matmul_fwd, _matmul_bwd)


@jax.custom_vjp
def matmul_add(a, b, c):
    return _mm_call(a, b, False, False, add=c, name="mm_add_fwd")


def _matmul_add_fwd(a, b, c):
    return _mm_call(a, b, False, False, add=c, name="mm_add_fwd"), (a, b)


def _matmul_add_bwd(res, g):
    a, b = res
    da = _mm_call(g, b, False, True, out_dtype=a.dtype, name="mm_da")
    db = _mm_call(a, g, True, False, out_dtype=b.dtype, name="mm_db")
    return da, db, g


matmul_add.defvjp(_matmul_add_fwd, _matmul_add_bwd)


def rowwise(f, n_rows, n_aux, tile, name, passthrough=False):
    def specs(arrs, tiled):
        out = []
        for x in arrs:
            if tiled:
                out.append(pl.BlockSpec((tile, x.shape[1]), lambda i: (i, 0)))
            else:
                out.append(pl.BlockSpec(x.shape, lambda i: (0, 0)))
        return out

    def tile_structs(args):
        rows_aux, params = args[: n_rows + n_aux], args[n_rows + n_aux:]
        return [jax.ShapeDtypeStruct((tile, x.shape[1]), x.dtype) for x in rows_aux] + [
            jax.ShapeDtypeStruct(p.shape, p.dtype) for p in params]

    def fwd_call(*args):
        s = args[0].shape[0]
        outs = jax.eval_shape(f, *tile_structs(args))
        n_in = len(args)

        def body(*refs):
            vals = [r[...] for r in refs[:n_in]]
            res = f(*vals)
            for o_ref, r in zip(refs[n_in:], res):
                o_ref[...] = r.astype(o_ref.dtype)

        return pl.pallas_call(
            body,
            name=name + "_fwd",
            grid=(s // tile,),
            in_specs=specs(args[: n_rows + n_aux], True) + specs(args[n_rows + n_aux:], False),
            out_specs=[pl.BlockSpec((tile, o.shape[1]), lambda i: (i, 0)) for o in outs],
            out_shape=[jax.ShapeDtypeStruct((s, o.shape[1]), o.dtype) for o in outs],
            compiler_params=_cparams("parallel"),
        )(*args)

    def bwd_call(args, gs):
        s = args[0].shape[0]
        rows, aux, params = args[:n_rows], args[n_rows:n_rows + n_aux], args[n_rows + n_aux:]
        n_in, n_g, n_p = len(args), len(gs), len(params)
        n_gf = n_g - 1 if passthrough else n_g

        def body(*refs):
            vals = [r[...] for r in refs[:n_in]]
            gvals = tuple(r[...] for r in refs[n_in:n_in + n_gf])
            out_refs = refs[n_in + n_g:]
            auxv = vals[n_rows:n_rows + n_aux]

            def g_(*rp):
                return tuple(f(*rp[:n_rows], *auxv, *rp[n_rows:]))

            _, vjp = jax.vjp(g_, *vals[:n_rows], *vals[n_rows + n_aux:])
            cts = list(vjp(gvals))
            if passthrough:
                cts[0] = cts[0] + refs[n_in + n_gf][...]
            for o_ref, ct in zip(out_refs[:n_rows], cts[:n_rows]):
                o_ref[...] = ct.astype(o_ref.dtype)
            if n_p:
                @pl.when(pl.program_id(0) == 0)
                def _():
                    for o_ref in out_refs[n_rows:]:
                        o_ref[...] = jnp.zeros_like(o_ref)

                for o_ref, ct in zip(out_refs[n_rows:], cts[n_rows:]):
                    o_ref[...] += ct.astype(o_ref.dtype)

        return pl.pallas_call(
            body,
            name=name + "_bwd",
            grid=(s // tile,),
            in_specs=specs(rows + aux, True) + specs(params, False) + specs(gs, True),
            out_specs=specs(rows, True) + specs(params, False),
            out_shape=[jax.ShapeDtypeStruct(x.shape, x.dtype) for x in rows + params],
            compiler_params=_cparams("arbitrary" if n_p else "parallel"),
        )(*args, *gs)

    @jax.custom_vjp
    def op(*args):
        return tuple(fwd_call(*args)) + ((args[0],) if passthrough else ())

    def op_fwd(*args):
        return op(*args), args

    def op_bwd(args, gs):
        cts = bwd_call(tuple(args), tuple(gs))
        rows_ct, par_ct = cts[:n_rows], cts[n_rows:]
        aux_ct = [jnp.zeros_like(a) for a in args[n_rows:n_rows + n_aux]]
        return tuple(rows_ct) + tuple(aux_ct) + tuple(par_ct)

    op.defvjp(op_fwd, op_bwd)
    return op


SCAN_SEGMENTS = SUBLANES
SCAN_TILE_ROWS = 512
SCAN_TILE_LANES = 512


def _scan_specs(s, n):
    tr = min(SCAN_TILE_ROWS, s)
    tn = min(SCAN_TILE_LANES, n)
    return tr, tn, s // tr, n // tn


def _scan_step(ar, ai, xr, xi, br, bi):
    return ar * xr - ai * xi + br, ar * xi + ai * xr + bi


def _scan_finals(b_r, b_i, a, reverse, name):
    s, n = b_r.shape
    tr, tn, nt, nc = _scan_specs(s, n)
    ti = tr // SUBLANES
    tmap = (lambda c, j: (nt - 1 - j, c)) if reverse else (lambda c, j: (j, c))

    def body(br_ref, bi_ref, a_ref, fr_ref, fi_ref, sr, si):
        j = pl.program_id(1)

        @pl.when(j == 0)
        def _():
            sr[...] = jnp.zeros_like(sr)
            si[...] = jnp.zeros_like(si)

        ar = jnp.broadcast_to(a_ref[0:1, :], (SUBLANES, tn))
        ai = jnp.broadcast_to(a_ref[1:2, :], (SUBLANES, tn))

        def step(ii, carry):
            i = (ti - 1 - ii) if reverse else ii
            off = pl.multiple_of(i * SUBLANES, SUBLANES)
            return _scan_step(ar, ai, carry[0], carry[1], br_ref[pl.ds(off, SUBLANES), :], bi_ref[pl.ds(off, SUBLANES), :])

        xr, xi = lax.fori_loop(0, ti, step, (sr[...], si[...]), unroll=4)
        sr[...] = xr
        si[...] = xi

        @pl.when(j == nt - 1)
        def _():
            fr_ref[...] = xr
            fi_ref[...] = xi

    bspec = pl.BlockSpec((tr, tn), tmap)
    fspec = pl.BlockSpec((SUBLANES, tn), lambda c, j: (0, c))
    return pl.pallas_call(
        body,
        name=name,
        grid=(nc, nt),
        in_specs=[bspec, bspec, pl.BlockSpec((2, tn), lambda c, j: (0, c))],
        out_specs=[fspec, fspec],
        out_shape=[jax.ShapeDtypeStruct((SUBLANES, n), F32)] * 2,
        scratch_shapes=[pltpu.VMEM((SUBLANES, tn), F32)] * 2,
        compiler_params=_cparams("parallel", "arbitrary"),
    )(b_r, b_i, a)


def _scan_states(b_r, b_i, a, f_r, f_i, reverse, xs, name):
    s, n = b_r.shape
    tr, tn, nt, nc = _scan_specs(s, n)
    ti = tr // SUBLANES
    seg_len = s // SCAN_SEGMENTS
    assert seg_len & (seg_len - 1) == 0
    with_acc = xs is not None
    tmap = (lambda c, j: (nt - 1 - j, c)) if reverse else (lambda c, j: (j, c))
    order = list(range(SCAN_SEGMENTS))[::-1] if reverse else list(range(SCAN_SEGMENTS))

    def body(*refs):
        br_ref, bi_ref, a_ref, fr_ref, fi_ref = refs[:5]
        pos = 5
        if with_acc:
            xr_ref, xi_ref = refs[5:7]
            pos = 7
        or_ref, oi_ref = refs[pos:pos + 2]
        pos += 2
        if with_acc:
            dr_ref, di_ref = refs[pos:pos + 2]
            pos += 2
        sr, si = refs[pos:pos + 2]
        if with_acc:
            accr, acci = refs[pos + 2:pos + 4]
        j = pl.program_id(1)
        a_r1, a_i1 = a_ref[0:1, :], a_ref[1:2, :]

        @pl.when(j == 0)
        def _():
            pr, pi = a_r1, a_i1
            for _ in range(seg_len.bit_length() - 1):
                pr, pi = pr * pr - pi * pi, 2.0 * pr * pi
            cr = jnp.zeros((1, tn), F32)
            ci = jnp.zeros((1, tn), F32)
            for idx, k in enumerate(order):
                if idx > 0:
                    kp = order[idx - 1]
                    cr, ci = (fr_ref[kp:kp + 1, :] + pr * cr - pi * ci, fi_ref[kp:kp + 1, :] + pr * ci + pi * cr)
                sr[k:k + 1, :] = cr
                si[k:k + 1, :] = ci
            if with_acc:
                accr[...] = jnp.zeros_like(accr)
                acci[...] = jnp.zeros_like(acci)

        ar = jnp.broadcast_to(a_r1, (SUBLANES, tn))
        ai = jnp.broadcast_to(a_i1, (SUBLANES, tn))

        def step(ii, carry):
            i = (ti - 1 - ii) if reverse else ii
            off = pl.multiple_of(i * SUBLANES, SUBLANES)
            rows = pl.ds(off, SUBLANES)
            xr, xi = carry[0], carry[1]
            if with_acc:
                zr, zi = xr_ref[rows, :], xi_ref[rows, :]
                acc = (carry[2] + xr * zr + xi * zi, carry[3] + xi * zr - xr * zi)
            nr, ni = _scan_step(ar, ai, xr, xi, br_ref[rows, :], bi_ref[rows, :])
            or_ref[rows, :] = nr
            oi_ref[rows, :] = ni
            return (nr, ni) + (acc if with_acc else ())

        init = (sr[...], si[...]) + ((accr[...], acci[...]) if with_acc else ())
        out = lax.fori_loop(0, ti, step, init, unroll=4)
        sr[...] = out[0]
        si[...] = out[1]
        if with_acc:
            accr[...] = out[2]
            acci[...] = out[3]

            @pl.when(j == nt - 1)
            def _():
                dr_ref[...] = jnp.sum(out[2], axis=0, keepdims=True)
                di_ref[...] = jnp.sum(out[3], axis=0, keepdims=True)

    bspec = pl.BlockSpec((tr, tn), tmap)
    fspec = pl.BlockSpec((SUBLANES, tn), lambda c, j: (0, c))
    dspec = pl.BlockSpec((1, tn), lambda c, j: (0, c))
    in_specs = [bspec, bspec, pl.BlockSpec((2, tn), lambda c, j: (0, c)), fspec, fspec] + ([bspec, bspec] if with_acc else [])
    out_specs = [bspec, bspec] + ([dspec, dspec] if with_acc else [])
    out_shape = [jax.ShapeDtypeStruct((s, n), F32)] * 2 + ([jax.ShapeDtypeStruct((1, n), F32)] * 2 if with_acc else [])
    scratch = [pltpu.VMEM((SUBLANES, tn), F32)] * (4 if with_acc else 2)
    args = (b_r, b_i, a, f_r, f_i) + (tuple(xs) if with_acc else ())
    return pl.pallas_call(
        body,
        name=name,
        grid=(nc, nt),
        in_specs=in_specs,
        out_specs=out_specs,
        out_shape=out_shape,
        scratch_shapes=scratch,
        compiler_params=_cparams("parallel", "arbitrary"),
    )(*args)


@jax.custom_vjp
def s5_scan(b_r, b_i, a):
    f_r, f_i = _scan_finals(b_r, b_i, a, False, "s5_scan_fin")
    return tuple(_scan_states(b_r, b_i, a, f_r, f_i, False, None, "s5_scan"))


def _s5_scan_fwd(b_r, b_i, a):
    xs = s5_scan(b_r, b_i, a)
    return xs, (a, xs)


def _s5_scan_bwd(res, g):
    a, xs = res
    a_conj = a * jnp.array([[1.0], [-1.0]], F32)
    f_r, f_i = _scan_finals(g[0], g[1], a_conj, True, "s5_rscan_fin")
    g_r, g_i, da_r, da_i = _scan_states(g[0], g[1], a_conj, f_r, f_i, True, xs, "s5_rscan")
    return g_r, g_i, jnp.concatenate([da_r, da_i], axis=0)


s5_scan.defvjp(_s5_scan_fwd, _s5_scan_bwd)


BD_SEG_ROWS = 128


def _bd_view(x, seg):
    return x if seg else x.reshape(SCAN_SEGMENTS, x.shape[0] // SCAN_SEGMENTS, x.shape[1])


def _bd_spec(seg, w, rows):
    if seg:
        return pl.BlockSpec((SCAN_SEGMENTS * rows, w), lambda i, j: (i, j))
    return pl.BlockSpec((SCAN_SEGMENTS, rows, w), lambda i, j: (0, i, j))


def _bd_apply_call(a, w, transpose_w, a_seg, out_seg, add, name):
    s = a.shape[0]
    nb, ka, kn = w.shape
    wi, wo = (kn, ka) if transpose_w else (ka, kn)
    convert = a_seg != out_seg
    a_seg_k, out_seg_k = (a_seg, out_seg) if convert else (True, True)
    ov_shape = (s, nb * wo) if out_seg_k else (SCAN_SEGMENTS, s // SCAN_SEGMENTS, nb * wo)
    dn = _NT_DIMS if transpose_w else _NN_DIMS
    has_add = add is not None
    rows = min(BD_SEG_ROWS, s // SCAN_SEGMENTS)

    assert not convert or (wo if a_seg_k else wi) == LANES

    def body(*refs):
        a_ref, w_ref = refs[0], refs[1]
        c_ref = refs[2] if has_add else None
        o_ref = refs[3] if has_add else refs[2]
        wb = w_ref[0].astype(BF16)
        mm = lambda a_val: lax.dot_general(a_val.astype(BF16), wb, dn, preferred_element_type=F32)
        if not convert:
            r = mm(a_ref[...])
            o_ref[...] = r + c_ref[...] if has_add else r
        elif out_seg_k:
            scr = refs[-1]
            for k in range(SCAN_SEGMENTS):
                scr[pl.ds(k, rows, stride=SCAN_SEGMENTS), :] = a_ref[k]
            r = mm(scr[...])
            o_ref[...] = r + c_ref[...] if has_add else r
        else:
            scr = refs[-1]
            scr[...] = mm(a_ref[...])
            for k in range(SCAN_SEGMENTS):
                r = scr[pl.ds(k, rows, stride=SCAN_SEGMENTS), :]
                o_ref[k] = r + c_ref[k] if has_add else r

    args = [_bd_view(a, a_seg_k), w] + ([_bd_view(add, out_seg_k)] if has_add else [])
    in_specs = ([_bd_spec(a_seg_k, wi, rows), pl.BlockSpec((1, ka, kn), lambda i, j: (j, 0, 0))]
                + ([_bd_spec(out_seg_k, wo, rows)] if has_add else []))
    out = pl.pallas_call(
        body,
        name=name,
        grid=(s // (SCAN_SEGMENTS * rows), nb),
        in_specs=in_specs,
        out_specs=_bd_spec(out_seg_k, wo, rows),
        out_shape=jax.ShapeDtypeStruct(ov_shape, F32),
        scratch_shapes=[pltpu.VMEM((SCAN_SEGMENTS * rows, LANES), F32)] if convert else [],
        compiler_params=_cparams("parallel", "parallel"),
    )(*args)
    return out.reshape(s, nb * wo)


def _bd_weight_grad_call(a, g, a_seg, g_seg, nb, name):
    s = a.shape[0]
    ka, kn = a.shape[1] // nb, g.shape[1] // nb
    seg_len = s // SCAN_SEGMENTS
    convert = a_seg != g_seg

    def view(x, seg, w):
        if not convert:
            return x, pl.BlockSpec((s, w), lambda j: (0, j))
        if seg:
            return x, pl.BlockSpec((s, w), lambda j: (0, j))
        return x.reshape(SCAN_SEGMENTS, seg_len, x.shape[1]), pl.BlockSpec((SCAN_SEGMENTS, seg_len, w), lambda j: (0, 0, j))

    av, a_spec = view(a, a_seg, ka)
    gv, g_spec = view(g, g_seg, kn)

    assert not convert or (kn if a_seg else ka) == LANES

    def body(a_ref, g_ref, o_ref, *scratch):
        tn = lambda x, y: lax.dot_general(x.astype(BF16), y.astype(BF16), _TN_DIMS, preferred_element_type=F32)
        if not convert:
            o_ref[0] = tn(a_ref[...], g_ref[...])
        else:
            scr = scratch[0]
            t_ref = g_ref if a_seg else a_ref
            for k in range(SCAN_SEGMENTS):
                scr[pl.ds(k, seg_len, stride=SCAN_SEGMENTS), :] = t_ref[k]
            o_ref[0] = tn(a_ref[...], scr[...]) if a_seg else tn(scr[...], g_ref[...])

    return pl.pallas_call(
        body,
        name=name,
        grid=(nb,),
        in_specs=[a_spec, g_spec],
        out_specs=pl.BlockSpec((1, ka, kn), lambda j: (j, 0, 0)),
        out_shape=jax.ShapeDtypeStruct((nb, ka, kn), F32),
        scratch_shapes=[pltpu.VMEM((s, LANES), F32)] if convert else [],
        compiler_params=_cparams("parallel"),
    )(av, gv)


_NT_DIMS = (((1,), (1,)), ((), ()))
_TN_DIMS = (((0,), (0,)), ((), ()))


@functools.partial(jax.custom_vjp, nondiff_argnums=(2, 3))
def bd_matmul(a, w, a_seg, out_seg):
    return _bd_apply_call(a, w, False, a_seg, out_seg, None, "bd_mm_fwd")


def _bd_matmul_fwd(a, w, a_seg, out_seg):
    return bd_matmul(a, w, a_seg, out_seg), (a, w)


def _bd_matmul_bwd(a_seg, out_seg, res, g):
    a, w = res
    da = _bd_apply_call(g, w, True, out_seg, a_seg, None, "bd_mm_da")
    dw = _bd_weight_grad_call(a, g, a_seg, out_seg, w.shape[0], "bd_mm_dw")
    return da, dw


bd_matmul.defvjp(_bd_matmul_fwd, _bd_matmul_bwd)


@functools.partial(jax.custom_vjp, nondiff_argnums=(3, 4))
def bd_matmul_add(a, w, c, a_seg, out_seg):
    return _bd_apply_call(a, w, False, a_seg, out_seg, c, "bd_mm_add_fwd")


def _bd_matmul_add_fwd(a, w, c, a_seg, out_seg):
    return bd_matmul_add(a, w, c, a_seg, out_seg), (a, w)


def _bd_matmul_add_bwd(a_seg, out_seg, res, g):
    return _bd_matmul_bwd(a_seg, out_seg, res, g) + (g,)


bd_matmul_add.defvjp(_bd_matmul_add_fwd, _bd_matmul_add_bwd)


_NN = (((1,), (0,)), ((), ()))
_NT = (((1,), (1,)), ((), ()))
_TN = (((0,), (0,)), ((), ()))


def _dot(a, b, dn):
    return lax.dot_general(a.astype(BF16), b.astype(BF16), dn, preferred_element_type=F32)


@jax.custom_vjp
def bdot_nn(a, b):
    return _dot(a, b, _NN)


bdot_nn.defvjp(lambda a, b: (_dot(a, b, _NN), (a, b)),
               lambda r, g: (_dot(g, r[1], _NT), _dot(r[0], g, _TN)))


@jax.custom_vjp
def bdot_nt(a, b):
    return _dot(a, b, _NT)


bdot_nt.defvjp(lambda a, b: (_dot(a, b, _NT), (a, b)),
               lambda r, g: (_dot(g, r[1], _NN), _dot(g, r[0], _TN)))


@jax.custom_vjp
def bdot_tn(a, b):
    return _dot(a, b, _TN)


bdot_tn.defvjp(lambda a, b: (_dot(a, b, _TN), (a, b)),
               lambda r, g: (_dot(r[1], g, _NT), _dot(r[0], g, _NN)))


def _split3(x):
    h = x.astype(BF16)
    r = x - h.astype(F32)
    m = r.astype(BF16)
    l = (r - m.astype(F32)).astype(BF16)
    return h, m, l


def _exact_dot(t, x, dn):
    h, m, l = _split3(x)
    d = lambda p: lax.dot_general(t, p, dn, preferred_element_type=F32)
    return d(h) + d(m) + d(l)


@jax.custom_vjp
def select_dot(t, x):
    return _exact_dot(t, x, _NN)


select_dot.defvjp(lambda t, x: (_exact_dot(t, x, _NN), t),
                  lambda t, g: (jnp.zeros_like(t), _exact_dot(t, g, _TN)))


def _split_rows_impl(x, h):
    return tuple(x[i * h:(i + 1) * h] for i in range(x.shape[0] // h))


@functools.partial(jax.custom_vjp, nondiff_argnums=(1,))
def split_rows(x, h):
    return _split_rows_impl(x, h)


split_rows.defvjp(lambda x, h: (_split_rows_impl(x, h), None),
                  lambda h, r, g: (jnp.concatenate(g, axis=0),))


@jax.custom_vjp
def join_rows(parts):
    return jnp.concatenate(parts, axis=0)


def _join_rows_bwd(hs, g):
    out, off = [], 0
    for h in hs:
        out.append(g[off:off + h])
        off += h
    return (tuple(out),)


join_rows.defvjp(lambda parts: (jnp.concatenate(parts, axis=0), tuple(p.shape[0] for p in parts)), _join_rows_bwd)


def _split_lanes_impl(x, w):
    return tuple(x[:, i * w:(i + 1) * w] for i in range(x.shape[1] // w))


@functools.partial(jax.custom_vjp, nondiff_argnums=(1,))
def split_lanes(x, w):
    return _split_lanes_impl(x, w)


split_lanes.defvjp(lambda x, w: (_split_lanes_impl(x, w), None),
                   lambda w, r, g: (jnp.concatenate(g, axis=1),))


def _join_impl(parts):
    return jnp.concatenate(parts, axis=1)


@jax.custom_vjp
def join_lanes(parts):
    return _join_impl(parts)


def _join_bwd(ws, g):
    out, off = [], 0
    for w in ws:
        out.append(g[:, off:off + w])
        off += w
    return (tuple(out),)


join_lanes.defvjp(lambda parts: (_join_impl(parts), tuple(p.shape[1] for p in parts)), _join_bwd)


def _rope_impl(x, c, sa, sb, shift):
    w = x.shape[1]
    return x * c + pltpu.roll(x, w - shift, 1) * sa + pltpu.roll(x, shift, 1) * sb


@functools.partial(jax.custom_vjp, nondiff_argnums=(4,))
def rope_lanes(x, c, sa, sb, shift):
    return _rope_impl(x, c, sa, sb, shift)


def _rope_bwd(shift, r, g):
    c, sa, sb = r
    w = g.shape[1]
    dx = g * c + pltpu.roll(g * sa, shift, 1) + pltpu.roll(g * sb, w - shift, 1)
    return dx, jnp.zeros_like(c), jnp.zeros_like(sa), jnp.zeros_like(sb)


rope_lanes.defvjp(lambda x, c, sa, sb, shift: (_rope_impl(x, c, sa, sb, shift), (c, sa, sb)), _rope_bwd)


RMS_EPS = 1e-6


def _rms(x, g):
    return x * lax.rsqrt(jnp.mean(x * x, axis=-1, keepdims=True) + RMS_EPS) * g


ATTN_BLOCK = 512
MASK_VALUE = -1e30
LOG2E = math.log2(math.e)
LN2 = math.log(2.0)
V_ONES_LANE = 64


def _causal_mask(t):
    r = lax.broadcasted_iota(jnp.int32, (t, t), 0)
    c = lax.broadcasted_iota(jnp.int32, (t, t), 1)
    return c <= r


def _attn_fwd_call(q, k, v):
    s, width = q.shape
    n_heads = width // LANES
    tq = min(ATTN_BLOCK, s)
    nq = s // tq

    def body(q_ref, k_ref, v_ref, o_ref, lse_ref):
        i = pl.program_id(1)
        qb = q_ref[...].astype(BF16)
        ones_lane = lax.broadcasted_iota(jnp.int32, (tq, LANES), 1) == V_ONES_LANE

        def block(kb, carry, masked):
            m, acc = carry
            rows = pl.ds(pl.multiple_of(kb * tq, tq), tq)
            sc = lax.dot_general(qb, k_ref[rows, :].astype(BF16), _NT, preferred_element_type=F32)
            if masked:
                sc = jnp.where(_causal_mask(tq), sc, MASK_VALUE)
            m_new = jnp.maximum(m, jnp.max(sc, axis=-1, keepdims=True))
            p = jnp.exp2(sc - m_new).astype(BF16)
            vb = jnp.where(ones_lane, 1.0, v_ref[rows, :]).astype(BF16)
            acc = jnp.exp2(m - m_new) * acc + lax.dot_general(p, vb, _NN, preferred_element_type=F32)
            return m_new, acc

        init = (jnp.full((tq, 1), MASK_VALUE, F32), jnp.zeros((tq, LANES), F32))
        carry = lax.fori_loop(0, i, lambda kb, c: block(kb, c, False), init)
        m, acc = block(i, carry, True)
        l = jnp.sum(jnp.where(ones_lane, acc, 0.0), axis=-1, keepdims=True)
        o_ref[...] = jnp.where(ones_lane, 0.0, acc / l).astype(o_ref.dtype)
        lse_ref[...] = jnp.broadcast_to(m + jnp.log2(l), (tq, LANES))

    qspec = pl.BlockSpec((tq, LANES), lambda h, i: (i, h))
    kspec = pl.BlockSpec((s, LANES), lambda h, i: (0, h))
    return pl.pallas_call(
        body,
        name="mla_attn_fwd",
        grid=(n_heads, nq),
        in_specs=[qspec, kspec, kspec],
        out_specs=[qspec, qspec],
        out_shape=[jax.ShapeDtypeStruct((s, width), BF16), jax.ShapeDtypeStruct((s, width), F32)],
        compiler_params=_cparams("parallel", "parallel"),
    )(q, k, v)


def _attn_bwd_call(q, k, v, o, lse, do):
    s, width = q.shape
    n_heads = width // LANES
    tq = min(ATTN_BLOCK, s)
    nq = s // tq

    def body(q_ref, k_ref, v_ref, o_ref, lse_ref, do_ref, dq_ref, dk_ref, dv_ref, dq_acc):
        j = pl.program_id(1)

        @pl.when(j == 0)
        def _():
            dq_acc[...] = jnp.zeros_like(dq_acc)

        kb = k_ref[...].astype(BF16)
        vb = v_ref[...].astype(BF16)

        def block(i, carry, masked):
            dk, dv = carry
            rows = pl.ds(pl.multiple_of(i * tq, tq), tq)
            qi = q_ref[rows, :].astype(BF16)
            doi = do_ref[rows, :].astype(F32)
            delta = jnp.sum(doi * o_ref[rows, :].astype(F32), axis=-1, keepdims=True)
            sc = lax.dot_general(qi, kb, _NT, preferred_element_type=F32)
            if masked:
                sc = jnp.where(_causal_mask(tq), sc, MASK_VALUE)
            p = jnp.exp2(sc - lse_ref[rows, 0:1])
            dob = doi.astype(BF16)
            dv = dv + lax.dot_general(p.astype(BF16), dob, _TN, preferred_element_type=F32)
            dp = lax.dot_general(dob, vb, _NT, preferred_element_type=F32)
            ds = (p * (dp - delta)).astype(BF16)
            dq_acc[rows, :] += lax.dot_general(ds, kb, _NN, preferred_element_type=F32)
            dk = dk + lax.dot_general(ds, qi, _TN, preferred_element_type=F32)
            return dk, dv

        zero = jnp.zeros((tq, LANES), F32)
        carry = block(j, (zero, zero), True)
        dk, dv = lax.fori_loop(j + 1, nq, lambda i, c: block(i, c, False), carry)
        dk_ref[...] = (dk * LN2).astype(dk_ref.dtype)
        dv_ref[...] = dv.astype(dv_ref.dtype)

        @pl.when(j == nq - 1)
        def _():
            dq_ref[...] = (dq_acc[...] * LN2).astype(dq_ref.dtype)

    full = pl.BlockSpec((s, LANES), lambda h, j: (0, h))
    blk = pl.BlockSpec((tq, LANES), lambda h, j: (j, h))
    return pl.pallas_call(
        body,
        name="mla_attn_bwd",
        grid=(n_heads, nq),
        in_specs=[full, blk, blk, full, full, full],
        out_specs=[full, blk, blk],
        out_shape=[jax.ShapeDtypeStruct((s, width), t.dtype) for t in (q, k, v)],
        scratch_shapes=[pltpu.VMEM((s, LANES), F32)],
        compiler_params=_cparams("parallel", "arbitrary"),
    )(q, k, v, o, lse, do)


@jax.custom_vjp
def causal_attention(q, k, v):
    return _attn_fwd_call(q, k, v)[0]


def _causal_attention_fwd(q, k, v):
    o, lse = _attn_fwd_call(q, k, v)
    return o, (q, k, v, o, lse)


def _causal_attention_bwd(res, do):
    return tuple(_attn_bwd_call(*res, do))


causal_attention.defvjp(_causal_attention_fwd, _causal_attention_bwd)


HG_HEADS = 4
HG_CHUNK = 32
HG_REF_ROW = HG_CHUNK // 2 - 1
HG_TILE_ROWS = 256
HG_EXP_CLAMP = 80.0


def _hg_tile_masks(t):
    shift = HG_CHUNK.bit_length() - 1
    r = lax.broadcasted_iota(jnp.int32, (t, t), 0)
    c = lax.broadcasted_iota(jnp.int32, (t, t), 1)
    start = lax.shift_left(lax.shift_right_logical(r, shift), shift)
    causal = (c >= start) & (c <= r)
    return causal, c == start + HG_REF_ROW, c == start + (HG_CHUNK - 1)


def _hg_tile(q, fl, v, lb, st):
    t = q.shape[0]
    causal, ref_sel, last_sel = _hg_tile_masks(t)
    f = lb + (1.0 - lb) * jax.nn.sigmoid(fl)
    kk = 1.0 - f
    qs = q * jax.nn.sigmoid(q)
    b = select_dot(causal.astype(BF16), jnp.log(f))
    b_ref = select_dot(ref_sel.astype(BF16), b)
    b_last = select_dot(last_sel.astype(BF16), b)
    q_in = qs * jnp.exp(jnp.minimum(b - b_ref, HG_EXP_CLAMP))
    k_in = kk * jnp.exp(jnp.minimum(b_ref - b, HG_EXP_CLAMP))
    o = bdot_nn(jnp.where(causal, bdot_nt(q_in, k_in), 0.0), v)
    q_hat = split_rows(qs * jnp.exp(b), HG_CHUNK)
    k_hat = split_rows(kk * jnp.exp(b_last - b), HG_CHUNK)
    decay = split_rows(jnp.exp(b_last), HG_CHUNK)
    vs = split_rows(v, HG_CHUNK)
    first_row = lax.broadcasted_iota(jnp.int32, (HG_CHUNK, LANES), 0) == 0
    inter = []
    for c in range(t // HG_CHUNK):
        inter.append(bdot_nt(q_hat[c], st))
        st = st * jnp.sum(jnp.where(first_row, decay[c], 0.0), axis=0, keepdims=True) + bdot_tn(vs[c], k_hat[c])
    return o + join_rows(tuple(inter)), st


def _hg_head(q, fl, v, gate, lb, gn, st):
    o, st = _hg_tile(q, fl, v, lb, st)
    return _rms(o, gn) * (gate * jax.nn.sigmoid(gate)), st


HG_PARTS = 4


def _hg_part_slices(h, width):
    return [slice(p * width + h * LANES, p * width + (h + 1) * LANES) for p in range(HG_PARTS)]


def _hg_fwd_call(x, lb, gn):
    s = x.shape[0]
    width = x.shape[1] // HG_PARTS
    tr = min(HG_TILE_ROWS, s)
    nt = s // tr

    def body(x_ref, lb_ref, gn_ref, o_ref, sts_ref, st_ref):
        @pl.when(pl.program_id(0) == 0)
        def _():
            st_ref[...] = jnp.zeros_like(st_ref)

        for h in range(HG_HEADS):
            ln = slice(h * LANES, (h + 1) * LANES)
            st = st_ref[h]
            sts_ref[0, h] = st
            o, st_new = _hg_head(*(x_ref[:, sl] for sl in _hg_part_slices(h, width)), lb_ref[:, ln], gn_ref[...], st)
            o_ref[:, ln] = o.astype(o_ref.dtype)
            st_ref[h] = st_new

    const = lambda shape: pl.BlockSpec(shape, lambda j: (0, 0))
    return pl.pallas_call(
        body,
        name="hgrn2_fwd",
        grid=(nt,),
        in_specs=[pl.BlockSpec((tr, HG_PARTS * width), lambda j: (j, 0)), const((1, width)), const((1, LANES))],
        out_specs=[pl.BlockSpec((tr, width), lambda j: (j, 0)),
                   pl.BlockSpec((1, HG_HEADS, LANES, LANES), lambda j: (j, 0, 0, 0))],
        out_shape=[jax.ShapeDtypeStruct((s, width), BF16),
                   jax.ShapeDtypeStruct((nt, HG_HEADS, LANES, LANES), F32)],
        scratch_shapes=[pltpu.VMEM((HG_HEADS, LANES, LANES), F32)],
        compiler_params=_cparams("arbitrary"),
    )(x, lb, gn)


def _hg_bwd_call(x, lb, gn, sts, do):
    s = x.shape[0]
    width = x.shape[1] // HG_PARTS
    tr = min(HG_TILE_ROWS, s)
    nt = s // tr

    def body(x_ref, lb_ref, gn_ref, sts_ref, do_ref, dx_ref, dlb_ref, dgn_ref, dst_ref):
        @pl.when(pl.program_id(0) == 0)
        def _():
            dst_ref[...] = jnp.zeros_like(dst_ref)
            dlb_ref[...] = jnp.zeros_like(dlb_ref)
            dgn_ref[...] = jnp.zeros_like(dgn_ref)

        for h in range(HG_HEADS):
            ln = slice(h * LANES, (h + 1) * LANES)
            parts = _hg_part_slices(h, width)
            _, vjp = jax.vjp(_hg_head, *(x_ref[:, sl] for sl in parts), lb_ref[:, ln], gn_ref[...], sts_ref[0, h])
            cts = vjp((do_ref[:, ln].astype(F32), dst_ref[h]))
            for sl, ct in zip(parts, cts[:HG_PARTS]):
                dx_ref[:, sl] = ct
            dlb_ref[:, ln] += cts[HG_PARTS]
            dgn_ref[...] += cts[HG_PARTS + 1]
            dst_ref[h] = cts[HG_PARTS + 2]

    rev = lambda w: pl.BlockSpec((tr, w), lambda j: (nt - 1 - j, 0))
    const = lambda shape: pl.BlockSpec(shape, lambda j: (0, 0))
    return pl.pallas_call(
        body,
        name="hgrn2_bwd",
        grid=(nt,),
        in_specs=[rev(HG_PARTS * width), const((1, width)), const((1, LANES)),
                  pl.BlockSpec((1, HG_HEADS, LANES, LANES), lambda j: (nt - 1 - j, 0, 0, 0)), rev(width)],
        out_specs=[rev(HG_PARTS * width), const((1, width)), const((1, LANES))],
        out_shape=[jax.ShapeDtypeStruct(x.shape, F32), jax.ShapeDtypeStruct((1, width), F32),
                   jax.ShapeDtypeStruct((1, LANES), F32)],
        scratch_shapes=[pltpu.VMEM((HG_HEADS, LANES, LANES), F32)],
        compiler_params=_cparams("arbitrary"),
    )(x, lb, gn, sts, do)


@jax.custom_vjp
def hgrn2_core(x, lb, gn):
    return _hg_fwd_call(x, lb, gn)[0]


def _hgrn2_core_fwd(x, lb, gn):
    o, sts = _hg_fwd_call(x, lb, gn)
    return o, (x, lb, gn, sts)


def _hgrn2_core_bwd(res, do):
    return tuple(_hg_bwd_call(*res, do))


hgrn2_core.defvjp(_hgrn2_core_fwd, _hgrn2_core_bwd)


D_MODEL = 1024
DEPTH = 2
SSM_GROUPS, SSM_GROUP_CH, SSM_STATE = 32, 16, 64
SSM_WIDTH = SSM_GROUPS * SSM_GROUP_CH
MLA_HEADS, MLA_NOPE, MLA_ROPE, MLA_V = 8, 64, 32, 64
MLA_Q_RANK, MLA_KV_RANK = 512, 256
HG_WIDTH = HG_HEADS * LANES
X_HEADS, X_HEAD_DIM = 4, 128
X_WIDTH = X_HEADS * X_HEAD_DIM
D_FF = 2816
ROPE_THETA = 10000.0
IN_SPLITS = (SSM_WIDTH, MLA_Q_RANK, MLA_KV_RANK, MLA_ROPE, HG_WIDTH, HG_WIDTH, HG_WIDTH, HG_WIDTH, 3 * D_MODEL)
ROPE_LANE0 = MLA_NOPE
MLA_Q_SCALE = LOG2E / math.sqrt(MLA_NOPE + MLA_ROPE)
ROW_TILE = 256


def _t_rms(x, g):
    return (_rms(x, g).astype(BF16),)


def _t_s5_act(y, u, d):
    return (jax.nn.gelu(y + d * u).astype(BF16),)


def _t_glu(z):
    zo, zg = split_lanes(z, D_MODEL)
    return (zo * jax.nn.sigmoid(zg),)


def _t_mla_rope(q, k, kr, c, sa, sb):
    rep = lambda t: jnp.concatenate([t] * MLA_HEADS, axis=1)
    half = MLA_ROPE // 2
    q_out = rope_lanes(q, rep(c), rep(sa), rep(sb), half) * MLA_Q_SCALE
    kr_out = rope_lanes(kr, c, sa, sb, half)
    return q_out.astype(BF16), (k + join_lanes((kr_out,) * MLA_HEADS)).astype(BF16)


def _t_merge(y_ssm, y_mla, y_hg, gates):
    g0, g1, g2 = split_lanes(gates, D_MODEL)
    return ((jax.nn.sigmoid(g0) * y_ssm + jax.nn.sigmoid(g1) * y_mla + jax.nn.sigmoid(g2) * y_hg).astype(BF16),)


def _t_xattn(q, kv):
    scale = 1.0 / math.sqrt(X_HEAD_DIM)
    heads = split_lanes(kv, X_HEAD_DIM)
    outs = []
    for qh, kh, vh in zip(split_lanes(q, X_HEAD_DIM), heads[:X_HEADS], heads[X_HEADS:]):
        sc = bdot_nt(qh, kh) * scale
        p = jnp.exp(sc - jnp.max(sc, axis=-1, keepdims=True))
        p = p / jnp.sum(p, axis=-1, keepdims=True)
        outs.append(bdot_nn(p, vh))
    return (join_lanes(tuple(outs)).astype(BF16),)


def _t_swiglu(gate_up):
    gt, up = split_lanes(gate_up, D_FF)
    return ((gt * jax.nn.sigmoid(gt) * up).astype(BF16),)


def _t_loss(x, tgt, g):
    e = _rms(x, g) - tgt
    return (jnp.broadcast_to(jnp.mean(e * e, axis=-1, keepdims=True), (x.shape[0], LANES)),)


rms_op = rowwise(_t_rms, 1, 0, ROW_TILE, "rmsnorm")
rms_res_op = rowwise(_t_rms, 1, 0, ROW_TILE, "rmsnorm_res", passthrough=True)
s5_act_op = rowwise(_t_s5_act, 2, 0, ROW_TILE, "s5_act")
glu_op = rowwise(_t_glu, 1, 0, ROW_TILE, "glu")
mla_rope_op = rowwise(_t_mla_rope, 3, 3, ROW_TILE, "mla_rope")
merge_op = rowwise(_t_merge, 4, 0, ROW_TILE, "merge")
xattn_op = rowwise(_t_xattn, 1, 0, ROW_TILE, "xattn")
swiglu_op = rowwise(_t_swiglu, 1, 0, ROW_TILE, "swiglu")
loss_op = rowwise(_t_loss, 1, 1, ROW_TILE, "loss")


def _rope_tables(positions):
    half = MLA_ROPE // 2
    inv_freq = ROPE_THETA ** (-jnp.arange(half, dtype=F32) / half)
    ang = positions.astype(F32)[:, None] * inv_freq
    cos, sin = jnp.cos(ang), jnp.sin(ang)
    s = positions.shape[0]
    z = lambda w: jnp.zeros((s, w), F32)
    tail = LANES - ROPE_LANE0 - MLA_ROPE
    c = jnp.concatenate([jnp.ones((s, ROPE_LANE0), F32), cos, cos, z(tail)], axis=1)
    sa = jnp.concatenate([z(ROPE_LANE0), -sin, z(half), z(tail)], axis=1)
    sb = jnp.concatenate([z(ROPE_LANE0), z(half), sin, z(tail)], axis=1)
    return c, sa, sb


def _s5_operators(lam_re, lam_im, b_re, b_im, c_re, c_im, log_step):
    g, p, h = SSM_GROUPS, SSM_STATE, SSM_GROUP_CH
    lam = lax.complex(lam_re, lam_im)
    lam_bar = jnp.exp(lam * jnp.exp(log_step)[:, None])
    b_bar = ((lam_bar - 1.0) / lam)[..., None] * lax.complex(b_re, b_im)
    per = LANES // h
    nb = g // per
    eye = jnp.eye(per, dtype=F32)
    bd = lambda t: jnp.einsum("jgph,gk->jghkp", t.reshape(nb, per, p, h), eye).reshape(nb, per * h, per * p)
    cd = lambda t: jnp.einsum("jghp,gk->jgpkh", t.reshape(nb, per, h, p), eye).reshape(nb, per * p, per * h)
    a = jnp.stack([jnp.real(lam_bar).reshape(-1), jnp.imag(lam_bar).reshape(-1)])
    return a, bd(jnp.real(b_bar)), bd(jnp.imag(b_bar)), cd(c_re), cd(-c_im)


LATENT_WIDTH = 1536
_LATENT = {}
_off = 0
for _name, _w in (("u", SSM_WIDTH), ("q_lat", MLA_Q_RANK), ("kv_lat", MLA_KV_RANK), ("k_rope", LANES)):
    _LATENT[_name] = (_off, _off + _w)
    _off += _w


def _layer_matrices(w, l):
    w_in = w["w_in"][l]
    d, dt = w_in.shape[0], w_in.dtype
    z = lambda n: jnp.zeros((d, n), dt)
    r0 = SSM_WIDTH + MLA_Q_RANK + MLA_KV_RANK
    r1 = r0 + MLA_ROPE
    r2 = r1 + HG_PARTS * HG_WIDTH
    w_latent = jnp.concatenate([w_in[:, :r0], z(ROPE_LANE0), w_in[:, r0:r1],
                                z(LATENT_WIDTH - r0 - ROPE_LANE0 - MLA_ROPE)], axis=1)
    pad_heads = lambda t: jnp.pad(t, ((0, 0), (0, 0), (0, LANES - t.shape[2]))).reshape(t.shape[0], -1)
    uq = w["mla_w_uq"][l].reshape(MLA_Q_RANK, MLA_HEADS, MLA_NOPE + MLA_ROPE)
    ukv = w["mla_w_ukv"][l].reshape(MLA_KV_RANK, MLA_HEADS, MLA_NOPE + MLA_V)
    wo = w["mla_w_o"][l].reshape(MLA_HEADS, MLA_V, D_MODEL)
    return dict(
        w_latent=w_latent, w_hg=w_in[:, r1:r2], w_gates=w_in[:, r2:], glu=w["ssm_w_glu"][l],
        uq=pad_heads(uq), uk=pad_heads(ukv[:, :, :MLA_NOPE]), uv=pad_heads(ukv[:, :, MLA_NOPE:]),
        mla_o=jnp.pad(wo, ((0, 0), (0, LANES - MLA_V), (0, 0))).reshape(MLA_HEADS * LANES, D_MODEL),
        hg_o=w["hg_w_o"][l], w_out=w["w_out"][l], x_q=w["x_w_q"][l], x_kv=w["x_w_kv"][l],
        x_o=w["x_w_o"][l], ffn_gu=w["ffn_w_gate_up"][l], ffn_d=w["ffn_w_down"][l])


def _layer(x, mem, tabs, m, sp, l, lower_bound):
    row = lambda name: sp[name][l].reshape(1, -1)
    h, x = rms_res_op(x, row("norm_mix"))
    latent = matmul(h, m["w_latent"])
    seg = lambda name: latent[:, _LATENT[name][0]:_LATENT[name][1]]
    a, bd_r, bd_i, cd_r, cd_i = _s5_operators(*(sp[n][l] for n in (
        "ssm_lam_re", "ssm_lam_im", "ssm_b_re", "ssm_b_im", "ssm_c_re", "ssm_c_im", "ssm_log_step")))
    u = seg("u")
    x_r, x_i = s5_scan(bd_matmul(u, bd_r, False, True), bd_matmul(u, bd_i, False, True), a)
    y = bd_matmul_add(x_i, cd_i, bd_matmul(x_r, cd_r, True, False), True, False)
    (ya,) = s5_act_op(y, u, row("ssm_d"))
    (y_ssm,) = glu_op(matmul(ya, m["glu"]))
    (qn,) = rms_op(seg("q_lat"), row("mla_q_norm"))
    (kvn,) = rms_op(seg("kv_lat"), row("mla_kv_norm"))
    q, k = mla_rope_op(matmul(qn, m["uq"]), matmul(kvn, m["uk"]), seg("k_rope"), *tabs)
    o = causal_attention(q, k, matmul(kvn, m["uv"]))
    y_mla = matmul(o, m["mla_o"])
    y_hg = matmul(hgrn2_core(matmul(h, m["w_hg"]), lower_bound, row("hg_g_norm")), m["hg_o"])
    (merged,) = merge_op(y_ssm, y_mla, y_hg, matmul(h, m["w_gates"]))
    x = matmul_add(merged, m["w_out"], x)
    hc, x = rms_res_op(x, row("norm_cross"))
    (mn,) = rms_op(mem, row("norm_mem"))
    (ox,) = xattn_op(matmul(hc, m["x_q"]), matmul(mn, m["x_kv"]))
    x = matmul_add(ox, m["x_o"], x)
    hf, x = rms_res_op(x, row("norm_ffn"))
    (act,) = swiglu_op(matmul(hf, m["ffn_gu"]))
    return matmul_add(act, m["ffn_d"], x)


def _local_loss(x, mem, positions, target, w, sp):
    tabs = _rope_tables(positions)
    lb_p = jax.nn.softmax(sp["hg_lb"], axis=0)
    lower = jnp.cumsum(lb_p, axis=0) - lb_p[0:1]
    for l in range(DEPTH):
        x = _layer(x, mem, tabs, _layer_matrices(w, l), sp, l, lower[l].reshape(1, -1))
    (row_loss,) = loss_op(x, target, sp["norm_final"].reshape(1, -1))
    return 0.5 * jnp.sum(row_loss[:, 0])


N_DEV = 8
N_CHIPS = 4
COMM_LANES = 512
MESH_ID = pl.DeviceIdType.MESH
_ANY = pl.BlockSpec(memory_space=pl.ANY)
_OTHER_CHIPS = ((1, 0), (0, 1), (1, 1))


def _place():
    return lax.axis_index("x"), lax.axis_index("y"), lax.axis_index("c")


def _all_gather_call(blocks, name):
    n = len(blocks)

    def body(*refs):
        x_refs, out_refs = refs[:n], refs[n:2 * n]
        send_sems, recv_sems, local_sems = refs[2 * n:]
        x, y, c = _place()
        me, sibling = (x, y, c), (x, y, 1 - c)
        chips = [(x ^ fx, y ^ fy) for fx, fy in _OTHER_CHIPS]

        def slot(i, px, py, pc):
            return out_refs[i].at[4 * px + 2 * py + pc]

        def copy(i, k, blk, to, src=None):
            return pltpu.make_async_remote_copy(
                src_ref=slot(i, *blk) if src is None else src, dst_ref=slot(i, *blk),
                send_sem=send_sems.at[i, k], recv_sem=recv_sems.at[i, k], device_id=to, device_id_type=MESH_ID)

        mine = [pltpu.make_async_copy(x_refs[i], slot(i, *me), local_sems.at[i]) for i in range(n)]
        first = []
        for i in range(n):
            first.append(copy(i, 0, me, sibling, src=x_refs[i]))
            first += [copy(i, 1 + j, me, (*chip, c), src=x_refs[i]) for j, chip in enumerate(chips)]
        for cp in mine + first:
            cp.start()
        passed = []
        for j, chip in enumerate(chips):
            for i in range(n):
                copy(i, 1 + j, (*chip, c), me).wait_recv()
                passed.append(copy(i, 4 + j, (*chip, c), sibling))
                passed[-1].start()
        for i in range(n):
            copy(i, 0, sibling, me).wait_recv()
            for j, chip in enumerate(chips):
                copy(i, 4 + j, (*chip, 1 - c), me).wait_recv()
        for cp in first + passed:
            cp.wait_send()
        for cp in mine:
            cp.wait()

    return pl.pallas_call(
        body,
        name=name,
        out_shape=[jax.ShapeDtypeStruct((N_DEV,) + b.shape, b.dtype) for b in blocks],
        in_specs=[_ANY] * n,
        out_specs=[_ANY] * n,
        scratch_shapes=[pltpu.SemaphoreType.DMA((n, 7)), pltpu.SemaphoreType.DMA((n, 7)), pltpu.SemaphoreType.DMA((n,))],
    )(*blocks)


def _pair_exchange_call(gs, name):
    n = len(gs)

    def body(*refs):
        g_refs, got_refs = refs[:n], refs[n:2 * n]
        send_sems, recv_sems = refs[2 * n:]
        x, y, c = _place()
        sends = [pltpu.make_async_remote_copy(
            src_ref=g_refs[i].at[2 * p + (1 - c)], dst_ref=got_refs[i].at[p],
            send_sem=send_sems.at[i, p], recv_sem=recv_sems.at[i, p], device_id=(x, y, 1 - c), device_id_type=MESH_ID)
            for i in range(n) for p in range(N_CHIPS)]
        for cp in sends:
            cp.start()
        for cp in sends:
            cp.wait_recv()
        for cp in sends:
            cp.wait_send()

    return pl.pallas_call(
        body,
        name=name,
        out_shape=[jax.ShapeDtypeStruct((N_CHIPS,) + g.shape[1:], g.dtype) for g in gs],
        in_specs=[_ANY] * n,
        out_specs=[_ANY] * n,
        scratch_shapes=[pltpu.SemaphoreType.DMA((n, N_CHIPS))] * 2,
    )(*gs)


def _chip_exchange_call(parts, name):
    n = len(parts)

    def body(*refs):
        p_refs, got_refs = refs[:n], refs[n:2 * n]
        send_sems, recv_sems = refs[2 * n:]
        x, y, c = _place()
        sends = []
        for i in range(n):
            for k, (fx, fy) in enumerate(_OTHER_CHIPS):
                px, py = x ^ fx, y ^ fy
                sends.append(pltpu.make_async_remote_copy(
                    src_ref=p_refs[i].at[2 * px + py], dst_ref=got_refs[i].at[k],
                    send_sem=send_sems.at[i, k], recv_sem=recv_sems.at[i, k], device_id=(px, py, c), device_id_type=MESH_ID))
        for cp in sends:
            cp.start()
        for cp in sends:
            cp.wait_recv()
        for cp in sends:
            cp.wait_send()

    return pl.pallas_call(
        body,
        name=name,
        out_shape=[jax.ShapeDtypeStruct((3,) + p.shape[1:], p.dtype) for p in parts],
        in_specs=[_ANY] * n,
        out_specs=[_ANY] * n,
        scratch_shapes=[pltpu.SemaphoreType.DMA((n, 3))] * 2,
    )(*parts)


def _rows_cols(shape):
    return math.prod(shape[:-1]), shape[-1]


def _pair_sum_call(g, got, c_idx, name):
    rows, cols = _rows_cols(got.shape[1:])
    tr = _pick_tile(rows, (512, 256, 128, 64, 32, 16))

    def body(c_ref, a_ref, b_ref, o_ref):
        o_ref[...] = (a_ref[...].astype(F32) + b_ref[...].astype(F32)).astype(o_ref.dtype)

    spec = pl.BlockSpec((1, tr, cols), lambda p, i, c_ref: (p, i, 0))
    out = pl.pallas_call(
        body,
        name=name,
        grid_spec=pltpu.PrefetchScalarGridSpec(
            num_scalar_prefetch=1, grid=(N_CHIPS, rows // tr),
            in_specs=[pl.BlockSpec((1, tr, cols), lambda p, i, c_ref: (2 * p + c_ref[0], i, 0)), spec],
            out_specs=spec),
        out_shape=jax.ShapeDtypeStruct((N_CHIPS, rows, cols), got.dtype),
        compiler_params=_cparams("parallel", "parallel"),
    )(c_idx, g.reshape(N_DEV, rows, cols), got.reshape(N_CHIPS, rows, cols))
    return out.reshape(got.shape)


def _chip_sum_call(part, got, chip_idx, name):
    rows, cols = _rows_cols(got.shape[1:])
    tr = _pick_tile(rows, (512, 256, 128, 64, 32, 16))

    def body(p_ref, a_ref, b_ref, o_ref):
        acc = a_ref[0].astype(F32)
        for k in range(3):
            acc = acc + b_ref[k].astype(F32)
        o_ref[...] = acc

    out = pl.pallas_call(
        body,
        name=name,
        grid_spec=pltpu.PrefetchScalarGridSpec(
            num_scalar_prefetch=1, grid=(rows // tr,),
            in_specs=[pl.BlockSpec((1, tr, cols), lambda i, p_ref: (p_ref[0], i, 0)),
                      pl.BlockSpec((3, tr, cols), lambda i, p_ref: (0, i, 0))],
            out_specs=pl.BlockSpec((tr, cols), lambda i, p_ref: (i, 0))),
        out_shape=jax.ShapeDtypeStruct((rows, cols), F32),
        compiler_params=_cparams("parallel"),
    )(chip_idx, part.reshape(N_CHIPS, rows, cols), got.reshape(3, rows, cols))
    return out.reshape(got.shape[1:])


def _reduce_scatter(gs, name):
    x, y, c = _place()
    c_idx = c.astype(jnp.int32).reshape(1)
    chip_idx = (2 * x + y).astype(jnp.int32).reshape(1)
    gots = _pair_exchange_call(gs, name + "_pair")
    parts = [_pair_sum_call(g, got, c_idx, name + "_pair_sum") for g, got in zip(gs, gots)]
    gots = _chip_exchange_call(parts, name + "_chip")
    return [_chip_sum_call(p, got, chip_idx, name + "_chip_sum") for p, got in zip(parts, gots)]


SMALL_BLOCK_ROWS = 16


def _pack_small(parts):
    flat = jnp.concatenate([p.reshape(-1) for p in parts])
    chunk = N_DEV * SMALL_BLOCK_ROWS * COMM_LANES
    flat = jnp.pad(flat, (0, (-flat.shape[0]) % chunk))
    return flat.reshape(N_DEV, -1, COMM_LANES)


def _unpack_small(buf, shapes):
    flat = buf.reshape(-1)
    out, off = [], 0
    for shp in shapes:
        n = math.prod(shp)
        out.append(flat[off:off + n].reshape(shp))
        off += n
    return out


SHARDED = dict(w_in=2, ssm_w_glu=2, mla_w_uq=2, mla_w_ukv=2, mla_w_o=2, hg_w_o=2, w_out=1, x_w_q=1, x_w_kv=1,
               x_w_o=2, ffn_w_gate_up=2, ffn_w_down=1)
REPLICATED = ("norm_mix", "ssm_lam_re", "ssm_lam_im", "ssm_b_re", "ssm_b_im", "ssm_c_re", "ssm_c_im", "ssm_d",
              "ssm_log_step", "mla_q_norm", "mla_kv_norm", "hg_lb", "hg_g_norm", "norm_cross", "norm_mem", "norm_ffn",
              "norm_final")


def _join_shards(stacked, axis):
    n, l, a, b = stacked.shape
    if axis == 1:
        return stacked.transpose(1, 0, 2, 3).reshape(l, n * a, b)
    return stacked.transpose(1, 2, 0, 3).reshape(l, a, n * b)


def _split_shards(full, axis):
    l, a, b = full.shape
    if axis == 1:
        return full.reshape(l, N_DEV, a // N_DEV, b).transpose(1, 0, 2, 3)
    return full.reshape(l, a, N_DEV, b // N_DEV).transpose(2, 0, 1, 3)


@jax.custom_vjp
def gather_weights(shards):
    got = _all_gather_call([s.astype(BF16) for s in shards], "weights_all_gather")
    return tuple(_join_shards(p, ax) for p, ax in zip(got, SHARDED.values()))


def _gather_weights_fwd(shards):
    return gather_weights(shards), None


def _gather_weights_bwd(_, cts):
    gs = [_split_shards(ct, ax) for ct, ax in zip(cts, SHARDED.values())]
    return (tuple(_reduce_scatter(gs, "grads_reduce_scatter")),)


gather_weights.defvjp(_gather_weights_fwd, _gather_weights_bwd)


@jax.custom_vjp
def sync_replicated(params):
    return params


def _sync_replicated_bwd(shapes, cts):
    (mine,) = _reduce_scatter([_pack_small(cts)], "small_reduce_scatter")
    (total,) = _all_gather_call([mine], "small_all_gather")
    return (tuple(_unpack_small(total, shapes)),)


sync_replicated.defvjp(lambda params: (params, tuple(p.shape for p in params)), _sync_replicated_bwd)


ADAM_LR, ADAM_B1, ADAM_B2, ADAM_EPS, ADAM_WD, ADAM_STEP = 0.001, 0.9, 0.999, 1e-08, 0.01, 10


def _adamw_call(w, g, m, v, name):
    shape = w.shape
    cols = shape[-1]
    rows = math.prod(shape[:-1]) if len(shape) > 1 else 1
    tr = _pick_tile(rows, (512, 256, 128, 64, 32, 16, 8))

    def body(w_ref, g_ref, m_ref, v_ref, d_ref, nm_ref, nv_ref):
        gg = g_ref[...]
        m_new = ADAM_B1 * m_ref[...] + (1.0 - ADAM_B1) * gg
        v_new = ADAM_B2 * v_ref[...] + (1.0 - ADAM_B2) * jnp.square(gg)
        m_hat = m_new / (1.0 - ADAM_B1 ** ADAM_STEP)
        v_hat = v_new / (1.0 - ADAM_B2 ** ADAM_STEP)
        d_ref[...] = -ADAM_LR * (m_hat / (jnp.sqrt(v_hat) + ADAM_EPS) + ADAM_WD * w_ref[...])
        nm_ref[...] = m_new
        nv_ref[...] = v_new

    spec = pl.BlockSpec((tr, cols), lambda i: (i, 0))
    outs = pl.pallas_call(
        body, name=name, grid=(rows // tr,), in_specs=[spec] * 4, out_specs=[spec] * 3,
        out_shape=[jax.ShapeDtypeStruct((rows, cols), F32)] * 3, compiler_params=_cparams("parallel"),
    )(*(t.reshape(rows, cols) for t in (w, g, m, v)))
    return tuple(o.reshape(shape) for o in outs)


WEIGHTS = ("norm_mix", "w_in", "ssm_lam_re", "ssm_lam_im", "ssm_b_re", "ssm_b_im", "ssm_c_re", "ssm_c_im", "ssm_d",
           "ssm_log_step", "ssm_w_glu", "mla_q_norm", "mla_kv_norm", "mla_w_uq", "mla_w_ukv", "mla_w_o", "hg_lb",
           "hg_g_norm", "hg_w_o", "w_out", "norm_cross", "norm_mem", "x_w_q", "x_w_kv", "x_w_o", "norm_ffn",
           "ffn_w_gate_up", "ffn_w_down", "norm_final")


def kernel(x, mem, positions, norm_mix, w_in, ssm_lam_re, ssm_lam_im, ssm_b_re, ssm_b_im, ssm_c_re, ssm_c_im, ssm_d, ssm_log_step, ssm_w_glu, mla_q_norm, mla_kv_norm, mla_w_uq, mla_w_ukv, mla_w_o, hg_lb, hg_g_norm, hg_w_o, w_out, norm_cross, norm_mem, x_w_q, x_w_kv, x_w_o, norm_ffn, ffn_w_gate_up, ffn_w_down, norm_final, loss_target, m_norm_mix, m_w_in, m_ssm_lam_re, m_ssm_lam_im, m_ssm_b_re, m_ssm_b_im, m_ssm_c_re, m_ssm_c_im, m_ssm_d, m_ssm_log_step, m_ssm_w_glu, m_mla_q_norm, m_mla_kv_norm, m_mla_w_uq, m_mla_w_ukv, m_mla_w_o, m_hg_lb, m_hg_g_norm, m_hg_w_o, m_w_out, m_norm_cross, m_norm_mem, m_x_w_q, m_x_w_kv, m_x_w_o, m_norm_ffn, m_ffn_w_gate_up, m_ffn_w_down, m_norm_final, v_norm_mix, v_w_in, v_ssm_lam_re, v_ssm_lam_im, v_ssm_b_re, v_ssm_b_im, v_ssm_c_re, v_ssm_c_im, v_ssm_d, v_ssm_log_step, v_ssm_w_glu, v_mla_q_norm, v_mla_kv_norm, v_mla_w_uq, v_mla_w_ukv, v_mla_w_o, v_hg_lb, v_hg_g_norm, v_hg_w_o, v_w_out, v_norm_cross, v_norm_mem, v_x_w_q, v_x_w_kv, v_x_w_o, v_norm_ffn, v_ffn_w_gate_up, v_ffn_w_down, v_norm_final):
    given = dict(locals())
    weights = {n: given[n] for n in WEIGHTS}

    def loss_fn(xs, shards, small):
        full = dict(zip(SHARDED, gather_weights(shards)))
        sp = dict(zip(REPLICATED, sync_replicated(small)))
        return _local_loss(xs, mem[0], positions[0], loss_target[0], full, sp)

    shards = tuple(weights[n] for n in SHARDED)
    small = tuple(weights[n] for n in REPLICATED)
    loss_local, (gx, g_shards, g_small) = jax.value_and_grad(loss_fn, argnums=(0, 1, 2))(x[0], shards, small)
    loss = lax.psum(loss_local, ("x", "y", "c"))
    grads = {**dict(zip(SHARDED, g_shards)), **dict(zip(REPLICATED, g_small))}
    steps = {n: _adamw_call(weights[n], grads[n], given["m_" + n], given["v_" + n], "adamw_" + n) for n in WEIGHTS}
    return (loss, gx[None], *[grads[n] for n in WEIGHTS], *[steps[n][0] for n in WEIGHTS],
            *[steps[n][1] for n in WEIGHTS], *[steps[n][2] for n in WEIGHTS])
```

```python
import functools
import math

import jax
import jax.numpy as jnp
from jax import lax
from jax.experimental import pallas as pl
from jax.experimental.pallas import tpu as pltpu

F32 = jnp.float32
BF16 = jnp.bfloat16

VMEM_LIMIT_BYTES = 48 * 1024 * 1024
LANES = 128
SUBLANES = 8


def _cparams(*sem):
    return pltpu.CompilerParams(dimension_semantics=sem, vmem_limit_bytes=VMEM_LIMIT_BYTES)


def _pick_tile(n, cands):
    for c in cands:
        if n % c == 0:
            return c
    return n


MM_VMEM_BUDGET = 38 * 1024 * 1024
MM_STEP_US = 0.35
HBM_BYTES_PER_US = 3.0e6
VREG_RMW_PER_US = 1.5e3


def _divisor_tiles(dim, cands):
    out = [t for t in cands if dim % t == 0]
    return out or [dim]


def _mm_tiles(m, n, k, sa, sb, so):
    tms = _divisor_tiles(m, (1024, 512, 256, 128, 64, 32, 16, 8))[:2]
    tns = _divisor_tiles(n, (2048, 1536, 1408, 1024, 768, 512, 384, 256, 128))
    tks = [k // d for d in (1, 2, 4, 8, 13, 16, 26, 32, 52) if k % d == 0 and (k // d) % LANES == 0] or [k]
    best = None
    for tk in tks:
        nk = k // tk
        for tm in tms:
            for tn in tns:
                vmem = 2 * (tm * tk * sa + tk * tn * sb + tm * tn * so) + (tm * tn * 4 if nk > 1 else 0)
                if vmem > MM_VMEM_BUDGET:
                    continue
                steps = (m // tm) * (n // tn) * nk
                a_reads = m * k * sa * (n // tn if nk > 1 else 1)
                b_reads = k * n * sb * (m // tm if (nk > 1 or n // tn > 1) else 1)
                cost = (steps * MM_STEP_US + (a_reads + b_reads) / HBM_BYTES_PER_US
                        + (m * n * nk / 1024 / VREG_RMW_PER_US if nk > 1 else 0.0))
                if best is None or cost < best[0]:
                    best = (cost, tm, tn, tk)
    assert best is not None, (m, n, k)
    return best[1:]


def _mm_tiles_cached_t(m, n, k, sa, sb, so):
    for tm in _divisor_tiles(m, (1024, 512, 256, 128)):
        for tn in _divisor_tiles(n, (1024, 512, 384, 256, 128)):
            if 2 * (k * tm * sa + k * tn * sb + tm * tn * so) + tm * k * 2 <= MM_VMEM_BUDGET:
                return tm, tn
    return None


def _mm_tn_cached_call(a, b, tiles, out_dtype, name):
    k, m = a.shape
    n = b.shape[1]
    tm, tn = tiles

    def body(a_ref, b_ref, o_ref, at_ref):
        @pl.when(pl.program_id(1) == 0)
        def _():
            at_ref[...] = a_ref[...].astype(BF16).T

        o_ref[...] = lax.dot_general(at_ref[...], b_ref[...].astype(BF16), _NN_DIMS,
                                     preferred_element_type=F32).astype(out_dtype)

    return pl.pallas_call(
        body,
        name=name,
        grid=(m // tm, n // tn),
        in_specs=[pl.BlockSpec((k, tm), lambda i, j: (0, i)), pl.BlockSpec((k, tn), lambda i, j: (0, j))],
        out_specs=pl.BlockSpec((tm, tn), lambda i, j: (i, j)),
        out_shape=jax.ShapeDtypeStruct((m, n), out_dtype),
        scratch_shapes=[pltpu.VMEM((tm, k), BF16)],
        compiler_params=_cparams("parallel", "arbitrary"),
    )(a, b)


_NN_DIMS = (((1,), (0,)), ((), ()))


def _mm_call(a, b, ta, tb, add=None, out_dtype=F32, name="mm"):
    m, k = (a.shape[1], a.shape[0]) if ta else a.shape
    k2, n = (b.shape[1], b.shape[0]) if tb else b.shape
    assert k == k2, (a.shape, b.shape, ta, tb)
    sizes = (a.dtype.itemsize, b.dtype.itemsize, jnp.dtype(out_dtype).itemsize + (add.dtype.itemsize if add is not None else 0))
    if ta:
        tiles = _mm_tiles_cached_t(m, n, k, *sizes)
        if tiles is not None:
            return _mm_tn_cached_call(a, b, tiles, out_dtype, name)
    tm, tn, tk = _mm_tiles(m, n, k, *sizes)
    nk = k // tk
    a_spec = pl.BlockSpec((tk, tm), lambda i, j, kk: (kk, i)) if ta else pl.BlockSpec((tm, tk), lambda i, j, kk: (i, kk))
    b_spec = pl.BlockSpec((tn, tk), lambda i, j, kk: (j, kk)) if tb else pl.BlockSpec((tk, tn), lambda i, j, kk: (kk, j))
    o_spec = pl.BlockSpec((tm, tn), lambda i, j, kk: (i, j))
    dn = (((0 if ta else 1,), (1 if tb else 0,)), ((), ()))
    has_add = add is not None

    def body(*refs):
        a_ref, b_ref = refs[0], refs[1]
        c_ref = refs[2] if has_add else None
        o_ref = refs[3] if has_add else refs[2]
        p = lax.dot_general(a_ref[...].astype(BF16), b_ref[...].astype(BF16), dn, preferred_element_type=F32)

        def finish(r):
            if has_add:
                r = r + c_ref[...].astype(F32)
            o_ref[...] = r.astype(out_dtype)

        if nk == 1:
            finish(p)
        else:
            acc_ref = refs[-1]
            kk = pl.program_id(2)

            @pl.when(kk == 0)
            def _():
                acc_ref[...] = p

            @pl.when(kk > 0)
            def _():
                acc_ref[...] += p

            @pl.when(kk == nk - 1)
            def _():
                finish(acc_ref[...])

    in_specs = [a_spec, b_spec] + ([o_spec] if has_add else [])
    args = (a, b) + ((add,) if has_add else ())
    return pl.pallas_call(
        body,
        name=name,
        grid=(m // tm, n // tn, nk),
        in_specs=in_specs,
        out_specs=o_spec,
        out_shape=jax.ShapeDtypeStruct((m, n), out_dtype),
        scratch_shapes=[] if nk == 1 else [pltpu.VMEM((tm, tn), F32)],
        compiler_params=_cparams("parallel", "parallel", "arbitrary"),
    )(*args)


@functools.partial(jax.custom_vjp, nondiff_argnums=(2,))
def matmul(a, b, out_dtype=F32):
    return _mm_call(a, b, False, False, out_dtype=out_dtype, name="mm_fwd")


def _matmul_fwd(a, b, out_dtype):
    return matmul(a, b, out_dtype), (a, b)


def _matmul_bwd(out_dtype, res, g):
    a, b = res
    da = _mm_call(g, b, False, True, out_dtype=a.dtype, name="mm_da")
    db = _mm_call(a, g, True, False, out_dtype=b.dtype, name="mm_db")
    return da, db


matmul.defvjp(_matmul_fwd, _matmul_bwd)


@jax.custom_vjp
def matmul_add(a, b, c):
    return _mm_call(a, b, False, False, add=c, name="mm_add_fwd")


def _matmul_add_fwd(a, b, c):
    return _mm_call(a, b, False, False, add=c, name="mm_add_fwd"), (a, b)


def _matmul_add_bwd(res, g):
    a, b = res
    da = _mm_call(g, b, False, True, out_dtype=a.dtype, name="mm_da")
    db = _mm_call(a, g, True, False, out_dtype=b.dtype, name="mm_db")
    return da, db, g


matmul_add.defvjp(_matmul_add_fwd, _matmul_add_bwd)


def rowwise(f, n_rows, n_aux, tile, name, passthrough=False):
    def specs(arrs, tiled):
        out = []
        for x in arrs:
            if tiled:
                out.append(pl.BlockSpec((tile, x.shape[1]), lambda i: (i, 0)))
            else:
                out.append(pl.BlockSpec(x.shape, lambda i: (0, 0)))
        return out

    def tile_structs(args):
        rows_aux, params = args[: n_rows + n_aux], args[n_rows + n_aux:]
        return [jax.ShapeDtypeStruct((tile, x.shape[1]), x.dtype) for x in rows_aux] + [
            jax.ShapeDtypeStruct(p.shape, p.dtype) for p in params]

    def fwd_call(*args):
        s = args[0].shape[0]
        outs = jax.eval_shape(f, *tile_structs(args))
        n_in = len(args)

        def body(*refs):
            vals = [r[...] for r in refs[:n_in]]
            res = f(*vals)
            for o_ref, r in zip(refs[n_in:], res):
                o_ref[...] = r.astype(o_ref.dtype)

        return pl.pallas_call(
            body,
            name=name + "_fwd",
            grid=(s // tile,),
            in_specs=specs(args[: n_rows + n_aux], True) + specs(args[n_rows + n_aux:], False),
            out_specs=[pl.BlockSpec((tile, o.shape[1]), lambda i: (i, 0)) for o in outs],
            out_shape=[jax.ShapeDtypeStruct((s, o.shape[1]), o.dtype) for o in outs],
            compiler_params=_cparams("parallel"),
        )(*args)

    def bwd_call(args, gs):
        s = args[0].shape[0]
        rows, aux, params = args[:n_rows], args[n_rows:n_rows + n_aux], args[n_rows + n_aux:]
        n_in, n_g, n_p = len(args), len(gs), len(params)
        n_gf = n_g - 1 if passthrough else n_g

        def body(*refs):
            vals = [r[...] for r in refs[:n_in]]
            gvals = tuple(r[...] for r in refs[n_in:n_in + n_gf])
            out_refs = refs[n_in + n_g:]
            auxv = vals[n_rows:n_rows + n_aux]

            def g_(*rp):
                return tuple(f(*rp[:n_rows], *auxv, *rp[n_rows:]))

            _, vjp = jax.vjp(g_, *vals[:n_rows], *vals[n_rows + n_aux:])
            cts = list(vjp(gvals))
            if passthrough:
                cts[0] = cts[0] + refs[n_in + n_gf][...]
            for o_ref, ct in zip(out_refs[:n_rows], cts[:n_rows]):
                o_ref[...] = ct.astype(o_ref.dtype)
            if n_p:
                @pl.when(pl.program_id(0) == 0)
                def _():
                    for o_ref in out_refs[n_rows:]:
                        o_ref[...] = jnp.zeros_like(o_ref)

                for o_ref, ct in zip(out_refs[n_rows:], cts[n_rows:]):
                    o_ref[...] += ct.astype(o_ref.dtype)

        return pl.pallas_call(
            body,
            name=name + "_bwd",
            grid=(s // tile,),
            in_specs=specs(rows + aux, True) + specs(params, False) + specs(gs, True),
            out_specs=specs(rows, True) + specs(params, False),
            out_shape=[jax.ShapeDtypeStruct(x.shape, x.dtype) for x in rows + params],
            compiler_params=_cparams("arbitrary" if n_p else "parallel"),
        )(*args, *gs)

    @jax.custom_vjp
    def op(*args):
        return tuple(fwd_call(*args)) + ((args[0],) if passthrough else ())

    def op_fwd(*args):
        return op(*args), args

    def op_bwd(args, gs):
        cts = bwd_call(tuple(args), tuple(gs))
        rows_ct, par_ct = cts[:n_rows], cts[n_rows:]
        aux_ct = [jnp.zeros_like(a) for a in args[n_rows:n_rows + n_aux]]
        return tuple(rows_ct) + tuple(aux_ct) + tuple(par_ct)

    op.defvjp(op_fwd, op_bwd)
    return op


SCAN_SEGMENTS = SUBLANES
SCAN_TILE_ROWS = 512
SCAN_TILE_LANES = 512


def _scan_specs(s, n):
    tr = min(SCAN_TILE_ROWS, s)
    tn = min(SCAN_TILE_LANES, n)
    return tr, tn, s // tr, n // tn


def _scan_step(ar, ai, xr, xi, br, bi):
    return ar * xr - ai * xi + br, ar * xi + ai * xr + bi


def _scan_finals(b_r, b_i, a, reverse, name):
    s, n = b_r.shape
    tr, tn, nt, nc = _scan_specs(s, n)
    ti = tr // SUBLANES
    tmap = (lambda c, j: (nt - 1 - j, c)) if reverse else (lambda c, j: (j, c))

    def body(br_ref, bi_ref, a_ref, fr_ref, fi_ref, sr, si):
        j = pl.program_id(1)

        @pl.when(j == 0)
        def _():
            sr[...] = jnp.zeros_like(sr)
            si[...] = jnp.zeros_like(si)

        ar = jnp.broadcast_to(a_ref[0:1, :], (SUBLANES, tn))
        ai = jnp.broadcast_to(a_ref[1:2, :], (SUBLANES, tn))

        def step(ii, carry):
            i = (ti - 1 - ii) if reverse else ii
            off = pl.multiple_of(i * SUBLANES, SUBLANES)
            return _scan_step(ar, ai, carry[0], carry[1], br_ref[pl.ds(off, SUBLANES), :], bi_ref[pl.ds(off, SUBLANES), :])

        xr, xi = lax.fori_loop(0, ti, step, (sr[...], si[...]), unroll=4)
        sr[...] = xr
        si[...] = xi

        @pl.when(j == nt - 1)
        def _():
            fr_ref[...] = xr
            fi_ref[...] = xi

    bspec = pl.BlockSpec((tr, tn), tmap)
    fspec = pl.BlockSpec((SUBLANES, tn), lambda c, j: (0, c))
    return pl.pallas_call(
        body,
        name=name,
        grid=(nc, nt),
        in_specs=[bspec, bspec, pl.BlockSpec((2, tn), lambda c, j: (0, c))],
        out_specs=[fspec, fspec],
        out_shape=[jax.ShapeDtypeStruct((SUBLANES, n), F32)] * 2,
        scratch_shapes=[pltpu.VMEM((SUBLANES, tn), F32)] * 2,
        compiler_params=_cparams("parallel", "arbitrary"),
    )(b_r, b_i, a)


def _scan_states(b_r, b_i, a, f_r, f_i, reverse, xs, name):
    s, n = b_r.shape
    tr, tn, nt, nc = _scan_specs(s, n)
    ti = tr // SUBLANES
    seg_len = s // SCAN_SEGMENTS
    assert seg_len & (seg_len - 1) == 0
    with_acc = xs is not None
    tmap = (lambda c, j: (nt - 1 - j, c)) if reverse else (lambda c, j: (j, c))
    order = list(range(SCAN_SEGMENTS))[::-1] if reverse else list(range(SCAN_SEGMENTS))

    def body(*refs):
        br_ref, bi_ref, a_ref, fr_ref, fi_ref = refs[:5]
        pos = 5
        if with_acc:
            xr_ref, xi_ref = refs[5:7]
            pos = 7
        or_ref, oi_ref = refs[pos:pos + 2]
        pos += 2
        if with_acc:
            dr_ref, di_ref = refs[pos:pos + 2]
            pos += 2
        sr, si = refs[pos:pos + 2]
        if with_acc:
            accr, acci = refs[pos + 2:pos + 4]
        j = pl.program_id(1)
        a_r1, a_i1 = a_ref[0:1, :], a_ref[1:2, :]

        @pl.when(j == 0)
        def _():
            pr, pi = a_r1, a_i1
            for _ in range(seg_len.bit_length() - 1):
                pr, pi = pr * pr - pi * pi, 2.0 * pr * pi
            cr = jnp.zeros((1, tn), F32)
            ci = jnp.zeros((1, tn), F32)
            for idx, k in enumerate(order):
                if idx > 0:
                    kp = order[idx - 1]
                    cr, ci = (fr_ref[kp:kp + 1, :] + pr * cr - pi * ci, fi_ref[kp:kp + 1, :] + pr * ci + pi * cr)
                sr[k:k + 1, :] = cr
                si[k:k + 1, :] = ci
            if with_acc:
                accr[...] = jnp.zeros_like(accr)
                acci[...] = jnp.zeros_like(acci)

        ar = jnp.broadcast_to(a_r1, (SUBLANES, tn))
        ai = jnp.broadcast_to(a_i1, (SUBLANES, tn))

        def step(ii, carry):
            i = (ti - 1 - ii) if reverse else ii
            off = pl.multiple_of(i * SUBLANES, SUBLANES)
            rows = pl.ds(off, SUBLANES)
            xr, xi = carry[0], carry[1]
            if with_acc:
                zr, zi = xr_ref[rows, :], xi_ref[rows, :]
                acc = (carry[2] + xr * zr + xi * zi, carry[3] + xi * zr - xr * zi)
            nr, ni = _scan_step(ar, ai, xr, xi, br_ref[rows, :], bi_ref[rows, :])
            or_ref[rows, :] = nr
            oi_ref[rows, :] = ni
            return (nr, ni) + (acc if with_acc else ())

        init = (sr[...], si[...]) + ((accr[...], acci[...]) if with_acc else ())
        out = lax.fori_loop(0, ti, step, init, unroll=4)
        sr[...] = out[0]
        si[...] = out[1]
        if with_acc:
            accr[...] = out[2]
            acci[...] = out[3]

            @pl.when(j == nt - 1)
            def _():
                dr_ref[...] = jnp.sum(out[2], axis=0, keepdims=True)
                di_ref[...] = jnp.sum(out[3], axis=0, keepdims=True)

    bspec = pl.BlockSpec((tr, tn), tmap)
    fspec = pl.BlockSpec((SUBLANES, tn), lambda c, j: (0, c))
    dspec = pl.BlockSpec((1, tn), lambda c, j: (0, c))
    in_specs = [bspec, bspec, pl.BlockSpec((2, tn), lambda c, j: (0, c)), fspec, fspec] + ([bspec, bspec] if with_acc else [])
    out_specs = [bspec, bspec] + ([dspec, dspec] if with_acc else [])
    out_shape = [jax.ShapeDtypeStruct((s, n), F32)] * 2 + ([jax.ShapeDtypeStruct((1, n), F32)] * 2 if with_acc else [])
    scratch = [pltpu.VMEM((SUBLANES, tn), F32)] * (4 if with_acc else 2)
    args = (b_r, b_i, a, f_r, f_i) + (tuple(xs) if with_acc else ())
    return pl.pallas_call(
        body,
        name=name,
        grid=(nc, nt),
        in_specs=in_specs,
        out_specs=out_specs,
        out_shape=out_shape,
        scratch_shapes=scratch,
        compiler_params=_cparams("parallel", "arbitrary"),
    )(*args)


@jax.custom_vjp
def s5_scan(b_r, b_i, a):
    f_r, f_i = _scan_finals(b_r, b_i, a, False, "s5_scan_fin")
    return tuple(_scan_states(b_r, b_i, a, f_r, f_i, False, None, "s5_scan"))


def _s5_scan_fwd(b_r, b_i, a):
    xs = s5_scan(b_r, b_i, a)
    return xs, (a, xs)


def _s5_scan_bwd(res, g):
    a, xs = res
    a_conj = a * jnp.array([[1.0], [-1.0]], F32)
    f_r, f_i = _scan_finals(g[0], g[1], a_conj, True, "s5_rscan_fin")
    g_r, g_i, da_r, da_i = _scan_states(g[0], g[1], a_conj, f_r, f_i, True, xs, "s5_rscan")
    return g_r, g_i, jnp.concatenate([da_r, da_i], axis=0)


s5_scan.defvjp(_s5_scan_fwd, _s5_scan_bwd)


BD_SEG_ROWS = 128


def _bd_view(x, seg):
    return x if seg else x.reshape(SCAN_SEGMENTS, x.shape[0] // SCAN_SEGMENTS, x.shape[1])


def _bd_spec(seg, w, rows):
    if seg:
        return pl.BlockSpec((SCAN_SEGMENTS * rows, w), lambda i, j: (i, j))
    return pl.BlockSpec((SCAN_SEGMENTS, rows, w), lambda i, j: (0, i, j))


def _bd_apply_call(a, w, transpose_w, a_seg, out_seg, add, name):
    s = a.shape[0]
    nb, ka, kn = w.shape
    wi, wo = (kn, ka) if transpose_w else (ka, kn)
    convert = a_seg != out_seg
    a_seg_k, out_seg_k = (a_seg, out_seg) if convert else (True, True)
    ov_shape = (s, nb * wo) if out_seg_k else (SCAN_SEGMENTS, s // SCAN_SEGMENTS, nb * wo)
    dn = _NT_DIMS if transpose_w else _NN_DIMS
    has_add = add is not None
    rows = min(BD_SEG_ROWS, s // SCAN_SEGMENTS)

    assert not convert or (wo if a_seg_k else wi) == LANES

    def body(*refs):
        a_ref, w_ref = refs[0], refs[1]
        c_ref = refs[2] if has_add else None
        o_ref = refs[3] if has_add else refs[2]
        wb = w_ref[0].astype(BF16)
        mm = lambda a_val: lax.dot_general(a_val.astype(BF16), wb, dn, preferred_element_type=F32)
        if not convert:
            r = mm(a_ref[...])
            o_ref[...] = r + c_ref[...] if has_add else r
        elif out_seg_k:
            scr = refs[-1]
            for k in range(SCAN_SEGMENTS):
                scr[pl.ds(k, rows, stride=SCAN_SEGMENTS), :] = a_ref[k]
            r = mm(scr[...])
            o_ref[...] = r + c_ref[...] if has_add else r
        else:
            scr = refs[-1]
            scr[...] = mm(a_ref[...])
            for k in range(SCAN_SEGMENTS):
                r = scr[pl.ds(k, rows, stride=SCAN_SEGMENTS), :]
                o_ref[k] = r + c_ref[k] if has_add else r

    args = [_bd_view(a, a_seg_k), w] + ([_bd_view(add, out_seg_k)] if has_add else [])
    in_specs = ([_bd_spec(a_seg_k, wi, rows), pl.BlockSpec((1, ka, kn), lambda i, j: (j, 0, 0))]
                + ([_bd_spec(out_seg_k, wo, rows)] if has_add else []))
    out = pl.pallas_call(
        body,
        name=name,
        grid=(s // (SCAN_SEGMENTS * rows), nb),
        in_specs=in_specs,
        out_specs=_bd_spec(out_seg_k, wo, rows),
        out_shape=jax.ShapeDtypeStruct(ov_shape, F32),
        scratch_shapes=[pltpu.VMEM((SCAN_SEGMENTS * rows, LANES), F32)] if convert else [],
        compiler_params=_cparams("parallel", "parallel"),
    )(*args)
    return out.reshape(s, nb * wo)


def _bd_weight_grad_call(a, g, a_seg, g_seg, nb, name):
    s = a.shape[0]
    ka, kn = a.shape[1] // nb, g.shape[1] // nb
    seg_len = s // SCAN_SEGMENTS
    convert = a_seg != g_seg

    def view(x, seg, w):
        if not convert:
            return x, pl.BlockSpec((s, w), lambda j: (0, j))
        if seg:
            return x, pl.BlockSpec((s, w), lambda j: (0, j))
        return x.reshape(SCAN_SEGMENTS, seg_len, x.shape[1]), pl.BlockSpec((SCAN_SEGMENTS, seg_len, w), lambda j: (0, 0, j))

    av, a_spec = view(a, a_seg, ka)
    gv, g_spec = view(g, g_seg, kn)

    assert not convert or (kn if a_seg else ka) == LANES

    def body(a_ref, g_ref, o_ref, *scratch):
        tn = lambda x, y: lax.dot_general(x.astype(BF16), y.astype(BF16), _TN_DIMS, preferred_element_type=F32)
        if not convert:
            o_ref[0] = tn(a_ref[...], g_ref[...])
        else:
            scr = scratch[0]
            t_ref = g_ref if a_seg else a_ref
            for k in range(SCAN_SEGMENTS):
                scr[pl.ds(k, seg_len, stride=SCAN_SEGMENTS), :] = t_ref[k]
            o_ref[0] = tn(a_ref[...], scr[...]) if a_seg else tn(scr[...], g_ref[...])

    return pl.pallas_call(
        body,
        name=name,
        grid=(nb,),
        in_specs=[a_spec, g_spec],
        out_specs=pl.BlockSpec((1, ka, kn), lambda j: (j, 0, 0)),
        out_shape=jax.ShapeDtypeStruct((nb, ka, kn), F32),
        scratch_shapes=[pltpu.VMEM((s, LANES), F32)] if convert else [],
        compiler_params=_cparams("parallel"),
    )(av, gv)


_NT_DIMS = (((1,), (1,)), ((), ()))
_TN_DIMS = (((0,), (0,)), ((), ()))


@functools.partial(jax.custom_vjp, nondiff_argnums=(2, 3))
def bd_matmul(a, w, a_seg, out_seg):
    return _bd_apply_call(a, w, False, a_seg, out_seg, None, "bd_mm_fwd")


def _bd_matmul_fwd(a, w, a_seg, out_seg):
    return bd_matmul(a, w, a_seg, out_seg), (a, w)


def _bd_matmul_bwd(a_seg, out_seg, res, g):
    a, w = res
    da = _bd_apply_call(g, w, True, out_seg, a_seg, None, "bd_mm_da")
    dw = _bd_weight_grad_call(a, g, a_seg, out_seg, w.shape[0], "bd_mm_dw")
    return da, dw


bd_matmul.defvjp(_bd_matmul_fwd, _bd_matmul_bwd)


@functools.partial(jax.custom_vjp, nondiff_argnums=(3, 4))
def bd_matmul_add(a, w, c, a_seg, out_seg):
    return _bd_apply_call(a, w, False, a_seg, out_seg, c, "bd_mm_add_fwd")


def _bd_matmul_add_fwd(a, w, c, a_seg, out_seg):
    return bd_matmul_add(a, w, c, a_seg, out_seg), (a, w)


def _bd_matmul_add_bwd(a_seg, out_seg, res, g):
    return _bd_matmul_bwd(a_seg, out_seg, res, g) + (g,)


bd_matmul_add.defvjp(_bd_matmul_add_fwd, _bd_matmul_add_bwd)


_NN = (((1,), (0,)), ((), ()))
_NT = (((1,), (1,)), ((), ()))
_TN = (((0,), (0,)), ((), ()))


def _dot(a, b, dn):
    return lax.dot_general(a.astype(BF16), b.astype(BF16), dn, preferred_element_type=F32)


@jax.custom_vjp
def bdot_nn(a, b):
    return _dot(a, b, _NN)


bdot_nn.defvjp(lambda a, b: (_dot(a, b, _NN), (a, b)),
               lambda r, g: (_dot(g, r[1], _NT).astype(r[0].dtype), _dot(r[0], g, _TN).astype(r[1].dtype)))


@jax.custom_vjp
def bdot_nt(a, b):
    return _dot(a, b, _NT)


bdot_nt.defvjp(lambda a, b: (_dot(a, b, _NT), (a, b)),
               lambda r, g: (_dot(g, r[1], _NN).astype(r[0].dtype), _dot(g, r[0], _TN).astype(r[1].dtype)))


@jax.custom_vjp
def bdot_tn(a, b):
    return _dot(a, b, _TN)


bdot_tn.defvjp(lambda a, b: (_dot(a, b, _TN), (a, b)),
               lambda r, g: (_dot(r[1], g, _NT).astype(r[0].dtype), _dot(r[0], g, _NN).astype(r[1].dtype)))


def _split3(x):
    h = x.astype(BF16)
    r = x - h.astype(F32)
    m = r.astype(BF16)
    l = (r - m.astype(F32)).astype(BF16)
    return h, m, l


def _exact_dot(t, x, dn):
    h, m, l = _split3(x)
    d = lambda p: lax.dot_general(t, p, dn, preferred_element_type=F32)
    return d(h) + d(m) + d(l)


@jax.custom_vjp
def select_dot(t, x):
    return _exact_dot(t, x, _NN)


select_dot.defvjp(lambda t, x: (_exact_dot(t, x, _NN), t),
                  lambda t, g: (jnp.zeros_like(t), _exact_dot(t, g, _TN)))


def _split_rows_impl(x, h):
    return tuple(x[i * h:(i + 1) * h] for i in range(x.shape[0] // h))


@functools.partial(jax.custom_vjp, nondiff_argnums=(1,))
def split_rows(x, h):
    return _split_rows_impl(x, h)


split_rows.defvjp(lambda x, h: (_split_rows_impl(x, h), None),
                  lambda h, r, g: (jnp.concatenate(g, axis=0),))


@jax.custom_vjp
def join_rows(parts):
    return jnp.concatenate(parts, axis=0)


def _join_rows_bwd(hs, g):
    out, off = [], 0
    for h in hs:
        out.append(g[off:off + h])
        off += h
    return (tuple(out),)


join_rows.defvjp(lambda parts: (jnp.concatenate(parts, axis=0), tuple(p.shape[0] for p in parts)), _join_rows_bwd)


def _split_lanes_impl(x, w):
    return tuple(x[:, i * w:(i + 1) * w] for i in range(x.shape[1] // w))


@functools.partial(jax.custom_vjp, nondiff_argnums=(1,))
def split_lanes(x, w):
    return _split_lanes_impl(x, w)


split_lanes.defvjp(lambda x, w: (_split_lanes_impl(x, w), None),
                   lambda w, r, g: (jnp.concatenate(g, axis=1),))


def _join_impl(parts):
    return jnp.concatenate(parts, axis=1)


@jax.custom_vjp
def join_lanes(parts):
    return _join_impl(parts)


def _join_bwd(ws, g):
    out, off = [], 0
    for w in ws:
        out.append(g[:, off:off + w])
        off += w
    return (tuple(out),)


join_lanes.defvjp(lambda parts: (_join_impl(parts), tuple(p.shape[1] for p in parts)), _join_bwd)


def _rope_impl(x, c, sa, sb, shift):
    w = x.shape[1]
    return x * c + pltpu.roll(x, w - shift, 1) * sa + pltpu.roll(x, shift, 1) * sb


@functools.partial(jax.custom_vjp, nondiff_argnums=(4,))
def rope_lanes(x, c, sa, sb, shift):
    return _rope_impl(x, c, sa, sb, shift)


def _rope_bwd(shift, r, g):
    c, sa, sb = r
    w = g.shape[1]
    dx = g * c + pltpu.roll(g * sa, shift, 1) + pltpu.roll(g * sb, w - shift, 1)
    return dx, jnp.zeros_like(c), jnp.zeros_like(sa), jnp.zeros_like(sb)


rope_lanes.defvjp(lambda x, c, sa, sb, shift: (_rope_impl(x, c, sa, sb, shift), (c, sa, sb)), _rope_bwd)


RMS_EPS = 1e-6


def _rms(x, g):
    return x * lax.rsqrt(jnp.mean(x * x, axis=-1, keepdims=True) + RMS_EPS) * g


ATTN_BLOCK = 512
MASK_VALUE = -1e30
LOG2E = math.log2(math.e)
LN2 = math.log(2.0)
V_ONES_LANE = 64


def _causal_mask(t):
    r = lax.broadcasted_iota(jnp.int32, (t, t), 0)
    c = lax.broadcasted_iota(jnp.int32, (t, t), 1)
    return c <= r


def _attn_fwd_call(q, k, v):
    s, width = q.shape
    n_heads = width // LANES
    tq = min(ATTN_BLOCK, s)
    nq = s // tq

    def body(q_ref, k_ref, v_ref, o_ref, lse_ref):
        i = pl.program_id(1)
        qb = q_ref[...].astype(BF16)
        ones_lane = lax.broadcasted_iota(jnp.int32, (tq, LANES), 1) == V_ONES_LANE

        def block(kb, carry, masked):
            m, acc = carry
            rows = pl.ds(pl.multiple_of(kb * tq, tq), tq)
            sc = lax.dot_general(qb, k_ref[rows, :].astype(BF16), _NT, preferred_element_type=F32)
            if masked:
                sc = jnp.where(_causal_mask(tq), sc, MASK_VALUE)
            m_new = jnp.maximum(m, jnp.max(sc, axis=-1, keepdims=True))
            p = jnp.exp2(sc - m_new).astype(BF16)
            vb = jnp.where(ones_lane, 1.0, v_ref[rows, :]).astype(BF16)
            acc = jnp.exp2(m - m_new) * acc + lax.dot_general(p, vb, _NN, preferred_element_type=F32)
            return m_new, acc

        init = (jnp.full((tq, 1), MASK_VALUE, F32), jnp.zeros((tq, LANES), F32))
        carry = lax.fori_loop(0, i, lambda kb, c: block(kb, c, False), init)
        m, acc = block(i, carry, True)
        l = jnp.sum(jnp.where(ones_lane, acc, 0.0), axis=-1, keepdims=True)
        o_ref[...] = jnp.where(ones_lane, 0.0, acc / l).astype(o_ref.dtype)
        lse_ref[...] = jnp.broadcast_to(m + jnp.log2(l), (tq, LANES))

    qspec = pl.BlockSpec((tq, LANES), lambda h, i: (i, h))
    kspec = pl.BlockSpec((s, LANES), lambda h, i: (0, h))
    return pl.pallas_call(
        body,
        name="mla_attn_fwd",
        grid=(n_heads, nq),
        in_specs=[qspec, kspec, kspec],
        out_specs=[qspec, qspec],
        out_shape=[jax.ShapeDtypeStruct((s, width), BF16), jax.ShapeDtypeStruct((s, width), F32)],
        compiler_params=_cparams("parallel", "parallel"),
    )(q, k, v)


def _attn_bwd_call(q, k, v, o, lse, do):
    s, width = q.shape
    n_heads = width // LANES
    tq = min(ATTN_BLOCK, s)
    nq = s // tq

    def body(q_ref, k_ref, v_ref, o_ref, lse_ref, do_ref, dq_ref, dk_ref, dv_ref, dq_acc):
        j = pl.program_id(1)

        @pl.when(j == 0)
        def _():
            dq_acc[...] = jnp.zeros_like(dq_acc)

        kb = k_ref[...].astype(BF16)
        vb = v_ref[...].astype(BF16)

        def block(i, carry, masked):
            dk, dv = carry
            rows = pl.ds(pl.multiple_of(i * tq, tq), tq)
            qi = q_ref[rows, :].astype(BF16)
            doi = do_ref[rows, :].astype(F32)
            delta = jnp.sum(doi * o_ref[rows, :].astype(F32), axis=-1, keepdims=True)
            sc = lax.dot_general(qi, kb, _NT, preferred_element_type=F32)
            if masked:
                sc = jnp.where(_causal_mask(tq), sc, MASK_VALUE)
            p = jnp.exp2(sc - lse_ref[rows, 0:1])
            dob = doi.astype(BF16)
            dv = dv + lax.dot_general(p.astype(BF16), dob, _TN, preferred_element_type=F32)
            dp = lax.dot_general(dob, vb, _NT, preferred_element_type=F32)
            ds = (p * (dp - delta)).astype(BF16)
            dq_acc[rows, :] += lax.dot_general(ds, kb, _NN, preferred_element_type=F32)
            dk = dk + lax.dot_general(ds, qi, _TN, preferred_element_type=F32)
            return dk, dv

        zero = jnp.zeros((tq, LANES), F32)
        carry = block(j, (zero, zero), True)
        dk, dv = lax.fori_loop(j + 1, nq, lambda i, c: block(i, c, False), carry)
        dk_ref[...] = (dk * LN2).astype(dk_ref.dtype)
        dv_ref[...] = dv.astype(dv_ref.dtype)

        @pl.when(j == nq - 1)
        def _():
            dq_ref[...] = (dq_acc[...] * LN2).astype(dq_ref.dtype)

    full = pl.BlockSpec((s, LANES), lambda h, j: (0, h))
    blk = pl.BlockSpec((tq, LANES), lambda h, j: (j, h))
    return pl.pallas_call(
        body,
        name="mla_attn_bwd",
        grid=(n_heads, nq),
        in_specs=[full, blk, blk, full, full, full],
        out_specs=[full, blk, blk],
        out_shape=[jax.ShapeDtypeStruct((s, width), t.dtype) for t in (q, k, v)],
        scratch_shapes=[pltpu.VMEM((s, LANES), F32)],
        compiler_params=_cparams("parallel", "arbitrary"),
    )(q, k, v, o, lse, do)


@jax.custom_vjp
def causal_attention(q, k, v):
    return _attn_fwd_call(q, k, v)[0]


def _causal_attention_fwd(q, k, v):
    o, lse = _attn_fwd_call(q, k, v)
    return o, (q, k, v, o, lse)


def _causal_attention_bwd(res, do):
    return tuple(_attn_bwd_call(*res, do))


causal_attention.defvjp(_causal_attention_fwd, _causal_attention_bwd)


HG_HEADS = 4
HG_CHUNK = 32
HG_REF_ROW = HG_CHUNK // 2 - 1
HG_TILE_ROWS = 256
HG_EXP_CLAMP = 80.0


def _hg_tile_masks(t):
    shift = HG_CHUNK.bit_length() - 1
    r = lax.broadcasted_iota(jnp.int32, (t, t), 0)
    c = lax.broadcasted_iota(jnp.int32, (t, t), 1)
    start = lax.shift_left(lax.shift_right_logical(r, shift), shift)
    causal = (c >= start) & (c <= r)
    return causal, c == start + HG_REF_ROW, c == start + (HG_CHUNK - 1)


def _hg_tile(q, fl, v, lb, st):
    t = q.shape[0]
    causal, ref_sel, last_sel = _hg_tile_masks(t)
    f = lb + (1.0 - lb) * jax.nn.sigmoid(fl)
    kk = 1.0 - f
    qs = q * jax.nn.sigmoid(q)
    b = select_dot(causal.astype(BF16), jnp.log(f))
    b_ref = select_dot(ref_sel.astype(BF16), b)
    b_last = select_dot(last_sel.astype(BF16), b)
    q_in = qs * jnp.exp(jnp.minimum(b - b_ref, HG_EXP_CLAMP))
    k_in = kk * jnp.exp(jnp.minimum(b_ref - b, HG_EXP_CLAMP))
    o = bdot_nn(jnp.where(causal, bdot_nt(q_in, k_in), 0.0), v)
    q_hat = split_rows(qs * jnp.exp(b), HG_CHUNK)
    k_hat = split_rows(kk * jnp.exp(b_last - b), HG_CHUNK)
    decay = split_rows(jnp.exp(b_last), HG_CHUNK)
    vs = split_rows(v, HG_CHUNK)
    first_row = lax.broadcasted_iota(jnp.int32, (HG_CHUNK, LANES), 0) == 0
    inter = []
    for c in range(t // HG_CHUNK):
        inter.append(bdot_nt(q_hat[c], st))
        st = st * jnp.sum(jnp.where(first_row, decay[c], 0.0), axis=0, keepdims=True) + bdot_tn(vs[c], k_hat[c])
    return o + join_rows(tuple(inter)), st


def _hg_head(q, fl, v, gate, lb, gn, st):
    o, st = _hg_tile(q, fl, v, lb, st)
    return _rms(o, gn) * (gate * jax.nn.sigmoid(gate)), st


HG_PARTS = 4


def _hg_part_slices(h, width):
    return [slice(p * width + h * LANES, p * width + (h + 1) * LANES) for p in range(HG_PARTS)]


def _hg_fwd_call(x, lb, gn):
    s = x.shape[0]
    width = x.shape[1] // HG_PARTS
    tr = min(HG_TILE_ROWS, s)
    nt = s // tr

    def body(x_ref, lb_ref, gn_ref, o_ref, sts_ref, st_ref):
        @pl.when(pl.program_id(0) == 0)
        def _():
            st_ref[...] = jnp.zeros_like(st_ref)

        for h in range(HG_HEADS):
            ln = slice(h * LANES, (h + 1) * LANES)
            st = st_ref[h]
            sts_ref[0, h] = st
            o, st_new = _hg_head(*(x_ref[:, sl] for sl in _hg_part_slices(h, width)), lb_ref[:, ln], gn_ref[...], st)
            o_ref[:, ln] = o.astype(o_ref.dtype)
            st_ref[h] = st_new

    const = lambda shape: pl.BlockSpec(shape, lambda j: (0, 0))
    return pl.pallas_call(
        body,
        name="hgrn2_fwd",
        grid=(nt,),
        in_specs=[pl.BlockSpec((tr, HG_PARTS * width), lambda j: (j, 0)), const((1, width)), const((1, LANES))],
        out_specs=[pl.BlockSpec((tr, width), lambda j: (j, 0)),
                   pl.BlockSpec((1, HG_HEADS, LANES, LANES), lambda j: (j, 0, 0, 0))],
        out_shape=[jax.ShapeDtypeStruct((s, width), BF16),
                   jax.ShapeDtypeStruct((nt, HG_HEADS, LANES, LANES), F32)],
        scratch_shapes=[pltpu.VMEM((HG_HEADS, LANES, LANES), F32)],
        compiler_params=_cparams("arbitrary"),
    )(x, lb, gn)


def _hg_bwd_call(x, lb, gn, sts, do):
    s = x.shape[0]
    width = x.shape[1] // HG_PARTS
    tr = min(HG_TILE_ROWS, s)
    nt = s // tr

    def body(x_ref, lb_ref, gn_ref, sts_ref, do_ref, dx_ref, dlb_ref, dgn_ref, dst_ref):
        @pl.when(pl.program_id(0) == 0)
        def _():
            dst_ref[...] = jnp.zeros_like(dst_ref)
            dlb_ref[...] = jnp.zeros_like(dlb_ref)
            dgn_ref[...] = jnp.zeros_like(dgn_ref)

        for h in range(HG_HEADS):
            ln = slice(h * LANES, (h + 1) * LANES)
            parts = _hg_part_slices(h, width)
            _, vjp = jax.vjp(_hg_head, *(x_ref[:, sl] for sl in parts), lb_ref[:, ln], gn_ref[...], sts_ref[0, h])
            cts = vjp((do_ref[:, ln].astype(F32), dst_ref[h]))
            for sl, ct in zip(parts, cts[:HG_PARTS]):
                dx_ref[:, sl] = ct.astype(dx_ref.dtype)
            dlb_ref[:, ln] += cts[HG_PARTS]
            dgn_ref[...] += cts[HG_PARTS + 1]
            dst_ref[h] = cts[HG_PARTS + 2]

    rev = lambda w: pl.BlockSpec((tr, w), lambda j: (nt - 1 - j, 0))
    const = lambda shape: pl.BlockSpec(shape, lambda j: (0, 0))
    return pl.pallas_call(
        body,
        name="hgrn2_bwd",
        grid=(nt,),
        in_specs=[rev(HG_PARTS * width), const((1, width)), const((1, LANES)),
                  pl.BlockSpec((1, HG_HEADS, LANES, LANES), lambda j: (nt - 1 - j, 0, 0, 0)), rev(width)],
        out_specs=[rev(HG_PARTS * width), const((1, width)), const((1, LANES))],
        out_shape=[jax.ShapeDtypeStruct(x.shape, BF16), jax.ShapeDtypeStruct((1, width), F32),
                   jax.ShapeDtypeStruct((1, LANES), F32)],
        scratch_shapes=[pltpu.VMEM((HG_HEADS, LANES, LANES), F32)],
        compiler_params=_cparams("arbitrary"),
    )(x, lb, gn, sts, do)


@jax.custom_vjp
def hgrn2_mixer(h, w, lb, gn):
    return _hg_fwd_call(_mm_call(h, w, False, False, name="hgrn2_proj"), lb, gn)[0]


def _hgrn2_mixer_fwd(h, w, lb, gn):
    x = _mm_call(h, w, False, False, name="hgrn2_proj")
    o, sts = _hg_fwd_call(x, lb, gn)
    return o, (h, w, x, lb, gn, sts)


def _hgrn2_mixer_bwd(res, do):
    h, w, x, lb, gn, sts = res
    dx, dlb, dgn = _hg_bwd_call(x, lb, gn, sts, do)
    dh = _mm_call(dx, w, False, True, out_dtype=h.dtype, name="hgrn2_proj_da")
    dw = _mm_call(h, dx, True, False, out_dtype=w.dtype, name="hgrn2_proj_db")
    return dh, dw, dlb, dgn


hgrn2_mixer.defvjp(_hgrn2_mixer_fwd, _hgrn2_mixer_bwd)


D_MODEL = 1024
DEPTH = 2
SSM_GROUPS, SSM_GROUP_CH, SSM_STATE = 32, 16, 64
SSM_WIDTH = SSM_GROUPS * SSM_GROUP_CH
MLA_HEADS, MLA_NOPE, MLA_ROPE, MLA_V = 8, 64, 32, 64
MLA_Q_RANK, MLA_KV_RANK = 512, 256
HG_WIDTH = HG_HEADS * LANES
X_HEADS, X_HEAD_DIM = 4, 128
X_WIDTH = X_HEADS * X_HEAD_DIM
D_FF = 2816
ROPE_THETA = 10000.0
IN_SPLITS = (SSM_WIDTH, MLA_Q_RANK, MLA_KV_RANK, MLA_ROPE, HG_WIDTH, HG_WIDTH, HG_WIDTH, HG_WIDTH, 3 * D_MODEL)
ROPE_LANE0 = MLA_NOPE
MLA_Q_SCALE = LOG2E / math.sqrt(MLA_NOPE + MLA_ROPE)
ROW_TILE = 256


def _t_rms(x, g):
    return (_rms(x, g).astype(BF16),)


def _t_s5_act(y, u, d):
    return (jax.nn.gelu(y + d * u).astype(BF16),)


def _t_glu(z):
    zo, zg = split_lanes(z.astype(F32), D_MODEL)
    return (zo * jax.nn.sigmoid(zg),)


def _t_mla_rope(q, k, kr, c, sa, sb):
    rep = lambda t: jnp.concatenate([t] * MLA_HEADS, axis=1)
    half = MLA_ROPE // 2
    q_out = rope_lanes(q, rep(c), rep(sa), rep(sb), half) * MLA_Q_SCALE
    kr_out = rope_lanes(kr, c, sa, sb, half)
    return q_out.astype(BF16), (k + join_lanes((kr_out,) * MLA_HEADS)).astype(BF16)


def _t_merge(y_ssm, y_mla, y_hg, gates):
    g0, g1, g2 = split_lanes(gates.astype(F32), D_MODEL)
    return ((jax.nn.sigmoid(g0) * y_ssm + jax.nn.sigmoid(g1) * y_mla + jax.nn.sigmoid(g2) * y_hg).astype(BF16),)


def _t_xattn(q, kv):
    scale = 1.0 / math.sqrt(X_HEAD_DIM)
    heads = split_lanes(kv, X_HEAD_DIM)
    outs = []
    for qh, kh, vh in zip(split_lanes(q, X_HEAD_DIM), heads[:X_HEADS], heads[X_HEADS:]):
        sc = bdot_nt(qh, kh) * scale
        p = jnp.exp(sc - jnp.max(sc, axis=-1, keepdims=True))
        p = p / jnp.sum(p, axis=-1, keepdims=True)
        outs.append(bdot_nn(p, vh))
    return (join_lanes(tuple(outs)).astype(BF16),)


def _t_swiglu(gate_up):
    gt, up = split_lanes(gate_up.astype(F32), D_FF)
    return ((gt * jax.nn.sigmoid(gt) * up).astype(BF16),)


def _t_loss(x, tgt, g):
    e = _rms(x, g) - tgt
    return (jnp.broadcast_to(jnp.mean(e * e, axis=-1, keepdims=True), (x.shape[0], LANES)),)


rms_op = rowwise(_t_rms, 1, 0, ROW_TILE, "rmsnorm")
rms_res_op = rowwise(_t_rms, 1, 0, ROW_TILE, "rmsnorm_res", passthrough=True)
s5_act_op = rowwise(_t_s5_act, 2, 0, ROW_TILE, "s5_act")
glu_op = rowwise(_t_glu, 1, 0, ROW_TILE, "glu")
mla_rope_op = rowwise(_t_mla_rope, 3, 3, ROW_TILE, "mla_rope")
merge_op = rowwise(_t_merge, 4, 0, ROW_TILE, "merge")
xattn_op = rowwise(_t_xattn, 1, 0, ROW_TILE, "xattn")
swiglu_op = rowwise(_t_swiglu, 1, 0, ROW_TILE, "swiglu")
loss_op = rowwise(_t_loss, 1, 1, ROW_TILE, "loss")


def _rope_tables(positions):
    half = MLA_ROPE // 2
    inv_freq = ROPE_THETA ** (-jnp.arange(half, dtype=F32) / half)
    ang = positions.astype(F32)[:, None] * inv_freq
    cos, sin = jnp.cos(ang), jnp.sin(ang)
    s = positions.shape[0]
    z = lambda w: jnp.zeros((s, w), F32)
    tail = LANES - ROPE_LANE0 - MLA_ROPE
    c = jnp.concatenate([jnp.ones((s, ROPE_LANE0), F32), cos, cos, z(tail)], axis=1)
    sa = jnp.concatenate([z(ROPE_LANE0), -sin, z(half), z(tail)], axis=1)
    sb = jnp.concatenate([z(ROPE_LANE0), z(half), sin, z(tail)], axis=1)
    return c, sa, sb


def _s5_operators(lam_re, lam_im, b_re, b_im, c_re, c_im, log_step):
    g, p, h = SSM_GROUPS, SSM_STATE, SSM_GROUP_CH
    lam = lax.complex(lam_re, lam_im)
    lam_bar = jnp.exp(lam * jnp.exp(log_step)[:, None])
    b_bar = ((lam_bar - 1.0) / lam)[..., None] * lax.complex(b_re, b_im)
    per = LANES // h
    nb = g // per
    eye = jnp.eye(per, dtype=F32)
    bd = lambda t: jnp.einsum("jgph,gk->jghkp", t.reshape(nb, per, p, h), eye).reshape(nb, per * h, per * p)
    cd = lambda t: jnp.einsum("jghp,gk->jgpkh", t.reshape(nb, per, h, p), eye).reshape(nb, per * p, per * h)
    a = jnp.stack([jnp.real(lam_bar).reshape(-1), jnp.imag(lam_bar).reshape(-1)])
    return a, bd(jnp.real(b_bar)), bd(jnp.imag(b_bar)), cd(c_re), cd(-c_im)


LATENT_WIDTH = 1536
_LATENT = {}
_off = 0
for _name, _w in (("u", SSM_WIDTH), ("q_lat", MLA_Q_RANK), ("kv_lat", MLA_KV_RANK), ("k_rope", LANES)):
    _LATENT[_name] = (_off, _off + _w)
    _off += _w


def _layer_matrices(w, l):
    w_in = w["w_in"][l]
    d, dt = w_in.shape[0], w_in.dtype
    z = lambda n: jnp.zeros((d, n), dt)
    r0 = SSM_WIDTH + MLA_Q_RANK + MLA_KV_RANK
    r1 = r0 + MLA_ROPE
    r2 = r1 + HG_PARTS * HG_WIDTH
    w_latent = jnp.concatenate([w_in[:, :r0], z(ROPE_LANE0), w_in[:, r0:r1],
                                z(LATENT_WIDTH - r0 - ROPE_LANE0 - MLA_ROPE)], axis=1)
    pad_heads = lambda t: jnp.pad(t, ((0, 0), (0, 0), (0, LANES - t.shape[2]))).reshape(t.shape[0], -1)
    uq = w["mla_w_uq"][l].reshape(MLA_Q_RANK, MLA_HEADS, MLA_NOPE + MLA_ROPE)
    ukv = w["mla_w_ukv"][l].reshape(MLA_KV_RANK, MLA_HEADS, MLA_NOPE + MLA_V)
    wo = w["mla_w_o"][l].reshape(MLA_HEADS, MLA_V, D_MODEL)
    return dict(
        w_latent=w_latent, w_hg=w_in[:, r1:r2], w_gates=w_in[:, r2:], glu=w["ssm_w_glu"][l],
        uq=pad_heads(uq), uk=pad_heads(ukv[:, :, :MLA_NOPE]), uv=pad_heads(ukv[:, :, MLA_NOPE:]),
        mla_o=jnp.pad(wo, ((0, 0), (0, LANES - MLA_V), (0, 0))).reshape(MLA_HEADS * LANES, D_MODEL),
        hg_o=w["hg_w_o"][l], w_out=w["w_out"][l], x_q=w["x_w_q"][l], x_kv=w["x_w_kv"][l],
        x_o=w["x_w_o"][l], ffn_gu=w["ffn_w_gate_up"][l], ffn_d=w["ffn_w_down"][l])


def _layer(x, mem, tabs, m, sp, l, lower_bound):
    row = lambda name: sp[name][l].reshape(1, -1)
    h, x = rms_res_op(x, row("norm_mix"))
    latent = matmul(h, m["w_latent"])
    seg = lambda name: latent[:, _LATENT[name][0]:_LATENT[name][1]]
    a, bd_r, bd_i, cd_r, cd_i = _s5_operators(*(sp[n][l] for n in (
        "ssm_lam_re", "ssm_lam_im", "ssm_b_re", "ssm_b_im", "ssm_c_re", "ssm_c_im", "ssm_log_step")))
    u = seg("u")
    x_r, x_i = s5_scan(bd_matmul(u, bd_r, False, True), bd_matmul(u, bd_i, False, True), a)
    y = bd_matmul_add(x_i, cd_i, bd_matmul(x_r, cd_r, True, False), True, False)
    (ya,) = s5_act_op(y, u, row("ssm_d"))
    (y_ssm,) = glu_op(matmul(ya, m["glu"], BF16))
    (qn,) = rms_op(seg("q_lat"), row("mla_q_norm"))
    (kvn,) = rms_op(seg("kv_lat"), row("mla_kv_norm"))
    q, k = mla_rope_op(matmul(qn, m["uq"]), matmul(kvn, m["uk"]), seg("k_rope"), *tabs)
    o = causal_attention(q, k, matmul(kvn, m["uv"], BF16))
    y_mla = matmul(o, m["mla_o"])
    y_hg = matmul(hgrn2_mixer(h, m["w_hg"], lower_bound, row("hg_g_norm")), m["hg_o"])
    (merged,) = merge_op(y_ssm, y_mla, y_hg, matmul(h, m["w_gates"], BF16))
    x = matmul_add(merged, m["w_out"], x)
    hc, x = rms_res_op(x, row("norm_cross"))
    (mn,) = rms_op(mem, row("norm_mem"))
    (ox,) = xattn_op(matmul(hc, m["x_q"], BF16), matmul(mn, m["x_kv"]))
    x = matmul_add(ox, m["x_o"], x)
    hf, x = rms_res_op(x, row("norm_ffn"))
    (act,) = swiglu_op(matmul(hf, m["ffn_gu"], BF16))
    return matmul_add(act, m["ffn_d"], x)


def _local_loss(x, mem, positions, target, w, sp):
    tabs = _rope_tables(positions)
    lb_p = jax.nn.softmax(sp["hg_lb"], axis=0)
    lower = jnp.cumsum(lb_p, axis=0) - lb_p[0:1]
    for l in range(DEPTH):
        x = _layer(x, mem, tabs, _layer_matrices(w, l), sp, l, lower[l].reshape(1, -1))
    (row_loss,) = loss_op(x, target, sp["norm_final"].reshape(1, -1))
    return 0.5 * jnp.sum(row_loss[:, 0])


N_DEV = 8
N_CHIPS = 4
COMM_LANES = 512
MESH_ID = pl.DeviceIdType.MESH
_ANY = pl.BlockSpec(memory_space=pl.ANY)
_OTHER_CHIPS = ((1, 0), (0, 1), (1, 1))


def _place():
    return lax.axis_index("x"), lax.axis_index("y"), lax.axis_index("c")


def _all_gather_call(blocks, name):
    n = len(blocks)

    def body(*refs):
        x_refs, out_refs = refs[:n], refs[n:2 * n]
        send_sems, recv_sems, local_sems = refs[2 * n:]
        x, y, c = _place()
        me, sibling = (x, y, c), (x, y, 1 - c)
        chips = [(x ^ fx, y ^ fy) for fx, fy in _OTHER_CHIPS]

        def slot(i, px, py, pc):
            return out_refs[i].at[4 * px + 2 * py + pc]

        def copy(i, k, blk, to, src=None):
            return pltpu.make_async_remote_copy(
                src_ref=slot(i, *blk) if src is None else src, dst_ref=slot(i, *blk),
                send_sem=send_sems.at[i, k], recv_sem=recv_sems.at[i, k], device_id=to, device_id_type=MESH_ID)

        mine = [pltpu.make_async_copy(x_refs[i], slot(i, *me), local_sems.at[i]) for i in range(n)]
        first = []
        for i in range(n):
            first.append(copy(i, 0, me, sibling, src=x_refs[i]))
            first += [copy(i, 1 + j, me, (*chip, c), src=x_refs[i]) for j, chip in enumerate(chips)]
        for cp in mine + first:
            cp.start()
        passed = []
        for j, chip in enumerate(chips):
            for i in range(n):
                copy(i, 1 + j, (*chip, c), me).wait_recv()
                passed.append(copy(i, 4 + j, (*chip, c), sibling))
                passed[-1].start()
        for i in range(n):
            copy(i, 0, sibling, me).wait_recv()
            for j, chip in enumerate(chips):
                copy(i, 4 + j, (*chip, 1 - c), me).wait_recv()
        for cp in first + passed:
            cp.wait_send()
        for cp in mine:
            cp.wait()

    return pl.pallas_call(
        body,
        name=name,
        out_shape=[jax.ShapeDtypeStruct((N_DEV,) + b.shape, b.dtype) for b in blocks],
        in_specs=[_ANY] * n,
        out_specs=[_ANY] * n,
        scratch_shapes=[pltpu.SemaphoreType.DMA((n, 7)), pltpu.SemaphoreType.DMA((n, 7)), pltpu.SemaphoreType.DMA((n,))],
    )(*blocks)


def _pair_exchange_call(gs, name):
    n = len(gs)

    def body(*refs):
        g_refs, got_refs = refs[:n], refs[n:2 * n]
        send_sems, recv_sems = refs[2 * n:]
        x, y, c = _place()
        sends = [pltpu.make_async_remote_copy(
            src_ref=g_refs[i].at[2 * p + (1 - c)], dst_ref=got_refs[i].at[p],
            send_sem=send_sems.at[i, p], recv_sem=recv_sems.at[i, p], device_id=(x, y, 1 - c), device_id_type=MESH_ID)
            for i in range(n) for p in range(N_CHIPS)]
        for cp in sends:
            cp.start()
        for cp in sends:
            cp.wait_recv()
        for cp in sends:
            cp.wait_send()

    return pl.pallas_call(
        body,
        name=name,
        out_shape=[jax.ShapeDtypeStruct((N_CHIPS,) + g.shape[1:], g.dtype) for g in gs],
        in_specs=[_ANY] * n,
        out_specs=[_ANY] * n,
        scratch_shapes=[pltpu.SemaphoreType.DMA((n, N_CHIPS))] * 2,
    )(*gs)


def _chip_exchange_call(parts, name):
    n = len(parts)

    def body(*refs):
        p_refs, got_refs = refs[:n], refs[n:2 * n]
        send_sems, recv_sems = refs[2 * n:]
        x, y, c = _place()
        sends = []
        for i in range(n):
            for k, (fx, fy) in enumerate(_OTHER_CHIPS):
                px, py = x ^ fx, y ^ fy
                sends.append(pltpu.make_async_remote_copy(
                    src_ref=p_refs[i].at[2 * px + py], dst_ref=got_refs[i].at[k],
                    send_sem=send_sems.at[i, k], recv_sem=recv_sems.at[i, k], device_id=(px, py, c), device_id_type=MESH_ID))
        for cp in sends:
            cp.start()
        for cp in sends:
            cp.wait_recv()
        for cp in sends:
            cp.wait_send()

    return pl.pallas_call(
        body,
        name=name,
        out_shape=[jax.ShapeDtypeStruct((3,) + p.shape[1:], p.dtype) for p in parts],
        in_specs=[_ANY] * n,
        out_specs=[_ANY] * n,
        scratch_shapes=[pltpu.SemaphoreType.DMA((n, 3))] * 2,
    )(*parts)


def _rows_cols(shape):
    return math.prod(shape[:-1]), shape[-1]


def _pair_sum_call(g, got, c_idx, name):
    rows, cols = _rows_cols(got.shape[1:])
    tr = _pick_tile(rows, (512, 256, 128, 64, 32, 16))

    def body(c_ref, a_ref, b_ref, o_ref):
        o_ref[...] = (a_ref[...].astype(F32) + b_ref[...].astype(F32)).astype(o_ref.dtype)

    spec = pl.BlockSpec((1, tr, cols), lambda p, i, c_ref: (p, i, 0))
    out = pl.pallas_call(
        body,
        name=name,
        grid_spec=pltpu.PrefetchScalarGridSpec(
            num_scalar_prefetch=1, grid=(N_CHIPS, rows // tr),
            in_specs=[pl.BlockSpec((1, tr, cols), lambda p, i, c_ref: (2 * p + c_ref[0], i, 0)), spec],
            out_specs=spec),
        out_shape=jax.ShapeDtypeStruct((N_CHIPS, rows, cols), got.dtype),
        compiler_params=_cparams("parallel", "parallel"),
    )(c_idx, g.reshape(N_DEV, rows, cols), got.reshape(N_CHIPS, rows, cols))
    return out.reshape(got.shape)


def _chip_sum_call(part, got, chip_idx, name):
    rows, cols = _rows_cols(got.shape[1:])
    tr = _pick_tile(rows, (512, 256, 128, 64, 32, 16))

    def body(p_ref, a_ref, b_ref, o_ref):
        acc = a_ref[0].astype(F32)
        for k in range(3):
            acc = acc + b_ref[k].astype(F32)
        o_ref[...] = acc

    out = pl.pallas_call(
        body,
        name=name,
        grid_spec=pltpu.PrefetchScalarGridSpec(
            num_scalar_prefetch=1, grid=(rows // tr,),
            in_specs=[pl.BlockSpec((1, tr, cols), lambda i, p_ref: (p_ref[0], i, 0)),
                      pl.BlockSpec((3, tr, cols), lambda i, p_ref: (0, i, 0))],
            out_specs=pl.BlockSpec((tr, cols), lambda i, p_ref: (i, 0))),
        out_shape=jax.ShapeDtypeStruct((rows, cols), F32),
        compiler_params=_cparams("parallel"),
    )(chip_idx, part.reshape(N_CHIPS, rows, cols), got.reshape(3, rows, cols))
    return out.reshape(got.shape[1:])


def _reduce_scatter(gs, name):
    x, y, c = _place()
    c_idx = c.astype(jnp.int32).reshape(1)
    chip_idx = (2 * x + y).astype(jnp.int32).reshape(1)
    gots = _pair_exchange_call(gs, name + "_pair")
    parts = [_pair_sum_call(g, got, c_idx, name + "_pair_sum") for g, got in zip(gs, gots)]
    gots = _chip_exchange_call(parts, name + "_chip")
    return [_chip_sum_call(p, got, chip_idx, name + "_chip_sum") for p, got in zip(parts, gots)]


SMALL_BLOCK_ROWS = 16


def _pack_small(parts):
    flat = jnp.concatenate([p.reshape(-1) for p in parts])
    chunk = N_DEV * SMALL_BLOCK_ROWS * COMM_LANES
    flat = jnp.pad(flat, (0, (-flat.shape[0]) % chunk))
    return flat.reshape(N_DEV, -1, COMM_LANES)


def _unpack_small(buf, shapes):
    flat = buf.reshape(-1)
    out, off = [], 0
    for shp in shapes:
        n = math.prod(shp)
        out.append(flat[off:off + n].reshape(shp))
        off += n
    return out


SHARDED = dict(w_in=2, ssm_w_glu=2, mla_w_uq=2, mla_w_ukv=2, mla_w_o=2, hg_w_o=2, w_out=1, x_w_q=1, x_w_kv=1,
               x_w_o=2, ffn_w_gate_up=2, ffn_w_down=1)
REPLICATED = ("norm_mix", "ssm_lam_re", "ssm_lam_im", "ssm_b_re", "ssm_b_im", "ssm_c_re", "ssm_c_im", "ssm_d",
              "ssm_log_step", "mla_q_norm", "mla_kv_norm", "hg_lb", "hg_g_norm", "norm_cross", "norm_mem", "norm_ffn",
              "norm_final")


def _join_shards(stacked, axis):
    n, l, a, b = stacked.shape
    if axis == 1:
        return stacked.transpose(1, 0, 2, 3).reshape(l, n * a, b)
    return stacked.transpose(1, 2, 0, 3).reshape(l, a, n * b)


def _split_shards(full, axis):
    l, a, b = full.shape
    if axis == 1:
        return full.reshape(l, N_DEV, a // N_DEV, b).transpose(1, 0, 2, 3)
    return full.reshape(l, a, N_DEV, b // N_DEV).transpose(2, 0, 1, 3)


@jax.custom_vjp
def gather_weights(shards):
    got = _all_gather_call([s.astype(BF16) for s in shards], "weights_all_gather")
    return tuple(_join_shards(p, ax) for p, ax in zip(got, SHARDED.values()))


def _gather_weights_fwd(shards):
    return gather_weights(shards), None


def _gather_weights_bwd(_, cts):
    gs = [_split_shards(ct, ax) for ct, ax in zip(cts, SHARDED.values())]
    return (tuple(_reduce_scatter(gs, "grads_reduce_scatter")),)


gather_weights.defvjp(_gather_weights_fwd, _gather_weights_bwd)


@jax.custom_vjp
def sync_replicated(params):
    return params


def _sync_replicated_bwd(shapes, cts):
    (mine,) = _reduce_scatter([_pack_small(cts)], "small_reduce_scatter")
    (total,) = _all_gather_call([mine], "small_all_gather")
    return (tuple(_unpack_small(total, shapes)),)


sync_replicated.defvjp(lambda params: (params, tuple(p.shape for p in params)), _sync_replicated_bwd)


ADAM_LR, ADAM_B1, ADAM_B2, ADAM_EPS, ADAM_WD, ADAM_STEP = 0.001, 0.9, 0.999, 1e-08, 0.01, 10


def _adamw_call(w, g, m, v, name):
    shape = w.shape
    cols = shape[-1]
    rows = math.prod(shape[:-1]) if len(shape) > 1 else 1
    tr = _pick_tile(rows, (512, 256, 128, 64, 32, 16, 8))

    def body(w_ref, g_ref, m_ref, v_ref, d_ref, nm_ref, nv_ref):
        gg = g_ref[...]
        m_new = ADAM_B1 * m_ref[...] + (1.0 - ADAM_B1) * gg
        v_new = ADAM_B2 * v_ref[...] + (1.0 - ADAM_B2) * jnp.square(gg)
        m_hat = m_new / (1.0 - ADAM_B1 ** ADAM_STEP)
        v_hat = v_new / (1.0 - ADAM_B2 ** ADAM_STEP)
        d_ref[...] = -ADAM_LR * (m_hat / (jnp.sqrt(v_hat) + ADAM_EPS) + ADAM_WD * w_ref[...])
        nm_ref[...] = m_new
        nv_ref[...] = v_new

    spec = pl.BlockSpec((tr, cols), lambda i: (i, 0))
    outs = pl.pallas_call(
        body, name=name, grid=(rows // tr,), in_specs=[spec] * 4, out_specs=[spec] * 3,
        out_shape=[jax.ShapeDtypeStruct((rows, cols), F32)] * 3, compiler_params=_cparams("parallel"),
    )(*(t.reshape(rows, cols) for t in (w, g, m, v)))
    return tuple(o.reshape(shape) for o in outs)


WEIGHTS = ("norm_mix", "w_in", "ssm_lam_re", "ssm_lam_im", "ssm_b_re", "ssm_b_im", "ssm_c_re", "ssm_c_im", "ssm_d",
           "ssm_log_step", "ssm_w_glu", "mla_q_norm", "mla_kv_norm", "mla_w_uq", "mla_w_ukv", "mla_w_o", "hg_lb",
           "hg_g_norm", "hg_w_o", "w_out", "norm_cross", "norm_mem", "x_w_q", "x_w_kv", "x_w_o", "norm_ffn",
           "ffn_w_gate_up", "ffn_w_down", "norm_final")


def kernel(x, mem, positions, norm_mix, w_in, ssm_lam_re, ssm_lam_im, ssm_b_re, ssm_b_im, ssm_c_re, ssm_c_im, ssm_d, ssm_log_step, ssm_w_glu, mla_q_norm, mla_kv_norm, mla_w_uq, mla_w_ukv, mla_w_o, hg_lb, hg_g_norm, hg_w_o, w_out, norm_cross, norm_mem, x_w_q, x_w_kv, x_w_o, norm_ffn, ffn_w_gate_up, ffn_w_down, norm_final, loss_target, m_norm_mix, m_w_in, m_ssm_lam_re, m_ssm_lam_im, m_ssm_b_re, m_ssm_b_im, m_ssm_c_re, m_ssm_c_im, m_ssm_d, m_ssm_log_step, m_ssm_w_glu, m_mla_q_norm, m_mla_kv_norm, m_mla_w_uq, m_mla_w_ukv, m_mla_w_o, m_hg_lb, m_hg_g_norm, m_hg_w_o, m_w_out, m_norm_cross, m_norm_mem, m_x_w_q, m_x_w_kv, m_x_w_o, m_norm_ffn, m_ffn_w_gate_up, m_ffn_w_down, m_norm_final, v_norm_mix, v_w_in, v_ssm_lam_re, v_ssm_lam_im, v_ssm_b_re, v_ssm_b_im, v_ssm_c_re, v_ssm_c_im, v_ssm_d, v_ssm_log_step, v_ssm_w_glu, v_mla_q_norm, v_mla_kv_norm, v_mla_w_uq, v_mla_w_ukv, v_mla_w_o, v_hg_lb, v_hg_g_norm, v_hg_w_o, v_w_out, v_norm_cross, v_norm_mem, v_x_w_q, v_x_w_kv, v_x_w_o, v_norm_ffn, v_ffn_w_gate_up, v_ffn_w_down, v_norm_final):
    given = dict(locals())
    weights = {n: given[n] for n in WEIGHTS}

    def loss_fn(xs, shards, small):
        full = dict(zip(SHARDED, gather_weights(shards)))
        sp = dict(zip(REPLICATED, sync_replicated(small)))
        return _local_loss(xs, mem[0], positions[0], loss_target[0], full, sp)

    shards = tuple(weights[n] for n in SHARDED)
    small = tuple(weights[n] for n in REPLICATED)
    loss_local, (gx, g_shards, g_small) = jax.value_and_grad(loss_fn, argnums=(0, 1, 2))(x[0], shards, small)
    loss = lax.psum(loss_local, ("x", "y", "c"))
    grads = {**dict(zip(SHARDED, g_shards)), **dict(zip(REPLICATED, g_small))}
    steps = {n: _adamw_call(weights[n], grads[n], given["m_" + n], given["v_" + n], "adamw_" + n) for n in WEIGHTS}
    return (loss, gx[None], *[grads[n] for n in WEIGHTS], *[steps[n][0] for n in WEIGHTS],
            *[steps[n][1] for n in WEIGHTS], *[steps[n][2] for n in WEIGHTS])
```

```python
import functools
import math

import jax
import jax.numpy as jnp
from jax import lax
from jax.experimental import pallas as pl
from jax.experimental.pallas import tpu as pltpu

F32 = jnp.float32
BF16 = jnp.bfloat16

VMEM_LIMIT_BYTES = 48 * 1024 * 1024
LANES = 128
SUBLANES = 8


def _cparams(*sem):
    return pltpu.CompilerParams(dimension_semantics=sem, vmem_limit_bytes=VMEM_LIMIT_BYTES)


def _pick_tile(n, cands):
    for c in cands:
        if n % c == 0:
            return c
    return n


MM_VMEM_BUDGET = 38 * 1024 * 1024
MM_STEP_US = 0.35
HBM_BYTES_PER_US = 3.0e6
VREG_RMW_PER_US = 1.5e3


def _divisor_tiles(dim, cands):
    out = [t for t in cands if dim % t == 0]
    return out or [dim]


def _mm_tiles(m, n, k, sa, sb, so):
    tms = _divisor_tiles(m, (1024, 512, 256, 128, 64, 32, 16, 8))[:2]
    tns = _divisor_tiles(n, (2048, 1536, 1408, 1024, 768, 512, 384, 256, 128))
    tks = [k // d for d in (1, 2, 4, 8, 13, 16, 26, 32, 52) if k % d == 0 and (k // d) % LANES == 0] or [k]
    best = None
    for tk in tks:
        nk = k // tk
        for tm in tms:
            for tn in tns:
                vmem = 2 * (tm * tk * sa + tk * tn * sb + tm * tn * so) + (tm * tn * 4 if nk > 1 else 0)
                if vmem > MM_VMEM_BUDGET:
                    continue
                steps = (m // tm) * (n // tn) * nk
                a_reads = m * k * sa * (n // tn if nk > 1 else 1)
                b_reads = k * n * sb * (m // tm if (nk > 1 or n // tn > 1) else 1)
                cost = (steps * MM_STEP_US + (a_reads + b_reads) / HBM_BYTES_PER_US
                        + (m * n * nk / 1024 / VREG_RMW_PER_US if nk > 1 else 0.0))
                if best is None or cost < best[0]:
                    best = (cost, tm, tn, tk)
    assert best is not None, (m, n, k)
    return best[1:]


def _mm_tiles_cached_t(m, n, k, sa, sb, so):
    for tm in _divisor_tiles(m, (1024, 512, 256, 128)):
        for tn in _divisor_tiles(n, (1024, 512, 384, 256, 128)):
            if 2 * (k * tm * sa + k * tn * sb + tm * tn * so) + tm * k * 2 <= MM_VMEM_BUDGET:
                return tm, tn
    return None


def _mm_tn_cached_call(a, b, tiles, out_dtype, name):
    k, m = a.shape
    n = b.shape[1]
    tm, tn = tiles

    def body(a_ref, b_ref, o_ref, at_ref):
        @pl.when(pl.program_id(1) == 0)
        def _():
            at_ref[...] = a_ref[...].astype(BF16).T

        o_ref[...] = lax.dot_general(at_ref[...], b_ref[...].astype(BF16), _NN_DIMS,
                                     preferred_element_type=F32).astype(out_dtype)

    return pl.pallas_call(
        body,
        name=name,
        grid=(m // tm, n // tn),
        in_specs=[pl.BlockSpec((k, tm), lambda i, j: (0, i)), pl.BlockSpec((k, tn), lambda i, j: (0, j))],
        out_specs=pl.BlockSpec((tm, tn), lambda i, j: (i, j)),
        out_shape=jax.ShapeDtypeStruct((m, n), out_dtype),
        scratch_shapes=[pltpu.VMEM((tm, k), BF16)],
        compiler_params=_cparams("parallel", "arbitrary"),
    )(a, b)


_NN_DIMS = (((1,), (0,)), ((), ()))


def _mm_call(a, b, ta, tb, add=None, out_dtype=F32, name="mm"):
    m, k = (a.shape[1], a.shape[0]) if ta else a.shape
    k2, n = (b.shape[1], b.shape[0]) if tb else b.shape
    assert k == k2, (a.shape, b.shape, ta, tb)
    sizes = (a.dtype.itemsize, b.dtype.itemsize, jnp.dtype(out_dtype).itemsize + (add.dtype.itemsize if add is not None else 0))
    if ta:
        tiles = _mm_tiles_cached_t(m, n, k, *sizes)
        if tiles is not None:
            return _mm_tn_cached_call(a, b, tiles, out_dtype, name)
    tm, tn, tk = _mm_tiles(m, n, k, *sizes)
    nk = k // tk
    a_spec = pl.BlockSpec((tk, tm), lambda i, j, kk: (kk, i)) if ta else pl.BlockSpec((tm, tk), lambda i, j, kk: (i, kk))
    b_spec = pl.BlockSpec((tn, tk), lambda i, j, kk: (j, kk)) if tb else pl.BlockSpec((tk, tn), lambda i, j, kk: (kk, j))
    o_spec = pl.BlockSpec((tm, tn), lambda i, j, kk: (i, j))
    dn = (((0 if ta else 1,), (1 if tb else 0,)), ((), ()))
    has_add = add is not None

    def body(*refs):
        a_ref, b_ref = refs[0], refs[1]
        c_ref = refs[2] if has_add else None
        o_ref = refs[3] if has_add else refs[2]
        p = lax.dot_general(a_ref[...].astype(BF16), b_ref[...].astype(BF16), dn, preferred_element_type=F32)

        def finish(r):
            if has_add:
                r = r + c_ref[...].astype(F32)
            o_ref[...] = r.astype(out_dtype)

        if nk == 1:
            finish(p)
        else:
            acc_ref = refs[-1]
            kk = pl.program_id(2)

            @pl.when(kk == 0)
            def _():
                acc_ref[...] = p

            @pl.when(kk > 0)
            def _():
                acc_ref[...] += p

            @pl.when(kk == nk - 1)
            def _():
                finish(acc_ref[...])

    in_specs = [a_spec, b_spec] + ([o_spec] if has_add else [])
    args = (a, b) + ((add,) if has_add else ())
    return pl.pallas_call(
        body,
        name=name,
        grid=(m // tm, n // tn, nk),
        in_specs=in_specs,
        out_specs=o_spec,
        out_shape=jax.ShapeDtypeStruct((m, n), out_dtype),
        scratch_shapes=[] if nk == 1 else [pltpu.VMEM((tm, tn), F32)],
        compiler_params=_cparams("parallel", "parallel", "arbitrary"),
    )(*args)


@functools.partial(jax.custom_vjp, nondiff_argnums=(2,))
def matmul(a, b, out_dtype=F32):
    return _mm_call(a, b, False, False, out_dtype=out_dtype, name="mm_fwd")


def _matmul_fwd(a, b, out_dtype):
    return matmul(a, b, out_dtype), (a, b)


def _matmul_bwd(out_dtype, res, g):
    a, b = res
    da = _mm_call(g, b, False, True, out_dtype=a.dtype, name="mm_da")
    db = _mm_call(a, g, True, False, out_dtype=b.dtype, name="mm_db")
    return da, db


matmul.defvjp(_matmul_fwd, _matmul_bwd)


@jax.custom_vjp
def matmul_add(a, b, c):
    return _mm_call(a, b, False, False, add=c, name="mm_add_fwd")


def _matmul_add_fwd(a, b, c):
    return _mm_call(a, b, False, False, add=c, name="mm_add_fwd"), (a, b)


def _matmul_add_bwd(res, g):
    a, b = res
    da = _mm_call(g, b, False, True, out_dtype=a.dtype, name="mm_da")
    db = _mm_call(a, g, True, False, out_dtype=b.dtype, name="mm_db")
    return da, db, g


matmul_add.defvjp(_matmul_add_fwd, _matmul_add_bwd)


def rowwise(f, n_rows, n_aux, tile, name, passthrough=False):
    def specs(arrs, tiled):
        out = []
        for x in arrs:
            if tiled:
                out.append(pl.BlockSpec((tile, x.shape[1]), lambda i: (i, 0)))
            else:
                out.append(pl.BlockSpec(x.shape, lambda i: (0, 0)))
        return out

    def tile_structs(args):
        rows_aux, params = args[: n_rows + n_aux], args[n_rows + n_aux:]
        return [jax.ShapeDtypeStruct((tile, x.shape[1]), x.dtype) for x in rows_aux] + [
            jax.ShapeDtypeStruct(p.shape, p.dtype) for p in params]

    def fwd_call(*args):
        s = args[0].shape[0]
        outs = jax.eval_shape(f, *tile_structs(args))
        n_in = len(args)

        def body(*refs):
            vals = [r[...] for r in refs[:n_in]]
            res = f(*vals)
            for o_ref, r in zip(refs[n_in:], res):
                o_ref[...] = r.astype(o_ref.dtype)

        return pl.pallas_call(
            body,
            name=name + "_fwd",
            grid=(s // tile,),
            in_specs=specs(args[: n_rows + n_aux], True) + specs(args[n_rows + n_aux:], False),
            out_specs=[pl.BlockSpec((tile, o.shape[1]), lambda i: (i, 0)) for o in outs],
            out_shape=[jax.ShapeDtypeStruct((s, o.shape[1]), o.dtype) for o in outs],
            compiler_params=_cparams("parallel"),
        )(*args)

    def bwd_call(args, gs):
        s = args[0].shape[0]
        rows, aux, params = args[:n_rows], args[n_rows:n_rows + n_aux], args[n_rows + n_aux:]
        n_in, n_g, n_p = len(args), len(gs), len(params)
        n_gf = n_g - 1 if passthrough else n_g

        def body(*refs):
            vals = [r[...] for r in refs[:n_in]]
            gvals = tuple(r[...] for r in refs[n_in:n_in + n_gf])
            out_refs = refs[n_in + n_g:]
            auxv = vals[n_rows:n_rows + n_aux]

            def g_(*rp):
                return tuple(f(*rp[:n_rows], *auxv, *rp[n_rows:]))

            _, vjp = jax.vjp(g_, *vals[:n_rows], *vals[n_rows + n_aux:])
            cts = list(vjp(gvals))
            if passthrough:
                cts[0] = cts[0] + refs[n_in + n_gf][...]
            for o_ref, ct in zip(out_refs[:n_rows], cts[:n_rows]):
                o_ref[...] = ct.astype(o_ref.dtype)
            if n_p:
                @pl.when(pl.program_id(0) == 0)
                def _():
                    for o_ref in out_refs[n_rows:]:
                        o_ref[...] = jnp.zeros_like(o_ref)

                for o_ref, ct in zip(out_refs[n_rows:], cts[n_rows:]):
                    o_ref[...] += ct.astype(o_ref.dtype)

        return pl.pallas_call(
            body,
            name=name + "_bwd",
            grid=(s // tile,),
            in_specs=specs(rows + aux, True) + specs(params, False) + specs(gs, True),
            out_specs=specs(rows, True) + specs(params, False),
            out_shape=[jax.ShapeDtypeStruct(x.shape, x.dtype) for x in rows + params],
            compiler_params=_cparams("arbitrary" if n_p else "parallel"),
        )(*args, *gs)

    @jax.custom_vjp
    def op(*args):
        return tuple(fwd_call(*args)) + ((args[0],) if passthrough else ())

    def op_fwd(*args):
        return op(*args), args

    def op_bwd(args, gs):
        cts = bwd_call(tuple(args), tuple(gs))
        rows_ct, par_ct = cts[:n_rows], cts[n_rows:]
        aux_ct = [jnp.zeros_like(a) for a in args[n_rows:n_rows + n_aux]]
        return tuple(rows_ct) + tuple(aux_ct) + tuple(par_ct)

    op.defvjp(op_fwd, op_bwd)
    return op


SCAN_SEGMENTS = SUBLANES
SCAN_TILE_ROWS = 512
SCAN_TILE_LANES = 512


def _scan_specs(s, n):
    tr = min(SCAN_TILE_ROWS, s)
    tn = min(SCAN_TILE_LANES, n)
    return tr, tn, s // tr, n // tn


def _scan_step(ar, ai, xr, xi, br, bi):
    return ar * xr - ai * xi + br, ar * xi + ai * xr + bi


def _scan_finals(b_r, b_i, a, reverse, name):
    s, n = b_r.shape
    tr, tn, nt, nc = _scan_specs(s, n)
    ti = tr // SUBLANES
    tmap = (lambda c, j: (nt - 1 - j, c)) if reverse else (lambda c, j: (j, c))

    def body(br_ref, bi_ref, a_ref, fr_ref, fi_ref, sr, si):
        j = pl.program_id(1)

        @pl.when(j == 0)
        def _():
            sr[...] = jnp.zeros_like(sr)
            si[...] = jnp.zeros_like(si)

        ar = jnp.broadcast_to(a_ref[0:1, :], (SUBLANES, tn))
        ai = jnp.broadcast_to(a_ref[1:2, :], (SUBLANES, tn))

        def step(ii, carry):
            i = (ti - 1 - ii) if reverse else ii
            off = pl.multiple_of(i * SUBLANES, SUBLANES)
            return _scan_step(ar, ai, carry[0], carry[1], br_ref[pl.ds(off, SUBLANES), :], bi_ref[pl.ds(off, SUBLANES), :])

        xr, xi = lax.fori_loop(0, ti, step, (sr[...], si[...]), unroll=4)
        sr[...] = xr
        si[...] = xi

        @pl.when(j == nt - 1)
        def _():
            fr_ref[...] = xr
            fi_ref[...] = xi

    bspec = pl.BlockSpec((tr, tn), tmap)
    fspec = pl.BlockSpec((SUBLANES, tn), lambda c, j: (0, c))
    return pl.pallas_call(
        body,
        name=name,
        grid=(nc, nt),
        in_specs=[bspec, bspec, pl.BlockSpec((2, tn), lambda c, j: (0, c))],
        out_specs=[fspec, fspec],
        out_shape=[jax.ShapeDtypeStruct((SUBLANES, n), F32)] * 2,
        scratch_shapes=[pltpu.VMEM((SUBLANES, tn), F32)] * 2,
        compiler_params=_cparams("parallel", "arbitrary"),
    )(b_r, b_i, a)


def _scan_states(b_r, b_i, a, f_r, f_i, reverse, xs, name):
    s, n = b_r.shape
    tr, tn, nt, nc = _scan_specs(s, n)
    ti = tr // SUBLANES
    seg_len = s // SCAN_SEGMENTS
    assert seg_len & (seg_len - 1) == 0
    with_acc = xs is not None
    tmap = (lambda c, j: (nt - 1 - j, c)) if reverse else (lambda c, j: (j, c))
    order = list(range(SCAN_SEGMENTS))[::-1] if reverse else list(range(SCAN_SEGMENTS))

    def body(*refs):
        br_ref, bi_ref, a_ref, fr_ref, fi_ref = refs[:5]
        pos = 5
        if with_acc:
            xr_ref, xi_ref = refs[5:7]
            pos = 7
        or_ref, oi_ref = refs[pos:pos + 2]
        pos += 2
        if with_acc:
            dr_ref, di_ref = refs[pos:pos + 2]
            pos += 2
        sr, si = refs[pos:pos + 2]
        if with_acc:
            accr, acci = refs[pos + 2:pos + 4]
        j = pl.program_id(1)
        a_r1, a_i1 = a_ref[0:1, :], a_ref[1:2, :]

        @pl.when(j == 0)
        def _():
            pr, pi = a_r1, a_i1
            for _ in range(seg_len.bit_length() - 1):
                pr, pi = pr * pr - pi * pi, 2.0 * pr * pi
            cr = jnp.zeros((1, tn), F32)
            ci = jnp.zeros((1, tn), F32)
            for idx, k in enumerate(order):
                if idx > 0:
                    kp = order[idx - 1]
                    cr, ci = (fr_ref[kp:kp + 1, :] + pr * cr - pi * ci, fi_ref[kp:kp + 1, :] + pr * ci + pi * cr)
                sr[k:k + 1, :] = cr
                si[k:k + 1, :] = ci
            if with_acc:
                accr[...] = jnp.zeros_like(accr)
                acci[...] = jnp.zeros_like(acci)

        ar = jnp.broadcast_to(a_r1, (SUBLANES, tn))
        ai = jnp.broadcast_to(a_i1, (SUBLANES, tn))

        def step(ii, carry):
            i = (ti - 1 - ii) if reverse else ii
            off = pl.multiple_of(i * SUBLANES, SUBLANES)
            rows = pl.ds(off, SUBLANES)
            xr, xi = carry[0], carry[1]
            if with_acc:
                zr, zi = xr_ref[rows, :], xi_ref[rows, :]
                acc = (carry[2] + xr * zr + xi * zi, carry[3] + xi * zr - xr * zi)
            nr, ni = _scan_step(ar, ai, xr, xi, br_ref[rows, :], bi_ref[rows, :])
            or_ref[rows, :] = nr
            oi_ref[rows, :] = ni
            return (nr, ni) + (acc if with_acc else ())

        init = (sr[...], si[...]) + ((accr[...], acci[...]) if with_acc else ())
        out = lax.fori_loop(0, ti, step, init, unroll=4)
        sr[...] = out[0]
        si[...] = out[1]
        if with_acc:
            accr[...] = out[2]
            acci[...] = out[3]

            @pl.when(j == nt - 1)
            def _():
                dr_ref[...] = jnp.sum(out[2], axis=0, keepdims=True)
                di_ref[...] = jnp.sum(out[3], axis=0, keepdims=True)

    bspec = pl.BlockSpec((tr, tn), tmap)
    fspec = pl.BlockSpec((SUBLANES, tn), lambda c, j: (0, c))
    dspec = pl.BlockSpec((1, tn), lambda c, j: (0, c))
    in_specs = [bspec, bspec, pl.BlockSpec((2, tn), lambda c, j: (0, c)), fspec, fspec] + ([bspec, bspec] if with_acc else [])
    out_specs = [bspec, bspec] + ([dspec, dspec] if with_acc else [])
    out_shape = [jax.ShapeDtypeStruct((s, n), F32)] * 2 + ([jax.ShapeDtypeStruct((1, n), F32)] * 2 if with_acc else [])
    scratch = [pltpu.VMEM((SUBLANES, tn), F32)] * (4 if with_acc else 2)
    args = (b_r, b_i, a, f_r, f_i) + (tuple(xs) if with_acc else ())
    return pl.pallas_call(
        body,
        name=name,
        grid=(nc, nt),
        in_specs=in_specs,
        out_specs=out_specs,
        out_shape=out_shape,
        scratch_shapes=scratch,
        compiler_params=_cparams("parallel", "arbitrary"),
    )(*args)


@jax.custom_vjp
def s5_scan(b_r, b_i, a):
    f_r, f_i = _scan_finals(b_r, b_i, a, False, "s5_scan_fin")
    return tuple(_scan_states(b_r, b_i, a, f_r, f_i, False, None, "s5_scan"))


def _s5_scan_fwd(b_r, b_i, a):
    xs = s5_scan(b_r, b_i, a)
    return xs, (a, xs)


def _s5_scan_bwd(res, g):
    a, xs = res
    a_conj = a * jnp.array([[1.0], [-1.0]], F32)
    f_r, f_i = _scan_finals(g[0], g[1], a_conj, True, "s5_rscan_fin")
    g_r, g_i, da_r, da_i = _scan_states(g[0], g[1], a_conj, f_r, f_i, True, xs, "s5_rscan")
    return g_r, g_i, jnp.concatenate([da_r, da_i], axis=0)


s5_scan.defvjp(_s5_scan_fwd, _s5_scan_bwd)


BD_SEG_ROWS = 128


def _bd_view(x, seg):
    return x if seg else x.reshape(SCAN_SEGMENTS, x.shape[0] // SCAN_SEGMENTS, x.shape[1])


def _bd_spec(seg, w, rows):
    if seg:
        return pl.BlockSpec((SCAN_SEGMENTS * rows, w), lambda i, j: (i, j))
    return pl.BlockSpec((SCAN_SEGMENTS, rows, w), lambda i, j: (0, i, j))


def _bd_apply_call(a, w, transpose_w, a_seg, out_seg, add, name):
    s = a.shape[0]
    nb, ka, kn = w.shape
    wi, wo = (kn, ka) if transpose_w else (ka, kn)
    convert = a_seg != out_seg
    a_seg_k, out_seg_k = (a_seg, out_seg) if convert else (True, True)
    ov_shape = (s, nb * wo) if out_seg_k else (SCAN_SEGMENTS, s // SCAN_SEGMENTS, nb * wo)
    dn = _NT_DIMS if transpose_w else _NN_DIMS
    has_add = add is not None
    rows = min(BD_SEG_ROWS, s // SCAN_SEGMENTS)

    assert not convert or (wo if a_seg_k else wi) == LANES

    def body(*refs):
        a_ref, w_ref = refs[0], refs[1]
        c_ref = refs[2] if has_add else None
        o_ref = refs[3] if has_add else refs[2]
        wb = w_ref[0].astype(BF16)
        mm = lambda a_val: lax.dot_general(a_val.astype(BF16), wb, dn, preferred_element_type=F32)
        if not convert:
            r = mm(a_ref[...])
            o_ref[...] = r + c_ref[...] if has_add else r
        elif out_seg_k:
            scr = refs[-1]
            for k in range(SCAN_SEGMENTS):
                scr[pl.ds(k, rows, stride=SCAN_SEGMENTS), :] = a_ref[k]
            r = mm(scr[...])
            o_ref[...] = r + c_ref[...] if has_add else r
        else:
            scr = refs[-1]
            scr[...] = mm(a_ref[...])
            for k in range(SCAN_SEGMENTS):
                r = scr[pl.ds(k, rows, stride=SCAN_SEGMENTS), :]
                o_ref[k] = r + c_ref[k] if has_add else r

    args = [_bd_view(a, a_seg_k), w] + ([_bd_view(add, out_seg_k)] if has_add else [])
    in_specs = ([_bd_spec(a_seg_k, wi, rows), pl.BlockSpec((1, ka, kn), lambda i, j: (j, 0, 0))]
                + ([_bd_spec(out_seg_k, wo, rows)] if has_add else []))
    out = pl.pallas_call(
        body,
        name=name,
        grid=(s // (SCAN_SEGMENTS * rows), nb),
        in_specs=in_specs,
        out_specs=_bd_spec(out_seg_k, wo, rows),
        out_shape=jax.ShapeDtypeStruct(ov_shape, F32),
        scratch_shapes=[pltpu.VMEM((SCAN_SEGMENTS * rows, LANES), F32)] if convert else [],
        compiler_params=_cparams("parallel", "parallel"),
    )(*args)
    return out.reshape(s, nb * wo)


def _bd_weight_grad_call(a, g, a_seg, g_seg, nb, name):
    s = a.shape[0]
    ka, kn = a.shape[1] // nb, g.shape[1] // nb
    seg_len = s // SCAN_SEGMENTS
    convert = a_seg != g_seg

    def view(x, seg, w):
        if not convert:
            return x, pl.BlockSpec((s, w), lambda j: (0, j))
        if seg:
            return x, pl.BlockSpec((s, w), lambda j: (0, j))
        return x.reshape(SCAN_SEGMENTS, seg_len, x.shape[1]), pl.BlockSpec((SCAN_SEGMENTS, seg_len, w), lambda j: (0, 0, j))

    av, a_spec = view(a, a_seg, ka)
    gv, g_spec = view(g, g_seg, kn)

    assert not convert or (kn if a_seg else ka) == LANES

    def body(a_ref, g_ref, o_ref, *scratch):
        tn = lambda x, y: lax.dot_general(x.astype(BF16), y.astype(BF16), _TN_DIMS, preferred_element_type=F32)
        if not convert:
            o_ref[0] = tn(a_ref[...], g_ref[...])
        else:
            scr = scratch[0]
            t_ref = g_ref if a_seg else a_ref
            for k in range(SCAN_SEGMENTS):
                scr[pl.ds(k, seg_len, stride=SCAN_SEGMENTS), :] = t_ref[k]
            o_ref[0] = tn(a_ref[...], scr[...]) if a_seg else tn(scr[...], g_ref[...])

    return pl.pallas_call(
        body,
        name=name,
        grid=(nb,),
        in_specs=[a_spec, g_spec],
        out_specs=pl.BlockSpec((1, ka, kn), lambda j: (j, 0, 0)),
        out_shape=jax.ShapeDtypeStruct((nb, ka, kn), F32),
        scratch_shapes=[pltpu.VMEM((s, LANES), F32)] if convert else [],
        compiler_params=_cparams("parallel"),
    )(av, gv)


_NT_DIMS = (((1,), (1,)), ((), ()))
_TN_DIMS = (((0,), (0,)), ((), ()))


@functools.partial(jax.custom_vjp, nondiff_argnums=(2, 3))
def bd_matmul(a, w, a_seg, out_seg):
    return _bd_apply_call(a, w, False, a_seg, out_seg, None, "bd_mm_fwd")


def _bd_matmul_fwd(a, w, a_seg, out_seg):
    return bd_matmul(a, w, a_seg, out_seg), (a, w)


def _bd_matmul_bwd(a_seg, out_seg, res, g):
    a, w = res
    da = _bd_apply_call(g, w, True, out_seg, a_seg, None, "bd_mm_da")
    dw = _bd_weight_grad_call(a, g, a_seg, out_seg, w.shape[0], "bd_mm_dw")
    return da, dw


bd_matmul.defvjp(_bd_matmul_fwd, _bd_matmul_bwd)


@functools.partial(jax.custom_vjp, nondiff_argnums=(3, 4))
def bd_matmul_add(a, w, c, a_seg, out_seg):
    return _bd_apply_call(a, w, False, a_seg, out_seg, c, "bd_mm_add_fwd")


def _bd_matmul_add_fwd(a, w, c, a_seg, out_seg):
    return bd_matmul_add(a, w, c, a_seg, out_seg), (a, w)


def _bd_matmul_add_bwd(a_seg, out_seg, res, g):
    return _bd_matmul_bwd(a_seg, out_seg, res, g) + (g,)


bd_matmul_add.defvjp(_bd_matmul_add_fwd, _bd_matmul_add_bwd)


_NN = (((1,), (0,)), ((), ()))
_NT = (((1,), (1,)), ((), ()))
_TN = (((0,), (0,)), ((), ()))


def _dot(a, b, dn):
    return lax.dot_general(a.astype(BF16), b.astype(BF16), dn, preferred_element_type=F32)


@jax.custom_vjp
def bdot_nn(a, b):
    return _dot(a, b, _NN)


bdot_nn.defvjp(lambda a, b: (_dot(a, b, _NN), (a, b)),
               lambda r, g: (_dot(g, r[1], _NT).astype(r[0].dtype), _dot(r[0], g, _TN).astype(r[1].dtype)))


@jax.custom_vjp
def bdot_nt(a, b):
    return _dot(a, b, _NT)


bdot_nt.defvjp(lambda a, b: (_dot(a, b, _NT), (a, b)),
               lambda r, g: (_dot(g, r[1], _NN).astype(r[0].dtype), _dot(g, r[0], _TN).astype(r[1].dtype)))


@jax.custom_vjp
def bdot_tn(a, b):
    return _dot(a, b, _TN)


bdot_tn.defvjp(lambda a, b: (_dot(a, b, _TN), (a, b)),
               lambda r, g: (_dot(r[1], g, _NT).astype(r[0].dtype), _dot(r[0], g, _NN).astype(r[1].dtype)))


def _split3(x):
    h = x.astype(BF16)
    r = x - h.astype(F32)
    m = r.astype(BF16)
    l = (r - m.astype(F32)).astype(BF16)
    return h, m, l


def _exact_dot(t, x, dn):
    h, m, l = _split3(x)
    d = lambda p: lax.dot_general(t, p, dn, preferred_element_type=F32)
    return d(h) + d(m) + d(l)


@jax.custom_vjp
def select_dot(t, x):
    return _exact_dot(t, x, _NN)


select_dot.defvjp(lambda t, x: (_exact_dot(t, x, _NN), t),
                  lambda t, g: (jnp.zeros_like(t), _exact_dot(t, g, _TN)))


def _split_rows_impl(x, h):
    return tuple(x[i * h:(i + 1) * h] for i in range(x.shape[0] // h))


@functools.partial(jax.custom_vjp, nondiff_argnums=(1,))
def split_rows(x, h):
    return _split_rows_impl(x, h)


split_rows.defvjp(lambda x, h: (_split_rows_impl(x, h), None),
                  lambda h, r, g: (jnp.concatenate(g, axis=0),))


@jax.custom_vjp
def join_rows(parts):
    return jnp.concatenate(parts, axis=0)


def _join_rows_bwd(hs, g):
    out, off = [], 0
    for h in hs:
        out.append(g[off:off + h])
        off += h
    return (tuple(out),)


join_rows.defvjp(lambda parts: (jnp.concatenate(parts, axis=0), tuple(p.shape[0] for p in parts)), _join_rows_bwd)


def _split_lanes_impl(x, w):
    return tuple(x[:, i * w:(i + 1) * w] for i in range(x.shape[1] // w))


@functools.partial(jax.custom_vjp, nondiff_argnums=(1,))
def split_lanes(x, w):
    return _split_lanes_impl(x, w)


split_lanes.defvjp(lambda x, w: (_split_lanes_impl(x, w), None),
                   lambda w, r, g: (jnp.concatenate(g, axis=1),))


def _join_impl(parts):
    return jnp.concatenate(parts, axis=1)


@jax.custom_vjp
def join_lanes(parts):
    return _join_impl(parts)


def _join_bwd(ws, g):
    out, off = [], 0
    for w in ws:
        out.append(g[:, off:off + w])
        off += w
    return (tuple(out),)


join_lanes.defvjp(lambda parts: (_join_impl(parts), tuple(p.shape[1] for p in parts)), _join_bwd)


def _rope_impl(x, c, sa, sb, shift):
    w = x.shape[1]
    return x * c + pltpu.roll(x, w - shift, 1) * sa + pltpu.roll(x, shift, 1) * sb


@functools.partial(jax.custom_vjp, nondiff_argnums=(4,))
def rope_lanes(x, c, sa, sb, shift):
    return _rope_impl(x, c, sa, sb, shift)


def _rope_bwd(shift, r, g):
    c, sa, sb = r
    w = g.shape[1]
    dx = g * c + pltpu.roll(g * sa, shift, 1) + pltpu.roll(g * sb, w - shift, 1)
    return dx, jnp.zeros_like(c), jnp.zeros_like(sa), jnp.zeros_like(sb)


rope_lanes.defvjp(lambda x, c, sa, sb, shift: (_rope_impl(x, c, sa, sb, shift), (c, sa, sb)), _rope_bwd)


RMS_EPS = 1e-6


def _rms(x, g):
    return x * lax.rsqrt(jnp.mean(x * x, axis=-1, keepdims=True) + RMS_EPS) * g


ATTN_BLOCK = 512
MASK_VALUE = -1e30
LOG2E = math.log2(math.e)
LN2 = math.log(2.0)
V_ONES_LANE = 64


def _causal_mask(t):
    r = lax.broadcasted_iota(jnp.int32, (t, t), 0)
    c = lax.broadcasted_iota(jnp.int32, (t, t), 1)
    return c <= r


def _attn_fwd_call(q, k, v):
    s, width = q.shape
    n_heads = width // LANES
    tq = min(ATTN_BLOCK, s)
    nq = s // tq

    def body(q_ref, k_ref, v_ref, o_ref, lse_ref):
        i = pl.program_id(1)
        qb = q_ref[...].astype(BF16)
        ones_lane = lax.broadcasted_iota(jnp.int32, (tq, LANES), 1) == V_ONES_LANE

        def block(kb, carry, masked):
            m, acc = carry
            rows = pl.ds(pl.multiple_of(kb * tq, tq), tq)
            sc = lax.dot_general(qb, k_ref[rows, :].astype(BF16), _NT, preferred_element_type=F32)
            if masked:
                sc = jnp.where(_causal_mask(tq), sc, MASK_VALUE)
            m_new = jnp.maximum(m, jnp.max(sc, axis=-1, keepdims=True))
            p = jnp.exp2(sc - m_new).astype(BF16)
            vb = jnp.where(ones_lane, 1.0, v_ref[rows, :]).astype(BF16)
            acc = jnp.exp2(m - m_new) * acc + lax.dot_general(p, vb, _NN, preferred_element_type=F32)
            return m_new, acc

        init = (jnp.full((tq, 1), MASK_VALUE, F32), jnp.zeros((tq, LANES), F32))
        carry = lax.fori_loop(0, i, lambda kb, c: block(kb, c, False), init)
        m, acc = block(i, carry, True)
        l = jnp.sum(jnp.where(ones_lane, acc, 0.0), axis=-1, keepdims=True)
        o_ref[...] = jnp.where(ones_lane, 0.0, acc / l).astype(o_ref.dtype)
        lse_ref[...] = jnp.broadcast_to(m + jnp.log2(l), (tq, LANES))

    qspec = pl.BlockSpec((tq, LANES), lambda h, i: (i, h))
    kspec = pl.BlockSpec((s, LANES), lambda h, i: (0, h))
    return pl.pallas_call(
        body,
        name="mla_attn_fwd",
        grid=(n_heads, nq),
        in_specs=[qspec, kspec, kspec],
        out_specs=[qspec, qspec],
        out_shape=[jax.ShapeDtypeStruct((s, width), BF16), jax.ShapeDtypeStruct((s, width), F32)],
        compiler_params=_cparams("parallel", "parallel"),
    )(q, k, v)


def _attn_bwd_call(q, k, v, o, lse, do):
    s, width = q.shape
    n_heads = width // LANES
    tq = min(ATTN_BLOCK, s)
    nq = s // tq

    def body(q_ref, k_ref, v_ref, o_ref, lse_ref, do_ref, dq_ref, dk_ref, dv_ref, dq_acc):
        j = pl.program_id(1)

        @pl.when(j == 0)
        def _():
            dq_acc[...] = jnp.zeros_like(dq_acc)

        kb = k_ref[...].astype(BF16)
        vb = v_ref[...].astype(BF16)

        def block(i, carry, masked):
            dk, dv = carry
            rows = pl.ds(pl.multiple_of(i * tq, tq), tq)
            qi = q_ref[rows, :].astype(BF16)
            doi = do_ref[rows, :].astype(F32)
            delta = jnp.sum(doi * o_ref[rows, :].astype(F32), axis=-1, keepdims=True)
            sc = lax.dot_general(qi, kb, _NT, preferred_element_type=F32)
            if masked:
                sc = jnp.where(_causal_mask(tq), sc, MASK_VALUE)
            p = jnp.exp2(sc - lse_ref[rows, 0:1])
            dob = doi.astype(BF16)
            dv = dv + lax.dot_general(p.astype(BF16), dob, _TN, preferred_element_type=F32)
            dp = lax.dot_general(dob, vb, _NT, preferred_element_type=F32)
            ds = (p * (dp - delta)).astype(BF16)
            dq_acc[rows, :] += lax.dot_general(ds, kb, _NN, preferred_element_type=F32)
            dk = dk + lax.dot_general(ds, qi, _TN, preferred_element_type=F32)
            return dk, dv

        zero = jnp.zeros((tq, LANES), F32)
        carry = block(j, (zero, zero), True)
        dk, dv = lax.fori_loop(j + 1, nq, lambda i, c: block(i, c, False), carry)
        dk_ref[...] = (dk * LN2).astype(dk_ref.dtype)
        dv_ref[...] = dv.astype(dv_ref.dtype)

        @pl.when(j == nq - 1)
        def _():
            dq_ref[...] = (dq_acc[...] * LN2).astype(dq_ref.dtype)

    full = pl.BlockSpec((s, LANES), lambda h, j: (0, h))
    blk = pl.BlockSpec((tq, LANES), lambda h, j: (j, h))
    return pl.pallas_call(
        body,
        name="mla_attn_bwd",
        grid=(n_heads, nq),
        in_specs=[full, blk, blk, full, full, full],
        out_specs=[full, blk, blk],
        out_shape=[jax.ShapeDtypeStruct((s, width), t.dtype) for t in (q, k, v)],
        scratch_shapes=[pltpu.VMEM((s, LANES), F32)],
        compiler_params=_cparams("parallel", "arbitrary"),
    )(q, k, v, o, lse, do)


@jax.custom_vjp
def causal_attention(q, k, v):
    return _attn_fwd_call(q, k, v)[0]


def _causal_attention_fwd(q, k, v):
    o, lse = _attn_fwd_call(q, k, v)
    return o, (q, k, v, o, lse)


def _causal_attention_bwd(res, do):
    return tuple(_attn_bwd_call(*res, do))


causal_attention.defvjp(_causal_attention_fwd, _causal_attention_bwd)


HG_HEADS = 4
HG_CHUNK = 32
HG_REF_ROW = HG_CHUNK // 2 - 1
HG_TILE_ROWS = 256
HG_EXP_CLAMP = 80.0


def _hg_tile_masks(t):
    shift = HG_CHUNK.bit_length() - 1
    r = lax.broadcasted_iota(jnp.int32, (t, t), 0)
    c = lax.broadcasted_iota(jnp.int32, (t, t), 1)
    start = lax.shift_left(lax.shift_right_logical(r, shift), shift)
    causal = (c >= start) & (c <= r)
    return causal, c == start + HG_REF_ROW, c == start + (HG_CHUNK - 1)


def _hg_tile(q, fl, v, lb, st):
    t = q.shape[0]
    causal, ref_sel, last_sel = _hg_tile_masks(t)
    f = lb + (1.0 - lb) * jax.nn.sigmoid(fl)
    kk = 1.0 - f
    qs = q * jax.nn.sigmoid(q)
    b = select_dot(causal.astype(BF16), jnp.log(f))
    b_ref = select_dot(ref_sel.astype(BF16), b)
    b_last = select_dot(last_sel.astype(BF16), b)
    q_in = qs * jnp.exp(jnp.minimum(b - b_ref, HG_EXP_CLAMP))
    k_in = kk * jnp.exp(jnp.minimum(b_ref - b, HG_EXP_CLAMP))
    o = bdot_nn(jnp.where(causal, bdot_nt(q_in, k_in), 0.0), v)
    q_hat = split_rows(qs * jnp.exp(b), HG_CHUNK)
    k_hat = split_rows(kk * jnp.exp(b_last - b), HG_CHUNK)
    decay = split_rows(jnp.exp(b_last), HG_CHUNK)
    vs = split_rows(v, HG_CHUNK)
    first_row = lax.broadcasted_iota(jnp.int32, (HG_CHUNK, LANES), 0) == 0
    inter = []
    for c in range(t // HG_CHUNK):
        inter.append(bdot_nt(q_hat[c], st))
        st = st * jnp.sum(jnp.where(first_row, decay[c], 0.0), axis=0, keepdims=True) + bdot_tn(vs[c], k_hat[c])
    return o + join_rows(tuple(inter)), st


def _hg_head(q, fl, v, gate, lb, gn, st):
    o, st = _hg_tile(q, fl, v, lb, st)
    return _rms(o, gn) * (gate * jax.nn.sigmoid(gate)), st


HG_PARTS = 4


def _hg_part_slices(h, width):
    return [slice(p * width + h * LANES, p * width + (h + 1) * LANES) for p in range(HG_PARTS)]


def _hg_fwd_call(x, lb, gn):
    s = x.shape[0]
    width = x.shape[1] // HG_PARTS
    tr = min(HG_TILE_ROWS, s)
    nt = s // tr

    def body(x_ref, lb_ref, gn_ref, o_ref, sts_ref, st_ref):
        @pl.when(pl.program_id(0) == 0)
        def _():
            st_ref[...] = jnp.zeros_like(st_ref)

        for h in range(HG_HEADS):
            ln = slice(h * LANES, (h + 1) * LANES)
            st = st_ref[h]
            sts_ref[0, h] = st
            o, st_new = _hg_head(*(x_ref[:, sl] for sl in _hg_part_slices(h, width)), lb_ref[:, ln], gn_ref[...], st)
            o_ref[:, ln] = o.astype(o_ref.dtype)
            st_ref[h] = st_new

    const = lambda shape: pl.BlockSpec(shape, lambda j: (0, 0))
    return pl.pallas_call(
        body,
        name="hgrn2_fwd",
        grid=(nt,),
        in_specs=[pl.BlockSpec((tr, HG_PARTS * width), lambda j: (j, 0)), const((1, width)), const((1, LANES))],
        out_specs=[pl.BlockSpec((tr, width), lambda j: (j, 0)),
                   pl.BlockSpec((1, HG_HEADS, LANES, LANES), lambda j: (j, 0, 0, 0))],
        out_shape=[jax.ShapeDtypeStruct((s, width), BF16),
                   jax.ShapeDtypeStruct((nt, HG_HEADS, LANES, LANES), F32)],
        scratch_shapes=[pltpu.VMEM((HG_HEADS, LANES, LANES), F32)],
        compiler_params=_cparams("arbitrary"),
    )(x, lb, gn)


def _hg_bwd_call(x, lb, gn, sts, do):
    s = x.shape[0]
    width = x.shape[1] // HG_PARTS
    tr = min(HG_TILE_ROWS, s)
    nt = s // tr

    def body(x_ref, lb_ref, gn_ref, sts_ref, do_ref, dx_ref, dlb_ref, dgn_ref, dst_ref):
        @pl.when(pl.program_id(0) == 0)
        def _():
            dst_ref[...] = jnp.zeros_like(dst_ref)
            dlb_ref[...] = jnp.zeros_like(dlb_ref)
            dgn_ref[...] = jnp.zeros_like(dgn_ref)

        for h in range(HG_HEADS):
            ln = slice(h * LANES, (h + 1) * LANES)
            parts = _hg_part_slices(h, width)
            _, vjp = jax.vjp(_hg_head, *(x_ref[:, sl] for sl in parts), lb_ref[:, ln], gn_ref[...], sts_ref[0, h])
            cts = vjp((do_ref[:, ln].astype(F32), dst_ref[h]))
            for sl, ct in zip(parts, cts[:HG_PARTS]):
                dx_ref[:, sl] = ct.astype(dx_ref.dtype)
            dlb_ref[:, ln] += cts[HG_PARTS]
            dgn_ref[...] += cts[HG_PARTS + 1]
            dst_ref[h] = cts[HG_PARTS + 2]

    rev = lambda w: pl.BlockSpec((tr, w), lambda j: (nt - 1 - j, 0))
    const = lambda shape: pl.BlockSpec(shape, lambda j: (0, 0))
    return pl.pallas_call(
        body,
        name="hgrn2_bwd",
        grid=(nt,),
        in_specs=[rev(HG_PARTS * width), const((1, width)), const((1, LANES)),
                  pl.BlockSpec((1, HG_HEADS, LANES, LANES), lambda j: (nt - 1 - j, 0, 0, 0)), rev(width)],
        out_specs=[rev(HG_PARTS * width), const((1, width)), const((1, LANES))],
        out_shape=[jax.ShapeDtypeStruct(x.shape, BF16), jax.ShapeDtypeStruct((1, width), F32),
                   jax.ShapeDtypeStruct((1, LANES), F32)],
        scratch_shapes=[pltpu.VMEM((HG_HEADS, LANES, LANES), F32)],
        compiler_params=_cparams("arbitrary"),
    )(x, lb, gn, sts, do)


@jax.custom_vjp
def hgrn2_mixer(h, w, lb, gn):
    return _hg_fwd_call(_mm_call(h, w, False, False, name="hgrn2_proj"), lb, gn)[0]


def _hgrn2_mixer_fwd(h, w, lb, gn):
    x = _mm_call(h, w, False, False, name="hgrn2_proj")
    o, sts = _hg_fwd_call(x, lb, gn)
    return o, (h, w, x, lb, gn, sts)


def _hgrn2_mixer_bwd(res, do):
    h, w, x, lb, gn, sts = res
    dx, dlb, dgn = _hg_bwd_call(x, lb, gn, sts, do)
    dh = _mm_call(dx, w, False, True, out_dtype=h.dtype, name="hgrn2_proj_da")
    dw = _mm_call(h, dx, True, False, out_dtype=w.dtype, name="hgrn2_proj_db")
    return dh, dw, dlb, dgn


hgrn2_mixer.defvjp(_hgrn2_mixer_fwd, _hgrn2_mixer_bwd)


D_MODEL = 1024
DEPTH = 2
SSM_GROUPS, SSM_GROUP_CH, SSM_STATE = 32, 16, 64
SSM_WIDTH = SSM_GROUPS * SSM_GROUP_CH
MLA_HEADS, MLA_NOPE, MLA_ROPE, MLA_V = 8, 64, 32, 64
MLA_Q_RANK, MLA_KV_RANK = 512, 256
HG_WIDTH = HG_HEADS * LANES
X_HEADS, X_HEAD_DIM = 4, 128
X_WIDTH = X_HEADS * X_HEAD_DIM
D_FF = 2816
ROPE_THETA = 10000.0
IN_SPLITS = (SSM_WIDTH, MLA_Q_RANK, MLA_KV_RANK, MLA_ROPE, HG_WIDTH, HG_WIDTH, HG_WIDTH, HG_WIDTH, 3 * D_MODEL)
ROPE_LANE0 = MLA_NOPE
MLA_Q_SCALE = LOG2E / math.sqrt(MLA_NOPE + MLA_ROPE)
ROW_TILE = 256


def _t_rms(x, g):
    return (_rms(x, g).astype(BF16),)


def _t_s5_act(y, u, d):
    return (jax.nn.gelu(y + d * u).astype(BF16),)


def _t_glu(z):
    zo, zg = split_lanes(z.astype(F32), D_MODEL)
    return (zo * jax.nn.sigmoid(zg),)


def _t_mla_rope(q, k, kr, c, sa, sb):
    rep = lambda t: jnp.concatenate([t] * MLA_HEADS, axis=1)
    half = MLA_ROPE // 2
    q_out = rope_lanes(q, rep(c), rep(sa), rep(sb), half) * MLA_Q_SCALE
    kr_out = rope_lanes(kr, c, sa, sb, half)
    return q_out.astype(BF16), (k + join_lanes((kr_out,) * MLA_HEADS)).astype(BF16)


def _t_merge(y_ssm, y_mla, y_hg, gates):
    g0, g1, g2 = split_lanes(gates.astype(F32), D_MODEL)
    return ((jax.nn.sigmoid(g0) * y_ssm + jax.nn.sigmoid(g1) * y_mla + jax.nn.sigmoid(g2) * y_hg).astype(BF16),)


def _t_xattn(q, kv):
    scale = 1.0 / math.sqrt(X_HEAD_DIM)
    heads = split_lanes(kv, X_HEAD_DIM)
    outs = []
    for qh, kh, vh in zip(split_lanes(q, X_HEAD_DIM), heads[:X_HEADS], heads[X_HEADS:]):
        sc = bdot_nt(qh, kh) * scale
        p = jnp.exp(sc - jnp.max(sc, axis=-1, keepdims=True))
        p = p / jnp.sum(p, axis=-1, keepdims=True)
        outs.append(bdot_nn(p, vh))
    return (join_lanes(tuple(outs)).astype(BF16),)


def _t_swiglu(gate_up):
    gt, up = split_lanes(gate_up.astype(F32), D_FF)
    return ((gt * jax.nn.sigmoid(gt) * up).astype(BF16),)


def _t_loss(x, tgt, g):
    e = _rms(x, g) - tgt
    return (jnp.broadcast_to(jnp.mean(e * e, axis=-1, keepdims=True), (x.shape[0], LANES)),)


rms_op = rowwise(_t_rms, 1, 0, ROW_TILE, "rmsnorm")
rms_res_op = rowwise(_t_rms, 1, 0, ROW_TILE, "rmsnorm_res", passthrough=True)
s5_act_op = rowwise(_t_s5_act, 2, 0, ROW_TILE, "s5_act")
glu_op = rowwise(_t_glu, 1, 0, ROW_TILE, "glu")
mla_rope_op = rowwise(_t_mla_rope, 3, 3, ROW_TILE, "mla_rope")
merge_op = rowwise(_t_merge, 4, 0, ROW_TILE, "merge")
xattn_op = rowwise(_t_xattn, 1, 0, ROW_TILE, "xattn")
swiglu_op = rowwise(_t_swiglu, 1, 0, ROW_TILE, "swiglu")
loss_op = rowwise(_t_loss, 1, 1, ROW_TILE, "loss")


def _rope_tables(positions):
    half = MLA_ROPE // 2
    inv_freq = ROPE_THETA ** (-jnp.arange(half, dtype=F32) / half)
    ang = positions.astype(F32)[:, None] * inv_freq
    cos, sin = jnp.cos(ang), jnp.sin(ang)
    s = positions.shape[0]
    z = lambda w: jnp.zeros((s, w), F32)
    tail = LANES - ROPE_LANE0 - MLA_ROPE
    c = jnp.concatenate([jnp.ones((s, ROPE_LANE0), F32), cos, cos, z(tail)], axis=1)
    sa = jnp.concatenate([z(ROPE_LANE0), -sin, z(half), z(tail)], axis=1)
    sb = jnp.concatenate([z(ROPE_LANE0), z(half), sin, z(tail)], axis=1)
    return c, sa, sb


def _s5_operators(lam_re, lam_im, b_re, b_im, c_re, c_im, log_step):
    g, p, h = SSM_GROUPS, SSM_STATE, SSM_GROUP_CH
    lam = lax.complex(lam_re, lam_im)
    lam_bar = jnp.exp(lam * jnp.exp(log_step)[:, None])
    b_bar = ((lam_bar - 1.0) / lam)[..., None] * lax.complex(b_re, b_im)
    per = LANES // h
    nb = g // per
    eye = jnp.eye(per, dtype=F32)
    bd = lambda t: jnp.einsum("jgph,gk->jghkp", t.reshape(nb, per, p, h), eye).reshape(nb, per * h, per * p)
    cd = lambda t: jnp.einsum("jghp,gk->jgpkh", t.reshape(nb, per, h, p), eye).reshape(nb, per * p, per * h)
    a = jnp.stack([jnp.real(lam_bar).reshape(-1), jnp.imag(lam_bar).reshape(-1)])
    return a, bd(jnp.real(b_bar)), bd(jnp.imag(b_bar)), cd(c_re), cd(-c_im)


LATENT_WIDTH = 1536
_LATENT = {}
_off = 0
for _name, _w in (("u", SSM_WIDTH), ("q_lat", MLA_Q_RANK), ("kv_lat", MLA_KV_RANK), ("k_rope", LANES)):
    _LATENT[_name] = (_off, _off + _w)
    _off += _w


def _layer_matrices(w, l):
    w_in = w["w_in"][l]
    d, dt = w_in.shape[0], w_in.dtype
    z = lambda n: jnp.zeros((d, n), dt)
    r0 = SSM_WIDTH + MLA_Q_RANK + MLA_KV_RANK
    r1 = r0 + MLA_ROPE
    r2 = r1 + HG_PARTS * HG_WIDTH
    w_latent = jnp.concatenate([w_in[:, :r0], z(ROPE_LANE0), w_in[:, r0:r1],
                                z(LATENT_WIDTH - r0 - ROPE_LANE0 - MLA_ROPE)], axis=1)
    pad_heads = lambda t: jnp.pad(t, ((0, 0), (0, 0), (0, LANES - t.shape[2]))).reshape(t.shape[0], -1)
    uq = w["mla_w_uq"][l].reshape(MLA_Q_RANK, MLA_HEADS, MLA_NOPE + MLA_ROPE)
    ukv = w["mla_w_ukv"][l].reshape(MLA_KV_RANK, MLA_HEADS, MLA_NOPE + MLA_V)
    wo = w["mla_w_o"][l].reshape(MLA_HEADS, MLA_V, D_MODEL)
    return dict(
        w_latent=w_latent, w_hg=w_in[:, r1:r2], w_gates=w_in[:, r2:], glu=w["ssm_w_glu"][l],
        uq=pad_heads(uq), uk=pad_heads(ukv[:, :, :MLA_NOPE]), uv=pad_heads(ukv[:, :, MLA_NOPE:]),
        mla_o=jnp.pad(wo, ((0, 0), (0, LANES - MLA_V), (0, 0))).reshape(MLA_HEADS * LANES, D_MODEL),
        hg_o=w["hg_w_o"][l], w_out=w["w_out"][l], x_q=w["x_w_q"][l], x_kv=w["x_w_kv"][l],
        x_o=w["x_w_o"][l], ffn_gu=w["ffn_w_gate_up"][l], ffn_d=w["ffn_w_down"][l])


def _layer(x, mem, tabs, m, sp, l, lower_bound):
    row = lambda name: sp[name][l].reshape(1, -1)
    h, x = rms_res_op(x, row("norm_mix"))
    latent = matmul(h, m["w_latent"])
    seg = lambda name: latent[:, _LATENT[name][0]:_LATENT[name][1]]
    a, bd_r, bd_i, cd_r, cd_i = _s5_operators(*(sp[n][l] for n in (
        "ssm_lam_re", "ssm_lam_im", "ssm_b_re", "ssm_b_im", "ssm_c_re", "ssm_c_im", "ssm_log_step")))
    u = seg("u")
    x_r, x_i = s5_scan(bd_matmul(u, bd_r, False, True), bd_matmul(u, bd_i, False, True), a)
    y = bd_matmul_add(x_i, cd_i, bd_matmul(x_r, cd_r, True, False), True, False)
    (ya,) = s5_act_op(y, u, row("ssm_d"))
    (y_ssm,) = glu_op(matmul(ya, m["glu"], BF16))
    (qn,) = rms_op(seg("q_lat"), row("mla_q_norm"))
    (kvn,) = rms_op(seg("kv_lat"), row("mla_kv_norm"))
    q, k = mla_rope_op(matmul(qn, m["uq"]), matmul(kvn, m["uk"]), seg("k_rope"), *tabs)
    o = causal_attention(q, k, matmul(kvn, m["uv"], BF16))
    y_mla = matmul(o, m["mla_o"])
    y_hg = matmul(hgrn2_mixer(h, m["w_hg"], lower_bound, row("hg_g_norm")), m["hg_o"])
    (merged,) = merge_op(y_ssm, y_mla, y_hg, matmul(h, m["w_gates"], BF16))
    x = matmul_add(merged, m["w_out"], x)
    hc, x = rms_res_op(x, row("norm_cross"))
    (mn,) = rms_op(mem, row("norm_mem"))
    (ox,) = xattn_op(matmul(hc, m["x_q"], BF16), matmul(mn, m["x_kv"]))
    x = matmul_add(ox, m["x_o"], x)
    hf, x = rms_res_op(x, row("norm_ffn"))
    (act,) = swiglu_op(matmul(hf, m["ffn_gu"], BF16))
    return matmul_add(act, m["ffn_d"], x)


def _lower_bounds(hg_lb):
    lb_p = jax.nn.softmax(hg_lb, axis=0)
    return jnp.cumsum(lb_p, axis=0) - lb_p[0:1]


def _final_loss(target, x, norm_final):
    (row_loss,) = loss_op(x, target, norm_final.reshape(1, -1))
    return 0.5 * jnp.sum(row_loss[:, 0])


def _local_loss(x, mem, positions, target, w, sp):
    tabs = _rope_tables(positions)
    lower = _lower_bounds(sp["hg_lb"])
    for l in range(DEPTH):
        x = _layer(x, mem, tabs, _layer_matrices(w, l), sp, l, lower[l].reshape(1, -1))
    return _final_loss(target, x, sp["norm_final"])


N_DEV = 8
N_CHIPS = 4
COMM_LANES = 512
MESH_ID = pl.DeviceIdType.MESH
_ANY = pl.BlockSpec(memory_space=pl.ANY)
_OTHER_CHIPS = ((1, 0), (0, 1), (1, 1))


def _place():
    return lax.axis_index("x"), lax.axis_index("y"), lax.axis_index("c")


def _all_gather_call(blocks, name):
    n = len(blocks)

    def body(*refs):
        x_refs, out_refs = refs[:n], refs[n:2 * n]
        send_sems, recv_sems, local_sems = refs[2 * n:]
        x, y, c = _place()
        me, sibling = (x, y, c), (x, y, 1 - c)
        chips = [(x ^ fx, y ^ fy) for fx, fy in _OTHER_CHIPS]

        def slot(i, px, py, pc):
            return out_refs[i].at[4 * px + 2 * py + pc]

        def copy(i, k, blk, to, src=None):
            return pltpu.make_async_remote_copy(
                src_ref=slot(i, *blk) if src is None else src, dst_ref=slot(i, *blk),
                send_sem=send_sems.at[i, k], recv_sem=recv_sems.at[i, k], device_id=to, device_id_type=MESH_ID)

        mine = [pltpu.make_async_copy(x_refs[i], slot(i, *me), local_sems.at[i]) for i in range(n)]
        first = []
        for i in range(n):
            first.append(copy(i, 0, me, sibling, src=x_refs[i]))
            first += [copy(i, 1 + j, me, (*chip, c), src=x_refs[i]) for j, chip in enumerate(chips)]
        for cp in mine + first:
            cp.start()
        passed = []
        for j, chip in enumerate(chips):
            for i in range(n):
                copy(i, 1 + j, (*chip, c), me).wait_recv()
                passed.append(copy(i, 4 + j, (*chip, c), sibling))
                passed[-1].start()
        for i in range(n):
            copy(i, 0, sibling, me).wait_recv()
            for j, chip in enumerate(chips):
                copy(i, 4 + j, (*chip, 1 - c), me).wait_recv()
        for cp in first + passed:
            cp.wait_send()
        for cp in mine:
            cp.wait()

    return pl.pallas_call(
        body,
        name=name,
        out_shape=[jax.ShapeDtypeStruct((N_DEV,) + b.shape, b.dtype) for b in blocks],
        in_specs=[_ANY] * n,
        out_specs=[_ANY] * n,
        scratch_shapes=[pltpu.SemaphoreType.DMA((n, 7)), pltpu.SemaphoreType.DMA((n, 7)), pltpu.SemaphoreType.DMA((n,))],
    )(*blocks)


def _pair_exchange_call(gs, name):
    n = len(gs)

    def body(*refs):
        g_refs, got_refs = refs[:n], refs[n:2 * n]
        send_sems, recv_sems = refs[2 * n:]
        x, y, c = _place()
        sends = [pltpu.make_async_remote_copy(
            src_ref=g_refs[i].at[2 * p + (1 - c)], dst_ref=got_refs[i].at[p],
            send_sem=send_sems.at[i, p], recv_sem=recv_sems.at[i, p], device_id=(x, y, 1 - c), device_id_type=MESH_ID)
            for i in range(n) for p in range(N_CHIPS)]
        for cp in sends:
            cp.start()
        for cp in sends:
            cp.wait_recv()
        for cp in sends:
            cp.wait_send()

    return pl.pallas_call(
        body,
        name=name,
        out_shape=[jax.ShapeDtypeStruct((N_CHIPS,) + g.shape[1:], g.dtype) for g in gs],
        in_specs=[_ANY] * n,
        out_specs=[_ANY] * n,
        scratch_shapes=[pltpu.SemaphoreType.DMA((n, N_CHIPS))] * 2,
    )(*gs)


def _chip_exchange_call(parts, name):
    n = len(parts)

    def body(*refs):
        p_refs, got_refs = refs[:n], refs[n:2 * n]
        send_sems, recv_sems = refs[2 * n:]
        x, y, c = _place()
        sends = []
        for i in range(n):
            for k, (fx, fy) in enumerate(_OTHER_CHIPS):
                px, py = x ^ fx, y ^ fy
                sends.append(pltpu.make_async_remote_copy(
                    src_ref=p_refs[i].at[2 * px + py], dst_ref=got_refs[i].at[k],
                    send_sem=send_sems.at[i, k], recv_sem=recv_sems.at[i, k], device_id=(px, py, c), device_id_type=MESH_ID))
        for cp in sends:
            cp.start()
        for cp in sends:
            cp.wait_recv()
        for cp in sends:
            cp.wait_send()

    return pl.pallas_call(
        body,
        name=name,
        out_shape=[jax.ShapeDtypeStruct((3,) + p.shape[1:], p.dtype) for p in parts],
        in_specs=[_ANY] * n,
        out_specs=[_ANY] * n,
        scratch_shapes=[pltpu.SemaphoreType.DMA((n, 3))] * 2,
    )(*parts)


def _rows_cols(shape):
    return math.prod(shape[:-1]), shape[-1]


def _pair_sum_call(g, got, c_idx, name):
    rows, cols = _rows_cols(got.shape[1:])
    tr = _pick_tile(rows, (512, 256, 128, 64, 32, 16))

    def body(c_ref, a_ref, b_ref, o_ref):
        o_ref[...] = (a_ref[...].astype(F32) + b_ref[...].astype(F32)).astype(o_ref.dtype)

    spec = pl.BlockSpec((1, tr, cols), lambda p, i, c_ref: (p, i, 0))
    out = pl.pallas_call(
        body,
        name=name,
        grid_spec=pltpu.PrefetchScalarGridSpec(
            num_scalar_prefetch=1, grid=(N_CHIPS, rows // tr),
            in_specs=[pl.BlockSpec((1, tr, cols), lambda p, i, c_ref: (2 * p + c_ref[0], i, 0)), spec],
            out_specs=spec),
        out_shape=jax.ShapeDtypeStruct((N_CHIPS, rows, cols), got.dtype),
        compiler_params=_cparams("parallel", "parallel"),
    )(c_idx, g.reshape(N_DEV, rows, cols), got.reshape(N_CHIPS, rows, cols))
    return out.reshape(got.shape)


def _chip_sum_call(part, got, chip_idx, name):
    rows, cols = _rows_cols(got.shape[1:])
    tr = _pick_tile(rows, (512, 256, 128, 64, 32, 16))

    def body(p_ref, a_ref, b_ref, o_ref):
        acc = a_ref[0].astype(F32)
        for k in range(3):
            acc = acc + b_ref[k].astype(F32)
        o_ref[...] = acc

    out = pl.pallas_call(
        body,
        name=name,
        grid_spec=pltpu.PrefetchScalarGridSpec(
            num_scalar_prefetch=1, grid=(rows // tr,),
            in_specs=[pl.BlockSpec((1, tr, cols), lambda i, p_ref: (p_ref[0], i, 0)),
                      pl.BlockSpec((3, tr, cols), lambda i, p_ref: (0, i, 0))],
            out_specs=pl.BlockSpec((tr, cols), lambda i, p_ref: (i, 0))),
        out_shape=jax.ShapeDtypeStruct((rows, cols), F32),
        compiler_params=_cparams("parallel"),
    )(chip_idx, part.reshape(N_CHIPS, rows, cols), got.reshape(3, rows, cols))
    return out.reshape(got.shape[1:])


def _reduce_scatter(gs, name):
    x, y, c = _place()
    c_idx = c.astype(jnp.int32).reshape(1)
    chip_idx = (2 * x + y).astype(jnp.int32).reshape(1)
    gots = _pair_exchange_call(gs, name + "_pair")
    parts = [_pair_sum_call(g, got, c_idx, name + "_pair_sum") for g, got in zip(gs, gots)]
    gots = _chip_exchange_call(parts, name + "_chip")
    return [_chip_sum_call(p, got, chip_idx, name + "_chip_sum") for p, got in zip(parts, gots)]


_HBM = pl.BlockSpec(memory_space=pltpu.HBM)
_SEM = pl.BlockSpec(memory_space=pltpu.SEMAPHORE)
_SIDE_EFFECT = pltpu.SideEffectType.DATAFLOW_SIDE_EFFECTING
N_PEERS = N_DEV - 1


def _peer(k):
    x, y, c = _place()
    px, py, pc = x ^ ((k >> 2) & 1), y ^ ((k >> 1) & 1), c ^ (k & 1)
    return (px, py, pc), 4 * px + 2 * py + pc


def _exchange_copy(src_ref, land_ref, send_sems, recv_sems, i, k, scatter, receiving):
    x, y, c = _place()
    me = 4 * x + 2 * y + c
    peer, peer_idx = _peer(k)
    sem = i * N_PEERS + k - 1
    return pltpu.make_async_remote_copy(
        src_ref=src_ref.at[peer_idx] if scatter else src_ref, dst_ref=land_ref.at[peer_idx if receiving else me],
        send_sem=send_sems.at[sem], recv_sem=recv_sems.at[sem], device_id=peer, device_id_type=MESH_ID)


def _exchange_start_call(srcs, scatter, name):
    n = len(srcs)
    slot_shapes = [s.shape[1:] if scatter else s.shape for s in srcs]

    def body(*refs):
        src_refs, land_refs = refs[:n], refs[n:2 * n]
        send_sems, recv_sems = refs[2 * n], refs[2 * n + 1]
        token = refs[-1]
        for i in range(n):
            for k in range(1, N_DEV):
                _exchange_copy(src_refs[i], land_refs[i], send_sems, recv_sems, i, k, scatter, False).start()
        token[...] = jnp.zeros_like(token)

    lands = [pltpu.with_memory_space_constraint(lax.empty((N_DEV,) + shp, s.dtype), pltpu.HBM)
             for shp, s in zip(slot_shapes, srcs)]
    out = pl.pallas_call(
        body,
        name=name,
        out_shape=([pltpu.SemaphoreType.DMA((n * N_PEERS,)), pltpu.SemaphoreType.DMA((n * N_PEERS,))]
                   + [pltpu.HBM(s.shape, s.dtype) for s in srcs] + [pltpu.HBM(l.shape, l.dtype) for l in lands]
                   + [jax.ShapeDtypeStruct((SUBLANES, LANES), F32)]),
        in_specs=[_HBM] * (2 * n),
        out_specs=[_SEM, _SEM] + [_HBM] * (2 * n) + [pl.BlockSpec(memory_space=pltpu.VMEM)],
        input_output_aliases={j: 2 + j for j in range(2 * n)},
        compiler_params=pltpu.CompilerParams(has_side_effects=_SIDE_EFFECT),
    )(*[pltpu.with_memory_space_constraint(s, pltpu.HBM) for s in srcs], *lands)
    return out[0], out[1], list(out[2:2 + n]), list(out[2 + n:2 + 2 * n]), out[-1]


def _exchange_wait_call(started, after, scatter, name):
    send_sems, recv_sems, srcs, lands, _ = started
    n = len(srcs)

    def body(*refs):
        src_refs, land_refs = refs[:n], refs[n:2 * n]
        send_s, recv_s = refs[2 * n], refs[2 * n + 1]
        for i in range(n):
            for k in range(1, N_DEV):
                cp = _exchange_copy(src_refs[i], land_refs[i], send_s, recv_s, i, k, scatter, True)
                cp.wait_send()
                cp.wait_recv()

    out = pl.pallas_call(
        body,
        name=name,
        out_shape=[pltpu.HBM(s.shape, s.dtype) for s in srcs] + [pltpu.HBM(l.shape, l.dtype) for l in lands],
        in_specs=[_HBM] * (2 * n) + [_SEM, _SEM, pl.BlockSpec(memory_space=pl.ANY)],
        out_specs=[_HBM] * (2 * n),
        input_output_aliases={j: j for j in range(2 * n)},
        compiler_params=pltpu.CompilerParams(has_side_effects=_SIDE_EFFECT),
    )(*srcs, *lands, send_sems, recv_sems, after)
    return list(out[n:])


def _own_slot(land, own):
    x, y, c = _place()
    return lax.dynamic_update_index_in_dim(land, own, 4 * x + 2 * y + c, 0)


def _slot_sum_call(land, name):
    rows, cols = _rows_cols(land.shape[1:])
    tr = _pick_tile(rows, (256, 128, 64, 32, 16))

    def body(land_ref, o_ref):
        acc = land_ref[0].astype(F32)
        for s in range(1, N_DEV):
            acc = acc + land_ref[s].astype(F32)
        o_ref[...] = acc

    out = pl.pallas_call(
        body,
        name=name,
        grid=(rows // tr,),
        in_specs=[pl.BlockSpec((N_DEV, tr, cols), lambda i: (0, i, 0))],
        out_specs=pl.BlockSpec((tr, cols), lambda i: (i, 0)),
        out_shape=jax.ShapeDtypeStruct((rows, cols), F32),
        compiler_params=_cparams("parallel"),
    )(land.reshape(N_DEV, rows, cols))
    return out.reshape(land.shape[1:])


SMALL_BLOCK_ROWS = 16


def _pack_small(parts):
    flat = jnp.concatenate([p.reshape(-1) for p in parts])
    chunk = N_DEV * SMALL_BLOCK_ROWS * COMM_LANES
    flat = jnp.pad(flat, (0, (-flat.shape[0]) % chunk))
    return flat.reshape(N_DEV, -1, COMM_LANES)


def _unpack_small(buf, shapes):
    flat = buf.reshape(-1)
    out, off = [], 0
    for shp in shapes:
        n = math.prod(shp)
        out.append(flat[off:off + n].reshape(shp))
        off += n
    return out


SHARDED = dict(w_in=2, ssm_w_glu=2, mla_w_uq=2, mla_w_ukv=2, mla_w_o=2, hg_w_o=2, w_out=1, x_w_q=1, x_w_kv=1,
               x_w_o=2, ffn_w_gate_up=2, ffn_w_down=1)
REPLICATED = ("norm_mix", "ssm_lam_re", "ssm_lam_im", "ssm_b_re", "ssm_b_im", "ssm_c_re", "ssm_c_im", "ssm_d",
              "ssm_log_step", "mla_q_norm", "mla_kv_norm", "hg_lb", "hg_g_norm", "norm_cross", "norm_mem", "norm_ffn",
              "norm_final")


def _join_shards(stacked, axis):
    n, l, a, b = stacked.shape
    if axis == 1:
        return stacked.transpose(1, 0, 2, 3).reshape(l, n * a, b)
    return stacked.transpose(1, 2, 0, 3).reshape(l, a, n * b)


def _split_shards(full, axis):
    l, a, b = full.shape
    if axis == 1:
        return full.reshape(l, N_DEV, a // N_DEV, b).transpose(1, 0, 2, 3)
    return full.reshape(l, a, N_DEV, b // N_DEV).transpose(2, 0, 1, 3)


def _layer_fn(l, mem, tabs):
    def f(x, full, small, lower):
        m = _layer_matrices(dict(zip(SHARDED, full)), 0)
        return _layer(x, mem, tabs, m, dict(zip(REPLICATED, small)), l, lower[l].reshape(1, -1))
    return f


ADAM_LR, ADAM_B1, ADAM_B2, ADAM_EPS, ADAM_WD, ADAM_STEP = 0.001, 0.9, 0.999, 1e-08, 0.01, 10


def _adamw_call(w, g, m, v, name):
    shape = w.shape
    cols = shape[-1]
    rows = math.prod(shape[:-1]) if len(shape) > 1 else 1
    tr = _pick_tile(rows, (512, 256, 128, 64, 32, 16, 8))

    def body(w_ref, g_ref, m_ref, v_ref, d_ref, nm_ref, nv_ref):
        gg = g_ref[...]
        m_new = ADAM_B1 * m_ref[...] + (1.0 - ADAM_B1) * gg
        v_new = ADAM_B2 * v_ref[...] + (1.0 - ADAM_B2) * jnp.square(gg)
        m_hat = m_new / (1.0 - ADAM_B1 ** ADAM_STEP)
        v_hat = v_new / (1.0 - ADAM_B2 ** ADAM_STEP)
        d_ref[...] = -ADAM_LR * (m_hat / (jnp.sqrt(v_hat) + ADAM_EPS) + ADAM_WD * w_ref[...])
        nm_ref[...] = m_new
        nv_ref[...] = v_new

    spec = pl.BlockSpec((tr, cols), lambda i: (i, 0))
    outs = pl.pallas_call(
        body, name=name, grid=(rows // tr,), in_specs=[spec] * 4, out_specs=[spec] * 3,
        out_shape=[jax.ShapeDtypeStruct((rows, cols), F32)] * 3, compiler_params=_cparams("parallel"),
    )(*(t.reshape(rows, cols) for t in (w, g, m, v)))
    return tuple(o.reshape(shape) for o in outs)


WEIGHTS = ("norm_mix", "w_in", "ssm_lam_re", "ssm_lam_im", "ssm_b_re", "ssm_b_im", "ssm_c_re", "ssm_c_im", "ssm_d",
           "ssm_log_step", "ssm_w_glu", "mla_q_norm", "mla_kv_norm", "mla_w_uq", "mla_w_ukv", "mla_w_o", "hg_lb",
           "hg_g_norm", "hg_w_o", "w_out", "norm_cross", "norm_mem", "x_w_q", "x_w_kv", "x_w_o", "norm_ffn",
           "ffn_w_gate_up", "ffn_w_down", "norm_final")


def kernel(x, mem, positions, norm_mix, w_in, ssm_lam_re, ssm_lam_im, ssm_b_re, ssm_b_im, ssm_c_re, ssm_c_im, ssm_d, ssm_log_step, ssm_w_glu, mla_q_norm, mla_kv_norm, mla_w_uq, mla_w_ukv, mla_w_o, hg_lb, hg_g_norm, hg_w_o, w_out, norm_cross, norm_mem, x_w_q, x_w_kv, x_w_o, norm_ffn, ffn_w_gate_up, ffn_w_down, norm_final, loss_target, m_norm_mix, m_w_in, m_ssm_lam_re, m_ssm_lam_im, m_ssm_b_re, m_ssm_b_im, m_ssm_c_re, m_ssm_c_im, m_ssm_d, m_ssm_log_step, m_ssm_w_glu, m_mla_q_norm, m_mla_kv_norm, m_mla_w_uq, m_mla_w_ukv, m_mla_w_o, m_hg_lb, m_hg_g_norm, m_hg_w_o, m_w_out, m_norm_cross, m_norm_mem, m_x_w_q, m_x_w_kv, m_x_w_o, m_norm_ffn, m_ffn_w_gate_up, m_ffn_w_down, m_norm_final, v_norm_mix, v_w_in, v_ssm_lam_re, v_ssm_lam_im, v_ssm_b_re, v_ssm_b_im, v_ssm_c_re, v_ssm_c_im, v_ssm_d, v_ssm_log_step, v_ssm_w_glu, v_mla_q_norm, v_mla_kv_norm, v_mla_w_uq, v_mla_w_ukv, v_mla_w_o, v_hg_lb, v_hg_g_norm, v_hg_w_o, v_w_out, v_norm_cross, v_norm_mem, v_x_w_q, v_x_w_kv, v_x_w_o, v_norm_ffn, v_ffn_w_gate_up, v_ffn_w_down, v_norm_final):
    given = dict(locals())
    weights = {n: given[n] for n in WEIGHTS}
    axes = tuple(SHARDED.values())
    small = tuple(weights[n] for n in REPLICATED)
    layer_shards = lambda l: [weights[n][l:l + 1].astype(BF16) for n in SHARDED]
    join = lambda stacked: tuple(_join_shards(p, ax) for p, ax in zip(stacked, axes))
    split = lambda cts: [_split_shards(ct, ax) for ct, ax in zip(cts, axes)]
    xs, tabs = x[0], _rope_tables(positions[0])
    lower, vjp_lower = jax.vjp(_lower_bounds, hg_lb)

    full0 = join(_all_gather_call(layer_shards(0), "weights_all_gather_l0"))
    shards1 = layer_shards(1)
    gather1 = _exchange_start_call(shards1, False, "weights_gather_start_l1")
    xs, _ = lax.optimization_barrier((xs, gather1[4]))
    x1, vjp0 = jax.vjp(_layer_fn(0, mem[0], tabs), xs, full0, small, lower)
    lands = _exchange_wait_call(gather1, x1, False, "weights_gather_wait_l1")
    full1 = join([_own_slot(land, own) for land, own in zip(lands, shards1)])
    x2, vjp1 = jax.vjp(_layer_fn(1, mem[0], tabs), x1, full1, small, lower)
    loss_local, vjp_loss = jax.vjp(functools.partial(_final_loss, loss_target[0]), x2, norm_final)

    dx2, d_norm_final = vjp_loss(jnp.ones((), F32))
    dx1, dfull1, dsmall1, dlower1 = vjp1(dx2)
    gs1 = split(dfull1)
    scatter1 = _exchange_start_call(gs1, True, "grads_scatter_start_l1")
    dx1, _ = lax.optimization_barrier((dx1, scatter1[4]))
    gx, dfull0, dsmall0, dlower0 = vjp0(dx1)
    lands = _exchange_wait_call(scatter1, gx, True, "grads_scatter_wait_l1")
    me = 4 * lax.axis_index("x") + 2 * lax.axis_index("y") + lax.axis_index("c")
    g_l1 = [_slot_sum_call(_own_slot(land, lax.dynamic_index_in_dim(g, me, 0, keepdims=False)), "grads_slot_sum_l1")
            for land, g in zip(lands, gs1)]
    g_l0 = _reduce_scatter(split(dfull0), "grads_reduce_scatter_l0")
    grads = {n: jnp.concatenate([a, b], axis=0) for n, a, b in zip(SHARDED, g_l0, g_l1)}

    d_small = dict(zip(REPLICATED, (a + b for a, b in zip(dsmall0, dsmall1))))
    d_small["norm_final"] = d_small["norm_final"] + d_norm_final
    d_small["hg_lb"] = d_small["hg_lb"] + vjp_lower(dlower0 + dlower1)[0]
    shapes = [d_small[n].shape for n in REPLICATED]
    (mine,) = _reduce_scatter([_pack_small([d_small[n] for n in REPLICATED])], "small_reduce_scatter")
    (total,) = _all_gather_call([mine], "small_all_gather")
    grads.update(zip(REPLICATED, _unpack_small(total, shapes)))

    loss = lax.psum(loss_local, ("x", "y", "c"))
    steps = {n: _adamw_call(weights[n], grads[n], given["m_" + n], given["v_" + n], "adamw_" + n) for n in WEIGHTS}
    return (loss, gx[None], *[grads[n] for n in WEIGHTS], *[steps[n][0] for n in WEIGHTS],
            *[steps[n][1] for n in WEIGHTS], *[steps[n][2] for n in WEIGHTS])
```

```python
import functools
import math

import jax
import jax.numpy as jnp
from jax import lax
from jax.experimental import pallas as pl
from jax.experimental.pallas import tpu as pltpu

F32 = jnp.float32
BF16 = jnp.bfloat16

VMEM_LIMIT_BYTES = 48 * 1024 * 1024
LANES = 128
SUBLANES = 8


def _cparams(*sem):
    return pltpu.CompilerParams(dimension_semantics=sem, vmem_limit_bytes=VMEM_LIMIT_BYTES)


def _pick_tile(n, cands):
    for c in cands:
        if n % c == 0:
            return c
    return n


MM_VMEM_BUDGET = 38 * 1024 * 1024
MM_STEP_US = 0.35
HBM_BYTES_PER_US = 3.0e6
VREG_RMW_PER_US = 1.5e3


def _divisor_tiles(dim, cands):
    out = [t for t in cands if dim % t == 0]
    return out or [dim]


def _mm_tiles(m, n, k, sa, sb, so):
    tms = _divisor_tiles(m, (1024, 512, 256, 128, 64, 32, 16, 8))[:2]
    tns = _divisor_tiles(n, (2048, 1536, 1408, 1024, 768, 512, 384, 256, 128))
    tks = [k // d for d in (1, 2, 4, 8, 13, 16, 26, 32, 52) if k % d == 0 and (k // d) % LANES == 0] or [k]
    best = None
    for tk in tks:
        nk = k // tk
        for tm in tms:
            for tn in tns:
                vmem = 2 * (tm * tk * sa + tk * tn * sb + tm * tn * so) + (tm * tn * 4 if nk > 1 else 0)
                if vmem > MM_VMEM_BUDGET:
                    continue
                steps = (m // tm) * (n // tn) * nk
                a_reads = m * k * sa * (n // tn if nk > 1 else 1)
                b_reads = k * n * sb * (m // tm if (nk > 1 or n // tn > 1) else 1)
                cost = (steps * MM_STEP_US + (a_reads + b_reads) / HBM_BYTES_PER_US
                        + (m * n * nk / 1024 / VREG_RMW_PER_US if nk > 1 else 0.0))
                if best is None or cost < best[0]:
                    best = (cost, tm, tn, tk)
    assert best is not None, (m, n, k)
    return best[1:]


def _mm_tiles_cached_t(m, n, k, sa, sb, so):
    for tm in _divisor_tiles(m, (1024, 512, 256, 128)):
        for tn in _divisor_tiles(n, (1024, 512, 384, 256, 128)):
            if 2 * (k * tm * sa + k * tn * sb + tm * tn * so) + tm * k * 2 <= MM_VMEM_BUDGET:
                return tm, tn
    return None


def _mm_tn_cached_call(a, b, tiles, out_dtype, name):
    k, m = a.shape
    n = b.shape[1]
    tm, tn = tiles

    def body(a_ref, b_ref, o_ref, at_ref):
        @pl.when(pl.program_id(1) == 0)
        def _():
            at_ref[...] = a_ref[...].astype(BF16).T

        o_ref[...] = lax.dot_general(at_ref[...], b_ref[...].astype(BF16), _NN_DIMS,
                                     preferred_element_type=F32).astype(out_dtype)

    return pl.pallas_call(
        body,
        name=name,
        grid=(m // tm, n // tn),
        in_specs=[pl.BlockSpec((k, tm), lambda i, j: (0, i)), pl.BlockSpec((k, tn), lambda i, j: (0, j))],
        out_specs=pl.BlockSpec((tm, tn), lambda i, j: (i, j)),
        out_shape=jax.ShapeDtypeStruct((m, n), out_dtype),
        scratch_shapes=[pltpu.VMEM((tm, k), BF16)],
        compiler_params=_cparams("parallel", "arbitrary"),
    )(a, b)


_NN_DIMS = (((1,), (0,)), ((), ()))


def _mm_call(a, b, ta, tb, add=None, out_dtype=F32, name="mm"):
    m, k = (a.shape[1], a.shape[0]) if ta else a.shape
    k2, n = (b.shape[1], b.shape[0]) if tb else b.shape
    assert k == k2, (a.shape, b.shape, ta, tb)
    sizes = (a.dtype.itemsize, b.dtype.itemsize, jnp.dtype(out_dtype).itemsize + (add.dtype.itemsize if add is not None else 0))
    if ta:
        tiles = _mm_tiles_cached_t(m, n, k, *sizes)
        if tiles is not None:
            return _mm_tn_cached_call(a, b, tiles, out_dtype, name)
    tm, tn, tk = _mm_tiles(m, n, k, *sizes)
    nk = k // tk
    a_spec = pl.BlockSpec((tk, tm), lambda i, j, kk: (kk, i)) if ta else pl.BlockSpec((tm, tk), lambda i, j, kk: (i, kk))
    b_spec = pl.BlockSpec((tn, tk), lambda i, j, kk: (j, kk)) if tb else pl.BlockSpec((tk, tn), lambda i, j, kk: (kk, j))
    o_spec = pl.BlockSpec((tm, tn), lambda i, j, kk: (i, j))
    dn = (((0 if ta else 1,), (1 if tb else 0,)), ((), ()))
    has_add = add is not None

    def body(*refs):
        a_ref, b_ref = refs[0], refs[1]
        c_ref = refs[2] if has_add else None
        o_ref = refs[3] if has_add else refs[2]
        p = lax.dot_general(a_ref[...].astype(BF16), b_ref[...].astype(BF16), dn, preferred_element_type=F32)

        def finish(r):
            if has_add:
                r = r + c_ref[...].astype(F32)
            o_ref[...] = r.astype(out_dtype)

        if nk == 1:
            finish(p)
        else:
            acc_ref = refs[-1]
            kk = pl.program_id(2)

            @pl.when(kk == 0)
            def _():
                acc_ref[...] = p

            @pl.when(kk > 0)
            def _():
                acc_ref[...] += p

            @pl.when(kk == nk - 1)
            def _():
                finish(acc_ref[...])

    in_specs = [a_spec, b_spec] + ([o_spec] if has_add else [])
    args = (a, b) + ((add,) if has_add else ())
    return pl.pallas_call(
        body,
        name=name,
        grid=(m // tm, n // tn, nk),
        in_specs=in_specs,
        out_specs=o_spec,
        out_shape=jax.ShapeDtypeStruct((m, n), out_dtype),
        scratch_shapes=[] if nk == 1 else [pltpu.VMEM((tm, tn), F32)],
        compiler_params=_cparams("parallel", "parallel", "arbitrary"),
    )(*args)


@functools.partial(jax.custom_vjp, nondiff_argnums=(2,))
def matmul(a, b, out_dtype=F32):
    return _mm_call(a, b, False, False, out_dtype=out_dtype, name="mm_fwd")


def _matmul_fwd(a, b, out_dtype):
    return matmul(a, b, out_dtype), (a, b)


def _matmul_bwd(out_dtype, res, g):
    a, b = res
    da = _mm_call(g, b, False, True, out_dtype=a.dtype, name="mm_da")
    db = _mm_call(a, g, True, False, out_dtype=b.dtype, name="mm_db")
    return da, db


matmul.defvjp(_matmul_fwd, _matmul_bwd)


@jax.custom_vjp
def matmul_add(a, b, c):
    return _mm_call(a, b, False, False, add=c, name="mm_add_fwd")


def _matmul_add_fwd(a, b, c):
    return _mm_call(a, b, False, False, add=c, name="mm_add_fwd"), (a, b)


def _matmul_add_bwd(res, g):
    a, b = res
    da = _mm_call(g, b, False, True, out_dtype=a.dtype, name="mm_da")
    db = _mm_call(a, g, True, False, out_dtype=b.dtype, name="mm_db")
    return da, db, g


matmul_add.defvjp(_matmul_add_fwd, _matmul_add_bwd)


def rowwise(f, n_rows, n_aux, tile, name, passthrough=False):
    def specs(arrs, tiled):
        out = []
        for x in arrs:
            if tiled:
                out.append(pl.BlockSpec((tile, x.shape[1]), lambda i: (i, 0)))
            else:
                out.append(pl.BlockSpec(x.shape, lambda i: (0, 0)))
        return out

    def tile_structs(args):
        rows_aux, params = args[: n_rows + n_aux], args[n_rows + n_aux:]
        return [jax.ShapeDtypeStruct((tile, x.shape[1]), x.dtype) for x in rows_aux] + [
            jax.ShapeDtypeStruct(p.shape, p.dtype) for p in params]

    def fwd_call(*args):
        s = args[0].shape[0]
        outs = jax.eval_shape(f, *tile_structs(args))
        n_in = len(args)

        def body(*refs):
            vals = [r[...] for r in refs[:n_in]]
            res = f(*vals)
            for o_ref, r in zip(refs[n_in:], res):
                o_ref[...] = r.astype(o_ref.dtype)

        return pl.pallas_call(
            body,
            name=name + "_fwd",
            grid=(s // tile,),
            in_specs=specs(args[: n_rows + n_aux], True) + specs(args[n_rows + n_aux:], False),
            out_specs=[pl.BlockSpec((tile, o.shape[1]), lambda i: (i, 0)) for o in outs],
            out_shape=[jax.ShapeDtypeStruct((s, o.shape[1]), o.dtype) for o in outs],
            compiler_params=_cparams("parallel"),
        )(*args)

    def bwd_call(args, gs):
        s = args[0].shape[0]
        rows, aux, params = args[:n_rows], args[n_rows:n_rows + n_aux], args[n_rows + n_aux:]
        n_in, n_g, n_p = len(args), len(gs), len(params)
        n_gf = n_g - 1 if passthrough else n_g

        def body(*refs):
            vals = [r[...] for r in refs[:n_in]]
            gvals = tuple(r[...] for r in refs[n_in:n_in + n_gf])
            out_refs = refs[n_in + n_g:]
            auxv = vals[n_rows:n_rows + n_aux]

            def g_(*rp):
                return tuple(f(*rp[:n_rows], *auxv, *rp[n_rows:]))

            _, vjp = jax.vjp(g_, *vals[:n_rows], *vals[n_rows + n_aux:])
            cts = list(vjp(gvals))
            if passthrough:
                cts[0] = cts[0] + refs[n_in + n_gf][...]
            for o_ref, ct in zip(out_refs[:n_rows], cts[:n_rows]):
                o_ref[...] = ct.astype(o_ref.dtype)
            if n_p:
                @pl.when(pl.program_id(0) == 0)
                def _():
                    for o_ref in out_refs[n_rows:]:
                        o_ref[...] = jnp.zeros_like(o_ref)

                for o_ref, ct in zip(out_refs[n_rows:], cts[n_rows:]):
                    o_ref[...] += ct.astype(o_ref.dtype)

        return pl.pallas_call(
            body,
            name=name + "_bwd",
            grid=(s // tile,),
            in_specs=specs(rows + aux, True) + specs(params, False) + specs(gs, True),
            out_specs=specs(rows, True) + specs(params, False),
            out_shape=[jax.ShapeDtypeStruct(x.shape, x.dtype) for x in rows + params],
            compiler_params=_cparams("arbitrary" if n_p else "parallel"),
        )(*args, *gs)

    @jax.custom_vjp
    def op(*args):
        return tuple(fwd_call(*args)) + ((args[0],) if passthrough else ())

    def op_fwd(*args):
        return op(*args), args

    def op_bwd(args, gs):
        cts = bwd_call(tuple(args), tuple(gs))
        rows_ct, par_ct = cts[:n_rows], cts[n_rows:]
        aux_ct = [jnp.zeros_like(a) for a in args[n_rows:n_rows + n_aux]]
        return tuple(rows_ct) + tuple(aux_ct) + tuple(par_ct)

    op.defvjp(op_fwd, op_bwd)
    return op


SCAN_SEGMENTS = SUBLANES
SCAN_TILE_ROWS = 512
SCAN_TILE_LANES = 512


def _scan_specs(s, n):
    tr = min(SCAN_TILE_ROWS, s)
    tn = min(SCAN_TILE_LANES, n)
    return tr, tn, s // tr, n // tn


def _scan_step(ar, ai, xr, xi, br, bi):
    return ar * xr - ai * xi + br, ar * xi + ai * xr + bi


def _scan_finals(b_r, b_i, a, reverse, name):
    s, n = b_r.shape
    tr, tn, nt, nc = _scan_specs(s, n)
    ti = tr // SUBLANES
    tmap = (lambda c, j: (nt - 1 - j, c)) if reverse else (lambda c, j: (j, c))

    def body(br_ref, bi_ref, a_ref, fr_ref, fi_ref, sr, si):
        j = pl.program_id(1)

        @pl.when(j == 0)
        def _():
            sr[...] = jnp.zeros_like(sr)
            si[...] = jnp.zeros_like(si)

        ar = jnp.broadcast_to(a_ref[0:1, :], (SUBLANES, tn))
        ai = jnp.broadcast_to(a_ref[1:2, :], (SUBLANES, tn))

        def step(ii, carry):
            i = (ti - 1 - ii) if reverse else ii
            off = pl.multiple_of(i * SUBLANES, SUBLANES)
            return _scan_step(ar, ai, carry[0], carry[1], br_ref[pl.ds(off, SUBLANES), :], bi_ref[pl.ds(off, SUBLANES), :])

        xr, xi = lax.fori_loop(0, ti, step, (sr[...], si[...]), unroll=4)
        sr[...] = xr
        si[...] = xi

        @pl.when(j == nt - 1)
        def _():
            fr_ref[...] = xr
            fi_ref[...] = xi

    bspec = pl.BlockSpec((tr, tn), tmap)
    fspec = pl.BlockSpec((SUBLANES, tn), lambda c, j: (0, c))
    return pl.pallas_call(
        body,
        name=name,
        grid=(nc, nt),
        in_specs=[bspec, bspec, pl.BlockSpec((2, tn), lambda c, j: (0, c))],
        out_specs=[fspec, fspec],
        out_shape=[jax.ShapeDtypeStruct((SUBLANES, n), F32)] * 2,
        scratch_shapes=[pltpu.VMEM((SUBLANES, tn), F32)] * 2,
        compiler_params=_cparams("parallel", "arbitrary"),
    )(b_r, b_i, a)


def _scan_states(b_r, b_i, a, f_r, f_i, reverse, xs, name):
    s, n = b_r.shape
    tr, tn, nt, nc = _scan_specs(s, n)
    ti = tr // SUBLANES
    seg_len = s // SCAN_SEGMENTS
    assert seg_len & (seg_len - 1) == 0
    with_acc = xs is not None
    tmap = (lambda c, j: (nt - 1 - j, c)) if reverse else (lambda c, j: (j, c))
    order = list(range(SCAN_SEGMENTS))[::-1] if reverse else list(range(SCAN_SEGMENTS))

    def body(*refs):
        br_ref, bi_ref, a_ref, fr_ref, fi_ref = refs[:5]
        pos = 5
        if with_acc:
            xr_ref, xi_ref = refs[5:7]
            pos = 7
        or_ref, oi_ref = refs[pos:pos + 2]
        pos += 2
        if with_acc:
            dr_ref, di_ref = refs[pos:pos + 2]
            pos += 2
        sr, si = refs[pos:pos + 2]
        if with_acc:
            accr, acci = refs[pos + 2:pos + 4]
        j = pl.program_id(1)
        a_r1, a_i1 = a_ref[0:1, :], a_ref[1:2, :]

        @pl.when(j == 0)
        def _():
            pr, pi = a_r1, a_i1
            for _ in range(seg_len.bit_length() - 1):
                pr, pi = pr * pr - pi * pi, 2.0 * pr * pi
            cr = jnp.zeros((1, tn), F32)
            ci = jnp.zeros((1, tn), F32)
            for idx, k in enumerate(order):
                if idx > 0:
                    kp = order[idx - 1]
                    cr, ci = (fr_ref[kp:kp + 1, :] + pr * cr - pi * ci, fi_ref[kp:kp + 1, :] + pr * ci + pi * cr)
                sr[k:k + 1, :] = cr
                si[k:k + 1, :] = ci
            if with_acc:
                accr[...] = jnp.zeros_like(accr)
                acci[...] = jnp.zeros_like(acci)

        ar = jnp.broadcast_to(a_r1, (SUBLANES, tn))
        ai = jnp.broadcast_to(a_i1, (SUBLANES, tn))

        def step(ii, carry):
            i = (ti - 1 - ii) if reverse else ii
            off = pl.multiple_of(i * SUBLANES, SUBLANES)
            rows = pl.ds(off, SUBLANES)
            xr, xi = carry[0], carry[1]
            if with_acc:
                zr, zi = xr_ref[rows, :], xi_ref[rows, :]
                acc = (carry[2] + xr * zr + xi * zi, carry[3] + xi * zr - xr * zi)
            nr, ni = _scan_step(ar, ai, xr, xi, br_ref[rows, :], bi_ref[rows, :])
            or_ref[rows, :] = nr
            oi_ref[rows, :] = ni
            return (nr, ni) + (acc if with_acc else ())

        init = (sr[...], si[...]) + ((accr[...], acci[...]) if with_acc else ())
        out = lax.fori_loop(0, ti, step, init, unroll=4)
        sr[...] = out[0]
        si[...] = out[1]
        if with_acc:
            accr[...] = out[2]
            acci[...] = out[3]

            @pl.when(j == nt - 1)
            def _():
                dr_ref[...] = jnp.sum(out[2], axis=0, keepdims=True)
                di_ref[...] = jnp.sum(out[3], axis=0, keepdims=True)

    bspec = pl.BlockSpec((tr, tn), tmap)
    fspec = pl.BlockSpec((SUBLANES, tn), lambda c, j: (0, c))
    dspec = pl.BlockSpec((1, tn), lambda c, j: (0, c))
    in_specs = [bspec, bspec, pl.BlockSpec((2, tn), lambda c, j: (0, c)), fspec, fspec] + ([bspec, bspec] if with_acc else [])
    out_specs = [bspec, bspec] + ([dspec, dspec] if with_acc else [])
    out_shape = [jax.ShapeDtypeStruct((s, n), F32)] * 2 + ([jax.ShapeDtypeStruct((1, n), F32)] * 2 if with_acc else [])
    scratch = [pltpu.VMEM((SUBLANES, tn), F32)] * (4 if with_acc else 2)
    args = (b_r, b_i, a, f_r, f_i) + (tuple(xs) if with_acc else ())
    return pl.pallas_call(
        body,
        name=name,
        grid=(nc, nt),
        in_specs=in_specs,
        out_specs=out_specs,
        out_shape=out_shape,
        scratch_shapes=scratch,
        compiler_params=_cparams("parallel", "arbitrary"),
    )(*args)


@jax.custom_vjp
def s5_scan(b_r, b_i, a):
    f_r, f_i = _scan_finals(b_r, b_i, a, False, "s5_scan_fin")
    return tuple(_scan_states(b_r, b_i, a, f_r, f_i, False, None, "s5_scan"))


def _s5_scan_fwd(b_r, b_i, a):
    xs = s5_scan(b_r, b_i, a)
    return xs, (a, xs)


def _s5_scan_bwd(res, g):
    a, xs = res
    a_conj = a * jnp.array([[1.0], [-1.0]], F32)
    f_r, f_i = _scan_finals(g[0], g[1], a_conj, True, "s5_rscan_fin")
    g_r, g_i, da_r, da_i = _scan_states(g[0], g[1], a_conj, f_r, f_i, True, xs, "s5_rscan")
    return g_r, g_i, jnp.concatenate([da_r, da_i], axis=0)


s5_scan.defvjp(_s5_scan_fwd, _s5_scan_bwd)


BD_SEG_ROWS = 128


def _bd_view(x, seg):
    return x if seg else x.reshape(SCAN_SEGMENTS, x.shape[0] // SCAN_SEGMENTS, x.shape[1])


def _bd_spec(seg, w, rows):
    if seg:
        return pl.BlockSpec((SCAN_SEGMENTS * rows, w), lambda i, j: (i, j))
    return pl.BlockSpec((SCAN_SEGMENTS, rows, w), lambda i, j: (0, i, j))


def _bd_apply_call(a, w, transpose_w, a_seg, out_seg, add, name):
    s = a.shape[0]
    nb, ka, kn = w.shape
    wi, wo = (kn, ka) if transpose_w else (ka, kn)
    convert = a_seg != out_seg
    a_seg_k, out_seg_k = (a_seg, out_seg) if convert else (True, True)
    ov_shape = (s, nb * wo) if out_seg_k else (SCAN_SEGMENTS, s // SCAN_SEGMENTS, nb * wo)
    dn = _NT_DIMS if transpose_w else _NN_DIMS
    has_add = add is not None
    rows = min(BD_SEG_ROWS, s // SCAN_SEGMENTS)

    assert not convert or (wo if a_seg_k else wi) == LANES

    def body(*refs):
        a_ref, w_ref = refs[0], refs[1]
        c_ref = refs[2] if has_add else None
        o_ref = refs[3] if has_add else refs[2]
        wb = w_ref[0].astype(BF16)
        mm = lambda a_val: lax.dot_general(a_val.astype(BF16), wb, dn, preferred_element_type=F32)
        if not convert:
            r = mm(a_ref[...])
            o_ref[...] = r + c_ref[...] if has_add else r
        elif out_seg_k:
            scr = refs[-1]
            for k in range(SCAN_SEGMENTS):
                scr[pl.ds(k, rows, stride=SCAN_SEGMENTS), :] = a_ref[k]
            r = mm(scr[...])
            o_ref[...] = r + c_ref[...] if has_add else r
        else:
            scr = refs[-1]
            scr[...] = mm(a_ref[...])
            for k in range(SCAN_SEGMENTS):
                r = scr[pl.ds(k, rows, stride=SCAN_SEGMENTS), :]
                o_ref[k] = r + c_ref[k] if has_add else r

    args = [_bd_view(a, a_seg_k), w] + ([_bd_view(add, out_seg_k)] if has_add else [])
    in_specs = ([_bd_spec(a_seg_k, wi, rows), pl.BlockSpec((1, ka, kn), lambda i, j: (j, 0, 0))]
                + ([_bd_spec(out_seg_k, wo, rows)] if has_add else []))
    out = pl.pallas_call(
        body,
        name=name,
        grid=(s // (SCAN_SEGMENTS * rows), nb),
        in_specs=in_specs,
        out_specs=_bd_spec(out_seg_k, wo, rows),
        out_shape=jax.ShapeDtypeStruct(ov_shape, F32),
        scratch_shapes=[pltpu.VMEM((SCAN_SEGMENTS * rows, LANES), F32)] if convert else [],
        compiler_params=_cparams("parallel", "parallel"),
    )(*args)
    return out.reshape(s, nb * wo)


def _bd_weight_grad_call(a, g, a_seg, g_seg, nb, name):
    s = a.shape[0]
    ka, kn = a.shape[1] // nb, g.shape[1] // nb
    seg_len = s // SCAN_SEGMENTS
    convert = a_seg != g_seg

    def view(x, seg, w):
        if not convert:
            return x, pl.BlockSpec((s, w), lambda j: (0, j))
        if seg:
            return x, pl.BlockSpec((s, w), lambda j: (0, j))
        return x.reshape(SCAN_SEGMENTS, seg_len, x.shape[1]), pl.BlockSpec((SCAN_SEGMENTS, seg_len, w), lambda j: (0, 0, j))

    av, a_spec = view(a, a_seg, ka)
    gv, g_spec = view(g, g_seg, kn)

    assert not convert or (kn if a_seg else ka) == LANES

    def body(a_ref, g_ref, o_ref, *scratch):
        tn = lambda x, y: lax.dot_general(x.astype(BF16), y.astype(BF16), _TN_DIMS, preferred_element_type=F32)
        if not convert:
            o_ref[0] = tn(a_ref[...], g_ref[...])
        else:
            scr = scratch[0]
            t_ref = g_ref if a_seg else a_ref
            for k in range(SCAN_SEGMENTS):
                scr[pl.ds(k, seg_len, stride=SCAN_SEGMENTS), :] = t_ref[k]
            o_ref[0] = tn(a_ref[...], scr[...]) if a_seg else tn(scr[...], g_ref[...])

    return pl.pallas_call(
        body,
        name=name,
        grid=(nb,),
        in_specs=[a_spec, g_spec],
        out_specs=pl.BlockSpec((1, ka, kn), lambda j: (j, 0, 0)),
        out_shape=jax.ShapeDtypeStruct((nb, ka, kn), F32),
        scratch_shapes=[pltpu.VMEM((s, LANES), F32)] if convert else [],
        compiler_params=_cparams("parallel"),
    )(av, gv)


_NT_DIMS = (((1,), (1,)), ((), ()))
_TN_DIMS = (((0,), (0,)), ((), ()))


@functools.partial(jax.custom_vjp, nondiff_argnums=(2, 3))
def bd_matmul(a, w, a_seg, out_seg):
    return _bd_apply_call(a, w, False, a_seg, out_seg, None, "bd_mm_fwd")


def _bd_matmul_fwd(a, w, a_seg, out_seg):
    return bd_matmul(a, w, a_seg, out_seg), (a, w)


def _bd_matmul_bwd(a_seg, out_seg, res, g):
    a, w = res
    da = _bd_apply_call(g, w, True, out_seg, a_seg, None, "bd_mm_da")
    dw = _bd_weight_grad_call(a, g, a_seg, out_seg, w.shape[0], "bd_mm_dw")
    return da, dw


bd_matmul.defvjp(_bd_matmul_fwd, _bd_matmul_bwd)


@functools.partial(jax.custom_vjp, nondiff_argnums=(3, 4))
def bd_matmul_add(a, w, c, a_seg, out_seg):
    return _bd_apply_call(a, w, False, a_seg, out_seg, c, "bd_mm_add_fwd")


def _bd_matmul_add_fwd(a, w, c, a_seg, out_seg):
    return bd_matmul_add(a, w, c, a_seg, out_seg), (a, w)


def _bd_matmul_add_bwd(a_seg, out_seg, res, g):
    return _bd_matmul_bwd(a_seg, out_seg, res, g) + (g,)


bd_matmul_add.defvjp(_bd_matmul_add_fwd, _bd_matmul_add_bwd)


_NN = (((1,), (0,)), ((), ()))
_NT = (((1,), (1,)), ((), ()))
_TN = (((0,), (0,)), ((), ()))


def _dot(a, b, dn):
    return lax.dot_general(a.astype(BF16), b.astype(BF16), dn, preferred_element_type=F32)


@jax.custom_vjp
def bdot_nn(a, b):
    return _dot(a, b, _NN)


bdot_nn.defvjp(lambda a, b: (_dot(a, b, _NN), (a, b)),
               lambda r, g: (_dot(g, r[1], _NT).astype(r[0].dtype), _dot(r[0], g, _TN).astype(r[1].dtype)))


@jax.custom_vjp
def bdot_nt(a, b):
    return _dot(a, b, _NT)


bdot_nt.defvjp(lambda a, b: (_dot(a, b, _NT), (a, b)),
               lambda r, g: (_dot(g, r[1], _NN).astype(r[0].dtype), _dot(g, r[0], _TN).astype(r[1].dtype)))


@jax.custom_vjp
def bdot_tn(a, b):
    return _dot(a, b, _TN)


bdot_tn.defvjp(lambda a, b: (_dot(a, b, _TN), (a, b)),
               lambda r, g: (_dot(r[1], g, _NT).astype(r[0].dtype), _dot(r[0], g, _NN).astype(r[1].dtype)))


def _split3(x):
    h = x.astype(BF16)
    r = x - h.astype(F32)
    m = r.astype(BF16)
    l = (r - m.astype(F32)).astype(BF16)
    return h, m, l


def _exact_dot(t, x, dn):
    h, m, l = _split3(x)
    d = lambda p: lax.dot_general(t, p, dn, preferred_element_type=F32)
    return d(h) + d(m) + d(l)


@jax.custom_vjp
def select_dot(t, x):
    return _exact_dot(t, x, _NN)


select_dot.defvjp(lambda t, x: (_exact_dot(t, x, _NN), t),
                  lambda t, g: (jnp.zeros_like(t), _exact_dot(t, g, _TN)))


def _split_rows_impl(x, h):
    return tuple(x[i * h:(i + 1) * h] for i in range(x.shape[0] // h))


@functools.partial(jax.custom_vjp, nondiff_argnums=(1,))
def split_rows(x, h):
    return _split_rows_impl(x, h)


split_rows.defvjp(lambda x, h: (_split_rows_impl(x, h), None),
                  lambda h, r, g: (jnp.concatenate(g, axis=0),))


@jax.custom_vjp
def join_rows(parts):
    return jnp.concatenate(parts, axis=0)


def _join_rows_bwd(hs, g):
    out, off = [], 0
    for h in hs:
        out.append(g[off:off + h])
        off += h
    return (tuple(out),)


join_rows.defvjp(lambda parts: (jnp.concatenate(parts, axis=0), tuple(p.shape[0] for p in parts)), _join_rows_bwd)


def _split_lanes_impl(x, w):
    return tuple(x[:, i * w:(i + 1) * w] for i in range(x.shape[1] // w))


@functools.partial(jax.custom_vjp, nondiff_argnums=(1,))
def split_lanes(x, w):
    return _split_lanes_impl(x, w)


split_lanes.defvjp(lambda x, w: (_split_lanes_impl(x, w), None),
                   lambda w, r, g: (jnp.concatenate(g, axis=1),))


def _join_impl(parts):
    return jnp.concatenate(parts, axis=1)


@jax.custom_vjp
def join_lanes(parts):
    return _join_impl(parts)


def _join_bwd(ws, g):
    out, off = [], 0
    for w in ws:
        out.append(g[:, off:off + w])
        off += w
    return (tuple(out),)


join_lanes.defvjp(lambda parts: (_join_impl(parts), tuple(p.shape[1] for p in parts)), _join_bwd)


def _rope_impl(x, c, sa, sb, shift):
    w = x.shape[1]
    return x * c + pltpu.roll(x, w - shift, 1) * sa + pltpu.roll(x, shift, 1) * sb


@functools.partial(jax.custom_vjp, nondiff_argnums=(4,))
def rope_lanes(x, c, sa, sb, shift):
    return _rope_impl(x, c, sa, sb, shift)


def _rope_bwd(shift, r, g):
    c, sa, sb = r
    w = g.shape[1]
    dx = g * c + pltpu.roll(g * sa, shift, 1) + pltpu.roll(g * sb, w - shift, 1)
    return dx, jnp.zeros_like(c), jnp.zeros_like(sa), jnp.zeros_like(sb)


rope_lanes.defvjp(lambda x, c, sa, sb, shift: (_rope_impl(x, c, sa, sb, shift), (c, sa, sb)), _rope_bwd)


RMS_EPS = 1e-6


def _rms(x, g):
    return x * lax.rsqrt(jnp.mean(x * x, axis=-1, keepdims=True) + RMS_EPS) * g


ATTN_BLOCK = 512
MASK_VALUE = -1e30
LOG2E = math.log2(math.e)
LN2 = math.log(2.0)
V_ONES_LANE = 64


def _causal_mask(t):
    r = lax.broadcasted_iota(jnp.int32, (t, t), 0)
    c = lax.broadcasted_iota(jnp.int32, (t, t), 1)
    return c <= r


def _attn_fwd_call(q, k, v):
    s, width = q.shape
    n_heads = width // LANES
    tq = min(ATTN_BLOCK, s)
    nq = s // tq

    def body(q_ref, k_ref, v_ref, o_ref, lse_ref):
        i = pl.program_id(1)
        qb = q_ref[...].astype(BF16)
        ones_lane = lax.broadcasted_iota(jnp.int32, (tq, LANES), 1) == V_ONES_LANE

        def block(kb, carry, masked):
            m, acc = carry
            rows = pl.ds(pl.multiple_of(kb * tq, tq), tq)
            sc = lax.dot_general(qb, k_ref[rows, :].astype(BF16), _NT, preferred_element_type=F32)
            if masked:
                sc = jnp.where(_causal_mask(tq), sc, MASK_VALUE)
            m_new = jnp.maximum(m, jnp.max(sc, axis=-1, keepdims=True))
            p = jnp.exp2(sc - m_new).astype(BF16)
            vb = jnp.where(ones_lane, 1.0, v_ref[rows, :]).astype(BF16)
            acc = jnp.exp2(m - m_new) * acc + lax.dot_general(p, vb, _NN, preferred_element_type=F32)
            return m_new, acc

        init = (jnp.full((tq, 1), MASK_VALUE, F32), jnp.zeros((tq, LANES), F32))
        carry = lax.fori_loop(0, i, lambda kb, c: block(kb, c, False), init)
        m, acc = block(i, carry, True)
        l = jnp.sum(jnp.where(ones_lane, acc, 0.0), axis=-1, keepdims=True)
        o_ref[...] = jnp.where(ones_lane, 0.0, acc / l).astype(o_ref.dtype)
        lse_ref[...] = jnp.broadcast_to(m + jnp.log2(l), (tq, LANES))

    qspec = pl.BlockSpec((tq, LANES), lambda h, i: (i, h))
    kspec = pl.BlockSpec((s, LANES), lambda h, i: (0, h))
    return pl.pallas_call(
        body,
        name="mla_attn_fwd",
        grid=(n_heads, nq),
        in_specs=[qspec, kspec, kspec],
        out_specs=[qspec, qspec],
        out_shape=[jax.ShapeDtypeStruct((s, width), BF16), jax.ShapeDtypeStruct((s, width), F32)],
        compiler_params=_cparams("parallel", "parallel"),
    )(q, k, v)


def _attn_bwd_call(q, k, v, o, lse, do):
    s, width = q.shape
    n_heads = width // LANES
    tq = min(ATTN_BLOCK, s)
    nq = s // tq

    def body(q_ref, k_ref, v_ref, o_ref, lse_ref, do_ref, dq_ref, dk_ref, dv_ref, dq_acc):
        j = pl.program_id(1)

        @pl.when(j == 0)
        def _():
            dq_acc[...] = jnp.zeros_like(dq_acc)

        kb = k_ref[...].astype(BF16)
        vb = v_ref[...].astype(BF16)

        def block(i, carry, masked):
            dk, dv = carry
            rows = pl.ds(pl.multiple_of(i * tq, tq), tq)
            qi = q_ref[rows, :].astype(BF16)
            doi = do_ref[rows, :].astype(F32)
            delta = jnp.sum(doi * o_ref[rows, :].astype(F32), axis=-1, keepdims=True)
            sc = lax.dot_general(qi, kb, _NT, preferred_element_type=F32)
            if masked:
                sc = jnp.where(_causal_mask(tq), sc, MASK_VALUE)
            p = jnp.exp2(sc - lse_ref[rows, 0:1])
            dob = doi.astype(BF16)
            dv = dv + lax.dot_general(p.astype(BF16), dob, _TN, preferred_element_type=F32)
            dp = lax.dot_general(dob, vb, _NT, preferred_element_type=F32)
            ds = (p * (dp - delta)).astype(BF16)
            dq_acc[rows, :] += lax.dot_general(ds, kb, _NN, preferred_element_type=F32)
            dk = dk + lax.dot_general(ds, qi, _TN, preferred_element_type=F32)
            return dk, dv

        zero = jnp.zeros((tq, LANES), F32)
        carry = block(j, (zero, zero), True)
        dk, dv = lax.fori_loop(j + 1, nq, lambda i, c: block(i, c, False), carry)
        dk_ref[...] = (dk * LN2).astype(dk_ref.dtype)
        dv_ref[...] = dv.astype(dv_ref.dtype)

        @pl.when(j == nq - 1)
        def _():
            dq_ref[...] = (dq_acc[...] * LN2).astype(dq_ref.dtype)

    full = pl.BlockSpec((s, LANES), lambda h, j: (0, h))
    blk = pl.BlockSpec((tq, LANES), lambda h, j: (j, h))
    return pl.pallas_call(
        body,
        name="mla_attn_bwd",
        grid=(n_heads, nq),
        in_specs=[full, blk, blk, full, full, full],
        out_specs=[full, blk, blk],
        out_shape=[jax.ShapeDtypeStruct((s, width), t.dtype) for t in (q, k, v)],
        scratch_shapes=[pltpu.VMEM((s, LANES), F32)],
        compiler_params=_cparams("parallel", "arbitrary"),
    )(q, k, v, o, lse, do)


@jax.custom_vjp
def causal_attention(q, k, v):
    return _attn_fwd_call(q, k, v)[0]


def _causal_attention_fwd(q, k, v):
    o, lse = _attn_fwd_call(q, k, v)
    return o, (q, k, v, o, lse)


def _causal_attention_bwd(res, do):
    return tuple(_attn_bwd_call(*res, do))


causal_attention.defvjp(_causal_attention_fwd, _causal_attention_bwd)


HG_HEADS = 4
HG_CHUNK = 32
HG_REF_ROW = HG_CHUNK // 2 - 1
HG_TILE_ROWS = 256
HG_EXP_CLAMP = 80.0


def _hg_tile_masks(t):
    shift = HG_CHUNK.bit_length() - 1
    r = lax.broadcasted_iota(jnp.int32, (t, t), 0)
    c = lax.broadcasted_iota(jnp.int32, (t, t), 1)
    start = lax.shift_left(lax.shift_right_logical(r, shift), shift)
    causal = (c >= start) & (c <= r)
    return causal, c == start + HG_REF_ROW, c == start + (HG_CHUNK - 1)


def _hg_tile(q, fl, v, lb, st):
    t = q.shape[0]
    causal, ref_sel, last_sel = _hg_tile_masks(t)
    f = lb + (1.0 - lb) * jax.nn.sigmoid(fl)
    kk = 1.0 - f
    qs = q * jax.nn.sigmoid(q)
    b = select_dot(causal.astype(BF16), jnp.log(f))
    b_ref = select_dot(ref_sel.astype(BF16), b)
    b_last = select_dot(last_sel.astype(BF16), b)
    q_in = qs * jnp.exp(jnp.minimum(b - b_ref, HG_EXP_CLAMP))
    k_in = kk * jnp.exp(jnp.minimum(b_ref - b, HG_EXP_CLAMP))
    o = bdot_nn(jnp.where(causal, bdot_nt(q_in, k_in), 0.0), v)
    q_hat = split_rows(qs * jnp.exp(b), HG_CHUNK)
    k_hat = split_rows(kk * jnp.exp(b_last - b), HG_CHUNK)
    decay = split_rows(jnp.exp(b_last), HG_CHUNK)
    vs = split_rows(v, HG_CHUNK)
    first_row = lax.broadcasted_iota(jnp.int32, (HG_CHUNK, LANES), 0) == 0
    inter = []
    for c in range(t // HG_CHUNK):
        inter.append(bdot_nt(q_hat[c], st))
        st = st * jnp.sum(jnp.where(first_row, decay[c], 0.0), axis=0, keepdims=True) + bdot_tn(vs[c], k_hat[c])
    return o + join_rows(tuple(inter)), st


def _hg_head(q, fl, v, gate, lb, gn, st):
    o, st = _hg_tile(q, fl, v, lb, st)
    return _rms(o, gn) * (gate * jax.nn.sigmoid(gate)), st


HG_PARTS = 4


def _hg_part_slices(h, width):
    return [slice(p * width + h * LANES, p * width + (h + 1) * LANES) for p in range(HG_PARTS)]


def _hg_fwd_call(x, lb, gn):
    s = x.shape[0]
    width = x.shape[1] // HG_PARTS
    tr = min(HG_TILE_ROWS, s)
    nt = s // tr

    def body(x_ref, lb_ref, gn_ref, o_ref, sts_ref, st_ref):
        @pl.when(pl.program_id(0) == 0)
        def _():
            st_ref[...] = jnp.zeros_like(st_ref)

        for h in range(HG_HEADS):
            ln = slice(h * LANES, (h + 1) * LANES)
            st = st_ref[h]
            sts_ref[0, h] = st
            o, st_new = _hg_head(*(x_ref[:, sl] for sl in _hg_part_slices(h, width)), lb_ref[:, ln], gn_ref[...], st)
            o_ref[:, ln] = o.astype(o_ref.dtype)
            st_ref[h] = st_new

    const = lambda shape: pl.BlockSpec(shape, lambda j: (0, 0))
    return pl.pallas_call(
        body,
        name="hgrn2_fwd",
        grid=(nt,),
        in_specs=[pl.BlockSpec((tr, HG_PARTS * width), lambda j: (j, 0)), const((1, width)), const((1, LANES))],
        out_specs=[pl.BlockSpec((tr, width), lambda j: (j, 0)),
                   pl.BlockSpec((1, HG_HEADS, LANES, LANES), lambda j: (j, 0, 0, 0))],
        out_shape=[jax.ShapeDtypeStruct((s, width), BF16),
                   jax.ShapeDtypeStruct((nt, HG_HEADS, LANES, LANES), F32)],
        scratch_shapes=[pltpu.VMEM((HG_HEADS, LANES, LANES), F32)],
        compiler_params=_cparams("arbitrary"),
    )(x, lb, gn)


def _hg_bwd_call(x, lb, gn, sts, do):
    s = x.shape[0]
    width = x.shape[1] // HG_PARTS
    tr = min(HG_TILE_ROWS, s)
    nt = s // tr

    def body(x_ref, lb_ref, gn_ref, sts_ref, do_ref, dx_ref, dlb_ref, dgn_ref, dst_ref):
        @pl.when(pl.program_id(0) == 0)
        def _():
            dst_ref[...] = jnp.zeros_like(dst_ref)
            dlb_ref[...] = jnp.zeros_like(dlb_ref)
            dgn_ref[...] = jnp.zeros_like(dgn_ref)

        for h in range(HG_HEADS):
            ln = slice(h * LANES, (h + 1) * LANES)
            parts = _hg_part_slices(h, width)
            _, vjp = jax.vjp(_hg_head, *(x_ref[:, sl] for sl in parts), lb_ref[:, ln], gn_ref[...], sts_ref[0, h])
            cts = vjp((do_ref[:, ln].astype(F32), dst_ref[h]))
            for sl, ct in zip(parts, cts[:HG_PARTS]):
                dx_ref[:, sl] = ct.astype(dx_ref.dtype)
            dlb_ref[:, ln] += cts[HG_PARTS]
            dgn_ref[...] += cts[HG_PARTS + 1]
            dst_ref[h] = cts[HG_PARTS + 2]

    rev = lambda w: pl.BlockSpec((tr, w), lambda j: (nt - 1 - j, 0))
    const = lambda shape: pl.BlockSpec(shape, lambda j: (0, 0))
    return pl.pallas_call(
        body,
        name="hgrn2_bwd",
        grid=(nt,),
        in_specs=[rev(HG_PARTS * width), const((1, width)), const((1, LANES)),
                  pl.BlockSpec((1, HG_HEADS, LANES, LANES), lambda j: (nt - 1 - j, 0, 0, 0)), rev(width)],
        out_specs=[rev(HG_PARTS * width), const((1, width)), const((1, LANES))],
        out_shape=[jax.ShapeDtypeStruct(x.shape, BF16), jax.ShapeDtypeStruct((1, width), F32),
                   jax.ShapeDtypeStruct((1, LANES), F32)],
        scratch_shapes=[pltpu.VMEM((HG_HEADS, LANES, LANES), F32)],
        compiler_params=_cparams("arbitrary"),
    )(x, lb, gn, sts, do)


@jax.custom_vjp
def hgrn2_mixer(h, w, lb, gn):
    return _hg_fwd_call(_mm_call(h, w, False, False, name="hgrn2_proj"), lb, gn)[0]


def _hgrn2_mixer_fwd(h, w, lb, gn):
    x = _mm_call(h, w, False, False, name="hgrn2_proj")
    o, sts = _hg_fwd_call(x, lb, gn)
    return o, (h, w, x, lb, gn, sts)


def _hgrn2_mixer_bwd(res, do):
    h, w, x, lb, gn, sts = res
    dx, dlb, dgn = _hg_bwd_call(x, lb, gn, sts, do)
    dh = _mm_call(dx, w, False, True, out_dtype=h.dtype, name="hgrn2_proj_da")
    dw = _mm_call(h, dx, True, False, out_dtype=w.dtype, name="hgrn2_proj_db")
    return dh, dw, dlb, dgn


hgrn2_mixer.defvjp(_hgrn2_mixer_fwd, _hgrn2_mixer_bwd)


D_MODEL = 1024
DEPTH = 2
SSM_GROUPS, SSM_GROUP_CH, SSM_STATE = 32, 16, 64
SSM_WIDTH = SSM_GROUPS * SSM_GROUP_CH
MLA_HEADS, MLA_NOPE, MLA_ROPE, MLA_V = 8, 64, 32, 64
MLA_Q_RANK, MLA_KV_RANK = 512, 256
HG_WIDTH = HG_HEADS * LANES
X_HEADS, X_HEAD_DIM = 4, 128
X_WIDTH = X_HEADS * X_HEAD_DIM
D_FF = 2816
ROPE_THETA = 10000.0
IN_SPLITS = (SSM_WIDTH, MLA_Q_RANK, MLA_KV_RANK, MLA_ROPE, HG_WIDTH, HG_WIDTH, HG_WIDTH, HG_WIDTH, 3 * D_MODEL)
ROPE_LANE0 = MLA_NOPE
MLA_Q_SCALE = LOG2E / math.sqrt(MLA_NOPE + MLA_ROPE)
ROW_TILE = 256


def _t_rms(x, g):
    return (_rms(x, g).astype(BF16),)


def _t_s5_act(y, u, d):
    return (jax.nn.gelu(y + d * u).astype(BF16),)


def _t_glu(z):
    zo, zg = split_lanes(z.astype(F32), D_MODEL)
    return (zo * jax.nn.sigmoid(zg),)


def _t_mla_rope(q, k, kr, c, sa, sb):
    rep = lambda t: jnp.concatenate([t] * MLA_HEADS, axis=1)
    half = MLA_ROPE // 2
    q_out = rope_lanes(q, rep(c), rep(sa), rep(sb), half) * MLA_Q_SCALE
    kr_out = rope_lanes(kr, c, sa, sb, half)
    return q_out.astype(BF16), (k + join_lanes((kr_out,) * MLA_HEADS)).astype(BF16)


def _t_merge(y_ssm, y_mla, y_hg, gates):
    g0, g1, g2 = split_lanes(gates.astype(F32), D_MODEL)
    return ((jax.nn.sigmoid(g0) * y_ssm + jax.nn.sigmoid(g1) * y_mla + jax.nn.sigmoid(g2) * y_hg).astype(BF16),)


def _t_xattn(q, kv):
    scale = 1.0 / math.sqrt(X_HEAD_DIM)
    heads = split_lanes(kv, X_HEAD_DIM)
    outs = []
    for qh, kh, vh in zip(split_lanes(q, X_HEAD_DIM), heads[:X_HEADS], heads[X_HEADS:]):
        sc = bdot_nt(qh, kh) * scale
        p = jnp.exp(sc - jnp.max(sc, axis=-1, keepdims=True))
        p = p / jnp.sum(p, axis=-1, keepdims=True)
        outs.append(bdot_nn(p, vh))
    return (join_lanes(tuple(outs)).astype(BF16),)


def _t_swiglu(gate_up):
    gt, up = split_lanes(gate_up.astype(F32), D_FF)
    return ((gt * jax.nn.sigmoid(gt) * up).astype(BF16),)


def _t_loss(x, tgt, g):
    e = _rms(x, g) - tgt
    return (jnp.broadcast_to(jnp.mean(e * e, axis=-1, keepdims=True), (x.shape[0], LANES)),)


rms_op = rowwise(_t_rms, 1, 0, ROW_TILE, "rmsnorm")
rms_res_op = rowwise(_t_rms, 1, 0, ROW_TILE, "rmsnorm_res", passthrough=True)
s5_act_op = rowwise(_t_s5_act, 2, 0, ROW_TILE, "s5_act")
glu_op = rowwise(_t_glu, 1, 0, ROW_TILE, "glu")
mla_rope_op = rowwise(_t_mla_rope, 3, 3, ROW_TILE, "mla_rope")
merge_op = rowwise(_t_merge, 4, 0, ROW_TILE, "merge")
xattn_op = rowwise(_t_xattn, 1, 0, ROW_TILE, "xattn")
swiglu_op = rowwise(_t_swiglu, 1, 0, ROW_TILE, "swiglu")
loss_op = rowwise(_t_loss, 1, 1, ROW_TILE, "loss")


def _rope_tables(positions):
    half = MLA_ROPE // 2
    inv_freq = ROPE_THETA ** (-jnp.arange(half, dtype=F32) / half)
    ang = positions.astype(F32)[:, None] * inv_freq
    cos, sin = jnp.cos(ang), jnp.sin(ang)
    s = positions.shape[0]
    z = lambda w: jnp.zeros((s, w), F32)
    tail = LANES - ROPE_LANE0 - MLA_ROPE
    c = jnp.concatenate([jnp.ones((s, ROPE_LANE0), F32), cos, cos, z(tail)], axis=1)
    sa = jnp.concatenate([z(ROPE_LANE0), -sin, z(half), z(tail)], axis=1)
    sb = jnp.concatenate([z(ROPE_LANE0), z(half), sin, z(tail)], axis=1)
    return c, sa, sb


def _s5_operators(lam_re, lam_im, b_re, b_im, c_re, c_im, log_step):
    g, p, h = SSM_GROUPS, SSM_STATE, SSM_GROUP_CH
    lam = lax.complex(lam_re, lam_im)
    lam_bar = jnp.exp(lam * jnp.exp(log_step)[:, None])
    b_bar = ((lam_bar - 1.0) / lam)[..., None] * lax.complex(b_re, b_im)
    per = LANES // h
    nb = g // per
    eye = jnp.eye(per, dtype=F32)
    bd = lambda t: jnp.einsum("jgph,gk->jghkp", t.reshape(nb, per, p, h), eye).reshape(nb, per * h, per * p)
    cd = lambda t: jnp.einsum("jghp,gk->jgpkh", t.reshape(nb, per, h, p), eye).reshape(nb, per * p, per * h)
    a = jnp.stack([jnp.real(lam_bar).reshape(-1), jnp.imag(lam_bar).reshape(-1)])
    return a, bd(jnp.real(b_bar)), bd(jnp.imag(b_bar)), cd(c_re), cd(-c_im)


LATENT_WIDTH = 1536
_LATENT = {}
_off = 0
for _name, _w in (("u", SSM_WIDTH), ("q_lat", MLA_Q_RANK), ("kv_lat", MLA_KV_RANK), ("k_rope", LANES)):
    _LATENT[_name] = (_off, _off + _w)
    _off += _w


def _layer_matrices(w, l):
    w_in = w["w_in"][l]
    d, dt = w_in.shape[0], w_in.dtype
    z = lambda n: jnp.zeros((d, n), dt)
    r0 = SSM_WIDTH + MLA_Q_RANK + MLA_KV_RANK
    r1 = r0 + MLA_ROPE
    r2 = r1 + HG_PARTS * HG_WIDTH
    w_latent = jnp.concatenate([w_in[:, :r0], z(ROPE_LANE0), w_in[:, r0:r1],
                                z(LATENT_WIDTH - r0 - ROPE_LANE0 - MLA_ROPE)], axis=1)
    pad_heads = lambda t: jnp.pad(t, ((0, 0), (0, 0), (0, LANES - t.shape[2]))).reshape(t.shape[0], -1)
    uq = w["mla_w_uq"][l].reshape(MLA_Q_RANK, MLA_HEADS, MLA_NOPE + MLA_ROPE)
    ukv = w["mla_w_ukv"][l].reshape(MLA_KV_RANK, MLA_HEADS, MLA_NOPE + MLA_V)
    wo = w["mla_w_o"][l].reshape(MLA_HEADS, MLA_V, D_MODEL)
    return dict(
        w_latent=w_latent, w_hg=w_in[:, r1:r2], w_gates=w_in[:, r2:], glu=w["ssm_w_glu"][l],
        uq=pad_heads(uq), uk=pad_heads(ukv[:, :, :MLA_NOPE]), uv=pad_heads(ukv[:, :, MLA_NOPE:]),
        mla_o=jnp.pad(wo, ((0, 0), (0, LANES - MLA_V), (0, 0))).reshape(MLA_HEADS * LANES, D_MODEL),
        hg_o=w["hg_w_o"][l], w_out=w["w_out"][l], x_q=w["x_w_q"][l], x_kv=w["x_w_kv"][l],
        x_o=w["x_w_o"][l], ffn_gu=w["ffn_w_gate_up"][l], ffn_d=w["ffn_w_down"][l])


def _layer(x, mem, tabs, m, sp, l, lower_bound):
    row = lambda name: sp[name][l].reshape(1, -1)
    h, x = rms_res_op(x, row("norm_mix"))
    latent = matmul(h, m["w_latent"])
    seg = lambda name: latent[:, _LATENT[name][0]:_LATENT[name][1]]
    a, bd_r, bd_i, cd_r, cd_i = _s5_operators(*(sp[n][l] for n in (
        "ssm_lam_re", "ssm_lam_im", "ssm_b_re", "ssm_b_im", "ssm_c_re", "ssm_c_im", "ssm_log_step")))
    u = seg("u")
    x_r, x_i = s5_scan(bd_matmul(u, bd_r, False, True), bd_matmul(u, bd_i, False, True), a)
    y = bd_matmul_add(x_i, cd_i, bd_matmul(x_r, cd_r, True, False), True, False)
    (ya,) = s5_act_op(y, u, row("ssm_d"))
    (y_ssm,) = glu_op(matmul(ya, m["glu"], BF16))
    (qn,) = rms_op(seg("q_lat"), row("mla_q_norm"))
    (kvn,) = rms_op(seg("kv_lat"), row("mla_kv_norm"))
    q, k = mla_rope_op(matmul(qn, m["uq"]), matmul(kvn, m["uk"]), seg("k_rope"), *tabs)
    o = causal_attention(q, k, matmul(kvn, m["uv"], BF16))
    y_mla = matmul(o, m["mla_o"])
    y_hg = matmul(hgrn2_mixer(h, m["w_hg"], lower_bound, row("hg_g_norm")), m["hg_o"])
    (merged,) = merge_op(y_ssm, y_mla, y_hg, matmul(h, m["w_gates"], BF16))
    x = matmul_add(merged, m["w_out"], x)
    hc, x = rms_res_op(x, row("norm_cross"))
    (mn,) = rms_op(mem, row("norm_mem"))
    (ox,) = xattn_op(matmul(hc, m["x_q"], BF16), matmul(mn, m["x_kv"]))
    x = matmul_add(ox, m["x_o"], x)
    hf, x = rms_res_op(x, row("norm_ffn"))
    (act,) = swiglu_op(matmul(hf, m["ffn_gu"], BF16))
    return matmul_add(act, m["ffn_d"], x)


def _lower_bounds(hg_lb):
    lb_p = jax.nn.softmax(hg_lb, axis=0)
    return jnp.cumsum(lb_p, axis=0) - lb_p[0:1]


def _final_loss(target, x, norm_final):
    (row_loss,) = loss_op(x, target, norm_final.reshape(1, -1))
    return 0.5 * jnp.sum(row_loss[:, 0])


def _local_loss(x, mem, positions, target, w, sp):
    tabs = _rope_tables(positions)
    lower = _lower_bounds(sp["hg_lb"])
    for l in range(DEPTH):
        x = _layer(x, mem, tabs, _layer_matrices(w, l), sp, l, lower[l].reshape(1, -1))
    return _final_loss(target, x, sp["norm_final"])


N_DEV = 8
N_CHIPS = 4
COMM_LANES = 512
MESH_ID = pl.DeviceIdType.MESH
_ANY = pl.BlockSpec(memory_space=pl.ANY)
_OTHER_CHIPS = ((1, 0), (0, 1), (1, 1))


def _place():
    return lax.axis_index("x"), lax.axis_index("y"), lax.axis_index("c")


def _all_gather_call(blocks, name):
    n = len(blocks)

    def body(*refs):
        x_refs, out_refs = refs[:n], refs[n:2 * n]
        send_sems, recv_sems, local_sems = refs[2 * n:]
        x, y, c = _place()
        me, sibling = (x, y, c), (x, y, 1 - c)
        chips = [(x ^ fx, y ^ fy) for fx, fy in _OTHER_CHIPS]

        def slot(i, px, py, pc):
            return out_refs[i].at[4 * px + 2 * py + pc]

        def copy(i, k, blk, to, src=None):
            return pltpu.make_async_remote_copy(
                src_ref=slot(i, *blk) if src is None else src, dst_ref=slot(i, *blk),
                send_sem=send_sems.at[i, k], recv_sem=recv_sems.at[i, k], device_id=to, device_id_type=MESH_ID)

        mine = [pltpu.make_async_copy(x_refs[i], slot(i, *me), local_sems.at[i]) for i in range(n)]
        first = []
        for i in range(n):
            first.append(copy(i, 0, me, sibling, src=x_refs[i]))
            first += [copy(i, 1 + j, me, (*chip, c), src=x_refs[i]) for j, chip in enumerate(chips)]
        for cp in mine + first:
            cp.start()
        passed = []
        for j, chip in enumerate(chips):
            for i in range(n):
                copy(i, 1 + j, (*chip, c), me).wait_recv()
                passed.append(copy(i, 4 + j, (*chip, c), sibling))
                passed[-1].start()
        for i in range(n):
            copy(i, 0, sibling, me).wait_recv()
            for j, chip in enumerate(chips):
                copy(i, 4 + j, (*chip, 1 - c), me).wait_recv()
        for cp in first + passed:
            cp.wait_send()
        for cp in mine:
            cp.wait()

    return pl.pallas_call(
        body,
        name=name,
        out_shape=[jax.ShapeDtypeStruct((N_DEV,) + b.shape, b.dtype) for b in blocks],
        in_specs=[_ANY] * n,
        out_specs=[_ANY] * n,
        scratch_shapes=[pltpu.SemaphoreType.DMA((n, 7)), pltpu.SemaphoreType.DMA((n, 7)), pltpu.SemaphoreType.DMA((n,))],
    )(*blocks)


def _pair_exchange_call(gs, name):
    n = len(gs)

    def body(*refs):
        g_refs, got_refs = refs[:n], refs[n:2 * n]
        send_sems, recv_sems = refs[2 * n:]
        x, y, c = _place()
        sends = [pltpu.make_async_remote_copy(
            src_ref=g_refs[i].at[2 * p + (1 - c)], dst_ref=got_refs[i].at[p],
            send_sem=send_sems.at[i, p], recv_sem=recv_sems.at[i, p], device_id=(x, y, 1 - c), device_id_type=MESH_ID)
            for i in range(n) for p in range(N_CHIPS)]
        for cp in sends:
            cp.start()
        for cp in sends:
            cp.wait_recv()
        for cp in sends:
            cp.wait_send()

    return pl.pallas_call(
        body,
        name=name,
        out_shape=[jax.ShapeDtypeStruct((N_CHIPS,) + g.shape[1:], g.dtype) for g in gs],
        in_specs=[_ANY] * n,
        out_specs=[_ANY] * n,
        scratch_shapes=[pltpu.SemaphoreType.DMA((n, N_CHIPS))] * 2,
    )(*gs)


def _chip_exchange_call(parts, name):
    n = len(parts)

    def body(*refs):
        p_refs, got_refs = refs[:n], refs[n:2 * n]
        send_sems, recv_sems = refs[2 * n:]
        x, y, c = _place()
        sends = []
        for i in range(n):
            for k, (fx, fy) in enumerate(_OTHER_CHIPS):
                px, py = x ^ fx, y ^ fy
                sends.append(pltpu.make_async_remote_copy(
                    src_ref=p_refs[i].at[2 * px + py], dst_ref=got_refs[i].at[k],
                    send_sem=send_sems.at[i, k], recv_sem=recv_sems.at[i, k], device_id=(px, py, c), device_id_type=MESH_ID))
        for cp in sends:
            cp.start()
        for cp in sends:
            cp.wait_recv()
        for cp in sends:
            cp.wait_send()

    return pl.pallas_call(
        body,
        name=name,
        out_shape=[jax.ShapeDtypeStruct((3,) + p.shape[1:], p.dtype) for p in parts],
        in_specs=[_ANY] * n,
        out_specs=[_ANY] * n,
        scratch_shapes=[pltpu.SemaphoreType.DMA((n, 3))] * 2,
    )(*parts)


def _rows_cols(shape):
    return math.prod(shape[:-1]), shape[-1]


def _pair_sum_call(g, got, c_idx, name):
    rows, cols = _rows_cols(got.shape[1:])
    tr = _pick_tile(rows, (512, 256, 128, 64, 32, 16))

    def body(c_ref, a_ref, b_ref, o_ref):
        o_ref[...] = (a_ref[...].astype(F32) + b_ref[...].astype(F32)).astype(o_ref.dtype)

    spec = pl.BlockSpec((1, tr, cols), lambda p, i, c_ref: (p, i, 0))
    out = pl.pallas_call(
        body,
        name=name,
        grid_spec=pltpu.PrefetchScalarGridSpec(
            num_scalar_prefetch=1, grid=(N_CHIPS, rows // tr),
            in_specs=[pl.BlockSpec((1, tr, cols), lambda p, i, c_ref: (2 * p + c_ref[0], i, 0)), spec],
            out_specs=spec),
        out_shape=jax.ShapeDtypeStruct((N_CHIPS, rows, cols), got.dtype),
        compiler_params=_cparams("parallel", "parallel"),
    )(c_idx, g.reshape(N_DEV, rows, cols), got.reshape(N_CHIPS, rows, cols))
    return out.reshape(got.shape)


def _chip_sum_call(part, got, chip_idx, name):
    rows, cols = _rows_cols(got.shape[1:])
    tr = _pick_tile(rows, (512, 256, 128, 64, 32, 16))

    def body(p_ref, a_ref, b_ref, o_ref):
        acc = a_ref[0].astype(F32)
        for k in range(3):
            acc = acc + b_ref[k].astype(F32)
        o_ref[...] = acc

    out = pl.pallas_call(
        body,
        name=name,
        grid_spec=pltpu.PrefetchScalarGridSpec(
            num_scalar_prefetch=1, grid=(rows // tr,),
            in_specs=[pl.BlockSpec((1, tr, cols), lambda i, p_ref: (p_ref[0], i, 0)),
                      pl.BlockSpec((3, tr, cols), lambda i, p_ref: (0, i, 0))],
            out_specs=pl.BlockSpec((tr, cols), lambda i, p_ref: (i, 0))),
        out_shape=jax.ShapeDtypeStruct((rows, cols), F32),
        compiler_params=_cparams("parallel"),
    )(chip_idx, part.reshape(N_CHIPS, rows, cols), got.reshape(3, rows, cols))
    return out.reshape(got.shape[1:])


def _reduce_scatter(gs, name):
    x, y, c = _place()
    c_idx = c.astype(jnp.int32).reshape(1)
    chip_idx = (2 * x + y).astype(jnp.int32).reshape(1)
    gots = _pair_exchange_call(gs, name + "_pair")
    parts = [_pair_sum_call(g, got, c_idx, name + "_pair_sum") for g, got in zip(gs, gots)]
    gots = _chip_exchange_call(parts, name + "_chip")
    return [_chip_sum_call(p, got, chip_idx, name + "_chip_sum") for p, got in zip(parts, gots)]


_HBM = pl.BlockSpec(memory_space=pltpu.HBM)
_SEM = pl.BlockSpec(memory_space=pltpu.SEMAPHORE)
_SIDE_EFFECT = pltpu.SideEffectType.DATAFLOW_SIDE_EFFECTING
N_PEERS = N_DEV - 1


def _peer(k):
    x, y, c = _place()
    px, py, pc = x ^ ((k >> 2) & 1), y ^ ((k >> 1) & 1), c ^ (k & 1)
    return (px, py, pc), 4 * px + 2 * py + pc


def _exchange_copy(src_ref, land_ref, send_sems, recv_sems, i, k, scatter, receiving):
    x, y, c = _place()
    me = 4 * x + 2 * y + c
    peer, peer_idx = _peer(k)
    sem = i * N_PEERS + k - 1
    return pltpu.make_async_remote_copy(
        src_ref=src_ref.at[peer_idx] if scatter else src_ref, dst_ref=land_ref.at[peer_idx if receiving else me],
        send_sem=send_sems.at[sem], recv_sem=recv_sems.at[sem], device_id=peer, device_id_type=MESH_ID)


def _exchange_start_call(srcs, after, scatter, name):
    n = len(srcs)
    slot_shapes = [s.shape[1:] if scatter else s.shape for s in srcs]

    def body(*refs):
        src_refs, land_refs = refs[:n], refs[n:2 * n]
        send_sems, recv_sems = refs[2 * n + 1], refs[2 * n + 2]
        token = refs[-1]
        for i in range(n):
            for k in range(1, N_DEV):
                _exchange_copy(src_refs[i], land_refs[i], send_sems, recv_sems, i, k, scatter, False).start()
        token[...] = jnp.zeros_like(token)

    lands = [pltpu.with_memory_space_constraint(lax.empty((N_DEV,) + shp, s.dtype), pltpu.HBM)
             for shp, s in zip(slot_shapes, srcs)]
    out = pl.pallas_call(
        body,
        name=name,
        out_shape=([pltpu.SemaphoreType.DMA((n * N_PEERS,)), pltpu.SemaphoreType.DMA((n * N_PEERS,))]
                   + [pltpu.HBM(s.shape, s.dtype) for s in srcs] + [pltpu.HBM(l.shape, l.dtype) for l in lands]
                   + [jax.ShapeDtypeStruct((SUBLANES, LANES), F32)]),
        in_specs=[_HBM] * (2 * n) + [pl.BlockSpec(memory_space=pl.ANY)],
        out_specs=[_SEM, _SEM] + [_HBM] * (2 * n) + [pl.BlockSpec(memory_space=pltpu.VMEM)],
        input_output_aliases={j: 2 + j for j in range(2 * n)},
        compiler_params=pltpu.CompilerParams(has_side_effects=_SIDE_EFFECT),
    )(*[pltpu.with_memory_space_constraint(s, pltpu.HBM) for s in srcs], *lands, after)
    return out[0], out[1], list(out[2:2 + n]), list(out[2 + n:2 + 2 * n]), out[-1]


def _exchange_wait_call(started, after, scatter, name):
    send_sems, recv_sems, srcs, lands, _ = started
    n = len(srcs)

    def body(*refs):
        src_refs, land_refs = refs[:n], refs[n:2 * n]
        send_s, recv_s = refs[2 * n], refs[2 * n + 1]
        for i in range(n):
            for k in range(1, N_DEV):
                cp = _exchange_copy(src_refs[i], land_refs[i], send_s, recv_s, i, k, scatter, True)
                cp.wait_send()
                cp.wait_recv()

    out = pl.pallas_call(
        body,
        name=name,
        out_shape=[pltpu.HBM(s.shape, s.dtype) for s in srcs] + [pltpu.HBM(l.shape, l.dtype) for l in lands],
        in_specs=[_HBM] * (2 * n) + [_SEM, _SEM, pl.BlockSpec(memory_space=pl.ANY)],
        out_specs=[_HBM] * (2 * n),
        input_output_aliases={j: j for j in range(2 * n)},
        compiler_params=pltpu.CompilerParams(has_side_effects=_SIDE_EFFECT),
    )(*srcs, *lands, send_sems, recv_sems, after)
    return list(out[n:])


def _own_slot(land, own):
    x, y, c = _place()
    return lax.dynamic_update_index_in_dim(land, own, 4 * x + 2 * y + c, 0)


def _slot_sum_call(land, name):
    rows, cols = _rows_cols(land.shape[1:])
    tr = _pick_tile(rows, (256, 128, 64, 32, 16))

    def body(land_ref, o_ref):
        acc = land_ref[0].astype(F32)
        for s in range(1, N_DEV):
            acc = acc + land_ref[s].astype(F32)
        o_ref[...] = acc

    out = pl.pallas_call(
        body,
        name=name,
        grid=(rows // tr,),
        in_specs=[pl.BlockSpec((N_DEV, tr, cols), lambda i: (0, i, 0))],
        out_specs=pl.BlockSpec((tr, cols), lambda i: (i, 0)),
        out_shape=jax.ShapeDtypeStruct((rows, cols), F32),
        compiler_params=_cparams("parallel"),
    )(land.reshape(N_DEV, rows, cols))
    return out.reshape(land.shape[1:])


SMALL_BLOCK_ROWS = 16


def _pack_small(parts):
    flat = jnp.concatenate([p.reshape(-1) for p in parts])
    chunk = N_DEV * SMALL_BLOCK_ROWS * COMM_LANES
    flat = jnp.pad(flat, (0, (-flat.shape[0]) % chunk))
    return flat.reshape(N_DEV, -1, COMM_LANES)


def _unpack_small(buf, shapes):
    flat = buf.reshape(-1)
    out, off = [], 0
    for shp in shapes:
        n = math.prod(shp)
        out.append(flat[off:off + n].reshape(shp))
        off += n
    return out


SHARDED = dict(w_in=2, ssm_w_glu=2, mla_w_uq=2, mla_w_ukv=2, mla_w_o=2, hg_w_o=2, w_out=1, x_w_q=1, x_w_kv=1,
               x_w_o=2, ffn_w_gate_up=2, ffn_w_down=1)
REPLICATED = ("norm_mix", "ssm_lam_re", "ssm_lam_im", "ssm_b_re", "ssm_b_im", "ssm_c_re", "ssm_c_im", "ssm_d",
              "ssm_log_step", "mla_q_norm", "mla_kv_norm", "hg_lb", "hg_g_norm", "norm_cross", "norm_mem", "norm_ffn",
              "norm_final")


def _join_shards(stacked, axis):
    n, l, a, b = stacked.shape
    if axis == 1:
        return stacked.transpose(1, 0, 2, 3).reshape(l, n * a, b)
    return stacked.transpose(1, 2, 0, 3).reshape(l, a, n * b)


def _split_shards(full, axis):
    l, a, b = full.shape
    if axis == 1:
        return full.reshape(l, N_DEV, a // N_DEV, b).transpose(1, 0, 2, 3)
    return full.reshape(l, a, N_DEV, b // N_DEV).transpose(2, 0, 1, 3)


def _layer_fn(l, mem, tabs):
    def f(x, full, small, lower):
        m = _layer_matrices(dict(zip(SHARDED, full)), 0)
        return _layer(x, mem, tabs, m, dict(zip(REPLICATED, small)), l, lower[l].reshape(1, -1))
    return f


ADAM_LR, ADAM_B1, ADAM_B2, ADAM_EPS, ADAM_WD, ADAM_STEP = 0.001, 0.9, 0.999, 1e-08, 0.01, 10


def _adamw_call(w, g, m, v, name):
    shape = w.shape
    cols = shape[-1]
    rows = math.prod(shape[:-1]) if len(shape) > 1 else 1
    tr = _pick_tile(rows, (512, 256, 128, 64, 32, 16, 8))

    def body(w_ref, g_ref, m_ref, v_ref, d_ref, nm_ref, nv_ref):
        gg = g_ref[...]
        m_new = ADAM_B1 * m_ref[...] + (1.0 - ADAM_B1) * gg
        v_new = ADAM_B2 * v_ref[...] + (1.0 - ADAM_B2) * jnp.square(gg)
        m_hat = m_new / (1.0 - ADAM_B1 ** ADAM_STEP)
        v_hat = v_new / (1.0 - ADAM_B2 ** ADAM_STEP)
        d_ref[...] = -ADAM_LR * (m_hat / (jnp.sqrt(v_hat) + ADAM_EPS) + ADAM_WD * w_ref[...])
        nm_ref[...] = m_new
        nv_ref[...] = v_new

    spec = pl.BlockSpec((tr, cols), lambda i: (i, 0))
    outs = pl.pallas_call(
        body, name=name, grid=(rows // tr,), in_specs=[spec] * 4, out_specs=[spec] * 3,
        out_shape=[jax.ShapeDtypeStruct((rows, cols), F32)] * 3, compiler_params=_cparams("parallel"),
    )(*(t.reshape(rows, cols) for t in (w, g, m, v)))
    return tuple(o.reshape(shape) for o in outs)


WEIGHTS = ("norm_mix", "w_in", "ssm_lam_re", "ssm_lam_im", "ssm_b_re", "ssm_b_im", "ssm_c_re", "ssm_c_im", "ssm_d",
           "ssm_log_step", "ssm_w_glu", "mla_q_norm", "mla_kv_norm", "mla_w_uq", "mla_w_ukv", "mla_w_o", "hg_lb",
           "hg_g_norm", "hg_w_o", "w_out", "norm_cross", "norm_mem", "x_w_q", "x_w_kv", "x_w_o", "norm_ffn",
           "ffn_w_gate_up", "ffn_w_down", "norm_final")


def kernel(x, mem, positions, norm_mix, w_in, ssm_lam_re, ssm_lam_im, ssm_b_re, ssm_b_im, ssm_c_re, ssm_c_im, ssm_d, ssm_log_step, ssm_w_glu, mla_q_norm, mla_kv_norm, mla_w_uq, mla_w_ukv, mla_w_o, hg_lb, hg_g_norm, hg_w_o, w_out, norm_cross, norm_mem, x_w_q, x_w_kv, x_w_o, norm_ffn, ffn_w_gate_up, ffn_w_down, norm_final, loss_target, m_norm_mix, m_w_in, m_ssm_lam_re, m_ssm_lam_im, m_ssm_b_re, m_ssm_b_im, m_ssm_c_re, m_ssm_c_im, m_ssm_d, m_ssm_log_step, m_ssm_w_glu, m_mla_q_norm, m_mla_kv_norm, m_mla_w_uq, m_mla_w_ukv, m_mla_w_o, m_hg_lb, m_hg_g_norm, m_hg_w_o, m_w_out, m_norm_cross, m_norm_mem, m_x_w_q, m_x_w_kv, m_x_w_o, m_norm_ffn, m_ffn_w_gate_up, m_ffn_w_down, m_norm_final, v_norm_mix, v_w_in, v_ssm_lam_re, v_ssm_lam_im, v_ssm_b_re, v_ssm_b_im, v_ssm_c_re, v_ssm_c_im, v_ssm_d, v_ssm_log_step, v_ssm_w_glu, v_mla_q_norm, v_mla_kv_norm, v_mla_w_uq, v_mla_w_ukv, v_mla_w_o, v_hg_lb, v_hg_g_norm, v_hg_w_o, v_w_out, v_norm_cross, v_norm_mem, v_x_w_q, v_x_w_kv, v_x_w_o, v_norm_ffn, v_ffn_w_gate_up, v_ffn_w_down, v_norm_final):
    given = dict(locals())
    weights = {n: given[n] for n in WEIGHTS}
    axes = tuple(SHARDED.values())
    small = tuple(weights[n] for n in REPLICATED)
    layer_shards = lambda l: [weights[n][l:l + 1].astype(BF16) for n in SHARDED]
    join = lambda stacked: tuple(_join_shards(p, ax) for p, ax in zip(stacked, axes))
    split = lambda cts: [_split_shards(ct, ax) for ct, ax in zip(cts, axes)]
    xs, tabs = x[0], _rope_tables(positions[0])
    lower, vjp_lower = jax.vjp(_lower_bounds, hg_lb)

    got0 = _all_gather_call(layer_shards(0), "weights_all_gather_l0")
    full0 = join(got0)
    shards1 = layer_shards(1)
    gather1 = _exchange_start_call(shards1, got0[0], False, "weights_gather_start_l1")
    xs = xs + gather1[4][0, 0]
    x1, vjp0 = jax.vjp(_layer_fn(0, mem[0], tabs), xs, full0, small, lower)
    lands = _exchange_wait_call(gather1, x1, False, "weights_gather_wait_l1")
    full1 = join([_own_slot(land, own) for land, own in zip(lands, shards1)])
    x2, vjp1 = jax.vjp(_layer_fn(1, mem[0], tabs), x1, full1, small, lower)
    loss_local, vjp_loss = jax.vjp(functools.partial(_final_loss, loss_target[0]), x2, norm_final)

    dx2, d_norm_final = vjp_loss(jnp.ones((), F32))
    dx1, dfull1, dsmall1, dlower1 = vjp1(dx2)
    gs1 = split(dfull1)
    scatter1 = _exchange_start_call(gs1, dx1, True, "grads_scatter_start_l1")
    dx1 = dx1 + scatter1[4][0, 0]
    gx, dfull0, dsmall0, dlower0 = vjp0(dx1)
    lands = _exchange_wait_call(scatter1, gx, True, "grads_scatter_wait_l1")
    me = 4 * lax.axis_index("x") + 2 * lax.axis_index("y") + lax.axis_index("c")
    g_l1 = [_slot_sum_call(_own_slot(land, lax.dynamic_index_in_dim(g, me, 0, keepdims=False)), "grads_slot_sum_l1")
            for land, g in zip(lands, gs1)]
    g_l0 = _reduce_scatter(split(dfull0), "grads_reduce_scatter_l0")
    grads = {n: jnp.concatenate([a, b], axis=0) for n, a, b in zip(SHARDED, g_l0, g_l1)}

    d_small = dict(zip(REPLICATED, (a + b for a, b in zip(dsmall0, dsmall1))))
    d_small["norm_final"] = d_small["norm_final"] + d_norm_final
    d_small["hg_lb"] = d_small["hg_lb"] + vjp_lower(dlower0 + dlower1)[0]
    shapes = [d_small[n].shape for n in REPLICATED]
    (mine,) = _reduce_scatter([_pack_small([d_small[n] for n in REPLICATED])], "small_reduce_scatter")
    (total,) = _all_gather_call([mine], "small_all_gather")
    grads.update(zip(REPLICATED, _unpack_small(total, shapes)))

    loss = lax.psum(loss_local, ("x", "y", "c"))
    steps = {n: _adamw_call(weights[n], grads[n], given["m_" + n], given["v_" + n], "adamw_" + n) for n in WEIGHTS}
    return (loss, gx[None], *[grads[n] for n in WEIGHTS], *[steps[n][0] for n in WEIGHTS],
            *[steps[n][1] for n in WEIGHTS], *[steps[n][2] for n in WEIGHTS])
```

```python
import functools
import math

import jax
import jax.numpy as jnp
from jax import lax
from jax.experimental import pallas as pl
from jax.experimental.pallas import tpu as pltpu

F32 = jnp.float32
BF16 = jnp.bfloat16

VMEM_LIMIT_BYTES = 48 * 1024 * 1024
LANES = 128
SUBLANES = 8


def _cparams(*sem):
    return pltpu.CompilerParams(dimension_semantics=sem, vmem_limit_bytes=VMEM_LIMIT_BYTES)


def _pick_tile(n, cands):
    for c in cands:
        if n % c == 0:
            return c
    return n


MM_VMEM_BUDGET = 38 * 1024 * 1024
MM_STEP_US = 0.35
HBM_BYTES_PER_US = 3.0e6
VREG_RMW_PER_US = 1.5e3


def _divisor_tiles(dim, cands):
    out = [t for t in cands if dim % t == 0]
    return out or [dim]


def _mm_tiles(m, n, k, sa, sb, so):
    tms = _divisor_tiles(m, (1024, 512, 256, 128, 64, 32, 16, 8))[:2]
    tns = _divisor_tiles(n, (2048, 1536, 1408, 1024, 768, 512, 384, 256, 128))
    tks = [k // d for d in (1, 2, 4, 8, 13, 16, 26, 32, 52) if k % d == 0 and (k // d) % LANES == 0] or [k]
    best = None
    for tk in tks:
        nk = k // tk
        for tm in tms:
            for tn in tns:
                vmem = 2 * (tm * tk * sa + tk * tn * sb + tm * tn * so) + (tm * tn * 4 if nk > 1 else 0)
                if vmem > MM_VMEM_BUDGET:
                    continue
                steps = (m // tm) * (n // tn) * nk
                a_reads = m * k * sa * (n // tn if nk > 1 else 1)
                b_reads = k * n * sb * (m // tm if (nk > 1 or n // tn > 1) else 1)
                cost = (steps * MM_STEP_US + (a_reads + b_reads) / HBM_BYTES_PER_US
                        + (m * n * nk / 1024 / VREG_RMW_PER_US if nk > 1 else 0.0))
                if best is None or cost < best[0]:
                    best = (cost, tm, tn, tk)
    assert best is not None, (m, n, k)
    return best[1:]


def _mm_tiles_cached_t(m, n, k, sa, sb, so):
    for tm in _divisor_tiles(m, (1024, 512, 256, 128)):
        for tn in _divisor_tiles(n, (1024, 512, 384, 256, 128)):
            if 2 * (k * tm * sa + k * tn * sb + tm * tn * so) + tm * k * 2 <= MM_VMEM_BUDGET:
                return tm, tn
    return None


def _mm_tn_cached_call(a, b, tiles, out_dtype, name):
    k, m = a.shape
    n = b.shape[1]
    tm, tn = tiles

    def body(a_ref, b_ref, o_ref, at_ref):
        @pl.when(pl.program_id(1) == 0)
        def _():
            at_ref[...] = a_ref[...].astype(BF16).T

        o_ref[...] = lax.dot_general(at_ref[...], b_ref[...].astype(BF16), _NN_DIMS,
                                     preferred_element_type=F32).astype(out_dtype)

    return pl.pallas_call(
        body,
        name=name,
        grid=(m // tm, n // tn),
        in_specs=[pl.BlockSpec((k, tm), lambda i, j: (0, i)), pl.BlockSpec((k, tn), lambda i, j: (0, j))],
        out_specs=pl.BlockSpec((tm, tn), lambda i, j: (i, j)),
        out_shape=jax.ShapeDtypeStruct((m, n), out_dtype),
        scratch_shapes=[pltpu.VMEM((tm, k), BF16)],
        compiler_params=_cparams("parallel", "arbitrary"),
    )(a, b)


_NN_DIMS = (((1,), (0,)), ((), ()))


def _mm_call(a, b, ta, tb, add=None, out_dtype=F32, name="mm"):
    m, k = (a.shape[1], a.shape[0]) if ta else a.shape
    k2, n = (b.shape[1], b.shape[0]) if tb else b.shape
    assert k == k2, (a.shape, b.shape, ta, tb)
    sizes = (a.dtype.itemsize, b.dtype.itemsize, jnp.dtype(out_dtype).itemsize + (add.dtype.itemsize if add is not None else 0))
    if ta:
        tiles = _mm_tiles_cached_t(m, n, k, *sizes)
        if tiles is not None:
            return _mm_tn_cached_call(a, b, tiles, out_dtype, name)
    tm, tn, tk = _mm_tiles(m, n, k, *sizes)
    nk = k // tk
    a_spec = pl.BlockSpec((tk, tm), lambda i, j, kk: (kk, i)) if ta else pl.BlockSpec((tm, tk), lambda i, j, kk: (i, kk))
    b_spec = pl.BlockSpec((tn, tk), lambda i, j, kk: (j, kk)) if tb else pl.BlockSpec((tk, tn), lambda i, j, kk: (kk, j))
    o_spec = pl.BlockSpec((tm, tn), lambda i, j, kk: (i, j))
    dn = (((0 if ta else 1,), (1 if tb else 0,)), ((), ()))
    has_add = add is not None

    def body(*refs):
        a_ref, b_ref = refs[0], refs[1]
        c_ref = refs[2] if has_add else None
        o_ref = refs[3] if has_add else refs[2]
        p = lax.dot_general(a_ref[...].astype(BF16), b_ref[...].astype(BF16), dn, preferred_element_type=F32)

        def finish(r):
            if has_add:
                r = r + c_ref[...].astype(F32)
            o_ref[...] = r.astype(out_dtype)

        if nk == 1:
            finish(p)
        else:
            acc_ref = refs[-1]
            kk = pl.program_id(2)

            @pl.when(kk == 0)
            def _():
                acc_ref[...] = p

            @pl.when(kk > 0)
            def _():
                acc_ref[...] += p

            @pl.when(kk == nk - 1)
            def _():
                finish(acc_ref[...])

    in_specs = [a_spec, b_spec] + ([o_spec] if has_add else [])
    args = (a, b) + ((add,) if has_add else ())
    return pl.pallas_call(
        body,
        name=name,
        grid=(m // tm, n // tn, nk),
        in_specs=in_specs,
        out_specs=o_spec,
        out_shape=jax.ShapeDtypeStruct((m, n), out_dtype),
        scratch_shapes=[] if nk == 1 else [pltpu.VMEM((tm, tn), F32)],
        compiler_params=_cparams("parallel", "parallel", "arbitrary"),
    )(*args)


@functools.partial(jax.custom_vjp, nondiff_argnums=(2,))
def matmul(a, b, out_dtype=F32):
    return _mm_call(a, b, False, False, out_dtype=out_dtype, name="mm_fwd")


def _matmul_fwd(a, b, out_dtype):
    return matmul(a, b, out_dtype), (a, b)


def _matmul_bwd(out_dtype, res, g):
    a, b = res
    da = _mm_call(g, b, False, True, out_dtype=a.dtype, name="mm_da")
    db = _mm_call(a, g, True, False, out_dtype=b.dtype, name="mm_db")
    return da, db


matmul.defvjp(_matmul_fwd, _matmul_bwd)


@jax.custom_vjp
def matmul_add(a, b, c):
    return _mm_call(a, b, False, False, add=c, name="mm_add_fwd")


def _matmul_add_fwd(a, b, c):
    return _mm_call(a, b, False, False, add=c, name="mm_add_fwd"), (a, b)


def _matmul_add_bwd(res, g):
    a, b = res
    da = _mm_call(g, b, False, True, out_dtype=a.dtype, name="mm_da")
    db = _mm_call(a, g, True, False, out_dtype=b.dtype, name="mm_db")
    return da, db, g


matmul_add.defvjp(_matmul_add_fwd, _matmul_add_bwd)


def rowwise(f, n_rows, n_aux, tile, name, passthrough=False):
    def specs(arrs, tiled):
        out = []
        for x in arrs:
            if tiled:
                out.append(pl.BlockSpec((tile, x.shape[1]), lambda i: (i, 0)))
            else:
                out.append(pl.BlockSpec(x.shape, lambda i: (0, 0)))
        return out

    def tile_structs(args):
        rows_aux, params = args[: n_rows + n_aux], args[n_rows + n_aux:]
        return [jax.ShapeDtypeStruct((tile, x.shape[1]), x.dtype) for x in rows_aux] + [
            jax.ShapeDtypeStruct(p.shape, p.dtype) for p in params]

    def fwd_call(*args):
        s = args[0].shape[0]
        outs = jax.eval_shape(f, *tile_structs(args))
        n_in = len(args)

        def body(*refs):
            vals = [r[...] for r in refs[:n_in]]
            res = f(*vals)
            for o_ref, r in zip(refs[n_in:], res):
                o_ref[...] = r.astype(o_ref.dtype)

        return pl.pallas_call(
            body,
            name=name + "_fwd",
            grid=(s // tile,),
            in_specs=specs(args[: n_rows + n_aux], True) + specs(args[n_rows + n_aux:], False),
            out_specs=[pl.BlockSpec((tile, o.shape[1]), lambda i: (i, 0)) for o in outs],
            out_shape=[jax.ShapeDtypeStruct((s, o.shape[1]), o.dtype) for o in outs],
            compiler_params=_cparams("parallel"),
        )(*args)

    def bwd_call(args, gs):
        s = args[0].shape[0]
        rows, aux, params = args[:n_rows], args[n_rows:n_rows + n_aux], args[n_rows + n_aux:]
        n_in, n_g, n_p = len(args), len(gs), len(params)
        n_gf = n_g - 1 if passthrough else n_g

        def body(*refs):
            vals = [r[...] for r in refs[:n_in]]
            gvals = tuple(r[...] for r in refs[n_in:n_in + n_gf])
            out_refs = refs[n_in + n_g:]
            auxv = vals[n_rows:n_rows + n_aux]

            def g_(*rp):
                return tuple(f(*rp[:n_rows], *auxv, *rp[n_rows:]))

            _, vjp = jax.vjp(g_, *vals[:n_rows], *vals[n_rows + n_aux:])
            cts = list(vjp(gvals))
            if passthrough:
                cts[0] = cts[0] + refs[n_in + n_gf][...]
            for o_ref, ct in zip(out_refs[:n_rows], cts[:n_rows]):
                o_ref[...] = ct.astype(o_ref.dtype)
            if n_p:
                @pl.when(pl.program_id(0) == 0)
                def _():
                    for o_ref in out_refs[n_rows:]:
                        o_ref[...] = jnp.zeros_like(o_ref)

                for o_ref, ct in zip(out_refs[n_rows:], cts[n_rows:]):
                    o_ref[...] += ct.astype(o_ref.dtype)

        return pl.pallas_call(
            body,
            name=name + "_bwd",
            grid=(s // tile,),
            in_specs=specs(rows + aux, True) + specs(params, False) + specs(gs, True),
            out_specs=specs(rows, True) + specs(params, False),
            out_shape=[jax.ShapeDtypeStruct(x.shape, x.dtype) for x in rows + params],
            compiler_params=_cparams("arbitrary" if n_p else "parallel"),
        )(*args, *gs)

    @jax.custom_vjp
    def op(*args):
        return tuple(fwd_call(*args)) + ((args[0],) if passthrough else ())

    def op_fwd(*args):
        return op(*args), args

    def op_bwd(args, gs):
        cts = bwd_call(tuple(args), tuple(gs))
        rows_ct, par_ct = cts[:n_rows], cts[n_rows:]
        aux_ct = [jnp.zeros_like(a) for a in args[n_rows:n_rows + n_aux]]
        return tuple(rows_ct) + tuple(aux_ct) + tuple(par_ct)

    op.defvjp(op_fwd, op_bwd)
    return op


SCAN_SEGMENTS = SUBLANES
SCAN_TILE_ROWS = 512
SCAN_TILE_LANES = 512


def _scan_specs(s, n):
    tr = min(SCAN_TILE_ROWS, s)
    tn = min(SCAN_TILE_LANES, n)
    return tr, tn, s // tr, n // tn


def _scan_step(ar, ai, xr, xi, br, bi):
    return ar * xr - ai * xi + br, ar * xi + ai * xr + bi


def _scan_finals(b_r, b_i, a, reverse, name):
    s, n = b_r.shape
    tr, tn, nt, nc = _scan_specs(s, n)
    ti = tr // SUBLANES
    tmap = (lambda c, j: (nt - 1 - j, c)) if reverse else (lambda c, j: (j, c))

    def body(br_ref, bi_ref, a_ref, fr_ref, fi_ref, sr, si):
        j = pl.program_id(1)

        @pl.when(j == 0)
        def _():
            sr[...] = jnp.zeros_like(sr)
            si[...] = jnp.zeros_like(si)

        ar = jnp.broadcast_to(a_ref[0:1, :], (SUBLANES, tn))
        ai = jnp.broadcast_to(a_ref[1:2, :], (SUBLANES, tn))

        def step(ii, carry):
            i = (ti - 1 - ii) if reverse else ii
            off = pl.multiple_of(i * SUBLANES, SUBLANES)
            return _scan_step(ar, ai, carry[0], carry[1], br_ref[pl.ds(off, SUBLANES), :], bi_ref[pl.ds(off, SUBLANES), :])

        xr, xi = lax.fori_loop(0, ti, step, (sr[...], si[...]), unroll=4)
        sr[...] = xr
        si[...] = xi

        @pl.when(j == nt - 1)
        def _():
            fr_ref[...] = xr
            fi_ref[...] = xi

    bspec = pl.BlockSpec((tr, tn), tmap)
    fspec = pl.BlockSpec((SUBLANES, tn), lambda c, j: (0, c))
    return pl.pallas_call(
        body,
        name=name,
        grid=(nc, nt),
        in_specs=[bspec, bspec, pl.BlockSpec((2, tn), lambda c, j: (0, c))],
        out_specs=[fspec, fspec],
        out_shape=[jax.ShapeDtypeStruct((SUBLANES, n), F32)] * 2,
        scratch_shapes=[pltpu.VMEM((SUBLANES, tn), F32)] * 2,
        compiler_params=_cparams("parallel", "arbitrary"),
    )(b_r, b_i, a)


def _scan_states(b_r, b_i, a, f_r, f_i, reverse, xs, name):
    s, n = b_r.shape
    tr, tn, nt, nc = _scan_specs(s, n)
    ti = tr // SUBLANES
    seg_len = s // SCAN_SEGMENTS
    assert seg_len & (seg_len - 1) == 0
    with_acc = xs is not None
    tmap = (lambda c, j: (nt - 1 - j, c)) if reverse else (lambda c, j: (j, c))
    order = list(range(SCAN_SEGMENTS))[::-1] if reverse else list(range(SCAN_SEGMENTS))

    def body(*refs):
        br_ref, bi_ref, a_ref, fr_ref, fi_ref = refs[:5]
        pos = 5
        if with_acc:
            xr_ref, xi_ref = refs[5:7]
            pos = 7
        or_ref, oi_ref = refs[pos:pos + 2]
        pos += 2
        if with_acc:
            dr_ref, di_ref = refs[pos:pos + 2]
            pos += 2
        sr, si = refs[pos:pos + 2]
        if with_acc:
            accr, acci = refs[pos + 2:pos + 4]
        j = pl.program_id(1)
        a_r1, a_i1 = a_ref[0:1, :], a_ref[1:2, :]

        @pl.when(j == 0)
        def _():
            pr, pi = a_r1, a_i1
            for _ in range(seg_len.bit_length() - 1):
                pr, pi = pr * pr - pi * pi, 2.0 * pr * pi
            cr = jnp.zeros((1, tn), F32)
            ci = jnp.zeros((1, tn), F32)
            for idx, k in enumerate(order):
                if idx > 0:
                    kp = order[idx - 1]
                    cr, ci = (fr_ref[kp:kp + 1, :] + pr * cr - pi * ci, fi_ref[kp:kp + 1, :] + pr * ci + pi * cr)
                sr[k:k + 1, :] = cr
                si[k:k + 1, :] = ci
            if with_acc:
                accr[...] = jnp.zeros_like(accr)
                acci[...] = jnp.zeros_like(acci)

        ar = jnp.broadcast_to(a_r1, (SUBLANES, tn))
        ai = jnp.broadcast_to(a_i1, (SUBLANES, tn))

        def step(ii, carry):
            i = (ti - 1 - ii) if reverse else ii
            off = pl.multiple_of(i * SUBLANES, SUBLANES)
            rows = pl.ds(off, SUBLANES)
            xr, xi = carry[0], carry[1]
            if with_acc:
                zr, zi = xr_ref[rows, :], xi_ref[rows, :]
                acc = (carry[2] + xr * zr + xi * zi, carry[3] + xi * zr - xr * zi)
            nr, ni = _scan_step(ar, ai, xr, xi, br_ref[rows, :], bi_ref[rows, :])
            or_ref[rows, :] = nr
            oi_ref[rows, :] = ni
            return (nr, ni) + (acc if with_acc else ())

        init = (sr[...], si[...]) + ((accr[...], acci[...]) if with_acc else ())
        out = lax.fori_loop(0, ti, step, init, unroll=4)
        sr[...] = out[0]
        si[...] = out[1]
        if with_acc:
            accr[...] = out[2]
            acci[...] = out[3]

            @pl.when(j == nt - 1)
            def _():
                dr_ref[...] = jnp.sum(out[2], axis=0, keepdims=True)
                di_ref[...] = jnp.sum(out[3], axis=0, keepdims=True)

    bspec = pl.BlockSpec((tr, tn), tmap)
    fspec = pl.BlockSpec((SUBLANES, tn), lambda c, j: (0, c))
    dspec = pl.BlockSpec((1, tn), lambda c, j: (0, c))
    in_specs = [bspec, bspec, pl.BlockSpec((2, tn), lambda c, j: (0, c)), fspec, fspec] + ([bspec, bspec] if with_acc else [])
    out_specs = [bspec, bspec] + ([dspec, dspec] if with_acc else [])
    out_shape = [jax.ShapeDtypeStruct((s, n), F32)] * 2 + ([jax.ShapeDtypeStruct((1, n), F32)] * 2 if with_acc else [])
    scratch = [pltpu.VMEM((SUBLANES, tn), F32)] * (4 if with_acc else 2)
    args = (b_r, b_i, a, f_r, f_i) + (tuple(xs) if with_acc else ())
    return pl.pallas_call(
        body,
        name=name,
        grid=(nc, nt),
        in_specs=in_specs,
        out_specs=out_specs,
        out_shape=out_shape,
        scratch_shapes=scratch,
        compiler_params=_cparams("parallel", "arbitrary"),
    )(*args)


@jax.custom_vjp
def s5_scan(b_r, b_i, a):
    f_r, f_i = _scan_finals(b_r, b_i, a, False, "s5_scan_fin")
    return tuple(_scan_states(b_r, b_i, a, f_r, f_i, False, None, "s5_scan"))


def _s5_scan_fwd(b_r, b_i, a):
    xs = s5_scan(b_r, b_i, a)
    return xs, (a, xs)


def _s5_scan_bwd(res, g):
    a, xs = res
    a_conj = a * jnp.array([[1.0], [-1.0]], F32)
    f_r, f_i = _scan_finals(g[0], g[1], a_conj, True, "s5_rscan_fin")
    g_r, g_i, da_r, da_i = _scan_states(g[0], g[1], a_conj, f_r, f_i, True, xs, "s5_rscan")
    return g_r, g_i, jnp.concatenate([da_r, da_i], axis=0)


s5_scan.defvjp(_s5_scan_fwd, _s5_scan_bwd)


BD_SEG_ROWS = 128


def _bd_view(x, seg):
    return x if seg else x.reshape(SCAN_SEGMENTS, x.shape[0] // SCAN_SEGMENTS, x.shape[1])


def _bd_spec(seg, w, rows):
    if seg:
        return pl.BlockSpec((SCAN_SEGMENTS * rows, w), lambda i, j: (i, j))
    return pl.BlockSpec((SCAN_SEGMENTS, rows, w), lambda i, j: (0, i, j))


def _bd_apply_call(a, w, transpose_w, a_seg, out_seg, add, name):
    s = a.shape[0]
    nb, ka, kn = w.shape
    wi, wo = (kn, ka) if transpose_w else (ka, kn)
    convert = a_seg != out_seg
    a_seg_k, out_seg_k = (a_seg, out_seg) if convert else (True, True)
    ov_shape = (s, nb * wo) if out_seg_k else (SCAN_SEGMENTS, s // SCAN_SEGMENTS, nb * wo)
    dn = _NT_DIMS if transpose_w else _NN_DIMS
    has_add = add is not None
    rows = min(BD_SEG_ROWS, s // SCAN_SEGMENTS)

    assert not convert or (wo if a_seg_k else wi) == LANES

    def body(*refs):
        a_ref, w_ref = refs[0], refs[1]
        c_ref = refs[2] if has_add else None
        o_ref = refs[3] if has_add else refs[2]
        wb = w_ref[0].astype(BF16)
        mm = lambda a_val: lax.dot_general(a_val.astype(BF16), wb, dn, preferred_element_type=F32)
        if not convert:
            r = mm(a_ref[...])
            o_ref[...] = r + c_ref[...] if has_add else r
        elif out_seg_k:
            scr = refs[-1]
            for k in range(SCAN_SEGMENTS):
                scr[pl.ds(k, rows, stride=SCAN_SEGMENTS), :] = a_ref[k]
            r = mm(scr[...])
            o_ref[...] = r + c_ref[...] if has_add else r
        else:
            scr = refs[-1]
            scr[...] = mm(a_ref[...])
            for k in range(SCAN_SEGMENTS):
                r = scr[pl.ds(k, rows, stride=SCAN_SEGMENTS), :]
                o_ref[k] = r + c_ref[k] if has_add else r

    args = [_bd_view(a, a_seg_k), w] + ([_bd_view(add, out_seg_k)] if has_add else [])
    in_specs = ([_bd_spec(a_seg_k, wi, rows), pl.BlockSpec((1, ka, kn), lambda i, j: (j, 0, 0))]
                + ([_bd_spec(out_seg_k, wo, rows)] if has_add else []))
    out = pl.pallas_call(
        body,
        name=name,
        grid=(s // (SCAN_SEGMENTS * rows), nb),
        in_specs=in_specs,
        out_specs=_bd_spec(out_seg_k, wo, rows),
        out_shape=jax.ShapeDtypeStruct(ov_shape, F32),
        scratch_shapes=[pltpu.VMEM((SCAN_SEGMENTS * rows, LANES), F32)] if convert else [],
        compiler_params=_cparams("parallel", "parallel"),
    )(*args)
    return out.reshape(s, nb * wo)


def _bd_weight_grad_call(a, g, a_seg, g_seg, nb, name):
    s = a.shape[0]
    ka, kn = a.shape[1] // nb, g.shape[1] // nb
    seg_len = s // SCAN_SEGMENTS
    convert = a_seg != g_seg

    def view(x, seg, w):
        if not convert:
            return x, pl.BlockSpec((s, w), lambda j: (0, j))
        if seg:
            return x, pl.BlockSpec((s, w), lambda j: (0, j))
        return x.reshape(SCAN_SEGMENTS, seg_len, x.shape[1]), pl.BlockSpec((SCAN_SEGMENTS, seg_len, w), lambda j: (0, 0, j))

    av, a_spec = view(a, a_seg, ka)
    gv, g_spec = view(g, g_seg, kn)

    assert not convert or (kn if a_seg else ka) == LANES

    def body(a_ref, g_ref, o_ref, *scratch):
        tn = lambda x, y: lax.dot_general(x.astype(BF16), y.astype(BF16), _TN_DIMS, preferred_element_type=F32)
        if not convert:
            o_ref[0] = tn(a_ref[...], g_ref[...])
        else:
            scr = scratch[0]
            t_ref = g_ref if a_seg else a_ref
            for k in range(SCAN_SEGMENTS):
                scr[pl.ds(k, seg_len, stride=SCAN_SEGMENTS), :] = t_ref[k]
            o_ref[0] = tn(a_ref[...], scr[...]) if a_seg else tn(scr[...], g_ref[...])

    return pl.pallas_call(
        body,
        name=name,
        grid=(nb,),
        in_specs=[a_spec, g_spec],
        out_specs=pl.BlockSpec((1, ka, kn), lambda j: (j, 0, 0)),
        out_shape=jax.ShapeDtypeStruct((nb, ka, kn), F32),
        scratch_shapes=[pltpu.VMEM((s, LANES), F32)] if convert else [],
        compiler_params=_cparams("parallel"),
    )(av, gv)


_NT_DIMS = (((1,), (1,)), ((), ()))
_TN_DIMS = (((0,), (0,)), ((), ()))


@functools.partial(jax.custom_vjp, nondiff_argnums=(2, 3))
def bd_matmul(a, w, a_seg, out_seg):
    return _bd_apply_call(a, w, False, a_seg, out_seg, None, "bd_mm_fwd")


def _bd_matmul_fwd(a, w, a_seg, out_seg):
    return bd_matmul(a, w, a_seg, out_seg), (a, w)


def _bd_matmul_bwd(a_seg, out_seg, res, g):
    a, w = res
    da = _bd_apply_call(g, w, True, out_seg, a_seg, None, "bd_mm_da")
    dw = _bd_weight_grad_call(a, g, a_seg, out_seg, w.shape[0], "bd_mm_dw")
    return da, dw


bd_matmul.defvjp(_bd_matmul_fwd, _bd_matmul_bwd)


@functools.partial(jax.custom_vjp, nondiff_argnums=(3, 4))
def bd_matmul_add(a, w, c, a_seg, out_seg):
    return _bd_apply_call(a, w, False, a_seg, out_seg, c, "bd_mm_add_fwd")


def _bd_matmul_add_fwd(a, w, c, a_seg, out_seg):
    return bd_matmul_add(a, w, c, a_seg, out_seg), (a, w)


def _bd_matmul_add_bwd(a_seg, out_seg, res, g):
    return _bd_matmul_bwd(a_seg, out_seg, res, g) + (g,)


bd_matmul_add.defvjp(_bd_matmul_add_fwd, _bd_matmul_add_bwd)


_NN = (((1,), (0,)), ((), ()))
_NT = (((1,), (1,)), ((), ()))
_TN = (((0,), (0,)), ((), ()))


def _dot(a, b, dn):
    return lax.dot_general(a.astype(BF16), b.astype(BF16), dn, preferred_element_type=F32)


@jax.custom_vjp
def bdot_nn(a, b):
    return _dot(a, b, _NN)


bdot_nn.defvjp(lambda a, b: (_dot(a, b, _NN), (a, b)),
               lambda r, g: (_dot(g, r[1], _NT).astype(r[0].dtype), _dot(r[0], g, _TN).astype(r[1].dtype)))


@jax.custom_vjp
def bdot_nt(a, b):
    return _dot(a, b, _NT)


bdot_nt.defvjp(lambda a, b: (_dot(a, b, _NT), (a, b)),
               lambda r, g: (_dot(g, r[1], _NN).astype(r[0].dtype), _dot(g, r[0], _TN).astype(r[1].dtype)))


@jax.custom_vjp
def bdot_tn(a, b):
    return _dot(a, b, _TN)


bdot_tn.defvjp(lambda a, b: (_dot(a, b, _TN), (a, b)),
               lambda r, g: (_dot(r[1], g, _NT).astype(r[0].dtype), _dot(r[0], g, _NN).astype(r[1].dtype)))


def _split3(x):
    h = x.astype(BF16)
    r = x - h.astype(F32)
    m = r.astype(BF16)
    l = (r - m.astype(F32)).astype(BF16)
    return h, m, l


def _exact_dot(t, x, dn):
    h, m, l = _split3(x)
    d = lambda p: lax.dot_general(t, p, dn, preferred_element_type=F32)
    return d(h) + d(m) + d(l)


@jax.custom_vjp
def select_dot(t, x):
    return _exact_dot(t, x, _NN)


select_dot.defvjp(lambda t, x: (_exact_dot(t, x, _NN), t),
                  lambda t, g: (jnp.zeros_like(t), _exact_dot(t, g, _TN)))


def _split_rows_impl(x, h):
    return tuple(x[i * h:(i + 1) * h] for i in range(x.shape[0] // h))


@functools.partial(jax.custom_vjp, nondiff_argnums=(1,))
def split_rows(x, h):
    return _split_rows_impl(x, h)


split_rows.defvjp(lambda x, h: (_split_rows_impl(x, h), None),
                  lambda h, r, g: (jnp.concatenate(g, axis=0),))


@jax.custom_vjp
def join_rows(parts):
    return jnp.concatenate(parts, axis=0)


def _join_rows_bwd(hs, g):
    out, off = [], 0
    for h in hs:
        out.append(g[off:off + h])
        off += h
    return (tuple(out),)


join_rows.defvjp(lambda parts: (jnp.concatenate(parts, axis=0), tuple(p.shape[0] for p in parts)), _join_rows_bwd)


def _split_lanes_impl(x, w):
    return tuple(x[:, i * w:(i + 1) * w] for i in range(x.shape[1] // w))


@functools.partial(jax.custom_vjp, nondiff_argnums=(1,))
def split_lanes(x, w):
    return _split_lanes_impl(x, w)


split_lanes.defvjp(lambda x, w: (_split_lanes_impl(x, w), None),
                   lambda w, r, g: (jnp.concatenate(g, axis=1),))


def _join_impl(parts):
    return jnp.concatenate(parts, axis=1)


@jax.custom_vjp
def join_lanes(parts):
    return _join_impl(parts)


def _join_bwd(ws, g):
    out, off = [], 0
    for w in ws:
        out.append(g[:, off:off + w])
        off += w
    return (tuple(out),)


join_lanes.defvjp(lambda parts: (_join_impl(parts), tuple(p.shape[1] for p in parts)), _join_bwd)


def _rope_impl(x, c, sa, sb, shift):
    w = x.shape[1]
    return x * c + pltpu.roll(x, w - shift, 1) * sa + pltpu.roll(x, shift, 1) * sb


@functools.partial(jax.custom_vjp, nondiff_argnums=(4,))
def rope_lanes(x, c, sa, sb, shift):
    return _rope_impl(x, c, sa, sb, shift)


def _rope_bwd(shift, r, g):
    c, sa, sb = r
    w = g.shape[1]
    dx = g * c + pltpu.roll(g * sa, shift, 1) + pltpu.roll(g * sb, w - shift, 1)
    return dx, jnp.zeros_like(c), jnp.zeros_like(sa), jnp.zeros_like(sb)


rope_lanes.defvjp(lambda x, c, sa, sb, shift: (_rope_impl(x, c, sa, sb, shift), (c, sa, sb)), _rope_bwd)


RMS_EPS = 1e-6


def _rms(x, g):
    return x * lax.rsqrt(jnp.mean(x * x, axis=-1, keepdims=True) + RMS_EPS) * g


ATTN_BLOCK = 512
MASK_VALUE = -1e30
LOG2E = math.log2(math.e)
LN2 = math.log(2.0)
V_ONES_LANE = 64


def _causal_mask(t):
    r = lax.broadcasted_iota(jnp.int32, (t, t), 0)
    c = lax.broadcasted_iota(jnp.int32, (t, t), 1)
    return c <= r


def _attn_fwd_call(q, k, v):
    s, width = q.shape
    n_heads = width // LANES
    tq = min(ATTN_BLOCK, s)
    nq = s // tq

    def body(q_ref, k_ref, v_ref, o_ref, lse_ref):
        i = pl.program_id(1)
        qb = q_ref[...].astype(BF16)
        ones_lane = lax.broadcasted_iota(jnp.int32, (tq, LANES), 1) == V_ONES_LANE

        def block(kb, carry, masked):
            m, acc = carry
            rows = pl.ds(pl.multiple_of(kb * tq, tq), tq)
            sc = lax.dot_general(qb, k_ref[rows, :].astype(BF16), _NT, preferred_element_type=F32)
            if masked:
                sc = jnp.where(_causal_mask(tq), sc, MASK_VALUE)
            m_new = jnp.maximum(m, jnp.max(sc, axis=-1, keepdims=True))
            p = jnp.exp2(sc - m_new).astype(BF16)
            vb = jnp.where(ones_lane, 1.0, v_ref[rows, :]).astype(BF16)
            acc = jnp.exp2(m - m_new) * acc + lax.dot_general(p, vb, _NN, preferred_element_type=F32)
            return m_new, acc

        init = (jnp.full((tq, 1), MASK_VALUE, F32), jnp.zeros((tq, LANES), F32))
        carry = lax.fori_loop(0, i, lambda kb, c: block(kb, c, False), init)
        m, acc = block(i, carry, True)
        l = jnp.sum(jnp.where(ones_lane, acc, 0.0), axis=-1, keepdims=True)
        o_ref[...] = jnp.where(ones_lane, 0.0, acc / l).astype(o_ref.dtype)
        lse_ref[...] = jnp.broadcast_to(m + jnp.log2(l), (tq, LANES))

    qspec = pl.BlockSpec((tq, LANES), lambda h, i: (i, h))
    kspec = pl.BlockSpec((s, LANES), lambda h, i: (0, h))
    return pl.pallas_call(
        body,
        name="mla_attn_fwd",
        grid=(n_heads, nq),
        in_specs=[qspec, kspec, kspec],
        out_specs=[qspec, qspec],
        out_shape=[jax.ShapeDtypeStruct((s, width), BF16), jax.ShapeDtypeStruct((s, width), F32)],
        compiler_params=_cparams("parallel", "parallel"),
    )(q, k, v)


def _attn_bwd_call(q, k, v, o, lse, do):
    s, width = q.shape
    n_heads = width // LANES
    tq = min(ATTN_BLOCK, s)
    nq = s // tq

    def body(q_ref, k_ref, v_ref, o_ref, lse_ref, do_ref, dq_ref, dk_ref, dv_ref, dq_acc):
        j = pl.program_id(1)

        @pl.when(j == 0)
        def _():
            dq_acc[...] = jnp.zeros_like(dq_acc)

        kb = k_ref[...].astype(BF16)
        vb = v_ref[...].astype(BF16)

        def block(i, carry, masked):
            dk, dv = carry
            rows = pl.ds(pl.multiple_of(i * tq, tq), tq)
            qi = q_ref[rows, :].astype(BF16)
            doi = do_ref[rows, :].astype(F32)
            delta = jnp.sum(doi * o_ref[rows, :].astype(F32), axis=-1, keepdims=True)
            sc = lax.dot_general(qi, kb, _NT, preferred_element_type=F32)
            if masked:
                sc = jnp.where(_causal_mask(tq), sc, MASK_VALUE)
            p = jnp.exp2(sc - lse_ref[rows, 0:1])
            dob = doi.astype(BF16)
            dv = dv + lax.dot_general(p.astype(BF16), dob, _TN, preferred_element_type=F32)
            dp = lax.dot_general(dob, vb, _NT, preferred_element_type=F32)
            ds = (p * (dp - delta)).astype(BF16)
            dq_acc[rows, :] += lax.dot_general(ds, kb, _NN, preferred_element_type=F32)
            dk = dk + lax.dot_general(ds, qi, _TN, preferred_element_type=F32)
            return dk, dv

        zero = jnp.zeros((tq, LANES), F32)
        carry = block(j, (zero, zero), True)
        dk, dv = lax.fori_loop(j + 1, nq, lambda i, c: block(i, c, False), carry)
        dk_ref[...] = (dk * LN2).astype(dk_ref.dtype)
        dv_ref[...] = dv.astype(dv_ref.dtype)

        @pl.when(j == nq - 1)
        def _():
            dq_ref[...] = (dq_acc[...] * LN2).astype(dq_ref.dtype)

    full = pl.BlockSpec((s, LANES), lambda h, j: (0, h))
    blk = pl.BlockSpec((tq, LANES), lambda h, j: (j, h))
    return pl.pallas_call(
        body,
        name="mla_attn_bwd",
        grid=(n_heads, nq),
        in_specs=[full, blk, blk, full, full, full],
        out_specs=[full, blk, blk],
        out_shape=[jax.ShapeDtypeStruct((s, width), t.dtype) for t in (q, k, v)],
        scratch_shapes=[pltpu.VMEM((s, LANES), F32)],
        compiler_params=_cparams("parallel", "arbitrary"),
    )(q, k, v, o, lse, do)


@jax.custom_vjp
def causal_attention(q, k, v):
    return _attn_fwd_call(q, k, v)[0]


def _causal_attention_fwd(q, k, v):
    o, lse = _attn_fwd_call(q, k, v)
    return o, (q, k, v, o, lse)


def _causal_attention_bwd(res, do):
    return tuple(_attn_bwd_call(*res, do))


causal_attention.defvjp(_causal_attention_fwd, _causal_attention_bwd)


HG_HEADS = 4
HG_CHUNK = 32
HG_REF_ROW = HG_CHUNK // 2 - 1
HG_TILE_ROWS = 256
HG_EXP_CLAMP = 80.0


def _hg_tile_masks(t):
    shift = HG_CHUNK.bit_length() - 1
    r = lax.broadcasted_iota(jnp.int32, (t, t), 0)
    c = lax.broadcasted_iota(jnp.int32, (t, t), 1)
    start = lax.shift_left(lax.shift_right_logical(r, shift), shift)
    causal = (c >= start) & (c <= r)
    return causal, c == start + HG_REF_ROW, c == start + (HG_CHUNK - 1)


def _hg_tile(q, fl, v, lb, st):
    t = q.shape[0]
    causal, ref_sel, last_sel = _hg_tile_masks(t)
    f = lb + (1.0 - lb) * jax.nn.sigmoid(fl)
    kk = 1.0 - f
    qs = q * jax.nn.sigmoid(q)
    b = select_dot(causal.astype(BF16), jnp.log(f))
    b_ref = select_dot(ref_sel.astype(BF16), b)
    b_last = select_dot(last_sel.astype(BF16), b)
    q_in = qs * jnp.exp(jnp.minimum(b - b_ref, HG_EXP_CLAMP))
    k_in = kk * jnp.exp(jnp.minimum(b_ref - b, HG_EXP_CLAMP))
    o = bdot_nn(jnp.where(causal, bdot_nt(q_in, k_in), 0.0), v)
    q_hat = split_rows(qs * jnp.exp(b), HG_CHUNK)
    k_hat = split_rows(kk * jnp.exp(b_last - b), HG_CHUNK)
    decay = split_rows(jnp.exp(b_last), HG_CHUNK)
    vs = split_rows(v, HG_CHUNK)
    first_row = lax.broadcasted_iota(jnp.int32, (HG_CHUNK, LANES), 0) == 0
    inter = []
    for c in range(t // HG_CHUNK):
        inter.append(bdot_nt(q_hat[c], st))
        st = st * jnp.sum(jnp.where(first_row, decay[c], 0.0), axis=0, keepdims=True) + bdot_tn(vs[c], k_hat[c])
    return o + join_rows(tuple(inter)), st


def _hg_head(q, fl, v, gate, lb, gn, st):
    o, st = _hg_tile(q, fl, v, lb, st)
    return _rms(o, gn) * (gate * jax.nn.sigmoid(gate)), st


HG_PARTS = 4


def _hg_part_slices(h, width):
    return [slice(p * width + h * LANES, p * width + (h + 1) * LANES) for p in range(HG_PARTS)]


def _hg_fwd_call(x, lb, gn):
    s = x.shape[0]
    width = x.shape[1] // HG_PARTS
    tr = min(HG_TILE_ROWS, s)
    nt = s // tr

    def body(x_ref, lb_ref, gn_ref, o_ref, sts_ref, st_ref):
        @pl.when(pl.program_id(0) == 0)
        def _():
            st_ref[...] = jnp.zeros_like(st_ref)

        for h in range(HG_HEADS):
            ln = slice(h * LANES, (h + 1) * LANES)
            st = st_ref[h]
            sts_ref[0, h] = st
            o, st_new = _hg_head(*(x_ref[:, sl] for sl in _hg_part_slices(h, width)), lb_ref[:, ln], gn_ref[...], st)
            o_ref[:, ln] = o.astype(o_ref.dtype)
            st_ref[h] = st_new

    const = lambda shape: pl.BlockSpec(shape, lambda j: (0, 0))
    return pl.pallas_call(
        body,
        name="hgrn2_fwd",
        grid=(nt,),
        in_specs=[pl.BlockSpec((tr, HG_PARTS * width), lambda j: (j, 0)), const((1, width)), const((1, LANES))],
        out_specs=[pl.BlockSpec((tr, width), lambda j: (j, 0)),
                   pl.BlockSpec((1, HG_HEADS, LANES, LANES), lambda j: (j, 0, 0, 0))],
        out_shape=[jax.ShapeDtypeStruct((s, width), BF16),
                   jax.ShapeDtypeStruct((nt, HG_HEADS, LANES, LANES), F32)],
        scratch_shapes=[pltpu.VMEM((HG_HEADS, LANES, LANES), F32)],
        compiler_params=_cparams("arbitrary"),
    )(x, lb, gn)


def _hg_bwd_call(x, lb, gn, sts, do):
    s = x.shape[0]
    width = x.shape[1] // HG_PARTS
    tr = min(HG_TILE_ROWS, s)
    nt = s // tr

    def body(x_ref, lb_ref, gn_ref, sts_ref, do_ref, dx_ref, dlb_ref, dgn_ref, dst_ref):
        @pl.when(pl.program_id(0) == 0)
        def _():
            dst_ref[...] = jnp.zeros_like(dst_ref)
            dlb_ref[...] = jnp.zeros_like(dlb_ref)
            dgn_ref[...] = jnp.zeros_like(dgn_ref)

        for h in range(HG_HEADS):
            ln = slice(h * LANES, (h + 1) * LANES)
            parts = _hg_part_slices(h, width)
            _, vjp = jax.vjp(_hg_head, *(x_ref[:, sl] for sl in parts), lb_ref[:, ln], gn_ref[...], sts_ref[0, h])
            cts = vjp((do_ref[:, ln].astype(F32), dst_ref[h]))
            for sl, ct in zip(parts, cts[:HG_PARTS]):
                dx_ref[:, sl] = ct.astype(dx_ref.dtype)
            dlb_ref[:, ln] += cts[HG_PARTS]
            dgn_ref[...] += cts[HG_PARTS + 1]
            dst_ref[h] = cts[HG_PARTS + 2]

    rev = lambda w: pl.BlockSpec((tr, w), lambda j: (nt - 1 - j, 0))
    const = lambda shape: pl.BlockSpec(shape, lambda j: (0, 0))
    return pl.pallas_call(
        body,
        name="hgrn2_bwd",
        grid=(nt,),
        in_specs=[rev(HG_PARTS * width), const((1, width)), const((1, LANES)),
                  pl.BlockSpec((1, HG_HEADS, LANES, LANES), lambda j: (nt - 1 - j, 0, 0, 0)), rev(width)],
        out_specs=[rev(HG_PARTS * width), const((1, width)), const((1, LANES))],
        out_shape=[jax.ShapeDtypeStruct(x.shape, BF16), jax.ShapeDtypeStruct((1, width), F32),
                   jax.ShapeDtypeStruct((1, LANES), F32)],
        scratch_shapes=[pltpu.VMEM((HG_HEADS, LANES, LANES), F32)],
        compiler_params=_cparams("arbitrary"),
    )(x, lb, gn, sts, do)


@jax.custom_vjp
def hgrn2_mixer(h, w, lb, gn):
    return _hg_fwd_call(_mm_call(h, w, False, False, name="hgrn2_proj"), lb, gn)[0]


def _hgrn2_mixer_fwd(h, w, lb, gn):
    x = _mm_call(h, w, False, False, name="hgrn2_proj")
    o, sts = _hg_fwd_call(x, lb, gn)
    return o, (h, w, x, lb, gn, sts)


def _hgrn2_mixer_bwd(res, do):
    h, w, x, lb, gn, sts = res
    dx, dlb, dgn = _hg_bwd_call(x, lb, gn, sts, do)
    dh = _mm_call(dx, w, False, True, out_dtype=h.dtype, name="hgrn2_proj_da")
    dw = _mm_call(h, dx, True, False, out_dtype=w.dtype, name="hgrn2_proj_db")
    return dh, dw, dlb, dgn


hgrn2_mixer.defvjp(_hgrn2_mixer_fwd, _hgrn2_mixer_bwd)


D_MODEL = 1024
DEPTH = 2
SSM_GROUPS, SSM_GROUP_CH, SSM_STATE = 32, 16, 64
SSM_WIDTH = SSM_GROUPS * SSM_GROUP_CH
MLA_HEADS, MLA_NOPE, MLA_ROPE, MLA_V = 8, 64, 32, 64
MLA_Q_RANK, MLA_KV_RANK = 512, 256
HG_WIDTH = HG_HEADS * LANES
X_HEADS, X_HEAD_DIM = 4, 128
X_WIDTH = X_HEADS * X_HEAD_DIM
D_FF = 2816
ROPE_THETA = 10000.0
IN_SPLITS = (SSM_WIDTH, MLA_Q_RANK, MLA_KV_RANK, MLA_ROPE, HG_WIDTH, HG_WIDTH, HG_WIDTH, HG_WIDTH, 3 * D_MODEL)
ROPE_LANE0 = MLA_NOPE
MLA_Q_SCALE = LOG2E / math.sqrt(MLA_NOPE + MLA_ROPE)
ROW_TILE = 256


def _t_rms(x, g):
    return (_rms(x, g).astype(BF16),)


def _t_s5_act(y, u, d):
    return (jax.nn.gelu(y + d * u).astype(BF16),)


def _t_glu(z):
    zo, zg = split_lanes(z.astype(F32), D_MODEL)
    return (zo * jax.nn.sigmoid(zg),)


def _t_mla_rope(q, k, kr, c, sa, sb):
    rep = lambda t: jnp.concatenate([t] * MLA_HEADS, axis=1)
    half = MLA_ROPE // 2
    q_out = rope_lanes(q, rep(c), rep(sa), rep(sb), half) * MLA_Q_SCALE
    kr_out = rope_lanes(kr, c, sa, sb, half)
    return q_out.astype(BF16), (k + join_lanes((kr_out,) * MLA_HEADS)).astype(BF16)


def _t_merge(y_ssm, y_mla, y_hg, gates):
    g0, g1, g2 = split_lanes(gates.astype(F32), D_MODEL)
    return ((jax.nn.sigmoid(g0) * y_ssm + jax.nn.sigmoid(g1) * y_mla + jax.nn.sigmoid(g2) * y_hg).astype(BF16),)


def _t_xattn(q, kv):
    scale = 1.0 / math.sqrt(X_HEAD_DIM)
    heads = split_lanes(kv, X_HEAD_DIM)
    outs = []
    for qh, kh, vh in zip(split_lanes(q, X_HEAD_DIM), heads[:X_HEADS], heads[X_HEADS:]):
        sc = bdot_nt(qh, kh) * scale
        p = jnp.exp(sc - jnp.max(sc, axis=-1, keepdims=True))
        p = p / jnp.sum(p, axis=-1, keepdims=True)
        outs.append(bdot_nn(p, vh))
    return (join_lanes(tuple(outs)).astype(BF16),)


def _t_swiglu(gate_up):
    gt, up = split_lanes(gate_up.astype(F32), D_FF)
    return ((gt * jax.nn.sigmoid(gt) * up).astype(BF16),)


def _t_loss(x, tgt, g):
    e = _rms(x, g) - tgt
    return (jnp.broadcast_to(jnp.mean(e * e, axis=-1, keepdims=True), (x.shape[0], LANES)),)


rms_op = rowwise(_t_rms, 1, 0, ROW_TILE, "rmsnorm")
rms_res_op = rowwise(_t_rms, 1, 0, ROW_TILE, "rmsnorm_res", passthrough=True)
s5_act_op = rowwise(_t_s5_act, 2, 0, ROW_TILE, "s5_act")
glu_op = rowwise(_t_glu, 1, 0, ROW_TILE, "glu")
mla_rope_op = rowwise(_t_mla_rope, 3, 3, ROW_TILE, "mla_rope")
merge_op = rowwise(_t_merge, 4, 0, ROW_TILE, "merge")
xattn_op = rowwise(_t_xattn, 1, 0, ROW_TILE, "xattn")
swiglu_op = rowwise(_t_swiglu, 1, 0, ROW_TILE, "swiglu")
loss_op = rowwise(_t_loss, 1, 1, ROW_TILE, "loss")


def _rope_tables(positions):
    half = MLA_ROPE // 2
    inv_freq = ROPE_THETA ** (-jnp.arange(half, dtype=F32) / half)
    ang = positions.astype(F32)[:, None] * inv_freq
    cos, sin = jnp.cos(ang), jnp.sin(ang)
    s = positions.shape[0]
    z = lambda w: jnp.zeros((s, w), F32)
    tail = LANES - ROPE_LANE0 - MLA_ROPE
    c = jnp.concatenate([jnp.ones((s, ROPE_LANE0), F32), cos, cos, z(tail)], axis=1)
    sa = jnp.concatenate([z(ROPE_LANE0), -sin, z(half), z(tail)], axis=1)
    sb = jnp.concatenate([z(ROPE_LANE0), z(half), sin, z(tail)], axis=1)
    return c, sa, sb


def _s5_operators(lam_re, lam_im, b_re, b_im, c_re, c_im, log_step):
    g, p, h = SSM_GROUPS, SSM_STATE, SSM_GROUP_CH
    lam = lax.complex(lam_re, lam_im)
    lam_bar = jnp.exp(lam * jnp.exp(log_step)[:, None])
    b_bar = ((lam_bar - 1.0) / lam)[..., None] * lax.complex(b_re, b_im)
    per = LANES // h
    nb = g // per
    eye = jnp.eye(per, dtype=F32)
    bd = lambda t: jnp.einsum("jgph,gk->jghkp", t.reshape(nb, per, p, h), eye).reshape(nb, per * h, per * p)
    cd = lambda t: jnp.einsum("jghp,gk->jgpkh", t.reshape(nb, per, h, p), eye).reshape(nb, per * p, per * h)
    a = jnp.stack([jnp.real(lam_bar).reshape(-1), jnp.imag(lam_bar).reshape(-1)])
    return a, bd(jnp.real(b_bar)), bd(jnp.imag(b_bar)), cd(c_re), cd(-c_im)


LATENT_WIDTH = 1536
_LATENT = {}
_off = 0
for _name, _w in (("u", SSM_WIDTH), ("q_lat", MLA_Q_RANK), ("kv_lat", MLA_KV_RANK), ("k_rope", LANES)):
    _LATENT[_name] = (_off, _off + _w)
    _off += _w


def _layer_matrices(w, l):
    return {**_mixer_matrices(w, l), **_tail_matrices(w, l)}


def _tail_matrices(w, l):
    return dict(x_q=w["x_w_q"][l], x_kv=w["x_w_kv"][l], x_o=w["x_w_o"][l], ffn_gu=w["ffn_w_gate_up"][l],
                ffn_d=w["ffn_w_down"][l])


def _mixer_matrices(w, l):
    w_in = w["w_in"][l]
    d, dt = w_in.shape[0], w_in.dtype
    z = lambda n: jnp.zeros((d, n), dt)
    r0 = SSM_WIDTH + MLA_Q_RANK + MLA_KV_RANK
    r1 = r0 + MLA_ROPE
    r2 = r1 + HG_PARTS * HG_WIDTH
    w_latent = jnp.concatenate([w_in[:, :r0], z(ROPE_LANE0), w_in[:, r0:r1],
                                z(LATENT_WIDTH - r0 - ROPE_LANE0 - MLA_ROPE)], axis=1)
    pad_heads = lambda t: jnp.pad(t, ((0, 0), (0, 0), (0, LANES - t.shape[2]))).reshape(t.shape[0], -1)
    uq = w["mla_w_uq"][l].reshape(MLA_Q_RANK, MLA_HEADS, MLA_NOPE + MLA_ROPE)
    ukv = w["mla_w_ukv"][l].reshape(MLA_KV_RANK, MLA_HEADS, MLA_NOPE + MLA_V)
    wo = w["mla_w_o"][l].reshape(MLA_HEADS, MLA_V, D_MODEL)
    return dict(
        w_latent=w_latent, w_hg=w_in[:, r1:r2], w_gates=w_in[:, r2:], glu=w["ssm_w_glu"][l],
        uq=pad_heads(uq), uk=pad_heads(ukv[:, :, :MLA_NOPE]), uv=pad_heads(ukv[:, :, MLA_NOPE:]),
        mla_o=jnp.pad(wo, ((0, 0), (0, LANES - MLA_V), (0, 0))).reshape(MLA_HEADS * LANES, D_MODEL),
        hg_o=w["hg_w_o"][l], w_out=w["w_out"][l])


def _layer(x, mem, tabs, m, sp, l, lower_bound):
    return _tail(_mixer(x, tabs, m, sp, l, lower_bound), mem, m, sp, l)


def _mixer(x, tabs, m, sp, l, lower_bound):
    row = lambda name: sp[name][l].reshape(1, -1)
    h, x = rms_res_op(x, row("norm_mix"))
    latent = matmul(h, m["w_latent"])
    seg = lambda name: latent[:, _LATENT[name][0]:_LATENT[name][1]]
    a, bd_r, bd_i, cd_r, cd_i = _s5_operators(*(sp[n][l] for n in (
        "ssm_lam_re", "ssm_lam_im", "ssm_b_re", "ssm_b_im", "ssm_c_re", "ssm_c_im", "ssm_log_step")))
    u = seg("u")
    x_r, x_i = s5_scan(bd_matmul(u, bd_r, False, True), bd_matmul(u, bd_i, False, True), a)
    y = bd_matmul_add(x_i, cd_i, bd_matmul(x_r, cd_r, True, False), True, False)
    (ya,) = s5_act_op(y, u, row("ssm_d"))
    (y_ssm,) = glu_op(matmul(ya, m["glu"], BF16))
    (qn,) = rms_op(seg("q_lat"), row("mla_q_norm"))
    (kvn,) = rms_op(seg("kv_lat"), row("mla_kv_norm"))
    q, k = mla_rope_op(matmul(qn, m["uq"]), matmul(kvn, m["uk"]), seg("k_rope"), *tabs)
    o = causal_attention(q, k, matmul(kvn, m["uv"], BF16))
    y_mla = matmul(o, m["mla_o"])
    y_hg = matmul(hgrn2_mixer(h, m["w_hg"], lower_bound, row("hg_g_norm")), m["hg_o"])
    (merged,) = merge_op(y_ssm, y_mla, y_hg, matmul(h, m["w_gates"], BF16))
    return matmul_add(merged, m["w_out"], x)


def _tail(x, mem, m, sp, l):
    row = lambda name: sp[name][l].reshape(1, -1)
    hc, x = rms_res_op(x, row("norm_cross"))
    (mn,) = rms_op(mem, row("norm_mem"))
    (ox,) = xattn_op(matmul(hc, m["x_q"], BF16), matmul(mn, m["x_kv"]))
    x = matmul_add(ox, m["x_o"], x)
    hf, x = rms_res_op(x, row("norm_ffn"))
    (act,) = swiglu_op(matmul(hf, m["ffn_gu"], BF16))
    return matmul_add(act, m["ffn_d"], x)


def _lower_bounds(hg_lb):
    lb_p = jax.nn.softmax(hg_lb, axis=0)
    return jnp.cumsum(lb_p, axis=0) - lb_p[0:1]


def _final_loss(target, x, norm_final):
    (row_loss,) = loss_op(x, target, norm_final.reshape(1, -1))
    return 0.5 * jnp.sum(row_loss[:, 0])


def _local_loss(x, mem, positions, target, w, sp):
    tabs = _rope_tables(positions)
    lower = _lower_bounds(sp["hg_lb"])
    for l in range(DEPTH):
        x = _layer(x, mem, tabs, _layer_matrices(w, l), sp, l, lower[l].reshape(1, -1))
    return _final_loss(target, x, sp["norm_final"])


N_DEV = 8
N_CHIPS = 4
COMM_LANES = 512
MESH_ID = pl.DeviceIdType.MESH
_ANY = pl.BlockSpec(memory_space=pl.ANY)
_OTHER_CHIPS = ((1, 0), (0, 1), (1, 1))


def _place():
    return lax.axis_index("x"), lax.axis_index("y"), lax.axis_index("c")


def _all_gather_call(blocks, name):
    n = len(blocks)

    def body(*refs):
        x_refs, out_refs = refs[:n], refs[n:2 * n]
        send_sems, recv_sems, local_sems = refs[2 * n:]
        x, y, c = _place()
        me, sibling = (x, y, c), (x, y, 1 - c)
        chips = [(x ^ fx, y ^ fy) for fx, fy in _OTHER_CHIPS]

        def slot(i, px, py, pc):
            return out_refs[i].at[4 * px + 2 * py + pc]

        def copy(i, k, blk, to, src=None):
            return pltpu.make_async_remote_copy(
                src_ref=slot(i, *blk) if src is None else src, dst_ref=slot(i, *blk),
                send_sem=send_sems.at[i, k], recv_sem=recv_sems.at[i, k], device_id=to, device_id_type=MESH_ID)

        mine = [pltpu.make_async_copy(x_refs[i], slot(i, *me), local_sems.at[i]) for i in range(n)]
        first = []
        for i in range(n):
            first.append(copy(i, 0, me, sibling, src=x_refs[i]))
            first += [copy(i, 1 + j, me, (*chip, c), src=x_refs[i]) for j, chip in enumerate(chips)]
        for cp in mine + first:
            cp.start()
        passed = []
        for j, chip in enumerate(chips):
            for i in range(n):
                copy(i, 1 + j, (*chip, c), me).wait_recv()
                passed.append(copy(i, 4 + j, (*chip, c), sibling))
                passed[-1].start()
        for i in range(n):
            copy(i, 0, sibling, me).wait_recv()
            for j, chip in enumerate(chips):
                copy(i, 4 + j, (*chip, 1 - c), me).wait_recv()
        for cp in first + passed:
            cp.wait_send()
        for cp in mine:
            cp.wait()

    return pl.pallas_call(
        body,
        name=name,
        out_shape=[jax.ShapeDtypeStruct((N_DEV,) + b.shape, b.dtype) for b in blocks],
        in_specs=[_ANY] * n,
        out_specs=[_ANY] * n,
        scratch_shapes=[pltpu.SemaphoreType.DMA((n, 7)), pltpu.SemaphoreType.DMA((n, 7)), pltpu.SemaphoreType.DMA((n,))],
    )(*blocks)


def _pair_exchange_call(gs, name):
    n = len(gs)

    def body(*refs):
        g_refs, got_refs = refs[:n], refs[n:2 * n]
        send_sems, recv_sems = refs[2 * n:]
        x, y, c = _place()
        sends = [pltpu.make_async_remote_copy(
            src_ref=g_refs[i].at[2 * p + (1 - c)], dst_ref=got_refs[i].at[p],
            send_sem=send_sems.at[i, p], recv_sem=recv_sems.at[i, p], device_id=(x, y, 1 - c), device_id_type=MESH_ID)
            for i in range(n) for p in range(N_CHIPS)]
        for cp in sends:
            cp.start()
        for cp in sends:
            cp.wait_recv()
        for cp in sends:
            cp.wait_send()

    return pl.pallas_call(
        body,
        name=name,
        out_shape=[jax.ShapeDtypeStruct((N_CHIPS,) + g.shape[1:], g.dtype) for g in gs],
        in_specs=[_ANY] * n,
        out_specs=[_ANY] * n,
        scratch_shapes=[pltpu.SemaphoreType.DMA((n, N_CHIPS))] * 2,
    )(*gs)


def _chip_exchange_call(parts, name):
    n = len(parts)

    def body(*refs):
        p_refs, got_refs = refs[:n], refs[n:2 * n]
        send_sems, recv_sems = refs[2 * n:]
        x, y, c = _place()
        sends = []
        for i in range(n):
            for k, (fx, fy) in enumerate(_OTHER_CHIPS):
                px, py = x ^ fx, y ^ fy
                sends.append(pltpu.make_async_remote_copy(
                    src_ref=p_refs[i].at[2 * px + py], dst_ref=got_refs[i].at[k],
                    send_sem=send_sems.at[i, k], recv_sem=recv_sems.at[i, k], device_id=(px, py, c), device_id_type=MESH_ID))
        for cp in sends:
            cp.start()
        for cp in sends:
            cp.wait_recv()
        for cp in sends:
            cp.wait_send()

    return pl.pallas_call(
        body,
        name=name,
        out_shape=[jax.ShapeDtypeStruct((3,) + p.shape[1:], p.dtype) for p in parts],
        in_specs=[_ANY] * n,
        out_specs=[_ANY] * n,
        scratch_shapes=[pltpu.SemaphoreType.DMA((n, 3))] * 2,
    )(*parts)


def _rows_cols(shape):
    return math.prod(shape[:-1]), shape[-1]


def _pair_sum_call(g, got, c_idx, name):
    rows, cols = _rows_cols(got.shape[1:])
    tr = _pick_tile(rows, (512, 256, 128, 64, 32, 16))

    def body(c_ref, a_ref, b_ref, o_ref):
        o_ref[...] = (a_ref[...].astype(F32) + b_ref[...].astype(F32)).astype(o_ref.dtype)

    spec = pl.BlockSpec((1, tr, cols), lambda p, i, c_ref: (p, i, 0))
    out = pl.pallas_call(
        body,
        name=name,
        grid_spec=pltpu.PrefetchScalarGridSpec(
            num_scalar_prefetch=1, grid=(N_CHIPS, rows // tr),
            in_specs=[pl.BlockSpec((1, tr, cols), lambda p, i, c_ref: (2 * p + c_ref[0], i, 0)), spec],
            out_specs=spec),
        out_shape=jax.ShapeDtypeStruct((N_CHIPS, rows, cols), got.dtype),
        compiler_params=_cparams("parallel", "parallel"),
    )(c_idx, g.reshape(N_DEV, rows, cols), got.reshape(N_CHIPS, rows, cols))
    return out.reshape(got.shape)


def _chip_sum_call(part, got, chip_idx, name):
    rows, cols = _rows_cols(got.shape[1:])
    tr = _pick_tile(rows, (512, 256, 128, 64, 32, 16))

    def body(p_ref, a_ref, b_ref, o_ref):
        acc = a_ref[0].astype(F32)
        for k in range(3):
            acc = acc + b_ref[k].astype(F32)
        o_ref[...] = acc

    out = pl.pallas_call(
        body,
        name=name,
        grid_spec=pltpu.PrefetchScalarGridSpec(
            num_scalar_prefetch=1, grid=(rows // tr,),
            in_specs=[pl.BlockSpec((1, tr, cols), lambda i, p_ref: (p_ref[0], i, 0)),
                      pl.BlockSpec((3, tr, cols), lambda i, p_ref: (0, i, 0))],
            out_specs=pl.BlockSpec((tr, cols), lambda i, p_ref: (i, 0))),
        out_shape=jax.ShapeDtypeStruct((rows, cols), F32),
        compiler_params=_cparams("parallel"),
    )(chip_idx, part.reshape(N_CHIPS, rows, cols), got.reshape(3, rows, cols))
    return out.reshape(got.shape[1:])


def _reduce_scatter(gs, name):
    x, y, c = _place()
    c_idx = c.astype(jnp.int32).reshape(1)
    chip_idx = (2 * x + y).astype(jnp.int32).reshape(1)
    gots = _pair_exchange_call(gs, name + "_pair")
    parts = [_pair_sum_call(g, got, c_idx, name + "_pair_sum") for g, got in zip(gs, gots)]
    gots = _chip_exchange_call(parts, name + "_chip")
    return [_chip_sum_call(p, got, chip_idx, name + "_chip_sum") for p, got in zip(parts, gots)]


_HBM = pl.BlockSpec(memory_space=pltpu.HBM)
_SEM = pl.BlockSpec(memory_space=pltpu.SEMAPHORE)
_SIDE_EFFECT = pltpu.SideEffectType.DATAFLOW_SIDE_EFFECTING
N_PEERS = N_DEV - 1


def _peer(k):
    x, y, c = _place()
    px, py, pc = x ^ ((k >> 2) & 1), y ^ ((k >> 1) & 1), c ^ (k & 1)
    return (px, py, pc), 4 * px + 2 * py + pc


def _exchange_copy(src_ref, land_ref, send_sems, recv_sems, i, k, scatter, receiving):
    x, y, c = _place()
    me = 4 * x + 2 * y + c
    peer, peer_idx = _peer(k)
    sem = i * N_PEERS + k - 1
    return pltpu.make_async_remote_copy(
        src_ref=src_ref.at[peer_idx] if scatter else src_ref, dst_ref=land_ref.at[peer_idx if receiving else me],
        send_sem=send_sems.at[sem], recv_sem=recv_sems.at[sem], device_id=peer, device_id_type=MESH_ID)


def _exchange_start_call(srcs, after, scatter, name):
    n = len(srcs)
    slot_shapes = [s.shape[1:] if scatter else s.shape for s in srcs]

    def body(*refs):
        src_refs, land_refs = refs[:n], refs[n:2 * n]
        send_sems, recv_sems = refs[2 * n + 1], refs[2 * n + 2]
        token = refs[-1]
        for i in range(n):
            for k in range(1, N_DEV):
                _exchange_copy(src_refs[i], land_refs[i], send_sems, recv_sems, i, k, scatter, False).start()
        token[...] = jnp.zeros_like(token)

    lands = [pltpu.with_memory_space_constraint(lax.empty((N_DEV,) + shp, s.dtype), pltpu.HBM)
             for shp, s in zip(slot_shapes, srcs)]
    out = pl.pallas_call(
        body,
        name=name,
        out_shape=([pltpu.SemaphoreType.DMA((n * N_PEERS,)), pltpu.SemaphoreType.DMA((n * N_PEERS,))]
                   + [pltpu.HBM(s.shape, s.dtype) for s in srcs] + [pltpu.HBM(l.shape, l.dtype) for l in lands]
                   + [jax.ShapeDtypeStruct((SUBLANES, LANES), F32)]),
        in_specs=[_HBM] * (2 * n) + [pl.BlockSpec(memory_space=pl.ANY)],
        out_specs=[_SEM, _SEM] + [_HBM] * (2 * n) + [pl.BlockSpec(memory_space=pltpu.VMEM)],
        input_output_aliases={j: 2 + j for j in range(2 * n)},
        compiler_params=pltpu.CompilerParams(has_side_effects=_SIDE_EFFECT),
    )(*[pltpu.with_memory_space_constraint(s, pltpu.HBM) for s in srcs], *lands, after)
    return out[0], out[1], list(out[2:2 + n]), list(out[2 + n:2 + 2 * n]), out[-1]


def _exchange_wait_call(started, after, scatter, name):
    send_sems, recv_sems, srcs, lands, _ = started
    n = len(srcs)

    def body(*refs):
        src_refs, land_refs = refs[:n], refs[n:2 * n]
        send_s, recv_s = refs[2 * n], refs[2 * n + 1]
        for i in range(n):
            for k in range(1, N_DEV):
                cp = _exchange_copy(src_refs[i], land_refs[i], send_s, recv_s, i, k, scatter, True)
                cp.wait_send()
                cp.wait_recv()

    out = pl.pallas_call(
        body,
        name=name,
        out_shape=[pltpu.HBM(s.shape, s.dtype) for s in srcs] + [pltpu.HBM(l.shape, l.dtype) for l in lands],
        in_specs=[_HBM] * (2 * n) + [_SEM, _SEM, pl.BlockSpec(memory_space=pl.ANY)],
        out_specs=[_HBM] * (2 * n),
        input_output_aliases={j: j for j in range(2 * n)},
        compiler_params=pltpu.CompilerParams(has_side_effects=_SIDE_EFFECT),
    )(*srcs, *lands, send_sems, recv_sems, after)
    return list(out[n:])


def _own_slot(land, own):
    x, y, c = _place()
    return lax.dynamic_update_index_in_dim(land, own, 4 * x + 2 * y + c, 0)


def _slot_sum_call(land, name):
    rows, cols = _rows_cols(land.shape[1:])
    tr = _pick_tile(rows, (256, 128, 64, 32, 16))

    def body(land_ref, o_ref):
        acc = land_ref[0].astype(F32)
        for s in range(1, N_DEV):
            acc = acc + land_ref[s].astype(F32)
        o_ref[...] = acc

    out = pl.pallas_call(
        body,
        name=name,
        grid=(rows // tr,),
        in_specs=[pl.BlockSpec((N_DEV, tr, cols), lambda i: (0, i, 0))],
        out_specs=pl.BlockSpec((tr, cols), lambda i: (i, 0)),
        out_shape=jax.ShapeDtypeStruct((rows, cols), F32),
        compiler_params=_cparams("parallel"),
    )(land.reshape(N_DEV, rows, cols))
    return out.reshape(land.shape[1:])


SMALL_BLOCK_ROWS = 16


def _pack_small(parts):
    flat = jnp.concatenate([p.reshape(-1) for p in parts])
    chunk = N_DEV * SMALL_BLOCK_ROWS * COMM_LANES
    flat = jnp.pad(flat, (0, (-flat.shape[0]) % chunk))
    return flat.reshape(N_DEV, -1, COMM_LANES)


def _unpack_small(buf, shapes):
    flat = buf.reshape(-1)
    out, off = [], 0
    for shp in shapes:
        n = math.prod(shp)
        out.append(flat[off:off + n].reshape(shp))
        off += n
    return out


SHARDED = dict(w_in=2, ssm_w_glu=2, mla_w_uq=2, mla_w_ukv=2, mla_w_o=2, hg_w_o=2, w_out=1, x_w_q=1, x_w_kv=1,
               x_w_o=2, ffn_w_gate_up=2, ffn_w_down=1)
REPLICATED = ("norm_mix", "ssm_lam_re", "ssm_lam_im", "ssm_b_re", "ssm_b_im", "ssm_c_re", "ssm_c_im", "ssm_d",
              "ssm_log_step", "mla_q_norm", "mla_kv_norm", "hg_lb", "hg_g_norm", "norm_cross", "norm_mem", "norm_ffn",
              "norm_final")


def _join_shards(stacked, axis):
    n, l, a, b = stacked.shape
    if axis == 1:
        return stacked.transpose(1, 0, 2, 3).reshape(l, n * a, b)
    return stacked.transpose(1, 2, 0, 3).reshape(l, a, n * b)


def _split_shards(full, axis):
    l, a, b = full.shape
    if axis == 1:
        return full.reshape(l, N_DEV, a // N_DEV, b).transpose(1, 0, 2, 3)
    return full.reshape(l, a, N_DEV, b // N_DEV).transpose(2, 0, 1, 3)


MIXER_WEIGHTS = ("w_in", "ssm_w_glu", "mla_w_uq", "mla_w_ukv", "mla_w_o", "hg_w_o", "w_out")
TAIL_WEIGHTS = ("x_w_q", "x_w_kv", "x_w_o", "ffn_w_gate_up", "ffn_w_down")


def _mixer_fn(l, tabs):
    def f(x, full, small, lower):
        m = _mixer_matrices(dict(zip(MIXER_WEIGHTS, full)), 0)
        return _mixer(x, tabs, m, dict(zip(REPLICATED, small)), l, lower[l].reshape(1, -1))
    return f


def _tail_fn(l, mem):
    def f(x, full, small):
        return _tail(x, mem, _tail_matrices(dict(zip(TAIL_WEIGHTS, full)), 0), dict(zip(REPLICATED, small)), l)
    return f


ADAM_LR, ADAM_B1, ADAM_B2, ADAM_EPS, ADAM_WD, ADAM_STEP = 0.001, 0.9, 0.999, 1e-08, 0.01, 10


def _adamw_call(w, g, m, v, name):
    shape = w.shape
    cols = shape[-1]
    rows = math.prod(shape[:-1]) if len(shape) > 1 else 1
    tr = _pick_tile(rows, (512, 256, 128, 64, 32, 16, 8))

    def body(w_ref, g_ref, m_ref, v_ref, d_ref, nm_ref, nv_ref):
        gg = g_ref[...]
        m_new = ADAM_B1 * m_ref[...] + (1.0 - ADAM_B1) * gg
        v_new = ADAM_B2 * v_ref[...] + (1.0 - ADAM_B2) * jnp.square(gg)
        m_hat = m_new / (1.0 - ADAM_B1 ** ADAM_STEP)
        v_hat = v_new / (1.0 - ADAM_B2 ** ADAM_STEP)
        d_ref[...] = -ADAM_LR * (m_hat / (jnp.sqrt(v_hat) + ADAM_EPS) + ADAM_WD * w_ref[...])
        nm_ref[...] = m_new
        nv_ref[...] = v_new

    spec = pl.BlockSpec((tr, cols), lambda i: (i, 0))
    outs = pl.pallas_call(
        body, name=name, grid=(rows // tr,), in_specs=[spec] * 4, out_specs=[spec] * 3,
        out_shape=[jax.ShapeDtypeStruct((rows, cols), F32)] * 3, compiler_params=_cparams("parallel"),
    )(*(t.reshape(rows, cols) for t in (w, g, m, v)))
    return tuple(o.reshape(shape) for o in outs)


WEIGHTS = ("norm_mix", "w_in", "ssm_lam_re", "ssm_lam_im", "ssm_b_re", "ssm_b_im", "ssm_c_re", "ssm_c_im", "ssm_d",
           "ssm_log_step", "ssm_w_glu", "mla_q_norm", "mla_kv_norm", "mla_w_uq", "mla_w_ukv", "mla_w_o", "hg_lb",
           "hg_g_norm", "hg_w_o", "w_out", "norm_cross", "norm_mem", "x_w_q", "x_w_kv", "x_w_o", "norm_ffn",
           "ffn_w_gate_up", "ffn_w_down", "norm_final")


def kernel(x, mem, positions, norm_mix, w_in, ssm_lam_re, ssm_lam_im, ssm_b_re, ssm_b_im, ssm_c_re, ssm_c_im, ssm_d, ssm_log_step, ssm_w_glu, mla_q_norm, mla_kv_norm, mla_w_uq, mla_w_ukv, mla_w_o, hg_lb, hg_g_norm, hg_w_o, w_out, norm_cross, norm_mem, x_w_q, x_w_kv, x_w_o, norm_ffn, ffn_w_gate_up, ffn_w_down, norm_final, loss_target, m_norm_mix, m_w_in, m_ssm_lam_re, m_ssm_lam_im, m_ssm_b_re, m_ssm_b_im, m_ssm_c_re, m_ssm_c_im, m_ssm_d, m_ssm_log_step, m_ssm_w_glu, m_mla_q_norm, m_mla_kv_norm, m_mla_w_uq, m_mla_w_ukv, m_mla_w_o, m_hg_lb, m_hg_g_norm, m_hg_w_o, m_w_out, m_norm_cross, m_norm_mem, m_x_w_q, m_x_w_kv, m_x_w_o, m_norm_ffn, m_ffn_w_gate_up, m_ffn_w_down, m_norm_final, v_norm_mix, v_w_in, v_ssm_lam_re, v_ssm_lam_im, v_ssm_b_re, v_ssm_b_im, v_ssm_c_re, v_ssm_c_im, v_ssm_d, v_ssm_log_step, v_ssm_w_glu, v_mla_q_norm, v_mla_kv_norm, v_mla_w_uq, v_mla_w_ukv, v_mla_w_o, v_hg_lb, v_hg_g_norm, v_hg_w_o, v_w_out, v_norm_cross, v_norm_mem, v_x_w_q, v_x_w_kv, v_x_w_o, v_norm_ffn, v_ffn_w_gate_up, v_ffn_w_down, v_norm_final):
    given = dict(locals())
    weights = {n: given[n] for n in WEIGHTS}
    small = tuple(weights[n] for n in REPLICATED)
    layer1 = MIXER_WEIGHTS + TAIL_WEIGHTS
    shards = lambda names, l: [weights[n][l:l + 1].astype(BF16) for n in names]
    join = lambda names, stacked: tuple(_join_shards(p, SHARDED[n]) for n, p in zip(names, stacked))
    split = lambda names, cts: [_split_shards(ct, SHARDED[n]) for n, ct in zip(names, cts)]
    landed = lambda names, lands, own: join(names, [_own_slot(land, o) for land, o in zip(lands, own)])
    xs, tabs = x[0], _rope_tables(positions[0])
    lower, vjp_lower = jax.vjp(_lower_bounds, hg_lb)
    me = 4 * lax.axis_index("x") + 2 * lax.axis_index("y") + lax.axis_index("c")

    got_m0 = _all_gather_call(shards(MIXER_WEIGHTS, 0), "weights_all_gather_m0")
    own_t0, own_l1 = shards(TAIL_WEIGHTS, 0), shards(layer1, 1)
    gather_t0 = _exchange_start_call(own_t0, got_m0[0], False, "weights_gather_start_t0")
    gather_l1 = _exchange_start_call(own_l1, gather_t0[4], False, "weights_gather_start_l1")
    xs = xs + gather_l1[4][0, 0]
    xa0, vjp_m0 = jax.vjp(_mixer_fn(0, tabs), xs, join(MIXER_WEIGHTS, got_m0), small, lower)
    full_t0 = landed(TAIL_WEIGHTS, _exchange_wait_call(gather_t0, xa0, False, "weights_gather_wait_t0"), own_t0)
    x1, vjp_t0 = jax.vjp(_tail_fn(0, mem[0]), xa0, full_t0, small)
    full_l1 = landed(layer1, _exchange_wait_call(gather_l1, x1, False, "weights_gather_wait_l1"), own_l1)
    xa1, vjp_m1 = jax.vjp(_mixer_fn(1, tabs), x1, full_l1[:len(MIXER_WEIGHTS)], small, lower)
    x2, vjp_t1 = jax.vjp(_tail_fn(1, mem[0]), xa1, full_l1[len(MIXER_WEIGHTS):], small)
    loss_local, vjp_loss = jax.vjp(functools.partial(_final_loss, loss_target[0]), x2, norm_final)

    def scatter_start(names, cts, dx, tag):
        gs = split(names, cts)
        started = _exchange_start_call(gs, dx, True, "grads_scatter_start_" + tag)
        return (started, gs), dx + started[4][0, 0]

    def scatter_finish(pending, after, tag):
        started, gs = pending
        lands = _exchange_wait_call(started, after, True, "grads_scatter_wait_" + tag)
        return [_slot_sum_call(_own_slot(land, lax.dynamic_index_in_dim(g, me, 0, keepdims=False)), "grads_slot_sum_" + tag)
                for land, g in zip(lands, gs)]

    dx2, d_norm_final = vjp_loss(jnp.ones((), F32))
    dxa1, dfull_t1, dsmall_t1 = vjp_t1(dx2)
    pend_t1, dxa1 = scatter_start(TAIL_WEIGHTS, dfull_t1, dxa1, "t1")
    dx1, dfull_m1, dsmall_m1, dlower1 = vjp_m1(dxa1)
    pend_m1, dx1 = scatter_start(MIXER_WEIGHTS, dfull_m1, dx1, "m1")
    dxa0, dfull_t0, dsmall_t0 = vjp_t0(dx1)
    pend_t0, dxa0 = scatter_start(TAIL_WEIGHTS, dfull_t0, dxa0, "t0")
    gx, dfull_m0, dsmall_m0, dlower0 = vjp_m0(dxa0)
    by_layer = {
        0: dict(zip(MIXER_WEIGHTS + TAIL_WEIGHTS,
                    _reduce_scatter(split(MIXER_WEIGHTS, dfull_m0), "grads_reduce_scatter_m0") + scatter_finish(pend_t0, gx, "t0"))),
        1: dict(zip(layer1, scatter_finish(pend_m1, gx, "m1") + scatter_finish(pend_t1, gx, "t1")))}
    grads = {n: jnp.concatenate([by_layer[0][n], by_layer[1][n]], axis=0) for n in SHARDED}

    d_small = dict(zip(REPLICATED, (a + b + c + d for a, b, c, d in zip(dsmall_m0, dsmall_t0, dsmall_m1, dsmall_t1))))
    d_small["norm_final"] = d_small["norm_final"] + d_norm_final
    d_small["hg_lb"] = d_small["hg_lb"] + vjp_lower(dlower0 + dlower1)[0]
    shapes = [d_small[n].shape for n in REPLICATED]
    (mine,) = _reduce_scatter([_pack_small([d_small[n] for n in REPLICATED])], "small_reduce_scatter")
    (total,) = _all_gather_call([mine], "small_all_gather")
    grads.update(zip(REPLICATED, _unpack_small(total, shapes)))

    loss = lax.psum(loss_local, ("x", "y", "c"))
    steps = {n: _adamw_call(weights[n], grads[n], given["m_" + n], given["v_" + n], "adamw_" + n) for n in WEIGHTS}
    return (loss, gx[None], *[grads[n] for n in WEIGHTS], *[steps[n][0] for n in WEIGHTS],
            *[steps[n][1] for n in WEIGHTS], *[steps[n][2] for n in WEIGHTS])
```

```python
import functools
import math

import jax
import jax.numpy as jnp
from jax import lax
from jax.experimental import pallas as pl
from jax.experimental.pallas import tpu as pltpu

F32 = jnp.float32
BF16 = jnp.bfloat16

VMEM_LIMIT_BYTES = 48 * 1024 * 1024
LANES = 128
SUBLANES = 8


def _cparams(*sem):
    return pltpu.CompilerParams(dimension_semantics=sem, vmem_limit_bytes=VMEM_LIMIT_BYTES)


def _pick_tile(n, cands):
    for c in cands:
        if n % c == 0:
            return c
    return n


MM_VMEM_BUDGET = 38 * 1024 * 1024
MM_STEP_US = 0.35
HBM_BYTES_PER_US = 3.0e6
VREG_RMW_PER_US = 1.5e3


def _divisor_tiles(dim, cands):
    out = [t for t in cands if dim % t == 0]
    return out or [dim]


def _mm_tiles(m, n, k, sa, sb, so):
    tms = _divisor_tiles(m, (1408, 1024, 512, 256, 128, 64, 32, 16, 8))[:2]
    tns = _divisor_tiles(n, (2048, 1536, 1408, 1024, 768, 512, 384, 256, 128))
    tks = [k // d for d in (1, 2, 4, 8, 13, 16, 26, 32, 52) if k % d == 0 and (k // d) % LANES == 0] or [k]
    best = None
    for tk in tks:
        nk = k // tk
        for tm in tms:
            for tn in tns:
                vmem = 2 * (tm * tk * sa + tk * tn * sb + tm * tn * so) + (tm * tn * 4 if nk > 1 else 0)
                if vmem > MM_VMEM_BUDGET:
                    continue
                steps = (m // tm) * (n // tn) * nk
                a_reads = m * k * sa * (n // tn if nk > 1 else 1)
                b_reads = k * n * sb * (m // tm if (nk > 1 or n // tn > 1) else 1)
                cost = (steps * MM_STEP_US + (a_reads + b_reads) / HBM_BYTES_PER_US
                        + (m * n * nk / 1024 / VREG_RMW_PER_US if nk > 1 else 0.0))
                if best is None or cost < best[0]:
                    best = (cost, tm, tn, tk)
    assert best is not None, (m, n, k)
    return best[1:]


def _mm_tiles_cached_t(m, n, k, sa, sb, so):
    for tm in _divisor_tiles(m, (1024, 512)):
        if m % tm:
            break
        for tn in _divisor_tiles(n, (1024, 512, 384, 256, 128)):
            if 2 * (k * tm * sa + k * tn * sb + tm * tn * so) + tm * k * 2 <= MM_VMEM_BUDGET:
                return tm, tn
    return None


def _mm_tn_cached_call(a, b, tiles, out_dtype, name):
    k, m = a.shape
    n = b.shape[1]
    tm, tn = tiles

    def body(a_ref, b_ref, o_ref, at_ref):
        @pl.when(pl.program_id(1) == 0)
        def _():
            at_ref[...] = a_ref[...].astype(BF16).T

        o_ref[...] = lax.dot_general(at_ref[...], b_ref[...].astype(BF16), _NN_DIMS,
                                     preferred_element_type=F32).astype(out_dtype)

    return pl.pallas_call(
        body,
        name=name,
        grid=(m // tm, n // tn),
        in_specs=[pl.BlockSpec((k, tm), lambda i, j: (0, i)), pl.BlockSpec((k, tn), lambda i, j: (0, j))],
        out_specs=pl.BlockSpec((tm, tn), lambda i, j: (i, j)),
        out_shape=jax.ShapeDtypeStruct((m, n), out_dtype),
        scratch_shapes=[pltpu.VMEM((tm, k), BF16)],
        compiler_params=_cparams("parallel", "arbitrary"),
    )(a, b)


_NN_DIMS = (((1,), (0,)), ((), ()))


def _mm_call(a, b, ta, tb, add=None, out_dtype=F32, name="mm"):
    m, k = (a.shape[1], a.shape[0]) if ta else a.shape
    k2, n = (b.shape[1], b.shape[0]) if tb else b.shape
    assert k == k2, (a.shape, b.shape, ta, tb)
    sizes = (a.dtype.itemsize, b.dtype.itemsize, jnp.dtype(out_dtype).itemsize + (add.dtype.itemsize if add is not None else 0))
    if ta:
        tiles = _mm_tiles_cached_t(m, n, k, *sizes)
        if tiles is not None:
            return _mm_tn_cached_call(a, b, tiles, out_dtype, name)
    tm, tn, tk = _mm_tiles(m, n, k, *sizes)
    nk = k // tk
    a_spec = pl.BlockSpec((tk, tm), lambda i, j, kk: (kk, i)) if ta else pl.BlockSpec((tm, tk), lambda i, j, kk: (i, kk))
    b_spec = pl.BlockSpec((tn, tk), lambda i, j, kk: (j, kk)) if tb else pl.BlockSpec((tk, tn), lambda i, j, kk: (kk, j))
    o_spec = pl.BlockSpec((tm, tn), lambda i, j, kk: (i, j))
    dn = (((0 if ta else 1,), (1 if tb else 0,)), ((), ()))
    has_add = add is not None

    def body(*refs):
        a_ref, b_ref = refs[0], refs[1]
        c_ref = refs[2] if has_add else None
        o_ref = refs[3] if has_add else refs[2]
        p = lax.dot_general(a_ref[...].astype(BF16), b_ref[...].astype(BF16), dn, preferred_element_type=F32)

        def finish(r):
            if has_add:
                r = r + c_ref[...].astype(F32)
            o_ref[...] = r.astype(out_dtype)

        if nk == 1:
            finish(p)
        else:
            acc_ref = refs[-1]
            kk = pl.program_id(2)

            @pl.when(kk == 0)
            def _():
                acc_ref[...] = p

            @pl.when(kk > 0)
            def _():
                acc_ref[...] += p

            @pl.when(kk == nk - 1)
            def _():
                finish(acc_ref[...])

    in_specs = [a_spec, b_spec] + ([o_spec] if has_add else [])
    args = (a, b) + ((add,) if has_add else ())
    return pl.pallas_call(
        body,
        name=name,
        grid=(m // tm, n // tn, nk),
        in_specs=in_specs,
        out_specs=o_spec,
        out_shape=jax.ShapeDtypeStruct((m, n), out_dtype),
        scratch_shapes=[] if nk == 1 else [pltpu.VMEM((tm, tn), F32)],
        compiler_params=_cparams("parallel", "parallel", "arbitrary"),
    )(*args)


@functools.partial(jax.custom_vjp, nondiff_argnums=(2,))
def matmul(a, b, out_dtype=F32):
    return _mm_call(a, b, False, False, out_dtype=out_dtype, name="mm_fwd")


def _matmul_fwd(a, b, out_dtype):
    return matmul(a, b, out_dtype), (a, b)


def _matmul_bwd(out_dtype, res, g):
    a, b = res
    da = _mm_call(g, b, False, True, out_dtype=a.dtype, name="mm_da")
    db = _mm_call(a, g, True, False, out_dtype=b.dtype, name="mm_db")
    return da, db


matmul.defvjp(_matmul_fwd, _matmul_bwd)


@jax.custom_vjp
def matmul_add(a, b, c):
    return _mm_call(a, b, False, False, add=c, name="mm_add_fwd")


def _matmul_add_fwd(a, b, c):
    return _mm_call(a, b, False, False, add=c, name="mm_add_fwd"), (a, b)


def _matmul_add_bwd(res, g):
    a, b = res
    da = _mm_call(g, b, False, True, out_dtype=a.dtype, name="mm_da")
    db = _mm_call(a, g, True, False, out_dtype=b.dtype, name="mm_db")
    return da, db, g


matmul_add.defvjp(_matmul_add_fwd, _matmul_add_bwd)


def rowwise(f, n_rows, n_aux, tile, name, passthrough=False):
    def specs(arrs, tiled):
        out = []
        for x in arrs:
            if tiled:
                out.append(pl.BlockSpec((tile, x.shape[1]), lambda i: (i, 0)))
            else:
                out.append(pl.BlockSpec(x.shape, lambda i: (0, 0)))
        return out

    def tile_structs(args):
        rows_aux, params = args[: n_rows + n_aux], args[n_rows + n_aux:]
        return [jax.ShapeDtypeStruct((tile, x.shape[1]), x.dtype) for x in rows_aux] + [
            jax.ShapeDtypeStruct(p.shape, p.dtype) for p in params]

    def fwd_call(*args):
        s = args[0].shape[0]
        outs = jax.eval_shape(f, *tile_structs(args))
        n_in = len(args)

        def body(*refs):
            vals = [r[...] for r in refs[:n_in]]
            res = f(*vals)
            for o_ref, r in zip(refs[n_in:], res):
                o_ref[...] = r.astype(o_ref.dtype)

        return pl.pallas_call(
            body,
            name=name + "_fwd",
            grid=(s // tile,),
            in_specs=specs(args[: n_rows + n_aux], True) + specs(args[n_rows + n_aux:], False),
            out_specs=[pl.BlockSpec((tile, o.shape[1]), lambda i: (i, 0)) for o in outs],
            out_shape=[jax.ShapeDtypeStruct((s, o.shape[1]), o.dtype) for o in outs],
            compiler_params=_cparams("parallel"),
        )(*args)

    def bwd_call(args, gs):
        s = args[0].shape[0]
        rows, aux, params = args[:n_rows], args[n_rows:n_rows + n_aux], args[n_rows + n_aux:]
        n_in, n_g, n_p = len(args), len(gs), len(params)
        n_gf = n_g - 1 if passthrough else n_g

        def body(*refs):
            vals = [r[...] for r in refs[:n_in]]
            gvals = tuple(r[...] for r in refs[n_in:n_in + n_gf])
            out_refs = refs[n_in + n_g:]
            auxv = vals[n_rows:n_rows + n_aux]

            def g_(*rp):
                return tuple(f(*rp[:n_rows], *auxv, *rp[n_rows:]))

            _, vjp = jax.vjp(g_, *vals[:n_rows], *vals[n_rows + n_aux:])
            cts = list(vjp(gvals))
            if passthrough:
                cts[0] = cts[0] + refs[n_in + n_gf][...]
            for o_ref, ct in zip(out_refs[:n_rows], cts[:n_rows]):
                o_ref[...] = ct.astype(o_ref.dtype)
            if n_p:
                @pl.when(pl.program_id(0) == 0)
                def _():
                    for o_ref in out_refs[n_rows:]:
                        o_ref[...] = jnp.zeros_like(o_ref)

                for o_ref, ct in zip(out_refs[n_rows:], cts[n_rows:]):
                    o_ref[...] += ct.astype(o_ref.dtype)

        return pl.pallas_call(
            body,
            name=name + "_bwd",
            grid=(s // tile,),
            in_specs=specs(rows + aux, True) + specs(params, False) + specs(gs, True),
            out_specs=specs(rows, True) + specs(params, False),
            out_shape=[jax.ShapeDtypeStruct(x.shape, x.dtype) for x in rows + params],
            compiler_params=_cparams("arbitrary" if n_p else "parallel"),
        )(*args, *gs)

    @jax.custom_vjp
    def op(*args):
        return tuple(fwd_call(*args)) + ((args[0],) if passthrough else ())

    def op_fwd(*args):
        return op(*args), args

    def op_bwd(args, gs):
        cts = bwd_call(tuple(args), tuple(gs))
        rows_ct, par_ct = cts[:n_rows], cts[n_rows:]
        aux_ct = [jnp.zeros_like(a) for a in args[n_rows:n_rows + n_aux]]
        return tuple(rows_ct) + tuple(aux_ct) + tuple(par_ct)

    op.defvjp(op_fwd, op_bwd)
    return op


SCAN_SEGMENTS = SUBLANES
SCAN_TILE_ROWS = 512
SCAN_TILE_LANES = 512


def _scan_specs(s, n):
    tr = min(SCAN_TILE_ROWS, s)
    tn = min(SCAN_TILE_LANES, n)
    return tr, tn, s // tr, n // tn


def _scan_step(ar, ai, xr, xi, br, bi):
    return ar * xr - ai * xi + br, ar * xi + ai * xr + bi


def _scan_finals(b_r, b_i, a, reverse, name):
    s, n = b_r.shape
    tr, tn, nt, nc = _scan_specs(s, n)
    ti = tr // SUBLANES
    tmap = (lambda c, j: (nt - 1 - j, c)) if reverse else (lambda c, j: (j, c))

    def body(br_ref, bi_ref, a_ref, fr_ref, fi_ref, sr, si):
        j = pl.program_id(1)

        @pl.when(j == 0)
        def _():
            sr[...] = jnp.zeros_like(sr)
            si[...] = jnp.zeros_like(si)

        ar = jnp.broadcast_to(a_ref[0:1, :], (SUBLANES, tn))
        ai = jnp.broadcast_to(a_ref[1:2, :], (SUBLANES, tn))

        def step(ii, carry):
            i = (ti - 1 - ii) if reverse else ii
            off = pl.multiple_of(i * SUBLANES, SUBLANES)
            return _scan_step(ar, ai, carry[0], carry[1], br_ref[pl.ds(off, SUBLANES), :], bi_ref[pl.ds(off, SUBLANES), :])

        xr, xi = lax.fori_loop(0, ti, step, (sr[...], si[...]), unroll=4)
        sr[...] = xr
        si[...] = xi

        @pl.when(j == nt - 1)
        def _():
            fr_ref[...] = xr
            fi_ref[...] = xi

    bspec = pl.BlockSpec((tr, tn), tmap)
    fspec = pl.BlockSpec((SUBLANES, tn), lambda c, j: (0, c))
    return pl.pallas_call(
        body,
        name=name,
        grid=(nc, nt),
        in_specs=[bspec, bspec, pl.BlockSpec((2, tn), lambda c, j: (0, c))],
        out_specs=[fspec, fspec],
        out_shape=[jax.ShapeDtypeStruct((SUBLANES, n), F32)] * 2,
        scratch_shapes=[pltpu.VMEM((SUBLANES, tn), F32)] * 2,
        compiler_params=_cparams("parallel", "arbitrary"),
    )(b_r, b_i, a)


def _scan_states(b_r, b_i, a, f_r, f_i, reverse, xs, name):
    s, n = b_r.shape
    tr, tn, nt, nc = _scan_specs(s, n)
    ti = tr // SUBLANES
    seg_len = s // SCAN_SEGMENTS
    assert seg_len & (seg_len - 1) == 0
    with_acc = xs is not None
    tmap = (lambda c, j: (nt - 1 - j, c)) if reverse else (lambda c, j: (j, c))
    order = list(range(SCAN_SEGMENTS))[::-1] if reverse else list(range(SCAN_SEGMENTS))

    def body(*refs):
        br_ref, bi_ref, a_ref, fr_ref, fi_ref = refs[:5]
        pos = 5
        if with_acc:
            xr_ref, xi_ref = refs[5:7]
            pos = 7
        or_ref, oi_ref = refs[pos:pos + 2]
        pos += 2
        if with_acc:
            dr_ref, di_ref = refs[pos:pos + 2]
            pos += 2
        sr, si = refs[pos:pos + 2]
        if with_acc:
            accr, acci = refs[pos + 2:pos + 4]
        j = pl.program_id(1)
        a_r1, a_i1 = a_ref[0:1, :], a_ref[1:2, :]

        @pl.when(j == 0)
        def _():
            pr, pi = a_r1, a_i1
            for _ in range(seg_len.bit_length() - 1):
                pr, pi = pr * pr - pi * pi, 2.0 * pr * pi
            cr = jnp.zeros((1, tn), F32)
            ci = jnp.zeros((1, tn), F32)
            for idx, k in enumerate(order):
                if idx > 0:
                    kp = order[idx - 1]
                    cr, ci = (fr_ref[kp:kp + 1, :] + pr * cr - pi * ci, fi_ref[kp:kp + 1, :] + pr * ci + pi * cr)
                sr[k:k + 1, :] = cr
                si[k:k + 1, :] = ci
            if with_acc:
                accr[...] = jnp.zeros_like(accr)
                acci[...] = jnp.zeros_like(acci)

        ar = jnp.broadcast_to(a_r1, (SUBLANES, tn))
        ai = jnp.broadcast_to(a_i1, (SUBLANES, tn))

        def step(ii, carry):
            i = (ti - 1 - ii) if reverse else ii
            off = pl.multiple_of(i * SUBLANES, SUBLANES)
            rows = pl.ds(off, SUBLANES)
            xr, xi = carry[0], carry[1]
            if with_acc:
                zr, zi = xr_ref[rows, :], xi_ref[rows, :]
                acc = (carry[2] + xr * zr + xi * zi, carry[3] + xi * zr - xr * zi)
            nr, ni = _scan_step(ar, ai, xr, xi, br_ref[rows, :], bi_ref[rows, :])
            or_ref[rows, :] = nr
            oi_ref[rows, :] = ni
            return (nr, ni) + (acc if with_acc else ())

        init = (sr[...], si[...]) + ((accr[...], acci[...]) if with_acc else ())
        out = lax.fori_loop(0, ti, step, init, unroll=4)
        sr[...] = out[0]
        si[...] = out[1]
        if with_acc:
            accr[...] = out[2]
            acci[...] = out[3]

            @pl.when(j == nt - 1)
            def _():
                dr_ref[...] = jnp.sum(out[2], axis=0, keepdims=True)
                di_ref[...] = jnp.sum(out[3], axis=0, keepdims=True)

    bspec = pl.BlockSpec((tr, tn), tmap)
    fspec = pl.BlockSpec((SUBLANES, tn), lambda c, j: (0, c))
    dspec = pl.BlockSpec((1, tn), lambda c, j: (0, c))
    in_specs = [bspec, bspec, pl.BlockSpec((2, tn), lambda c, j: (0, c)), fspec, fspec] + ([bspec, bspec] if with_acc else [])
    out_specs = [bspec, bspec] + ([dspec, dspec] if with_acc else [])
    out_shape = [jax.ShapeDtypeStruct((s, n), F32)] * 2 + ([jax.ShapeDtypeStruct((1, n), F32)] * 2 if with_acc else [])
    scratch = [pltpu.VMEM((SUBLANES, tn), F32)] * (4 if with_acc else 2)
    args = (b_r, b_i, a, f_r, f_i) + (tuple(xs) if with_acc else ())
    return pl.pallas_call(
        body,
        name=name,
        grid=(nc, nt),
        in_specs=in_specs,
        out_specs=out_specs,
        out_shape=out_shape,
        scratch_shapes=scratch,
        compiler_params=_cparams("parallel", "arbitrary"),
    )(*args)


@jax.custom_vjp
def s5_scan(b_r, b_i, a):
    f_r, f_i = _scan_finals(b_r, b_i, a, False, "s5_scan_fin")
    return tuple(_scan_states(b_r, b_i, a, f_r, f_i, False, None, "s5_scan"))


def _s5_scan_fwd(b_r, b_i, a):
    xs = s5_scan(b_r, b_i, a)
    return xs, (a, xs)


def _s5_scan_bwd(res, g):
    a, xs = res
    a_conj = a * jnp.array([[1.0], [-1.0]], F32)
    f_r, f_i = _scan_finals(g[0], g[1], a_conj, True, "s5_rscan_fin")
    g_r, g_i, da_r, da_i = _scan_states(g[0], g[1], a_conj, f_r, f_i, True, xs, "s5_rscan")
    return g_r, g_i, jnp.concatenate([da_r, da_i], axis=0)


s5_scan.defvjp(_s5_scan_fwd, _s5_scan_bwd)


BD_SEG_ROWS = 128


def _bd_view(x, seg):
    return x if seg else x.reshape(SCAN_SEGMENTS, x.shape[0] // SCAN_SEGMENTS, x.shape[1])


def _bd_spec(seg, w, rows):
    if seg:
        return pl.BlockSpec((SCAN_SEGMENTS * rows, w), lambda i, j: (i, j))
    return pl.BlockSpec((SCAN_SEGMENTS, rows, w), lambda i, j: (0, i, j))


def _bd_apply_call(a, w, transpose_w, a_seg, out_seg, add, name):
    s = a.shape[0]
    nb, ka, kn = w.shape
    wi, wo = (kn, ka) if transpose_w else (ka, kn)
    convert = a_seg != out_seg
    a_seg_k, out_seg_k = (a_seg, out_seg) if convert else (True, True)
    ov_shape = (s, nb * wo) if out_seg_k else (SCAN_SEGMENTS, s // SCAN_SEGMENTS, nb * wo)
    dn = _NT_DIMS if transpose_w else _NN_DIMS
    has_add = add is not None
    rows = min(BD_SEG_ROWS, s // SCAN_SEGMENTS)

    assert not convert or (wo if a_seg_k else wi) == LANES

    def body(*refs):
        a_ref, w_ref = refs[0], refs[1]
        c_ref = refs[2] if has_add else None
        o_ref = refs[3] if has_add else refs[2]
        wb = w_ref[0].astype(BF16)
        mm = lambda a_val: lax.dot_general(a_val.astype(BF16), wb, dn, preferred_element_type=F32)
        if not convert:
            r = mm(a_ref[...])
            o_ref[...] = r + c_ref[...] if has_add else r
        elif out_seg_k:
            scr = refs[-1]
            for k in range(SCAN_SEGMENTS):
                scr[pl.ds(k, rows, stride=SCAN_SEGMENTS), :] = a_ref[k]
            r = mm(scr[...])
            o_ref[...] = r + c_ref[...] if has_add else r
        else:
            scr = refs[-1]
            scr[...] = mm(a_ref[...])
            for k in range(SCAN_SEGMENTS):
                r = scr[pl.ds(k, rows, stride=SCAN_SEGMENTS), :]
                o_ref[k] = r + c_ref[k] if has_add else r

    args = [_bd_view(a, a_seg_k), w] + ([_bd_view(add, out_seg_k)] if has_add else [])
    in_specs = ([_bd_spec(a_seg_k, wi, rows), pl.BlockSpec((1, ka, kn), lambda i, j: (j, 0, 0))]
                + ([_bd_spec(out_seg_k, wo, rows)] if has_add else []))
    out = pl.pallas_call(
        body,
        name=name,
        grid=(s // (SCAN_SEGMENTS * rows), nb),
        in_specs=in_specs,
        out_specs=_bd_spec(out_seg_k, wo, rows),
        out_shape=jax.ShapeDtypeStruct(ov_shape, F32),
        scratch_shapes=[pltpu.VMEM((SCAN_SEGMENTS * rows, LANES), F32)] if convert else [],
        compiler_params=_cparams("parallel", "parallel"),
    )(*args)
    return out.reshape(s, nb * wo)


def _bd_weight_grad_call(a, g, a_seg, g_seg, nb, name):
    s = a.shape[0]
    ka, kn = a.shape[1] // nb, g.shape[1] // nb
    seg_len = s // SCAN_SEGMENTS
    convert = a_seg != g_seg

    def view(x, seg, w):
        if not convert:
            return x, pl.BlockSpec((s, w), lambda j: (0, j))
        if seg:
            return x, pl.BlockSpec((s, w), lambda j: (0, j))
        return x.reshape(SCAN_SEGMENTS, seg_len, x.shape[1]), pl.BlockSpec((SCAN_SEGMENTS, seg_len, w), lambda j: (0, 0, j))

    av, a_spec = view(a, a_seg, ka)
    gv, g_spec = view(g, g_seg, kn)

    assert not convert or (kn if a_seg else ka) == LANES

    def body(a_ref, g_ref, o_ref, *scratch):
        tn = lambda x, y: lax.dot_general(x.astype(BF16), y.astype(BF16), _TN_DIMS, preferred_element_type=F32)
        if not convert:
            o_ref[0] = tn(a_ref[...], g_ref[...])
        else:
            scr = scratch[0]
            t_ref = g_ref if a_seg else a_ref
            for k in range(SCAN_SEGMENTS):
                scr[pl.ds(k, seg_len, stride=SCAN_SEGMENTS), :] = t_ref[k]
            o_ref[0] = tn(a_ref[...], scr[...]) if a_seg else tn(scr[...], g_ref[...])

    return pl.pallas_call(
        body,
        name=name,
        grid=(nb,),
        in_specs=[a_spec, g_spec],
        out_specs=pl.BlockSpec((1, ka, kn), lambda j: (j, 0, 0)),
        out_shape=jax.ShapeDtypeStruct((nb, ka, kn), F32),
        scratch_shapes=[pltpu.VMEM((s, LANES), F32)] if convert else [],
        compiler_params=_cparams("parallel"),
    )(av, gv)


_NT_DIMS = (((1,), (1,)), ((), ()))
_TN_DIMS = (((0,), (0,)), ((), ()))


@functools.partial(jax.custom_vjp, nondiff_argnums=(2, 3))
def bd_matmul(a, w, a_seg, out_seg):
    return _bd_apply_call(a, w, False, a_seg, out_seg, None, "bd_mm_fwd")


def _bd_matmul_fwd(a, w, a_seg, out_seg):
    return bd_matmul(a, w, a_seg, out_seg), (a, w)


def _bd_matmul_bwd(a_seg, out_seg, res, g):
    a, w = res
    da = _bd_apply_call(g, w, True, out_seg, a_seg, None, "bd_mm_da")
    dw = _bd_weight_grad_call(a, g, a_seg, out_seg, w.shape[0], "bd_mm_dw")
    return da, dw


bd_matmul.defvjp(_bd_matmul_fwd, _bd_matmul_bwd)


@functools.partial(jax.custom_vjp, nondiff_argnums=(3, 4))
def bd_matmul_add(a, w, c, a_seg, out_seg):
    return _bd_apply_call(a, w, False, a_seg, out_seg, c, "bd_mm_add_fwd")


def _bd_matmul_add_fwd(a, w, c, a_seg, out_seg):
    return bd_matmul_add(a, w, c, a_seg, out_seg), (a, w)


def _bd_matmul_add_bwd(a_seg, out_seg, res, g):
    return _bd_matmul_bwd(a_seg, out_seg, res, g) + (g,)


bd_matmul_add.defvjp(_bd_matmul_add_fwd, _bd_matmul_add_bwd)


_NN = (((1,), (0,)), ((), ()))
_NT = (((1,), (1,)), ((), ()))
_TN = (((0,), (0,)), ((), ()))


def _dot(a, b, dn):
    return lax.dot_general(a.astype(BF16), b.astype(BF16), dn, preferred_element_type=F32)


@jax.custom_vjp
def bdot_nn(a, b):
    return _dot(a, b, _NN)


bdot_nn.defvjp(lambda a, b: (_dot(a, b, _NN), (a, b)),
               lambda r, g: (_dot(g, r[1], _NT).astype(r[0].dtype), _dot(r[0], g, _TN).astype(r[1].dtype)))


@jax.custom_vjp
def bdot_nt(a, b):
    return _dot(a, b, _NT)


bdot_nt.defvjp(lambda a, b: (_dot(a, b, _NT), (a, b)),
               lambda r, g: (_dot(g, r[1], _NN).astype(r[0].dtype), _dot(g, r[0], _TN).astype(r[1].dtype)))


@jax.custom_vjp
def bdot_tn(a, b):
    return _dot(a, b, _TN)


bdot_tn.defvjp(lambda a, b: (_dot(a, b, _TN), (a, b)),
               lambda r, g: (_dot(r[1], g, _NT).astype(r[0].dtype), _dot(r[0], g, _NN).astype(r[1].dtype)))


def _split3(x):
    h = x.astype(BF16)
    r = x - h.astype(F32)
    m = r.astype(BF16)
    l = (r - m.astype(F32)).astype(BF16)
    return h, m, l


def _exact_dot(t, x, dn):
    h, m, l = _split3(x)
    d = lambda p: lax.dot_general(t, p, dn, preferred_element_type=F32)
    return d(h) + d(m) + d(l)


@jax.custom_vjp
def select_dot(t, x):
    return _exact_dot(t, x, _NN)


select_dot.defvjp(lambda t, x: (_exact_dot(t, x, _NN), t),
                  lambda t, g: (jnp.zeros_like(t), _exact_dot(t, g, _TN)))


def _split_rows_impl(x, h):
    return tuple(x[i * h:(i + 1) * h] for i in range(x.shape[0] // h))


@functools.partial(jax.custom_vjp, nondiff_argnums=(1,))
def split_rows(x, h):
    return _split_rows_impl(x, h)


split_rows.defvjp(lambda x, h: (_split_rows_impl(x, h), None),
                  lambda h, r, g: (jnp.concatenate(g, axis=0),))


@jax.custom_vjp
def join_rows(parts):
    return jnp.concatenate(parts, axis=0)


def _join_rows_bwd(hs, g):
    out, off = [], 0
    for h in hs:
        out.append(g[off:off + h])
        off += h
    return (tuple(out),)


join_rows.defvjp(lambda parts: (jnp.concatenate(parts, axis=0), tuple(p.shape[0] for p in parts)), _join_rows_bwd)


def _split_lanes_impl(x, w):
    return tuple(x[:, i * w:(i + 1) * w] for i in range(x.shape[1] // w))


@functools.partial(jax.custom_vjp, nondiff_argnums=(1,))
def split_lanes(x, w):
    return _split_lanes_impl(x, w)


split_lanes.defvjp(lambda x, w: (_split_lanes_impl(x, w), None),
                   lambda w, r, g: (jnp.concatenate(g, axis=1),))


def _join_impl(parts):
    return jnp.concatenate(parts, axis=1)


@jax.custom_vjp
def join_lanes(parts):
    return _join_impl(parts)


def _join_bwd(ws, g):
    out, off = [], 0
    for w in ws:
        out.append(g[:, off:off + w])
        off += w
    return (tuple(out),)


join_lanes.defvjp(lambda parts: (_join_impl(parts), tuple(p.shape[1] for p in parts)), _join_bwd)


def _rope_impl(x, c, sa, sb, shift):
    w = x.shape[1]
    return x * c + pltpu.roll(x, w - shift, 1) * sa + pltpu.roll(x, shift, 1) * sb


@functools.partial(jax.custom_vjp, nondiff_argnums=(4,))
def rope_lanes(x, c, sa, sb, shift):
    return _rope_impl(x, c, sa, sb, shift)


def _rope_bwd(shift, r, g):
    c, sa, sb = r
    w = g.shape[1]
    dx = g * c + pltpu.roll(g * sa, shift, 1) + pltpu.roll(g * sb, w - shift, 1)
    return dx, jnp.zeros_like(c), jnp.zeros_like(sa), jnp.zeros_like(sb)


rope_lanes.defvjp(lambda x, c, sa, sb, shift: (_rope_impl(x, c, sa, sb, shift), (c, sa, sb)), _rope_bwd)


RMS_EPS = 1e-6


def _rms(x, g):
    return x * lax.rsqrt(jnp.mean(x * x, axis=-1, keepdims=True) + RMS_EPS) * g


ATTN_BLOCK = 512
MASK_VALUE = -1e30
LOG2E = math.log2(math.e)
LN2 = math.log(2.0)
V_ONES_LANE = 64


def _causal_mask(t):
    r = lax.broadcasted_iota(jnp.int32, (t, t), 0)
    c = lax.broadcasted_iota(jnp.int32, (t, t), 1)
    return c <= r


def _attn_fwd_call(q, k, v):
    s, width = q.shape
    n_heads = width // LANES
    tq = min(ATTN_BLOCK, s)
    nq = s // tq

    def body(q_ref, k_ref, v_ref, o_ref, lse_ref):
        i = pl.program_id(1)
        qb = q_ref[...].astype(BF16)
        ones_lane = lax.broadcasted_iota(jnp.int32, (tq, LANES), 1) == V_ONES_LANE

        def block(kb, carry, masked):
            m, acc = carry
            rows = pl.ds(pl.multiple_of(kb * tq, tq), tq)
            sc = lax.dot_general(qb, k_ref[rows, :].astype(BF16), _NT, preferred_element_type=F32)
            if masked:
                sc = jnp.where(_causal_mask(tq), sc, MASK_VALUE)
            m_new = jnp.maximum(m, jnp.max(sc, axis=-1, keepdims=True))
            p = jnp.exp2(sc - m_new).astype(BF16)
            vb = jnp.where(ones_lane, 1.0, v_ref[rows, :]).astype(BF16)
            acc = jnp.exp2(m - m_new) * acc + lax.dot_general(p, vb, _NN, preferred_element_type=F32)
            return m_new, acc

        init = (jnp.full((tq, 1), MASK_VALUE, F32), jnp.zeros((tq, LANES), F32))
        carry = lax.fori_loop(0, i, lambda kb, c: block(kb, c, False), init)
        m, acc = block(i, carry, True)
        l = jnp.sum(jnp.where(ones_lane, acc, 0.0), axis=-1, keepdims=True)
        o_ref[...] = jnp.where(ones_lane, 0.0, acc / l).astype(o_ref.dtype)
        lse_ref[...] = jnp.broadcast_to(m + jnp.log2(l), (tq, LANES))

    qspec = pl.BlockSpec((tq, LANES), lambda h, i: (i, h))
    kspec = pl.BlockSpec((s, LANES), lambda h, i: (0, h))
    return pl.pallas_call(
        body,
        name="mla_attn_fwd",
        grid=(n_heads, nq),
        in_specs=[qspec, kspec, kspec],
        out_specs=[qspec, qspec],
        out_shape=[jax.ShapeDtypeStruct((s, width), BF16), jax.ShapeDtypeStruct((s, width), F32)],
        compiler_params=_cparams("parallel", "parallel"),
    )(q, k, v)


def _attn_bwd_call(q, k, v, o, lse, do):
    s, width = q.shape
    n_heads = width // LANES
    tq = min(ATTN_BLOCK, s)
    nq = s // tq

    def body(q_ref, k_ref, v_ref, o_ref, lse_ref, do_ref, dq_ref, dk_ref, dv_ref, dq_acc):
        j = pl.program_id(1)

        @pl.when(j == 0)
        def _():
            dq_acc[...] = jnp.zeros_like(dq_acc)

        kb = k_ref[...].astype(BF16)
        vb = v_ref[...].astype(BF16)

        def block(i, carry, masked):
            dk, dv = carry
            rows = pl.ds(pl.multiple_of(i * tq, tq), tq)
            qi = q_ref[rows, :].astype(BF16)
            doi = do_ref[rows, :].astype(F32)
            delta = jnp.sum(doi * o_ref[rows, :].astype(F32), axis=-1, keepdims=True)
            sc = lax.dot_general(qi, kb, _NT, preferred_element_type=F32)
            if masked:
                sc = jnp.where(_causal_mask(tq), sc, MASK_VALUE)
            p = jnp.exp2(sc - lse_ref[rows, 0:1])
            dob = doi.astype(BF16)
            dv = dv + lax.dot_general(p.astype(BF16), dob, _TN, preferred_element_type=F32)
            dp = lax.dot_general(dob, vb, _NT, preferred_element_type=F32)
            ds = (p * (dp - delta)).astype(BF16)
            dq_acc[rows, :] += lax.dot_general(ds, kb, _NN, preferred_element_type=F32)
            dk = dk + lax.dot_general(ds, qi, _TN, preferred_element_type=F32)
            return dk, dv

        zero = jnp.zeros((tq, LANES), F32)
        carry = block(j, (zero, zero), True)
        dk, dv = lax.fori_loop(j + 1, nq, lambda i, c: block(i, c, False), carry)
        dk_ref[...] = (dk * LN2).astype(dk_ref.dtype)
        dv_ref[...] = dv.astype(dv_ref.dtype)

        @pl.when(j == nq - 1)
        def _():
            dq_ref[...] = (dq_acc[...] * LN2).astype(dq_ref.dtype)

    full = pl.BlockSpec((s, LANES), lambda h, j: (0, h))
    blk = pl.BlockSpec((tq, LANES), lambda h, j: (j, h))
    return pl.pallas_call(
        body,
        name="mla_attn_bwd",
        grid=(n_heads, nq),
        in_specs=[full, blk, blk, full, full, full],
        out_specs=[full, blk, blk],
        out_shape=[jax.ShapeDtypeStruct((s, width), t.dtype) for t in (q, k, v)],
        scratch_shapes=[pltpu.VMEM((s, LANES), F32)],
        compiler_params=_cparams("parallel", "arbitrary"),
    )(q, k, v, o, lse, do)


@jax.custom_vjp
def causal_attention(q, k, v):
    return _attn_fwd_call(q, k, v)[0]


def _causal_attention_fwd(q, k, v):
    o, lse = _attn_fwd_call(q, k, v)
    return o, (q, k, v, o, lse)


def _causal_attention_bwd(res, do):
    return tuple(_attn_bwd_call(*res, do))


causal_attention.defvjp(_causal_attention_fwd, _causal_attention_bwd)


HG_HEADS = 4
HG_CHUNK = 32
HG_REF_ROW = HG_CHUNK // 2 - 1
HG_TILE_ROWS = 256
HG_EXP_CLAMP = 80.0


def _hg_tile_masks(t):
    shift = HG_CHUNK.bit_length() - 1
    r = lax.broadcasted_iota(jnp.int32, (t, t), 0)
    c = lax.broadcasted_iota(jnp.int32, (t, t), 1)
    start = lax.shift_left(lax.shift_right_logical(r, shift), shift)
    causal = (c >= start) & (c <= r)
    return causal, c == start + HG_REF_ROW, c == start + (HG_CHUNK - 1)


def _hg_tile(q, fl, v, lb, st):
    t = q.shape[0]
    causal, ref_sel, last_sel = _hg_tile_masks(t)
    f = lb + (1.0 - lb) * jax.nn.sigmoid(fl)
    kk = 1.0 - f
    qs = q * jax.nn.sigmoid(q)
    b = select_dot(causal.astype(BF16), jnp.log(f))
    b_ref = select_dot(ref_sel.astype(BF16), b)
    b_last = select_dot(last_sel.astype(BF16), b)
    q_in = qs * jnp.exp(jnp.minimum(b - b_ref, HG_EXP_CLAMP))
    k_in = kk * jnp.exp(jnp.minimum(b_ref - b, HG_EXP_CLAMP))
    o = bdot_nn(jnp.where(causal, bdot_nt(q_in, k_in), 0.0), v)
    q_hat = split_rows(qs * jnp.exp(b), HG_CHUNK)
    k_hat = split_rows(kk * jnp.exp(b_last - b), HG_CHUNK)
    decay = split_rows(jnp.exp(b_last), HG_CHUNK)
    vs = split_rows(v, HG_CHUNK)
    first_row = lax.broadcasted_iota(jnp.int32, (HG_CHUNK, LANES), 0) == 0
    inter = []
    for c in range(t // HG_CHUNK):
        inter.append(bdot_nt(q_hat[c], st))
        st = st * jnp.sum(jnp.where(first_row, decay[c], 0.0), axis=0, keepdims=True) + bdot_tn(vs[c], k_hat[c])
    return o + join_rows(tuple(inter)), st


def _hg_head(q, fl, v, gate, lb, gn, st):
    o, st = _hg_tile(q, fl, v, lb, st)
    return _rms(o, gn) * (gate * jax.nn.sigmoid(gate)), st


HG_PARTS = 4


def _hg_part_slices(h, width):
    return [slice(p * width + h * LANES, p * width + (h + 1) * LANES) for p in range(HG_PARTS)]


def _hg_fwd_call(x, lb, gn):
    s = x.shape[0]
    width = x.shape[1] // HG_PARTS
    tr = min(HG_TILE_ROWS, s)
    nt = s // tr

    def body(x_ref, lb_ref, gn_ref, o_ref, sts_ref, st_ref):
        @pl.when(pl.program_id(0) == 0)
        def _():
            st_ref[...] = jnp.zeros_like(st_ref)

        for h in range(HG_HEADS):
            ln = slice(h * LANES, (h + 1) * LANES)
            st = st_ref[h]
            sts_ref[0, h] = st
            o, st_new = _hg_head(*(x_ref[:, sl] for sl in _hg_part_slices(h, width)), lb_ref[:, ln], gn_ref[...], st)
            o_ref[:, ln] = o.astype(o_ref.dtype)
            st_ref[h] = st_new

    const = lambda shape: pl.BlockSpec(shape, lambda j: (0, 0))
    return pl.pallas_call(
        body,
        name="hgrn2_fwd",
        grid=(nt,),
        in_specs=[pl.BlockSpec((tr, HG_PARTS * width), lambda j: (j, 0)), const((1, width)), const((1, LANES))],
        out_specs=[pl.BlockSpec((tr, width), lambda j: (j, 0)),
                   pl.BlockSpec((1, HG_HEADS, LANES, LANES), lambda j: (j, 0, 0, 0))],
        out_shape=[jax.ShapeDtypeStruct((s, width), BF16),
                   jax.ShapeDtypeStruct((nt, HG_HEADS, LANES, LANES), F32)],
        scratch_shapes=[pltpu.VMEM((HG_HEADS, LANES, LANES), F32)],
        compiler_params=_cparams("arbitrary"),
    )(x, lb, gn)


def _hg_bwd_call(x, lb, gn, sts, do):
    s = x.shape[0]
    width = x.shape[1] // HG_PARTS
    tr = min(HG_TILE_ROWS, s)
    nt = s // tr

    def body(x_ref, lb_ref, gn_ref, sts_ref, do_ref, dx_ref, dlb_ref, dgn_ref, dst_ref):
        @pl.when(pl.program_id(0) == 0)
        def _():
            dst_ref[...] = jnp.zeros_like(dst_ref)
            dlb_ref[...] = jnp.zeros_like(dlb_ref)
            dgn_ref[...] = jnp.zeros_like(dgn_ref)

        for h in range(HG_HEADS):
            ln = slice(h * LANES, (h + 1) * LANES)
            parts = _hg_part_slices(h, width)
            _, vjp = jax.vjp(_hg_head, *(x_ref[:, sl] for sl in parts), lb_ref[:, ln], gn_ref[...], sts_ref[0, h])
            cts = vjp((do_ref[:, ln].astype(F32), dst_ref[h]))
            for sl, ct in zip(parts, cts[:HG_PARTS]):
                dx_ref[:, sl] = ct.astype(dx_ref.dtype)
            dlb_ref[:, ln] += cts[HG_PARTS]
            dgn_ref[...] += cts[HG_PARTS + 1]
            dst_ref[h] = cts[HG_PARTS + 2]

    rev = lambda w: pl.BlockSpec((tr, w), lambda j: (nt - 1 - j, 0))
    const = lambda shape: pl.BlockSpec(shape, lambda j: (0, 0))
    return pl.pallas_call(
        body,
        name="hgrn2_bwd",
        grid=(nt,),
        in_specs=[rev(HG_PARTS * width), const((1, width)), const((1, LANES)),
                  pl.BlockSpec((1, HG_HEADS, LANES, LANES), lambda j: (nt - 1 - j, 0, 0, 0)), rev(width)],
        out_specs=[rev(HG_PARTS * width), const((1, width)), const((1, LANES))],
        out_shape=[jax.ShapeDtypeStruct(x.shape, BF16), jax.ShapeDtypeStruct((1, width), F32),
                   jax.ShapeDtypeStruct((1, LANES), F32)],
        scratch_shapes=[pltpu.VMEM((HG_HEADS, LANES, LANES), F32)],
        compiler_params=_cparams("arbitrary"),
    )(x, lb, gn, sts, do)


@jax.custom_vjp
def hgrn2_mixer(h, w, lb, gn):
    return _hg_fwd_call(_mm_call(h, w, False, False, name="hgrn2_proj"), lb, gn)[0]


def _hgrn2_mixer_fwd(h, w, lb, gn):
    x = _mm_call(h, w, False, False, name="hgrn2_proj")
    o, sts = _hg_fwd_call(x, lb, gn)
    return o, (h, w, x, lb, gn, sts)


def _hgrn2_mixer_bwd(res, do):
    h, w, x, lb, gn, sts = res
    dx, dlb, dgn = _hg_bwd_call(x, lb, gn, sts, do)
    dh = _mm_call(dx, w, False, True, out_dtype=h.dtype, name="hgrn2_proj_da")
    dw = _mm_call(h, dx, True, False, out_dtype=w.dtype, name="hgrn2_proj_db")
    return dh, dw, dlb, dgn


hgrn2_mixer.defvjp(_hgrn2_mixer_fwd, _hgrn2_mixer_bwd)


D_MODEL = 1024
DEPTH = 2
SSM_GROUPS, SSM_GROUP_CH, SSM_STATE = 32, 16, 64
SSM_WIDTH = SSM_GROUPS * SSM_GROUP_CH
MLA_HEADS, MLA_NOPE, MLA_ROPE, MLA_V = 8, 64, 32, 64
MLA_Q_RANK, MLA_KV_RANK = 512, 256
HG_WIDTH = HG_HEADS * LANES
X_HEADS, X_HEAD_DIM = 4, 128
X_WIDTH = X_HEADS * X_HEAD_DIM
D_FF = 2816
ROPE_THETA = 10000.0
IN_SPLITS = (SSM_WIDTH, MLA_Q_RANK, MLA_KV_RANK, MLA_ROPE, HG_WIDTH, HG_WIDTH, HG_WIDTH, HG_WIDTH, 3 * D_MODEL)
ROPE_LANE0 = MLA_NOPE
MLA_Q_SCALE = LOG2E / math.sqrt(MLA_NOPE + MLA_ROPE)
ROW_TILE = 256


def _t_rms(x, g):
    return (_rms(x, g).astype(BF16),)


def _t_s5_act(y, u, d):
    return (jax.nn.gelu(y + d * u).astype(BF16),)


def _t_glu(z):
    zo, zg = split_lanes(z.astype(F32), D_MODEL)
    return (zo * jax.nn.sigmoid(zg),)


def _t_mla_rope(q, k, kr, c, sa, sb):
    rep = lambda t: jnp.concatenate([t] * MLA_HEADS, axis=1)
    half = MLA_ROPE // 2
    q_out = rope_lanes(q, rep(c), rep(sa), rep(sb), half) * MLA_Q_SCALE
    kr_out = rope_lanes(kr, c, sa, sb, half)
    return q_out.astype(BF16), (k + join_lanes((kr_out,) * MLA_HEADS)).astype(BF16)


def _t_merge(y_ssm, y_mla, y_hg, gates):
    g0, g1, g2 = split_lanes(gates.astype(F32), D_MODEL)
    return ((jax.nn.sigmoid(g0) * y_ssm + jax.nn.sigmoid(g1) * y_mla + jax.nn.sigmoid(g2) * y_hg).astype(BF16),)


def _t_xattn(q, kv):
    scale = 1.0 / math.sqrt(X_HEAD_DIM)
    heads = split_lanes(kv, X_HEAD_DIM)
    outs = []
    for qh, kh, vh in zip(split_lanes(q, X_HEAD_DIM), heads[:X_HEADS], heads[X_HEADS:]):
        sc = bdot_nt(qh, kh) * scale
        p = jnp.exp(sc - jnp.max(sc, axis=-1, keepdims=True))
        p = p / jnp.sum(p, axis=-1, keepdims=True)
        outs.append(bdot_nn(p, vh))
    return (join_lanes(tuple(outs)).astype(BF16),)


def _t_swiglu(gate_up):
    gt, up = split_lanes(gate_up.astype(F32), D_FF)
    return ((gt * jax.nn.sigmoid(gt) * up).astype(BF16),)


def _t_loss(x, tgt, g):
    e = _rms(x, g) - tgt
    return (jnp.broadcast_to(jnp.mean(e * e, axis=-1, keepdims=True), (x.shape[0], LANES)),)


rms_op = rowwise(_t_rms, 1, 0, ROW_TILE, "rmsnorm")
rms_res_op = rowwise(_t_rms, 1, 0, ROW_TILE, "rmsnorm_res", passthrough=True)
s5_act_op = rowwise(_t_s5_act, 2, 0, ROW_TILE, "s5_act")
glu_op = rowwise(_t_glu, 1, 0, ROW_TILE, "glu")
mla_rope_op = rowwise(_t_mla_rope, 3, 3, ROW_TILE, "mla_rope")
merge_op = rowwise(_t_merge, 4, 0, ROW_TILE, "merge")
xattn_op = rowwise(_t_xattn, 1, 0, ROW_TILE, "xattn")
swiglu_op = rowwise(_t_swiglu, 1, 0, ROW_TILE, "swiglu")
loss_op = rowwise(_t_loss, 1, 1, ROW_TILE, "loss")


def _rope_tables(positions):
    half = MLA_ROPE // 2
    inv_freq = ROPE_THETA ** (-jnp.arange(half, dtype=F32) / half)
    ang = positions.astype(F32)[:, None] * inv_freq
    cos, sin = jnp.cos(ang), jnp.sin(ang)
    s = positions.shape[0]
    z = lambda w: jnp.zeros((s, w), F32)
    tail = LANES - ROPE_LANE0 - MLA_ROPE
    c = jnp.concatenate([jnp.ones((s, ROPE_LANE0), F32), cos, cos, z(tail)], axis=1)
    sa = jnp.concatenate([z(ROPE_LANE0), -sin, z(half), z(tail)], axis=1)
    sb = jnp.concatenate([z(ROPE_LANE0), z(half), sin, z(tail)], axis=1)
    return c, sa, sb


def _s5_operators(lam_re, lam_im, b_re, b_im, c_re, c_im, log_step):
    g, p, h = SSM_GROUPS, SSM_STATE, SSM_GROUP_CH
    lam = lax.complex(lam_re, lam_im)
    lam_bar = jnp.exp(lam * jnp.exp(log_step)[:, None])
    b_bar = ((lam_bar - 1.0) / lam)[..., None] * lax.complex(b_re, b_im)
    per = LANES // h
    nb = g // per
    eye = jnp.eye(per, dtype=F32)
    bd = lambda t: jnp.einsum("jgph,gk->jghkp", t.reshape(nb, per, p, h), eye).reshape(nb, per * h, per * p)
    cd = lambda t: jnp.einsum("jghp,gk->jgpkh", t.reshape(nb, per, h, p), eye).reshape(nb, per * p, per * h)
    a = jnp.stack([jnp.real(lam_bar).reshape(-1), jnp.imag(lam_bar).reshape(-1)])
    return a, bd(jnp.real(b_bar)), bd(jnp.imag(b_bar)), cd(c_re), cd(-c_im)


LATENT_WIDTH = 1536
_LATENT = {}
_off = 0
for _name, _w in (("u", SSM_WIDTH), ("q_lat", MLA_Q_RANK), ("kv_lat", MLA_KV_RANK), ("k_rope", LANES)):
    _LATENT[_name] = (_off, _off + _w)
    _off += _w


def _layer_matrices(w, l):
    return {**_mixer_matrices(w, l), **_tail_matrices(w, l)}


def _tail_matrices(w, l):
    return dict(x_q=w["x_w_q"][l], x_kv=w["x_w_kv"][l], x_o=w["x_w_o"][l], ffn_gu=w["ffn_w_gate_up"][l],
                ffn_d=w["ffn_w_down"][l])


def _mixer_matrices(w, l):
    w_in = w["w_in"][l]
    d, dt = w_in.shape[0], w_in.dtype
    z = lambda n: jnp.zeros((d, n), dt)
    r0 = SSM_WIDTH + MLA_Q_RANK + MLA_KV_RANK
    r1 = r0 + MLA_ROPE
    r2 = r1 + HG_PARTS * HG_WIDTH
    w_latent = jnp.concatenate([w_in[:, :r0], z(ROPE_LANE0), w_in[:, r0:r1],
                                z(LATENT_WIDTH - r0 - ROPE_LANE0 - MLA_ROPE)], axis=1)
    pad_heads = lambda t: jnp.pad(t, ((0, 0), (0, 0), (0, LANES - t.shape[2]))).reshape(t.shape[0], -1)
    uq = w["mla_w_uq"][l].reshape(MLA_Q_RANK, MLA_HEADS, MLA_NOPE + MLA_ROPE)
    ukv = w["mla_w_ukv"][l].reshape(MLA_KV_RANK, MLA_HEADS, MLA_NOPE + MLA_V)
    wo = w["mla_w_o"][l].reshape(MLA_HEADS, MLA_V, D_MODEL)
    return dict(
        w_latent=w_latent, w_hg=w_in[:, r1:r2], w_gates=w_in[:, r2:], glu=w["ssm_w_glu"][l],
        uq=pad_heads(uq), uk=pad_heads(ukv[:, :, :MLA_NOPE]), uv=pad_heads(ukv[:, :, MLA_NOPE:]),
        mla_o=jnp.pad(wo, ((0, 0), (0, LANES - MLA_V), (0, 0))).reshape(MLA_HEADS * LANES, D_MODEL),
        hg_o=w["hg_w_o"][l], w_out=w["w_out"][l])


def _layer(x, mem, tabs, m, sp, l, lower_bound):
    return _tail(_mixer(x, tabs, m, sp, l, lower_bound), mem, m, sp, l)


def _mixer(x, tabs, m, sp, l, lower_bound):
    row = lambda name: sp[name][l].reshape(1, -1)
    h, x = rms_res_op(x, row("norm_mix"))
    latent = matmul(h, m["w_latent"])
    seg = lambda name: latent[:, _LATENT[name][0]:_LATENT[name][1]]
    a, bd_r, bd_i, cd_r, cd_i = _s5_operators(*(sp[n][l] for n in (
        "ssm_lam_re", "ssm_lam_im", "ssm_b_re", "ssm_b_im", "ssm_c_re", "ssm_c_im", "ssm_log_step")))
    u = seg("u")
    x_r, x_i = s5_scan(bd_matmul(u, bd_r, False, True), bd_matmul(u, bd_i, False, True), a)
    y = bd_matmul_add(x_i, cd_i, bd_matmul(x_r, cd_r, True, False), True, False)
    (ya,) = s5_act_op(y, u, row("ssm_d"))
    (y_ssm,) = glu_op(matmul(ya, m["glu"], BF16))
    (qn,) = rms_op(seg("q_lat"), row("mla_q_norm"))
    (kvn,) = rms_op(seg("kv_lat"), row("mla_kv_norm"))
    q, k = mla_rope_op(matmul(qn, m["uq"]), matmul(kvn, m["uk"]), seg("k_rope"), *tabs)
    o = causal_attention(q, k, matmul(kvn, m["uv"], BF16))
    y_mla = matmul(o, m["mla_o"])
    y_hg = matmul(hgrn2_mixer(h, m["w_hg"], lower_bound, row("hg_g_norm")), m["hg_o"])
    (merged,) = merge_op(y_ssm, y_mla, y_hg, matmul(h, m["w_gates"], BF16))
    return matmul_add(merged, m["w_out"], x)


def _tail(x, mem, m, sp, l):
    row = lambda name: sp[name][l].reshape(1, -1)
    hc, x = rms_res_op(x, row("norm_cross"))
    (mn,) = rms_op(mem, row("norm_mem"))
    (ox,) = xattn_op(matmul(hc, m["x_q"], BF16), matmul(mn, m["x_kv"]))
    x = matmul_add(ox, m["x_o"], x)
    hf, x = rms_res_op(x, row("norm_ffn"))
    (act,) = swiglu_op(matmul(hf, m["ffn_gu"], BF16))
    return matmul_add(act, m["ffn_d"], x)


def _lower_bounds(hg_lb):
    lb_p = jax.nn.softmax(hg_lb, axis=0)
    return jnp.cumsum(lb_p, axis=0) - lb_p[0:1]


def _final_loss(target, x, norm_final):
    (row_loss,) = loss_op(x, target, norm_final.reshape(1, -1))
    return 0.5 * jnp.sum(row_loss[:, 0])


def _local_loss(x, mem, positions, target, w, sp):
    tabs = _rope_tables(positions)
    lower = _lower_bounds(sp["hg_lb"])
    for l in range(DEPTH):
        x = _layer(x, mem, tabs, _layer_matrices(w, l), sp, l, lower[l].reshape(1, -1))
    return _final_loss(target, x, sp["norm_final"])


N_DEV = 8
N_CHIPS = 4
COMM_LANES = 512
MESH_ID = pl.DeviceIdType.MESH
_ANY = pl.BlockSpec(memory_space=pl.ANY)
_OTHER_CHIPS = ((1, 0), (0, 1), (1, 1))


def _place():
    return lax.axis_index("x"), lax.axis_index("y"), lax.axis_index("c")


def _all_gather_call(blocks, name):
    n = len(blocks)

    def body(*refs):
        x_refs, out_refs = refs[:n], refs[n:2 * n]
        send_sems, recv_sems, local_sems = refs[2 * n:]
        x, y, c = _place()
        me, sibling = (x, y, c), (x, y, 1 - c)
        chips = [(x ^ fx, y ^ fy) for fx, fy in _OTHER_CHIPS]

        def slot(i, px, py, pc):
            return out_refs[i].at[4 * px + 2 * py + pc]

        def copy(i, k, blk, to, src=None):
            return pltpu.make_async_remote_copy(
                src_ref=slot(i, *blk) if src is None else src, dst_ref=slot(i, *blk),
                send_sem=send_sems.at[i, k], recv_sem=recv_sems.at[i, k], device_id=to, device_id_type=MESH_ID)

        mine = [pltpu.make_async_copy(x_refs[i], slot(i, *me), local_sems.at[i]) for i in range(n)]
        first = []
        for i in range(n):
            first.append(copy(i, 0, me, sibling, src=x_refs[i]))
            first += [copy(i, 1 + j, me, (*chip, c), src=x_refs[i]) for j, chip in enumerate(chips)]
        for cp in mine + first:
            cp.start()
        passed = []
        for j, chip in enumerate(chips):
            for i in range(n):
                copy(i, 1 + j, (*chip, c), me).wait_recv()
                passed.append(copy(i, 4 + j, (*chip, c), sibling))
                passed[-1].start()
        for i in range(n):
            copy(i, 0, sibling, me).wait_recv()
            for j, chip in enumerate(chips):
                copy(i, 4 + j, (*chip, 1 - c), me).wait_recv()
        for cp in first + passed:
            cp.wait_send()
        for cp in mine:
            cp.wait()

    return pl.pallas_call(
        body,
        name=name,
        out_shape=[jax.ShapeDtypeStruct((N_DEV,) + b.shape, b.dtype) for b in blocks],
        in_specs=[_ANY] * n,
        out_specs=[_ANY] * n,
        scratch_shapes=[pltpu.SemaphoreType.DMA((n, 7)), pltpu.SemaphoreType.DMA((n, 7)), pltpu.SemaphoreType.DMA((n,))],
    )(*blocks)


def _pair_exchange_call(gs, name):
    n = len(gs)

    def body(*refs):
        g_refs, got_refs = refs[:n], refs[n:2 * n]
        send_sems, recv_sems = refs[2 * n:]
        x, y, c = _place()
        sends = [pltpu.make_async_remote_copy(
            src_ref=g_refs[i].at[2 * p + (1 - c)], dst_ref=got_refs[i].at[p],
            send_sem=send_sems.at[i, p], recv_sem=recv_sems.at[i, p], device_id=(x, y, 1 - c), device_id_type=MESH_ID)
            for i in range(n) for p in range(N_CHIPS)]
        for cp in sends:
            cp.start()
        for cp in sends:
            cp.wait_recv()
        for cp in sends:
            cp.wait_send()

    return pl.pallas_call(
        body,
        name=name,
        out_shape=[jax.ShapeDtypeStruct((N_CHIPS,) + g.shape[1:], g.dtype) for g in gs],
        in_specs=[_ANY] * n,
        out_specs=[_ANY] * n,
        scratch_shapes=[pltpu.SemaphoreType.DMA((n, N_CHIPS))] * 2,
    )(*gs)


def _chip_exchange_call(parts, name):
    n = len(parts)

    def body(*refs):
        p_refs, got_refs = refs[:n], refs[n:2 * n]
        send_sems, recv_sems = refs[2 * n:]
        x, y, c = _place()
        sends = []
        for i in range(n):
            for k, (fx, fy) in enumerate(_OTHER_CHIPS):
                px, py = x ^ fx, y ^ fy
                sends.append(pltpu.make_async_remote_copy(
                    src_ref=p_refs[i].at[2 * px + py], dst_ref=got_refs[i].at[k],
                    send_sem=send_sems.at[i, k], recv_sem=recv_sems.at[i, k], device_id=(px, py, c), device_id_type=MESH_ID))
        for cp in sends:
            cp.start()
        for cp in sends:
            cp.wait_recv()
        for cp in sends:
            cp.wait_send()

    return pl.pallas_call(
        body,
        name=name,
        out_shape=[jax.ShapeDtypeStruct((3,) + p.shape[1:], p.dtype) for p in parts],
        in_specs=[_ANY] * n,
        out_specs=[_ANY] * n,
        scratch_shapes=[pltpu.SemaphoreType.DMA((n, 3))] * 2,
    )(*parts)


def _rows_cols(shape):
    return math.prod(shape[:-1]), shape[-1]


def _pair_sum_call(g, got, c_idx, name):
    rows, cols = _rows_cols(got.shape[1:])
    tr = _pick_tile(rows, (512, 256, 128, 64, 32, 16))

    def body(c_ref, a_ref, b_ref, o_ref):
        o_ref[...] = (a_ref[...].astype(F32) + b_ref[...].astype(F32)).astype(o_ref.dtype)

    spec = pl.BlockSpec((1, tr, cols), lambda p, i, c_ref: (p, i, 0))
    out = pl.pallas_call(
        body,
        name=name,
        grid_spec=pltpu.PrefetchScalarGridSpec(
            num_scalar_prefetch=1, grid=(N_CHIPS, rows // tr),
            in_specs=[pl.BlockSpec((1, tr, cols), lambda p, i, c_ref: (2 * p + c_ref[0], i, 0)), spec],
            out_specs=spec),
        out_shape=jax.ShapeDtypeStruct((N_CHIPS, rows, cols), got.dtype),
        compiler_params=_cparams("parallel", "parallel"),
    )(c_idx, g.reshape(N_DEV, rows, cols), got.reshape(N_CHIPS, rows, cols))
    return out.reshape(got.shape)


def _chip_sum_call(part, got, chip_idx, name):
    rows, cols = _rows_cols(got.shape[1:])
    tr = _pick_tile(rows, (512, 256, 128, 64, 32, 16))

    def body(p_ref, a_ref, b_ref, o_ref):
        acc = a_ref[0].astype(F32)
        for k in range(3):
            acc = acc + b_ref[k].astype(F32)
        o_ref[...] = acc

    out = pl.pallas_call(
        body,
        name=name,
        grid_spec=pltpu.PrefetchScalarGridSpec(
            num_scalar_prefetch=1, grid=(rows // tr,),
            in_specs=[pl.BlockSpec((1, tr, cols), lambda i, p_ref: (p_ref[0], i, 0)),
                      pl.BlockSpec((3, tr, cols), lambda i, p_ref: (0, i, 0))],
            out_specs=pl.BlockSpec((tr, cols), lambda i, p_ref: (i, 0))),
        out_shape=jax.ShapeDtypeStruct((rows, cols), F32),
        compiler_params=_cparams("parallel"),
    )(chip_idx, part.reshape(N_CHIPS, rows, cols), got.reshape(3, rows, cols))
    return out.reshape(got.shape[1:])


def _reduce_scatter(gs, name):
    x, y, c = _place()
    c_idx = c.astype(jnp.int32).reshape(1)
    chip_idx = (2 * x + y).astype(jnp.int32).reshape(1)
    gots = _pair_exchange_call(gs, name + "_pair")
    parts = [_pair_sum_call(g, got, c_idx, name + "_pair_sum") for g, got in zip(gs, gots)]
    gots = _chip_exchange_call(parts, name + "_chip")
    return [_chip_sum_call(p, got, chip_idx, name + "_chip_sum") for p, got in zip(parts, gots)]


_HBM = pl.BlockSpec(memory_space=pltpu.HBM)
_SEM = pl.BlockSpec(memory_space=pltpu.SEMAPHORE)
_SIDE_EFFECT = pltpu.SideEffectType.DATAFLOW_SIDE_EFFECTING
N_PEERS = N_DEV - 1


def _peer(k):
    x, y, c = _place()
    px, py, pc = x ^ ((k >> 2) & 1), y ^ ((k >> 1) & 1), c ^ (k & 1)
    return (px, py, pc), 4 * px + 2 * py + pc


def _exchange_copy(src_ref, land_ref, send_sems, recv_sems, i, k, scatter, receiving):
    x, y, c = _place()
    me = 4 * x + 2 * y + c
    peer, peer_idx = _peer(k)
    sem = i * N_PEERS + k - 1
    return pltpu.make_async_remote_copy(
        src_ref=src_ref.at[peer_idx] if scatter else src_ref, dst_ref=land_ref.at[peer_idx if receiving else me],
        send_sem=send_sems.at[sem], recv_sem=recv_sems.at[sem], device_id=peer, device_id_type=MESH_ID)


def _exchange_start_call(srcs, after, scatter, name):
    n = len(srcs)
    slot_shapes = [s.shape[1:] if scatter else s.shape for s in srcs]

    def body(*refs):
        src_refs, land_refs = refs[:n], refs[n:2 * n]
        send_sems, recv_sems = refs[2 * n + 1], refs[2 * n + 2]
        token = refs[-1]
        for i in range(n):
            for k in range(1, N_DEV):
                _exchange_copy(src_refs[i], land_refs[i], send_sems, recv_sems, i, k, scatter, False).start()
        token[...] = jnp.zeros_like(token)

    lands = [pltpu.with_memory_space_constraint(lax.empty((N_DEV,) + shp, s.dtype), pltpu.HBM)
             for shp, s in zip(slot_shapes, srcs)]
    out = pl.pallas_call(
        body,
        name=name,
        out_shape=([pltpu.SemaphoreType.DMA((n * N_PEERS,)), pltpu.SemaphoreType.DMA((n * N_PEERS,))]
                   + [pltpu.HBM(s.shape, s.dtype) for s in srcs] + [pltpu.HBM(l.shape, l.dtype) for l in lands]
                   + [jax.ShapeDtypeStruct((SUBLANES, LANES), F32)]),
        in_specs=[_HBM] * (2 * n) + [pl.BlockSpec(memory_space=pl.ANY)],
        out_specs=[_SEM, _SEM] + [_HBM] * (2 * n) + [pl.BlockSpec(memory_space=pltpu.VMEM)],
        input_output_aliases={j: 2 + j for j in range(2 * n)},
        compiler_params=pltpu.CompilerParams(has_side_effects=_SIDE_EFFECT),
    )(*[pltpu.with_memory_space_constraint(s, pltpu.HBM) for s in srcs], *lands, after)
    return out[0], out[1], list(out[2:2 + n]), list(out[2 + n:2 + 2 * n]), out[-1]


def _exchange_wait_call(started, after, scatter, name):
    send_sems, recv_sems, srcs, lands, _ = started
    n = len(srcs)

    def body(*refs):
        src_refs, land_refs = refs[:n], refs[n:2 * n]
        send_s, recv_s = refs[2 * n], refs[2 * n + 1]
        for i in range(n):
            for k in range(1, N_DEV):
                cp = _exchange_copy(src_refs[i], land_refs[i], send_s, recv_s, i, k, scatter, True)
                cp.wait_send()
                cp.wait_recv()

    out = pl.pallas_call(
        body,
        name=name,
        out_shape=[pltpu.HBM(s.shape, s.dtype) for s in srcs] + [pltpu.HBM(l.shape, l.dtype) for l in lands],
        in_specs=[_HBM] * (2 * n) + [_SEM, _SEM, pl.BlockSpec(memory_space=pl.ANY)],
        out_specs=[_HBM] * (2 * n),
        input_output_aliases={j: j for j in range(2 * n)},
        compiler_params=pltpu.CompilerParams(has_side_effects=_SIDE_EFFECT),
    )(*srcs, *lands, send_sems, recv_sems, after)
    return list(out[n:])


def _own_slot(land, own):
    x, y, c = _place()
    return lax.dynamic_update_index_in_dim(land, own, 4 * x + 2 * y + c, 0)


def _slot_sum_call(land, name):
    rows, cols = _rows_cols(land.shape[1:])
    tr = _pick_tile(rows, (256, 128, 64, 32, 16))

    def body(land_ref, o_ref):
        acc = land_ref[0].astype(F32)
        for s in range(1, N_DEV):
            acc = acc + land_ref[s].astype(F32)
        o_ref[...] = acc

    out = pl.pallas_call(
        body,
        name=name,
        grid=(rows // tr,),
        in_specs=[pl.BlockSpec((N_DEV, tr, cols), lambda i: (0, i, 0))],
        out_specs=pl.BlockSpec((tr, cols), lambda i: (i, 0)),
        out_shape=jax.ShapeDtypeStruct((rows, cols), F32),
        compiler_params=_cparams("parallel"),
    )(land.reshape(N_DEV, rows, cols))
    return out.reshape(land.shape[1:])


SMALL_BLOCK_ROWS = 16


def _pack_small(parts):
    flat = jnp.concatenate([p.reshape(-1) for p in parts])
    chunk = N_DEV * SMALL_BLOCK_ROWS * COMM_LANES
    flat = jnp.pad(flat, (0, (-flat.shape[0]) % chunk))
    return flat.reshape(N_DEV, -1, COMM_LANES)


def _unpack_small(buf, shapes):
    flat = buf.reshape(-1)
    out, off = [], 0
    for shp in shapes:
        n = math.prod(shp)
        out.append(flat[off:off + n].reshape(shp))
        off += n
    return out


SHARDED = dict(w_in=2, ssm_w_glu=2, mla_w_uq=2, mla_w_ukv=2, mla_w_o=2, hg_w_o=2, w_out=1, x_w_q=1, x_w_kv=1,
               x_w_o=2, ffn_w_gate_up=2, ffn_w_down=1)
REPLICATED = ("norm_mix", "ssm_lam_re", "ssm_lam_im", "ssm_b_re", "ssm_b_im", "ssm_c_re", "ssm_c_im", "ssm_d",
              "ssm_log_step", "mla_q_norm", "mla_kv_norm", "hg_lb", "hg_g_norm", "norm_cross", "norm_mem", "norm_ffn",
              "norm_final")


def _join_shards(stacked, axis):
    n, l, a, b = stacked.shape
    if axis == 1:
        return stacked.transpose(1, 0, 2, 3).reshape(l, n * a, b)
    return stacked.transpose(1, 2, 0, 3).reshape(l, a, n * b)


def _split_shards(full, axis):
    l, a, b = full.shape
    if axis == 1:
        return full.reshape(l, N_DEV, a // N_DEV, b).transpose(1, 0, 2, 3)
    return full.reshape(l, a, N_DEV, b // N_DEV).transpose(2, 0, 1, 3)


MIXER_WEIGHTS = ("w_in", "ssm_w_glu", "mla_w_uq", "mla_w_ukv", "mla_w_o", "hg_w_o", "w_out")
TAIL_WEIGHTS = ("x_w_q", "x_w_kv", "x_w_o", "ffn_w_gate_up", "ffn_w_down")


def _mixer_fn(l, tabs):
    def f(x, full, small, lower):
        m = _mixer_matrices(dict(zip(MIXER_WEIGHTS, full)), 0)
        return _mixer(x, tabs, m, dict(zip(REPLICATED, small)), l, lower[l].reshape(1, -1))
    return f


def _tail_fn(l, mem):
    def f(x, full, small):
        return _tail(x, mem, _tail_matrices(dict(zip(TAIL_WEIGHTS, full)), 0), dict(zip(REPLICATED, small)), l)
    return f


ADAM_LR, ADAM_B1, ADAM_B2, ADAM_EPS, ADAM_WD, ADAM_STEP = 0.001, 0.9, 0.999, 1e-08, 0.01, 10


def _adamw_update(w, g, m, v):
    m_new = ADAM_B1 * m + (1.0 - ADAM_B1) * g
    v_new = ADAM_B2 * v + (1.0 - ADAM_B2) * jnp.square(g)
    m_hat = m_new / (1.0 - ADAM_B1 ** ADAM_STEP)
    v_hat = v_new / (1.0 - ADAM_B2 ** ADAM_STEP)
    return -ADAM_LR * (m_hat / (jnp.sqrt(v_hat) + ADAM_EPS) + ADAM_WD * w), m_new, v_new


def _adamw_stacked_call(w, g, m, v, name):
    depth, rows, cols = w.shape
    tr = _pick_tile(rows, (512, 256, 128, 64, 32, 16, 8))

    def body(w_ref, g_ref, m_ref, v_ref, d_ref, nm_ref, nv_ref):
        d_ref[...], nm_ref[...], nv_ref[...] = _adamw_update(w_ref[...], g_ref[...], m_ref[...], v_ref[...])

    spec = pl.BlockSpec((None, tr, cols), lambda l, i: (l, i, 0))
    return tuple(pl.pallas_call(
        body, name=name, grid=(depth, rows // tr), in_specs=[spec] * 4, out_specs=[spec] * 3,
        out_shape=[jax.ShapeDtypeStruct(w.shape, F32)] * 3, compiler_params=_cparams("parallel", "parallel"),
    )(w, g, m, v))


def _adamw_call(w, g, m, v, name):
    shape = w.shape
    if len(shape) == 3:
        return _adamw_stacked_call(w, g, m, v, name)
    cols = shape[-1]
    rows = math.prod(shape[:-1]) if len(shape) > 1 else 1
    tr = _pick_tile(rows, (512, 256, 128, 64, 32, 16, 8))

    def body(w_ref, g_ref, m_ref, v_ref, d_ref, nm_ref, nv_ref):
        d_ref[...], nm_ref[...], nv_ref[...] = _adamw_update(w_ref[...], g_ref[...], m_ref[...], v_ref[...])

    spec = pl.BlockSpec((tr, cols), lambda i: (i, 0))
    outs = pl.pallas_call(
        body, name=name, grid=(rows // tr,), in_specs=[spec] * 4, out_specs=[spec] * 3,
        out_shape=[jax.ShapeDtypeStruct((rows, cols), F32)] * 3, compiler_params=_cparams("parallel"),
    )(*(t.reshape(rows, cols) for t in (w, g, m, v)))
    return tuple(o.reshape(shape) for o in outs)


WEIGHTS = ("norm_mix", "w_in", "ssm_lam_re", "ssm_lam_im", "ssm_b_re", "ssm_b_im", "ssm_c_re", "ssm_c_im", "ssm_d",
           "ssm_log_step", "ssm_w_glu", "mla_q_norm", "mla_kv_norm", "mla_w_uq", "mla_w_ukv", "mla_w_o", "hg_lb",
           "hg_g_norm", "hg_w_o", "w_out", "norm_cross", "norm_mem", "x_w_q", "x_w_kv", "x_w_o", "norm_ffn",
           "ffn_w_gate_up", "ffn_w_down", "norm_final")


def kernel(x, mem, positions, norm_mix, w_in, ssm_lam_re, ssm_lam_im, ssm_b_re, ssm_b_im, ssm_c_re, ssm_c_im, ssm_d, ssm_log_step, ssm_w_glu, mla_q_norm, mla_kv_norm, mla_w_uq, mla_w_ukv, mla_w_o, hg_lb, hg_g_norm, hg_w_o, w_out, norm_cross, norm_mem, x_w_q, x_w_kv, x_w_o, norm_ffn, ffn_w_gate_up, ffn_w_down, norm_final, loss_target, m_norm_mix, m_w_in, m_ssm_lam_re, m_ssm_lam_im, m_ssm_b_re, m_ssm_b_im, m_ssm_c_re, m_ssm_c_im, m_ssm_d, m_ssm_log_step, m_ssm_w_glu, m_mla_q_norm, m_mla_kv_norm, m_mla_w_uq, m_mla_w_ukv, m_mla_w_o, m_hg_lb, m_hg_g_norm, m_hg_w_o, m_w_out, m_norm_cross, m_norm_mem, m_x_w_q, m_x_w_kv, m_x_w_o, m_norm_ffn, m_ffn_w_gate_up, m_ffn_w_down, m_norm_final, v_norm_mix, v_w_in, v_ssm_lam_re, v_ssm_lam_im, v_ssm_b_re, v_ssm_b_im, v_ssm_c_re, v_ssm_c_im, v_ssm_d, v_ssm_log_step, v_ssm_w_glu, v_mla_q_norm, v_mla_kv_norm, v_mla_w_uq, v_mla_w_ukv, v_mla_w_o, v_hg_lb, v_hg_g_norm, v_hg_w_o, v_w_out, v_norm_cross, v_norm_mem, v_x_w_q, v_x_w_kv, v_x_w_o, v_norm_ffn, v_ffn_w_gate_up, v_ffn_w_down, v_norm_final):
    given = dict(locals())
    weights = {n: given[n] for n in WEIGHTS}
    small = tuple(weights[n] for n in REPLICATED)
    layer1 = MIXER_WEIGHTS + TAIL_WEIGHTS
    shards = lambda names, l: [weights[n][l:l + 1].astype(BF16) for n in names]
    join = lambda names, stacked: tuple(_join_shards(p, SHARDED[n]) for n, p in zip(names, stacked))
    split = lambda names, cts: [_split_shards(ct, SHARDED[n]) for n, ct in zip(names, cts)]
    landed = lambda names, lands, own: join(names, [_own_slot(land, o) for land, o in zip(lands, own)])
    xs, tabs = x[0], _rope_tables(positions[0])
    lower, vjp_lower = jax.vjp(_lower_bounds, hg_lb)
    me = 4 * lax.axis_index("x") + 2 * lax.axis_index("y") + lax.axis_index("c")

    got_m0 = _all_gather_call(shards(MIXER_WEIGHTS, 0), "weights_all_gather_m0")
    own_t0, own_l1 = shards(TAIL_WEIGHTS, 0), shards(layer1, 1)
    gather_t0 = _exchange_start_call(own_t0, got_m0[0], False, "weights_gather_start_t0")
    gather_l1 = _exchange_start_call(own_l1, gather_t0[4], False, "weights_gather_start_l1")
    xs = xs + gather_l1[4][0, 0]
    xa0, vjp_m0 = jax.vjp(_mixer_fn(0, tabs), xs, join(MIXER_WEIGHTS, got_m0), small, lower)
    full_t0 = landed(TAIL_WEIGHTS, _exchange_wait_call(gather_t0, xa0, False, "weights_gather_wait_t0"), own_t0)
    x1, vjp_t0 = jax.vjp(_tail_fn(0, mem[0]), xa0, full_t0, small)
    full_l1 = landed(layer1, _exchange_wait_call(gather_l1, x1, False, "weights_gather_wait_l1"), own_l1)
    xa1, vjp_m1 = jax.vjp(_mixer_fn(1, tabs), x1, full_l1[:len(MIXER_WEIGHTS)], small, lower)
    x2, vjp_t1 = jax.vjp(_tail_fn(1, mem[0]), xa1, full_l1[len(MIXER_WEIGHTS):], small)
    loss_local, vjp_loss = jax.vjp(functools.partial(_final_loss, loss_target[0]), x2, norm_final)

    def scatter_start(names, cts, dx, tag):
        gs = split(names, cts)
        started = _exchange_start_call(gs, dx, True, "grads_scatter_start_" + tag)
        return (started, gs), dx + started[4][0, 0]

    def scatter_finish(pending, after, tag):
        started, gs = pending
        lands = _exchange_wait_call(started, after, True, "grads_scatter_wait_" + tag)
        return [_slot_sum_call(_own_slot(land, lax.dynamic_index_in_dim(g, me, 0, keepdims=False)), "grads_slot_sum_" + tag)
                for land, g in zip(lands, gs)]

    dx2, d_norm_final = vjp_loss(jnp.ones((), F32))
    dxa1, dfull_t1, dsmall_t1 = vjp_t1(dx2)
    pend_t1, dxa1 = scatter_start(TAIL_WEIGHTS, dfull_t1, dxa1, "t1")
    dx1, dfull_m1, dsmall_m1, dlower1 = vjp_m1(dxa1)
    pend_m1, dx1 = scatter_start(MIXER_WEIGHTS, dfull_m1, dx1, "m1")
    dxa0, dfull_t0, dsmall_t0 = vjp_t0(dx1)
    pend_t0, dxa0 = scatter_start(TAIL_WEIGHTS, dfull_t0, dxa0, "t0")
    gx, dfull_m0, dsmall_m0, dlower0 = vjp_m0(dxa0)
    by_layer = {
        0: dict(zip(MIXER_WEIGHTS + TAIL_WEIGHTS,
                    _reduce_scatter(split(MIXER_WEIGHTS, dfull_m0), "grads_reduce_scatter_m0") + scatter_finish(pend_t0, gx, "t0"))),
        1: dict(zip(layer1, scatter_finish(pend_m1, gx, "m1") + scatter_finish(pend_t1, gx, "t1")))}
    grads = {n: jnp.concatenate([by_layer[0][n], by_layer[1][n]], axis=0) for n in SHARDED}

    d_small = dict(zip(REPLICATED, (a + b + c + d for a, b, c, d in zip(dsmall_m0, dsmall_t0, dsmall_m1, dsmall_t1))))
    d_small["norm_final"] = d_small["norm_final"] + d_norm_final
    d_small["hg_lb"] = d_small["hg_lb"] + vjp_lower(dlower0 + dlower1)[0]
    shapes = [d_small[n].shape for n in REPLICATED]
    (mine,) = _reduce_scatter([_pack_small([d_small[n] for n in REPLICATED])], "small_reduce_scatter")
    (total,) = _all_gather_call([mine], "small_all_gather")
    grads.update(zip(REPLICATED, _unpack_small(total, shapes)))

    loss = lax.psum(loss_local, ("x", "y", "c"))
    steps = {n: _adamw_call(weights[n], grads[n], given["m_" + n], given["v_" + n], "adamw_" + n) for n in WEIGHTS}
    return (loss, gx[None], *[grads[n] for n in WEIGHTS], *[steps[n][0] for n in WEIGHTS],
            *[steps[n][1] for n in WEIGHTS], *[steps[n][2] for n in WEIGHTS])
```

```python
import functools
import math

import jax
import jax.numpy as jnp
from jax import lax
from jax.experimental import pallas as pl
from jax.experimental.pallas import tpu as pltpu

F32 = jnp.float32
BF16 = jnp.bfloat16

VMEM_LIMIT_BYTES = 48 * 1024 * 1024
LANES = 128
SUBLANES = 8


def _cparams(*sem):
    return pltpu.CompilerParams(dimension_semantics=sem, vmem_limit_bytes=VMEM_LIMIT_BYTES)


def _pick_tile(n, cands):
    for c in cands:
        if n % c == 0:
            return c
    return n


MM_VMEM_BUDGET = 38 * 1024 * 1024
MM_STEP_US = 0.35
HBM_BYTES_PER_US = 3.0e6
VREG_RMW_PER_US = 1.5e3


def _divisor_tiles(dim, cands):
    out = [t for t in cands if dim % t == 0]
    return out or [dim]


def _mm_tiles(m, n, k, sa, sb, so):
    tms = _divisor_tiles(m, (1408, 1024, 512, 256, 128, 64, 32, 16, 8))[:2]
    tns = _divisor_tiles(n, (2048, 1536, 1408, 1024, 768, 512, 384, 256, 128))
    tks = [k // d for d in (1, 2, 4, 8, 13, 16, 26, 32, 52) if k % d == 0 and (k // d) % LANES == 0] or [k]
    best = None
    for tk in tks:
        nk = k // tk
        for tm in tms:
            for tn in tns:
                vmem = 2 * (tm * tk * sa + tk * tn * sb + tm * tn * so) + (tm * tn * 4 if nk > 1 else 0)
                if vmem > MM_VMEM_BUDGET:
                    continue
                steps = (m // tm) * (n // tn) * nk
                a_reads = m * k * sa * (n // tn if nk > 1 else 1)
                b_reads = k * n * sb * (m // tm if (nk > 1 or n // tn > 1) else 1)
                cost = (steps * MM_STEP_US + (a_reads + b_reads) / HBM_BYTES_PER_US
                        + (m * n * nk / 1024 / VREG_RMW_PER_US if nk > 1 else 0.0))
                if best is None or cost < best[0]:
                    best = (cost, tm, tn, tk)
    assert best is not None, (m, n, k)
    return best[1:]


def _mm_tiles_cached_t(m, n, k, sa, sb, so):
    for tm in _divisor_tiles(m, (1024, 512)):
        if m % tm:
            break
        for tn in _divisor_tiles(n, (1024, 512, 384, 256, 128)):
            if 2 * (k * tm * sa + k * tn * sb + tm * tn * so) + tm * k * 2 <= MM_VMEM_BUDGET:
                return tm, tn
    return None


def _mm_tn_cached_call(a, b, tiles, out_dtype, name):
    k, m = a.shape
    n = b.shape[1]
    tm, tn = tiles

    def body(a_ref, b_ref, o_ref, at_ref):
        @pl.when(pl.program_id(1) == 0)
        def _():
            at_ref[...] = a_ref[...].astype(BF16).T

        o_ref[...] = lax.dot_general(at_ref[...], b_ref[...].astype(BF16), _NN_DIMS,
                                     preferred_element_type=F32).astype(out_dtype)

    return pl.pallas_call(
        body,
        name=name,
        grid=(m // tm, n // tn),
        in_specs=[pl.BlockSpec((k, tm), lambda i, j: (0, i)), pl.BlockSpec((k, tn), lambda i, j: (0, j))],
        out_specs=pl.BlockSpec((tm, tn), lambda i, j: (i, j)),
        out_shape=jax.ShapeDtypeStruct((m, n), out_dtype),
        scratch_shapes=[pltpu.VMEM((tm, k), BF16)],
        compiler_params=_cparams("parallel", "arbitrary"),
    )(a, b)


_NN_DIMS = (((1,), (0,)), ((), ()))


def _mm_call(a, b, ta, tb, add=None, out_dtype=F32, name="mm"):
    m, k = (a.shape[1], a.shape[0]) if ta else a.shape
    k2, n = (b.shape[1], b.shape[0]) if tb else b.shape
    assert k == k2, (a.shape, b.shape, ta, tb)
    sizes = (a.dtype.itemsize, b.dtype.itemsize, jnp.dtype(out_dtype).itemsize + (add.dtype.itemsize if add is not None else 0))
    if ta:
        tiles = _mm_tiles_cached_t(m, n, k, *sizes)
        if tiles is not None:
            return _mm_tn_cached_call(a, b, tiles, out_dtype, name)
    tm, tn, tk = _mm_tiles(m, n, k, *sizes)
    nk = k // tk
    a_spec = pl.BlockSpec((tk, tm), lambda i, j, kk: (kk, i)) if ta else pl.BlockSpec((tm, tk), lambda i, j, kk: (i, kk))
    b_spec = pl.BlockSpec((tn, tk), lambda i, j, kk: (j, kk)) if tb else pl.BlockSpec((tk, tn), lambda i, j, kk: (kk, j))
    o_spec = pl.BlockSpec((tm, tn), lambda i, j, kk: (i, j))
    dn = (((0 if ta else 1,), (1 if tb else 0,)), ((), ()))
    has_add = add is not None

    def body(*refs):
        a_ref, b_ref = refs[0], refs[1]
        c_ref = refs[2] if has_add else None
        o_ref = refs[3] if has_add else refs[2]
        p = lax.dot_general(a_ref[...].astype(BF16), b_ref[...].astype(BF16), dn, preferred_element_type=F32)

        def finish(r):
            if has_add:
                r = r + c_ref[...].astype(F32)
            o_ref[...] = r.astype(out_dtype)

        if nk == 1:
            finish(p)
        else:
            acc_ref = refs[-1]
            kk = pl.program_id(2)

            @pl.when(kk == 0)
            def _():
                acc_ref[...] = p

            @pl.when(kk > 0)
            def _():
                acc_ref[...] += p

            @pl.when(kk == nk - 1)
            def _():
                finish(acc_ref[...])

    in_specs = [a_spec, b_spec] + ([o_spec] if has_add else [])
    args = (a, b) + ((add,) if has_add else ())
    return pl.pallas_call(
        body,
        name=name,
        grid=(m // tm, n // tn, nk),
        in_specs=in_specs,
        out_specs=o_spec,
        out_shape=jax.ShapeDtypeStruct((m, n), out_dtype),
        scratch_shapes=[] if nk == 1 else [pltpu.VMEM((tm, tn), F32)],
        compiler_params=_cparams("parallel", "parallel", "arbitrary"),
    )(*args)


@functools.partial(jax.custom_vjp, nondiff_argnums=(2,))
def matmul(a, b, out_dtype=F32):
    return _mm_call(a, b, False, False, out_dtype=out_dtype, name="mm_fwd")


def _matmul_fwd(a, b, out_dtype):
    return matmul(a, b, out_dtype), (a, b)


def _matmul_bwd(out_dtype, res, g):
    a, b = res
    da = _mm_call(g, b, False, True, out_dtype=a.dtype, name="mm_da")
    db = _mm_call(a, g, True, False, out_dtype=b.dtype, name="mm_db")
    return da, db


matmul.defvjp(_matmul_fwd, _matmul_bwd)


@jax.custom_vjp
def matmul_add(a, b, c):
    return _mm_call(a, b, False, False, add=c, name="mm_add_fwd")


def _matmul_add_fwd(a, b, c):
    return _mm_call(a, b, False, False, add=c, name="mm_add_fwd"), (a, b)


def _matmul_add_bwd(res, g):
    a, b = res
    da = _mm_call(g, b, False, True, out_dtype=a.dtype, name="mm_da")
    db = _mm_call(a, g, True, False, out_dtype=b.dtype, name="mm_db")
    return da, db, g


matmul_add.defvjp(_matmul_add_fwd, _matmul_add_bwd)


def rowwise(f, n_rows, n_aux, tile, name, passthrough=False):
    def specs(arrs, tiled):
        out = []
        for x in arrs:
            if tiled:
                out.append(pl.BlockSpec((tile, x.shape[1]), lambda i: (i, 0)))
            else:
                out.append(pl.BlockSpec(x.shape, lambda i: (0, 0)))
        return out

    def tile_structs(args):
        rows_aux, params = args[: n_rows + n_aux], args[n_rows + n_aux:]
        return [jax.ShapeDtypeStruct((tile, x.shape[1]), x.dtype) for x in rows_aux] + [
            jax.ShapeDtypeStruct(p.shape, p.dtype) for p in params]

    def fwd_call(*args):
        s = args[0].shape[0]
        outs = jax.eval_shape(f, *tile_structs(args))
        n_in = len(args)

        def body(*refs):
            vals = [r[...] for r in refs[:n_in]]
            res = f(*vals)
            for o_ref, r in zip(refs[n_in:], res):
                o_ref[...] = r.astype(o_ref.dtype)

        return pl.pallas_call(
            body,
            name=name + "_fwd",
            grid=(s // tile,),
            in_specs=specs(args[: n_rows + n_aux], True) + specs(args[n_rows + n_aux:], False),
            out_specs=[pl.BlockSpec((tile, o.shape[1]), lambda i: (i, 0)) for o in outs],
            out_shape=[jax.ShapeDtypeStruct((s, o.shape[1]), o.dtype) for o in outs],
            compiler_params=_cparams("parallel"),
        )(*args)

    def bwd_call(args, gs):
        s = args[0].shape[0]
        rows, aux, params = args[:n_rows], args[n_rows:n_rows + n_aux], args[n_rows + n_aux:]
        n_in, n_g, n_p = len(args), len(gs), len(params)
        n_gf = n_g - 1 if passthrough else n_g

        def body(*refs):
            vals = [r[...] for r in refs[:n_in]]
            gvals = tuple(r[...] for r in refs[n_in:n_in + n_gf])
            out_refs = refs[n_in + n_g:]
            auxv = vals[n_rows:n_rows + n_aux]

            def g_(*rp):
                return tuple(f(*rp[:n_rows], *auxv, *rp[n_rows:]))

            _, vjp = jax.vjp(g_, *vals[:n_rows], *vals[n_rows + n_aux:])
            cts = list(vjp(gvals))
            if passthrough:
                cts[0] = cts[0] + refs[n_in + n_gf][...]
            for o_ref, ct in zip(out_refs[:n_rows], cts[:n_rows]):
                o_ref[...] = ct.astype(o_ref.dtype)
            if n_p:
                @pl.when(pl.program_id(0) == 0)
                def _():
                    for o_ref in out_refs[n_rows:]:
                        o_ref[...] = jnp.zeros_like(o_ref)

                for o_ref, ct in zip(out_refs[n_rows:], cts[n_rows:]):
                    o_ref[...] += ct.astype(o_ref.dtype)

        return pl.pallas_call(
            body,
            name=name + "_bwd",
            grid=(s // tile,),
            in_specs=specs(rows + aux, True) + specs(params, False) + specs(gs, True),
            out_specs=specs(rows, True) + specs(params, False),
            out_shape=[jax.ShapeDtypeStruct(x.shape, x.dtype) for x in rows + params],
            compiler_params=_cparams("arbitrary" if n_p else "parallel"),
        )(*args, *gs)

    @jax.custom_vjp
    def op(*args):
        return tuple(fwd_call(*args)) + ((args[0],) if passthrough else ())

    def op_fwd(*args):
        return op(*args), args

    def op_bwd(args, gs):
        cts = bwd_call(tuple(args), tuple(gs))
        rows_ct, par_ct = cts[:n_rows], cts[n_rows:]
        aux_ct = [jnp.zeros_like(a) for a in args[n_rows:n_rows + n_aux]]
        return tuple(rows_ct) + tuple(aux_ct) + tuple(par_ct)

    op.defvjp(op_fwd, op_bwd)
    return op


SCAN_SEGMENTS = SUBLANES


def _scan_step(ar, ai, xr, xi, br, bi):
    return ar * xr - ai * xi + br, ar * xi + ai * xr + bi


_NT_DIMS = (((1,), (1,)), ((), ()))
_TN_DIMS = (((0,), (0,)), ((), ()))


S5_GROUPS_PER_STEP = 64
S5_STATE_LANES = 512


def _s5_interleave(src_ref, scr_ref, rows):
    for k in range(SCAN_SEGMENTS):
        scr_ref[pl.ds(k, rows, stride=SCAN_SEGMENTS), :] = src_ref[k].astype(F32)


def _s5_deinterleave(val, scr_ref, dst_ref, rows):
    scr_ref[...] = val
    for k in range(SCAN_SEGMENTS):
        dst_ref[k] = scr_ref[pl.ds(k, rows, stride=SCAN_SEGMENTS), :]


def _s5_segment_starts(a_ref, fr_ref, fi_ref, sr, si, seg_len, order):
    tn = sr.shape[1]
    pr, pi = a_ref[0:1, :], a_ref[1:2, :]
    for _ in range(seg_len.bit_length() - 1):
        pr, pi = pr * pr - pi * pi, 2.0 * pr * pi
    cr = jnp.zeros((1, tn), F32)
    ci = jnp.zeros((1, tn), F32)
    for idx, k in enumerate(order):
        if idx > 0:
            kp = order[idx - 1]
            cr, ci = (fr_ref[kp:kp + 1, :] + pr * cr - pi * ci, fi_ref[kp:kp + 1, :] + pr * ci + pi * cr)
        sr[k:k + 1, :] = cr
        si[k:k + 1, :] = ci


def _s5_pass_call(src, w_r, w_i, a, transpose_w, reverse, finals, extra, name):
    s = src.shape[0]
    nb = w_r.shape[0]
    n = nb * S5_STATE_LANES
    seg_len = s // SCAN_SEGMENTS
    ti = min(S5_GROUPS_PER_STEP, seg_len)
    nt = seg_len // ti
    tr = SCAN_SEGMENTS * ti
    tn = S5_STATE_LANES
    assert seg_len & (seg_len - 1) == 0
    order = list(range(SCAN_SEGMENTS))[::-1] if reverse else list(range(SCAN_SEGMENTS))
    dn_in = _NT_DIMS if transpose_w else _NN_DIMS
    first = finals is None
    backward = (not first) and reverse
    forward = (not first) and not reverse
    tmap3 = (lambda c, j: (0, nt - 1 - j, c)) if reverse else (lambda c, j: (0, j, c))
    tmap2 = (lambda c, j: (nt - 1 - j, c)) if reverse else (lambda c, j: (j, c))
    bf = lambda v: v.astype(BF16)

    def body(*refs):
        it = iter(refs)
        src_ref, wr_ref, wi_ref, a_ref = next(it), next(it), next(it), next(it)
        if not first:
            fr_ref, fi_ref = next(it), next(it)
        if forward:
            cdr_ref, cdi_ref = next(it), next(it)
            xr_out, xi_out, y_ref = next(it), next(it), next(it)
        if backward:
            xr_ref, xi_ref, u_ref, bdr_ref, bdi_ref = next(it), next(it), next(it), next(it), next(it)
            du_ref, dar_ref, dai_ref, dbr_ref, dbi_ref, dcr_ref, dci_ref = (next(it) for _ in range(7))
        if first:
            fr_out, fi_out = next(it), next(it)
        sr, si, in_scr, dr_scr, di_scr = next(it), next(it), next(it), next(it), next(it)
        if backward:
            accr, acci, u_scr = next(it), next(it), next(it)
        j = pl.program_id(1)

        @pl.when(j == 0)
        def _():
            if first:
                sr[...] = jnp.zeros_like(sr)
                si[...] = jnp.zeros_like(si)
            else:
                _s5_segment_starts(a_ref, fr_ref, fi_ref, sr, si, seg_len, order)
            if backward:
                for r in (accr, acci, dbr_ref, dbi_ref, dcr_ref, dci_ref):
                    r[...] = jnp.zeros_like(r)

        _s5_interleave(src_ref, in_scr, ti)
        src_b = bf(in_scr[...])
        dr_scr[...] = lax.dot_general(src_b, bf(wr_ref[0]), dn_in, preferred_element_type=F32)
        di_scr[...] = lax.dot_general(src_b, bf(wi_ref[0]), dn_in, preferred_element_type=F32)
        ar = jnp.broadcast_to(a_ref[0:1, :], (SUBLANES, tn))
        ai = jnp.broadcast_to(a_ref[1:2, :], (SUBLANES, tn))

        def step(ii, carry):
            i = (ti - 1 - ii) if reverse else ii
            rows = pl.ds(pl.multiple_of(i * SUBLANES, SUBLANES), SUBLANES)
            xr, xi = carry[0], carry[1]
            if backward:
                zr, zi = xr_ref[rows, :], xi_ref[rows, :]
                acc = (carry[2] + xr * zr + xi * zi, carry[3] + xi * zr - xr * zi)
            nr, ni = _scan_step(ar, ai, xr, xi, dr_scr[rows, :], di_scr[rows, :])
            if forward:
                xr_out[rows, :] = nr
                xi_out[rows, :] = ni
            if backward:
                dr_scr[rows, :] = nr
                di_scr[rows, :] = ni
            return (nr, ni) + (acc if backward else ())

        init = (sr[...], si[...]) + ((accr[...], acci[...]) if backward else ())
        out = lax.fori_loop(0, ti, step, init, unroll=4)
        sr[...] = out[0]
        si[...] = out[1]
        if first:
            @pl.when(j == nt - 1)
            def _():
                fr_out[...] = out[0]
                fi_out[...] = out[1]
        if forward:
            y = (lax.dot_general(bf(xr_out[...]), bf(cdr_ref[0]), _NN_DIMS, preferred_element_type=F32)
                 + lax.dot_general(bf(xi_out[...]), bf(cdi_ref[0]), _NN_DIMS, preferred_element_type=F32))
            _s5_deinterleave(y, in_scr, y_ref, ti)
        if backward:
            accr[...] = out[2]
            acci[...] = out[3]
            g_r, g_i = bf(dr_scr[...]), bf(di_scr[...])
            dcr_ref[0] += lax.dot_general(bf(xr_ref[...]), src_b, _TN_DIMS, preferred_element_type=F32)
            dci_ref[0] += lax.dot_general(bf(xi_ref[...]), src_b, _TN_DIMS, preferred_element_type=F32)
            _s5_interleave(u_ref, u_scr, ti)
            u_b = bf(u_scr[...])
            dbr_ref[0] += lax.dot_general(u_b, g_r, _TN_DIMS, preferred_element_type=F32)
            dbi_ref[0] += lax.dot_general(u_b, g_i, _TN_DIMS, preferred_element_type=F32)
            du = (lax.dot_general(g_r, bf(bdr_ref[0]), _NT_DIMS, preferred_element_type=F32)
                  + lax.dot_general(g_i, bf(bdi_ref[0]), _NT_DIMS, preferred_element_type=F32))
            _s5_deinterleave(du, u_scr, du_ref, ti)

            @pl.when(j == nt - 1)
            def _():
                dar_ref[...] = jnp.sum(out[2], axis=0, keepdims=True)
                dai_ref[...] = jnp.sum(out[3], axis=0, keepdims=True)

    view3 = lambda t: t.reshape(SCAN_SEGMENTS, seg_len, t.shape[1])
    spec3 = pl.BlockSpec((SCAN_SEGMENTS, ti, LANES), tmap3)
    wspec = lambda w: pl.BlockSpec((1,) + w.shape[1:], lambda c, j: (c, 0, 0))
    aspec = pl.BlockSpec((2, tn), lambda c, j: (0, c))
    fspec = pl.BlockSpec((SUBLANES, tn), lambda c, j: (0, c))
    xspec = pl.BlockSpec((tr, tn), tmap2)
    dspec = pl.BlockSpec((1, tn), lambda c, j: (0, c))
    f32 = lambda *shape: jax.ShapeDtypeStruct(shape, F32)
    args, in_specs = [view3(src), w_r, w_i, a], [spec3, wspec(w_r), wspec(w_i), aspec]
    if not first:
        args += list(finals)
        in_specs += [fspec, fspec]
    if forward:
        args += list(extra)
        in_specs += [wspec(extra[0]), wspec(extra[1])]
        out_specs, out_shape = [xspec, xspec, spec3], [f32(s, n), f32(s, n), f32(SCAN_SEGMENTS, seg_len, nb * LANES)]
    elif backward:
        x_r, x_i, u, bd_r, bd_i = extra
        args += [x_r, x_i, view3(u), bd_r, bd_i]
        in_specs += [xspec, xspec, spec3, wspec(bd_r), wspec(bd_i)]
        out_specs = [spec3, dspec, dspec, wspec(bd_r), wspec(bd_i), wspec(w_r), wspec(w_i)]
        out_shape = [f32(SCAN_SEGMENTS, seg_len, nb * LANES), f32(1, n), f32(1, n), f32(*bd_r.shape), f32(*bd_i.shape),
                     f32(*w_r.shape), f32(*w_i.shape)]
    else:
        out_specs, out_shape = [fspec, fspec], [f32(SUBLANES, n), f32(SUBLANES, n)]
    scratch = ([pltpu.VMEM((SUBLANES, tn), F32)] * 2 + [pltpu.VMEM((tr, LANES), F32)] + [pltpu.VMEM((tr, tn), F32)] * 2
               + ([pltpu.VMEM((SUBLANES, tn), F32)] * 2 + [pltpu.VMEM((tr, LANES), F32)] if backward else []))
    return pl.pallas_call(
        body, name=name, grid=(nb, nt), in_specs=in_specs, out_specs=out_specs, out_shape=out_shape,
        scratch_shapes=scratch, compiler_params=_cparams("parallel", "arbitrary"),
    )(*args)


@jax.custom_vjp
def s5_core(u, a, bd_r, bd_i, cd_r, cd_i):
    return _s5_core_fwd(u, a, bd_r, bd_i, cd_r, cd_i)[0]


def _s5_core_fwd(u, a, bd_r, bd_i, cd_r, cd_i):
    fin = _s5_pass_call(u, bd_r, bd_i, a, False, False, None, None, "s5_fwd_finals")
    x_r, x_i, y = _s5_pass_call(u, bd_r, bd_i, a, False, False, fin, (cd_r, cd_i), "s5_fwd_scan")
    return y.reshape(u.shape), (u, a, bd_r, bd_i, cd_r, cd_i, x_r, x_i)


def _s5_core_bwd(res, dy):
    u, a, bd_r, bd_i, cd_r, cd_i, x_r, x_i = res
    a_conj = a * jnp.array([[1.0], [-1.0]], F32)
    fin = _s5_pass_call(dy, cd_r, cd_i, a_conj, True, True, None, None, "s5_bwd_finals")
    du, da_r, da_i, dbd_r, dbd_i, dcd_r, dcd_i = _s5_pass_call(
        dy, cd_r, cd_i, a_conj, True, True, fin, (x_r, x_i, u, bd_r, bd_i), "s5_bwd_scan")
    return du.reshape(u.shape), jnp.concatenate([da_r, da_i], axis=0), dbd_r, dbd_i, dcd_r, dcd_i


s5_core.defvjp(_s5_core_fwd, _s5_core_bwd)


_NN = (((1,), (0,)), ((), ()))
_NT = (((1,), (1,)), ((), ()))
_TN = (((0,), (0,)), ((), ()))


def _dot(a, b, dn):
    return lax.dot_general(a.astype(BF16), b.astype(BF16), dn, preferred_element_type=F32)


@jax.custom_vjp
def bdot_nn(a, b):
    return _dot(a, b, _NN)


bdot_nn.defvjp(lambda a, b: (_dot(a, b, _NN), (a, b)),
               lambda r, g: (_dot(g, r[1], _NT).astype(r[0].dtype), _dot(r[0], g, _TN).astype(r[1].dtype)))


@jax.custom_vjp
def bdot_nt(a, b):
    return _dot(a, b, _NT)


bdot_nt.defvjp(lambda a, b: (_dot(a, b, _NT), (a, b)),
               lambda r, g: (_dot(g, r[1], _NN).astype(r[0].dtype), _dot(g, r[0], _TN).astype(r[1].dtype)))


@jax.custom_vjp
def bdot_tn(a, b):
    return _dot(a, b, _TN)


bdot_tn.defvjp(lambda a, b: (_dot(a, b, _TN), (a, b)),
               lambda r, g: (_dot(r[1], g, _NT).astype(r[0].dtype), _dot(r[0], g, _NN).astype(r[1].dtype)))


def _split3(x):
    h = x.astype(BF16)
    r = x - h.astype(F32)
    m = r.astype(BF16)
    l = (r - m.astype(F32)).astype(BF16)
    return h, m, l


def _exact_dot(t, x, dn):
    h, m, l = _split3(x)
    d = lambda p: lax.dot_general(t, p, dn, preferred_element_type=F32)
    return d(h) + d(m) + d(l)


@jax.custom_vjp
def select_dot(t, x):
    return _exact_dot(t, x, _NN)


select_dot.defvjp(lambda t, x: (_exact_dot(t, x, _NN), t),
                  lambda t, g: (jnp.zeros_like(t), _exact_dot(t, g, _TN)))


def _split_rows_impl(x, h):
    return tuple(x[i * h:(i + 1) * h] for i in range(x.shape[0] // h))


@functools.partial(jax.custom_vjp, nondiff_argnums=(1,))
def split_rows(x, h):
    return _split_rows_impl(x, h)


split_rows.defvjp(lambda x, h: (_split_rows_impl(x, h), None),
                  lambda h, r, g: (jnp.concatenate(g, axis=0),))


@jax.custom_vjp
def join_rows(parts):
    return jnp.concatenate(parts, axis=0)


def _join_rows_bwd(hs, g):
    out, off = [], 0
    for h in hs:
        out.append(g[off:off + h])
        off += h
    return (tuple(out),)


join_rows.defvjp(lambda parts: (jnp.concatenate(parts, axis=0), tuple(p.shape[0] for p in parts)), _join_rows_bwd)


def _split_lanes_impl(x, w):
    return tuple(x[:, i * w:(i + 1) * w] for i in range(x.shape[1] // w))


@functools.partial(jax.custom_vjp, nondiff_argnums=(1,))
def split_lanes(x, w):
    return _split_lanes_impl(x, w)


split_lanes.defvjp(lambda x, w: (_split_lanes_impl(x, w), None),
                   lambda w, r, g: (jnp.concatenate(g, axis=1),))


def _join_impl(parts):
    return jnp.concatenate(parts, axis=1)


@jax.custom_vjp
def join_lanes(parts):
    return _join_impl(parts)


def _join_bwd(ws, g):
    out, off = [], 0
    for w in ws:
        out.append(g[:, off:off + w])
        off += w
    return (tuple(out),)


join_lanes.defvjp(lambda parts: (_join_impl(parts), tuple(p.shape[1] for p in parts)), _join_bwd)


def _rope_impl(x, c, sa, sb, shift):
    w = x.shape[1]
    return x * c + pltpu.roll(x, w - shift, 1) * sa + pltpu.roll(x, shift, 1) * sb


@functools.partial(jax.custom_vjp, nondiff_argnums=(4,))
def rope_lanes(x, c, sa, sb, shift):
    return _rope_impl(x, c, sa, sb, shift)


def _rope_bwd(shift, r, g):
    c, sa, sb = r
    w = g.shape[1]
    dx = g * c + pltpu.roll(g * sa, shift, 1) + pltpu.roll(g * sb, w - shift, 1)
    return dx, jnp.zeros_like(c), jnp.zeros_like(sa), jnp.zeros_like(sb)


rope_lanes.defvjp(lambda x, c, sa, sb, shift: (_rope_impl(x, c, sa, sb, shift), (c, sa, sb)), _rope_bwd)


RMS_EPS = 1e-6


def _rms(x, g):
    return x * lax.rsqrt(jnp.mean(x * x, axis=-1, keepdims=True) + RMS_EPS) * g


ATTN_BLOCK = 512
MASK_VALUE = -1e30
LOG2E = math.log2(math.e)
LN2 = math.log(2.0)
V_ONES_LANE = 64


def _causal_mask(t):
    r = lax.broadcasted_iota(jnp.int32, (t, t), 0)
    c = lax.broadcasted_iota(jnp.int32, (t, t), 1)
    return c <= r


def _attn_fwd_call(q, k, v):
    s, width = q.shape
    n_heads = width // LANES
    tq = min(ATTN_BLOCK, s)
    nq = s // tq

    def body(q_ref, k_ref, v_ref, o_ref, lse_ref):
        i = pl.program_id(1)
        qb = q_ref[...].astype(BF16)
        ones_lane = lax.broadcasted_iota(jnp.int32, (tq, LANES), 1) == V_ONES_LANE

        def block(kb, carry, masked):
            m, acc = carry
            rows = pl.ds(pl.multiple_of(kb * tq, tq), tq)
            sc = lax.dot_general(qb, k_ref[rows, :].astype(BF16), _NT, preferred_element_type=F32)
            if masked:
                sc = jnp.where(_causal_mask(tq), sc, MASK_VALUE)
            m_new = jnp.maximum(m, jnp.max(sc, axis=-1, keepdims=True))
            p = jnp.exp2(sc - m_new).astype(BF16)
            vb = jnp.where(ones_lane, 1.0, v_ref[rows, :]).astype(BF16)
            acc = jnp.exp2(m - m_new) * acc + lax.dot_general(p, vb, _NN, preferred_element_type=F32)
            return m_new, acc

        init = (jnp.full((tq, 1), MASK_VALUE, F32), jnp.zeros((tq, LANES), F32))
        carry = lax.fori_loop(0, i, lambda kb, c: block(kb, c, False), init)
        m, acc = block(i, carry, True)
        l = jnp.sum(jnp.where(ones_lane, acc, 0.0), axis=-1, keepdims=True)
        o_ref[...] = jnp.where(ones_lane, 0.0, acc / l).astype(o_ref.dtype)
        lse_ref[...] = jnp.broadcast_to(m + jnp.log2(l), (tq, LANES))

    qspec = pl.BlockSpec((tq, LANES), lambda h, i: (i, h))
    kspec = pl.BlockSpec((s, LANES), lambda h, i: (0, h))
    return pl.pallas_call(
        body,
        name="mla_attn_fwd",
        grid=(n_heads, nq),
        in_specs=[qspec, kspec, kspec],
        out_specs=[qspec, qspec],
        out_shape=[jax.ShapeDtypeStruct((s, width), BF16), jax.ShapeDtypeStruct((s, width), F32)],
        compiler_params=_cparams("parallel", "parallel"),
    )(q, k, v)


def _attn_bwd_call(q, k, v, o, lse, do):
    s, width = q.shape
    n_heads = width // LANES
    tq = min(ATTN_BLOCK, s)
    nq = s // tq

    def body(q_ref, k_ref, v_ref, o_ref, lse_ref, do_ref, dq_ref, dk_ref, dv_ref, dq_acc):
        j = pl.program_id(1)

        @pl.when(j == 0)
        def _():
            dq_acc[...] = jnp.zeros_like(dq_acc)

        kb = k_ref[...].astype(BF16)
        vb = v_ref[...].astype(BF16)

        def block(i, carry, masked):
            dk, dv = carry
            rows = pl.ds(pl.multiple_of(i * tq, tq), tq)
            qi = q_ref[rows, :].astype(BF16)
            doi = do_ref[rows, :].astype(F32)
            delta = jnp.sum(doi * o_ref[rows, :].astype(F32), axis=-1, keepdims=True)
            sc = lax.dot_general(qi, kb, _NT, preferred_element_type=F32)
            if masked:
                sc = jnp.where(_causal_mask(tq), sc, MASK_VALUE)
            p = jnp.exp2(sc - lse_ref[rows, 0:1])
            dob = doi.astype(BF16)
            dv = dv + lax.dot_general(p.astype(BF16), dob, _TN, preferred_element_type=F32)
            dp = lax.dot_general(dob, vb, _NT, preferred_element_type=F32)
            ds = (p * (dp - delta)).astype(BF16)
            dq_acc[rows, :] += lax.dot_general(ds, kb, _NN, preferred_element_type=F32)
            dk = dk + lax.dot_general(ds, qi, _TN, preferred_element_type=F32)
            return dk, dv

        zero = jnp.zeros((tq, LANES), F32)
        carry = block(j, (zero, zero), True)
        dk, dv = lax.fori_loop(j + 1, nq, lambda i, c: block(i, c, False), carry)
        dk_ref[...] = (dk * LN2).astype(dk_ref.dtype)
        dv_ref[...] = dv.astype(dv_ref.dtype)

        @pl.when(j == nq - 1)
        def _():
            dq_ref[...] = (dq_acc[...] * LN2).astype(dq_ref.dtype)

    full = pl.BlockSpec((s, LANES), lambda h, j: (0, h))
    blk = pl.BlockSpec((tq, LANES), lambda h, j: (j, h))
    return pl.pallas_call(
        body,
        name="mla_attn_bwd",
        grid=(n_heads, nq),
        in_specs=[full, blk, blk, full, full, full],
        out_specs=[full, blk, blk],
        out_shape=[jax.ShapeDtypeStruct((s, width), t.dtype) for t in (q, k, v)],
        scratch_shapes=[pltpu.VMEM((s, LANES), F32)],
        compiler_params=_cparams("parallel", "arbitrary"),
    )(q, k, v, o, lse, do)


@jax.custom_vjp
def causal_attention(q, k, v):
    return _attn_fwd_call(q, k, v)[0]


def _causal_attention_fwd(q, k, v):
    o, lse = _attn_fwd_call(q, k, v)
    return o, (q, k, v, o, lse)


def _causal_attention_bwd(res, do):
    return tuple(_attn_bwd_call(*res, do))


causal_attention.defvjp(_causal_attention_fwd, _causal_attention_bwd)


HG_HEADS = 4
HG_CHUNK = 32
HG_REF_ROW = HG_CHUNK // 2 - 1
HG_TILE_ROWS = 256
HG_EXP_CLAMP = 80.0


def _hg_tile_masks(t):
    shift = HG_CHUNK.bit_length() - 1
    r = lax.broadcasted_iota(jnp.int32, (t, t), 0)
    c = lax.broadcasted_iota(jnp.int32, (t, t), 1)
    start = lax.shift_left(lax.shift_right_logical(r, shift), shift)
    causal = (c >= start) & (c <= r)
    return causal, c == start + HG_REF_ROW, c == start + (HG_CHUNK - 1)


def _hg_tile(q, fl, v, lb, st):
    t = q.shape[0]
    causal, ref_sel, last_sel = _hg_tile_masks(t)
    f = lb + (1.0 - lb) * jax.nn.sigmoid(fl)
    kk = 1.0 - f
    qs = q * jax.nn.sigmoid(q)
    b = select_dot(causal.astype(BF16), jnp.log(f))
    b_ref = select_dot(ref_sel.astype(BF16), b)
    b_last = select_dot(last_sel.astype(BF16), b)
    q_in = qs * jnp.exp(jnp.minimum(b - b_ref, HG_EXP_CLAMP))
    k_in = kk * jnp.exp(jnp.minimum(b_ref - b, HG_EXP_CLAMP))
    o = bdot_nn(jnp.where(causal, bdot_nt(q_in, k_in), 0.0), v)
    q_hat = split_rows(qs * jnp.exp(b), HG_CHUNK)
    k_hat = split_rows(kk * jnp.exp(b_last - b), HG_CHUNK)
    decay = split_rows(jnp.exp(b_last), HG_CHUNK)
    vs = split_rows(v, HG_CHUNK)
    first_row = lax.broadcasted_iota(jnp.int32, (HG_CHUNK, LANES), 0) == 0
    inter = []
    for c in range(t // HG_CHUNK):
        inter.append(bdot_nt(q_hat[c], st))
        st = st * jnp.sum(jnp.where(first_row, decay[c], 0.0), axis=0, keepdims=True) + bdot_tn(vs[c], k_hat[c])
    return o + join_rows(tuple(inter)), st


def _hg_head(q, fl, v, gate, lb, gn, st):
    o, st = _hg_tile(q, fl, v, lb, st)
    return _rms(o, gn) * (gate * jax.nn.sigmoid(gate)), st


HG_PARTS = 4


def _hg_part_slices(h, width):
    return [slice(p * width + h * LANES, p * width + (h + 1) * LANES) for p in range(HG_PARTS)]


def _hg_fwd_call(x, lb, gn):
    s = x.shape[0]
    width = x.shape[1] // HG_PARTS
    tr = min(HG_TILE_ROWS, s)
    nt = s // tr

    def body(x_ref, lb_ref, gn_ref, o_ref, sts_ref, st_ref):
        @pl.when(pl.program_id(0) == 0)
        def _():
            st_ref[...] = jnp.zeros_like(st_ref)

        for h in range(HG_HEADS):
            ln = slice(h * LANES, (h + 1) * LANES)
            st = st_ref[h]
            sts_ref[0, h] = st
            o, st_new = _hg_head(*(x_ref[:, sl] for sl in _hg_part_slices(h, width)), lb_ref[:, ln], gn_ref[...], st)
            o_ref[:, ln] = o.astype(o_ref.dtype)
            st_ref[h] = st_new

    const = lambda shape: pl.BlockSpec(shape, lambda j: (0, 0))
    return pl.pallas_call(
        body,
        name="hgrn2_fwd",
        grid=(nt,),
        in_specs=[pl.BlockSpec((tr, HG_PARTS * width), lambda j: (j, 0)), const((1, width)), const((1, LANES))],
        out_specs=[pl.BlockSpec((tr, width), lambda j: (j, 0)),
                   pl.BlockSpec((1, HG_HEADS, LANES, LANES), lambda j: (j, 0, 0, 0))],
        out_shape=[jax.ShapeDtypeStruct((s, width), BF16),
                   jax.ShapeDtypeStruct((nt, HG_HEADS, LANES, LANES), F32)],
        scratch_shapes=[pltpu.VMEM((HG_HEADS, LANES, LANES), F32)],
        compiler_params=_cparams("arbitrary"),
    )(x, lb, gn)


def _hg_bwd_call(x, lb, gn, sts, do):
    s = x.shape[0]
    width = x.shape[1] // HG_PARTS
    tr = min(HG_TILE_ROWS, s)
    nt = s // tr

    def body(x_ref, lb_ref, gn_ref, sts_ref, do_ref, dx_ref, dlb_ref, dgn_ref, dst_ref):
        @pl.when(pl.program_id(0) == 0)
        def _():
            dst_ref[...] = jnp.zeros_like(dst_ref)
            dlb_ref[...] = jnp.zeros_like(dlb_ref)
            dgn_ref[...] = jnp.zeros_like(dgn_ref)

        for h in range(HG_HEADS):
            ln = slice(h * LANES, (h + 1) * LANES)
            parts = _hg_part_slices(h, width)
            _, vjp = jax.vjp(_hg_head, *(x_ref[:, sl] for sl in parts), lb_ref[:, ln], gn_ref[...], sts_ref[0, h])
            cts = vjp((do_ref[:, ln].astype(F32), dst_ref[h]))
            for sl, ct in zip(parts, cts[:HG_PARTS]):
                dx_ref[:, sl] = ct.astype(dx_ref.dtype)
            dlb_ref[:, ln] += cts[HG_PARTS]
            dgn_ref[...] += cts[HG_PARTS + 1]
            dst_ref[h] = cts[HG_PARTS + 2]

    rev = lambda w: pl.BlockSpec((tr, w), lambda j: (nt - 1 - j, 0))
    const = lambda shape: pl.BlockSpec(shape, lambda j: (0, 0))
    return pl.pallas_call(
        body,
        name="hgrn2_bwd",
        grid=(nt,),
        in_specs=[rev(HG_PARTS * width), const((1, width)), const((1, LANES)),
                  pl.BlockSpec((1, HG_HEADS, LANES, LANES), lambda j: (nt - 1 - j, 0, 0, 0)), rev(width)],
        out_specs=[rev(HG_PARTS * width), const((1, width)), const((1, LANES))],
        out_shape=[jax.ShapeDtypeStruct(x.shape, BF16), jax.ShapeDtypeStruct((1, width), F32),
                   jax.ShapeDtypeStruct((1, LANES), F32)],
        scratch_shapes=[pltpu.VMEM((HG_HEADS, LANES, LANES), F32)],
        compiler_params=_cparams("arbitrary"),
    )(x, lb, gn, sts, do)


@jax.custom_vjp
def hgrn2_mixer(h, w, lb, gn):
    return _hg_fwd_call(_mm_call(h, w, False, False, name="hgrn2_proj"), lb, gn)[0]


def _hgrn2_mixer_fwd(h, w, lb, gn):
    x = _mm_call(h, w, False, False, name="hgrn2_proj")
    o, sts = _hg_fwd_call(x, lb, gn)
    return o, (h, w, x, lb, gn, sts)


def _hgrn2_mixer_bwd(res, do):
    h, w, x, lb, gn, sts = res
    dx, dlb, dgn = _hg_bwd_call(x, lb, gn, sts, do)
    dh = _mm_call(dx, w, False, True, out_dtype=h.dtype, name="hgrn2_proj_da")
    dw = _mm_call(h, dx, True, False, out_dtype=w.dtype, name="hgrn2_proj_db")
    return dh, dw, dlb, dgn


hgrn2_mixer.defvjp(_hgrn2_mixer_fwd, _hgrn2_mixer_bwd)


D_MODEL = 1024
DEPTH = 2
SSM_GROUPS, SSM_GROUP_CH, SSM_STATE = 32, 16, 64
SSM_WIDTH = SSM_GROUPS * SSM_GROUP_CH
MLA_HEADS, MLA_NOPE, MLA_ROPE, MLA_V = 8, 64, 32, 64
MLA_Q_RANK, MLA_KV_RANK = 512, 256
HG_WIDTH = HG_HEADS * LANES
X_HEADS, X_HEAD_DIM = 4, 128
X_WIDTH = X_HEADS * X_HEAD_DIM
D_FF = 2816
ROPE_THETA = 10000.0
IN_SPLITS = (SSM_WIDTH, MLA_Q_RANK, MLA_KV_RANK, MLA_ROPE, HG_WIDTH, HG_WIDTH, HG_WIDTH, HG_WIDTH, 3 * D_MODEL)
ROPE_LANE0 = MLA_NOPE
MLA_Q_SCALE = LOG2E / math.sqrt(MLA_NOPE + MLA_ROPE)
ROW_TILE = 256


def _t_rms(x, g):
    return (_rms(x, g).astype(BF16),)


def _t_s5_act(y, u, d):
    return (jax.nn.gelu(y + d * u).astype(BF16),)


def _t_glu(z):
    zo, zg = split_lanes(z.astype(F32), D_MODEL)
    return ((zo * jax.nn.sigmoid(zg)).astype(BF16),)


def _t_mla_rope(q, k, kr, c, sa, sb):
    rep = lambda t: jnp.concatenate([t] * MLA_HEADS, axis=1)
    half = MLA_ROPE // 2
    q_out = rope_lanes(q, rep(c), rep(sa), rep(sb), half) * MLA_Q_SCALE
    kr_out = rope_lanes(kr, c, sa, sb, half)
    return q_out.astype(BF16), (k + join_lanes((kr_out,) * MLA_HEADS)).astype(BF16)


def _t_merge(y_ssm, y_mla, y_hg, gates):
    g0, g1, g2 = split_lanes(gates.astype(F32), D_MODEL)
    mix = (jax.nn.sigmoid(g0) * y_ssm.astype(F32) + jax.nn.sigmoid(g1) * y_mla.astype(F32)
           + jax.nn.sigmoid(g2) * y_hg.astype(F32))
    return (mix.astype(BF16),)


def _t_xattn(q, kv):
    scale = 1.0 / math.sqrt(X_HEAD_DIM)
    heads = split_lanes(kv, X_HEAD_DIM)
    outs = []
    for qh, kh, vh in zip(split_lanes(q, X_HEAD_DIM), heads[:X_HEADS], heads[X_HEADS:]):
        sc = bdot_nt(qh, kh) * scale
        p = jnp.exp(sc - jnp.max(sc, axis=-1, keepdims=True))
        p = p / jnp.sum(p, axis=-1, keepdims=True)
        outs.append(bdot_nn(p, vh))
    return (join_lanes(tuple(outs)).astype(BF16),)


def _t_swiglu(gate_up):
    gt, up = split_lanes(gate_up.astype(F32), D_FF)
    return ((gt * jax.nn.sigmoid(gt) * up).astype(BF16),)


def _t_loss(x, tgt, g):
    e = _rms(x, g) - tgt
    return (jnp.broadcast_to(jnp.mean(e * e, axis=-1, keepdims=True), (x.shape[0], LANES)),)


rms_op = rowwise(_t_rms, 1, 0, ROW_TILE, "rmsnorm")
rms_res_op = rowwise(_t_rms, 1, 0, ROW_TILE, "rmsnorm_res", passthrough=True)
s5_act_op = rowwise(_t_s5_act, 2, 0, ROW_TILE, "s5_act")
glu_op = rowwise(_t_glu, 1, 0, ROW_TILE, "glu")
mla_rope_op = rowwise(_t_mla_rope, 3, 3, ROW_TILE, "mla_rope")
merge_op = rowwise(_t_merge, 4, 0, ROW_TILE, "merge")
xattn_op = rowwise(_t_xattn, 1, 0, ROW_TILE, "xattn")
swiglu_op = rowwise(_t_swiglu, 1, 0, ROW_TILE, "swiglu")
loss_op = rowwise(_t_loss, 1, 1, ROW_TILE, "loss")


def _rope_tables(positions):
    half = MLA_ROPE // 2
    inv_freq = ROPE_THETA ** (-jnp.arange(half, dtype=F32) / half)
    ang = positions.astype(F32)[:, None] * inv_freq
    cos, sin = jnp.cos(ang), jnp.sin(ang)
    s = positions.shape[0]
    z = lambda w: jnp.zeros((s, w), F32)
    tail = LANES - ROPE_LANE0 - MLA_ROPE
    c = jnp.concatenate([jnp.ones((s, ROPE_LANE0), F32), cos, cos, z(tail)], axis=1)
    sa = jnp.concatenate([z(ROPE_LANE0), -sin, z(half), z(tail)], axis=1)
    sb = jnp.concatenate([z(ROPE_LANE0), z(half), sin, z(tail)], axis=1)
    return c, sa, sb


def _s5_operators(lam_re, lam_im, b_re, b_im, c_re, c_im, log_step):
    g, p, h = SSM_GROUPS, SSM_STATE, SSM_GROUP_CH
    lam = lax.complex(lam_re, lam_im)
    lam_bar = jnp.exp(lam * jnp.exp(log_step)[:, None])
    b_bar = ((lam_bar - 1.0) / lam)[..., None] * lax.complex(b_re, b_im)
    per = LANES // h
    nb = g // per
    eye = jnp.eye(per, dtype=F32)
    bd = lambda t: jnp.einsum("jgph,gk->jghkp", t.reshape(nb, per, p, h), eye).reshape(nb, per * h, per * p)
    cd = lambda t: jnp.einsum("jghp,gk->jgpkh", t.reshape(nb, per, h, p), eye).reshape(nb, per * p, per * h)
    a = jnp.stack([jnp.real(lam_bar).reshape(-1), jnp.imag(lam_bar).reshape(-1)])
    return a, bd(jnp.real(b_bar)), bd(jnp.imag(b_bar)), cd(c_re), cd(-c_im)


LATENT_WIDTH = 1536
_LATENT = {}
_off = 0
for _name, _w in (("u", SSM_WIDTH), ("q_lat", MLA_Q_RANK), ("kv_lat", MLA_KV_RANK), ("k_rope", LANES)):
    _LATENT[_name] = (_off, _off + _w)
    _off += _w


def _layer_matrices(w, l):
    return {**_mixer_matrices(w, l), **_tail_matrices(w, l)}


def _tail_matrices(w, l):
    return dict(x_q=w["x_w_q"][l], x_kv=w["x_w_kv"][l], x_o=w["x_w_o"][l], ffn_gu=w["ffn_w_gate_up"][l],
                ffn_d=w["ffn_w_down"][l])


def _mixer_matrices(w, l):
    w_in = w["w_in"][l]
    d, dt = w_in.shape[0], w_in.dtype
    z = lambda n: jnp.zeros((d, n), dt)
    r0 = SSM_WIDTH + MLA_Q_RANK + MLA_KV_RANK
    r1 = r0 + MLA_ROPE
    r2 = r1 + HG_PARTS * HG_WIDTH
    w_latent = jnp.concatenate([w_in[:, :r0], z(ROPE_LANE0), w_in[:, r0:r1],
                                z(LATENT_WIDTH - r0 - ROPE_LANE0 - MLA_ROPE)], axis=1)
    pad_heads = lambda t: jnp.pad(t, ((0, 0), (0, 0), (0, LANES - t.shape[2]))).reshape(t.shape[0], -1)
    uq = w["mla_w_uq"][l].reshape(MLA_Q_RANK, MLA_HEADS, MLA_NOPE + MLA_ROPE)
    ukv = w["mla_w_ukv"][l].reshape(MLA_KV_RANK, MLA_HEADS, MLA_NOPE + MLA_V)
    wo = w["mla_w_o"][l].reshape(MLA_HEADS, MLA_V, D_MODEL)
    return dict(
        w_latent=w_latent, w_hg=w_in[:, r1:r2], w_gates=w_in[:, r2:], glu=w["ssm_w_glu"][l],
        uq=pad_heads(uq), uk=pad_heads(ukv[:, :, :MLA_NOPE]), uv=pad_heads(ukv[:, :, MLA_NOPE:]),
        mla_o=jnp.pad(wo, ((0, 0), (0, LANES - MLA_V), (0, 0))).reshape(MLA_HEADS * LANES, D_MODEL),
        hg_o=w["hg_w_o"][l], w_out=w["w_out"][l])


def _layer(x, mem, tabs, m, sp, l, lower_bound):
    return _tail(_mixer(x, tabs, m, sp, l, lower_bound), mem, m, sp, l)


def _mixer(x, tabs, m, sp, l, lower_bound):
    row = lambda name: sp[name][l].reshape(1, -1)
    h, x = rms_res_op(x, row("norm_mix"))
    latent = matmul(h, m["w_latent"])
    seg = lambda name: latent[:, _LATENT[name][0]:_LATENT[name][1]]
    a, bd_r, bd_i, cd_r, cd_i = _s5_operators(*(sp[n][l] for n in (
        "ssm_lam_re", "ssm_lam_im", "ssm_b_re", "ssm_b_im", "ssm_c_re", "ssm_c_im", "ssm_log_step")))
    u = seg("u")
    y = s5_core(u, a, bd_r, bd_i, cd_r, cd_i)
    (ya,) = s5_act_op(y, u, row("ssm_d"))
    (y_ssm,) = glu_op(matmul(ya, m["glu"], BF16))
    (qn,) = rms_op(seg("q_lat"), row("mla_q_norm"))
    (kvn,) = rms_op(seg("kv_lat"), row("mla_kv_norm"))
    q, k = mla_rope_op(matmul(qn, m["uq"]), matmul(kvn, m["uk"]), seg("k_rope"), *tabs)
    o = causal_attention(q, k, matmul(kvn, m["uv"], BF16))
    y_mla = matmul(o, m["mla_o"], BF16)
    y_hg = matmul(hgrn2_mixer(h, m["w_hg"], lower_bound, row("hg_g_norm")), m["hg_o"], BF16)
    (merged,) = merge_op(y_ssm, y_mla, y_hg, matmul(h, m["w_gates"], BF16))
    return matmul_add(merged, m["w_out"], x)


def _tail(x, mem, m, sp, l):
    row = lambda name: sp[name][l].reshape(1, -1)
    hc, x = rms_res_op(x, row("norm_cross"))
    (mn,) = rms_op(mem, row("norm_mem"))
    (ox,) = xattn_op(matmul(hc, m["x_q"], BF16), matmul(mn, m["x_kv"]))
    x = matmul_add(ox, m["x_o"], x)
    hf, x = rms_res_op(x, row("norm_ffn"))
    (act,) = swiglu_op(matmul(hf, m["ffn_gu"], BF16))
    return matmul_add(act, m["ffn_d"], x)


def _lower_bounds(hg_lb):
    lb_p = jax.nn.softmax(hg_lb, axis=0)
    return jnp.cumsum(lb_p, axis=0) - lb_p[0:1]


def _final_loss(target, x, norm_final):
    (row_loss,) = loss_op(x, target, norm_final.reshape(1, -1))
    return 0.5 * jnp.sum(row_loss[:, 0])


def _local_loss(x, mem, positions, target, w, sp):
    tabs = _rope_tables(positions)
    lower = _lower_bounds(sp["hg_lb"])
    for l in range(DEPTH):
        x = _layer(x, mem, tabs, _layer_matrices(w, l), sp, l, lower[l].reshape(1, -1))
    return _final_loss(target, x, sp["norm_final"])


N_DEV = 8
N_CHIPS = 4
COMM_LANES = 512
MESH_ID = pl.DeviceIdType.MESH
_ANY = pl.BlockSpec(memory_space=pl.ANY)
_OTHER_CHIPS = ((1, 0), (0, 1), (1, 1))


def _place():
    return lax.axis_index("x"), lax.axis_index("y"), lax.axis_index("c")


def _all_gather_call(blocks, name):
    n = len(blocks)

    def body(*refs):
        x_refs, out_refs = refs[:n], refs[n:2 * n]
        send_sems, recv_sems, local_sems = refs[2 * n:]
        x, y, c = _place()
        me, sibling = (x, y, c), (x, y, 1 - c)
        chips = [(x ^ fx, y ^ fy) for fx, fy in _OTHER_CHIPS]

        def slot(i, px, py, pc):
            return out_refs[i].at[4 * px + 2 * py + pc]

        def copy(i, k, blk, to, src=None):
            return pltpu.make_async_remote_copy(
                src_ref=slot(i, *blk) if src is None else src, dst_ref=slot(i, *blk),
                send_sem=send_sems.at[i, k], recv_sem=recv_sems.at[i, k], device_id=to, device_id_type=MESH_ID)

        mine = [pltpu.make_async_copy(x_refs[i], slot(i, *me), local_sems.at[i]) for i in range(n)]
        first = []
        for i in range(n):
            first.append(copy(i, 0, me, sibling, src=x_refs[i]))
            first += [copy(i, 1 + j, me, (*chip, c), src=x_refs[i]) for j, chip in enumerate(chips)]
        for cp in mine + first:
            cp.start()
        passed = []
        for j, chip in enumerate(chips):
            for i in range(n):
                copy(i, 1 + j, (*chip, c), me).wait_recv()
                passed.append(copy(i, 4 + j, (*chip, c), sibling))
                passed[-1].start()
        for i in range(n):
            copy(i, 0, sibling, me).wait_recv()
            for j, chip in enumerate(chips):
                copy(i, 4 + j, (*chip, 1 - c), me).wait_recv()
        for cp in first + passed:
            cp.wait_send()
        for cp in mine:
            cp.wait()

    return pl.pallas_call(
        body,
        name=name,
        out_shape=[jax.ShapeDtypeStruct((N_DEV,) + b.shape, b.dtype) for b in blocks],
        in_specs=[_ANY] * n,
        out_specs=[_ANY] * n,
        scratch_shapes=[pltpu.SemaphoreType.DMA((n, 7)), pltpu.SemaphoreType.DMA((n, 7)), pltpu.SemaphoreType.DMA((n,))],
    )(*blocks)


def _pair_exchange_call(gs, name):
    n = len(gs)

    def body(*refs):
        g_refs, got_refs = refs[:n], refs[n:2 * n]
        send_sems, recv_sems = refs[2 * n:]
        x, y, c = _place()
        sends = [pltpu.make_async_remote_copy(
            src_ref=g_refs[i].at[2 * p + (1 - c)], dst_ref=got_refs[i].at[p],
            send_sem=send_sems.at[i, p], recv_sem=recv_sems.at[i, p], device_id=(x, y, 1 - c), device_id_type=MESH_ID)
            for i in range(n) for p in range(N_CHIPS)]
        for cp in sends:
            cp.start()
        for cp in sends:
            cp.wait_recv()
        for cp in sends:
            cp.wait_send()

    return pl.pallas_call(
        body,
        name=name,
        out_shape=[jax.ShapeDtypeStruct((N_CHIPS,) + g.shape[1:], g.dtype) for g in gs],
        in_specs=[_ANY] * n,
        out_specs=[_ANY] * n,
        scratch_shapes=[pltpu.SemaphoreType.DMA((n, N_CHIPS))] * 2,
    )(*gs)


def _chip_exchange_call(parts, name):
    n = len(parts)

    def body(*refs):
        p_refs, got_refs = refs[:n], refs[n:2 * n]
        send_sems, recv_sems = refs[2 * n:]
        x, y, c = _place()
        sends = []
        for i in range(n):
            for k, (fx, fy) in enumerate(_OTHER_CHIPS):
                px, py = x ^ fx, y ^ fy
                sends.append(pltpu.make_async_remote_copy(
                    src_ref=p_refs[i].at[2 * px + py], dst_ref=got_refs[i].at[k],
                    send_sem=send_sems.at[i, k], recv_sem=recv_sems.at[i, k], device_id=(px, py, c), device_id_type=MESH_ID))
        for cp in sends:
            cp.start()
        for cp in sends:
            cp.wait_recv()
        for cp in sends:
            cp.wait_send()

    return pl.pallas_call(
        body,
        name=name,
        out_shape=[jax.ShapeDtypeStruct((3,) + p.shape[1:], p.dtype) for p in parts],
        in_specs=[_ANY] * n,
        out_specs=[_ANY] * n,
        scratch_shapes=[pltpu.SemaphoreType.DMA((n, 3))] * 2,
    )(*parts)


def _rows_cols(shape):
    return math.prod(shape[:-1]), shape[-1]


def _pair_sum_call(g, got, c_idx, name):
    rows, cols = _rows_cols(got.shape[1:])
    tr = _pick_tile(rows, (512, 256, 128, 64, 32, 16))

    def body(c_ref, a_ref, b_ref, o_ref):
        o_ref[...] = (a_ref[...].astype(F32) + b_ref[...].astype(F32)).astype(o_ref.dtype)

    spec = pl.BlockSpec((1, tr, cols), lambda p, i, c_ref: (p, i, 0))
    out = pl.pallas_call(
        body,
        name=name,
        grid_spec=pltpu.PrefetchScalarGridSpec(
            num_scalar_prefetch=1, grid=(N_CHIPS, rows // tr),
            in_specs=[pl.BlockSpec((1, tr, cols), lambda p, i, c_ref: (2 * p + c_ref[0], i, 0)), spec],
            out_specs=spec),
        out_shape=jax.ShapeDtypeStruct((N_CHIPS, rows, cols), got.dtype),
        compiler_params=_cparams("parallel", "parallel"),
    )(c_idx, g.reshape(N_DEV, rows, cols), got.reshape(N_CHIPS, rows, cols))
    return out.reshape(got.shape)


def _chip_sum_call(part, got, chip_idx, name):
    rows, cols = _rows_cols(got.shape[1:])
    tr = _pick_tile(rows, (512, 256, 128, 64, 32, 16))

    def body(p_ref, a_ref, b_ref, o_ref):
        acc = a_ref[0].astype(F32)
        for k in range(3):
            acc = acc + b_ref[k].astype(F32)
        o_ref[...] = acc

    out = pl.pallas_call(
        body,
        name=name,
        grid_spec=pltpu.PrefetchScalarGridSpec(
            num_scalar_prefetch=1, grid=(rows // tr,),
            in_specs=[pl.BlockSpec((1, tr, cols), lambda i, p_ref: (p_ref[0], i, 0)),
                      pl.BlockSpec((3, tr, cols), lambda i, p_ref: (0, i, 0))],
            out_specs=pl.BlockSpec((tr, cols), lambda i, p_ref: (i, 0))),
        out_shape=jax.ShapeDtypeStruct((rows, cols), F32),
        compiler_params=_cparams("parallel"),
    )(chip_idx, part.reshape(N_CHIPS, rows, cols), got.reshape(3, rows, cols))
    return out.reshape(got.shape[1:])


def _reduce_scatter(gs, name):
    x, y, c = _place()
    c_idx = c.astype(jnp.int32).reshape(1)
    chip_idx = (2 * x + y).astype(jnp.int32).reshape(1)
    gots = _pair_exchange_call(gs, name + "_pair")
    parts = [_pair_sum_call(g, got, c_idx, name + "_pair_sum") for g, got in zip(gs, gots)]
    gots = _chip_exchange_call(parts, name + "_chip")
    return [_chip_sum_call(p, got, chip_idx, name + "_chip_sum") for p, got in zip(parts, gots)]


_HBM = pl.BlockSpec(memory_space=pltpu.HBM)
_SEM = pl.BlockSpec(memory_space=pltpu.SEMAPHORE)
_SIDE_EFFECT = pltpu.SideEffectType.DATAFLOW_SIDE_EFFECTING
N_PEERS = N_DEV - 1


def _peer(k):
    x, y, c = _place()
    px, py, pc = x ^ ((k >> 2) & 1), y ^ ((k >> 1) & 1), c ^ (k & 1)
    return (px, py, pc), 4 * px + 2 * py + pc


def _exchange_copy(src_ref, land_ref, send_sems, recv_sems, i, k, scatter, receiving):
    x, y, c = _place()
    me = 4 * x + 2 * y + c
    peer, peer_idx = _peer(k)
    sem = i * N_PEERS + k - 1
    return pltpu.make_async_remote_copy(
        src_ref=src_ref.at[peer_idx] if scatter else src_ref, dst_ref=land_ref.at[peer_idx if receiving else me],
        send_sem=send_sems.at[sem], recv_sem=recv_sems.at[sem], device_id=peer, device_id_type=MESH_ID)


def _exchange_start_call(srcs, after, scatter, name):
    n = len(srcs)
    slot_shapes = [s.shape[1:] if scatter else s.shape for s in srcs]

    def body(*refs):
        src_refs, land_refs = refs[:n], refs[n:2 * n]
        send_sems, recv_sems = refs[2 * n + 1], refs[2 * n + 2]
        token = refs[-1]
        for i in range(n):
            for k in range(1, N_DEV):
                _exchange_copy(src_refs[i], land_refs[i], send_sems, recv_sems, i, k, scatter, False).start()
        token[...] = jnp.zeros_like(token)

    lands = [pltpu.with_memory_space_constraint(lax.empty((N_DEV,) + shp, s.dtype), pltpu.HBM)
             for shp, s in zip(slot_shapes, srcs)]
    out = pl.pallas_call(
        body,
        name=name,
        out_shape=([pltpu.SemaphoreType.DMA((n * N_PEERS,)), pltpu.SemaphoreType.DMA((n * N_PEERS,))]
                   + [pltpu.HBM(s.shape, s.dtype) for s in srcs] + [pltpu.HBM(l.shape, l.dtype) for l in lands]
                   + [jax.ShapeDtypeStruct((SUBLANES, LANES), F32)]),
        in_specs=[_HBM] * (2 * n) + [pl.BlockSpec(memory_space=pl.ANY)],
        out_specs=[_SEM, _SEM] + [_HBM] * (2 * n) + [pl.BlockSpec(memory_space=pltpu.VMEM)],
        input_output_aliases={j: 2 + j for j in range(2 * n)},
        compiler_params=pltpu.CompilerParams(has_side_effects=_SIDE_EFFECT),
    )(*[pltpu.with_memory_space_constraint(s, pltpu.HBM) for s in srcs], *lands, after)
    return out[0], out[1], list(out[2:2 + n]), list(out[2 + n:2 + 2 * n]), out[-1]


def _exchange_wait_call(started, after, scatter, name):
    send_sems, recv_sems, srcs, lands, _ = started
    n = len(srcs)

    def body(*refs):
        src_refs, land_refs = refs[:n], refs[n:2 * n]
        send_s, recv_s = refs[2 * n], refs[2 * n + 1]
        for i in range(n):
            for k in range(1, N_DEV):
                cp = _exchange_copy(src_refs[i], land_refs[i], send_s, recv_s, i, k, scatter, True)
                cp.wait_send()
                cp.wait_recv()

    out = pl.pallas_call(
        body,
        name=name,
        out_shape=[pltpu.HBM(s.shape, s.dtype) for s in srcs] + [pltpu.HBM(l.shape, l.dtype) for l in lands],
        in_specs=[_HBM] * (2 * n) + [_SEM, _SEM, pl.BlockSpec(memory_space=pl.ANY)],
        out_specs=[_HBM] * (2 * n),
        input_output_aliases={j: j for j in range(2 * n)},
        compiler_params=pltpu.CompilerParams(has_side_effects=_SIDE_EFFECT),
    )(*srcs, *lands, send_sems, recv_sems, after)
    return list(out[n:])


def _own_slot(land, own):
    x, y, c = _place()
    return lax.dynamic_update_index_in_dim(land, own, 4 * x + 2 * y + c, 0)


def _slot_sum_call(land, name):
    rows, cols = _rows_cols(land.shape[1:])
    tr = _pick_tile(rows, (256, 128, 64, 32, 16))

    def body(land_ref, o_ref):
        acc = land_ref[0].astype(F32)
        for s in range(1, N_DEV):
            acc = acc + land_ref[s].astype(F32)
        o_ref[...] = acc

    out = pl.pallas_call(
        body,
        name=name,
        grid=(rows // tr,),
        in_specs=[pl.BlockSpec((N_DEV, tr, cols), lambda i: (0, i, 0))],
        out_specs=pl.BlockSpec((tr, cols), lambda i: (i, 0)),
        out_shape=jax.ShapeDtypeStruct((rows, cols), F32),
        compiler_params=_cparams("parallel"),
    )(land.reshape(N_DEV, rows, cols))
    return out.reshape(land.shape[1:])


SMALL_BLOCK_ROWS = 16


def _pack_small(parts):
    flat = jnp.concatenate([p.reshape(-1) for p in parts])
    chunk = N_DEV * SMALL_BLOCK_ROWS * COMM_LANES
    flat = jnp.pad(flat, (0, (-flat.shape[0]) % chunk))
    return flat.reshape(N_DEV, -1, COMM_LANES)


def _unpack_small(buf, shapes):
    flat = buf.reshape(-1)
    out, off = [], 0
    for shp in shapes:
        n = math.prod(shp)
        out.append(flat[off:off + n].reshape(shp))
        off += n
    return out


SHARDED = dict(w_in=2, ssm_w_glu=2, mla_w_uq=2, mla_w_ukv=2, mla_w_o=2, hg_w_o=2, w_out=1, x_w_q=1, x_w_kv=1,
               x_w_o=2, ffn_w_gate_up=2, ffn_w_down=1)
REPLICATED = ("norm_mix", "ssm_lam_re", "ssm_lam_im", "ssm_b_re", "ssm_b_im", "ssm_c_re", "ssm_c_im", "ssm_d",
              "ssm_log_step", "mla_q_norm", "mla_kv_norm", "hg_lb", "hg_g_norm", "norm_cross", "norm_mem", "norm_ffn",
              "norm_final")


def _join_shards(stacked, axis):
    n, l, a, b = stacked.shape
    if axis == 1:
        return stacked.transpose(1, 0, 2, 3).reshape(l, n * a, b)
    return stacked.transpose(1, 2, 0, 3).reshape(l, a, n * b)


def _split_shards(full, axis):
    l, a, b = full.shape
    if axis == 1:
        return full.reshape(l, N_DEV, a // N_DEV, b).transpose(1, 0, 2, 3)
    return full.reshape(l, a, N_DEV, b // N_DEV).transpose(2, 0, 1, 3)


MIXER_WEIGHTS = ("w_in", "ssm_w_glu", "mla_w_uq", "mla_w_ukv", "mla_w_o", "hg_w_o", "w_out")
TAIL_WEIGHTS = ("x_w_q", "x_w_kv", "x_w_o", "ffn_w_gate_up", "ffn_w_down")


def _mixer_fn(l, tabs):
    def f(x, full, small, lower):
        m = _mixer_matrices(dict(zip(MIXER_WEIGHTS, full)), 0)
        return _mixer(x, tabs, m, dict(zip(REPLICATED, small)), l, lower[l].reshape(1, -1))
    return f


def _tail_fn(l, mem):
    def f(x, full, small):
        return _tail(x, mem, _tail_matrices(dict(zip(TAIL_WEIGHTS, full)), 0), dict(zip(REPLICATED, small)), l)
    return f


ADAM_LR, ADAM_B1, ADAM_B2, ADAM_EPS, ADAM_WD, ADAM_STEP = 0.001, 0.9, 0.999, 1e-08, 0.01, 10


def _adamw_update(w, g, m, v):
    m_new = ADAM_B1 * m + (1.0 - ADAM_B1) * g
    v_new = ADAM_B2 * v + (1.0 - ADAM_B2) * jnp.square(g)
    m_hat = m_new / (1.0 - ADAM_B1 ** ADAM_STEP)
    v_hat = v_new / (1.0 - ADAM_B2 ** ADAM_STEP)
    return -ADAM_LR * (m_hat / (jnp.sqrt(v_hat) + ADAM_EPS) + ADAM_WD * w), m_new, v_new


def _adamw_stacked_call(w, g, m, v, name):
    depth, rows, cols = w.shape
    tr = _pick_tile(rows, (512, 256, 128, 64, 32, 16, 8))

    def body(w_ref, g_ref, m_ref, v_ref, d_ref, nm_ref, nv_ref):
        d_ref[...], nm_ref[...], nv_ref[...] = _adamw_update(w_ref[...], g_ref[...], m_ref[...], v_ref[...])

    spec = pl.BlockSpec((None, tr, cols), lambda l, i: (l, i, 0))
    return tuple(pl.pallas_call(
        body, name=name, grid=(depth, rows // tr), in_specs=[spec] * 4, out_specs=[spec] * 3,
        out_shape=[jax.ShapeDtypeStruct(w.shape, F32)] * 3, compiler_params=_cparams("parallel", "parallel"),
    )(w, g, m, v))


def _adamw_call(w, g, m, v, name):
    shape = w.shape
    if len(shape) == 3:
        return _adamw_stacked_call(w, g, m, v, name)
    cols = shape[-1]
    rows = math.prod(shape[:-1]) if len(shape) > 1 else 1
    tr = _pick_tile(rows, (512, 256, 128, 64, 32, 16, 8))

    def body(w_ref, g_ref, m_ref, v_ref, d_ref, nm_ref, nv_ref):
        d_ref[...], nm_ref[...], nv_ref[...] = _adamw_update(w_ref[...], g_ref[...], m_ref[...], v_ref[...])

    spec = pl.BlockSpec((tr, cols), lambda i: (i, 0))
    outs = pl.pallas_call(
        body, name=name, grid=(rows // tr,), in_specs=[spec] * 4, out_specs=[spec] * 3,
        out_shape=[jax.ShapeDtypeStruct((rows, cols), F32)] * 3, compiler_params=_cparams("parallel"),
    )(*(t.reshape(rows, cols) for t in (w, g, m, v)))
    return tuple(o.reshape(shape) for o in outs)


WEIGHTS = ("norm_mix", "w_in", "ssm_lam_re", "ssm_lam_im", "ssm_b_re", "ssm_b_im", "ssm_c_re", "ssm_c_im", "ssm_d",
           "ssm_log_step", "ssm_w_glu", "mla_q_norm", "mla_kv_norm", "mla_w_uq", "mla_w_ukv", "mla_w_o", "hg_lb",
           "hg_g_norm", "hg_w_o", "w_out", "norm_cross", "norm_mem", "x_w_q", "x_w_kv", "x_w_o", "norm_ffn",
           "ffn_w_gate_up", "ffn_w_down", "norm_final")


def kernel(x, mem, positions, norm_mix, w_in, ssm_lam_re, ssm_lam_im, ssm_b_re, ssm_b_im, ssm_c_re, ssm_c_im, ssm_d, ssm_log_step, ssm_w_glu, mla_q_norm, mla_kv_norm, mla_w_uq, mla_w_ukv, mla_w_o, hg_lb, hg_g_norm, hg_w_o, w_out, norm_cross, norm_mem, x_w_q, x_w_kv, x_w_o, norm_ffn, ffn_w_gate_up, ffn_w_down, norm_final, loss_target, m_norm_mix, m_w_in, m_ssm_lam_re, m_ssm_lam_im, m_ssm_b_re, m_ssm_b_im, m_ssm_c_re, m_ssm_c_im, m_ssm_d, m_ssm_log_step, m_ssm_w_glu, m_mla_q_norm, m_mla_kv_norm, m_mla_w_uq, m_mla_w_ukv, m_mla_w_o, m_hg_lb, m_hg_g_norm, m_hg_w_o, m_w_out, m_norm_cross, m_norm_mem, m_x_w_q, m_x_w_kv, m_x_w_o, m_norm_ffn, m_ffn_w_gate_up, m_ffn_w_down, m_norm_final, v_norm_mix, v_w_in, v_ssm_lam_re, v_ssm_lam_im, v_ssm_b_re, v_ssm_b_im, v_ssm_c_re, v_ssm_c_im, v_ssm_d, v_ssm_log_step, v_ssm_w_glu, v_mla_q_norm, v_mla_kv_norm, v_mla_w_uq, v_mla_w_ukv, v_mla_w_o, v_hg_lb, v_hg_g_norm, v_hg_w_o, v_w_out, v_norm_cross, v_norm_mem, v_x_w_q, v_x_w_kv, v_x_w_o, v_norm_ffn, v_ffn_w_gate_up, v_ffn_w_down, v_norm_final):
    given = dict(locals())
    weights = {n: given[n] for n in WEIGHTS}
    small = tuple(weights[n] for n in REPLICATED)
    layer1 = MIXER_WEIGHTS + TAIL_WEIGHTS
    shards = lambda names, l: [weights[n][l:l + 1].astype(BF16) for n in names]
    join = lambda names, stacked: tuple(_join_shards(p, SHARDED[n]) for n, p in zip(names, stacked))
    split = lambda names, cts: [_split_shards(ct, SHARDED[n]) for n, ct in zip(names, cts)]
    landed = lambda names, lands, own: join(names, [_own_slot(land, o) for land, o in zip(lands, own)])
    xs, tabs = x[0], _rope_tables(positions[0])
    lower, vjp_lower = jax.vjp(_lower_bounds, hg_lb)
    me = 4 * lax.axis_index("x") + 2 * lax.axis_index("y") + lax.axis_index("c")

    got_m0 = _all_gather_call(shards(MIXER_WEIGHTS, 0), "weights_all_gather_m0")
    own_t0, own_l1 = shards(TAIL_WEIGHTS, 0), shards(layer1, 1)
    gather_t0 = _exchange_start_call(own_t0, got_m0[0], False, "weights_gather_start_t0")
    gather_l1 = _exchange_start_call(own_l1, gather_t0[4], False, "weights_gather_start_l1")
    xs = xs + gather_l1[4][0, 0]
    xa0, vjp_m0 = jax.vjp(_mixer_fn(0, tabs), xs, join(MIXER_WEIGHTS, got_m0), small, lower)
    full_t0 = landed(TAIL_WEIGHTS, _exchange_wait_call(gather_t0, xa0, False, "weights_gather_wait_t0"), own_t0)
    x1, vjp_t0 = jax.vjp(_tail_fn(0, mem[0]), xa0, full_t0, small)
    full_l1 = landed(layer1, _exchange_wait_call(gather_l1, x1, False, "weights_gather_wait_l1"), own_l1)
    xa1, vjp_m1 = jax.vjp(_mixer_fn(1, tabs), x1, full_l1[:len(MIXER_WEIGHTS)], small, lower)
    x2, vjp_t1 = jax.vjp(_tail_fn(1, mem[0]), xa1, full_l1[len(MIXER_WEIGHTS):], small)
    loss_local, vjp_loss = jax.vjp(functools.partial(_final_loss, loss_target[0]), x2, norm_final)

    def scatter_start(names, cts, dx, tag):
        gs = split(names, cts)
        started = _exchange_start_call(gs, dx, True, "grads_scatter_start_" + tag)
        return (started, gs), dx + started[4][0, 0]

    def scatter_finish(pending, after, tag):
        started, gs = pending
        lands = _exchange_wait_call(started, after, True, "grads_scatter_wait_" + tag)
        return [_slot_sum_call(_own_slot(land, lax.dynamic_index_in_dim(g, me, 0, keepdims=False)), "grads_slot_sum_" + tag)
                for land, g in zip(lands, gs)]

    dx2, d_norm_final = vjp_loss(jnp.ones((), F32))
    dxa1, dfull_t1, dsmall_t1 = vjp_t1(dx2)
    pend_t1, dxa1 = scatter_start(TAIL_WEIGHTS, dfull_t1, dxa1, "t1")
    dx1, dfull_m1, dsmall_m1, dlower1 = vjp_m1(dxa1)
    pend_m1, dx1 = scatter_start(MIXER_WEIGHTS, dfull_m1, dx1, "m1")
    dxa0, dfull_t0, dsmall_t0 = vjp_t0(dx1)
    pend_t0, dxa0 = scatter_start(TAIL_WEIGHTS, dfull_t0, dxa0, "t0")
    gx, dfull_m0, dsmall_m0, dlower0 = vjp_m0(dxa0)
    by_layer = {
        0: dict(zip(MIXER_WEIGHTS + TAIL_WEIGHTS,
                    _reduce_scatter(split(MIXER_WEIGHTS, dfull_m0), "grads_reduce_scatter_m0") + scatter_finish(pend_t0, gx, "t0"))),
        1: dict(zip(layer1, scatter_finish(pend_m1, gx, "m1") + scatter_finish(pend_t1, gx, "t1")))}
    grads = {n: jnp.concatenate([by_layer[0][n], by_layer[1][n]], axis=0) for n in SHARDED}

    d_small = dict(zip(REPLICATED, (a + b + c + d for a, b, c, d in zip(dsmall_m0, dsmall_t0, dsmall_m1, dsmall_t1))))
    d_small["norm_final"] = d_small["norm_final"] + d_norm_final
    d_small["hg_lb"] = d_small["hg_lb"] + vjp_lower(dlower0 + dlower1)[0]
    shapes = [d_small[n].shape for n in REPLICATED]
    (mine,) = _reduce_scatter([_pack_small([d_small[n] for n in REPLICATED])], "small_reduce_scatter")
    (total,) = _all_gather_call([mine], "small_all_gather")
    grads.update(zip(REPLICATED, _unpack_small(total, shapes)))

    loss = lax.psum(loss_local, ("x", "y", "c"))
    steps = {n: _adamw_call(weights[n], grads[n], given["m_" + n], given["v_" + n], "adamw_" + n) for n in WEIGHTS}
    return (loss, gx[None], *[grads[n] for n in WEIGHTS], *[steps[n][0] for n in WEIGHTS],
            *[steps[n][1] for n in WEIGHTS], *[steps[n][2] for n in WEIGHTS])
```

```python
import functools
import math

import jax
import jax.numpy as jnp
from jax import lax
from jax.experimental import pallas as pl
from jax.experimental.pallas import tpu as pltpu

F32 = jnp.float32
BF16 = jnp.bfloat16

VMEM_LIMIT_BYTES = 48 * 1024 * 1024
LANES = 128
SUBLANES = 8


def _cparams(*sem):
    return pltpu.CompilerParams(dimension_semantics=sem, vmem_limit_bytes=VMEM_LIMIT_BYTES)


def _pick_tile(n, cands):
    for c in cands:
        if n % c == 0:
            return c
    return n


MM_VMEM_BUDGET = 38 * 1024 * 1024
MM_STEP_US = 0.35
HBM_BYTES_PER_US = 3.0e6
VREG_RMW_PER_US = 1.5e3


def _divisor_tiles(dim, cands):
    out = [t for t in cands if dim % t == 0]
    return out or [dim]


def _mm_tiles(m, n, k, sa, sb, so):
    tms = _divisor_tiles(m, (1408, 1024, 512, 256, 128, 64, 32, 16, 8))[:2]
    tns = _divisor_tiles(n, (2048, 1536, 1408, 1024, 768, 512, 384, 256, 128))
    tks = [k // d for d in (1, 2, 4, 8, 13, 16, 26, 32, 52) if k % d == 0 and (k // d) % LANES == 0] or [k]
    best = None
    for tk in tks:
        nk = k // tk
        for tm in tms:
            for tn in tns:
                vmem = 2 * (tm * tk * sa + tk * tn * sb + tm * tn * so) + (tm * tn * 4 if nk > 1 else 0)
                if vmem > MM_VMEM_BUDGET:
                    continue
                steps = (m // tm) * (n // tn) * nk
                a_reads = m * k * sa * (n // tn if nk > 1 else 1)
                b_reads = k * n * sb * (m // tm if (nk > 1 or n // tn > 1) else 1)
                cost = (steps * MM_STEP_US + (a_reads + b_reads) / HBM_BYTES_PER_US
                        + (m * n * nk / 1024 / VREG_RMW_PER_US if nk > 1 else 0.0))
                if best is None or cost < best[0]:
                    best = (cost, tm, tn, tk)
    assert best is not None, (m, n, k)
    return best[1:]


def _mm_tiles_cached_t(m, n, k, sa, sb, so):
    for tm in _divisor_tiles(m, (1024, 512)):
        if m % tm:
            break
        for tn in _divisor_tiles(n, (1024, 512, 384, 256, 128)):
            if 2 * (k * tm * sa + k * tn * sb + tm * tn * so) + tm * k * 2 <= MM_VMEM_BUDGET:
                return tm, tn
    return None


def _mm_tn_cached_call(a, b, tiles, out_dtype, name):
    k, m = a.shape
    n = b.shape[1]
    tm, tn = tiles

    def body(a_ref, b_ref, o_ref, at_ref):
        @pl.when(pl.program_id(1) == 0)
        def _():
            at_ref[...] = a_ref[...].astype(BF16).T

        o_ref[...] = lax.dot_general(at_ref[...], b_ref[...].astype(BF16), _NN_DIMS,
                                     preferred_element_type=F32).astype(out_dtype)

    return pl.pallas_call(
        body,
        name=name,
        grid=(m // tm, n // tn),
        in_specs=[pl.BlockSpec((k, tm), lambda i, j: (0, i)), pl.BlockSpec((k, tn), lambda i, j: (0, j))],
        out_specs=pl.BlockSpec((tm, tn), lambda i, j: (i, j)),
        out_shape=jax.ShapeDtypeStruct((m, n), out_dtype),
        scratch_shapes=[pltpu.VMEM((tm, k), BF16)],
        compiler_params=_cparams("parallel", "arbitrary"),
    )(a, b)


_NN_DIMS = (((1,), (0,)), ((), ()))


def _mm_call(a, b, ta, tb, add=None, out_dtype=F32, name="mm"):
    m, k = (a.shape[1], a.shape[0]) if ta else a.shape
    k2, n = (b.shape[1], b.shape[0]) if tb else b.shape
    assert k == k2, (a.shape, b.shape, ta, tb)
    sizes = (a.dtype.itemsize, b.dtype.itemsize, jnp.dtype(out_dtype).itemsize + (add.dtype.itemsize if add is not None else 0))
    if ta:
        tiles = _mm_tiles_cached_t(m, n, k, *sizes)
        if tiles is not None:
            return _mm_tn_cached_call(a, b, tiles, out_dtype, name)
    tm, tn, tk = _mm_tiles(m, n, k, *sizes)
    nk = k // tk
    a_spec = pl.BlockSpec((tk, tm), lambda i, j, kk: (kk, i)) if ta else pl.BlockSpec((tm, tk), lambda i, j, kk: (i, kk))
    b_spec = pl.BlockSpec((tn, tk), lambda i, j, kk: (j, kk)) if tb else pl.BlockSpec((tk, tn), lambda i, j, kk: (kk, j))
    o_spec = pl.BlockSpec((tm, tn), lambda i, j, kk: (i, j))
    dn = (((0 if ta else 1,), (1 if tb else 0,)), ((), ()))
    has_add = add is not None

    def body(*refs):
        a_ref, b_ref = refs[0], refs[1]
        c_ref = refs[2] if has_add else None
        o_ref = refs[3] if has_add else refs[2]
        p = lax.dot_general(a_ref[...].astype(BF16), b_ref[...].astype(BF16), dn, preferred_element_type=F32)

        def finish(r):
            if has_add:
                r = r + c_ref[...].astype(F32)
            o_ref[...] = r.astype(out_dtype)

        if nk == 1:
            finish(p)
        else:
            acc_ref = refs[-1]
            kk = pl.program_id(2)

            @pl.when(kk == 0)
            def _():
                acc_ref[...] = p

            @pl.when(kk > 0)
            def _():
                acc_ref[...] += p

            @pl.when(kk == nk - 1)
            def _():
                finish(acc_ref[...])

    in_specs = [a_spec, b_spec] + ([o_spec] if has_add else [])
    args = (a, b) + ((add,) if has_add else ())
    return pl.pallas_call(
        body,
        name=name,
        grid=(m // tm, n // tn, nk),
        in_specs=in_specs,
        out_specs=o_spec,
        out_shape=jax.ShapeDtypeStruct((m, n), out_dtype),
        scratch_shapes=[] if nk == 1 else [pltpu.VMEM((tm, tn), F32)],
        compiler_params=_cparams("parallel", "parallel", "arbitrary"),
    )(*args)


@functools.partial(jax.custom_vjp, nondiff_argnums=(2,))
def matmul(a, b, out_dtype=F32):
    return _mm_call(a, b, False, False, out_dtype=out_dtype, name="mm_fwd")


def _matmul_fwd(a, b, out_dtype):
    return matmul(a, b, out_dtype), (a, b)


def _matmul_bwd(out_dtype, res, g):
    a, b = res
    da = _mm_call(g, b, False, True, out_dtype=a.dtype, name="mm_da")
    db = _mm_call(a, g, True, False, out_dtype=b.dtype, name="mm_db")
    return da, db


matmul.defvjp(_matmul_fwd, _matmul_bwd)


@jax.custom_vjp
def matmul_add(a, b, c):
    return _mm_call(a, b, False, False, add=c, name="mm_add_fwd")


def _matmul_add_fwd(a, b, c):
    return _mm_call(a, b, False, False, add=c, name="mm_add_fwd"), (a, b)


def _matmul_add_bwd(res, g):
    a, b = res
    da = _mm_call(g, b, False, True, out_dtype=a.dtype, name="mm_da")
    db = _mm_call(a, g, True, False, out_dtype=b.dtype, name="mm_db")
    return da, db, g


matmul_add.defvjp(_matmul_add_fwd, _matmul_add_bwd)


def rowwise(f, n_rows, n_aux, tile, name, passthrough=False):
    def specs(arrs, tiled):
        out = []
        for x in arrs:
            if tiled:
                out.append(pl.BlockSpec((tile, x.shape[1]), lambda i: (i, 0)))
            else:
                out.append(pl.BlockSpec(x.shape, lambda i: (0, 0)))
        return out

    def tile_structs(args):
        rows_aux, params = args[: n_rows + n_aux], args[n_rows + n_aux:]
        return [jax.ShapeDtypeStruct((tile, x.shape[1]), x.dtype) for x in rows_aux] + [
            jax.ShapeDtypeStruct(p.shape, p.dtype) for p in params]

    def fwd_call(*args):
        s = args[0].shape[0]
        outs = jax.eval_shape(f, *tile_structs(args))
        n_in = len(args)

        def body(*refs):
            vals = [r[...] for r in refs[:n_in]]
            res = f(*vals)
            for o_ref, r in zip(refs[n_in:], res):
                o_ref[...] = r.astype(o_ref.dtype)

        return pl.pallas_call(
            body,
            name=name + "_fwd",
            grid=(s // tile,),
            in_specs=specs(args[: n_rows + n_aux], True) + specs(args[n_rows + n_aux:], False),
            out_specs=[pl.BlockSpec((tile, o.shape[1]), lambda i: (i, 0)) for o in outs],
            out_shape=[jax.ShapeDtypeStruct((s, o.shape[1]), o.dtype) for o in outs],
            compiler_params=_cparams("parallel"),
        )(*args)

    def bwd_call(args, gs):
        s = args[0].shape[0]
        rows, aux, params = args[:n_rows], args[n_rows:n_rows + n_aux], args[n_rows + n_aux:]
        n_in, n_g, n_p = len(args), len(gs), len(params)
        n_gf = n_g - 1 if passthrough else n_g

        def body(*refs):
            vals = [r[...] for r in refs[:n_in]]
            gvals = tuple(r[...] for r in refs[n_in:n_in + n_gf])
            out_refs = refs[n_in + n_g:]
            auxv = vals[n_rows:n_rows + n_aux]

            def g_(*rp):
                return tuple(f(*rp[:n_rows], *auxv, *rp[n_rows:]))

            _, vjp = jax.vjp(g_, *vals[:n_rows], *vals[n_rows + n_aux:])
            cts = list(vjp(gvals))
            if passthrough:
                cts[0] = cts[0] + refs[n_in + n_gf][...]
            for o_ref, ct in zip(out_refs[:n_rows], cts[:n_rows]):
                o_ref[...] = ct.astype(o_ref.dtype)
            if n_p:
                @pl.when(pl.program_id(0) == 0)
                def _():
                    for o_ref in out_refs[n_rows:]:
                        o_ref[...] = jnp.zeros_like(o_ref)

                for o_ref, ct in zip(out_refs[n_rows:], cts[n_rows:]):
                    o_ref[...] += ct.astype(o_ref.dtype)

        return pl.pallas_call(
            body,
            name=name + "_bwd",
            grid=(s // tile,),
            in_specs=specs(rows + aux, True) + specs(params, False) + specs(gs, True),
            out_specs=specs(rows, True) + specs(params, False),
            out_shape=[jax.ShapeDtypeStruct(x.shape, x.dtype) for x in rows + params],
            compiler_params=_cparams("arbitrary" if n_p else "parallel"),
        )(*args, *gs)

    @jax.custom_vjp
    def op(*args):
        return tuple(fwd_call(*args)) + ((args[0],) if passthrough else ())

    def op_fwd(*args):
        return op(*args), args

    def op_bwd(args, gs):
        cts = bwd_call(tuple(args), tuple(gs))
        rows_ct, par_ct = cts[:n_rows], cts[n_rows:]
        aux_ct = [jnp.zeros_like(a) for a in args[n_rows:n_rows + n_aux]]
        return tuple(rows_ct) + tuple(aux_ct) + tuple(par_ct)

    op.defvjp(op_fwd, op_bwd)
    return op


SCAN_SEGMENTS = SUBLANES


def _scan_step(ar, ai, xr, xi, br, bi):
    return ar * xr - ai * xi + br, ar * xi + ai * xr + bi


_NT_DIMS = (((1,), (1,)), ((), ()))
_TN_DIMS = (((0,), (0,)), ((), ()))


S5_GROUPS_PER_STEP = 64
S5_STATE_LANES = 512


def _s5_interleave(src_ref, scr_ref, rows):
    for k in range(SCAN_SEGMENTS):
        scr_ref[pl.ds(k, rows, stride=SCAN_SEGMENTS), :] = src_ref[k].astype(F32)


def _s5_deinterleave(val, scr_ref, dst_ref, rows):
    scr_ref[...] = val
    for k in range(SCAN_SEGMENTS):
        dst_ref[k] = scr_ref[pl.ds(k, rows, stride=SCAN_SEGMENTS), :]


def _s5_segment_starts(a_ref, fr_ref, fi_ref, sr, si, seg_len, order):
    tn = sr.shape[1]
    pr, pi = a_ref[0:1, :], a_ref[1:2, :]
    for _ in range(seg_len.bit_length() - 1):
        pr, pi = pr * pr - pi * pi, 2.0 * pr * pi
    cr = jnp.zeros((1, tn), F32)
    ci = jnp.zeros((1, tn), F32)
    for idx, k in enumerate(order):
        if idx > 0:
            kp = order[idx - 1]
            cr, ci = (fr_ref[kp:kp + 1, :] + pr * cr - pi * ci, fi_ref[kp:kp + 1, :] + pr * ci + pi * cr)
        sr[k:k + 1, :] = cr
        si[k:k + 1, :] = ci


def _s5_pass_call(src, w_r, w_i, a, transpose_w, reverse, finals, extra, name):
    s = src.shape[0]
    nb = w_r.shape[0]
    n = nb * S5_STATE_LANES
    seg_len = s // SCAN_SEGMENTS
    ti = min(S5_GROUPS_PER_STEP, seg_len)
    nt = seg_len // ti
    tr = SCAN_SEGMENTS * ti
    tn = S5_STATE_LANES
    assert seg_len & (seg_len - 1) == 0
    order = list(range(SCAN_SEGMENTS))[::-1] if reverse else list(range(SCAN_SEGMENTS))
    dn_in = _NT_DIMS if transpose_w else _NN_DIMS
    first = finals is None
    backward = (not first) and reverse
    forward = (not first) and not reverse
    tmap3 = (lambda c, j: (0, nt - 1 - j, c)) if reverse else (lambda c, j: (0, j, c))
    tmap2 = (lambda c, j: (nt - 1 - j, c)) if reverse else (lambda c, j: (j, c))
    bf = lambda v: v.astype(BF16)

    def body(*refs):
        it = iter(refs)
        src_ref, wr_ref, wi_ref, a_ref = next(it), next(it), next(it), next(it)
        if not first:
            fr_ref, fi_ref = next(it), next(it)
        if forward:
            cdr_ref, cdi_ref = next(it), next(it)
            xr_out, xi_out, y_ref = next(it), next(it), next(it)
        if backward:
            xr_ref, xi_ref, u_ref, bdr_ref, bdi_ref = next(it), next(it), next(it), next(it), next(it)
            du_ref, dar_ref, dai_ref, dbr_ref, dbi_ref, dcr_ref, dci_ref = (next(it) for _ in range(7))
        if first:
            fr_out, fi_out = next(it), next(it)
        sr, si, in_scr, dr_scr, di_scr = next(it), next(it), next(it), next(it), next(it)
        if backward:
            accr, acci, u_scr = next(it), next(it), next(it)
        j = pl.program_id(1)

        @pl.when(j == 0)
        def _():
            if first:
                sr[...] = jnp.zeros_like(sr)
                si[...] = jnp.zeros_like(si)
            else:
                _s5_segment_starts(a_ref, fr_ref, fi_ref, sr, si, seg_len, order)
            if backward:
                for r in (accr, acci, dbr_ref, dbi_ref, dcr_ref, dci_ref):
                    r[...] = jnp.zeros_like(r)

        _s5_interleave(src_ref, in_scr, ti)
        src_b = bf(in_scr[...])
        dr_scr[...] = lax.dot_general(src_b, bf(wr_ref[0]), dn_in, preferred_element_type=F32)
        di_scr[...] = lax.dot_general(src_b, bf(wi_ref[0]), dn_in, preferred_element_type=F32)
        ar = jnp.broadcast_to(a_ref[0:1, :], (SUBLANES, tn))
        ai = jnp.broadcast_to(a_ref[1:2, :], (SUBLANES, tn))

        def step(ii, carry):
            i = (ti - 1 - ii) if reverse else ii
            rows = pl.ds(pl.multiple_of(i * SUBLANES, SUBLANES), SUBLANES)
            xr, xi = carry[0], carry[1]
            if backward:
                zr, zi = xr_ref[rows, :], xi_ref[rows, :]
                acc = (carry[2] + xr * zr + xi * zi, carry[3] + xi * zr - xr * zi)
            nr, ni = _scan_step(ar, ai, xr, xi, dr_scr[rows, :], di_scr[rows, :])
            if forward:
                xr_out[rows, :] = nr
                xi_out[rows, :] = ni
            if backward:
                dr_scr[rows, :] = nr
                di_scr[rows, :] = ni
            return (nr, ni) + (acc if backward else ())

        init = (sr[...], si[...]) + ((accr[...], acci[...]) if backward else ())
        out = lax.fori_loop(0, ti, step, init, unroll=4)
        sr[...] = out[0]
        si[...] = out[1]
        if first:
            @pl.when(j == nt - 1)
            def _():
                fr_out[...] = out[0]
                fi_out[...] = out[1]
        if forward:
            y = (lax.dot_general(bf(xr_out[...]), bf(cdr_ref[0]), _NN_DIMS, preferred_element_type=F32)
                 + lax.dot_general(bf(xi_out[...]), bf(cdi_ref[0]), _NN_DIMS, preferred_element_type=F32))
            _s5_deinterleave(y, in_scr, y_ref, ti)
        if backward:
            accr[...] = out[2]
            acci[...] = out[3]
            g_r, g_i = bf(dr_scr[...]), bf(di_scr[...])
            dcr_ref[0] += lax.dot_general(bf(xr_ref[...]), src_b, _TN_DIMS, preferred_element_type=F32)
            dci_ref[0] += lax.dot_general(bf(xi_ref[...]), src_b, _TN_DIMS, preferred_element_type=F32)
            _s5_interleave(u_ref, u_scr, ti)
            u_b = bf(u_scr[...])
            dbr_ref[0] += lax.dot_general(u_b, g_r, _TN_DIMS, preferred_element_type=F32)
            dbi_ref[0] += lax.dot_general(u_b, g_i, _TN_DIMS, preferred_element_type=F32)
            du = (lax.dot_general(g_r, bf(bdr_ref[0]), _NT_DIMS, preferred_element_type=F32)
                  + lax.dot_general(g_i, bf(bdi_ref[0]), _NT_DIMS, preferred_element_type=F32))
            _s5_deinterleave(du, u_scr, du_ref, ti)

            @pl.when(j == nt - 1)
            def _():
                dar_ref[...] = jnp.sum(out[2], axis=0, keepdims=True)
                dai_ref[...] = jnp.sum(out[3], axis=0, keepdims=True)

    view3 = lambda t: t.reshape(SCAN_SEGMENTS, seg_len, t.shape[1])
    spec3 = pl.BlockSpec((SCAN_SEGMENTS, ti, LANES), tmap3)
    wspec = lambda w: pl.BlockSpec((1,) + w.shape[1:], lambda c, j: (c, 0, 0))
    aspec = pl.BlockSpec((2, tn), lambda c, j: (0, c))
    fspec = pl.BlockSpec((SUBLANES, tn), lambda c, j: (0, c))
    xspec = pl.BlockSpec((tr, tn), tmap2)
    dspec = pl.BlockSpec((1, tn), lambda c, j: (0, c))
    f32 = lambda *shape: jax.ShapeDtypeStruct(shape, F32)
    args, in_specs = [view3(src), w_r, w_i, a], [spec3, wspec(w_r), wspec(w_i), aspec]
    if not first:
        args += list(finals)
        in_specs += [fspec, fspec]
    if forward:
        args += list(extra)
        in_specs += [wspec(extra[0]), wspec(extra[1])]
        out_specs, out_shape = [xspec, xspec, spec3], [f32(s, n), f32(s, n), f32(SCAN_SEGMENTS, seg_len, nb * LANES)]
    elif backward:
        x_r, x_i, u, bd_r, bd_i = extra
        args += [x_r, x_i, view3(u), bd_r, bd_i]
        in_specs += [xspec, xspec, spec3, wspec(bd_r), wspec(bd_i)]
        out_specs = [spec3, dspec, dspec, wspec(bd_r), wspec(bd_i), wspec(w_r), wspec(w_i)]
        out_shape = [f32(SCAN_SEGMENTS, seg_len, nb * LANES), f32(1, n), f32(1, n), f32(*bd_r.shape), f32(*bd_i.shape),
                     f32(*w_r.shape), f32(*w_i.shape)]
    else:
        out_specs, out_shape = [fspec, fspec], [f32(SUBLANES, n), f32(SUBLANES, n)]
    scratch = ([pltpu.VMEM((SUBLANES, tn), F32)] * 2 + [pltpu.VMEM((tr, LANES), F32)] + [pltpu.VMEM((tr, tn), F32)] * 2
               + ([pltpu.VMEM((SUBLANES, tn), F32)] * 2 + [pltpu.VMEM((tr, LANES), F32)] if backward else []))
    return pl.pallas_call(
        body, name=name, grid=(nb, nt), in_specs=in_specs, out_specs=out_specs, out_shape=out_shape,
        scratch_shapes=scratch, compiler_params=_cparams("parallel", "arbitrary"),
    )(*args)


@jax.custom_vjp
def s5_core(u, a, bd_r, bd_i, cd_r, cd_i):
    return _s5_core_fwd(u, a, bd_r, bd_i, cd_r, cd_i)[0]


def _s5_core_fwd(u, a, bd_r, bd_i, cd_r, cd_i):
    fin = _s5_pass_call(u, bd_r, bd_i, a, False, False, None, None, "s5_fwd_finals")
    x_r, x_i, y = _s5_pass_call(u, bd_r, bd_i, a, False, False, fin, (cd_r, cd_i), "s5_fwd_scan")
    return y.reshape(u.shape), (u, a, bd_r, bd_i, cd_r, cd_i, x_r, x_i)


def _s5_core_bwd(res, dy):
    u, a, bd_r, bd_i, cd_r, cd_i, x_r, x_i = res
    a_conj = a * jnp.array([[1.0], [-1.0]], F32)
    fin = _s5_pass_call(dy, cd_r, cd_i, a_conj, True, True, None, None, "s5_bwd_finals")
    du, da_r, da_i, dbd_r, dbd_i, dcd_r, dcd_i = _s5_pass_call(
        dy, cd_r, cd_i, a_conj, True, True, fin, (x_r, x_i, u, bd_r, bd_i), "s5_bwd_scan")
    return du.reshape(u.shape), jnp.concatenate([da_r, da_i], axis=0), dbd_r, dbd_i, dcd_r, dcd_i


s5_core.defvjp(_s5_core_fwd, _s5_core_bwd)


_NN = (((1,), (0,)), ((), ()))
_NT = (((1,), (1,)), ((), ()))
_TN = (((0,), (0,)), ((), ()))


def _dot(a, b, dn):
    return lax.dot_general(a.astype(BF16), b.astype(BF16), dn, preferred_element_type=F32)


@jax.custom_vjp
def bdot_nn(a, b):
    return _dot(a, b, _NN)


bdot_nn.defvjp(lambda a, b: (_dot(a, b, _NN), (a, b)),
               lambda r, g: (_dot(g, r[1], _NT).astype(r[0].dtype), _dot(r[0], g, _TN).astype(r[1].dtype)))


@jax.custom_vjp
def bdot_nt(a, b):
    return _dot(a, b, _NT)


bdot_nt.defvjp(lambda a, b: (_dot(a, b, _NT), (a, b)),
               lambda r, g: (_dot(g, r[1], _NN).astype(r[0].dtype), _dot(g, r[0], _TN).astype(r[1].dtype)))


@jax.custom_vjp
def bdot_tn(a, b):
    return _dot(a, b, _TN)


bdot_tn.defvjp(lambda a, b: (_dot(a, b, _TN), (a, b)),
               lambda r, g: (_dot(r[1], g, _NT).astype(r[0].dtype), _dot(r[0], g, _NN).astype(r[1].dtype)))


def _split3(x):
    h = x.astype(BF16)
    r = x - h.astype(F32)
    m = r.astype(BF16)
    l = (r - m.astype(F32)).astype(BF16)
    return h, m, l


def _exact_dot(t, x, dn):
    h, m, l = _split3(x)
    d = lambda p: lax.dot_general(t, p, dn, preferred_element_type=F32)
    return d(h) + d(m) + d(l)


@jax.custom_vjp
def select_dot(t, x):
    return _exact_dot(t, x, _NN)


select_dot.defvjp(lambda t, x: (_exact_dot(t, x, _NN), t),
                  lambda t, g: (jnp.zeros_like(t), _exact_dot(t, g, _TN)))


def _split_rows_impl(x, h):
    return tuple(x[i * h:(i + 1) * h] for i in range(x.shape[0] // h))


@functools.partial(jax.custom_vjp, nondiff_argnums=(1,))
def split_rows(x, h):
    return _split_rows_impl(x, h)


split_rows.defvjp(lambda x, h: (_split_rows_impl(x, h), None),
                  lambda h, r, g: (jnp.concatenate(g, axis=0),))


@jax.custom_vjp
def join_rows(parts):
    return jnp.concatenate(parts, axis=0)


def _join_rows_bwd(hs, g):
    out, off = [], 0
    for h in hs:
        out.append(g[off:off + h])
        off += h
    return (tuple(out),)


join_rows.defvjp(lambda parts: (jnp.concatenate(parts, axis=0), tuple(p.shape[0] for p in parts)), _join_rows_bwd)


def _split_lanes_impl(x, w):
    return tuple(x[:, i * w:(i + 1) * w] for i in range(x.shape[1] // w))


@functools.partial(jax.custom_vjp, nondiff_argnums=(1,))
def split_lanes(x, w):
    return _split_lanes_impl(x, w)


split_lanes.defvjp(lambda x, w: (_split_lanes_impl(x, w), None),
                   lambda w, r, g: (jnp.concatenate(g, axis=1),))


def _join_impl(parts):
    return jnp.concatenate(parts, axis=1)


@jax.custom_vjp
def join_lanes(parts):
    return _join_impl(parts)


def _join_bwd(ws, g):
    out, off = [], 0
    for w in ws:
        out.append(g[:, off:off + w])
        off += w
    return (tuple(out),)


join_lanes.defvjp(lambda parts: (_join_impl(parts), tuple(p.shape[1] for p in parts)), _join_bwd)


def _rope_impl(x, c, sa, sb, shift):
    w = x.shape[1]
    return x * c + pltpu.roll(x, w - shift, 1) * sa + pltpu.roll(x, shift, 1) * sb


@functools.partial(jax.custom_vjp, nondiff_argnums=(4,))
def rope_lanes(x, c, sa, sb, shift):
    return _rope_impl(x, c, sa, sb, shift)


def _rope_bwd(shift, r, g):
    c, sa, sb = r
    w = g.shape[1]
    dx = g * c + pltpu.roll(g * sa, shift, 1) + pltpu.roll(g * sb, w - shift, 1)
    return dx, jnp.zeros_like(c), jnp.zeros_like(sa), jnp.zeros_like(sb)


rope_lanes.defvjp(lambda x, c, sa, sb, shift: (_rope_impl(x, c, sa, sb, shift), (c, sa, sb)), _rope_bwd)


RMS_EPS = 1e-6


def _rms(x, g):
    return x * lax.rsqrt(jnp.mean(x * x, axis=-1, keepdims=True) + RMS_EPS) * g


ATTN_BLOCK = 512
MASK_VALUE = -1e30
LOG2E = math.log2(math.e)
LN2 = math.log(2.0)
V_ONES_LANE = 64


def _causal_mask(t):
    r = lax.broadcasted_iota(jnp.int32, (t, t), 0)
    c = lax.broadcasted_iota(jnp.int32, (t, t), 1)
    return c <= r


def _attn_fwd_call(q, k, v):
    s, width = q.shape
    n_heads = width // LANES
    tq = min(ATTN_BLOCK, s)
    nq = s // tq

    def body(q_ref, k_ref, v_ref, o_ref, lse_ref):
        i = pl.program_id(1)
        qb = q_ref[...].astype(BF16)
        ones_lane = lax.broadcasted_iota(jnp.int32, (tq, LANES), 1) == V_ONES_LANE

        def block(kb, carry, masked):
            m, acc = carry
            rows = pl.ds(pl.multiple_of(kb * tq, tq), tq)
            sc = lax.dot_general(qb, k_ref[rows, :].astype(BF16), _NT, preferred_element_type=F32)
            if masked:
                sc = jnp.where(_causal_mask(tq), sc, MASK_VALUE)
            m_new = jnp.maximum(m, jnp.max(sc, axis=-1, keepdims=True))
            p = jnp.exp2(sc - m_new).astype(BF16)
            vb = jnp.where(ones_lane, 1.0, v_ref[rows, :]).astype(BF16)
            acc = jnp.exp2(m - m_new) * acc + lax.dot_general(p, vb, _NN, preferred_element_type=F32)
            return m_new, acc

        init = (jnp.full((tq, 1), MASK_VALUE, F32), jnp.zeros((tq, LANES), F32))
        carry = lax.fori_loop(0, i, lambda kb, c: block(kb, c, False), init)
        m, acc = block(i, carry, True)
        l = jnp.sum(jnp.where(ones_lane, acc, 0.0), axis=-1, keepdims=True)
        o_ref[...] = jnp.where(ones_lane, 0.0, acc / l).astype(o_ref.dtype)
        lse_ref[...] = jnp.broadcast_to(m + jnp.log2(l), (tq, LANES))

    qspec = pl.BlockSpec((tq, LANES), lambda h, i: (i, h))
    kspec = pl.BlockSpec((s, LANES), lambda h, i: (0, h))
    return pl.pallas_call(
        body,
        name="mla_attn_fwd",
        grid=(n_heads, nq),
        in_specs=[qspec, kspec, kspec],
        out_specs=[qspec, qspec],
        out_shape=[jax.ShapeDtypeStruct((s, width), BF16), jax.ShapeDtypeStruct((s, width), F32)],
        compiler_params=_cparams("parallel", "parallel"),
    )(q, k, v)


def _attn_bwd_call(q, k, v, o, lse, do):
    s, width = q.shape
    n_heads = width // LANES
    tq = min(ATTN_BLOCK, s)
    nq = s // tq

    def body(q_ref, k_ref, v_ref, o_ref, lse_ref, do_ref, dq_ref, dk_ref, dv_ref, dq_acc):
        j = pl.program_id(1)

        @pl.when(j == 0)
        def _():
            dq_acc[...] = jnp.zeros_like(dq_acc)

        kb = k_ref[...].astype(BF16)
        vb = v_ref[...].astype(BF16)

        def block(i, carry, masked):
            dk, dv = carry
            rows = pl.ds(pl.multiple_of(i * tq, tq), tq)
            qi = q_ref[rows, :].astype(BF16)
            doi = do_ref[rows, :].astype(F32)
            delta = jnp.sum(doi * o_ref[rows, :].astype(F32), axis=-1, keepdims=True)
            sc = lax.dot_general(qi, kb, _NT, preferred_element_type=F32)
            if masked:
                sc = jnp.where(_causal_mask(tq), sc, MASK_VALUE)
            p = jnp.exp2(sc - lse_ref[rows, 0:1])
            dob = doi.astype(BF16)
            dv = dv + lax.dot_general(p.astype(BF16), dob, _TN, preferred_element_type=F32)
            dp = lax.dot_general(dob, vb, _NT, preferred_element_type=F32)
            ds = (p * (dp - delta)).astype(BF16)
            dq_acc[rows, :] += lax.dot_general(ds, kb, _NN, preferred_element_type=F32)
            dk = dk + lax.dot_general(ds, qi, _TN, preferred_element_type=F32)
            return dk, dv

        zero = jnp.zeros((tq, LANES), F32)
        carry = block(j, (zero, zero), True)
        dk, dv = lax.fori_loop(j + 1, nq, lambda i, c: block(i, c, False), carry)
        dk_ref[...] = (dk * LN2).astype(dk_ref.dtype)
        dv_ref[...] = dv.astype(dv_ref.dtype)

        @pl.when(j == nq - 1)
        def _():
            dq_ref[...] = (dq_acc[...] * LN2).astype(dq_ref.dtype)

    full = pl.BlockSpec((s, LANES), lambda h, j: (0, h))
    blk = pl.BlockSpec((tq, LANES), lambda h, j: (j, h))
    return pl.pallas_call(
        body,
        name="mla_attn_bwd",
        grid=(n_heads, nq),
        in_specs=[full, blk, blk, full, full, full],
        out_specs=[full, blk, blk],
        out_shape=[jax.ShapeDtypeStruct((s, width), t.dtype) for t in (q, k, v)],
        scratch_shapes=[pltpu.VMEM((s, LANES), F32)],
        compiler_params=_cparams("parallel", "arbitrary"),
    )(q, k, v, o, lse, do)


@jax.custom_vjp
def causal_attention(q, k, v):
    return _attn_fwd_call(q, k, v)[0]


def _causal_attention_fwd(q, k, v):
    o, lse = _attn_fwd_call(q, k, v)
    return o, (q, k, v, o, lse)


def _causal_attention_bwd(res, do):
    return tuple(_attn_bwd_call(*res, do))


causal_attention.defvjp(_causal_attention_fwd, _causal_attention_bwd)


HG_HEADS = 4
HG_CHUNK = 32
HG_REF_ROW = HG_CHUNK // 2 - 1
HG_TILE_ROWS = 256
HG_EXP_CLAMP = 80.0


def _hg_tile_masks(t):
    shift = HG_CHUNK.bit_length() - 1
    r = lax.broadcasted_iota(jnp.int32, (t, t), 0)
    c = lax.broadcasted_iota(jnp.int32, (t, t), 1)
    start = lax.shift_left(lax.shift_right_logical(r, shift), shift)
    in_chunk = (c >= start) & (c < start + HG_CHUNK)
    return in_chunk & (c <= r), in_chunk & (c <= start + HG_REF_ROW), in_chunk


def _hg_tile(q, fl, v, lb, st):
    t = q.shape[0]
    causal, to_ref, whole = _hg_tile_masks(t)
    f = lb + (1.0 - lb) * jax.nn.sigmoid(fl)
    kk = 1.0 - f
    qs = q * jax.nn.sigmoid(q)
    sums = jnp.concatenate([m.astype(BF16) for m in (causal, to_ref, whole)], axis=0)
    b, b_ref, b_last = split_rows(select_dot(sums, jnp.log(f)), t)
    q_in = qs * jnp.exp(jnp.minimum(b - b_ref, HG_EXP_CLAMP))
    k_in = kk * jnp.exp(jnp.minimum(b_ref - b, HG_EXP_CLAMP))
    o = bdot_nn(jnp.where(causal, bdot_nt(q_in, k_in), 0.0), v)
    q_hat = split_rows(qs * jnp.exp(b), HG_CHUNK)
    k_hat = split_rows(kk * jnp.exp(b_last - b), HG_CHUNK)
    decay = split_rows(jnp.exp(b_last), HG_CHUNK)
    vs = split_rows(v, HG_CHUNK)
    first_row = lax.broadcasted_iota(jnp.int32, (HG_CHUNK, LANES), 0) == 0
    inter = []
    for c in range(t // HG_CHUNK):
        inter.append(bdot_nt(q_hat[c], st))
        st = st * jnp.sum(jnp.where(first_row, decay[c], 0.0), axis=0, keepdims=True) + bdot_tn(vs[c], k_hat[c])
    return o + join_rows(tuple(inter)), st


def _hg_head(q, fl, v, gate, lb, gn, st):
    o, st = _hg_tile(q, fl, v, lb, st)
    return _rms(o, gn) * (gate * jax.nn.sigmoid(gate)), st


HG_PARTS = 4


def _hg_part_slices(h, width):
    return [slice(p * width + h * LANES, p * width + (h + 1) * LANES) for p in range(HG_PARTS)]


def _hg_fwd_call(x, lb, gn):
    s = x.shape[0]
    width = x.shape[1] // HG_PARTS
    tr = min(HG_TILE_ROWS, s)
    nt = s // tr

    def body(x_ref, lb_ref, gn_ref, o_ref, sts_ref, st_ref):
        @pl.when(pl.program_id(0) == 0)
        def _():
            st_ref[...] = jnp.zeros_like(st_ref)

        for h in range(HG_HEADS):
            ln = slice(h * LANES, (h + 1) * LANES)
            st = st_ref[h]
            sts_ref[0, h] = st
            o, st_new = _hg_head(*(x_ref[:, sl] for sl in _hg_part_slices(h, width)), lb_ref[:, ln], gn_ref[...], st)
            o_ref[:, ln] = o.astype(o_ref.dtype)
            st_ref[h] = st_new

    const = lambda shape: pl.BlockSpec(shape, lambda j: (0, 0))
    return pl.pallas_call(
        body,
        name="hgrn2_fwd",
        grid=(nt,),
        in_specs=[pl.BlockSpec((tr, HG_PARTS * width), lambda j: (j, 0)), const((1, width)), const((1, LANES))],
        out_specs=[pl.BlockSpec((tr, width), lambda j: (j, 0)),
                   pl.BlockSpec((1, HG_HEADS, LANES, LANES), lambda j: (j, 0, 0, 0))],
        out_shape=[jax.ShapeDtypeStruct((s, width), BF16),
                   jax.ShapeDtypeStruct((nt, HG_HEADS, LANES, LANES), F32)],
        scratch_shapes=[pltpu.VMEM((HG_HEADS, LANES, LANES), F32)],
        compiler_params=_cparams("arbitrary"),
    )(x, lb, gn)


def _hg_bwd_call(x, lb, gn, sts, do):
    s = x.shape[0]
    width = x.shape[1] // HG_PARTS
    tr = min(HG_TILE_ROWS, s)
    nt = s // tr

    def body(x_ref, lb_ref, gn_ref, sts_ref, do_ref, dx_ref, dlb_ref, dgn_ref, dst_ref):
        @pl.when(pl.program_id(0) == 0)
        def _():
            dst_ref[...] = jnp.zeros_like(dst_ref)
            dlb_ref[...] = jnp.zeros_like(dlb_ref)
            dgn_ref[...] = jnp.zeros_like(dgn_ref)

        for h in range(HG_HEADS):
            ln = slice(h * LANES, (h + 1) * LANES)
            parts = _hg_part_slices(h, width)
            _, vjp = jax.vjp(_hg_head, *(x_ref[:, sl] for sl in parts), lb_ref[:, ln], gn_ref[...], sts_ref[0, h])
            cts = vjp((do_ref[:, ln].astype(F32), dst_ref[h]))
            for sl, ct in zip(parts, cts[:HG_PARTS]):
                dx_ref[:, sl] = ct.astype(dx_ref.dtype)
            dlb_ref[:, ln] += cts[HG_PARTS]
            dgn_ref[...] += cts[HG_PARTS + 1]
            dst_ref[h] = cts[HG_PARTS + 2]

    rev = lambda w: pl.BlockSpec((tr, w), lambda j: (nt - 1 - j, 0))
    const = lambda shape: pl.BlockSpec(shape, lambda j: (0, 0))
    return pl.pallas_call(
        body,
        name="hgrn2_bwd",
        grid=(nt,),
        in_specs=[rev(HG_PARTS * width), const((1, width)), const((1, LANES)),
                  pl.BlockSpec((1, HG_HEADS, LANES, LANES), lambda j: (nt - 1 - j, 0, 0, 0)), rev(width)],
        out_specs=[rev(HG_PARTS * width), const((1, width)), const((1, LANES))],
        out_shape=[jax.ShapeDtypeStruct(x.shape, BF16), jax.ShapeDtypeStruct((1, width), F32),
                   jax.ShapeDtypeStruct((1, LANES), F32)],
        scratch_shapes=[pltpu.VMEM((HG_HEADS, LANES, LANES), F32)],
        compiler_params=_cparams("arbitrary"),
    )(x, lb, gn, sts, do)


@jax.custom_vjp
def hgrn2_mixer(h, w, lb, gn):
    return _hg_fwd_call(_mm_call(h, w, False, False, name="hgrn2_proj"), lb, gn)[0]


def _hgrn2_mixer_fwd(h, w, lb, gn):
    x = _mm_call(h, w, False, False, name="hgrn2_proj")
    o, sts = _hg_fwd_call(x, lb, gn)
    return o, (h, w, x, lb, gn, sts)


def _hgrn2_mixer_bwd(res, do):
    h, w, x, lb, gn, sts = res
    dx, dlb, dgn = _hg_bwd_call(x, lb, gn, sts, do)
    dh = _mm_call(dx, w, False, True, out_dtype=h.dtype, name="hgrn2_proj_da")
    dw = _mm_call(h, dx, True, False, out_dtype=w.dtype, name="hgrn2_proj_db")
    return dh, dw, dlb, dgn


hgrn2_mixer.defvjp(_hgrn2_mixer_fwd, _hgrn2_mixer_bwd)


D_MODEL = 1024
DEPTH = 2
SSM_GROUPS, SSM_GROUP_CH, SSM_STATE = 32, 16, 64
SSM_WIDTH = SSM_GROUPS * SSM_GROUP_CH
MLA_HEADS, MLA_NOPE, MLA_ROPE, MLA_V = 8, 64, 32, 64
MLA_Q_RANK, MLA_KV_RANK = 512, 256
HG_WIDTH = HG_HEADS * LANES
X_HEADS, X_HEAD_DIM = 4, 128
X_WIDTH = X_HEADS * X_HEAD_DIM
D_FF = 2816
ROPE_THETA = 10000.0
IN_SPLITS = (SSM_WIDTH, MLA_Q_RANK, MLA_KV_RANK, MLA_ROPE, HG_WIDTH, HG_WIDTH, HG_WIDTH, HG_WIDTH, 3 * D_MODEL)
ROPE_LANE0 = MLA_NOPE
MLA_Q_SCALE = LOG2E / math.sqrt(MLA_NOPE + MLA_ROPE)
ROW_TILE = 512
MEM_ROW_TILE = 256


def _t_rms(x, g):
    return (_rms(x, g).astype(BF16),)


def _t_s5_act(y, u, d):
    return (jax.nn.gelu(y + d * u).astype(BF16),)


def _t_glu(z):
    zo, zg = split_lanes(z.astype(F32), D_MODEL)
    return ((zo * jax.nn.sigmoid(zg)).astype(BF16),)


def _t_mla_rope(q, k, kr, c, sa, sb):
    rep = lambda t: jnp.concatenate([t] * MLA_HEADS, axis=1)
    half = MLA_ROPE // 2
    q_out = rope_lanes(q, rep(c), rep(sa), rep(sb), half) * MLA_Q_SCALE
    kr_out = rope_lanes(kr, c, sa, sb, half)
    return q_out.astype(BF16), (k + join_lanes((kr_out,) * MLA_HEADS)).astype(BF16)


def _t_merge(y_ssm, y_mla, y_hg, gates):
    g0, g1, g2 = split_lanes(gates.astype(F32), D_MODEL)
    mix = (jax.nn.sigmoid(g0) * y_ssm.astype(F32) + jax.nn.sigmoid(g1) * y_mla.astype(F32)
           + jax.nn.sigmoid(g2) * y_hg.astype(F32))
    return (mix.astype(BF16),)


def _t_xattn(q, kv):
    scale = 1.0 / math.sqrt(X_HEAD_DIM)
    heads = split_lanes(kv, X_HEAD_DIM)
    outs = []
    for qh, kh, vh in zip(split_lanes(q, X_HEAD_DIM), heads[:X_HEADS], heads[X_HEADS:]):
        sc = bdot_nt(qh, kh) * scale
        p = jnp.exp(sc - jnp.max(sc, axis=-1, keepdims=True))
        p = p / jnp.sum(p, axis=-1, keepdims=True)
        outs.append(bdot_nn(p, vh))
    return (join_lanes(tuple(outs)).astype(BF16),)


def _t_swiglu(gate_up):
    gt, up = split_lanes(gate_up.astype(F32), D_FF)
    return ((gt * jax.nn.sigmoid(gt) * up).astype(BF16),)


def _t_loss(x, tgt, g):
    e = _rms(x, g) - tgt
    return (jnp.broadcast_to(jnp.mean(e * e, axis=-1, keepdims=True), (x.shape[0], LANES)),)


rms_op = rowwise(_t_rms, 1, 0, ROW_TILE, "rmsnorm")
rms_mem_op = rowwise(_t_rms, 1, 0, MEM_ROW_TILE, "rmsnorm_mem")
rms_res_op = rowwise(_t_rms, 1, 0, ROW_TILE, "rmsnorm_res", passthrough=True)
s5_act_op = rowwise(_t_s5_act, 2, 0, ROW_TILE, "s5_act")
glu_op = rowwise(_t_glu, 1, 0, ROW_TILE, "glu")
mla_rope_op = rowwise(_t_mla_rope, 3, 3, ROW_TILE, "mla_rope")
merge_op = rowwise(_t_merge, 4, 0, ROW_TILE, "merge")
xattn_op = rowwise(_t_xattn, 1, 0, ROW_TILE, "xattn")
swiglu_op = rowwise(_t_swiglu, 1, 0, ROW_TILE, "swiglu")
loss_op = rowwise(_t_loss, 1, 1, ROW_TILE, "loss")


def _rope_tables(positions):
    half = MLA_ROPE // 2
    inv_freq = ROPE_THETA ** (-jnp.arange(half, dtype=F32) / half)
    ang = positions.astype(F32)[:, None] * inv_freq
    cos, sin = jnp.cos(ang), jnp.sin(ang)
    s = positions.shape[0]
    z = lambda w: jnp.zeros((s, w), F32)
    tail = LANES - ROPE_LANE0 - MLA_ROPE
    c = jnp.concatenate([jnp.ones((s, ROPE_LANE0), F32), cos, cos, z(tail)], axis=1)
    sa = jnp.concatenate([z(ROPE_LANE0), -sin, z(half), z(tail)], axis=1)
    sb = jnp.concatenate([z(ROPE_LANE0), z(half), sin, z(tail)], axis=1)
    return c, sa, sb


def _s5_operators(lam_re, lam_im, b_re, b_im, c_re, c_im, log_step):
    g, p, h = SSM_GROUPS, SSM_STATE, SSM_GROUP_CH
    lam = lax.complex(lam_re, lam_im)
    lam_bar = jnp.exp(lam * jnp.exp(log_step)[:, None])
    b_bar = ((lam_bar - 1.0) / lam)[..., None] * lax.complex(b_re, b_im)
    per = LANES // h
    nb = g // per
    eye = jnp.eye(per, dtype=F32)
    bd = lambda t: jnp.einsum("jgph,gk->jghkp", t.reshape(nb, per, p, h), eye).reshape(nb, per * h, per * p)
    cd = lambda t: jnp.einsum("jghp,gk->jgpkh", t.reshape(nb, per, h, p), eye).reshape(nb, per * p, per * h)
    a = jnp.stack([jnp.real(lam_bar).reshape(-1), jnp.imag(lam_bar).reshape(-1)])
    return a, bd(jnp.real(b_bar)), bd(jnp.imag(b_bar)), cd(c_re), cd(-c_im)


LATENT_WIDTH = 1536
_LATENT = {}
_off = 0
for _name, _w in (("u", SSM_WIDTH), ("q_lat", MLA_Q_RANK), ("kv_lat", MLA_KV_RANK), ("k_rope", LANES)):
    _LATENT[_name] = (_off, _off + _w)
    _off += _w


def _layer_matrices(w, l):
    return {**_mixer_matrices(w, l), **_tail_matrices(w, l)}


def _tail_matrices(w, l):
    return dict(x_q=w["x_w_q"][l], x_kv=w["x_w_kv"][l], x_o=w["x_w_o"][l], ffn_gu=w["ffn_w_gate_up"][l],
                ffn_d=w["ffn_w_down"][l])


def _mixer_matrices(w, l):
    w_in = w["w_in"][l]
    d, dt = w_in.shape[0], w_in.dtype
    z = lambda n: jnp.zeros((d, n), dt)
    r0 = SSM_WIDTH + MLA_Q_RANK + MLA_KV_RANK
    r1 = r0 + MLA_ROPE
    r2 = r1 + HG_PARTS * HG_WIDTH
    w_latent = jnp.concatenate([w_in[:, :r0], z(ROPE_LANE0), w_in[:, r0:r1],
                                z(LATENT_WIDTH - r0 - ROPE_LANE0 - MLA_ROPE)], axis=1)
    pad_heads = lambda t: jnp.pad(t, ((0, 0), (0, 0), (0, LANES - t.shape[2]))).reshape(t.shape[0], -1)
    uq = w["mla_w_uq"][l].reshape(MLA_Q_RANK, MLA_HEADS, MLA_NOPE + MLA_ROPE)
    ukv = w["mla_w_ukv"][l].reshape(MLA_KV_RANK, MLA_HEADS, MLA_NOPE + MLA_V)
    wo = w["mla_w_o"][l].reshape(MLA_HEADS, MLA_V, D_MODEL)
    return dict(
        w_latent=w_latent, w_hg=w_in[:, r1:r2], w_gates=w_in[:, r2:], glu=w["ssm_w_glu"][l],
        uq=pad_heads(uq), uk=pad_heads(ukv[:, :, :MLA_NOPE]), uv=pad_heads(ukv[:, :, MLA_NOPE:]),
        mla_o=jnp.pad(wo, ((0, 0), (0, LANES - MLA_V), (0, 0))).reshape(MLA_HEADS * LANES, D_MODEL),
        hg_o=w["hg_w_o"][l], w_out=w["w_out"][l])


def _layer(x, mem, tabs, m, sp, l, lower_bound):
    return _tail(_mixer(x, tabs, m, sp, l, lower_bound), mem, m, sp, l)


def _mixer(x, tabs, m, sp, l, lower_bound):
    row = lambda name: sp[name][l].reshape(1, -1)
    h, x = rms_res_op(x, row("norm_mix"))
    latent = matmul(h, m["w_latent"])
    seg = lambda name: latent[:, _LATENT[name][0]:_LATENT[name][1]]
    a, bd_r, bd_i, cd_r, cd_i = _s5_operators(*(sp[n][l] for n in (
        "ssm_lam_re", "ssm_lam_im", "ssm_b_re", "ssm_b_im", "ssm_c_re", "ssm_c_im", "ssm_log_step")))
    u = seg("u")
    y = s5_core(u, a, bd_r, bd_i, cd_r, cd_i)
    (ya,) = s5_act_op(y, u, row("ssm_d"))
    (y_ssm,) = glu_op(matmul(ya, m["glu"], BF16))
    (qn,) = rms_op(seg("q_lat"), row("mla_q_norm"))
    (kvn,) = rms_op(seg("kv_lat"), row("mla_kv_norm"))
    q, k = mla_rope_op(matmul(qn, m["uq"]), matmul(kvn, m["uk"]), seg("k_rope"), *tabs)
    o = causal_attention(q, k, matmul(kvn, m["uv"], BF16))
    y_mla = matmul(o, m["mla_o"], BF16)
    y_hg = matmul(hgrn2_mixer(h, m["w_hg"], lower_bound, row("hg_g_norm")), m["hg_o"], BF16)
    (merged,) = merge_op(y_ssm, y_mla, y_hg, matmul(h, m["w_gates"], BF16))
    return matmul_add(merged, m["w_out"], x)


def _tail(x, mem, m, sp, l):
    row = lambda name: sp[name][l].reshape(1, -1)
    hc, x = rms_res_op(x, row("norm_cross"))
    (mn,) = rms_mem_op(mem, row("norm_mem"))
    (ox,) = xattn_op(matmul(hc, m["x_q"], BF16), matmul(mn, m["x_kv"]))
    x = matmul_add(ox, m["x_o"], x)
    hf, x = rms_res_op(x, row("norm_ffn"))
    (act,) = swiglu_op(matmul(hf, m["ffn_gu"], BF16))
    return matmul_add(act, m["ffn_d"], x)


def _lower_bounds(hg_lb):
    lb_p = jax.nn.softmax(hg_lb, axis=0)
    return jnp.cumsum(lb_p, axis=0) - lb_p[0:1]


def _final_loss(target, x, norm_final):
    (row_loss,) = loss_op(x, target, norm_final.reshape(1, -1))
    return 0.5 * jnp.sum(row_loss[:, 0])


def _local_loss(x, mem, positions, target, w, sp):
    tabs = _rope_tables(positions)
    lower = _lower_bounds(sp["hg_lb"])
    for l in range(DEPTH):
        x = _layer(x, mem, tabs, _layer_matrices(w, l), sp, l, lower[l].reshape(1, -1))
    return _final_loss(target, x, sp["norm_final"])


N_DEV = 8
N_CHIPS = 4
COMM_LANES = 512
MESH_ID = pl.DeviceIdType.MESH
_ANY = pl.BlockSpec(memory_space=pl.ANY)
_OTHER_CHIPS = ((1, 0), (0, 1), (1, 1))


def _place():
    return lax.axis_index("x"), lax.axis_index("y"), lax.axis_index("c")


def _all_gather_call(blocks, name):
    n = len(blocks)

    def body(*refs):
        x_refs, out_refs = refs[:n], refs[n:2 * n]
        send_sems, recv_sems, local_sems = refs[2 * n:]
        x, y, c = _place()
        me, sibling = (x, y, c), (x, y, 1 - c)
        chips = [(x ^ fx, y ^ fy) for fx, fy in _OTHER_CHIPS]

        def slot(i, px, py, pc):
            return out_refs[i].at[4 * px + 2 * py + pc]

        def copy(i, k, blk, to, src=None):
            return pltpu.make_async_remote_copy(
                src_ref=slot(i, *blk) if src is None else src, dst_ref=slot(i, *blk),
                send_sem=send_sems.at[i, k], recv_sem=recv_sems.at[i, k], device_id=to, device_id_type=MESH_ID)

        mine = [pltpu.make_async_copy(x_refs[i], slot(i, *me), local_sems.at[i]) for i in range(n)]
        first = []
        for i in range(n):
            first.append(copy(i, 0, me, sibling, src=x_refs[i]))
            first += [copy(i, 1 + j, me, (*chip, c), src=x_refs[i]) for j, chip in enumerate(chips)]
        for cp in mine + first:
            cp.start()
        passed = []
        for j, chip in enumerate(chips):
            for i in range(n):
                copy(i, 1 + j, (*chip, c), me).wait_recv()
                passed.append(copy(i, 4 + j, (*chip, c), sibling))
                passed[-1].start()
        for i in range(n):
            copy(i, 0, sibling, me).wait_recv()
            for j, chip in enumerate(chips):
                copy(i, 4 + j, (*chip, 1 - c), me).wait_recv()
        for cp in first + passed:
            cp.wait_send()
        for cp in mine:
            cp.wait()

    return pl.pallas_call(
        body,
        name=name,
        out_shape=[jax.ShapeDtypeStruct((N_DEV,) + b.shape, b.dtype) for b in blocks],
        in_specs=[_ANY] * n,
        out_specs=[_ANY] * n,
        scratch_shapes=[pltpu.SemaphoreType.DMA((n, 7)), pltpu.SemaphoreType.DMA((n, 7)), pltpu.SemaphoreType.DMA((n,))],
    )(*blocks)


def _pair_exchange_call(gs, name):
    n = len(gs)

    def body(*refs):
        g_refs, got_refs = refs[:n], refs[n:2 * n]
        send_sems, recv_sems = refs[2 * n:]
        x, y, c = _place()
        sends = [pltpu.make_async_remote_copy(
            src_ref=g_refs[i].at[2 * p + (1 - c)], dst_ref=got_refs[i].at[p],
            send_sem=send_sems.at[i, p], recv_sem=recv_sems.at[i, p], device_id=(x, y, 1 - c), device_id_type=MESH_ID)
            for i in range(n) for p in range(N_CHIPS)]
        for cp in sends:
            cp.start()
        for cp in sends:
            cp.wait_recv()
        for cp in sends:
            cp.wait_send()

    return pl.pallas_call(
        body,
        name=name,
        out_shape=[jax.ShapeDtypeStruct((N_CHIPS,) + g.shape[1:], g.dtype) for g in gs],
        in_specs=[_ANY] * n,
        out_specs=[_ANY] * n,
        scratch_shapes=[pltpu.SemaphoreType.DMA((n, N_CHIPS))] * 2,
    )(*gs)


def _chip_exchange_call(parts, name):
    n = len(parts)

    def body(*refs):
        p_refs, got_refs = refs[:n], refs[n:2 * n]
        send_sems, recv_sems = refs[2 * n:]
        x, y, c = _place()
        sends = []
        for i in range(n):
            for k, (fx, fy) in enumerate(_OTHER_CHIPS):
                px, py = x ^ fx, y ^ fy
                sends.append(pltpu.make_async_remote_copy(
                    src_ref=p_refs[i].at[2 * px + py], dst_ref=got_refs[i].at[k],
                    send_sem=send_sems.at[i, k], recv_sem=recv_sems.at[i, k], device_id=(px, py, c), device_id_type=MESH_ID))
        for cp in sends:
            cp.start()
        for cp in sends:
            cp.wait_recv()
        for cp in sends:
            cp.wait_send()

    return pl.pallas_call(
        body,
        name=name,
        out_shape=[jax.ShapeDtypeStruct((3,) + p.shape[1:], p.dtype) for p in parts],
        in_specs=[_ANY] * n,
        out_specs=[_ANY] * n,
        scratch_shapes=[pltpu.SemaphoreType.DMA((n, 3))] * 2,
    )(*parts)


def _rows_cols(shape):
    return math.prod(shape[:-1]), shape[-1]


def _pair_sum_call(g, got, c_idx, name):
    rows, cols = _rows_cols(got.shape[1:])
    tr = _pick_tile(rows, (512, 256, 128, 64, 32, 16))

    def body(c_ref, a_ref, b_ref, o_ref):
        o_ref[...] = (a_ref[...].astype(F32) + b_ref[...].astype(F32)).astype(o_ref.dtype)

    spec = pl.BlockSpec((1, tr, cols), lambda p, i, c_ref: (p, i, 0))
    out = pl.pallas_call(
        body,
        name=name,
        grid_spec=pltpu.PrefetchScalarGridSpec(
            num_scalar_prefetch=1, grid=(N_CHIPS, rows // tr),
            in_specs=[pl.BlockSpec((1, tr, cols), lambda p, i, c_ref: (2 * p + c_ref[0], i, 0)), spec],
            out_specs=spec),
        out_shape=jax.ShapeDtypeStruct((N_CHIPS, rows, cols), got.dtype),
        compiler_params=_cparams("parallel", "parallel"),
    )(c_idx, g.reshape(N_DEV, rows, cols), got.reshape(N_CHIPS, rows, cols))
    return out.reshape(got.shape)


def _chip_sum_call(part, got, chip_idx, name):
    rows, cols = _rows_cols(got.shape[1:])
    tr = _pick_tile(rows, (512, 256, 128, 64, 32, 16))

    def body(p_ref, a_ref, b_ref, o_ref):
        acc = a_ref[0].astype(F32)
        for k in range(3):
            acc = acc + b_ref[k].astype(F32)
        o_ref[...] = acc

    out = pl.pallas_call(
        body,
        name=name,
        grid_spec=pltpu.PrefetchScalarGridSpec(
            num_scalar_prefetch=1, grid=(rows // tr,),
            in_specs=[pl.BlockSpec((1, tr, cols), lambda i, p_ref: (p_ref[0], i, 0)),
                      pl.BlockSpec((3, tr, cols), lambda i, p_ref: (0, i, 0))],
            out_specs=pl.BlockSpec((tr, cols), lambda i, p_ref: (i, 0))),
        out_shape=jax.ShapeDtypeStruct((rows, cols), F32),
        compiler_params=_cparams("parallel"),
    )(chip_idx, part.reshape(N_CHIPS, rows, cols), got.reshape(3, rows, cols))
    return out.reshape(got.shape[1:])


def _reduce_scatter(gs, name):
    x, y, c = _place()
    c_idx = c.astype(jnp.int32).reshape(1)
    chip_idx = (2 * x + y).astype(jnp.int32).reshape(1)
    gots = _pair_exchange_call(gs, name + "_pair")
    parts = [_pair_sum_call(g, got, c_idx, name + "_pair_sum") for g, got in zip(gs, gots)]
    gots = _chip_exchange_call(parts, name + "_chip")
    return [_chip_sum_call(p, got, chip_idx, name + "_chip_sum") for p, got in zip(parts, gots)]


_HBM = pl.BlockSpec(memory_space=pltpu.HBM)
_SEM = pl.BlockSpec(memory_space=pltpu.SEMAPHORE)
_SIDE_EFFECT = pltpu.SideEffectType.DATAFLOW_SIDE_EFFECTING
N_PEERS = N_DEV - 1


def _peer(k):
    x, y, c = _place()
    px, py, pc = x ^ ((k >> 2) & 1), y ^ ((k >> 1) & 1), c ^ (k & 1)
    return (px, py, pc), 4 * px + 2 * py + pc


def _exchange_copy(src_ref, land_ref, send_sems, recv_sems, i, k, scatter, receiving):
    x, y, c = _place()
    me = 4 * x + 2 * y + c
    peer, peer_idx = _peer(k)
    sem = i * N_PEERS + k - 1
    return pltpu.make_async_remote_copy(
        src_ref=src_ref.at[peer_idx] if scatter else src_ref, dst_ref=land_ref.at[peer_idx if receiving else me],
        send_sem=send_sems.at[sem], recv_sem=recv_sems.at[sem], device_id=peer, device_id_type=MESH_ID)


def _exchange_start_call(srcs, after, scatter, name):
    n = len(srcs)
    slot_shapes = [s.shape[1:] if scatter else s.shape for s in srcs]

    def body(*refs):
        src_refs, land_refs = refs[:n], refs[n:2 * n]
        send_sems, recv_sems = refs[2 * n + 1], refs[2 * n + 2]
        token = refs[-1]
        for i in range(n):
            for k in range(1, N_DEV):
                _exchange_copy(src_refs[i], land_refs[i], send_sems, recv_sems, i, k, scatter, False).start()
        token[...] = jnp.zeros_like(token)

    lands = [pltpu.with_memory_space_constraint(lax.empty((N_DEV,) + shp, s.dtype), pltpu.HBM)
             for shp, s in zip(slot_shapes, srcs)]
    out = pl.pallas_call(
        body,
        name=name,
        out_shape=([pltpu.SemaphoreType.DMA((n * N_PEERS,)), pltpu.SemaphoreType.DMA((n * N_PEERS,))]
                   + [pltpu.HBM(s.shape, s.dtype) for s in srcs] + [pltpu.HBM(l.shape, l.dtype) for l in lands]
                   + [jax.ShapeDtypeStruct((SUBLANES, LANES), F32)]),
        in_specs=[_HBM] * (2 * n) + [pl.BlockSpec(memory_space=pl.ANY)],
        out_specs=[_SEM, _SEM] + [_HBM] * (2 * n) + [pl.BlockSpec(memory_space=pltpu.VMEM)],
        input_output_aliases={j: 2 + j for j in range(2 * n)},
        compiler_params=pltpu.CompilerParams(has_side_effects=_SIDE_EFFECT),
    )(*[pltpu.with_memory_space_constraint(s, pltpu.HBM) for s in srcs], *lands, after)
    return out[0], out[1], list(out[2:2 + n]), list(out[2 + n:2 + 2 * n]), out[-1]


def _exchange_wait_call(started, after, scatter, name):
    send_sems, recv_sems, srcs, lands, _ = started
    n = len(srcs)

    def body(*refs):
        src_refs, land_refs = refs[:n], refs[n:2 * n]
        send_s, recv_s = refs[2 * n], refs[2 * n + 1]
        for i in range(n):
            for k in range(1, N_DEV):
                cp = _exchange_copy(src_refs[i], land_refs[i], send_s, recv_s, i, k, scatter, True)
                cp.wait_send()
                cp.wait_recv()

    out = pl.pallas_call(
        body,
        name=name,
        out_shape=[pltpu.HBM(s.shape, s.dtype) for s in srcs] + [pltpu.HBM(l.shape, l.dtype) for l in lands],
        in_specs=[_HBM] * (2 * n) + [_SEM, _SEM, pl.BlockSpec(memory_space=pl.ANY)],
        out_specs=[_HBM] * (2 * n),
        input_output_aliases={j: j for j in range(2 * n)},
        compiler_params=pltpu.CompilerParams(has_side_effects=_SIDE_EFFECT),
    )(*srcs, *lands, send_sems, recv_sems, after)
    return list(out[n:])


def _own_slot(land, own):
    x, y, c = _place()
    return lax.dynamic_update_index_in_dim(land, own, 4 * x + 2 * y + c, 0)


def _slot_sum_call(land, name):
    rows, cols = _rows_cols(land.shape[1:])
    tr = _pick_tile(rows, (256, 128, 64, 32, 16))

    def body(land_ref, o_ref):
        acc = land_ref[0].astype(F32)
        for s in range(1, N_DEV):
            acc = acc + land_ref[s].astype(F32)
        o_ref[...] = acc

    out = pl.pallas_call(
        body,
        name=name,
        grid=(rows // tr,),
        in_specs=[pl.BlockSpec((N_DEV, tr, cols), lambda i: (0, i, 0))],
        out_specs=pl.BlockSpec((tr, cols), lambda i: (i, 0)),
        out_shape=jax.ShapeDtypeStruct((rows, cols), F32),
        compiler_params=_cparams("parallel"),
    )(land.reshape(N_DEV, rows, cols))
    return out.reshape(land.shape[1:])


SMALL_BLOCK_ROWS = 16


def _pack_small(parts):
    flat = jnp.concatenate([p.reshape(-1) for p in parts])
    chunk = N_DEV * SMALL_BLOCK_ROWS * COMM_LANES
    flat = jnp.pad(flat, (0, (-flat.shape[0]) % chunk))
    return flat.reshape(N_DEV, -1, COMM_LANES)


def _unpack_small(buf, shapes):
    flat = buf.reshape(-1)
    out, off = [], 0
    for shp in shapes:
        n = math.prod(shp)
        out.append(flat[off:off + n].reshape(shp))
        off += n
    return out


SHARDED = dict(w_in=2, ssm_w_glu=2, mla_w_uq=2, mla_w_ukv=2, mla_w_o=2, hg_w_o=2, w_out=1, x_w_q=1, x_w_kv=1,
               x_w_o=2, ffn_w_gate_up=2, ffn_w_down=1)
REPLICATED = ("norm_mix", "ssm_lam_re", "ssm_lam_im", "ssm_b_re", "ssm_b_im", "ssm_c_re", "ssm_c_im", "ssm_d",
              "ssm_log_step", "mla_q_norm", "mla_kv_norm", "hg_lb", "hg_g_norm", "norm_cross", "norm_mem", "norm_ffn",
              "norm_final")


def _join_shards(stacked, axis):
    n, l, a, b = stacked.shape
    if axis == 1:
        return stacked.transpose(1, 0, 2, 3).reshape(l, n * a, b)
    return stacked.transpose(1, 2, 0, 3).reshape(l, a, n * b)


def _split_shards(full, axis):
    l, a, b = full.shape
    if axis == 1:
        return full.reshape(l, N_DEV, a // N_DEV, b).transpose(1, 0, 2, 3)
    return full.reshape(l, a, N_DEV, b // N_DEV).transpose(2, 0, 1, 3)


MIXER_WEIGHTS = ("w_in", "ssm_w_glu", "mla_w_uq", "mla_w_ukv", "mla_w_o", "hg_w_o", "w_out")
TAIL_WEIGHTS = ("x_w_q", "x_w_kv", "x_w_o", "ffn_w_gate_up", "ffn_w_down")


def _mixer_fn(l, tabs):
    def f(x, full, small, lower):
        m = _mixer_matrices(dict(zip(MIXER_WEIGHTS, full)), 0)
        return _mixer(x, tabs, m, dict(zip(REPLICATED, small)), l, lower[l].reshape(1, -1))
    return f


def _tail_fn(l, mem):
    def f(x, full, small):
        return _tail(x, mem, _tail_matrices(dict(zip(TAIL_WEIGHTS, full)), 0), dict(zip(REPLICATED, small)), l)
    return f


ADAM_LR, ADAM_B1, ADAM_B2, ADAM_EPS, ADAM_WD, ADAM_STEP = 0.001, 0.9, 0.999, 1e-08, 0.01, 10


def _adamw_update(w, g, m, v):
    m_new = ADAM_B1 * m + (1.0 - ADAM_B1) * g
    v_new = ADAM_B2 * v + (1.0 - ADAM_B2) * jnp.square(g)
    m_hat = m_new / (1.0 - ADAM_B1 ** ADAM_STEP)
    v_hat = v_new / (1.0 - ADAM_B2 ** ADAM_STEP)
    return -ADAM_LR * (m_hat / (jnp.sqrt(v_hat) + ADAM_EPS) + ADAM_WD * w), m_new, v_new


def _adamw_stacked_call(w, g, m, v, name):
    depth, rows, cols = w.shape
    tr = _pick_tile(rows, (512, 256, 128, 64, 32, 16, 8))

    def body(w_ref, g_ref, m_ref, v_ref, d_ref, nm_ref, nv_ref):
        d_ref[...], nm_ref[...], nv_ref[...] = _adamw_update(w_ref[...], g_ref[...], m_ref[...], v_ref[...])

    spec = pl.BlockSpec((None, tr, cols), lambda l, i: (l, i, 0))
    return tuple(pl.pallas_call(
        body, name=name, grid=(depth, rows // tr), in_specs=[spec] * 4, out_specs=[spec] * 3,
        out_shape=[jax.ShapeDtypeStruct(w.shape, F32)] * 3, compiler_params=_cparams("parallel", "parallel"),
    )(w, g, m, v))


def _adamw_call(w, g, m, v, name):
    shape = w.shape
    if len(shape) == 3:
        return _adamw_stacked_call(w, g, m, v, name)
    cols = shape[-1]
    rows = math.prod(shape[:-1]) if len(shape) > 1 else 1
    tr = _pick_tile(rows, (512, 256, 128, 64, 32, 16, 8))

    def body(w_ref, g_ref, m_ref, v_ref, d_ref, nm_ref, nv_ref):
        d_ref[...], nm_ref[...], nv_ref[...] = _adamw_update(w_ref[...], g_ref[...], m_ref[...], v_ref[...])

    spec = pl.BlockSpec((tr, cols), lambda i: (i, 0))
    outs = pl.pallas_call(
        body, name=name, grid=(rows // tr,), in_specs=[spec] * 4, out_specs=[spec] * 3,
        out_shape=[jax.ShapeDtypeStruct((rows, cols), F32)] * 3, compiler_params=_cparams("parallel"),
    )(*(t.reshape(rows, cols) for t in (w, g, m, v)))
    return tuple(o.reshape(shape) for o in outs)


WEIGHTS = ("norm_mix", "w_in", "ssm_lam_re", "ssm_lam_im", "ssm_b_re", "ssm_b_im", "ssm_c_re", "ssm_c_im", "ssm_d",
           "ssm_log_step", "ssm_w_glu", "mla_q_norm", "mla_kv_norm", "mla_w_uq", "mla_w_ukv", "mla_w_o", "hg_lb",
           "hg_g_norm", "hg_w_o", "w_out", "norm_cross", "norm_mem", "x_w_q", "x_w_kv", "x_w_o", "norm_ffn",
           "ffn_w_gate_up", "ffn_w_down", "norm_final")


def kernel(x, mem, positions, norm_mix, w_in, ssm_lam_re, ssm_lam_im, ssm_b_re, ssm_b_im, ssm_c_re, ssm_c_im, ssm_d, ssm_log_step, ssm_w_glu, mla_q_norm, mla_kv_norm, mla_w_uq, mla_w_ukv, mla_w_o, hg_lb, hg_g_norm, hg_w_o, w_out, norm_cross, norm_mem, x_w_q, x_w_kv, x_w_o, norm_ffn, ffn_w_gate_up, ffn_w_down, norm_final, loss_target, m_norm_mix, m_w_in, m_ssm_lam_re, m_ssm_lam_im, m_ssm_b_re, m_ssm_b_im, m_ssm_c_re, m_ssm_c_im, m_ssm_d, m_ssm_log_step, m_ssm_w_glu, m_mla_q_norm, m_mla_kv_norm, m_mla_w_uq, m_mla_w_ukv, m_mla_w_o, m_hg_lb, m_hg_g_norm, m_hg_w_o, m_w_out, m_norm_cross, m_norm_mem, m_x_w_q, m_x_w_kv, m_x_w_o, m_norm_ffn, m_ffn_w_gate_up, m_ffn_w_down, m_norm_final, v_norm_mix, v_w_in, v_ssm_lam_re, v_ssm_lam_im, v_ssm_b_re, v_ssm_b_im, v_ssm_c_re, v_ssm_c_im, v_ssm_d, v_ssm_log_step, v_ssm_w_glu, v_mla_q_norm, v_mla_kv_norm, v_mla_w_uq, v_mla_w_ukv, v_mla_w_o, v_hg_lb, v_hg_g_norm, v_hg_w_o, v_w_out, v_norm_cross, v_norm_mem, v_x_w_q, v_x_w_kv, v_x_w_o, v_norm_ffn, v_ffn_w_gate_up, v_ffn_w_down, v_norm_final):
    given = dict(locals())
    weights = {n: given[n] for n in WEIGHTS}
    small = tuple(weights[n] for n in REPLICATED)
    layer1 = MIXER_WEIGHTS + TAIL_WEIGHTS
    shards = lambda names, l: [weights[n][l:l + 1].astype(BF16) for n in names]
    join = lambda names, stacked: tuple(_join_shards(p, SHARDED[n]) for n, p in zip(names, stacked))
    split = lambda names, cts: [_split_shards(ct, SHARDED[n]) for n, ct in zip(names, cts)]
    landed = lambda names, lands, own: join(names, [_own_slot(land, o) for land, o in zip(lands, own)])
    xs, tabs = x[0], _rope_tables(positions[0])
    lower, vjp_lower = jax.vjp(_lower_bounds, hg_lb)
    me = 4 * lax.axis_index("x") + 2 * lax.axis_index("y") + lax.axis_index("c")

    got_m0 = _all_gather_call(shards(MIXER_WEIGHTS, 0), "weights_all_gather_m0")
    own_t0, own_l1 = shards(TAIL_WEIGHTS, 0), shards(layer1, 1)
    gather_t0 = _exchange_start_call(own_t0, got_m0[0], False, "weights_gather_start_t0")
    gather_l1 = _exchange_start_call(own_l1, gather_t0[4], False, "weights_gather_start_l1")
    xs = xs + gather_l1[4][0, 0]
    xa0, vjp_m0 = jax.vjp(_mixer_fn(0, tabs), xs, join(MIXER_WEIGHTS, got_m0), small, lower)
    full_t0 = landed(TAIL_WEIGHTS, _exchange_wait_call(gather_t0, xa0, False, "weights_gather_wait_t0"), own_t0)
    x1, vjp_t0 = jax.vjp(_tail_fn(0, mem[0]), xa0, full_t0, small)
    full_l1 = landed(layer1, _exchange_wait_call(gather_l1, x1, False, "weights_gather_wait_l1"), own_l1)
    xa1, vjp_m1 = jax.vjp(_mixer_fn(1, tabs), x1, full_l1[:len(MIXER_WEIGHTS)], small, lower)
    x2, vjp_t1 = jax.vjp(_tail_fn(1, mem[0]), xa1, full_l1[len(MIXER_WEIGHTS):], small)
    loss_local, vjp_loss = jax.vjp(functools.partial(_final_loss, loss_target[0]), x2, norm_final)

    def scatter_start(names, cts, dx, tag):
        gs = split(names, cts)
        started = _exchange_start_call(gs, dx, True, "grads_scatter_start_" + tag)
        return (started, gs), dx + started[4][0, 0]

    def scatter_finish(pending, after, tag):
        started, gs = pending
        lands = _exchange_wait_call(started, after, True, "grads_scatter_wait_" + tag)
        return [_slot_sum_call(_own_slot(land, lax.dynamic_index_in_dim(g, me, 0, keepdims=False)), "grads_slot_sum_" + tag)
                for land, g in zip(lands, gs)]

    dx2, d_norm_final = vjp_loss(jnp.ones((), F32))
    dxa1, dfull_t1, dsmall_t1 = vjp_t1(dx2)
    pend_t1, dxa1 = scatter_start(TAIL_WEIGHTS, dfull_t1, dxa1, "t1")
    dx1, dfull_m1, dsmall_m1, dlower1 = vjp_m1(dxa1)
    pend_m1, dx1 = scatter_start(MIXER_WEIGHTS, dfull_m1, dx1, "m1")
    dxa0, dfull_t0, dsmall_t0 = vjp_t0(dx1)
    pend_t0, dxa0 = scatter_start(TAIL_WEIGHTS, dfull_t0, dxa0, "t0")
    gx, dfull_m0, dsmall_m0, dlower0 = vjp_m0(dxa0)
    by_layer = {
        0: dict(zip(MIXER_WEIGHTS + TAIL_WEIGHTS,
                    _reduce_scatter(split(MIXER_WEIGHTS, dfull_m0), "grads_reduce_scatter_m0") + scatter_finish(pend_t0, gx, "t0"))),
        1: dict(zip(layer1, scatter_finish(pend_m1, gx, "m1") + scatter_finish(pend_t1, gx, "t1")))}
    grads = {n: jnp.concatenate([by_layer[0][n], by_layer[1][n]], axis=0) for n in SHARDED}

    d_small = dict(zip(REPLICATED, (a + b + c + d for a, b, c, d in zip(dsmall_m0, dsmall_t0, dsmall_m1, dsmall_t1))))
    d_small["norm_final"] = d_small["norm_final"] + d_norm_final
    d_small["hg_lb"] = d_small["hg_lb"] + vjp_lower(dlower0 + dlower1)[0]
    shapes = [d_small[n].shape for n in REPLICATED]
    (mine,) = _reduce_scatter([_pack_small([d_small[n] for n in REPLICATED])], "small_reduce_scatter")
    (total,) = _all_gather_call([mine], "small_all_gather")
    grads.update(zip(REPLICATED, _unpack_small(total, shapes)))

    loss = lax.psum(loss_local, ("x", "y", "c"))
    steps = {n: _adamw_call(weights[n], grads[n], given["m_" + n], given["v_" + n], "adamw_" + n) for n in WEIGHTS}
    return (loss, gx[None], *[grads[n] for n in WEIGHTS], *[steps[n][0] for n in WEIGHTS],
            *[steps[n][1] for n in WEIGHTS], *[steps[n][2] for n in WEIGHTS])
```

```python
import functools
import math

import jax
import jax.numpy as jnp
from jax import lax
from jax.experimental import pallas as pl
from jax.experimental.pallas import tpu as pltpu

F32 = jnp.float32
BF16 = jnp.bfloat16

VMEM_LIMIT_BYTES = 48 * 1024 * 1024
LANES = 128
SUBLANES = 8


def _cparams(*sem):
    return pltpu.CompilerParams(dimension_semantics=sem, vmem_limit_bytes=VMEM_LIMIT_BYTES)


def _pick_tile(n, cands):
    for c in cands:
        if n % c == 0:
            return c
    return n


MM_VMEM_BUDGET = 38 * 1024 * 1024
MM_STEP_US = 0.35
HBM_BYTES_PER_US = 3.0e6
VREG_RMW_PER_US = 1.5e3


def _divisor_tiles(dim, cands):
    out = [t for t in cands if dim % t == 0]
    return out or [dim]


def _mm_tiles(m, n, k, sa, sb, so):
    tms = _divisor_tiles(m, (1408, 1024, 512, 256, 128, 64, 32, 16, 8))[:2]
    tns = _divisor_tiles(n, (2048, 1536, 1408, 1024, 768, 512, 384, 256, 128))
    tks = [k // d for d in (1, 2, 4, 8, 13, 16, 26, 32, 52) if k % d == 0 and (k // d) % LANES == 0] or [k]
    best = None
    for tk in tks:
        nk = k // tk
        for tm in tms:
            for tn in tns:
                vmem = 2 * (tm * tk * sa + tk * tn * sb + tm * tn * so) + (tm * tn * 4 if nk > 1 else 0)
                if vmem > MM_VMEM_BUDGET:
                    continue
                steps = (m // tm) * (n // tn) * nk
                a_reads = m * k * sa * (n // tn if nk > 1 else 1)
                b_reads = k * n * sb * (m // tm if (nk > 1 or n // tn > 1) else 1)
                cost = (steps * MM_STEP_US + (a_reads + b_reads) / HBM_BYTES_PER_US
                        + (m * n * nk / 1024 / VREG_RMW_PER_US if nk > 1 else 0.0))
                if best is None or cost < best[0]:
                    best = (cost, tm, tn, tk)
    assert best is not None, (m, n, k)
    return best[1:]


def _mm_tiles_cached_t(m, n, k, sa, sb, so):
    for tm in _divisor_tiles(m, (1024, 512)):
        if m % tm:
            break
        for tn in _divisor_tiles(n, (1024, 512, 384, 256, 128)):
            if 2 * (k * tm * sa + k * tn * sb + tm * tn * so) + tm * k * 2 <= MM_VMEM_BUDGET:
                return tm, tn
    return None


def _mm_tn_cached_call(a, b, tiles, out_dtype, name):
    k, m = a.shape
    n = b.shape[1]
    tm, tn = tiles

    def body(a_ref, b_ref, o_ref, at_ref):
        @pl.when(pl.program_id(1) == 0)
        def _():
            at_ref[...] = a_ref[...].astype(BF16).T

        o_ref[...] = lax.dot_general(at_ref[...], b_ref[...].astype(BF16), _NN_DIMS,
                                     preferred_element_type=F32).astype(out_dtype)

    return pl.pallas_call(
        body,
        name=name,
        grid=(m // tm, n // tn),
        in_specs=[pl.BlockSpec((k, tm), lambda i, j: (0, i)), pl.BlockSpec((k, tn), lambda i, j: (0, j))],
        out_specs=pl.BlockSpec((tm, tn), lambda i, j: (i, j)),
        out_shape=jax.ShapeDtypeStruct((m, n), out_dtype),
        scratch_shapes=[pltpu.VMEM((tm, k), BF16)],
        compiler_params=_cparams("parallel", "arbitrary"),
    )(a, b)


_NN_DIMS = (((1,), (0,)), ((), ()))


def _mm_call(a, b, ta, tb, add=None, out_dtype=F32, name="mm"):
    m, k = (a.shape[1], a.shape[0]) if ta else a.shape
    k2, n = (b.shape[1], b.shape[0]) if tb else b.shape
    assert k == k2, (a.shape, b.shape, ta, tb)
    sizes = (a.dtype.itemsize, b.dtype.itemsize, jnp.dtype(out_dtype).itemsize + (add.dtype.itemsize if add is not None else 0))
    if ta:
        tiles = _mm_tiles_cached_t(m, n, k, *sizes)
        if tiles is not None:
            return _mm_tn_cached_call(a, b, tiles, out_dtype, name)
    tm, tn, tk = _mm_tiles(m, n, k, *sizes)
    nk = k // tk
    a_spec = pl.BlockSpec((tk, tm), lambda i, j, kk: (kk, i)) if ta else pl.BlockSpec((tm, tk), lambda i, j, kk: (i, kk))
    b_spec = pl.BlockSpec((tn, tk), lambda i, j, kk: (j, kk)) if tb else pl.BlockSpec((tk, tn), lambda i, j, kk: (kk, j))
    o_spec = pl.BlockSpec((tm, tn), lambda i, j, kk: (i, j))
    dn = (((0 if ta else 1,), (1 if tb else 0,)), ((), ()))
    has_add = add is not None

    def body(*refs):
        a_ref, b_ref = refs[0], refs[1]
        c_ref = refs[2] if has_add else None
        o_ref = refs[3] if has_add else refs[2]
        p = lax.dot_general(a_ref[...].astype(BF16), b_ref[...].astype(BF16), dn, preferred_element_type=F32)

        def finish(r):
            if has_add:
                r = r + c_ref[...].astype(F32)
            o_ref[...] = r.astype(out_dtype)

        if nk == 1:
            finish(p)
        else:
            acc_ref = refs[-1]
            kk = pl.program_id(2)

            @pl.when(kk == 0)
            def _():
                acc_ref[...] = p

            @pl.when(kk > 0)
            def _():
                acc_ref[...] += p

            @pl.when(kk == nk - 1)
            def _():
                finish(acc_ref[...])

    in_specs = [a_spec, b_spec] + ([o_spec] if has_add else [])
    args = (a, b) + ((add,) if has_add else ())
    return pl.pallas_call(
        body,
        name=name,
        grid=(m // tm, n // tn, nk),
        in_specs=in_specs,
        out_specs=o_spec,
        out_shape=jax.ShapeDtypeStruct((m, n), out_dtype),
        scratch_shapes=[] if nk == 1 else [pltpu.VMEM((tm, tn), F32)],
        compiler_params=_cparams("parallel", "parallel", "arbitrary"),
    )(*args)


@functools.partial(jax.custom_vjp, nondiff_argnums=(2,))
def matmul(a, b, out_dtype=F32):
    return _mm_call(a, b, False, False, out_dtype=out_dtype, name="mm_fwd")


def _matmul_fwd(a, b, out_dtype):
    return matmul(a, b, out_dtype), (a, b)


def _matmul_bwd(out_dtype, res, g):
    a, b = res
    da = _mm_call(g, b, False, True, out_dtype=a.dtype, name="mm_da")
    db = _mm_call(a, g, True, False, out_dtype=b.dtype, name="mm_db")
    return da, db


matmul.defvjp(_matmul_fwd, _matmul_bwd)


@jax.custom_vjp
def matmul_add(a, b, c):
    return _mm_call(a, b, False, False, add=c, name="mm_add_fwd")


def _matmul_add_fwd(a, b, c):
    return _mm_call(a, b, False, False, add=c, name="mm_add_fwd"), (a, b)


def _matmul_add_bwd(res, g):
    a, b = res
    da = _mm_call(g, b, False, True, out_dtype=a.dtype, name="mm_da")
    db = _mm_call(a, g, True, False, out_dtype=b.dtype, name="mm_db")
    return da, db, g


matmul_add.defvjp(_matmul_add_fwd, _matmul_add_bwd)


def rowwise(f, n_rows, n_aux, tile, name, passthrough=False):
    def specs(arrs, tiled):
        out = []
        for x in arrs:
            if tiled:
                out.append(pl.BlockSpec((tile, x.shape[1]), lambda i: (i, 0)))
            else:
                out.append(pl.BlockSpec(x.shape, lambda i: (0, 0)))
        return out

    def tile_structs(args):
        rows_aux, params = args[: n_rows + n_aux], args[n_rows + n_aux:]
        return [jax.ShapeDtypeStruct((tile, x.shape[1]), x.dtype) for x in rows_aux] + [
            jax.ShapeDtypeStruct(p.shape, p.dtype) for p in params]

    def fwd_call(*args):
        s = args[0].shape[0]
        outs = jax.eval_shape(f, *tile_structs(args))
        n_in = len(args)

        def body(*refs):
            vals = [r[...] for r in refs[:n_in]]
            res = f(*vals)
            for o_ref, r in zip(refs[n_in:], res):
                o_ref[...] = r.astype(o_ref.dtype)

        return pl.pallas_call(
            body,
            name=name + "_fwd",
            grid=(s // tile,),
            in_specs=specs(args[: n_rows + n_aux], True) + specs(args[n_rows + n_aux:], False),
            out_specs=[pl.BlockSpec((tile, o.shape[1]), lambda i: (i, 0)) for o in outs],
            out_shape=[jax.ShapeDtypeStruct((s, o.shape[1]), o.dtype) for o in outs],
            compiler_params=_cparams("parallel"),
        )(*args)

    def bwd_call(args, gs):
        s = args[0].shape[0]
        rows, aux, params = args[:n_rows], args[n_rows:n_rows + n_aux], args[n_rows + n_aux:]
        n_in, n_g, n_p = len(args), len(gs), len(params)
        n_gf = n_g - 1 if passthrough else n_g

        def body(*refs):
            vals = [r[...] for r in refs[:n_in]]
            gvals = tuple(r[...] for r in refs[n_in:n_in + n_gf])
            out_refs = refs[n_in + n_g:]
            auxv = vals[n_rows:n_rows + n_aux]

            def g_(*rp):
                return tuple(f(*rp[:n_rows], *auxv, *rp[n_rows:]))

            _, vjp = jax.vjp(g_, *vals[:n_rows], *vals[n_rows + n_aux:])
            cts = list(vjp(gvals))
            if passthrough:
                cts[0] = cts[0] + refs[n_in + n_gf][...]
            for o_ref, ct in zip(out_refs[:n_rows], cts[:n_rows]):
                o_ref[...] = ct.astype(o_ref.dtype)
            if n_p:
                @pl.when(pl.program_id(0) == 0)
                def _():
                    for o_ref in out_refs[n_rows:]:
                        o_ref[...] = jnp.zeros_like(o_ref)

                for o_ref, ct in zip(out_refs[n_rows:], cts[n_rows:]):
                    o_ref[...] += ct.astype(o_ref.dtype)

        return pl.pallas_call(
            body,
            name=name + "_bwd",
            grid=(s // tile,),
            in_specs=specs(rows + aux, True) + specs(params, False) + specs(gs, True),
            out_specs=specs(rows, True) + specs(params, False),
            out_shape=[jax.ShapeDtypeStruct(x.shape, x.dtype) for x in rows + params],
            compiler_params=_cparams("arbitrary" if n_p else "parallel"),
        )(*args, *gs)

    @jax.custom_vjp
    def op(*args):
        return tuple(fwd_call(*args)) + ((args[0],) if passthrough else ())

    def op_fwd(*args):
        return op(*args), args

    def op_bwd(args, gs):
        cts = bwd_call(tuple(args), tuple(gs))
        rows_ct, par_ct = cts[:n_rows], cts[n_rows:]
        aux_ct = [jnp.zeros_like(a) for a in args[n_rows:n_rows + n_aux]]
        return tuple(rows_ct) + tuple(aux_ct) + tuple(par_ct)

    op.defvjp(op_fwd, op_bwd)
    return op


SCAN_SEGMENTS = SUBLANES


def _scan_step(ar, ai, xr, xi, br, bi):
    return ar * xr - ai * xi + br, ar * xi + ai * xr + bi


_NT_DIMS = (((1,), (1,)), ((), ()))
_TN_DIMS = (((0,), (0,)), ((), ()))


S5_GROUPS_PER_STEP = 128
S5_STATE_LANES = 512


def _s5_interleave(src_ref, scr_ref, rows):
    for k in range(SCAN_SEGMENTS):
        scr_ref[pl.ds(k, rows, stride=SCAN_SEGMENTS), :] = src_ref[k].astype(F32)


def _s5_deinterleave(val, scr_ref, dst_ref, rows):
    scr_ref[...] = val
    for k in range(SCAN_SEGMENTS):
        dst_ref[k] = scr_ref[pl.ds(k, rows, stride=SCAN_SEGMENTS), :]


def _s5_segment_starts(a_ref, fr_ref, fi_ref, sr, si, seg_len, order):
    tn = sr.shape[1]
    pr, pi = a_ref[0:1, :], a_ref[1:2, :]
    for _ in range(seg_len.bit_length() - 1):
        pr, pi = pr * pr - pi * pi, 2.0 * pr * pi
    cr = jnp.zeros((1, tn), F32)
    ci = jnp.zeros((1, tn), F32)
    for idx, k in enumerate(order):
        if idx > 0:
            kp = order[idx - 1]
            cr, ci = (fr_ref[kp:kp + 1, :] + pr * cr - pi * ci, fi_ref[kp:kp + 1, :] + pr * ci + pi * cr)
        sr[k:k + 1, :] = cr
        si[k:k + 1, :] = ci


def _s5_pass_call(src, w_r, w_i, a, transpose_w, reverse, finals, extra, name):
    s = src.shape[0]
    nb = w_r.shape[0]
    n = nb * S5_STATE_LANES
    seg_len = s // SCAN_SEGMENTS
    ti = min(S5_GROUPS_PER_STEP, seg_len)
    nt = seg_len // ti
    tr = SCAN_SEGMENTS * ti
    tn = S5_STATE_LANES
    assert seg_len & (seg_len - 1) == 0
    order = list(range(SCAN_SEGMENTS))[::-1] if reverse else list(range(SCAN_SEGMENTS))
    dn_in = _NT_DIMS if transpose_w else _NN_DIMS
    first = finals is None
    backward = (not first) and reverse
    forward = (not first) and not reverse
    tmap3 = (lambda c, j: (0, nt - 1 - j, c)) if reverse else (lambda c, j: (0, j, c))
    tmap2 = (lambda c, j: (nt - 1 - j, c)) if reverse else (lambda c, j: (j, c))
    bf = lambda v: v.astype(BF16)

    def body(*refs):
        it = iter(refs)
        src_ref, wr_ref, wi_ref, a_ref = next(it), next(it), next(it), next(it)
        if not first:
            fr_ref, fi_ref = next(it), next(it)
        if forward:
            cdr_ref, cdi_ref = next(it), next(it)
            xr_out, xi_out, y_ref = next(it), next(it), next(it)
        if backward:
            xr_ref, xi_ref, u_ref, bdr_ref, bdi_ref = next(it), next(it), next(it), next(it), next(it)
            du_ref, dar_ref, dai_ref, dbr_ref, dbi_ref, dcr_ref, dci_ref = (next(it) for _ in range(7))
        if first:
            fr_out, fi_out = next(it), next(it)
        sr, si, in_scr, dr_scr, di_scr = next(it), next(it), next(it), next(it), next(it)
        if backward:
            accr, acci, u_scr = next(it), next(it), next(it)
        j = pl.program_id(1)

        @pl.when(j == 0)
        def _():
            if first:
                sr[...] = jnp.zeros_like(sr)
                si[...] = jnp.zeros_like(si)
            else:
                _s5_segment_starts(a_ref, fr_ref, fi_ref, sr, si, seg_len, order)
            if backward:
                for r in (accr, acci, dbr_ref, dbi_ref, dcr_ref, dci_ref):
                    r[...] = jnp.zeros_like(r)

        _s5_interleave(src_ref, in_scr, ti)
        src_b = bf(in_scr[...])
        dr_scr[...] = lax.dot_general(src_b, bf(wr_ref[0]), dn_in, preferred_element_type=F32)
        di_scr[...] = lax.dot_general(src_b, bf(wi_ref[0]), dn_in, preferred_element_type=F32)
        ar = jnp.broadcast_to(a_ref[0:1, :], (SUBLANES, tn))
        ai = jnp.broadcast_to(a_ref[1:2, :], (SUBLANES, tn))

        def step(ii, carry):
            i = (ti - 1 - ii) if reverse else ii
            rows = pl.ds(pl.multiple_of(i * SUBLANES, SUBLANES), SUBLANES)
            xr, xi = carry[0], carry[1]
            if backward:
                zr, zi = xr_ref[rows, :], xi_ref[rows, :]
                acc = (carry[2] + xr * zr + xi * zi, carry[3] + xi * zr - xr * zi)
            nr, ni = _scan_step(ar, ai, xr, xi, dr_scr[rows, :], di_scr[rows, :])
            if forward:
                xr_out[rows, :] = nr
                xi_out[rows, :] = ni
            if backward:
                dr_scr[rows, :] = nr
                di_scr[rows, :] = ni
            return (nr, ni) + (acc if backward else ())

        init = (sr[...], si[...]) + ((accr[...], acci[...]) if backward else ())
        out = lax.fori_loop(0, ti, step, init, unroll=4)
        sr[...] = out[0]
        si[...] = out[1]
        if first:
            @pl.when(j == nt - 1)
            def _():
                fr_out[...] = out[0]
                fi_out[...] = out[1]
        if forward:
            y = (lax.dot_general(bf(xr_out[...]), bf(cdr_ref[0]), _NN_DIMS, preferred_element_type=F32)
                 + lax.dot_general(bf(xi_out[...]), bf(cdi_ref[0]), _NN_DIMS, preferred_element_type=F32))
            _s5_deinterleave(y, in_scr, y_ref, ti)
        if backward:
            accr[...] = out[2]
            acci[...] = out[3]
            g_r, g_i = bf(dr_scr[...]), bf(di_scr[...])
            dcr_ref[0] += lax.dot_general(bf(xr_ref[...]), src_b, _TN_DIMS, preferred_element_type=F32)
            dci_ref[0] += lax.dot_general(bf(xi_ref[...]), src_b, _TN_DIMS, preferred_element_type=F32)
            _s5_interleave(u_ref, u_scr, ti)
            u_b = bf(u_scr[...])
            dbr_ref[0] += lax.dot_general(u_b, g_r, _TN_DIMS, preferred_element_type=F32)
            dbi_ref[0] += lax.dot_general(u_b, g_i, _TN_DIMS, preferred_element_type=F32)
            du = (lax.dot_general(g_r, bf(bdr_ref[0]), _NT_DIMS, preferred_element_type=F32)
                  + lax.dot_general(g_i, bf(bdi_ref[0]), _NT_DIMS, preferred_element_type=F32))
            _s5_deinterleave(du, u_scr, du_ref, ti)

            @pl.when(j == nt - 1)
            def _():
                dar_ref[...] = jnp.sum(out[2], axis=0, keepdims=True)
                dai_ref[...] = jnp.sum(out[3], axis=0, keepdims=True)

    view3 = lambda t: t.reshape(SCAN_SEGMENTS, seg_len, t.shape[1])
    spec3 = pl.BlockSpec((SCAN_SEGMENTS, ti, LANES), tmap3)
    wspec = lambda w: pl.BlockSpec((1,) + w.shape[1:], lambda c, j: (c, 0, 0))
    aspec = pl.BlockSpec((2, tn), lambda c, j: (0, c))
    fspec = pl.BlockSpec((SUBLANES, tn), lambda c, j: (0, c))
    xspec = pl.BlockSpec((tr, tn), tmap2)
    dspec = pl.BlockSpec((1, tn), lambda c, j: (0, c))
    f32 = lambda *shape: jax.ShapeDtypeStruct(shape, F32)
    args, in_specs = [view3(src), w_r, w_i, a], [spec3, wspec(w_r), wspec(w_i), aspec]
    if not first:
        args += list(finals)
        in_specs += [fspec, fspec]
    if forward:
        args += list(extra)
        in_specs += [wspec(extra[0]), wspec(extra[1])]
        out_specs, out_shape = [xspec, xspec, spec3], [f32(s, n), f32(s, n), f32(SCAN_SEGMENTS, seg_len, nb * LANES)]
    elif backward:
        x_r, x_i, u, bd_r, bd_i = extra
        args += [x_r, x_i, view3(u), bd_r, bd_i]
        in_specs += [xspec, xspec, spec3, wspec(bd_r), wspec(bd_i)]
        out_specs = [spec3, dspec, dspec, wspec(bd_r), wspec(bd_i), wspec(w_r), wspec(w_i)]
        out_shape = [f32(SCAN_SEGMENTS, seg_len, nb * LANES), f32(1, n), f32(1, n), f32(*bd_r.shape), f32(*bd_i.shape),
                     f32(*w_r.shape), f32(*w_i.shape)]
    else:
        out_specs, out_shape = [fspec, fspec], [f32(SUBLANES, n), f32(SUBLANES, n)]
    scratch = ([pltpu.VMEM((SUBLANES, tn), F32)] * 2 + [pltpu.VMEM((tr, LANES), F32)] + [pltpu.VMEM((tr, tn), F32)] * 2
               + ([pltpu.VMEM((SUBLANES, tn), F32)] * 2 + [pltpu.VMEM((tr, LANES), F32)] if backward else []))
    return pl.pallas_call(
        body, name=name, grid=(nb, nt), in_specs=in_specs, out_specs=out_specs, out_shape=out_shape,
        scratch_shapes=scratch, compiler_params=_cparams("parallel", "arbitrary"),
    )(*args)


@jax.custom_vjp
def s5_core(u, a, bd_r, bd_i, cd_r, cd_i):
    return _s5_core_fwd(u, a, bd_r, bd_i, cd_r, cd_i)[0]


def _s5_core_fwd(u, a, bd_r, bd_i, cd_r, cd_i):
    fin = _s5_pass_call(u, bd_r, bd_i, a, False, False, None, None, "s5_fwd_finals")
    x_r, x_i, y = _s5_pass_call(u, bd_r, bd_i, a, False, False, fin, (cd_r, cd_i), "s5_fwd_scan")
    return y.reshape(u.shape), (u, a, bd_r, bd_i, cd_r, cd_i, x_r, x_i)


def _s5_core_bwd(res, dy):
    u, a, bd_r, bd_i, cd_r, cd_i, x_r, x_i = res
    a_conj = a * jnp.array([[1.0], [-1.0]], F32)
    fin = _s5_pass_call(dy, cd_r, cd_i, a_conj, True, True, None, None, "s5_bwd_finals")
    du, da_r, da_i, dbd_r, dbd_i, dcd_r, dcd_i = _s5_pass_call(
        dy, cd_r, cd_i, a_conj, True, True, fin, (x_r, x_i, u, bd_r, bd_i), "s5_bwd_scan")
    return du.reshape(u.shape), jnp.concatenate([da_r, da_i], axis=0), dbd_r, dbd_i, dcd_r, dcd_i


s5_core.defvjp(_s5_core_fwd, _s5_core_bwd)


_NN = (((1,), (0,)), ((), ()))
_NT = (((1,), (1,)), ((), ()))
_TN = (((0,), (0,)), ((), ()))


def _dot(a, b, dn):
    return lax.dot_general(a.astype(BF16), b.astype(BF16), dn, preferred_element_type=F32)


@jax.custom_vjp
def bdot_nn(a, b):
    return _dot(a, b, _NN)


bdot_nn.defvjp(lambda a, b: (_dot(a, b, _NN), (a, b)),
               lambda r, g: (_dot(g, r[1], _NT).astype(r[0].dtype), _dot(r[0], g, _TN).astype(r[1].dtype)))


@jax.custom_vjp
def bdot_nt(a, b):
    return _dot(a, b, _NT)


bdot_nt.defvjp(lambda a, b: (_dot(a, b, _NT), (a, b)),
               lambda r, g: (_dot(g, r[1], _NN).astype(r[0].dtype), _dot(g, r[0], _TN).astype(r[1].dtype)))


@jax.custom_vjp
def bdot_tn(a, b):
    return _dot(a, b, _TN)


bdot_tn.defvjp(lambda a, b: (_dot(a, b, _TN), (a, b)),
               lambda r, g: (_dot(r[1], g, _NT).astype(r[0].dtype), _dot(r[0], g, _NN).astype(r[1].dtype)))


def _split3(x):
    h = x.astype(BF16)
    r = x - h.astype(F32)
    m = r.astype(BF16)
    l = (r - m.astype(F32)).astype(BF16)
    return h, m, l


def _exact_dot(t, x, dn):
    h, m, l = _split3(x)
    d = lambda p: lax.dot_general(t, p, dn, preferred_element_type=F32)
    return d(h) + d(m) + d(l)


@jax.custom_vjp
def select_dot(t, x):
    return _exact_dot(t, x, _NN)


select_dot.defvjp(lambda t, x: (_exact_dot(t, x, _NN), t),
                  lambda t, g: (jnp.zeros_like(t), _exact_dot(t, g, _TN)))


def _split_rows_impl(x, h):
    return tuple(x[i * h:(i + 1) * h] for i in range(x.shape[0] // h))


@functools.partial(jax.custom_vjp, nondiff_argnums=(1,))
def split_rows(x, h):
    return _split_rows_impl(x, h)


split_rows.defvjp(lambda x, h: (_split_rows_impl(x, h), None),
                  lambda h, r, g: (jnp.concatenate(g, axis=0),))


@jax.custom_vjp
def join_rows(parts):
    return jnp.concatenate(parts, axis=0)


def _join_rows_bwd(hs, g):
    out, off = [], 0
    for h in hs:
        out.append(g[off:off + h])
        off += h
    return (tuple(out),)


join_rows.defvjp(lambda parts: (jnp.concatenate(parts, axis=0), tuple(p.shape[0] for p in parts)), _join_rows_bwd)


def _split_lanes_impl(x, w):
    return tuple(x[:, i * w:(i + 1) * w] for i in range(x.shape[1] // w))


@functools.partial(jax.custom_vjp, nondiff_argnums=(1,))
def split_lanes(x, w):
    return _split_lanes_impl(x, w)


split_lanes.defvjp(lambda x, w: (_split_lanes_impl(x, w), None),
                   lambda w, r, g: (jnp.concatenate(g, axis=1),))


def _join_impl(parts):
    return jnp.concatenate(parts, axis=1)


@jax.custom_vjp
def join_lanes(parts):
    return _join_impl(parts)


def _join_bwd(ws, g):
    out, off = [], 0
    for w in ws:
        out.append(g[:, off:off + w])
        off += w
    return (tuple(out),)


join_lanes.defvjp(lambda parts: (_join_impl(parts), tuple(p.shape[1] for p in parts)), _join_bwd)


def _rope_impl(x, c, sa, sb, shift):
    w = x.shape[1]
    return x * c + pltpu.roll(x, w - shift, 1) * sa + pltpu.roll(x, shift, 1) * sb


@functools.partial(jax.custom_vjp, nondiff_argnums=(4,))
def rope_lanes(x, c, sa, sb, shift):
    return _rope_impl(x, c, sa, sb, shift)


def _rope_bwd(shift, r, g):
    c, sa, sb = r
    w = g.shape[1]
    dx = g * c + pltpu.roll(g * sa, shift, 1) + pltpu.roll(g * sb, w - shift, 1)
    return dx, jnp.zeros_like(c), jnp.zeros_like(sa), jnp.zeros_like(sb)


rope_lanes.defvjp(lambda x, c, sa, sb, shift: (_rope_impl(x, c, sa, sb, shift), (c, sa, sb)), _rope_bwd)


RMS_EPS = 1e-6


def _rms(x, g):
    return x * lax.rsqrt(jnp.mean(x * x, axis=-1, keepdims=True) + RMS_EPS) * g


ATTN_BLOCK = 512
MASK_VALUE = -1e30
LOG2E = math.log2(math.e)
LN2 = math.log(2.0)
V_ONES_LANE = 64


def _causal_mask(t):
    r = lax.broadcasted_iota(jnp.int32, (t, t), 0)
    c = lax.broadcasted_iota(jnp.int32, (t, t), 1)
    return c <= r


def _attn_fwd_call(q, k, v):
    s, width = q.shape
    n_heads = width // LANES
    tq = min(ATTN_BLOCK, s)
    nq = s // tq

    def body(q_ref, k_ref, v_ref, o_ref, lse_ref):
        i = pl.program_id(1)
        qb = q_ref[...].astype(BF16)
        ones_lane = lax.broadcasted_iota(jnp.int32, (tq, LANES), 1) == V_ONES_LANE

        def block(kb, carry, masked):
            m, acc = carry
            rows = pl.ds(pl.multiple_of(kb * tq, tq), tq)
            sc = lax.dot_general(qb, k_ref[rows, :].astype(BF16), _NT, preferred_element_type=F32)
            if masked:
                sc = jnp.where(_causal_mask(tq), sc, MASK_VALUE)
            m_new = jnp.maximum(m, jnp.max(sc, axis=-1, keepdims=True))
            p = jnp.exp2(sc - m_new).astype(BF16)
            vb = jnp.where(ones_lane, 1.0, v_ref[rows, :]).astype(BF16)
            acc = jnp.exp2(m - m_new) * acc + lax.dot_general(p, vb, _NN, preferred_element_type=F32)
            return m_new, acc

        init = (jnp.full((tq, 1), MASK_VALUE, F32), jnp.zeros((tq, LANES), F32))
        carry = lax.fori_loop(0, i, lambda kb, c: block(kb, c, False), init)
        m, acc = block(i, carry, True)
        l = jnp.sum(jnp.where(ones_lane, acc, 0.0), axis=-1, keepdims=True)
        o_ref[...] = jnp.where(ones_lane, 0.0, acc / l).astype(o_ref.dtype)
        lse_ref[...] = jnp.broadcast_to(m + jnp.log2(l), (tq, LANES))

    qspec = pl.BlockSpec((tq, LANES), lambda h, i: (i, h))
    kspec = pl.BlockSpec((s, LANES), lambda h, i: (0, h))
    return pl.pallas_call(
        body,
        name="mla_attn_fwd",
        grid=(n_heads, nq),
        in_specs=[qspec, kspec, kspec],
        out_specs=[qspec, qspec],
        out_shape=[jax.ShapeDtypeStruct((s, width), BF16), jax.ShapeDtypeStruct((s, width), F32)],
        compiler_params=_cparams("parallel", "parallel"),
    )(q, k, v)


def _attn_bwd_call(q, k, v, o, lse, do):
    s, width = q.shape
    n_heads = width // LANES
    tq = min(ATTN_BLOCK, s)
    nq = s // tq

    def body(q_ref, k_ref, v_ref, o_ref, lse_ref, do_ref, dq_ref, dk_ref, dv_ref, dq_acc):
        j = pl.program_id(1)

        @pl.when(j == 0)
        def _():
            dq_acc[...] = jnp.zeros_like(dq_acc)

        kb = k_ref[...].astype(BF16)
        vb = v_ref[...].astype(BF16)

        def block(i, carry, masked):
            dk, dv = carry
            rows = pl.ds(pl.multiple_of(i * tq, tq), tq)
            qi = q_ref[rows, :].astype(BF16)
            doi = do_ref[rows, :].astype(F32)
            delta = jnp.sum(doi * o_ref[rows, :].astype(F32), axis=-1, keepdims=True)
            sc = lax.dot_general(qi, kb, _NT, preferred_element_type=F32)
            if masked:
                sc = jnp.where(_causal_mask(tq), sc, MASK_VALUE)
            p = jnp.exp2(sc - lse_ref[rows, 0:1])
            dob = doi.astype(BF16)
            dv = dv + lax.dot_general(p.astype(BF16), dob, _TN, preferred_element_type=F32)
            dp = lax.dot_general(dob, vb, _NT, preferred_element_type=F32)
            ds = (p * (dp - delta)).astype(BF16)
            dq_acc[rows, :] += lax.dot_general(ds, kb, _NN, preferred_element_type=F32)
            dk = dk + lax.dot_general(ds, qi, _TN, preferred_element_type=F32)
            return dk, dv

        zero = jnp.zeros((tq, LANES), F32)
        carry = block(j, (zero, zero), True)
        dk, dv = lax.fori_loop(j + 1, nq, lambda i, c: block(i, c, False), carry)
        dk_ref[...] = (dk * LN2).astype(dk_ref.dtype)
        dv_ref[...] = dv.astype(dv_ref.dtype)

        @pl.when(j == nq - 1)
        def _():
            dq_ref[...] = (dq_acc[...] * LN2).astype(dq_ref.dtype)

    full = pl.BlockSpec((s, LANES), lambda h, j: (0, h))
    blk = pl.BlockSpec((tq, LANES), lambda h, j: (j, h))
    return pl.pallas_call(
        body,
        name="mla_attn_bwd",
        grid=(n_heads, nq),
        in_specs=[full, blk, blk, full, full, full],
        out_specs=[full, blk, blk],
        out_shape=[jax.ShapeDtypeStruct((s, width), t.dtype) for t in (q, k, v)],
        scratch_shapes=[pltpu.VMEM((s, LANES), F32)],
        compiler_params=_cparams("parallel", "arbitrary"),
    )(q, k, v, o, lse, do)


@jax.custom_vjp
def causal_attention(q, k, v):
    return _attn_fwd_call(q, k, v)[0]


def _causal_attention_fwd(q, k, v):
    o, lse = _attn_fwd_call(q, k, v)
    return o, (q, k, v, o, lse)


def _causal_attention_bwd(res, do):
    return tuple(_attn_bwd_call(*res, do))


causal_attention.defvjp(_causal_attention_fwd, _causal_attention_bwd)


HG_HEADS = 4
HG_CHUNK = 32
HG_REF_ROW = HG_CHUNK // 2 - 1
HG_TILE_ROWS = 256
HG_EXP_CLAMP = 80.0


def _hg_tile_masks(t):
    shift = HG_CHUNK.bit_length() - 1
    r = lax.broadcasted_iota(jnp.int32, (t, t), 0)
    c = lax.broadcasted_iota(jnp.int32, (t, t), 1)
    start = lax.shift_left(lax.shift_right_logical(r, shift), shift)
    causal = (c >= start) & (c <= r)
    return causal, c == start + HG_REF_ROW, c == start + (HG_CHUNK - 1)


def _hg_tile(q, fl, v, lb, st):
    t = q.shape[0]
    causal, ref_sel, last_sel = _hg_tile_masks(t)
    f = lb + (1.0 - lb) * jax.nn.sigmoid(fl)
    kk = 1.0 - f
    qs = q * jax.nn.sigmoid(q)
    b = select_dot(causal.astype(BF16), jnp.log(f))
    b_ref = select_dot(ref_sel.astype(BF16), b)
    b_last = select_dot(last_sel.astype(BF16), b)
    q_in = qs * jnp.exp(jnp.minimum(b - b_ref, HG_EXP_CLAMP))
    k_in = kk * jnp.exp(jnp.minimum(b_ref - b, HG_EXP_CLAMP))
    o = bdot_nn(jnp.where(causal, bdot_nt(q_in, k_in), 0.0), v)
    q_hat = split_rows(qs * jnp.exp(b), HG_CHUNK)
    k_hat = split_rows(kk * jnp.exp(b_last - b), HG_CHUNK)
    decay = split_rows(jnp.exp(b_last), HG_CHUNK)
    vs = split_rows(v, HG_CHUNK)
    first_row = lax.broadcasted_iota(jnp.int32, (HG_CHUNK, LANES), 0) == 0
    inter = []
    for c in range(t // HG_CHUNK):
        inter.append(bdot_nt(q_hat[c], st))
        st = st * jnp.sum(jnp.where(first_row, decay[c], 0.0), axis=0, keepdims=True) + bdot_tn(vs[c], k_hat[c])
    return o + join_rows(tuple(inter)), st


def _hg_head(q, fl, v, gate, lb, gn, st):
    o, st = _hg_tile(q, fl, v, lb, st)
    return _rms(o, gn) * (gate * jax.nn.sigmoid(gate)), st


HG_PARTS = 4


def _hg_part_slices(h, width):
    return [slice(p * width + h * LANES, p * width + (h + 1) * LANES) for p in range(HG_PARTS)]


def _hg_fwd_call(x, lb, gn):
    s = x.shape[0]
    width = x.shape[1] // HG_PARTS
    tr = min(HG_TILE_ROWS, s)
    nt = s // tr

    def body(x_ref, lb_ref, gn_ref, o_ref, sts_ref, st_ref):
        @pl.when(pl.program_id(0) == 0)
        def _():
            st_ref[...] = jnp.zeros_like(st_ref)

        for h in range(HG_HEADS):
            ln = slice(h * LANES, (h + 1) * LANES)
            st = st_ref[h]
            sts_ref[0, h] = st
            o, st_new = _hg_head(*(x_ref[:, sl] for sl in _hg_part_slices(h, width)), lb_ref[:, ln], gn_ref[...], st)
            o_ref[:, ln] = o.astype(o_ref.dtype)
            st_ref[h] = st_new

    const = lambda shape: pl.BlockSpec(shape, lambda j: (0, 0))
    return pl.pallas_call(
        body,
        name="hgrn2_fwd",
        grid=(nt,),
        in_specs=[pl.BlockSpec((tr, HG_PARTS * width), lambda j: (j, 0)), const((1, width)), const((1, LANES))],
        out_specs=[pl.BlockSpec((tr, width), lambda j: (j, 0)),
                   pl.BlockSpec((1, HG_HEADS, LANES, LANES), lambda j: (j, 0, 0, 0))],
        out_shape=[jax.ShapeDtypeStruct((s, width), BF16),
                   jax.ShapeDtypeStruct((nt, HG_HEADS, LANES, LANES), F32)],
        scratch_shapes=[pltpu.VMEM((HG_HEADS, LANES, LANES), F32)],
        compiler_params=_cparams("arbitrary"),
    )(x, lb, gn)


def _hg_bwd_call(x, lb, gn, sts, do):
    s = x.shape[0]
    width = x.shape[1] // HG_PARTS
    tr = min(HG_TILE_ROWS, s)
    nt = s // tr

    def body(x_ref, lb_ref, gn_ref, sts_ref, do_ref, dx_ref, dlb_ref, dgn_ref, dst_ref):
        @pl.when(pl.program_id(0) == 0)
        def _():
            dst_ref[...] = jnp.zeros_like(dst_ref)
            dlb_ref[...] = jnp.zeros_like(dlb_ref)
            dgn_ref[...] = jnp.zeros_like(dgn_ref)

        for h in range(HG_HEADS):
            ln = slice(h * LANES, (h + 1) * LANES)
            parts = _hg_part_slices(h, width)
            _, vjp = jax.vjp(_hg_head, *(x_ref[:, sl] for sl in parts), lb_ref[:, ln], gn_ref[...], sts_ref[0, h])
            cts = vjp((do_ref[:, ln].astype(F32), dst_ref[h]))
            for sl, ct in zip(parts, cts[:HG_PARTS]):
                dx_ref[:, sl] = ct.astype(dx_ref.dtype)
            dlb_ref[:, ln] += cts[HG_PARTS]
            dgn_ref[...] += cts[HG_PARTS + 1]
            dst_ref[h] = cts[HG_PARTS + 2]

    rev = lambda w: pl.BlockSpec((tr, w), lambda j: (nt - 1 - j, 0))
    const = lambda shape: pl.BlockSpec(shape, lambda j: (0, 0))
    return pl.pallas_call(
        body,
        name="hgrn2_bwd",
        grid=(nt,),
        in_specs=[rev(HG_PARTS * width), const((1, width)), const((1, LANES)),
                  pl.BlockSpec((1, HG_HEADS, LANES, LANES), lambda j: (nt - 1 - j, 0, 0, 0)), rev(width)],
        out_specs=[rev(HG_PARTS * width), const((1, width)), const((1, LANES))],
        out_shape=[jax.ShapeDtypeStruct(x.shape, BF16), jax.ShapeDtypeStruct((1, width), F32),
                   jax.ShapeDtypeStruct((1, LANES), F32)],
        scratch_shapes=[pltpu.VMEM((HG_HEADS, LANES, LANES), F32)],
        compiler_params=_cparams("arbitrary"),
    )(x, lb, gn, sts, do)


@jax.custom_vjp
def hgrn2_mixer(h, w, lb, gn):
    return _hg_fwd_call(_mm_call(h, w, False, False, name="hgrn2_proj"), lb, gn)[0]


def _hgrn2_mixer_fwd(h, w, lb, gn):
    x = _mm_call(h, w, False, False, name="hgrn2_proj")
    o, sts = _hg_fwd_call(x, lb, gn)
    return o, (h, w, x, lb, gn, sts)


def _hgrn2_mixer_bwd(res, do):
    h, w, x, lb, gn, sts = res
    dx, dlb, dgn = _hg_bwd_call(x, lb, gn, sts, do)
    dh = _mm_call(dx, w, False, True, out_dtype=h.dtype, name="hgrn2_proj_da")
    dw = _mm_call(h, dx, True, False, out_dtype=w.dtype, name="hgrn2_proj_db")
    return dh, dw, dlb, dgn


hgrn2_mixer.defvjp(_hgrn2_mixer_fwd, _hgrn2_mixer_bwd)


D_MODEL = 1024
DEPTH = 2
SSM_GROUPS, SSM_GROUP_CH, SSM_STATE = 32, 16, 64
SSM_WIDTH = SSM_GROUPS * SSM_GROUP_CH
MLA_HEADS, MLA_NOPE, MLA_ROPE, MLA_V = 8, 64, 32, 64
MLA_Q_RANK, MLA_KV_RANK = 512, 256
HG_WIDTH = HG_HEADS * LANES
X_HEADS, X_HEAD_DIM = 4, 128
X_WIDTH = X_HEADS * X_HEAD_DIM
D_FF = 2816
ROPE_THETA = 10000.0
IN_SPLITS = (SSM_WIDTH, MLA_Q_RANK, MLA_KV_RANK, MLA_ROPE, HG_WIDTH, HG_WIDTH, HG_WIDTH, HG_WIDTH, 3 * D_MODEL)
ROPE_LANE0 = MLA_NOPE
MLA_Q_SCALE = LOG2E / math.sqrt(MLA_NOPE + MLA_ROPE)
ROW_TILE = 512
MEM_ROW_TILE = 256


def _t_rms(x, g):
    return (_rms(x, g).astype(BF16),)


def _t_s5_act(y, u, d):
    return (jax.nn.gelu(y + d * u).astype(BF16),)


def _t_glu(z):
    zo, zg = split_lanes(z.astype(F32), D_MODEL)
    return ((zo * jax.nn.sigmoid(zg)).astype(BF16),)


def _t_mla_rope(q, k, kr, c, sa, sb):
    rep = lambda t: jnp.concatenate([t] * MLA_HEADS, axis=1)
    half = MLA_ROPE // 2
    q_out = rope_lanes(q, rep(c), rep(sa), rep(sb), half) * MLA_Q_SCALE
    kr_out = rope_lanes(kr, c, sa, sb, half)
    return q_out.astype(BF16), (k + join_lanes((kr_out,) * MLA_HEADS)).astype(BF16)


def _t_merge(y_ssm, y_mla, y_hg, gates):
    g0, g1, g2 = split_lanes(gates.astype(F32), D_MODEL)
    mix = (jax.nn.sigmoid(g0) * y_ssm.astype(F32) + jax.nn.sigmoid(g1) * y_mla.astype(F32)
           + jax.nn.sigmoid(g2) * y_hg.astype(F32))
    return (mix.astype(BF16),)


def _t_xattn(q, kv):
    scale = 1.0 / math.sqrt(X_HEAD_DIM)
    heads = split_lanes(kv, X_HEAD_DIM)
    outs = []
    for qh, kh, vh in zip(split_lanes(q, X_HEAD_DIM), heads[:X_HEADS], heads[X_HEADS:]):
        sc = bdot_nt(qh, kh) * scale
        p = jnp.exp(sc - jnp.max(sc, axis=-1, keepdims=True))
        p = p / jnp.sum(p, axis=-1, keepdims=True)
        outs.append(bdot_nn(p, vh))
    return (join_lanes(tuple(outs)).astype(BF16),)


def _t_swiglu(gate_up):
    gt, up = split_lanes(gate_up.astype(F32), D_FF)
    return ((gt * jax.nn.sigmoid(gt) * up).astype(BF16),)


def _t_loss(x, tgt, g):
    e = _rms(x, g) - tgt
    return (jnp.broadcast_to(jnp.mean(e * e, axis=-1, keepdims=True), (x.shape[0], LANES)),)


rms_op = rowwise(_t_rms, 1, 0, ROW_TILE, "rmsnorm")
rms_mem_op = rowwise(_t_rms, 1, 0, MEM_ROW_TILE, "rmsnorm_mem")
rms_res_op = rowwise(_t_rms, 1, 0, ROW_TILE, "rmsnorm_res", passthrough=True)
s5_act_op = rowwise(_t_s5_act, 2, 0, ROW_TILE, "s5_act")
glu_op = rowwise(_t_glu, 1, 0, ROW_TILE, "glu")
mla_rope_op = rowwise(_t_mla_rope, 3, 3, ROW_TILE, "mla_rope")
merge_op = rowwise(_t_merge, 4, 0, ROW_TILE, "merge")
xattn_op = rowwise(_t_xattn, 1, 0, ROW_TILE, "xattn")
swiglu_op = rowwise(_t_swiglu, 1, 0, ROW_TILE, "swiglu")
loss_op = rowwise(_t_loss, 1, 1, ROW_TILE, "loss")


def _rope_tables(positions):
    half = MLA_ROPE // 2
    inv_freq = ROPE_THETA ** (-jnp.arange(half, dtype=F32) / half)
    ang = positions.astype(F32)[:, None] * inv_freq
    cos, sin = jnp.cos(ang), jnp.sin(ang)
    s = positions.shape[0]
    z = lambda w: jnp.zeros((s, w), F32)
    tail = LANES - ROPE_LANE0 - MLA_ROPE
    c = jnp.concatenate([jnp.ones((s, ROPE_LANE0), F32), cos, cos, z(tail)], axis=1)
    sa = jnp.concatenate([z(ROPE_LANE0), -sin, z(half), z(tail)], axis=1)
    sb = jnp.concatenate([z(ROPE_LANE0), z(half), sin, z(tail)], axis=1)
    return c, sa, sb


def _s5_operators(lam_re, lam_im, b_re, b_im, c_re, c_im, log_step):
    g, p, h = SSM_GROUPS, SSM_STATE, SSM_GROUP_CH
    lam = lax.complex(lam_re, lam_im)
    lam_bar = jnp.exp(lam * jnp.exp(log_step)[:, None])
    b_bar = ((lam_bar - 1.0) / lam)[..., None] * lax.complex(b_re, b_im)
    per = LANES // h
    nb = g // per
    eye = jnp.eye(per, dtype=F32)
    bd = lambda t: jnp.einsum("jgph,gk->jghkp", t.reshape(nb, per, p, h), eye).reshape(nb, per * h, per * p)
    cd = lambda t: jnp.einsum("jghp,gk->jgpkh", t.reshape(nb, per, h, p), eye).reshape(nb, per * p, per * h)
    a = jnp.stack([jnp.real(lam_bar).reshape(-1), jnp.imag(lam_bar).reshape(-1)])
    return a, bd(jnp.real(b_bar)), bd(jnp.imag(b_bar)), cd(c_re), cd(-c_im)


LATENT_WIDTH = 1536
_LATENT = {}
_off = 0
for _name, _w in (("u", SSM_WIDTH), ("q_lat", MLA_Q_RANK), ("kv_lat", MLA_KV_RANK), ("k_rope", LANES)):
    _LATENT[_name] = (_off, _off + _w)
    _off += _w


def _layer_matrices(w, l):
    return {**_mixer_matrices(w, l), **_tail_matrices(w, l)}


def _tail_matrices(w, l):
    return dict(x_q=w["x_w_q"][l], x_kv=w["x_w_kv"][l], x_o=w["x_w_o"][l], ffn_gu=w["ffn_w_gate_up"][l],
                ffn_d=w["ffn_w_down"][l])


def _mixer_matrices(w, l):
    w_in = w["w_in"][l]
    d, dt = w_in.shape[0], w_in.dtype
    z = lambda n: jnp.zeros((d, n), dt)
    r0 = SSM_WIDTH + MLA_Q_RANK + MLA_KV_RANK
    r1 = r0 + MLA_ROPE
    r2 = r1 + HG_PARTS * HG_WIDTH
    w_latent = jnp.concatenate([w_in[:, :r0], z(ROPE_LANE0), w_in[:, r0:r1],
                                z(LATENT_WIDTH - r0 - ROPE_LANE0 - MLA_ROPE)], axis=1)
    pad_heads = lambda t: jnp.pad(t, ((0, 0), (0, 0), (0, LANES - t.shape[2]))).reshape(t.shape[0], -1)
    uq = w["mla_w_uq"][l].reshape(MLA_Q_RANK, MLA_HEADS, MLA_NOPE + MLA_ROPE)
    ukv = w["mla_w_ukv"][l].reshape(MLA_KV_RANK, MLA_HEADS, MLA_NOPE + MLA_V)
    wo = w["mla_w_o"][l].reshape(MLA_HEADS, MLA_V, D_MODEL)
    return dict(
        w_latent=w_latent, w_hg=w_in[:, r1:r2], w_gates=w_in[:, r2:], glu=w["ssm_w_glu"][l],
        uq=pad_heads(uq), uk=pad_heads(ukv[:, :, :MLA_NOPE]), uv=pad_heads(ukv[:, :, MLA_NOPE:]),
        mla_o=jnp.pad(wo, ((0, 0), (0, LANES - MLA_V), (0, 0))).reshape(MLA_HEADS * LANES, D_MODEL),
        hg_o=w["hg_w_o"][l], w_out=w["w_out"][l])


def _layer(x, mem, tabs, m, sp, l, lower_bound):
    return _tail(_mixer(x, tabs, m, sp, l, lower_bound), mem, m, sp, l)


def _mixer(x, tabs, m, sp, l, lower_bound):
    row = lambda name: sp[name][l].reshape(1, -1)
    h, x = rms_res_op(x, row("norm_mix"))
    latent = matmul(h, m["w_latent"])
    seg = lambda name: latent[:, _LATENT[name][0]:_LATENT[name][1]]
    a, bd_r, bd_i, cd_r, cd_i = _s5_operators(*(sp[n][l] for n in (
        "ssm_lam_re", "ssm_lam_im", "ssm_b_re", "ssm_b_im", "ssm_c_re", "ssm_c_im", "ssm_log_step")))
    u = seg("u")
    y = s5_core(u, a, bd_r, bd_i, cd_r, cd_i)
    (ya,) = s5_act_op(y, u, row("ssm_d"))
    (y_ssm,) = glu_op(matmul(ya, m["glu"], BF16))
    (qn,) = rms_op(seg("q_lat"), row("mla_q_norm"))
    (kvn,) = rms_op(seg("kv_lat"), row("mla_kv_norm"))
    q, k = mla_rope_op(matmul(qn, m["uq"]), matmul(kvn, m["uk"]), seg("k_rope"), *tabs)
    o = causal_attention(q, k, matmul(kvn, m["uv"], BF16))
    y_mla = matmul(o, m["mla_o"], BF16)
    y_hg = matmul(hgrn2_mixer(h, m["w_hg"], lower_bound, row("hg_g_norm")), m["hg_o"], BF16)
    (merged,) = merge_op(y_ssm, y_mla, y_hg, matmul(h, m["w_gates"], BF16))
    return matmul_add(merged, m["w_out"], x)


def _tail(x, mem, m, sp, l):
    row = lambda name: sp[name][l].reshape(1, -1)
    hc, x = rms_res_op(x, row("norm_cross"))
    (mn,) = rms_mem_op(mem, row("norm_mem"))
    (ox,) = xattn_op(matmul(hc, m["x_q"], BF16), matmul(mn, m["x_kv"]))
    x = matmul_add(ox, m["x_o"], x)
    hf, x = rms_res_op(x, row("norm_ffn"))
    (act,) = swiglu_op(matmul(hf, m["ffn_gu"], BF16))
    return matmul_add(act, m["ffn_d"], x)


def _lower_bounds(hg_lb):
    lb_p = jax.nn.softmax(hg_lb, axis=0)
    return jnp.cumsum(lb_p, axis=0) - lb_p[0:1]


def _final_loss(target, x, norm_final):
    (row_loss,) = loss_op(x, target, norm_final.reshape(1, -1))
    return 0.5 * jnp.sum(row_loss[:, 0])


def _local_loss(x, mem, positions, target, w, sp):
    tabs = _rope_tables(positions)
    lower = _lower_bounds(sp["hg_lb"])
    for l in range(DEPTH):
        x = _layer(x, mem, tabs, _layer_matrices(w, l), sp, l, lower[l].reshape(1, -1))
    return _final_loss(target, x, sp["norm_final"])


N_DEV = 8
N_CHIPS = 4
COMM_LANES = 512
MESH_ID = pl.DeviceIdType.MESH
_ANY = pl.BlockSpec(memory_space=pl.ANY)
_OTHER_CHIPS = ((1, 0), (0, 1), (1, 1))


def _place():
    return lax.axis_index("x"), lax.axis_index("y"), lax.axis_index("c")


def _all_gather_call(blocks, name):
    n = len(blocks)

    def body(*refs):
        x_refs, out_refs = refs[:n], refs[n:2 * n]
        send_sems, recv_sems, local_sems = refs[2 * n:]
        x, y, c = _place()
        me, sibling = (x, y, c), (x, y, 1 - c)
        chips = [(x ^ fx, y ^ fy) for fx, fy in _OTHER_CHIPS]

        def slot(i, px, py, pc):
            return out_refs[i].at[4 * px + 2 * py + pc]

        def copy(i, k, blk, to, src=None):
            return pltpu.make_async_remote_copy(
                src_ref=slot(i, *blk) if src is None else src, dst_ref=slot(i, *blk),
                send_sem=send_sems.at[i, k], recv_sem=recv_sems.at[i, k], device_id=to, device_id_type=MESH_ID)

        mine = [pltpu.make_async_copy(x_refs[i], slot(i, *me), local_sems.at[i]) for i in range(n)]
        first = []
        for i in range(n):
            first.append(copy(i, 0, me, sibling, src=x_refs[i]))
            first += [copy(i, 1 + j, me, (*chip, c), src=x_refs[i]) for j, chip in enumerate(chips)]
        for cp in mine + first:
            cp.start()
        passed = []
        for j, chip in enumerate(chips):
            for i in range(n):
                copy(i, 1 + j, (*chip, c), me).wait_recv()
                passed.append(copy(i, 4 + j, (*chip, c), sibling))
                passed[-1].start()
        for i in range(n):
            copy(i, 0, sibling, me).wait_recv()
            for j, chip in enumerate(chips):
                copy(i, 4 + j, (*chip, 1 - c), me).wait_recv()
        for cp in first + passed:
            cp.wait_send()
        for cp in mine:
            cp.wait()

    return pl.pallas_call(
        body,
        name=name,
        out_shape=[jax.ShapeDtypeStruct((N_DEV,) + b.shape, b.dtype) for b in blocks],
        in_specs=[_ANY] * n,
        out_specs=[_ANY] * n,
        scratch_shapes=[pltpu.SemaphoreType.DMA((n, 7)), pltpu.SemaphoreType.DMA((n, 7)), pltpu.SemaphoreType.DMA((n,))],
    )(*blocks)


def _pair_exchange_call(gs, name):
    n = len(gs)

    def body(*refs):
        g_refs, got_refs = refs[:n], refs[n:2 * n]
        send_sems, recv_sems = refs[2 * n:]
        x, y, c = _place()
        sends = [pltpu.make_async_remote_copy(
            src_ref=g_refs[i].at[2 * p + (1 - c)], dst_ref=got_refs[i].at[p],
            send_sem=send_sems.at[i, p], recv_sem=recv_sems.at[i, p], device_id=(x, y, 1 - c), device_id_type=MESH_ID)
            for i in range(n) for p in range(N_CHIPS)]
        for cp in sends:
            cp.start()
        for cp in sends:
            cp.wait_recv()
        for cp in sends:
            cp.wait_send()

    return pl.pallas_call(
        body,
        name=name,
        out_shape=[jax.ShapeDtypeStruct((N_CHIPS,) + g.shape[1:], g.dtype) for g in gs],
        in_specs=[_ANY] * n,
        out_specs=[_ANY] * n,
        scratch_shapes=[pltpu.SemaphoreType.DMA((n, N_CHIPS))] * 2,
    )(*gs)


def _chip_exchange_call(parts, name):
    n = len(parts)

    def body(*refs):
        p_refs, got_refs = refs[:n], refs[n:2 * n]
        send_sems, recv_sems = refs[2 * n:]
        x, y, c = _place()
        sends = []
        for i in range(n):
            for k, (fx, fy) in enumerate(_OTHER_CHIPS):
                px, py = x ^ fx, y ^ fy
                sends.append(pltpu.make_async_remote_copy(
                    src_ref=p_refs[i].at[2 * px + py], dst_ref=got_refs[i].at[k],
                    send_sem=send_sems.at[i, k], recv_sem=recv_sems.at[i, k], device_id=(px, py, c), device_id_type=MESH_ID))
        for cp in sends:
            cp.start()
        for cp in sends:
            cp.wait_recv()
        for cp in sends:
            cp.wait_send()

    return pl.pallas_call(
        body,
        name=name,
        out_shape=[jax.ShapeDtypeStruct((3,) + p.shape[1:], p.dtype) for p in parts],
        in_specs=[_ANY] * n,
        out_specs=[_ANY] * n,
        scratch_shapes=[pltpu.SemaphoreType.DMA((n, 3))] * 2,
    )(*parts)


def _rows_cols(shape):
    return math.prod(shape[:-1]), shape[-1]


def _pair_sum_call(g, got, c_idx, name):
    rows, cols = _rows_cols(got.shape[1:])
    tr = _pick_tile(rows, (512, 256, 128, 64, 32, 16))

    def body(c_ref, a_ref, b_ref, o_ref):
        o_ref[...] = (a_ref[...].astype(F32) + b_ref[...].astype(F32)).astype(o_ref.dtype)

    spec = pl.BlockSpec((1, tr, cols), lambda p, i, c_ref: (p, i, 0))
    out = pl.pallas_call(
        body,
        name=name,
        grid_spec=pltpu.PrefetchScalarGridSpec(
            num_scalar_prefetch=1, grid=(N_CHIPS, rows // tr),
            in_specs=[pl.BlockSpec((1, tr, cols), lambda p, i, c_ref: (2 * p + c_ref[0], i, 0)), spec],
            out_specs=spec),
        out_shape=jax.ShapeDtypeStruct((N_CHIPS, rows, cols), got.dtype),
        compiler_params=_cparams("parallel", "parallel"),
    )(c_idx, g.reshape(N_DEV, rows, cols), got.reshape(N_CHIPS, rows, cols))
    return out.reshape(got.shape)


def _chip_sum_call(part, got, chip_idx, name):
    rows, cols = _rows_cols(got.shape[1:])
    tr = _pick_tile(rows, (512, 256, 128, 64, 32, 16))

    def body(p_ref, a_ref, b_ref, o_ref):
        acc = a_ref[0].astype(F32)
        for k in range(3):
            acc = acc + b_ref[k].astype(F32)
        o_ref[...] = acc

    out = pl.pallas_call(
        body,
        name=name,
        grid_spec=pltpu.PrefetchScalarGridSpec(
            num_scalar_prefetch=1, grid=(rows // tr,),
            in_specs=[pl.BlockSpec((1, tr, cols), lambda i, p_ref: (p_ref[0], i, 0)),
                      pl.BlockSpec((3, tr, cols), lambda i, p_ref: (0, i, 0))],
            out_specs=pl.BlockSpec((tr, cols), lambda i, p_ref: (i, 0))),
        out_shape=jax.ShapeDtypeStruct((rows, cols), F32),
        compiler_params=_cparams("parallel"),
    )(chip_idx, part.reshape(N_CHIPS, rows, cols), got.reshape(3, rows, cols))
    return out.reshape(got.shape[1:])


def _reduce_scatter(gs, name):
    x, y, c = _place()
    c_idx = c.astype(jnp.int32).reshape(1)
    chip_idx = (2 * x + y).astype(jnp.int32).reshape(1)
    gots = _pair_exchange_call(gs, name + "_pair")
    parts = [_pair_sum_call(g, got, c_idx, name + "_pair_sum") for g, got in zip(gs, gots)]
    gots = _chip_exchange_call(parts, name + "_chip")
    return [_chip_sum_call(p, got, chip_idx, name + "_chip_sum") for p, got in zip(parts, gots)]


_HBM = pl.BlockSpec(memory_space=pltpu.HBM)
_SEM = pl.BlockSpec(memory_space=pltpu.SEMAPHORE)
_SIDE_EFFECT = pltpu.SideEffectType.DATAFLOW_SIDE_EFFECTING
N_PEERS = N_DEV - 1


def _peer(k):
    x, y, c = _place()
    px, py, pc = x ^ ((k >> 2) & 1), y ^ ((k >> 1) & 1), c ^ (k & 1)
    return (px, py, pc), 4 * px + 2 * py + pc


def _exchange_copy(src_ref, land_ref, send_sems, recv_sems, i, k, scatter, receiving):
    x, y, c = _place()
    me = 4 * x + 2 * y + c
    peer, peer_idx = _peer(k)
    sem = i * N_PEERS + k - 1
    return pltpu.make_async_remote_copy(
        src_ref=src_ref.at[peer_idx] if scatter else src_ref, dst_ref=land_ref.at[peer_idx if receiving else me],
        send_sem=send_sems.at[sem], recv_sem=recv_sems.at[sem], device_id=peer, device_id_type=MESH_ID)


def _exchange_start_call(srcs, after, scatter, name):
    n = len(srcs)
    slot_shapes = [s.shape[1:] if scatter else s.shape for s in srcs]

    def body(*refs):
        src_refs, land_refs = refs[:n], refs[n:2 * n]
        send_sems, recv_sems = refs[2 * n + 1], refs[2 * n + 2]
        token = refs[-1]
        for i in range(n):
            for k in range(1, N_DEV):
                _exchange_copy(src_refs[i], land_refs[i], send_sems, recv_sems, i, k, scatter, False).start()
        token[...] = jnp.zeros_like(token)

    lands = [pltpu.with_memory_space_constraint(lax.empty((N_DEV,) + shp, s.dtype), pltpu.HBM)
             for shp, s in zip(slot_shapes, srcs)]
    out = pl.pallas_call(
        body,
        name=name,
        out_shape=([pltpu.SemaphoreType.DMA((n * N_PEERS,)), pltpu.SemaphoreType.DMA((n * N_PEERS,))]
                   + [pltpu.HBM(s.shape, s.dtype) for s in srcs] + [pltpu.HBM(l.shape, l.dtype) for l in lands]
                   + [jax.ShapeDtypeStruct((SUBLANES, LANES), F32)]),
        in_specs=[_HBM] * (2 * n) + [pl.BlockSpec(memory_space=pl.ANY)],
        out_specs=[_SEM, _SEM] + [_HBM] * (2 * n) + [pl.BlockSpec(memory_space=pltpu.VMEM)],
        input_output_aliases={j: 2 + j for j in range(2 * n)},
        compiler_params=pltpu.CompilerParams(has_side_effects=_SIDE_EFFECT),
    )(*[pltpu.with_memory_space_constraint(s, pltpu.HBM) for s in srcs], *lands, after)
    return out[0], out[1], list(out[2:2 + n]), list(out[2 + n:2 + 2 * n]), out[-1]


def _exchange_wait_call(started, after, scatter, name):
    send_sems, recv_sems, srcs, lands, _ = started
    n = len(srcs)

    def body(*refs):
        src_refs, land_refs = refs[:n], refs[n:2 * n]
        send_s, recv_s = refs[2 * n], refs[2 * n + 1]
        for i in range(n):
            for k in range(1, N_DEV):
                cp = _exchange_copy(src_refs[i], land_refs[i], send_s, recv_s, i, k, scatter, True)
                cp.wait_send()
                cp.wait_recv()

    out = pl.pallas_call(
        body,
        name=name,
        out_shape=[pltpu.HBM(s.shape, s.dtype) for s in srcs] + [pltpu.HBM(l.shape, l.dtype) for l in lands],
        in_specs=[_HBM] * (2 * n) + [_SEM, _SEM, pl.BlockSpec(memory_space=pl.ANY)],
        out_specs=[_HBM] * (2 * n),
        input_output_aliases={j: j for j in range(2 * n)},
        compiler_params=pltpu.CompilerParams(has_side_effects=_SIDE_EFFECT),
    )(*srcs, *lands, send_sems, recv_sems, after)
    return list(out[n:])


def _own_slot(land, own):
    x, y, c = _place()
    return lax.dynamic_update_index_in_dim(land, own, 4 * x + 2 * y + c, 0)


def _slot_sum_call(land, name):
    rows, cols = _rows_cols(land.shape[1:])
    tr = _pick_tile(rows, (256, 128, 64, 32, 16))

    def body(land_ref, o_ref):
        acc = land_ref[0].astype(F32)
        for s in range(1, N_DEV):
            acc = acc + land_ref[s].astype(F32)
        o_ref[...] = acc

    out = pl.pallas_call(
        body,
        name=name,
        grid=(rows // tr,),
        in_specs=[pl.BlockSpec((N_DEV, tr, cols), lambda i: (0, i, 0))],
        out_specs=pl.BlockSpec((tr, cols), lambda i: (i, 0)),
        out_shape=jax.ShapeDtypeStruct((rows, cols), F32),
        compiler_params=_cparams("parallel"),
    )(land.reshape(N_DEV, rows, cols))
    return out.reshape(land.shape[1:])


SMALL_BLOCK_ROWS = 16


def _pack_small(parts):
    flat = jnp.concatenate([p.reshape(-1) for p in parts])
    chunk = N_DEV * SMALL_BLOCK_ROWS * COMM_LANES
    flat = jnp.pad(flat, (0, (-flat.shape[0]) % chunk))
    return flat.reshape(N_DEV, -1, COMM_LANES)


def _unpack_small(buf, shapes):
    flat = buf.reshape(-1)
    out, off = [], 0
    for shp in shapes:
        n = math.prod(shp)
        out.append(flat[off:off + n].reshape(shp))
        off += n
    return out


SHARDED = dict(w_in=2, ssm_w_glu=2, mla_w_uq=2, mla_w_ukv=2, mla_w_o=2, hg_w_o=2, w_out=1, x_w_q=1, x_w_kv=1,
               x_w_o=2, ffn_w_gate_up=2, ffn_w_down=1)
REPLICATED = ("norm_mix", "ssm_lam_re", "ssm_lam_im", "ssm_b_re", "ssm_b_im", "ssm_c_re", "ssm_c_im", "ssm_d",
              "ssm_log_step", "mla_q_norm", "mla_kv_norm", "hg_lb", "hg_g_norm", "norm_cross", "norm_mem", "norm_ffn",
              "norm_final")


def _join_shards(stacked, axis):
    n, l, a, b = stacked.shape
    if axis == 1:
        return stacked.transpose(1, 0, 2, 3).reshape(l, n * a, b)
    return stacked.transpose(1, 2, 0, 3).reshape(l, a, n * b)


def _split_shards(full, axis):
    l, a, b = full.shape
    if axis == 1:
        return full.reshape(l, N_DEV, a // N_DEV, b).transpose(1, 0, 2, 3)
    return full.reshape(l, a, N_DEV, b // N_DEV).transpose(2, 0, 1, 3)


MIXER_WEIGHTS = ("w_in", "ssm_w_glu", "mla_w_uq", "mla_w_ukv", "mla_w_o", "hg_w_o", "w_out")
TAIL_WEIGHTS = ("x_w_q", "x_w_kv", "x_w_o", "ffn_w_gate_up", "ffn_w_down")


def _mixer_fn(l, tabs):
    def f(x, full, small, lower):
        m = _mixer_matrices(dict(zip(MIXER_WEIGHTS, full)), 0)
        return _mixer(x, tabs, m, dict(zip(REPLICATED, small)), l, lower[l].reshape(1, -1))
    return f


def _tail_fn(l, mem):
    def f(x, full, small):
        return _tail(x, mem, _tail_matrices(dict(zip(TAIL_WEIGHTS, full)), 0), dict(zip(REPLICATED, small)), l)
    return f


ADAM_LR, ADAM_B1, ADAM_B2, ADAM_EPS, ADAM_WD, ADAM_STEP = 0.001, 0.9, 0.999, 1e-08, 0.01, 10


def _adamw_update(w, g, m, v):
    m_new = ADAM_B1 * m + (1.0 - ADAM_B1) * g
    v_new = ADAM_B2 * v + (1.0 - ADAM_B2) * jnp.square(g)
    m_hat = m_new / (1.0 - ADAM_B1 ** ADAM_STEP)
    v_hat = v_new / (1.0 - ADAM_B2 ** ADAM_STEP)
    return -ADAM_LR * (m_hat / (jnp.sqrt(v_hat) + ADAM_EPS) + ADAM_WD * w), m_new, v_new


def _adamw_stacked_call(w, g, m, v, name):
    depth, rows, cols = w.shape
    tr = _pick_tile(rows, (512, 256, 128, 64, 32, 16, 8))

    def body(w_ref, g_ref, m_ref, v_ref, d_ref, nm_ref, nv_ref):
        d_ref[...], nm_ref[...], nv_ref[...] = _adamw_update(w_ref[...], g_ref[...], m_ref[...], v_ref[...])

    spec = pl.BlockSpec((None, tr, cols), lambda l, i: (l, i, 0))
    return tuple(pl.pallas_call(
        body, name=name, grid=(depth, rows // tr), in_specs=[spec] * 4, out_specs=[spec] * 3,
        out_shape=[jax.ShapeDtypeStruct(w.shape, F32)] * 3, compiler_params=_cparams("parallel", "parallel"),
    )(w, g, m, v))


def _adamw_call(w, g, m, v, name):
    shape = w.shape
    if len(shape) == 3:
        return _adamw_stacked_call(w, g, m, v, name)
    cols = shape[-1]
    rows = math.prod(shape[:-1]) if len(shape) > 1 else 1
    tr = _pick_tile(rows, (512, 256, 128, 64, 32, 16, 8))

    def body(w_ref, g_ref, m_ref, v_ref, d_ref, nm_ref, nv_ref):
        d_ref[...], nm_ref[...], nv_ref[...] = _adamw_update(w_ref[...], g_ref[...], m_ref[...], v_ref[...])

    spec = pl.BlockSpec((tr, cols), lambda i: (i, 0))
    outs = pl.pallas_call(
        body, name=name, grid=(rows // tr,), in_specs=[spec] * 4, out_specs=[spec] * 3,
        out_shape=[jax.ShapeDtypeStruct((rows, cols), F32)] * 3, compiler_params=_cparams("parallel"),
    )(*(t.reshape(rows, cols) for t in (w, g, m, v)))
    return tuple(o.reshape(shape) for o in outs)


WEIGHTS = ("norm_mix", "w_in", "ssm_lam_re", "ssm_lam_im", "ssm_b_re", "ssm_b_im", "ssm_c_re", "ssm_c_im", "ssm_d",
           "ssm_log_step", "ssm_w_glu", "mla_q_norm", "mla_kv_norm", "mla_w_uq", "mla_w_ukv", "mla_w_o", "hg_lb",
           "hg_g_norm", "hg_w_o", "w_out", "norm_cross", "norm_mem", "x_w_q", "x_w_kv", "x_w_o", "norm_ffn",
           "ffn_w_gate_up", "ffn_w_down", "norm_final")


def kernel(x, mem, positions, norm_mix, w_in, ssm_lam_re, ssm_lam_im, ssm_b_re, ssm_b_im, ssm_c_re, ssm_c_im, ssm_d, ssm_log_step, ssm_w_glu, mla_q_norm, mla_kv_norm, mla_w_uq, mla_w_ukv, mla_w_o, hg_lb, hg_g_norm, hg_w_o, w_out, norm_cross, norm_mem, x_w_q, x_w_kv, x_w_o, norm_ffn, ffn_w_gate_up, ffn_w_down, norm_final, loss_target, m_norm_mix, m_w_in, m_ssm_lam_re, m_ssm_lam_im, m_ssm_b_re, m_ssm_b_im, m_ssm_c_re, m_ssm_c_im, m_ssm_d, m_ssm_log_step, m_ssm_w_glu, m_mla_q_norm, m_mla_kv_norm, m_mla_w_uq, m_mla_w_ukv, m_mla_w_o, m_hg_lb, m_hg_g_norm, m_hg_w_o, m_w_out, m_norm_cross, m_norm_mem, m_x_w_q, m_x_w_kv, m_x_w_o, m_norm_ffn, m_ffn_w_gate_up, m_ffn_w_down, m_norm_final, v_norm_mix, v_w_in, v_ssm_lam_re, v_ssm_lam_im, v_ssm_b_re, v_ssm_b_im, v_ssm_c_re, v_ssm_c_im, v_ssm_d, v_ssm_log_step, v_ssm_w_glu, v_mla_q_norm, v_mla_kv_norm, v_mla_w_uq, v_mla_w_ukv, v_mla_w_o, v_hg_lb, v_hg_g_norm, v_hg_w_o, v_w_out, v_norm_cross, v_norm_mem, v_x_w_q, v_x_w_kv, v_x_w_o, v_norm_ffn, v_ffn_w_gate_up, v_ffn_w_down, v_norm_final):
    given = dict(locals())
    weights = {n: given[n] for n in WEIGHTS}
    small = tuple(weights[n] for n in REPLICATED)
    layer1 = MIXER_WEIGHTS + TAIL_WEIGHTS
    shards = lambda names, l: [weights[n][l:l + 1].astype(BF16) for n in names]
    join = lambda names, stacked: tuple(_join_shards(p, SHARDED[n]) for n, p in zip(names, stacked))
    split = lambda names, cts: [_split_shards(ct, SHARDED[n]) for n, ct in zip(names, cts)]
    landed = lambda names, lands, own: join(names, [_own_slot(land, o) for land, o in zip(lands, own)])
    xs, tabs = x[0], _rope_tables(positions[0])
    lower, vjp_lower = jax.vjp(_lower_bounds, hg_lb)
    me = 4 * lax.axis_index("x") + 2 * lax.axis_index("y") + lax.axis_index("c")

    got_m0 = _all_gather_call(shards(MIXER_WEIGHTS, 0), "weights_all_gather_m0")
    own_t0, own_l1 = shards(TAIL_WEIGHTS, 0), shards(layer1, 1)
    gather_t0 = _exchange_start_call(own_t0, got_m0[0], False, "weights_gather_start_t0")
    gather_l1 = _exchange_start_call(own_l1, gather_t0[4], False, "weights_gather_start_l1")
    xs = xs + gather_l1[4][0, 0]
    xa0, vjp_m0 = jax.vjp(_mixer_fn(0, tabs), xs, join(MIXER_WEIGHTS, got_m0), small, lower)
    full_t0 = landed(TAIL_WEIGHTS, _exchange_wait_call(gather_t0, xa0, False, "weights_gather_wait_t0"), own_t0)
    x1, vjp_t0 = jax.vjp(_tail_fn(0, mem[0]), xa0, full_t0, small)
    full_l1 = landed(layer1, _exchange_wait_call(gather_l1, x1, False, "weights_gather_wait_l1"), own_l1)
    xa1, vjp_m1 = jax.vjp(_mixer_fn(1, tabs), x1, full_l1[:len(MIXER_WEIGHTS)], small, lower)
    x2, vjp_t1 = jax.vjp(_tail_fn(1, mem[0]), xa1, full_l1[len(MIXER_WEIGHTS):], small)
    loss_local, vjp_loss = jax.vjp(functools.partial(_final_loss, loss_target[0]), x2, norm_final)

    def scatter_start(names, cts, dx, tag):
        gs = split(names, cts)
        started = _exchange_start_call(gs, dx, True, "grads_scatter_start_" + tag)
        return (started, gs), dx + started[4][0, 0]

    def scatter_finish(pending, after, tag):
        started, gs = pending
        lands = _exchange_wait_call(started, after, True, "grads_scatter_wait_" + tag)
        return [_slot_sum_call(_own_slot(land, lax.dynamic_index_in_dim(g, me, 0, keepdims=False)), "grads_slot_sum_" + tag)
                for land, g in zip(lands, gs)]

    dx2, d_norm_final = vjp_loss(jnp.ones((), F32))
    dxa1, dfull_t1, dsmall_t1 = vjp_t1(dx2)
    pend_t1, dxa1 = scatter_start(TAIL_WEIGHTS, dfull_t1, dxa1, "t1")
    dx1, dfull_m1, dsmall_m1, dlower1 = vjp_m1(dxa1)
    pend_m1, dx1 = scatter_start(MIXER_WEIGHTS, dfull_m1, dx1, "m1")
    dxa0, dfull_t0, dsmall_t0 = vjp_t0(dx1)
    pend_t0, dxa0 = scatter_start(TAIL_WEIGHTS, dfull_t0, dxa0, "t0")
    gx, dfull_m0, dsmall_m0, dlower0 = vjp_m0(dxa0)
    by_layer = {
        0: dict(zip(MIXER_WEIGHTS + TAIL_WEIGHTS,
                    _reduce_scatter(split(MIXER_WEIGHTS, dfull_m0), "grads_reduce_scatter_m0") + scatter_finish(pend_t0, gx, "t0"))),
        1: dict(zip(layer1, scatter_finish(pend_m1, gx, "m1") + scatter_finish(pend_t1, gx, "t1")))}
    grads = {n: jnp.concatenate([by_layer[0][n], by_layer[1][n]], axis=0) for n in SHARDED}

    d_small = dict(zip(REPLICATED, (a + b + c + d for a, b, c, d in zip(dsmall_m0, dsmall_t0, dsmall_m1, dsmall_t1))))
    d_small["norm_final"] = d_small["norm_final"] + d_norm_final
    d_small["hg_lb"] = d_small["hg_lb"] + vjp_lower(dlower0 + dlower1)[0]
    shapes = [d_small[n].shape for n in REPLICATED]
    (mine,) = _reduce_scatter([_pack_small([d_small[n] for n in REPLICATED])], "small_reduce_scatter")
    (total,) = _all_gather_call([mine], "small_all_gather")
    grads.update(zip(REPLICATED, _unpack_small(total, shapes)))

    loss = lax.psum(loss_local, ("x", "y", "c"))
    steps = {n: _adamw_call(weights[n], grads[n], given["m_" + n], given["v_" + n], "adamw_" + n) for n in WEIGHTS}
    return (loss, gx[None], *[grads[n] for n in WEIGHTS], *[steps[n][0] for n in WEIGHTS],
            *[steps[n][1] for n in WEIGHTS], *[steps[n][2] for n in WEIGHTS])
```

```python
import functools
import math

import jax
import jax.numpy as jnp
from jax import lax
from jax.experimental import pallas as pl
from jax.experimental.pallas import tpu as pltpu

F32 = jnp.float32
BF16 = jnp.bfloat16

VMEM_LIMIT_BYTES = 48 * 1024 * 1024
LANES = 128
SUBLANES = 8


def _cparams(*sem):
    return pltpu.CompilerParams(dimension_semantics=sem, vmem_limit_bytes=VMEM_LIMIT_BYTES)


def _pick_tile(n, cands):
    for c in cands:
        if n % c == 0:
            return c
    return n


MM_VMEM_BUDGET = 38 * 1024 * 1024
MM_STEP_US = 0.35
HBM_BYTES_PER_US = 3.0e6
VREG_RMW_PER_US = 1.5e3


def _divisor_tiles(dim, cands):
    out = [t for t in cands if dim % t == 0]
    return out or [dim]


def _mm_tiles(m, n, k, sa, sb, so):
    tms = _divisor_tiles(m, (1408, 1024, 512, 256, 128, 64, 32, 16, 8))[:2]
    tns = _divisor_tiles(n, (2048, 1536, 1408, 1024, 768, 512, 384, 256, 128))
    tks = [k // d for d in (1, 2, 4, 8, 13, 16, 26, 32, 52) if k % d == 0 and (k // d) % LANES == 0] or [k]
    best = None
    for tk in tks:
        nk = k // tk
        for tm in tms:
            for tn in tns:
                vmem = 2 * (tm * tk * sa + tk * tn * sb + tm * tn * so) + (tm * tn * 4 if nk > 1 else 0)
                if vmem > MM_VMEM_BUDGET:
                    continue
                steps = (m // tm) * (n // tn) * nk
                a_reads = m * k * sa * (n // tn if nk > 1 else 1)
                b_reads = k * n * sb * (m // tm if (nk > 1 or n // tn > 1) else 1)
                cost = (steps * MM_STEP_US + (a_reads + b_reads) / HBM_BYTES_PER_US
                        + (m * n * nk / 1024 / VREG_RMW_PER_US if nk > 1 else 0.0))
                if best is None or cost < best[0]:
                    best = (cost, tm, tn, tk)
    assert best is not None, (m, n, k)
    return best[1:]


def _mm_tiles_cached_t(m, n, k, sa, sb, so):
    for tm in _divisor_tiles(m, (1024, 512)):
        if m % tm:
            break
        for tn in _divisor_tiles(n, (1024, 512, 384, 256, 128)):
            if 2 * (k * tm * sa + k * tn * sb + tm * tn * so) + tm * k * 2 <= MM_VMEM_BUDGET:
                return tm, tn
    return None


def _mm_tn_cached_call(a, b, tiles, out_dtype, name):
    k, m = a.shape
    n = b.shape[1]
    tm, tn = tiles

    def body(a_ref, b_ref, o_ref, at_ref):
        @pl.when(pl.program_id(1) == 0)
        def _():
            at_ref[...] = a_ref[...].astype(BF16).T

        o_ref[...] = lax.dot_general(at_ref[...], b_ref[...].astype(BF16), _NN_DIMS,
                                     preferred_element_type=F32).astype(out_dtype)

    return pl.pallas_call(
        body,
        name=name,
        grid=(m // tm, n // tn),
        in_specs=[pl.BlockSpec((k, tm), lambda i, j: (0, i)), pl.BlockSpec((k, tn), lambda i, j: (0, j))],
        out_specs=pl.BlockSpec((tm, tn), lambda i, j: (i, j)),
        out_shape=jax.ShapeDtypeStruct((m, n), out_dtype),
        scratch_shapes=[pltpu.VMEM((tm, k), BF16)],
        compiler_params=_cparams("parallel", "arbitrary"),
    )(a, b)


_NN_DIMS = (((1,), (0,)), ((), ()))


def _mm_call(a, b, ta, tb, add=None, out_dtype=F32, name="mm"):
    m, k = (a.shape[1], a.shape[0]) if ta else a.shape
    k2, n = (b.shape[1], b.shape[0]) if tb else b.shape
    assert k == k2, (a.shape, b.shape, ta, tb)
    sizes = (a.dtype.itemsize, b.dtype.itemsize, jnp.dtype(out_dtype).itemsize + (add.dtype.itemsize if add is not None else 0))
    if ta:
        tiles = _mm_tiles_cached_t(m, n, k, *sizes)
        if tiles is not None:
            return _mm_tn_cached_call(a, b, tiles, out_dtype, name)
    tm, tn, tk = _mm_tiles(m, n, k, *sizes)
    nk = k // tk
    a_spec = pl.BlockSpec((tk, tm), lambda i, j, kk: (kk, i)) if ta else pl.BlockSpec((tm, tk), lambda i, j, kk: (i, kk))
    b_spec = pl.BlockSpec((tn, tk), lambda i, j, kk: (j, kk)) if tb else pl.BlockSpec((tk, tn), lambda i, j, kk: (kk, j))
    o_spec = pl.BlockSpec((tm, tn), lambda i, j, kk: (i, j))
    dn = (((0 if ta else 1,), (1 if tb else 0,)), ((), ()))
    has_add = add is not None

    def body(*refs):
        a_ref, b_ref = refs[0], refs[1]
        c_ref = refs[2] if has_add else None
        o_ref = refs[3] if has_add else refs[2]
        p = lax.dot_general(a_ref[...].astype(BF16), b_ref[...].astype(BF16), dn, preferred_element_type=F32)

        def finish(r):
            if has_add:
                r = r + c_ref[...].astype(F32)
            o_ref[...] = r.astype(out_dtype)

        if nk == 1:
            finish(p)
        else:
            acc_ref = refs[-1]
            kk = pl.program_id(2)

            @pl.when(kk == 0)
            def _():
                acc_ref[...] = p

            @pl.when(kk > 0)
            def _():
                acc_ref[...] += p

            @pl.when(kk == nk - 1)
            def _():
                finish(acc_ref[...])

    in_specs = [a_spec, b_spec] + ([o_spec] if has_add else [])
    args = (a, b) + ((add,) if has_add else ())
    return pl.pallas_call(
        body,
        name=name,
        grid=(m // tm, n // tn, nk),
        in_specs=in_specs,
        out_specs=o_spec,
        out_shape=jax.ShapeDtypeStruct((m, n), out_dtype),
        scratch_shapes=[] if nk == 1 else [pltpu.VMEM((tm, tn), F32)],
        compiler_params=_cparams("parallel", "parallel", "arbitrary"),
    )(*args)


@functools.partial(jax.custom_vjp, nondiff_argnums=(2,))
def matmul(a, b, out_dtype=F32):
    return _mm_call(a, b, False, False, out_dtype=out_dtype, name="mm_fwd")


def _matmul_fwd(a, b, out_dtype):
    return matmul(a, b, out_dtype), (a, b)


def _matmul_bwd(out_dtype, res, g):
    a, b = res
    da = _mm_call(g, b, False, True, out_dtype=a.dtype, name="mm_da")
    db = _mm_call(a, g, True, False, out_dtype=b.dtype, name="mm_db")
    return da, db


matmul.defvjp(_matmul_fwd, _matmul_bwd)


@jax.custom_vjp
def matmul_add(a, b, c):
    return _mm_call(a, b, False, False, add=c, name="mm_add_fwd")


def _matmul_add_fwd(a, b, c):
    return _mm_call(a, b, False, False, add=c, name="mm_add_fwd"), (a, b)


def _matmul_add_bwd(res, g):
    a, b = res
    da = _mm_call(g, b, False, True, out_dtype=a.dtype, name="mm_da")
    db = _mm_call(a, g, True, False, out_dtype=b.dtype, name="mm_db")
    return da, db, g


matmul_add.defvjp(_matmul_add_fwd, _matmul_add_bwd)


def rowwise(f, n_rows, n_aux, tile, name, passthrough=False):
    def specs(arrs, tiled):
        out = []
        for x in arrs:
            if tiled:
                out.append(pl.BlockSpec((tile, x.shape[1]), lambda i: (i, 0)))
            else:
                out.append(pl.BlockSpec(x.shape, lambda i: (0, 0)))
        return out

    def tile_structs(args):
        rows_aux, params = args[: n_rows + n_aux], args[n_rows + n_aux:]
        return [jax.ShapeDtypeStruct((tile, x.shape[1]), x.dtype) for x in rows_aux] + [
            jax.ShapeDtypeStruct(p.shape, p.dtype) for p in params]

    def fwd_call(*args):
        s = args[0].shape[0]
        outs = jax.eval_shape(f, *tile_structs(args))
        n_in = len(args)

        def body(*refs):
            vals = [r[...] for r in refs[:n_in]]
            res = f(*vals)
            for o_ref, r in zip(refs[n_in:], res):
                o_ref[...] = r.astype(o_ref.dtype)

        return pl.pallas_call(
            body,
            name=name + "_fwd",
            grid=(s // tile,),
            in_specs=specs(args[: n_rows + n_aux], True) + specs(args[n_rows + n_aux:], False),
            out_specs=[pl.BlockSpec((tile, o.shape[1]), lambda i: (i, 0)) for o in outs],
            out_shape=[jax.ShapeDtypeStruct((s, o.shape[1]), o.dtype) for o in outs],
            compiler_params=_cparams("parallel"),
        )(*args)

    def bwd_call(args, gs):
        s = args[0].shape[0]
        rows, aux, params = args[:n_rows], args[n_rows:n_rows + n_aux], args[n_rows + n_aux:]
        n_in, n_g, n_p = len(args), len(gs), len(params)
        n_gf = n_g - 1 if passthrough else n_g

        def body(*refs):
            vals = [r[...] for r in refs[:n_in]]
            gvals = tuple(r[...] for r in refs[n_in:n_in + n_gf])
            out_refs = refs[n_in + n_g:]
            auxv = vals[n_rows:n_rows + n_aux]

            def g_(*rp):
                return tuple(f(*rp[:n_rows], *auxv, *rp[n_rows:]))

            _, vjp = jax.vjp(g_, *vals[:n_rows], *vals[n_rows + n_aux:])
            cts = list(vjp(gvals))
            if passthrough:
                cts[0] = cts[0] + refs[n_in + n_gf][...]
            for o_ref, ct in zip(out_refs[:n_rows], cts[:n_rows]):
                o_ref[...] = ct.astype(o_ref.dtype)
            if n_p:
                @pl.when(pl.program_id(0) == 0)
                def _():
                    for o_ref in out_refs[n_rows:]:
                        o_ref[...] = jnp.zeros_like(o_ref)

                for o_ref, ct in zip(out_refs[n_rows:], cts[n_rows:]):
                    o_ref[...] += ct.astype(o_ref.dtype)

        return pl.pallas_call(
            body,
            name=name + "_bwd",
            grid=(s // tile,),
            in_specs=specs(rows + aux, True) + specs(params, False) + specs(gs, True),
            out_specs=specs(rows, True) + specs(params, False),
            out_shape=[jax.ShapeDtypeStruct(x.shape, x.dtype) for x in rows + params],
            compiler_params=_cparams("arbitrary" if n_p else "parallel"),
        )(*args, *gs)

    @jax.custom_vjp
    def op(*args):
        return tuple(fwd_call(*args)) + ((args[0],) if passthrough else ())

    def op_fwd(*args):
        return op(*args), args

    def op_bwd(args, gs):
        cts = bwd_call(tuple(args), tuple(gs))
        rows_ct, par_ct = cts[:n_rows], cts[n_rows:]
        aux_ct = [jnp.zeros_like(a) for a in args[n_rows:n_rows + n_aux]]
        return tuple(rows_ct) + tuple(aux_ct) + tuple(par_ct)

    op.defvjp(op_fwd, op_bwd)
    return op


SCAN_SEGMENTS = SUBLANES


def _scan_step(ar, ai, xr, xi, br, bi):
    return ar * xr - ai * xi + br, ar * xi + ai * xr + bi


_NT_DIMS = (((1,), (1,)), ((), ()))
_TN_DIMS = (((0,), (0,)), ((), ()))


S5_GROUPS_PER_STEP = 128
S5_STATE_LANES = 512


def _s5_interleave(src_ref, scr_ref, rows):
    for k in range(SCAN_SEGMENTS):
        scr_ref[pl.ds(k, rows, stride=SCAN_SEGMENTS), :] = src_ref[k].astype(F32)


def _s5_deinterleave(val, scr_ref, dst_ref, rows):
    scr_ref[...] = val
    for k in range(SCAN_SEGMENTS):
        dst_ref[k] = scr_ref[pl.ds(k, rows, stride=SCAN_SEGMENTS), :]


def _s5_segment_starts(a_ref, fr_ref, fi_ref, sr, si, seg_len, order):
    tn = sr.shape[1]
    pr, pi = a_ref[0:1, :], a_ref[1:2, :]
    for _ in range(seg_len.bit_length() - 1):
        pr, pi = pr * pr - pi * pi, 2.0 * pr * pi
    cr = jnp.zeros((1, tn), F32)
    ci = jnp.zeros((1, tn), F32)
    for idx, k in enumerate(order):
        if idx > 0:
            kp = order[idx - 1]
            cr, ci = (fr_ref[kp:kp + 1, :] + pr * cr - pi * ci, fi_ref[kp:kp + 1, :] + pr * ci + pi * cr)
        sr[k:k + 1, :] = cr
        si[k:k + 1, :] = ci


def _s5_pass_call(src, w_r, w_i, a, transpose_w, reverse, finals, extra, name):
    s = src.shape[0]
    nb = w_r.shape[0]
    n = nb * S5_STATE_LANES
    seg_len = s // SCAN_SEGMENTS
    ti = min(S5_GROUPS_PER_STEP, seg_len)
    nt = seg_len // ti
    tr = SCAN_SEGMENTS * ti
    tn = S5_STATE_LANES
    assert seg_len & (seg_len - 1) == 0
    order = list(range(SCAN_SEGMENTS))[::-1] if reverse else list(range(SCAN_SEGMENTS))
    dn_in = _NT_DIMS if transpose_w else _NN_DIMS
    first = finals is None
    backward = (not first) and reverse
    forward = (not first) and not reverse
    tmap3 = (lambda c, j: (0, nt - 1 - j, c)) if reverse else (lambda c, j: (0, j, c))
    tmap2 = (lambda c, j: (nt - 1 - j, c)) if reverse else (lambda c, j: (j, c))
    bf = lambda v: v.astype(BF16)

    def body(*refs):
        it = iter(refs)
        src_ref, wr_ref, wi_ref, a_ref = next(it), next(it), next(it), next(it)
        if not first:
            fr_ref, fi_ref = next(it), next(it)
        if forward:
            cdr_ref, cdi_ref = next(it), next(it)
            xr_out, xi_out, y_ref = next(it), next(it), next(it)
        if backward:
            xr_ref, xi_ref, u_ref, bdr_ref, bdi_ref = next(it), next(it), next(it), next(it), next(it)
            du_ref, dar_ref, dai_ref, dbr_ref, dbi_ref, dcr_ref, dci_ref = (next(it) for _ in range(7))
        if first:
            fr_out, fi_out = next(it), next(it)
        sr, si, in_scr, dr_scr, di_scr = next(it), next(it), next(it), next(it), next(it)
        if backward:
            accr, acci, u_scr = next(it), next(it), next(it)
        j = pl.program_id(1)

        @pl.when(j == 0)
        def _():
            if first:
                sr[...] = jnp.zeros_like(sr)
                si[...] = jnp.zeros_like(si)
            else:
                _s5_segment_starts(a_ref, fr_ref, fi_ref, sr, si, seg_len, order)
            if backward:
                for r in (accr, acci, dbr_ref, dbi_ref, dcr_ref, dci_ref):
                    r[...] = jnp.zeros_like(r)

        _s5_interleave(src_ref, in_scr, ti)
        src_b = bf(in_scr[...])
        dr_scr[...] = lax.dot_general(src_b, bf(wr_ref[0]), dn_in, preferred_element_type=F32)
        di_scr[...] = lax.dot_general(src_b, bf(wi_ref[0]), dn_in, preferred_element_type=F32)
        ar = jnp.broadcast_to(a_ref[0:1, :], (SUBLANES, tn))
        ai = jnp.broadcast_to(a_ref[1:2, :], (SUBLANES, tn))

        def step(ii, carry):
            i = (ti - 1 - ii) if reverse else ii
            rows = pl.ds(pl.multiple_of(i * SUBLANES, SUBLANES), SUBLANES)
            xr, xi = carry[0], carry[1]
            if backward:
                zr, zi = xr_ref[rows, :], xi_ref[rows, :]
                acc = (carry[2] + xr * zr + xi * zi, carry[3] + xi * zr - xr * zi)
            nr, ni = _scan_step(ar, ai, xr, xi, dr_scr[rows, :], di_scr[rows, :])
            if forward:
                xr_out[rows, :] = nr
                xi_out[rows, :] = ni
            if backward:
                dr_scr[rows, :] = nr
                di_scr[rows, :] = ni
            return (nr, ni) + (acc if backward else ())

        init = (sr[...], si[...]) + ((accr[...], acci[...]) if backward else ())
        out = lax.fori_loop(0, ti, step, init, unroll=4)
        sr[...] = out[0]
        si[...] = out[1]
        if first:
            @pl.when(j == nt - 1)
            def _():
                fr_out[...] = out[0]
                fi_out[...] = out[1]
        if forward:
            y = (lax.dot_general(bf(xr_out[...]), bf(cdr_ref[0]), _NN_DIMS, preferred_element_type=F32)
                 + lax.dot_general(bf(xi_out[...]), bf(cdi_ref[0]), _NN_DIMS, preferred_element_type=F32))
            _s5_deinterleave(y, in_scr, y_ref, ti)
        if backward:
            accr[...] = out[2]
            acci[...] = out[3]
            g_r, g_i = bf(dr_scr[...]), bf(di_scr[...])
            dcr_ref[0] += lax.dot_general(bf(xr_ref[...]), src_b, _TN_DIMS, preferred_element_type=F32)
            dci_ref[0] += lax.dot_general(bf(xi_ref[...]), src_b, _TN_DIMS, preferred_element_type=F32)
            _s5_interleave(u_ref, u_scr, ti)
            u_b = bf(u_scr[...])
            dbr_ref[0] += lax.dot_general(u_b, g_r, _TN_DIMS, preferred_element_type=F32)
            dbi_ref[0] += lax.dot_general(u_b, g_i, _TN_DIMS, preferred_element_type=F32)
            du = (lax.dot_general(g_r, bf(bdr_ref[0]), _NT_DIMS, preferred_element_type=F32)
                  + lax.dot_general(g_i, bf(bdi_ref[0]), _NT_DIMS, preferred_element_type=F32))
            _s5_deinterleave(du, u_scr, du_ref, ti)

            @pl.when(j == nt - 1)
            def _():
                dar_ref[...] = jnp.sum(out[2], axis=0, keepdims=True)
                dai_ref[...] = jnp.sum(out[3], axis=0, keepdims=True)

    view3 = lambda t: t.reshape(SCAN_SEGMENTS, seg_len, t.shape[1])
    spec3 = pl.BlockSpec((SCAN_SEGMENTS, ti, LANES), tmap3)
    wspec = lambda w: pl.BlockSpec((1,) + w.shape[1:], lambda c, j: (c, 0, 0))
    aspec = pl.BlockSpec((2, tn), lambda c, j: (0, c))
    fspec = pl.BlockSpec((SUBLANES, tn), lambda c, j: (0, c))
    xspec = pl.BlockSpec((tr, tn), tmap2)
    dspec = pl.BlockSpec((1, tn), lambda c, j: (0, c))
    f32 = lambda *shape: jax.ShapeDtypeStruct(shape, F32)
    args, in_specs = [view3(src), w_r, w_i, a], [spec3, wspec(w_r), wspec(w_i), aspec]
    if not first:
        args += list(finals)
        in_specs += [fspec, fspec]
    if forward:
        args += list(extra)
        in_specs += [wspec(extra[0]), wspec(extra[1])]
        out_specs, out_shape = [xspec, xspec, spec3], [f32(s, n), f32(s, n), f32(SCAN_SEGMENTS, seg_len, nb * LANES)]
    elif backward:
        x_r, x_i, u, bd_r, bd_i = extra
        args += [x_r, x_i, view3(u), bd_r, bd_i]
        in_specs += [xspec, xspec, spec3, wspec(bd_r), wspec(bd_i)]
        out_specs = [spec3, dspec, dspec, wspec(bd_r), wspec(bd_i), wspec(w_r), wspec(w_i)]
        out_shape = [f32(SCAN_SEGMENTS, seg_len, nb * LANES), f32(1, n), f32(1, n), f32(*bd_r.shape), f32(*bd_i.shape),
                     f32(*w_r.shape), f32(*w_i.shape)]
    else:
        out_specs, out_shape = [fspec, fspec], [f32(SUBLANES, n), f32(SUBLANES, n)]
    scratch = ([pltpu.VMEM((SUBLANES, tn), F32)] * 2 + [pltpu.VMEM((tr, LANES), F32)] + [pltpu.VMEM((tr, tn), F32)] * 2
               + ([pltpu.VMEM((SUBLANES, tn), F32)] * 2 + [pltpu.VMEM((tr, LANES), F32)] if backward else []))
    return pl.pallas_call(
        body, name=name, grid=(nb, nt), in_specs=in_specs, out_specs=out_specs, out_shape=out_shape,
        scratch_shapes=scratch, compiler_params=_cparams("parallel", "arbitrary"),
    )(*args)


@jax.custom_vjp
def s5_core(u, a, bd_r, bd_i, cd_r, cd_i):
    return _s5_core_fwd(u, a, bd_r, bd_i, cd_r, cd_i)[0]


def _s5_core_fwd(u, a, bd_r, bd_i, cd_r, cd_i):
    fin = _s5_pass_call(u, bd_r, bd_i, a, False, False, None, None, "s5_fwd_finals")
    x_r, x_i, y = _s5_pass_call(u, bd_r, bd_i, a, False, False, fin, (cd_r, cd_i), "s5_fwd_scan")
    return y.reshape(u.shape), (u, a, bd_r, bd_i, cd_r, cd_i, x_r, x_i)


def _s5_core_bwd(res, dy):
    u, a, bd_r, bd_i, cd_r, cd_i, x_r, x_i = res
    a_conj = a * jnp.array([[1.0], [-1.0]], F32)
    fin = _s5_pass_call(dy, cd_r, cd_i, a_conj, True, True, None, None, "s5_bwd_finals")
    du, da_r, da_i, dbd_r, dbd_i, dcd_r, dcd_i = _s5_pass_call(
        dy, cd_r, cd_i, a_conj, True, True, fin, (x_r, x_i, u, bd_r, bd_i), "s5_bwd_scan")
    return du.reshape(u.shape), jnp.concatenate([da_r, da_i], axis=0), dbd_r, dbd_i, dcd_r, dcd_i


s5_core.defvjp(_s5_core_fwd, _s5_core_bwd)


_NN = (((1,), (0,)), ((), ()))
_NT = (((1,), (1,)), ((), ()))
_TN = (((0,), (0,)), ((), ()))


def _dot(a, b, dn):
    return lax.dot_general(a.astype(BF16), b.astype(BF16), dn, preferred_element_type=F32)


@jax.custom_vjp
def bdot_nn(a, b):
    return _dot(a, b, _NN)


bdot_nn.defvjp(lambda a, b: (_dot(a, b, _NN), (a, b)),
               lambda r, g: (_dot(g, r[1], _NT).astype(r[0].dtype), _dot(r[0], g, _TN).astype(r[1].dtype)))


@jax.custom_vjp
def bdot_nt(a, b):
    return _dot(a, b, _NT)


bdot_nt.defvjp(lambda a, b: (_dot(a, b, _NT), (a, b)),
               lambda r, g: (_dot(g, r[1], _NN).astype(r[0].dtype), _dot(g, r[0], _TN).astype(r[1].dtype)))


@jax.custom_vjp
def bdot_tn(a, b):
    return _dot(a, b, _TN)


bdot_tn.defvjp(lambda a, b: (_dot(a, b, _TN), (a, b)),
               lambda r, g: (_dot(r[1], g, _NT).astype(r[0].dtype), _dot(r[0], g, _NN).astype(r[1].dtype)))


def _split_hi_lo(x):
    h = x.astype(BF16)
    return h, (x - h.astype(F32)).astype(BF16)


def _exact_dot(t, x, dn):
    h, l = _split_hi_lo(x)
    d = lambda p: lax.dot_general(t, p, dn, preferred_element_type=F32)
    return d(h) + d(l)


@jax.custom_vjp
def select_dot(t, x):
    return _exact_dot(t, x, _NN)


select_dot.defvjp(lambda t, x: (_exact_dot(t, x, _NN), t),
                  lambda t, g: (jnp.zeros_like(t), _exact_dot(t, g, _TN)))


def _split_rows_impl(x, h):
    return tuple(x[i * h:(i + 1) * h] for i in range(x.shape[0] // h))


@functools.partial(jax.custom_vjp, nondiff_argnums=(1,))
def split_rows(x, h):
    return _split_rows_impl(x, h)


split_rows.defvjp(lambda x, h: (_split_rows_impl(x, h), None),
                  lambda h, r, g: (jnp.concatenate(g, axis=0),))


@jax.custom_vjp
def join_rows(parts):
    return jnp.concatenate(parts, axis=0)


def _join_rows_bwd(hs, g):
    out, off = [], 0
    for h in hs:
        out.append(g[off:off + h])
        off += h
    return (tuple(out),)


join_rows.defvjp(lambda parts: (jnp.concatenate(parts, axis=0), tuple(p.shape[0] for p in parts)), _join_rows_bwd)


def _split_lanes_impl(x, w):
    return tuple(x[:, i * w:(i + 1) * w] for i in range(x.shape[1] // w))


@functools.partial(jax.custom_vjp, nondiff_argnums=(1,))
def split_lanes(x, w):
    return _split_lanes_impl(x, w)


split_lanes.defvjp(lambda x, w: (_split_lanes_impl(x, w), None),
                   lambda w, r, g: (jnp.concatenate(g, axis=1),))


def _join_impl(parts):
    return jnp.concatenate(parts, axis=1)


@jax.custom_vjp
def join_lanes(parts):
    return _join_impl(parts)


def _join_bwd(ws, g):
    out, off = [], 0
    for w in ws:
        out.append(g[:, off:off + w])
        off += w
    return (tuple(out),)


join_lanes.defvjp(lambda parts: (_join_impl(parts), tuple(p.shape[1] for p in parts)), _join_bwd)


def _rope_impl(x, c, sa, sb, shift):
    w = x.shape[1]
    return x * c + pltpu.roll(x, w - shift, 1) * sa + pltpu.roll(x, shift, 1) * sb


@functools.partial(jax.custom_vjp, nondiff_argnums=(4,))
def rope_lanes(x, c, sa, sb, shift):
    return _rope_impl(x, c, sa, sb, shift)


def _rope_bwd(shift, r, g):
    c, sa, sb = r
    w = g.shape[1]
    dx = g * c + pltpu.roll(g * sa, shift, 1) + pltpu.roll(g * sb, w - shift, 1)
    return dx, jnp.zeros_like(c), jnp.zeros_like(sa), jnp.zeros_like(sb)


rope_lanes.defvjp(lambda x, c, sa, sb, shift: (_rope_impl(x, c, sa, sb, shift), (c, sa, sb)), _rope_bwd)


RMS_EPS = 1e-6


def _rms(x, g):
    return x * lax.rsqrt(jnp.mean(x * x, axis=-1, keepdims=True) + RMS_EPS) * g


ATTN_BLOCK = 512
MASK_VALUE = -1e30
LOG2E = math.log2(math.e)
LN2 = math.log(2.0)
V_ONES_LANE = 64


def _causal_mask(t):
    r = lax.broadcasted_iota(jnp.int32, (t, t), 0)
    c = lax.broadcasted_iota(jnp.int32, (t, t), 1)
    return c <= r


def _attn_fwd_call(q, k, v):
    s, width = q.shape
    n_heads = width // LANES
    tq = min(ATTN_BLOCK, s)
    nq = s // tq

    def body(q_ref, k_ref, v_ref, o_ref, lse_ref):
        i = pl.program_id(1)
        qb = q_ref[...].astype(BF16)
        ones_lane = lax.broadcasted_iota(jnp.int32, (tq, LANES), 1) == V_ONES_LANE

        def block(kb, carry, masked):
            m, acc = carry
            rows = pl.ds(pl.multiple_of(kb * tq, tq), tq)
            sc = lax.dot_general(qb, k_ref[rows, :].astype(BF16), _NT, preferred_element_type=F32)
            if masked:
                sc = jnp.where(_causal_mask(tq), sc, MASK_VALUE)
            m_new = jnp.maximum(m, jnp.max(sc, axis=-1, keepdims=True))
            p = jnp.exp2(sc - m_new).astype(BF16)
            vb = jnp.where(ones_lane, 1.0, v_ref[rows, :]).astype(BF16)
            acc = jnp.exp2(m - m_new) * acc + lax.dot_general(p, vb, _NN, preferred_element_type=F32)
            return m_new, acc

        init = (jnp.full((tq, 1), MASK_VALUE, F32), jnp.zeros((tq, LANES), F32))
        carry = lax.fori_loop(0, i, lambda kb, c: block(kb, c, False), init)
        m, acc = block(i, carry, True)
        l = jnp.sum(jnp.where(ones_lane, acc, 0.0), axis=-1, keepdims=True)
        o_ref[...] = jnp.where(ones_lane, 0.0, acc / l).astype(o_ref.dtype)
        lse_ref[...] = jnp.broadcast_to(m + jnp.log2(l), (tq, LANES))

    qspec = pl.BlockSpec((tq, LANES), lambda h, i: (i, h))
    kspec = pl.BlockSpec((s, LANES), lambda h, i: (0, h))
    return pl.pallas_call(
        body,
        name="mla_attn_fwd",
        grid=(n_heads, nq),
        in_specs=[qspec, kspec, kspec],
        out_specs=[qspec, qspec],
        out_shape=[jax.ShapeDtypeStruct((s, width), BF16), jax.ShapeDtypeStruct((s, width), F32)],
        compiler_params=_cparams("parallel", "parallel"),
    )(q, k, v)


def _attn_bwd_call(q, k, v, o, lse, do):
    s, width = q.shape
    n_heads = width // LANES
    tq = min(ATTN_BLOCK, s)
    nq = s // tq

    def body(q_ref, k_ref, v_ref, o_ref, lse_ref, do_ref, dq_ref, dk_ref, dv_ref, dq_acc):
        j = pl.program_id(1)

        @pl.when(j == 0)
        def _():
            dq_acc[...] = jnp.zeros_like(dq_acc)

        kb = k_ref[...].astype(BF16)
        vb = v_ref[...].astype(BF16)

        def block(i, carry, masked):
            dk, dv = carry
            rows = pl.ds(pl.multiple_of(i * tq, tq), tq)
            qi = q_ref[rows, :].astype(BF16)
            doi = do_ref[rows, :].astype(F32)
            delta = jnp.sum(doi * o_ref[rows, :].astype(F32), axis=-1, keepdims=True)
            sc = lax.dot_general(qi, kb, _NT, preferred_element_type=F32)
            if masked:
                sc = jnp.where(_causal_mask(tq), sc, MASK_VALUE)
            p = jnp.exp2(sc - lse_ref[rows, 0:1])
            dob = doi.astype(BF16)
            dv = dv + lax.dot_general(p.astype(BF16), dob, _TN, preferred_element_type=F32)
            dp = lax.dot_general(dob, vb, _NT, preferred_element_type=F32)
            ds = (p * (dp - delta)).astype(BF16)
            dq_acc[rows, :] += lax.dot_general(ds, kb, _NN, preferred_element_type=F32)
            dk = dk + lax.dot_general(ds, qi, _TN, preferred_element_type=F32)
            return dk, dv

        zero = jnp.zeros((tq, LANES), F32)
        carry = block(j, (zero, zero), True)
        dk, dv = lax.fori_loop(j + 1, nq, lambda i, c: block(i, c, False), carry)
        dk_ref[...] = (dk * LN2).astype(dk_ref.dtype)
        dv_ref[...] = dv.astype(dv_ref.dtype)

        @pl.when(j == nq - 1)
        def _():
            dq_ref[...] = (dq_acc[...] * LN2).astype(dq_ref.dtype)

    full = pl.BlockSpec((s, LANES), lambda h, j: (0, h))
    blk = pl.BlockSpec((tq, LANES), lambda h, j: (j, h))
    return pl.pallas_call(
        body,
        name="mla_attn_bwd",
        grid=(n_heads, nq),
        in_specs=[full, blk, blk, full, full, full],
        out_specs=[full, blk, blk],
        out_shape=[jax.ShapeDtypeStruct((s, width), t.dtype) for t in (q, k, v)],
        scratch_shapes=[pltpu.VMEM((s, LANES), F32)],
        compiler_params=_cparams("parallel", "arbitrary"),
    )(q, k, v, o, lse, do)


@jax.custom_vjp
def causal_attention(q, k, v):
    return _attn_fwd_call(q, k, v)[0]


def _causal_attention_fwd(q, k, v):
    o, lse = _attn_fwd_call(q, k, v)
    return o, (q, k, v, o, lse)


def _causal_attention_bwd(res, do):
    return tuple(_attn_bwd_call(*res, do))


causal_attention.defvjp(_causal_attention_fwd, _causal_attention_bwd)


HG_HEADS = 4
HG_CHUNK = 32
HG_REF_ROW = HG_CHUNK // 2 - 1
HG_TILE_ROWS = 256
HG_EXP_CLAMP = 80.0


def _hg_tile_masks(t):
    shift = HG_CHUNK.bit_length() - 1
    r = lax.broadcasted_iota(jnp.int32, (t, t), 0)
    c = lax.broadcasted_iota(jnp.int32, (t, t), 1)
    start = lax.shift_left(lax.shift_right_logical(r, shift), shift)
    causal = (c >= start) & (c <= r)
    return causal, c == start + HG_REF_ROW, c == start + (HG_CHUNK - 1)


def _hg_tile(q, fl, v, lb, st):
    t = q.shape[0]
    causal, ref_sel, last_sel = _hg_tile_masks(t)
    f = lb + (1.0 - lb) * jax.nn.sigmoid(fl)
    kk = 1.0 - f
    qs = q * jax.nn.sigmoid(q)
    b = select_dot(causal.astype(BF16), jnp.log(f))
    b_ref = select_dot(ref_sel.astype(BF16), b)
    b_last = select_dot(last_sel.astype(BF16), b)
    q_in = qs * jnp.exp(jnp.minimum(b - b_ref, HG_EXP_CLAMP))
    k_in = kk * jnp.exp(jnp.minimum(b_ref - b, HG_EXP_CLAMP))
    o = bdot_nn(jnp.where(causal, bdot_nt(q_in, k_in), 0.0), v)
    q_hat = split_rows(qs * jnp.exp(b), HG_CHUNK)
    k_hat = split_rows(kk * jnp.exp(b_last - b), HG_CHUNK)
    decay = split_rows(jnp.exp(b_last), HG_CHUNK)
    vs = split_rows(v, HG_CHUNK)
    first_row = lax.broadcasted_iota(jnp.int32, (HG_CHUNK, LANES), 0) == 0
    inter = []
    for c in range(t // HG_CHUNK):
        inter.append(bdot_nt(q_hat[c], st))
        st = st * jnp.sum(jnp.where(first_row, decay[c], 0.0), axis=0, keepdims=True) + bdot_tn(vs[c], k_hat[c])
    return o + join_rows(tuple(inter)), st


def _hg_head(q, fl, v, gate, lb, gn, st):
    o, st = _hg_tile(q, fl, v, lb, st)
    return _rms(o, gn) * (gate * jax.nn.sigmoid(gate)), st


HG_PARTS = 4


def _hg_part_slices(h, width):
    return [slice(p * width + h * LANES, p * width + (h + 1) * LANES) for p in range(HG_PARTS)]


def _hg_fwd_call(x, lb, gn):
    s = x.shape[0]
    width = x.shape[1] // HG_PARTS
    tr = min(HG_TILE_ROWS, s)
    nt = s // tr

    def body(x_ref, lb_ref, gn_ref, o_ref, sts_ref, st_ref):
        @pl.when(pl.program_id(0) == 0)
        def _():
            st_ref[...] = jnp.zeros_like(st_ref)

        for h in range(HG_HEADS):
            ln = slice(h * LANES, (h + 1) * LANES)
            st = st_ref[h]
            sts_ref[0, h] = st
            o, st_new = _hg_head(*(x_ref[:, sl] for sl in _hg_part_slices(h, width)), lb_ref[:, ln], gn_ref[...], st)
            o_ref[:, ln] = o.astype(o_ref.dtype)
            st_ref[h] = st_new

    const = lambda shape: pl.BlockSpec(shape, lambda j: (0, 0))
    return pl.pallas_call(
        body,
        name="hgrn2_fwd",
        grid=(nt,),
        in_specs=[pl.BlockSpec((tr, HG_PARTS * width), lambda j: (j, 0)), const((1, width)), const((1, LANES))],
        out_specs=[pl.BlockSpec((tr, width), lambda j: (j, 0)),
                   pl.BlockSpec((1, HG_HEADS, LANES, LANES), lambda j: (j, 0, 0, 0))],
        out_shape=[jax.ShapeDtypeStruct((s, width), BF16),
                   jax.ShapeDtypeStruct((nt, HG_HEADS, LANES, LANES), F32)],
        scratch_shapes=[pltpu.VMEM((HG_HEADS, LANES, LANES), F32)],
        compiler_params=_cparams("arbitrary"),
    )(x, lb, gn)


def _hg_bwd_call(x, lb, gn, sts, do):
    s = x.shape[0]
    width = x.shape[1] // HG_PARTS
    tr = min(HG_TILE_ROWS, s)
    nt = s // tr

    def body(x_ref, lb_ref, gn_ref, sts_ref, do_ref, dx_ref, dlb_ref, dgn_ref, dst_ref):
        @pl.when(pl.program_id(0) == 0)
        def _():
            dst_ref[...] = jnp.zeros_like(dst_ref)
            dlb_ref[...] = jnp.zeros_like(dlb_ref)
            dgn_ref[...] = jnp.zeros_like(dgn_ref)

        for h in range(HG_HEADS):
            ln = slice(h * LANES, (h + 1) * LANES)
            parts = _hg_part_slices(h, width)
            _, vjp = jax.vjp(_hg_head, *(x_ref[:, sl] for sl in parts), lb_ref[:, ln], gn_ref[...], sts_ref[0, h])
            cts = vjp((do_ref[:, ln].astype(F32), dst_ref[h]))
            for sl, ct in zip(parts, cts[:HG_PARTS]):
                dx_ref[:, sl] = ct.astype(dx_ref.dtype)
            dlb_ref[:, ln] += cts[HG_PARTS]
            dgn_ref[...] += cts[HG_PARTS + 1]
            dst_ref[h] = cts[HG_PARTS + 2]

    rev = lambda w: pl.BlockSpec((tr, w), lambda j: (nt - 1 - j, 0))
    const = lambda shape: pl.BlockSpec(shape, lambda j: (0, 0))
    return pl.pallas_call(
        body,
        name="hgrn2_bwd",
        grid=(nt,),
        in_specs=[rev(HG_PARTS * width), const((1, width)), const((1, LANES)),
                  pl.BlockSpec((1, HG_HEADS, LANES, LANES), lambda j: (nt - 1 - j, 0, 0, 0)), rev(width)],
        out_specs=[rev(HG_PARTS * width), const((1, width)), const((1, LANES))],
        out_shape=[jax.ShapeDtypeStruct(x.shape, BF16), jax.ShapeDtypeStruct((1, width), F32),
                   jax.ShapeDtypeStruct((1, LANES), F32)],
        scratch_shapes=[pltpu.VMEM((HG_HEADS, LANES, LANES), F32)],
        compiler_params=_cparams("arbitrary"),
    )(x, lb, gn, sts, do)


@jax.custom_vjp
def hgrn2_mixer(h, w, lb, gn):
    return _hg_fwd_call(_mm_call(h, w, False, False, name="hgrn2_proj"), lb, gn)[0]


def _hgrn2_mixer_fwd(h, w, lb, gn):
    x = _mm_call(h, w, False, False, name="hgrn2_proj")
    o, sts = _hg_fwd_call(x, lb, gn)
    return o, (h, w, x, lb, gn, sts)


def _hgrn2_mixer_bwd(res, do):
    h, w, x, lb, gn, sts = res
    dx, dlb, dgn = _hg_bwd_call(x, lb, gn, sts, do)
    dh = _mm_call(dx, w, False, True, out_dtype=h.dtype, name="hgrn2_proj_da")
    dw = _mm_call(h, dx, True, False, out_dtype=w.dtype, name="hgrn2_proj_db")
    return dh, dw, dlb, dgn


hgrn2_mixer.defvjp(_hgrn2_mixer_fwd, _hgrn2_mixer_bwd)


D_MODEL = 1024
DEPTH = 2
SSM_GROUPS, SSM_GROUP_CH, SSM_STATE = 32, 16, 64
SSM_WIDTH = SSM_GROUPS * SSM_GROUP_CH
MLA_HEADS, MLA_NOPE, MLA_ROPE, MLA_V = 8, 64, 32, 64
MLA_Q_RANK, MLA_KV_RANK = 512, 256
HG_WIDTH = HG_HEADS * LANES
X_HEADS, X_HEAD_DIM = 4, 128
X_WIDTH = X_HEADS * X_HEAD_DIM
D_FF = 2816
ROPE_THETA = 10000.0
IN_SPLITS = (SSM_WIDTH, MLA_Q_RANK, MLA_KV_RANK, MLA_ROPE, HG_WIDTH, HG_WIDTH, HG_WIDTH, HG_WIDTH, 3 * D_MODEL)
ROPE_LANE0 = MLA_NOPE
MLA_Q_SCALE = LOG2E / math.sqrt(MLA_NOPE + MLA_ROPE)
ROW_TILE = 512
MEM_ROW_TILE = 256


def _t_rms(x, g):
    return (_rms(x, g).astype(BF16),)


def _t_s5_act(y, u, d):
    return (jax.nn.gelu(y + d * u).astype(BF16),)


def _t_glu(z):
    zo, zg = split_lanes(z.astype(F32), D_MODEL)
    return ((zo * jax.nn.sigmoid(zg)).astype(BF16),)


def _t_mla_rope(q, k, kr, c, sa, sb):
    rep = lambda t: jnp.concatenate([t] * MLA_HEADS, axis=1)
    half = MLA_ROPE // 2
    q_out = rope_lanes(q, rep(c), rep(sa), rep(sb), half) * MLA_Q_SCALE
    kr_out = rope_lanes(kr, c, sa, sb, half)
    return q_out.astype(BF16), (k + join_lanes((kr_out,) * MLA_HEADS)).astype(BF16)


def _t_merge(y_ssm, y_mla, y_hg, gates):
    g0, g1, g2 = split_lanes(gates.astype(F32), D_MODEL)
    mix = (jax.nn.sigmoid(g0) * y_ssm.astype(F32) + jax.nn.sigmoid(g1) * y_mla.astype(F32)
           + jax.nn.sigmoid(g2) * y_hg.astype(F32))
    return (mix.astype(BF16),)


def _t_xattn(q, kv):
    scale = 1.0 / math.sqrt(X_HEAD_DIM)
    heads = split_lanes(kv, X_HEAD_DIM)
    outs = []
    for qh, kh, vh in zip(split_lanes(q, X_HEAD_DIM), heads[:X_HEADS], heads[X_HEADS:]):
        sc = bdot_nt(qh, kh) * scale
        p = jnp.exp(sc - jnp.max(sc, axis=-1, keepdims=True))
        p = p / jnp.sum(p, axis=-1, keepdims=True)
        outs.append(bdot_nn(p, vh))
    return (join_lanes(tuple(outs)).astype(BF16),)


def _t_swiglu(gate_up):
    gt, up = split_lanes(gate_up.astype(F32), D_FF)
    return ((gt * jax.nn.sigmoid(gt) * up).astype(BF16),)


def _t_loss(x, tgt, g):
    e = _rms(x, g) - tgt
    return (jnp.broadcast_to(jnp.mean(e * e, axis=-1, keepdims=True), (x.shape[0], LANES)),)


rms_op = rowwise(_t_rms, 1, 0, ROW_TILE, "rmsnorm")
rms_mem_op = rowwise(_t_rms, 1, 0, MEM_ROW_TILE, "rmsnorm_mem")
rms_res_op = rowwise(_t_rms, 1, 0, ROW_TILE, "rmsnorm_res", passthrough=True)
s5_act_op = rowwise(_t_s5_act, 2, 0, ROW_TILE, "s5_act")
glu_op = rowwise(_t_glu, 1, 0, ROW_TILE, "glu")
mla_rope_op = rowwise(_t_mla_rope, 3, 3, ROW_TILE, "mla_rope")
merge_op = rowwise(_t_merge, 4, 0, ROW_TILE, "merge")
xattn_op = rowwise(_t_xattn, 1, 0, ROW_TILE, "xattn")
swiglu_op = rowwise(_t_swiglu, 1, 0, ROW_TILE, "swiglu")
loss_op = rowwise(_t_loss, 1, 1, ROW_TILE, "loss")


def _rope_tables(positions):
    half = MLA_ROPE // 2
    inv_freq = ROPE_THETA ** (-jnp.arange(half, dtype=F32) / half)
    ang = positions.astype(F32)[:, None] * inv_freq
    cos, sin = jnp.cos(ang), jnp.sin(ang)
    s = positions.shape[0]
    z = lambda w: jnp.zeros((s, w), F32)
    tail = LANES - ROPE_LANE0 - MLA_ROPE
    c = jnp.concatenate([jnp.ones((s, ROPE_LANE0), F32), cos, cos, z(tail)], axis=1)
    sa = jnp.concatenate([z(ROPE_LANE0), -sin, z(half), z(tail)], axis=1)
    sb = jnp.concatenate([z(ROPE_LANE0), z(half), sin, z(tail)], axis=1)
    return c, sa, sb


def _s5_operators(lam_re, lam_im, b_re, b_im, c_re, c_im, log_step):
    g, p, h = SSM_GROUPS, SSM_STATE, SSM_GROUP_CH
    lam = lax.complex(lam_re, lam_im)
    lam_bar = jnp.exp(lam * jnp.exp(log_step)[:, None])
    b_bar = ((lam_bar - 1.0) / lam)[..., None] * lax.complex(b_re, b_im)
    per = LANES // h
    nb = g // per
    eye = jnp.eye(per, dtype=F32)
    bd = lambda t: jnp.einsum("jgph,gk->jghkp", t.reshape(nb, per, p, h), eye).reshape(nb, per * h, per * p)
    cd = lambda t: jnp.einsum("jghp,gk->jgpkh", t.reshape(nb, per, h, p), eye).reshape(nb, per * p, per * h)
    a = jnp.stack([jnp.real(lam_bar).reshape(-1), jnp.imag(lam_bar).reshape(-1)])
    return a, bd(jnp.real(b_bar)), bd(jnp.imag(b_bar)), cd(c_re), cd(-c_im)


LATENT_WIDTH = 1536
_LATENT = {}
_off = 0
for _name, _w in (("u", SSM_WIDTH), ("q_lat", MLA_Q_RANK), ("kv_lat", MLA_KV_RANK), ("k_rope", LANES)):
    _LATENT[_name] = (_off, _off + _w)
    _off += _w


def _layer_matrices(w, l):
    return {**_mixer_matrices(w, l), **_tail_matrices(w, l)}


def _tail_matrices(w, l):
    return dict(x_q=w["x_w_q"][l], x_kv=w["x_w_kv"][l], x_o=w["x_w_o"][l], ffn_gu=w["ffn_w_gate_up"][l],
                ffn_d=w["ffn_w_down"][l])


def _mixer_matrices(w, l):
    w_in = w["w_in"][l]
    d, dt = w_in.shape[0], w_in.dtype
    z = lambda n: jnp.zeros((d, n), dt)
    r0 = SSM_WIDTH + MLA_Q_RANK + MLA_KV_RANK
    r1 = r0 + MLA_ROPE
    r2 = r1 + HG_PARTS * HG_WIDTH
    w_latent = jnp.concatenate([w_in[:, :r0], z(ROPE_LANE0), w_in[:, r0:r1],
                                z(LATENT_WIDTH - r0 - ROPE_LANE0 - MLA_ROPE)], axis=1)
    pad_heads = lambda t: jnp.pad(t, ((0, 0), (0, 0), (0, LANES - t.shape[2]))).reshape(t.shape[0], -1)
    uq = w["mla_w_uq"][l].reshape(MLA_Q_RANK, MLA_HEADS, MLA_NOPE + MLA_ROPE)
    ukv = w["mla_w_ukv"][l].reshape(MLA_KV_RANK, MLA_HEADS, MLA_NOPE + MLA_V)
    wo = w["mla_w_o"][l].reshape(MLA_HEADS, MLA_V, D_MODEL)
    return dict(
        w_latent=w_latent, w_hg=w_in[:, r1:r2], w_gates=w_in[:, r2:], glu=w["ssm_w_glu"][l],
        uq=pad_heads(uq), uk=pad_heads(ukv[:, :, :MLA_NOPE]), uv=pad_heads(ukv[:, :, MLA_NOPE:]),
        mla_o=jnp.pad(wo, ((0, 0), (0, LANES - MLA_V), (0, 0))).reshape(MLA_HEADS * LANES, D_MODEL),
        hg_o=w["hg_w_o"][l], w_out=w["w_out"][l])


def _layer(x, mem, tabs, m, sp, l, lower_bound):
    return _tail(_mixer(x, tabs, m, sp, l, lower_bound), mem, m, sp, l)


def _mixer(x, tabs, m, sp, l, lower_bound):
    row = lambda name: sp[name][l].reshape(1, -1)
    h, x = rms_res_op(x, row("norm_mix"))
    latent = matmul(h, m["w_latent"])
    seg = lambda name: latent[:, _LATENT[name][0]:_LATENT[name][1]]
    a, bd_r, bd_i, cd_r, cd_i = _s5_operators(*(sp[n][l] for n in (
        "ssm_lam_re", "ssm_lam_im", "ssm_b_re", "ssm_b_im", "ssm_c_re", "ssm_c_im", "ssm_log_step")))
    u = seg("u")
    y = s5_core(u, a, bd_r, bd_i, cd_r, cd_i)
    (ya,) = s5_act_op(y, u, row("ssm_d"))
    (y_ssm,) = glu_op(matmul(ya, m["glu"], BF16))
    (qn,) = rms_op(seg("q_lat"), row("mla_q_norm"))
    (kvn,) = rms_op(seg("kv_lat"), row("mla_kv_norm"))
    q, k = mla_rope_op(matmul(qn, m["uq"]), matmul(kvn, m["uk"]), seg("k_rope"), *tabs)
    o = causal_attention(q, k, matmul(kvn, m["uv"], BF16))
    y_mla = matmul(o, m["mla_o"], BF16)
    y_hg = matmul(hgrn2_mixer(h, m["w_hg"], lower_bound, row("hg_g_norm")), m["hg_o"], BF16)
    (merged,) = merge_op(y_ssm, y_mla, y_hg, matmul(h, m["w_gates"], BF16))
    return matmul_add(merged, m["w_out"], x)


def _tail(x, mem, m, sp, l):
    row = lambda name: sp[name][l].reshape(1, -1)
    hc, x = rms_res_op(x, row("norm_cross"))
    (mn,) = rms_mem_op(mem, row("norm_mem"))
    (ox,) = xattn_op(matmul(hc, m["x_q"], BF16), matmul(mn, m["x_kv"]))
    x = matmul_add(ox, m["x_o"], x)
    hf, x = rms_res_op(x, row("norm_ffn"))
    (act,) = swiglu_op(matmul(hf, m["ffn_gu"], BF16))
    return matmul_add(act, m["ffn_d"], x)


def _lower_bounds(hg_lb):
    lb_p = jax.nn.softmax(hg_lb, axis=0)
    return jnp.cumsum(lb_p, axis=0) - lb_p[0:1]


def _final_loss(target, x, norm_final):
    (row_loss,) = loss_op(x, target, norm_final.reshape(1, -1))
    return 0.5 * jnp.sum(row_loss[:, 0])


def _local_loss(x, mem, positions, target, w, sp):
    tabs = _rope_tables(positions)
    lower = _lower_bounds(sp["hg_lb"])
    for l in range(DEPTH):
        x = _layer(x, mem, tabs, _layer_matrices(w, l), sp, l, lower[l].reshape(1, -1))
    return _final_loss(target, x, sp["norm_final"])


N_DEV = 8
N_CHIPS = 4
COMM_LANES = 512
MESH_ID = pl.DeviceIdType.MESH
_ANY = pl.BlockSpec(memory_space=pl.ANY)
_OTHER_CHIPS = ((1, 0), (0, 1), (1, 1))


def _place():
    return lax.axis_index("x"), lax.axis_index("y"), lax.axis_index("c")


def _all_gather_call(blocks, name):
    n = len(blocks)

    def body(*refs):
        x_refs, out_refs = refs[:n], refs[n:2 * n]
        send_sems, recv_sems, local_sems = refs[2 * n:]
        x, y, c = _place()
        me, sibling = (x, y, c), (x, y, 1 - c)
        chips = [(x ^ fx, y ^ fy) for fx, fy in _OTHER_CHIPS]

        def slot(i, px, py, pc):
            return out_refs[i].at[4 * px + 2 * py + pc]

        def copy(i, k, blk, to, src=None):
            return pltpu.make_async_remote_copy(
                src_ref=slot(i, *blk) if src is None else src, dst_ref=slot(i, *blk),
                send_sem=send_sems.at[i, k], recv_sem=recv_sems.at[i, k], device_id=to, device_id_type=MESH_ID)

        mine = [pltpu.make_async_copy(x_refs[i], slot(i, *me), local_sems.at[i]) for i in range(n)]
        first = []
        for i in range(n):
            first.append(copy(i, 0, me, sibling, src=x_refs[i]))
            first += [copy(i, 1 + j, me, (*chip, c), src=x_refs[i]) for j, chip in enumerate(chips)]
        for cp in mine + first:
            cp.start()
        passed = []
        for j, chip in enumerate(chips):
            for i in range(n):
                copy(i, 1 + j, (*chip, c), me).wait_recv()
                passed.append(copy(i, 4 + j, (*chip, c), sibling))
                passed[-1].start()
        for i in range(n):
            copy(i, 0, sibling, me).wait_recv()
            for j, chip in enumerate(chips):
                copy(i, 4 + j, (*chip, 1 - c), me).wait_recv()
        for cp in first + passed:
            cp.wait_send()
        for cp in mine:
            cp.wait()

    return pl.pallas_call(
        body,
        name=name,
        out_shape=[jax.ShapeDtypeStruct((N_DEV,) + b.shape, b.dtype) for b in blocks],
        in_specs=[_ANY] * n,
        out_specs=[_ANY] * n,
        scratch_shapes=[pltpu.SemaphoreType.DMA((n, 7)), pltpu.SemaphoreType.DMA((n, 7)), pltpu.SemaphoreType.DMA((n,))],
    )(*blocks)


def _pair_exchange_call(gs, name):
    n = len(gs)

    def body(*refs):
        g_refs, got_refs = refs[:n], refs[n:2 * n]
        send_sems, recv_sems = refs[2 * n:]
        x, y, c = _place()
        sends = [pltpu.make_async_remote_copy(
            src_ref=g_refs[i].at[2 * p + (1 - c)], dst_ref=got_refs[i].at[p],
            send_sem=send_sems.at[i, p], recv_sem=recv_sems.at[i, p], device_id=(x, y, 1 - c), device_id_type=MESH_ID)
            for i in range(n) for p in range(N_CHIPS)]
        for cp in sends:
            cp.start()
        for cp in sends:
            cp.wait_recv()
        for cp in sends:
            cp.wait_send()

    return pl.pallas_call(
        body,
        name=name,
        out_shape=[jax.ShapeDtypeStruct((N_CHIPS,) + g.shape[1:], g.dtype) for g in gs],
        in_specs=[_ANY] * n,
        out_specs=[_ANY] * n,
        scratch_shapes=[pltpu.SemaphoreType.DMA((n, N_CHIPS))] * 2,
    )(*gs)


def _chip_exchange_call(parts, name):
    n = len(parts)

    def body(*refs):
        p_refs, got_refs = refs[:n], refs[n:2 * n]
        send_sems, recv_sems = refs[2 * n:]
        x, y, c = _place()
        sends = []
        for i in range(n):
            for k, (fx, fy) in enumerate(_OTHER_CHIPS):
                px, py = x ^ fx, y ^ fy
                sends.append(pltpu.make_async_remote_copy(
                    src_ref=p_refs[i].at[2 * px + py], dst_ref=got_refs[i].at[k],
                    send_sem=send_sems.at[i, k], recv_sem=recv_sems.at[i, k], device_id=(px, py, c), device_id_type=MESH_ID))
        for cp in sends:
            cp.start()
        for cp in sends:
            cp.wait_recv()
        for cp in sends:
            cp.wait_send()

    return pl.pallas_call(
        body,
        name=name,
        out_shape=[jax.ShapeDtypeStruct((3,) + p.shape[1:], p.dtype) for p in parts],
        in_specs=[_ANY] * n,
        out_specs=[_ANY] * n,
        scratch_shapes=[pltpu.SemaphoreType.DMA((n, 3))] * 2,
    )(*parts)


def _rows_cols(shape):
    return math.prod(shape[:-1]), shape[-1]


def _pair_sum_call(g, got, c_idx, name):
    rows, cols = _rows_cols(got.shape[1:])
    tr = _pick_tile(rows, (512, 256, 128, 64, 32, 16))

    def body(c_ref, a_ref, b_ref, o_ref):
        o_ref[...] = (a_ref[...].astype(F32) + b_ref[...].astype(F32)).astype(o_ref.dtype)

    spec = pl.BlockSpec((1, tr, cols), lambda p, i, c_ref: (p, i, 0))
    out = pl.pallas_call(
        body,
        name=name,
        grid_spec=pltpu.PrefetchScalarGridSpec(
            num_scalar_prefetch=1, grid=(N_CHIPS, rows // tr),
            in_specs=[pl.BlockSpec((1, tr, cols), lambda p, i, c_ref: (2 * p + c_ref[0], i, 0)), spec],
            out_specs=spec),
        out_shape=jax.ShapeDtypeStruct((N_CHIPS, rows, cols), got.dtype),
        compiler_params=_cparams("parallel", "parallel"),
    )(c_idx, g.reshape(N_DEV, rows, cols), got.reshape(N_CHIPS, rows, cols))
    return out.reshape(got.shape)


def _chip_sum_call(part, got, chip_idx, name):
    rows, cols = _rows_cols(got.shape[1:])
    tr = _pick_tile(rows, (512, 256, 128, 64, 32, 16))

    def body(p_ref, a_ref, b_ref, o_ref):
        acc = a_ref[0].astype(F32)
        for k in range(3):
            acc = acc + b_ref[k].astype(F32)
        o_ref[...] = acc

    out = pl.pallas_call(
        body,
        name=name,
        grid_spec=pltpu.PrefetchScalarGridSpec(
            num_scalar_prefetch=1, grid=(rows // tr,),
            in_specs=[pl.BlockSpec((1, tr, cols), lambda i, p_ref: (p_ref[0], i, 0)),
                      pl.BlockSpec((3, tr, cols), lambda i, p_ref: (0, i, 0))],
            out_specs=pl.BlockSpec((tr, cols), lambda i, p_ref: (i, 0))),
        out_shape=jax.ShapeDtypeStruct((rows, cols), F32),
        compiler_params=_cparams("parallel"),
    )(chip_idx, part.reshape(N_CHIPS, rows, cols), got.reshape(3, rows, cols))
    return out.reshape(got.shape[1:])


def _reduce_scatter(gs, name):
    x, y, c = _place()
    c_idx = c.astype(jnp.int32).reshape(1)
    chip_idx = (2 * x + y).astype(jnp.int32).reshape(1)
    gots = _pair_exchange_call(gs, name + "_pair")
    parts = [_pair_sum_call(g, got, c_idx, name + "_pair_sum") for g, got in zip(gs, gots)]
    gots = _chip_exchange_call(parts, name + "_chip")
    return [_chip_sum_call(p, got, chip_idx, name + "_chip_sum") for p, got in zip(parts, gots)]


_HBM = pl.BlockSpec(memory_space=pltpu.HBM)
_SEM = pl.BlockSpec(memory_space=pltpu.SEMAPHORE)
_SIDE_EFFECT = pltpu.SideEffectType.DATAFLOW_SIDE_EFFECTING
N_PEERS = N_DEV - 1


def _peer(k):
    x, y, c = _place()
    px, py, pc = x ^ ((k >> 2) & 1), y ^ ((k >> 1) & 1), c ^ (k & 1)
    return (px, py, pc), 4 * px + 2 * py + pc


def _exchange_copy(src_ref, land_ref, send_sems, recv_sems, i, k, scatter, receiving):
    x, y, c = _place()
    me = 4 * x + 2 * y + c
    peer, peer_idx = _peer(k)
    sem = i * N_PEERS + k - 1
    return pltpu.make_async_remote_copy(
        src_ref=src_ref.at[peer_idx] if scatter else src_ref, dst_ref=land_ref.at[peer_idx if receiving else me],
        send_sem=send_sems.at[sem], recv_sem=recv_sems.at[sem], device_id=peer, device_id_type=MESH_ID)


def _exchange_start_call(srcs, after, scatter, name):
    n = len(srcs)
    slot_shapes = [s.shape[1:] if scatter else s.shape for s in srcs]

    def body(*refs):
        src_refs, land_refs = refs[:n], refs[n:2 * n]
        send_sems, recv_sems = refs[2 * n + 1], refs[2 * n + 2]
        token = refs[-1]
        for i in range(n):
            for k in range(1, N_DEV):
                _exchange_copy(src_refs[i], land_refs[i], send_sems, recv_sems, i, k, scatter, False).start()
        token[...] = jnp.zeros_like(token)

    lands = [pltpu.with_memory_space_constraint(lax.empty((N_DEV,) + shp, s.dtype), pltpu.HBM)
             for shp, s in zip(slot_shapes, srcs)]
    out = pl.pallas_call(
        body,
        name=name,
        out_shape=([pltpu.SemaphoreType.DMA((n * N_PEERS,)), pltpu.SemaphoreType.DMA((n * N_PEERS,))]
                   + [pltpu.HBM(s.shape, s.dtype) for s in srcs] + [pltpu.HBM(l.shape, l.dtype) for l in lands]
                   + [jax.ShapeDtypeStruct((SUBLANES, LANES), F32)]),
        in_specs=[_HBM] * (2 * n) + [pl.BlockSpec(memory_space=pl.ANY)],
        out_specs=[_SEM, _SEM] + [_HBM] * (2 * n) + [pl.BlockSpec(memory_space=pltpu.VMEM)],
        input_output_aliases={j: 2 + j for j in range(2 * n)},
        compiler_params=pltpu.CompilerParams(has_side_effects=_SIDE_EFFECT),
    )(*[pltpu.with_memory_space_constraint(s, pltpu.HBM) for s in srcs], *lands, after)
    return out[0], out[1], list(out[2:2 + n]), list(out[2 + n:2 + 2 * n]), out[-1]


def _exchange_wait_call(started, after, scatter, name):
    send_sems, recv_sems, srcs, lands, _ = started
    n = len(srcs)

    def body(*refs):
        src_refs, land_refs = refs[:n], refs[n:2 * n]
        send_s, recv_s = refs[2 * n], refs[2 * n + 1]
        for i in range(n):
            for k in range(1, N_DEV):
                cp = _exchange_copy(src_refs[i], land_refs[i], send_s, recv_s, i, k, scatter, True)
                cp.wait_send()
                cp.wait_recv()

    out = pl.pallas_call(
        body,
        name=name,
        out_shape=[pltpu.HBM(s.shape, s.dtype) for s in srcs] + [pltpu.HBM(l.shape, l.dtype) for l in lands],
        in_specs=[_HBM] * (2 * n) + [_SEM, _SEM, pl.BlockSpec(memory_space=pl.ANY)],
        out_specs=[_HBM] * (2 * n),
        input_output_aliases={j: j for j in range(2 * n)},
        compiler_params=pltpu.CompilerParams(has_side_effects=_SIDE_EFFECT),
    )(*srcs, *lands, send_sems, recv_sems, after)
    return list(out[n:])


def _own_slot(land, own):
    x, y, c = _place()
    return lax.dynamic_update_index_in_dim(land, own, 4 * x + 2 * y + c, 0)


def _slot_sum_call(land, name):
    rows, cols = _rows_cols(land.shape[1:])
    tr = _pick_tile(rows, (256, 128, 64, 32, 16))

    def body(land_ref, o_ref):
        acc = land_ref[0].astype(F32)
        for s in range(1, N_DEV):
            acc = acc + land_ref[s].astype(F32)
        o_ref[...] = acc

    out = pl.pallas_call(
        body,
        name=name,
        grid=(rows // tr,),
        in_specs=[pl.BlockSpec((N_DEV, tr, cols), lambda i: (0, i, 0))],
        out_specs=pl.BlockSpec((tr, cols), lambda i: (i, 0)),
        out_shape=jax.ShapeDtypeStruct((rows, cols), F32),
        compiler_params=_cparams("parallel"),
    )(land.reshape(N_DEV, rows, cols))
    return out.reshape(land.shape[1:])


SMALL_BLOCK_ROWS = 16


def _pack_small(parts):
    flat = jnp.concatenate([p.reshape(-1) for p in parts])
    chunk = N_DEV * SMALL_BLOCK_ROWS * COMM_LANES
    flat = jnp.pad(flat, (0, (-flat.shape[0]) % chunk))
    return flat.reshape(N_DEV, -1, COMM_LANES)


def _unpack_small(buf, shapes):
    flat = buf.reshape(-1)
    out, off = [], 0
    for shp in shapes:
        n = math.prod(shp)
        out.append(flat[off:off + n].reshape(shp))
        off += n
    return out


SHARDED = dict(w_in=2, ssm_w_glu=2, mla_w_uq=2, mla_w_ukv=2, mla_w_o=2, hg_w_o=2, w_out=1, x_w_q=1, x_w_kv=1,
               x_w_o=2, ffn_w_gate_up=2, ffn_w_down=1)
REPLICATED = ("norm_mix", "ssm_lam_re", "ssm_lam_im", "ssm_b_re", "ssm_b_im", "ssm_c_re", "ssm_c_im", "ssm_d",
              "ssm_log_step", "mla_q_norm", "mla_kv_norm", "hg_lb", "hg_g_norm", "norm_cross", "norm_mem", "norm_ffn",
              "norm_final")


def _join_shards(stacked, axis):
    n, l, a, b = stacked.shape
    if axis == 1:
        return stacked.transpose(1, 0, 2, 3).reshape(l, n * a, b)
    return stacked.transpose(1, 2, 0, 3).reshape(l, a, n * b)


def _split_shards(full, axis):
    l, a, b = full.shape
    if axis == 1:
        return full.reshape(l, N_DEV, a // N_DEV, b).transpose(1, 0, 2, 3)
    return full.reshape(l, a, N_DEV, b // N_DEV).transpose(2, 0, 1, 3)


MIXER_WEIGHTS = ("w_in", "ssm_w_glu", "mla_w_uq", "mla_w_ukv", "mla_w_o", "hg_w_o", "w_out")
TAIL_WEIGHTS = ("x_w_q", "x_w_kv", "x_w_o", "ffn_w_gate_up", "ffn_w_down")


def _mixer_fn(l, tabs):
    def f(x, full, small, lower):
        m = _mixer_matrices(dict(zip(MIXER_WEIGHTS, full)), 0)
        return _mixer(x, tabs, m, dict(zip(REPLICATED, small)), l, lower[l].reshape(1, -1))
    return f


def _tail_fn(l, mem):
    def f(x, full, small):
        return _tail(x, mem, _tail_matrices(dict(zip(TAIL_WEIGHTS, full)), 0), dict(zip(REPLICATED, small)), l)
    return f


ADAM_LR, ADAM_B1, ADAM_B2, ADAM_EPS, ADAM_WD, ADAM_STEP = 0.001, 0.9, 0.999, 1e-08, 0.01, 10


def _adamw_update(w, g, m, v):
    m_new = ADAM_B1 * m + (1.0 - ADAM_B1) * g
    v_new = ADAM_B2 * v + (1.0 - ADAM_B2) * jnp.square(g)
    m_hat = m_new / (1.0 - ADAM_B1 ** ADAM_STEP)
    v_hat = v_new / (1.0 - ADAM_B2 ** ADAM_STEP)
    return -ADAM_LR * (m_hat / (jnp.sqrt(v_hat) + ADAM_EPS) + ADAM_WD * w), m_new, v_new


def _adamw_stacked_call(w, g, m, v, name):
    depth, rows, cols = w.shape
    tr = _pick_tile(rows, (512, 256, 128, 64, 32, 16, 8))

    def body(w_ref, g_ref, m_ref, v_ref, d_ref, nm_ref, nv_ref):
        d_ref[...], nm_ref[...], nv_ref[...] = _adamw_update(w_ref[...], g_ref[...], m_ref[...], v_ref[...])

    spec = pl.BlockSpec((None, tr, cols), lambda l, i: (l, i, 0))
    return tuple(pl.pallas_call(
        body, name=name, grid=(depth, rows // tr), in_specs=[spec] * 4, out_specs=[spec] * 3,
        out_shape=[jax.ShapeDtypeStruct(w.shape, F32)] * 3, compiler_params=_cparams("parallel", "parallel"),
    )(w, g, m, v))


def _adamw_call(w, g, m, v, name):
    shape = w.shape
    if len(shape) == 3:
        return _adamw_stacked_call(w, g, m, v, name)
    cols = shape[-1]
    rows = math.prod(shape[:-1]) if len(shape) > 1 else 1
    tr = _pick_tile(rows, (512, 256, 128, 64, 32, 16, 8))

    def body(w_ref, g_ref, m_ref, v_ref, d_ref, nm_ref, nv_ref):
        d_ref[...], nm_ref[...], nv_ref[...] = _adamw_update(w_ref[...], g_ref[...], m_ref[...], v_ref[...])

    spec = pl.BlockSpec((tr, cols), lambda i: (i, 0))
    outs = pl.pallas_call(
        body, name=name, grid=(rows // tr,), in_specs=[spec] * 4, out_specs=[spec] * 3,
        out_shape=[jax.ShapeDtypeStruct((rows, cols), F32)] * 3, compiler_params=_cparams("parallel"),
    )(*(t.reshape(rows, cols) for t in (w, g, m, v)))
    return tuple(o.reshape(shape) for o in outs)


WEIGHTS = ("norm_mix", "w_in", "ssm_lam_re", "ssm_lam_im", "ssm_b_re", "ssm_b_im", "ssm_c_re", "ssm_c_im", "ssm_d",
           "ssm_log_step", "ssm_w_glu", "mla_q_norm", "mla_kv_norm", "mla_w_uq", "mla_w_ukv", "mla_w_o", "hg_lb",
           "hg_g_norm", "hg_w_o", "w_out", "norm_cross", "norm_mem", "x_w_q", "x_w_kv", "x_w_o", "norm_ffn",
           "ffn_w_gate_up", "ffn_w_down", "norm_final")


def kernel(x, mem, positions, norm_mix, w_in, ssm_lam_re, ssm_lam_im, ssm_b_re, ssm_b_im, ssm_c_re, ssm_c_im, ssm_d, ssm_log_step, ssm_w_glu, mla_q_norm, mla_kv_norm, mla_w_uq, mla_w_ukv, mla_w_o, hg_lb, hg_g_norm, hg_w_o, w_out, norm_cross, norm_mem, x_w_q, x_w_kv, x_w_o, norm_ffn, ffn_w_gate_up, ffn_w_down, norm_final, loss_target, m_norm_mix, m_w_in, m_ssm_lam_re, m_ssm_lam_im, m_ssm_b_re, m_ssm_b_im, m_ssm_c_re, m_ssm_c_im, m_ssm_d, m_ssm_log_step, m_ssm_w_glu, m_mla_q_norm, m_mla_kv_norm, m_mla_w_uq, m_mla_w_ukv, m_mla_w_o, m_hg_lb, m_hg_g_norm, m_hg_w_o, m_w_out, m_norm_cross, m_norm_mem, m_x_w_q, m_x_w_kv, m_x_w_o, m_norm_ffn, m_ffn_w_gate_up, m_ffn_w_down, m_norm_final, v_norm_mix, v_w_in, v_ssm_lam_re, v_ssm_lam_im, v_ssm_b_re, v_ssm_b_im, v_ssm_c_re, v_ssm_c_im, v_ssm_d, v_ssm_log_step, v_ssm_w_glu, v_mla_q_norm, v_mla_kv_norm, v_mla_w_uq, v_mla_w_ukv, v_mla_w_o, v_hg_lb, v_hg_g_norm, v_hg_w_o, v_w_out, v_norm_cross, v_norm_mem, v_x_w_q, v_x_w_kv, v_x_w_o, v_norm_ffn, v_ffn_w_gate_up, v_ffn_w_down, v_norm_final):
    given = dict(locals())
    weights = {n: given[n] for n in WEIGHTS}
    small = tuple(weights[n] for n in REPLICATED)
    layer1 = MIXER_WEIGHTS + TAIL_WEIGHTS
    shards = lambda names, l: [weights[n][l:l + 1].astype(BF16) for n in names]
    join = lambda names, stacked: tuple(_join_shards(p, SHARDED[n]) for n, p in zip(names, stacked))
    split = lambda names, cts: [_split_shards(ct, SHARDED[n]) for n, ct in zip(names, cts)]
    landed = lambda names, lands, own: join(names, [_own_slot(land, o) for land, o in zip(lands, own)])
    xs, tabs = x[0], _rope_tables(positions[0])
    lower, vjp_lower = jax.vjp(_lower_bounds, hg_lb)
    me = 4 * lax.axis_index("x") + 2 * lax.axis_index("y") + lax.axis_index("c")

    got_m0 = _all_gather_call(shards(MIXER_WEIGHTS, 0), "weights_all_gather_m0")
    own_t0, own_l1 = shards(TAIL_WEIGHTS, 0), shards(layer1, 1)
    gather_t0 = _exchange_start_call(own_t0, got_m0[0], False, "weights_gather_start_t0")
    gather_l1 = _exchange_start_call(own_l1, gather_t0[4], False, "weights_gather_start_l1")
    xs = xs + gather_l1[4][0, 0]
    xa0, vjp_m0 = jax.vjp(_mixer_fn(0, tabs), xs, join(MIXER_WEIGHTS, got_m0), small, lower)
    full_t0 = landed(TAIL_WEIGHTS, _exchange_wait_call(gather_t0, xa0, False, "weights_gather_wait_t0"), own_t0)
    x1, vjp_t0 = jax.vjp(_tail_fn(0, mem[0]), xa0, full_t0, small)
    full_l1 = landed(layer1, _exchange_wait_call(gather_l1, x1, False, "weights_gather_wait_l1"), own_l1)
    xa1, vjp_m1 = jax.vjp(_mixer_fn(1, tabs), x1, full_l1[:len(MIXER_WEIGHTS)], small, lower)
    x2, vjp_t1 = jax.vjp(_tail_fn(1, mem[0]), xa1, full_l1[len(MIXER_WEIGHTS):], small)
    loss_local, vjp_loss = jax.vjp(functools.partial(_final_loss, loss_target[0]), x2, norm_final)

    def scatter_start(names, cts, dx, tag):
        gs = split(names, cts)
        started = _exchange_start_call(gs, dx, True, "grads_scatter_start_" + tag)
        return (started, gs), dx + started[4][0, 0]

    def scatter_finish(pending, after, tag):
        started, gs = pending
        lands = _exchange_wait_call(started, after, True, "grads_scatter_wait_" + tag)
        return [_slot_sum_call(_own_slot(land, lax.dynamic_index_in_dim(g, me, 0, keepdims=False)), "grads_slot_sum_" + tag)
                for land, g in zip(lands, gs)]

    dx2, d_norm_final = vjp_loss(jnp.ones((), F32))
    dxa1, dfull_t1, dsmall_t1 = vjp_t1(dx2)
    dx1, dfull_m1, dsmall_m1, dlower1 = vjp_m1(dxa1)
    pend_l1, dx1 = scatter_start(layer1, dfull_m1 + dfull_t1, dx1, "l1")
    dxa0, dfull_t0, dsmall_t0 = vjp_t0(dx1)
    pend_t0, dxa0 = scatter_start(TAIL_WEIGHTS, dfull_t0, dxa0, "t0")
    gx, dfull_m0, dsmall_m0, dlower0 = vjp_m0(dxa0)
    by_layer = {
        0: dict(zip(MIXER_WEIGHTS + TAIL_WEIGHTS,
                    _reduce_scatter(split(MIXER_WEIGHTS, dfull_m0), "grads_reduce_scatter_m0") + scatter_finish(pend_t0, gx, "t0"))),
        1: dict(zip(layer1, scatter_finish(pend_l1, gx, "l1")))}
    grads = {n: jnp.concatenate([by_layer[0][n], by_layer[1][n]], axis=0) for n in SHARDED}

    d_small = dict(zip(REPLICATED, (a + b + c + d for a, b, c, d in zip(dsmall_m0, dsmall_t0, dsmall_m1, dsmall_t1))))
    d_small["norm_final"] = d_small["norm_final"] + d_norm_final
    d_small["hg_lb"] = d_small["hg_lb"] + vjp_lower(dlower0 + dlower1)[0]
    shapes = [d_small[n].shape for n in REPLICATED]
    (mine,) = _reduce_scatter([_pack_small([d_small[n] for n in REPLICATED])], "small_reduce_scatter")
    (total,) = _all_gather_call([mine], "small_all_gather")
    grads.update(zip(REPLICATED, _unpack_small(total, shapes)))

    loss = lax.psum(loss_local, ("x", "y", "c"))
    steps = {n: _adamw_call(weights[n], grads[n], given["m_" + n], given["v_" + n], "adamw_" + n) for n in WEIGHTS}
    return (loss, gx[None], *[grads[n] for n in WEIGHTS], *[steps[n][0] for n in WEIGHTS],
            *[steps[n][1] for n in WEIGHTS], *[steps[n][2] for n in WEIGHTS])
```

```python
import functools
import math

import jax
import jax.numpy as jnp
from jax import lax
from jax.experimental import pallas as pl
from jax.experimental.pallas import tpu as pltpu

F32 = jnp.float32
BF16 = jnp.bfloat16

VMEM_LIMIT_BYTES = 48 * 1024 * 1024
LANES = 128
SUBLANES = 8


def _cparams(*sem):
    return pltpu.CompilerParams(dimension_semantics=sem, vmem_limit_bytes=VMEM_LIMIT_BYTES)


def _pick_tile(n, cands):
    for c in cands:
        if n % c == 0:
            return c
    return n


MM_VMEM_BUDGET = 38 * 1024 * 1024
MM_STEP_US = 0.35
HBM_BYTES_PER_US = 3.0e6
VREG_RMW_PER_US = 1.5e3


def _divisor_tiles(dim, cands):
    out = [t for t in cands if dim % t == 0]
    return out or [dim]


def _mm_tiles(m, n, k, sa, sb, so):
    tms = _divisor_tiles(m, (1408, 1024, 512, 256, 128, 64, 32, 16, 8))[:2]
    tns = _divisor_tiles(n, (2048, 1536, 1408, 1024, 768, 512, 384, 256, 128))
    tks = [k // d for d in (1, 2, 4, 8, 13, 16, 26, 32, 52) if k % d == 0 and (k // d) % LANES == 0] or [k]
    best = None
    for tk in tks:
        nk = k // tk
        for tm in tms:
            for tn in tns:
                vmem = 2 * (tm * tk * sa + tk * tn * sb + tm * tn * so) + (tm * tn * 4 if nk > 1 else 0)
                if vmem > MM_VMEM_BUDGET:
                    continue
                steps = (m // tm) * (n // tn) * nk
                a_reads = m * k * sa * (n // tn if nk > 1 else 1)
                b_reads = k * n * sb * (m // tm if (nk > 1 or n // tn > 1) else 1)
                cost = (steps * MM_STEP_US + (a_reads + b_reads) / HBM_BYTES_PER_US
                        + (m * n * nk / 1024 / VREG_RMW_PER_US if nk > 1 else 0.0))
                if best is None or cost < best[0]:
                    best = (cost, tm, tn, tk)
    assert best is not None, (m, n, k)
    return best[1:]


def _mm_tiles_cached_t(m, n, k, sa, sb, so):
    for tm in _divisor_tiles(m, (1024, 512)):
        if m % tm:
            break
        for tn in _divisor_tiles(n, (1024, 512, 384, 256, 128)):
            if 2 * (k * tm * sa + k * tn * sb + tm * tn * so) + tm * k * 2 <= MM_VMEM_BUDGET:
                return tm, tn
    return None


def _mm_tn_cached_call(a, b, tiles, out_dtype, name):
    k, m = a.shape
    n = b.shape[1]
    tm, tn = tiles

    def body(a_ref, b_ref, o_ref, at_ref):
        @pl.when(pl.program_id(1) == 0)
        def _():
            at_ref[...] = a_ref[...].astype(BF16).T

        o_ref[...] = lax.dot_general(at_ref[...], b_ref[...].astype(BF16), _NN_DIMS,
                                     preferred_element_type=F32).astype(out_dtype)

    return pl.pallas_call(
        body,
        name=name,
        grid=(m // tm, n // tn),
        in_specs=[pl.BlockSpec((k, tm), lambda i, j: (0, i)), pl.BlockSpec((k, tn), lambda i, j: (0, j))],
        out_specs=pl.BlockSpec((tm, tn), lambda i, j: (i, j)),
        out_shape=jax.ShapeDtypeStruct((m, n), out_dtype),
        scratch_shapes=[pltpu.VMEM((tm, k), BF16)],
        compiler_params=_cparams("parallel", "arbitrary"),
    )(a, b)


_NN_DIMS = (((1,), (0,)), ((), ()))


def _mm_call(a, b, ta, tb, add=None, out_dtype=F32, name="mm"):
    m, k = (a.shape[1], a.shape[0]) if ta else a.shape
    k2, n = (b.shape[1], b.shape[0]) if tb else b.shape
    assert k == k2, (a.shape, b.shape, ta, tb)
    sizes = (a.dtype.itemsize, b.dtype.itemsize, jnp.dtype(out_dtype).itemsize + (add.dtype.itemsize if add is not None else 0))
    if ta:
        tiles = _mm_tiles_cached_t(m, n, k, *sizes)
        if tiles is not None:
            return _mm_tn_cached_call(a, b, tiles, out_dtype, name)
    tm, tn, tk = _mm_tiles(m, n, k, *sizes)
    nk = k // tk
    a_spec = pl.BlockSpec((tk, tm), lambda i, j, kk: (kk, i)) if ta else pl.BlockSpec((tm, tk), lambda i, j, kk: (i, kk))
    b_spec = pl.BlockSpec((tn, tk), lambda i, j, kk: (j, kk)) if tb else pl.BlockSpec((tk, tn), lambda i, j, kk: (kk, j))
    o_spec = pl.BlockSpec((tm, tn), lambda i, j, kk: (i, j))
    dn = (((0 if ta else 1,), (1 if tb else 0,)), ((), ()))
    has_add = add is not None

    def body(*refs):
        a_ref, b_ref = refs[0], refs[1]
        c_ref = refs[2] if has_add else None
        o_ref = refs[3] if has_add else refs[2]
        p = lax.dot_general(a_ref[...].astype(BF16), b_ref[...].astype(BF16), dn, preferred_element_type=F32)

        def finish(r):
            if has_add:
                r = r + c_ref[...].astype(F32)
            o_ref[...] = r.astype(out_dtype)

        if nk == 1:
            finish(p)
        else:
            acc_ref = refs[-1]
            kk = pl.program_id(2)

            @pl.when(kk == 0)
            def _():
                acc_ref[...] = p

            @pl.when(kk > 0)
            def _():
                acc_ref[...] += p

            @pl.when(kk == nk - 1)
            def _():
                finish(acc_ref[...])

    in_specs = [a_spec, b_spec] + ([o_spec] if has_add else [])
    args = (a, b) + ((add,) if has_add else ())
    return pl.pallas_call(
        body,
        name=name,
        grid=(m // tm, n // tn, nk),
        in_specs=in_specs,
        out_specs=o_spec,
        out_shape=jax.ShapeDtypeStruct((m, n), out_dtype),
        scratch_shapes=[] if nk == 1 else [pltpu.VMEM((tm, tn), F32)],
        compiler_params=_cparams("parallel", "parallel", "arbitrary"),
    )(*args)


@functools.partial(jax.custom_vjp, nondiff_argnums=(2,))
def matmul(a, b, out_dtype=F32):
    return _mm_call(a, b, False, False, out_dtype=out_dtype, name="mm_fwd")


def _matmul_fwd(a, b, out_dtype):
    return matmul(a, b, out_dtype), (a, b)


def _matmul_bwd(out_dtype, res, g):
    a, b = res
    da = _mm_call(g, b, False, True, out_dtype=a.dtype, name="mm_da")
    db = _mm_call(a, g, True, False, out_dtype=b.dtype, name="mm_db")
    return da, db


matmul.defvjp(_matmul_fwd, _matmul_bwd)


@jax.custom_vjp
def matmul_add(a, b, c):
    return _mm_call(a, b, False, False, add=c, name="mm_add_fwd")


def _matmul_add_fwd(a, b, c):
    return _mm_call(a, b, False, False, add=c, name="mm_add_fwd"), (a, b)


def _matmul_add_bwd(res, g):
    a, b = res
    da = _mm_call(g, b, False, True, out_dtype=a.dtype, name="mm_da")
    db = _mm_call(a, g, True, False, out_dtype=b.dtype, name="mm_db")
    return da, db, g


matmul_add.defvjp(_matmul_add_fwd, _matmul_add_bwd)


def rowwise(f, n_rows, n_aux, tile, name, passthrough=False):
    def specs(arrs, tiled):
        out = []
        for x in arrs:
            if tiled:
                out.append(pl.BlockSpec((tile, x.shape[1]), lambda i: (i, 0)))
            else:
                out.append(pl.BlockSpec(x.shape, lambda i: (0, 0)))
        return out

    def tile_structs(args):
        rows_aux, params = args[: n_rows + n_aux], args[n_rows + n_aux:]
        return [jax.ShapeDtypeStruct((tile, x.shape[1]), x.dtype) for x in rows_aux] + [
            jax.ShapeDtypeStruct(p.shape, p.dtype) for p in params]

    def fwd_call(*args):
        s = args[0].shape[0]
        outs = jax.eval_shape(f, *tile_structs(args))
        n_in = len(args)

        def body(*refs):
            vals = [r[...] for r in refs[:n_in]]
            res = f(*vals)
            for o_ref, r in zip(refs[n_in:], res):
                o_ref[...] = r.astype(o_ref.dtype)

        return pl.pallas_call(
            body,
            name=name + "_fwd",
            grid=(s // tile,),
            in_specs=specs(args[: n_rows + n_aux], True) + specs(args[n_rows + n_aux:], False),
            out_specs=[pl.BlockSpec((tile, o.shape[1]), lambda i: (i, 0)) for o in outs],
            out_shape=[jax.ShapeDtypeStruct((s, o.shape[1]), o.dtype) for o in outs],
            compiler_params=_cparams("parallel"),
        )(*args)

    def bwd_call(args, gs):
        s = args[0].shape[0]
        rows, aux, params = args[:n_rows], args[n_rows:n_rows + n_aux], args[n_rows + n_aux:]
        n_in, n_g, n_p = len(args), len(gs), len(params)
        n_gf = n_g - 1 if passthrough else n_g

        def body(*refs):
            vals = [r[...] for r in refs[:n_in]]
            gvals = tuple(r[...] for r in refs[n_in:n_in + n_gf])
            out_refs = refs[n_in + n_g:]
            auxv = vals[n_rows:n_rows + n_aux]

            def g_(*rp):
                return tuple(f(*rp[:n_rows], *auxv, *rp[n_rows:]))

            _, vjp = jax.vjp(g_, *vals[:n_rows], *vals[n_rows + n_aux:])
            cts = list(vjp(gvals))
            if passthrough:
                cts[0] = cts[0] + refs[n_in + n_gf][...]
            for o_ref, ct in zip(out_refs[:n_rows], cts[:n_rows]):
                o_ref[...] = ct.astype(o_ref.dtype)
            if n_p:
                @pl.when(pl.program_id(0) == 0)
                def _():
                    for o_ref in out_refs[n_rows:]:
                        o_ref[...] = jnp.zeros_like(o_ref)

                for o_ref, ct in zip(out_refs[n_rows:], cts[n_rows:]):
                    o_ref[...] += ct.astype(o_ref.dtype)

        return pl.pallas_call(
            body,
            name=name + "_bwd",
            grid=(s // tile,),
            in_specs=specs(rows + aux, True) + specs(params, False) + specs(gs, True),
            out_specs=specs(rows, True) + specs(params, False),
            out_shape=[jax.ShapeDtypeStruct(x.shape, x.dtype) for x in rows + params],
            compiler_params=_cparams("arbitrary" if n_p else "parallel"),
        )(*args, *gs)

    @jax.custom_vjp
    def op(*args):
        return tuple(fwd_call(*args)) + ((args[0],) if passthrough else ())

    def op_fwd(*args):
        return op(*args), args

    def op_bwd(args, gs):
        cts = bwd_call(tuple(args), tuple(gs))
        rows_ct, par_ct = cts[:n_rows], cts[n_rows:]
        aux_ct = [jnp.zeros_like(a) for a in args[n_rows:n_rows + n_aux]]
        return tuple(rows_ct) + tuple(aux_ct) + tuple(par_ct)

    op.defvjp(op_fwd, op_bwd)
    return op


SCAN_SEGMENTS = SUBLANES


def _scan_step(ar, ai, xr, xi, br, bi):
    return ar * xr - ai * xi + br, ar * xi + ai * xr + bi


_NT_DIMS = (((1,), (1,)), ((), ()))
_TN_DIMS = (((0,), (0,)), ((), ()))


S5_GROUPS_PER_STEP = 128
S5_STATE_LANES = 512


def _s5_interleave(src_ref, scr_ref, rows):
    for k in range(SCAN_SEGMENTS):
        scr_ref[pl.ds(k, rows, stride=SCAN_SEGMENTS), :] = src_ref[k].astype(F32)


def _s5_deinterleave(val, scr_ref, dst_ref, rows):
    scr_ref[...] = val
    for k in range(SCAN_SEGMENTS):
        dst_ref[k] = scr_ref[pl.ds(k, rows, stride=SCAN_SEGMENTS), :]


def _s5_segment_starts(a_ref, fr_ref, fi_ref, sr, si, seg_len, order):
    tn = sr.shape[1]
    pr, pi = a_ref[0:1, :], a_ref[1:2, :]
    for _ in range(seg_len.bit_length() - 1):
        pr, pi = pr * pr - pi * pi, 2.0 * pr * pi
    cr = jnp.zeros((1, tn), F32)
    ci = jnp.zeros((1, tn), F32)
    for idx, k in enumerate(order):
        if idx > 0:
            kp = order[idx - 1]
            cr, ci = (fr_ref[kp:kp + 1, :] + pr * cr - pi * ci, fi_ref[kp:kp + 1, :] + pr * ci + pi * cr)
        sr[k:k + 1, :] = cr
        si[k:k + 1, :] = ci


def _s5_pass_call(src, w_r, w_i, a, transpose_w, reverse, finals, extra, name):
    s = src.shape[0]
    nb = w_r.shape[0]
    n = nb * S5_STATE_LANES
    seg_len = s // SCAN_SEGMENTS
    ti = min(S5_GROUPS_PER_STEP, seg_len)
    nt = seg_len // ti
    tr = SCAN_SEGMENTS * ti
    tn = S5_STATE_LANES
    assert seg_len & (seg_len - 1) == 0
    order = list(range(SCAN_SEGMENTS))[::-1] if reverse else list(range(SCAN_SEGMENTS))
    dn_in = _NT_DIMS if transpose_w else _NN_DIMS
    first = finals is None
    backward = (not first) and reverse
    forward = (not first) and not reverse
    tmap3 = (lambda c, j: (0, nt - 1 - j, c)) if reverse else (lambda c, j: (0, j, c))
    tmap2 = (lambda c, j: (nt - 1 - j, c)) if reverse else (lambda c, j: (j, c))
    bf = lambda v: v.astype(BF16)

    def body(*refs):
        it = iter(refs)
        src_ref, wr_ref, wi_ref, a_ref = next(it), next(it), next(it), next(it)
        if not first:
            fr_ref, fi_ref = next(it), next(it)
        if forward:
            cdr_ref, cdi_ref = next(it), next(it)
            xr_out, xi_out, y_ref = next(it), next(it), next(it)
        if backward:
            xr_ref, xi_ref, u_ref, bdr_ref, bdi_ref = next(it), next(it), next(it), next(it), next(it)
            du_ref, dar_ref, dai_ref, dbr_ref, dbi_ref, dcr_ref, dci_ref = (next(it) for _ in range(7))
        if first:
            fr_out, fi_out = next(it), next(it)
        sr, si, in_scr, dr_scr, di_scr = next(it), next(it), next(it), next(it), next(it)
        if backward:
            accr, acci, u_scr = next(it), next(it), next(it)
        j = pl.program_id(1)

        @pl.when(j == 0)
        def _():
            if first:
                sr[...] = jnp.zeros_like(sr)
                si[...] = jnp.zeros_like(si)
            else:
                _s5_segment_starts(a_ref, fr_ref, fi_ref, sr, si, seg_len, order)
            if backward:
                for r in (accr, acci, dbr_ref, dbi_ref, dcr_ref, dci_ref):
                    r[...] = jnp.zeros_like(r)

        _s5_interleave(src_ref, in_scr, ti)
        src_b = bf(in_scr[...])
        dr_scr[...] = lax.dot_general(src_b, bf(wr_ref[0]), dn_in, preferred_element_type=F32)
        di_scr[...] = lax.dot_general(src_b, bf(wi_ref[0]), dn_in, preferred_element_type=F32)
        ar = jnp.broadcast_to(a_ref[0:1, :], (SUBLANES, tn))
        ai = jnp.broadcast_to(a_ref[1:2, :], (SUBLANES, tn))

        def step(ii, carry):
            i = (ti - 1 - ii) if reverse else ii
            rows = pl.ds(pl.multiple_of(i * SUBLANES, SUBLANES), SUBLANES)
            xr, xi = carry[0], carry[1]
            if backward:
                zr, zi = xr_ref[rows, :], xi_ref[rows, :]
                acc = (carry[2] + xr * zr + xi * zi, carry[3] + xi * zr - xr * zi)
            nr, ni = _scan_step(ar, ai, xr, xi, dr_scr[rows, :], di_scr[rows, :])
            if forward:
                xr_out[rows, :] = nr
                xi_out[rows, :] = ni
            if backward:
                dr_scr[rows, :] = nr
                di_scr[rows, :] = ni
            return (nr, ni) + (acc if backward else ())

        init = (sr[...], si[...]) + ((accr[...], acci[...]) if backward else ())
        out = lax.fori_loop(0, ti, step, init, unroll=4)
        sr[...] = out[0]
        si[...] = out[1]
        if first:
            @pl.when(j == nt - 1)
            def _():
                fr_out[...] = out[0]
                fi_out[...] = out[1]
        if forward:
            y = (lax.dot_general(bf(xr_out[...]), bf(cdr_ref[0]), _NN_DIMS, preferred_element_type=F32)
                 + lax.dot_general(bf(xi_out[...]), bf(cdi_ref[0]), _NN_DIMS, preferred_element_type=F32))
            _s5_deinterleave(y, in_scr, y_ref, ti)
        if backward:
            accr[...] = out[2]
            acci[...] = out[3]
            g_r, g_i = bf(dr_scr[...]), bf(di_scr[...])
            dcr_ref[0] += lax.dot_general(bf(xr_ref[...]), src_b, _TN_DIMS, preferred_element_type=F32)
            dci_ref[0] += lax.dot_general(bf(xi_ref[...]), src_b, _TN_DIMS, preferred_element_type=F32)
            _s5_interleave(u_ref, u_scr, ti)
            u_b = bf(u_scr[...])
            dbr_ref[0] += lax.dot_general(u_b, g_r, _TN_DIMS, preferred_element_type=F32)
            dbi_ref[0] += lax.dot_general(u_b, g_i, _TN_DIMS, preferred_element_type=F32)
            du = (lax.dot_general(g_r, bf(bdr_ref[0]), _NT_DIMS, preferred_element_type=F32)
                  + lax.dot_general(g_i, bf(bdi_ref[0]), _NT_DIMS, preferred_element_type=F32))
            _s5_deinterleave(du, u_scr, du_ref, ti)

            @pl.when(j == nt - 1)
            def _():
                dar_ref[...] = jnp.sum(out[2], axis=0, keepdims=True)
                dai_ref[...] = jnp.sum(out[3], axis=0, keepdims=True)

    view3 = lambda t: t.reshape(SCAN_SEGMENTS, seg_len, t.shape[1])
    spec3 = pl.BlockSpec((SCAN_SEGMENTS, ti, LANES), tmap3)
    wspec = lambda w: pl.BlockSpec((1,) + w.shape[1:], lambda c, j: (c, 0, 0))
    aspec = pl.BlockSpec((2, tn), lambda c, j: (0, c))
    fspec = pl.BlockSpec((SUBLANES, tn), lambda c, j: (0, c))
    xspec = pl.BlockSpec((tr, tn), tmap2)
    dspec = pl.BlockSpec((1, tn), lambda c, j: (0, c))
    f32 = lambda *shape: jax.ShapeDtypeStruct(shape, F32)
    args, in_specs = [view3(src), w_r, w_i, a], [spec3, wspec(w_r), wspec(w_i), aspec]
    if not first:
        args += list(finals)
        in_specs += [fspec, fspec]
    if forward:
        args += list(extra)
        in_specs += [wspec(extra[0]), wspec(extra[1])]
        out_specs, out_shape = [xspec, xspec, spec3], [f32(s, n), f32(s, n), f32(SCAN_SEGMENTS, seg_len, nb * LANES)]
    elif backward:
        x_r, x_i, u, bd_r, bd_i = extra
        args += [x_r, x_i, view3(u), bd_r, bd_i]
        in_specs += [xspec, xspec, spec3, wspec(bd_r), wspec(bd_i)]
        out_specs = [spec3, dspec, dspec, wspec(bd_r), wspec(bd_i), wspec(w_r), wspec(w_i)]
        out_shape = [f32(SCAN_SEGMENTS, seg_len, nb * LANES), f32(1, n), f32(1, n), f32(*bd_r.shape), f32(*bd_i.shape),
                     f32(*w_r.shape), f32(*w_i.shape)]
    else:
        out_specs, out_shape = [fspec, fspec], [f32(SUBLANES, n), f32(SUBLANES, n)]
    scratch = ([pltpu.VMEM((SUBLANES, tn), F32)] * 2 + [pltpu.VMEM((tr, LANES), F32)] + [pltpu.VMEM((tr, tn), F32)] * 2
               + ([pltpu.VMEM((SUBLANES, tn), F32)] * 2 + [pltpu.VMEM((tr, LANES), F32)] if backward else []))
    return pl.pallas_call(
        body, name=name, grid=(nb, nt), in_specs=in_specs, out_specs=out_specs, out_shape=out_shape,
        scratch_shapes=scratch, compiler_params=_cparams("parallel", "arbitrary"),
    )(*args)


@jax.custom_vjp
def s5_core(u, a, bd_r, bd_i, cd_r, cd_i):
    return _s5_core_fwd(u, a, bd_r, bd_i, cd_r, cd_i)[0]


def _s5_core_fwd(u, a, bd_r, bd_i, cd_r, cd_i):
    fin = _s5_pass_call(u, bd_r, bd_i, a, False, False, None, None, "s5_fwd_finals")
    x_r, x_i, y = _s5_pass_call(u, bd_r, bd_i, a, False, False, fin, (cd_r, cd_i), "s5_fwd_scan")
    return y.reshape(u.shape), (u, a, bd_r, bd_i, cd_r, cd_i, x_r, x_i)


def _s5_core_bwd(res, dy):
    u, a, bd_r, bd_i, cd_r, cd_i, x_r, x_i = res
    a_conj = a * jnp.array([[1.0], [-1.0]], F32)
    fin = _s5_pass_call(dy, cd_r, cd_i, a_conj, True, True, None, None, "s5_bwd_finals")
    du, da_r, da_i, dbd_r, dbd_i, dcd_r, dcd_i = _s5_pass_call(
        dy, cd_r, cd_i, a_conj, True, True, fin, (x_r, x_i, u, bd_r, bd_i), "s5_bwd_scan")
    return du.reshape(u.shape), jnp.concatenate([da_r, da_i], axis=0), dbd_r, dbd_i, dcd_r, dcd_i


s5_core.defvjp(_s5_core_fwd, _s5_core_bwd)


_NN = (((1,), (0,)), ((), ()))
_NT = (((1,), (1,)), ((), ()))
_TN = (((0,), (0,)), ((), ()))


def _dot(a, b, dn):
    return lax.dot_general(a.astype(BF16), b.astype(BF16), dn, preferred_element_type=F32)


@jax.custom_vjp
def bdot_nn(a, b):
    return _dot(a, b, _NN)


bdot_nn.defvjp(lambda a, b: (_dot(a, b, _NN), (a, b)),
               lambda r, g: (_dot(g, r[1], _NT).astype(r[0].dtype), _dot(r[0], g, _TN).astype(r[1].dtype)))


@jax.custom_vjp
def bdot_nt(a, b):
    return _dot(a, b, _NT)


bdot_nt.defvjp(lambda a, b: (_dot(a, b, _NT), (a, b)),
               lambda r, g: (_dot(g, r[1], _NN).astype(r[0].dtype), _dot(g, r[0], _TN).astype(r[1].dtype)))


@jax.custom_vjp
def bdot_tn(a, b):
    return _dot(a, b, _TN)


bdot_tn.defvjp(lambda a, b: (_dot(a, b, _TN), (a, b)),
               lambda r, g: (_dot(r[1], g, _NT).astype(r[0].dtype), _dot(r[0], g, _NN).astype(r[1].dtype)))


def _split_hi_lo(x):
    h = x.astype(BF16)
    return h, (x - h.astype(F32)).astype(BF16)


def _exact_dot(t, x, dn):
    h, l = _split_hi_lo(x)
    d = lambda p: lax.dot_general(t, p, dn, preferred_element_type=F32)
    return d(h) + d(l)


@jax.custom_vjp
def select_dot(t, x):
    return _exact_dot(t, x, _NN)


select_dot.defvjp(lambda t, x: (_exact_dot(t, x, _NN), t),
                  lambda t, g: (jnp.zeros_like(t), _exact_dot(t, g, _TN)))


def _split_rows_impl(x, h):
    return tuple(x[i * h:(i + 1) * h] for i in range(x.shape[0] // h))


@functools.partial(jax.custom_vjp, nondiff_argnums=(1,))
def split_rows(x, h):
    return _split_rows_impl(x, h)


split_rows.defvjp(lambda x, h: (_split_rows_impl(x, h), None),
                  lambda h, r, g: (jnp.concatenate(g, axis=0),))


@jax.custom_vjp
def join_rows(parts):
    return jnp.concatenate(parts, axis=0)


def _join_rows_bwd(hs, g):
    out, off = [], 0
    for h in hs:
        out.append(g[off:off + h])
        off += h
    return (tuple(out),)


join_rows.defvjp(lambda parts: (jnp.concatenate(parts, axis=0), tuple(p.shape[0] for p in parts)), _join_rows_bwd)


def _split_lanes_impl(x, w):
    return tuple(x[:, i * w:(i + 1) * w] for i in range(x.shape[1] // w))


@functools.partial(jax.custom_vjp, nondiff_argnums=(1,))
def split_lanes(x, w):
    return _split_lanes_impl(x, w)


split_lanes.defvjp(lambda x, w: (_split_lanes_impl(x, w), None),
                   lambda w, r, g: (jnp.concatenate(g, axis=1),))


def _join_impl(parts):
    return jnp.concatenate(parts, axis=1)


@jax.custom_vjp
def join_lanes(parts):
    return _join_impl(parts)


def _join_bwd(ws, g):
    out, off = [], 0
    for w in ws:
        out.append(g[:, off:off + w])
        off += w
    return (tuple(out),)


join_lanes.defvjp(lambda parts: (_join_impl(parts), tuple(p.shape[1] for p in parts)), _join_bwd)


def _rope_impl(x, c, sa, sb, shift):
    w = x.shape[1]
    return x * c + pltpu.roll(x, w - shift, 1) * sa + pltpu.roll(x, shift, 1) * sb


@functools.partial(jax.custom_vjp, nondiff_argnums=(4,))
def rope_lanes(x, c, sa, sb, shift):
    return _rope_impl(x, c, sa, sb, shift)


def _rope_bwd(shift, r, g):
    c, sa, sb = r
    w = g.shape[1]
    dx = g * c + pltpu.roll(g * sa, shift, 1) + pltpu.roll(g * sb, w - shift, 1)
    return dx, jnp.zeros_like(c), jnp.zeros_like(sa), jnp.zeros_like(sb)


rope_lanes.defvjp(lambda x, c, sa, sb, shift: (_rope_impl(x, c, sa, sb, shift), (c, sa, sb)), _rope_bwd)


RMS_EPS = 1e-6


def _rms(x, g):
    return x * lax.rsqrt(jnp.mean(x * x, axis=-1, keepdims=True) + RMS_EPS) * g


ATTN_BLOCK = 512
MASK_VALUE = -1e30
LOG2E = math.log2(math.e)
LN2 = math.log(2.0)
V_ONES_LANE = 64


def _causal_mask(t):
    r = lax.broadcasted_iota(jnp.int32, (t, t), 0)
    c = lax.broadcasted_iota(jnp.int32, (t, t), 1)
    return c <= r


def _attn_fwd_call(q, k, v):
    s, width = q.shape
    n_heads = width // LANES
    tq = min(ATTN_BLOCK, s)
    nq = s // tq

    def body(q_ref, k_ref, v_ref, o_ref, lse_ref):
        i = pl.program_id(1)
        qb = q_ref[...].astype(BF16)
        ones_lane = lax.broadcasted_iota(jnp.int32, (tq, LANES), 1) == V_ONES_LANE

        def block(kb, carry, masked):
            m, acc = carry
            rows = pl.ds(pl.multiple_of(kb * tq, tq), tq)
            sc = lax.dot_general(qb, k_ref[rows, :].astype(BF16), _NT, preferred_element_type=F32)
            if masked:
                sc = jnp.where(_causal_mask(tq), sc, MASK_VALUE)
            m_new = jnp.maximum(m, jnp.max(sc, axis=-1, keepdims=True))
            p = jnp.exp2(sc - m_new).astype(BF16)
            vb = jnp.where(ones_lane, 1.0, v_ref[rows, :]).astype(BF16)
            acc = jnp.exp2(m - m_new) * acc + lax.dot_general(p, vb, _NN, preferred_element_type=F32)
            return m_new, acc

        init = (jnp.full((tq, 1), MASK_VALUE, F32), jnp.zeros((tq, LANES), F32))
        carry = lax.fori_loop(0, i, lambda kb, c: block(kb, c, False), init)
        m, acc = block(i, carry, True)
        l = jnp.sum(jnp.where(ones_lane, acc, 0.0), axis=-1, keepdims=True)
        o_ref[...] = jnp.where(ones_lane, 0.0, acc / l).astype(o_ref.dtype)
        lse_ref[...] = jnp.broadcast_to(m + jnp.log2(l), (tq, LANES))

    qspec = pl.BlockSpec((tq, LANES), lambda h, i: (i, h))
    kspec = pl.BlockSpec((s, LANES), lambda h, i: (0, h))
    return pl.pallas_call(
        body,
        name="mla_attn_fwd",
        grid=(n_heads, nq),
        in_specs=[qspec, kspec, kspec],
        out_specs=[qspec, qspec],
        out_shape=[jax.ShapeDtypeStruct((s, width), BF16), jax.ShapeDtypeStruct((s, width), F32)],
        compiler_params=_cparams("parallel", "parallel"),
    )(q, k, v)


def _attn_bwd_call(q, k, v, o, lse, do):
    s, width = q.shape
    n_heads = width // LANES
    tq = min(ATTN_BLOCK, s)
    nq = s // tq

    def body(q_ref, k_ref, v_ref, o_ref, lse_ref, do_ref, dq_ref, dk_ref, dv_ref, dq_acc):
        j = pl.program_id(1)

        @pl.when(j == 0)
        def _():
            dq_acc[...] = jnp.zeros_like(dq_acc)

        kb = k_ref[...].astype(BF16)
        vb = v_ref[...].astype(BF16)

        def block(i, carry, masked):
            dk, dv = carry
            rows = pl.ds(pl.multiple_of(i * tq, tq), tq)
            qi = q_ref[rows, :].astype(BF16)
            doi = do_ref[rows, :].astype(F32)
            delta = jnp.sum(doi * o_ref[rows, :].astype(F32), axis=-1, keepdims=True)
            sc = lax.dot_general(qi, kb, _NT, preferred_element_type=F32)
            if masked:
                sc = jnp.where(_causal_mask(tq), sc, MASK_VALUE)
            p = jnp.exp2(sc - lse_ref[rows, 0:1])
            dob = doi.astype(BF16)
            dv = dv + lax.dot_general(p.astype(BF16), dob, _TN, preferred_element_type=F32)
            dp = lax.dot_general(dob, vb, _NT, preferred_element_type=F32)
            ds = (p * (dp - delta)).astype(BF16)
            dq_acc[rows, :] += lax.dot_general(ds, kb, _NN, preferred_element_type=F32)
            dk = dk + lax.dot_general(ds, qi, _TN, preferred_element_type=F32)
            return dk, dv

        zero = jnp.zeros((tq, LANES), F32)
        carry = block(j, (zero, zero), True)
        dk, dv = lax.fori_loop(j + 1, nq, lambda i, c: block(i, c, False), carry)
        dk_ref[...] = (dk * LN2).astype(dk_ref.dtype)
        dv_ref[...] = dv.astype(dv_ref.dtype)

        @pl.when(j == nq - 1)
        def _():
            dq_ref[...] = (dq_acc[...] * LN2).astype(dq_ref.dtype)

    full = pl.BlockSpec((s, LANES), lambda h, j: (0, h))
    blk = pl.BlockSpec((tq, LANES), lambda h, j: (j, h))
    return pl.pallas_call(
        body,
        name="mla_attn_bwd",
        grid=(n_heads, nq),
        in_specs=[full, blk, blk, full, full, full],
        out_specs=[full, blk, blk],
        out_shape=[jax.ShapeDtypeStruct((s, width), t.dtype) for t in (q, k, v)],
        scratch_shapes=[pltpu.VMEM((s, LANES), F32)],
        compiler_params=_cparams("parallel", "arbitrary"),
    )(q, k, v, o, lse, do)


@jax.custom_vjp
def causal_attention(q, k, v):
    return _attn_fwd_call(q, k, v)[0]


def _causal_attention_fwd(q, k, v):
    o, lse = _attn_fwd_call(q, k, v)
    return o, (q, k, v, o, lse)


def _causal_attention_bwd(res, do):
    return tuple(_attn_bwd_call(*res, do))


causal_attention.defvjp(_causal_attention_fwd, _causal_attention_bwd)


HG_HEADS = 4
HG_CHUNK = 32
HG_REF_ROW = HG_CHUNK // 2 - 1
HG_TILE_ROWS = 256
HG_EXP_CLAMP = 80.0


def _hg_tile_masks(t):
    shift = HG_CHUNK.bit_length() - 1
    r = lax.broadcasted_iota(jnp.int32, (t, t), 0)
    c = lax.broadcasted_iota(jnp.int32, (t, t), 1)
    start = lax.shift_left(lax.shift_right_logical(r, shift), shift)
    causal = (c >= start) & (c <= r)
    return causal, c == start + HG_REF_ROW, c == start + (HG_CHUNK - 1)


def _hg_tile(q, fl, v, lb, st):
    t = q.shape[0]
    causal, ref_sel, last_sel = _hg_tile_masks(t)
    f = lb + (1.0 - lb) * jax.nn.sigmoid(fl)
    kk = 1.0 - f
    qs = q * jax.nn.sigmoid(q)
    b = select_dot(causal.astype(BF16), jnp.log(f))
    b_ref = select_dot(ref_sel.astype(BF16), b)
    b_last = select_dot(last_sel.astype(BF16), b)
    q_in = qs * jnp.exp(jnp.minimum(b - b_ref, HG_EXP_CLAMP))
    k_in = kk * jnp.exp(jnp.minimum(b_ref - b, HG_EXP_CLAMP))
    o = bdot_nn(jnp.where(causal, bdot_nt(q_in, k_in), 0.0), v)
    q_hat = split_rows(qs * jnp.exp(b), HG_CHUNK)
    k_hat = split_rows(kk * jnp.exp(b_last - b), HG_CHUNK)
    decay = split_rows(jnp.exp(b_last), HG_CHUNK)
    vs = split_rows(v, HG_CHUNK)
    first_row = lax.broadcasted_iota(jnp.int32, (HG_CHUNK, LANES), 0) == 0
    inter = []
    for c in range(t // HG_CHUNK):
        inter.append(bdot_nt(q_hat[c], st))
        st = st * jnp.sum(jnp.where(first_row, decay[c], 0.0), axis=0, keepdims=True) + bdot_tn(vs[c], k_hat[c])
    return o + join_rows(tuple(inter)), st


def _hg_head(q, fl, v, gate, lb, gn, st):
    o, st = _hg_tile(q, fl, v, lb, st)
    return _rms(o, gn) * (gate * jax.nn.sigmoid(gate)), st


HG_PARTS = 4


def _hg_part_slices(h, width):
    return [slice(p * width + h * LANES, p * width + (h + 1) * LANES) for p in range(HG_PARTS)]


def _hg_fwd_call(x, lb, gn):
    s = x.shape[0]
    width = x.shape[1] // HG_PARTS
    tr = min(HG_TILE_ROWS, s)
    nt = s // tr

    def body(x_ref, lb_ref, gn_ref, o_ref, sts_ref, st_ref):
        @pl.when(pl.program_id(0) == 0)
        def _():
            st_ref[...] = jnp.zeros_like(st_ref)

        for h in range(HG_HEADS):
            ln = slice(h * LANES, (h + 1) * LANES)
            st = st_ref[h]
            sts_ref[0, h] = st
            o, st_new = _hg_head(*(x_ref[:, sl] for sl in _hg_part_slices(h, width)), lb_ref[:, ln], gn_ref[...], st)
            o_ref[:, ln] = o.astype(o_ref.dtype)
            st_ref[h] = st_new

    const = lambda shape: pl.BlockSpec(shape, lambda j: (0, 0))
    return pl.pallas_call(
        body,
        name="hgrn2_fwd",
        grid=(nt,),
        in_specs=[pl.BlockSpec((tr, HG_PARTS * width), lambda j: (j, 0)), const((1, width)), const((1, LANES))],
        out_specs=[pl.BlockSpec((tr, width), lambda j: (j, 0)),
                   pl.BlockSpec((1, HG_HEADS, LANES, LANES), lambda j: (j, 0, 0, 0))],
        out_shape=[jax.ShapeDtypeStruct((s, width), BF16),
                   jax.ShapeDtypeStruct((nt, HG_HEADS, LANES, LANES), F32)],
        scratch_shapes=[pltpu.VMEM((HG_HEADS, LANES, LANES), F32)],
        compiler_params=_cparams("arbitrary"),
    )(x, lb, gn)


def _hg_bwd_call(x, lb, gn, sts, do):
    s = x.shape[0]
    width = x.shape[1] // HG_PARTS
    tr = min(HG_TILE_ROWS, s)
    nt = s // tr

    def body(x_ref, lb_ref, gn_ref, sts_ref, do_ref, dx_ref, dlb_ref, dgn_ref, dst_ref):
        @pl.when(pl.program_id(0) == 0)
        def _():
            dst_ref[...] = jnp.zeros_like(dst_ref)
            dlb_ref[...] = jnp.zeros_like(dlb_ref)
            dgn_ref[...] = jnp.zeros_like(dgn_ref)

        for h in range(HG_HEADS):
            ln = slice(h * LANES, (h + 1) * LANES)
            parts = _hg_part_slices(h, width)
            _, vjp = jax.vjp(_hg_head, *(x_ref[:, sl] for sl in parts), lb_ref[:, ln], gn_ref[...], sts_ref[0, h])
            cts = vjp((do_ref[:, ln].astype(F32), dst_ref[h]))
            for sl, ct in zip(parts, cts[:HG_PARTS]):
                dx_ref[:, sl] = ct.astype(dx_ref.dtype)
            dlb_ref[:, ln] += cts[HG_PARTS]
            dgn_ref[...] += cts[HG_PARTS + 1]
            dst_ref[h] = cts[HG_PARTS + 2]

    rev = lambda w: pl.BlockSpec((tr, w), lambda j: (nt - 1 - j, 0))
    const = lambda shape: pl.BlockSpec(shape, lambda j: (0, 0))
    return pl.pallas_call(
        body,
        name="hgrn2_bwd",
        grid=(nt,),
        in_specs=[rev(HG_PARTS * width), const((1, width)), const((1, LANES)),
                  pl.BlockSpec((1, HG_HEADS, LANES, LANES), lambda j: (nt - 1 - j, 0, 0, 0)), rev(width)],
        out_specs=[rev(HG_PARTS * width), const((1, width)), const((1, LANES))],
        out_shape=[jax.ShapeDtypeStruct(x.shape, BF16), jax.ShapeDtypeStruct((1, width), F32),
                   jax.ShapeDtypeStruct((1, LANES), F32)],
        scratch_shapes=[pltpu.VMEM((HG_HEADS, LANES, LANES), F32)],
        compiler_params=_cparams("arbitrary"),
    )(x, lb, gn, sts, do)


@jax.custom_vjp
def hgrn2_mixer(h, w, lb, gn):
    return _hg_fwd_call(_mm_call(h, w, False, False, name="hgrn2_proj"), lb, gn)[0]


def _hgrn2_mixer_fwd(h, w, lb, gn):
    x = _mm_call(h, w, False, False, name="hgrn2_proj")
    o, sts = _hg_fwd_call(x, lb, gn)
    return o, (h, w, x, lb, gn, sts)


def _hgrn2_mixer_bwd(res, do):
    h, w, x, lb, gn, sts = res
    dx, dlb, dgn = _hg_bwd_call(x, lb, gn, sts, do)
    dh = _mm_call(dx, w, False, True, out_dtype=h.dtype, name="hgrn2_proj_da")
    dw = _mm_call(h, dx, True, False, out_dtype=w.dtype, name="hgrn2_proj_db")
    return dh, dw, dlb, dgn


hgrn2_mixer.defvjp(_hgrn2_mixer_fwd, _hgrn2_mixer_bwd)


D_MODEL = 1024
DEPTH = 2
SSM_GROUPS, SSM_GROUP_CH, SSM_STATE = 32, 16, 64
SSM_WIDTH = SSM_GROUPS * SSM_GROUP_CH
MLA_HEADS, MLA_NOPE, MLA_ROPE, MLA_V = 8, 64, 32, 64
MLA_Q_RANK, MLA_KV_RANK = 512, 256
HG_WIDTH = HG_HEADS * LANES
X_HEADS, X_HEAD_DIM = 4, 128
X_WIDTH = X_HEADS * X_HEAD_DIM
D_FF = 2816
ROPE_THETA = 10000.0
IN_SPLITS = (SSM_WIDTH, MLA_Q_RANK, MLA_KV_RANK, MLA_ROPE, HG_WIDTH, HG_WIDTH, HG_WIDTH, HG_WIDTH, 3 * D_MODEL)
ROPE_LANE0 = MLA_NOPE
MLA_Q_SCALE = LOG2E / math.sqrt(MLA_NOPE + MLA_ROPE)
ROW_TILE = 512
MEM_ROW_TILE = 256


def _t_rms(x, g):
    return (_rms(x, g).astype(BF16),)


def _t_s5_act(y, u, d):
    return (jax.nn.gelu(y + d * u).astype(BF16),)


def _t_glu(z):
    zo, zg = split_lanes(z.astype(F32), D_MODEL)
    return ((zo * jax.nn.sigmoid(zg)).astype(BF16),)


def _t_mla_rope(q, k, kr, c, sa, sb):
    rep = lambda t: jnp.concatenate([t] * MLA_HEADS, axis=1)
    half = MLA_ROPE // 2
    q_out = rope_lanes(q, rep(c), rep(sa), rep(sb), half) * MLA_Q_SCALE
    kr_out = rope_lanes(kr, c, sa, sb, half)
    return q_out.astype(BF16), (k + join_lanes((kr_out,) * MLA_HEADS)).astype(BF16)


def _t_merge(y_ssm, y_mla, y_hg, gates):
    g0, g1, g2 = split_lanes(gates.astype(F32), D_MODEL)
    mix = (jax.nn.sigmoid(g0) * y_ssm.astype(F32) + jax.nn.sigmoid(g1) * y_mla.astype(F32)
           + jax.nn.sigmoid(g2) * y_hg.astype(F32))
    return (mix.astype(BF16),)


def _t_xattn(q, kv):
    scale = 1.0 / math.sqrt(X_HEAD_DIM)
    heads = split_lanes(kv, X_HEAD_DIM)
    outs = []
    for qh, kh, vh in zip(split_lanes(q, X_HEAD_DIM), heads[:X_HEADS], heads[X_HEADS:]):
        sc = bdot_nt(qh, kh) * scale
        p = jnp.exp(sc - jnp.max(sc, axis=-1, keepdims=True))
        p = p / jnp.sum(p, axis=-1, keepdims=True)
        outs.append(bdot_nn(p, vh))
    return (join_lanes(tuple(outs)).astype(BF16),)


def _t_swiglu(gate_up):
    gt, up = split_lanes(gate_up.astype(F32), D_FF)
    return ((gt * jax.nn.sigmoid(gt) * up).astype(BF16),)


def _t_loss(x, tgt, g):
    e = _rms(x, g) - tgt
    return (jnp.broadcast_to(jnp.mean(e * e, axis=-1, keepdims=True), (x.shape[0], LANES)),)


rms_op = rowwise(_t_rms, 1, 0, ROW_TILE, "rmsnorm")
rms_mem_op = rowwise(_t_rms, 1, 0, MEM_ROW_TILE, "rmsnorm_mem")
rms_res_op = rowwise(_t_rms, 1, 0, ROW_TILE, "rmsnorm_res", passthrough=True)
s5_act_op = rowwise(_t_s5_act, 2, 0, ROW_TILE, "s5_act")
glu_op = rowwise(_t_glu, 1, 0, ROW_TILE, "glu")
mla_rope_op = rowwise(_t_mla_rope, 3, 3, ROW_TILE, "mla_rope")
merge_op = rowwise(_t_merge, 4, 0, ROW_TILE, "merge")
xattn_op = rowwise(_t_xattn, 1, 0, ROW_TILE, "xattn")
swiglu_op = rowwise(_t_swiglu, 1, 0, ROW_TILE, "swiglu")
loss_op = rowwise(_t_loss, 1, 1, ROW_TILE, "loss")


def _rope_tables(positions):
    half = MLA_ROPE // 2
    inv_freq = ROPE_THETA ** (-jnp.arange(half, dtype=F32) / half)
    ang = positions.astype(F32)[:, None] * inv_freq
    cos, sin = jnp.cos(ang), jnp.sin(ang)
    s = positions.shape[0]
    z = lambda w: jnp.zeros((s, w), F32)
    tail = LANES - ROPE_LANE0 - MLA_ROPE
    c = jnp.concatenate([jnp.ones((s, ROPE_LANE0), F32), cos, cos, z(tail)], axis=1)
    sa = jnp.concatenate([z(ROPE_LANE0), -sin, z(half), z(tail)], axis=1)
    sb = jnp.concatenate([z(ROPE_LANE0), z(half), sin, z(tail)], axis=1)
    return c, sa, sb


def _s5_operators(lam_re, lam_im, b_re, b_im, c_re, c_im, log_step):
    g, p, h = SSM_GROUPS, SSM_STATE, SSM_GROUP_CH
    lam = lax.complex(lam_re, lam_im)
    lam_bar = jnp.exp(lam * jnp.exp(log_step)[:, None])
    b_bar = ((lam_bar - 1.0) / lam)[..., None] * lax.complex(b_re, b_im)
    per = LANES // h
    nb = g // per
    eye = jnp.eye(per, dtype=F32)
    bd = lambda t: jnp.einsum("jgph,gk->jghkp", t.reshape(nb, per, p, h), eye).reshape(nb, per * h, per * p)
    cd = lambda t: jnp.einsum("jghp,gk->jgpkh", t.reshape(nb, per, h, p), eye).reshape(nb, per * p, per * h)
    a = jnp.stack([jnp.real(lam_bar).reshape(-1), jnp.imag(lam_bar).reshape(-1)])
    return a, bd(jnp.real(b_bar)), bd(jnp.imag(b_bar)), cd(c_re), cd(-c_im)


LATENT_WIDTH = 1536
_LATENT = {}
_off = 0
for _name, _w in (("u", SSM_WIDTH), ("q_lat", MLA_Q_RANK), ("kv_lat", MLA_KV_RANK), ("k_rope", LANES)):
    _LATENT[_name] = (_off, _off + _w)
    _off += _w


def _layer_matrices(w, l):
    return {**_mixer_matrices(w, l), **_tail_matrices(w, l)}


def _tail_matrices(w, l):
    return dict(x_q=w["x_w_q"][l], x_kv=w["x_w_kv"][l], x_o=w["x_w_o"][l], ffn_gu=w["ffn_w_gate_up"][l],
                ffn_d=w["ffn_w_down"][l])


def _mixer_matrices(w, l):
    w_in = w["w_in"][l]
    d, dt = w_in.shape[0], w_in.dtype
    z = lambda n: jnp.zeros((d, n), dt)
    r0 = SSM_WIDTH + MLA_Q_RANK + MLA_KV_RANK
    r1 = r0 + MLA_ROPE
    r2 = r1 + HG_PARTS * HG_WIDTH
    w_latent = jnp.concatenate([w_in[:, :r0], z(ROPE_LANE0), w_in[:, r0:r1],
                                z(LATENT_WIDTH - r0 - ROPE_LANE0 - MLA_ROPE)], axis=1)
    pad_heads = lambda t: jnp.pad(t, ((0, 0), (0, 0), (0, LANES - t.shape[2]))).reshape(t.shape[0], -1)
    uq = w["mla_w_uq"][l].reshape(MLA_Q_RANK, MLA_HEADS, MLA_NOPE + MLA_ROPE)
    ukv = w["mla_w_ukv"][l].reshape(MLA_KV_RANK, MLA_HEADS, MLA_NOPE + MLA_V)
    wo = w["mla_w_o"][l].reshape(MLA_HEADS, MLA_V, D_MODEL)
    return dict(
        w_latent=w_latent, w_hg=w_in[:, r1:r2], w_gates=w_in[:, r2:], glu=w["ssm_w_glu"][l],
        uq=pad_heads(uq), uk=pad_heads(ukv[:, :, :MLA_NOPE]), uv=pad_heads(ukv[:, :, MLA_NOPE:]),
        mla_o=jnp.pad(wo, ((0, 0), (0, LANES - MLA_V), (0, 0))).reshape(MLA_HEADS * LANES, D_MODEL),
        hg_o=w["hg_w_o"][l], w_out=w["w_out"][l])


def _layer(x, mem, tabs, m, sp, l, lower_bound):
    return _tail(_mixer(x, tabs, m, sp, l, lower_bound), mem, m, sp, l)


def _mixer(x, tabs, m, sp, l, lower_bound):
    row = lambda name: sp[name][l].reshape(1, -1)
    h, x = rms_res_op(x, row("norm_mix"))
    latent = matmul(h, m["w_latent"])
    seg = lambda name: latent[:, _LATENT[name][0]:_LATENT[name][1]]
    a, bd_r, bd_i, cd_r, cd_i = _s5_operators(*(sp[n][l] for n in (
        "ssm_lam_re", "ssm_lam_im", "ssm_b_re", "ssm_b_im", "ssm_c_re", "ssm_c_im", "ssm_log_step")))
    u = seg("u")
    y = s5_core(u, a, bd_r, bd_i, cd_r, cd_i)
    (ya,) = s5_act_op(y, u, row("ssm_d"))
    (y_ssm,) = glu_op(matmul(ya, m["glu"], BF16))
    (qn,) = rms_op(seg("q_lat"), row("mla_q_norm"))
    (kvn,) = rms_op(seg("kv_lat"), row("mla_kv_norm"))
    q, k = mla_rope_op(matmul(qn, m["uq"]), matmul(kvn, m["uk"]), seg("k_rope"), *tabs)
    o = causal_attention(q, k, matmul(kvn, m["uv"], BF16))
    y_mla = matmul(o, m["mla_o"], BF16)
    y_hg = matmul(hgrn2_mixer(h, m["w_hg"], lower_bound, row("hg_g_norm")), m["hg_o"], BF16)
    (merged,) = merge_op(y_ssm, y_mla, y_hg, matmul(h, m["w_gates"], BF16))
    return matmul_add(merged, m["w_out"], x)


def _tail(x, mem, m, sp, l):
    row = lambda name: sp[name][l].reshape(1, -1)
    hc, x = rms_res_op(x, row("norm_cross"))
    (mn,) = rms_mem_op(mem, row("norm_mem"))
    (ox,) = xattn_op(matmul(hc, m["x_q"], BF16), matmul(mn, m["x_kv"]))
    x = matmul_add(ox, m["x_o"], x)
    hf, x = rms_res_op(x, row("norm_ffn"))
    (act,) = swiglu_op(matmul(hf, m["ffn_gu"], BF16))
    return matmul_add(act, m["ffn_d"], x)


def _lower_bounds(hg_lb):
    lb_p = jax.nn.softmax(hg_lb, axis=0)
    return jnp.cumsum(lb_p, axis=0) - lb_p[0:1]


def _final_loss(target, x, norm_final):
    (row_loss,) = loss_op(x, target, norm_final.reshape(1, -1))
    return 0.5 * jnp.sum(row_loss[:, 0])


def _local_loss(x, mem, positions, target, w, sp):
    tabs = _rope_tables(positions)
    lower = _lower_bounds(sp["hg_lb"])
    for l in range(DEPTH):
        x = _layer(x, mem, tabs, _layer_matrices(w, l), sp, l, lower[l].reshape(1, -1))
    return _final_loss(target, x, sp["norm_final"])


N_DEV = 8
N_CHIPS = 4
COMM_LANES = 512
MESH_ID = pl.DeviceIdType.MESH
_ANY = pl.BlockSpec(memory_space=pl.ANY)
_OTHER_CHIPS = ((1, 0), (0, 1), (1, 1))


def _place():
    return lax.axis_index("x"), lax.axis_index("y"), lax.axis_index("c")


def _all_gather_call(blocks, name):
    n = len(blocks)

    def body(*refs):
        x_refs, out_refs = refs[:n], refs[n:2 * n]
        send_sems, recv_sems, local_sems = refs[2 * n:]
        x, y, c = _place()
        me, sibling = (x, y, c), (x, y, 1 - c)
        chips = [(x ^ fx, y ^ fy) for fx, fy in _OTHER_CHIPS]

        def slot(i, px, py, pc):
            return out_refs[i].at[4 * px + 2 * py + pc]

        def copy(i, k, blk, to, src=None):
            return pltpu.make_async_remote_copy(
                src_ref=slot(i, *blk) if src is None else src, dst_ref=slot(i, *blk),
                send_sem=send_sems.at[i, k], recv_sem=recv_sems.at[i, k], device_id=to, device_id_type=MESH_ID)

        mine = [pltpu.make_async_copy(x_refs[i], slot(i, *me), local_sems.at[i]) for i in range(n)]
        first = []
        for i in range(n):
            first.append(copy(i, 0, me, sibling, src=x_refs[i]))
            first += [copy(i, 1 + j, me, (*chip, c), src=x_refs[i]) for j, chip in enumerate(chips)]
        for cp in mine + first:
            cp.start()
        passed = []
        for j, chip in enumerate(chips):
            for i in range(n):
                copy(i, 1 + j, (*chip, c), me).wait_recv()
                passed.append(copy(i, 4 + j, (*chip, c), sibling))
                passed[-1].start()
        for i in range(n):
            copy(i, 0, sibling, me).wait_recv()
            for j, chip in enumerate(chips):
                copy(i, 4 + j, (*chip, 1 - c), me).wait_recv()
        for cp in first + passed:
            cp.wait_send()
        for cp in mine:
            cp.wait()

    return pl.pallas_call(
        body,
        name=name,
        out_shape=[jax.ShapeDtypeStruct((N_DEV,) + b.shape, b.dtype) for b in blocks],
        in_specs=[_ANY] * n,
        out_specs=[_ANY] * n,
        scratch_shapes=[pltpu.SemaphoreType.DMA((n, 7)), pltpu.SemaphoreType.DMA((n, 7)), pltpu.SemaphoreType.DMA((n,))],
    )(*blocks)


def _pair_exchange_call(gs, name):
    n = len(gs)

    def body(*refs):
        g_refs, got_refs = refs[:n], refs[n:2 * n]
        send_sems, recv_sems = refs[2 * n:]
        x, y, c = _place()
        sends = [pltpu.make_async_remote_copy(
            src_ref=g_refs[i].at[2 * p + (1 - c)], dst_ref=got_refs[i].at[p],
            send_sem=send_sems.at[i, p], recv_sem=recv_sems.at[i, p], device_id=(x, y, 1 - c), device_id_type=MESH_ID)
            for i in range(n) for p in range(N_CHIPS)]
        for cp in sends:
            cp.start()
        for cp in sends:
            cp.wait_recv()
        for cp in sends:
            cp.wait_send()

    return pl.pallas_call(
        body,
        name=name,
        out_shape=[jax.ShapeDtypeStruct((N_CHIPS,) + g.shape[1:], g.dtype) for g in gs],
        in_specs=[_ANY] * n,
        out_specs=[_ANY] * n,
        scratch_shapes=[pltpu.SemaphoreType.DMA((n, N_CHIPS))] * 2,
    )(*gs)


def _chip_exchange_call(parts, name):
    n = len(parts)

    def body(*refs):
        p_refs, got_refs = refs[:n], refs[n:2 * n]
        send_sems, recv_sems = refs[2 * n:]
        x, y, c = _place()
        sends = []
        for i in range(n):
            for k, (fx, fy) in enumerate(_OTHER_CHIPS):
                px, py = x ^ fx, y ^ fy
                sends.append(pltpu.make_async_remote_copy(
                    src_ref=p_refs[i].at[2 * px + py], dst_ref=got_refs[i].at[k],
                    send_sem=send_sems.at[i, k], recv_sem=recv_sems.at[i, k], device_id=(px, py, c), device_id_type=MESH_ID))
        for cp in sends:
            cp.start()
        for cp in sends:
            cp.wait_recv()
        for cp in sends:
            cp.wait_send()

    return pl.pallas_call(
        body,
        name=name,
        out_shape=[jax.ShapeDtypeStruct((3,) + p.shape[1:], p.dtype) for p in parts],
        in_specs=[_ANY] * n,
        out_specs=[_ANY] * n,
        scratch_shapes=[pltpu.SemaphoreType.DMA((n, 3))] * 2,
    )(*parts)


def _rows_cols(shape):
    return math.prod(shape[:-1]), shape[-1]


def _pair_sum_call(g, got, c_idx, name):
    rows, cols = _rows_cols(got.shape[1:])
    tr = _pick_tile(rows, (512, 256, 128, 64, 32, 16))

    def body(c_ref, a_ref, b_ref, o_ref):
        o_ref[...] = (a_ref[...].astype(F32) + b_ref[...].astype(F32)).astype(o_ref.dtype)

    spec = pl.BlockSpec((1, tr, cols), lambda p, i, c_ref: (p, i, 0))
    out = pl.pallas_call(
        body,
        name=name,
        grid_spec=pltpu.PrefetchScalarGridSpec(
            num_scalar_prefetch=1, grid=(N_CHIPS, rows // tr),
            in_specs=[pl.BlockSpec((1, tr, cols), lambda p, i, c_ref: (2 * p + c_ref[0], i, 0)), spec],
            out_specs=spec),
        out_shape=jax.ShapeDtypeStruct((N_CHIPS, rows, cols), got.dtype),
        compiler_params=_cparams("parallel", "parallel"),
    )(c_idx, g.reshape(N_DEV, rows, cols), got.reshape(N_CHIPS, rows, cols))
    return out.reshape(got.shape)


def _chip_sum_call(part, got, chip_idx, name):
    rows, cols = _rows_cols(got.shape[1:])
    tr = _pick_tile(rows, (512, 256, 128, 64, 32, 16))

    def body(p_ref, a_ref, b_ref, o_ref):
        acc = a_ref[0].astype(F32)
        for k in range(3):
            acc = acc + b_ref[k].astype(F32)
        o_ref[...] = acc

    out = pl.pallas_call(
        body,
        name=name,
        grid_spec=pltpu.PrefetchScalarGridSpec(
            num_scalar_prefetch=1, grid=(rows // tr,),
            in_specs=[pl.BlockSpec((1, tr, cols), lambda i, p_ref: (p_ref[0], i, 0)),
                      pl.BlockSpec((3, tr, cols), lambda i, p_ref: (0, i, 0))],
            out_specs=pl.BlockSpec((tr, cols), lambda i, p_ref: (i, 0))),
        out_shape=jax.ShapeDtypeStruct((rows, cols), F32),
        compiler_params=_cparams("parallel"),
    )(chip_idx, part.reshape(N_CHIPS, rows, cols), got.reshape(3, rows, cols))
    return out.reshape(got.shape[1:])


def _reduce_scatter(gs, name):
    x, y, c = _place()
    c_idx = c.astype(jnp.int32).reshape(1)
    chip_idx = (2 * x + y).astype(jnp.int32).reshape(1)
    gots = _pair_exchange_call(gs, name + "_pair")
    parts = [_pair_sum_call(g, got, c_idx, name + "_pair_sum") for g, got in zip(gs, gots)]
    gots = _chip_exchange_call(parts, name + "_chip")
    return [_chip_sum_call(p, got, chip_idx, name + "_chip_sum") for p, got in zip(parts, gots)]


_HBM = pl.BlockSpec(memory_space=pltpu.HBM)
_SEM = pl.BlockSpec(memory_space=pltpu.SEMAPHORE)
_SIDE_EFFECT = pltpu.SideEffectType.DATAFLOW_SIDE_EFFECTING
N_PEERS = N_DEV - 1


def _peer(k):
    x, y, c = _place()
    px, py, pc = x ^ ((k >> 2) & 1), y ^ ((k >> 1) & 1), c ^ (k & 1)
    return (px, py, pc), 4 * px + 2 * py + pc


def _exchange_copy(src_ref, land_ref, send_sems, recv_sems, i, k, scatter, receiving):
    x, y, c = _place()
    me = 4 * x + 2 * y + c
    peer, peer_idx = _peer(k)
    sem = i * N_PEERS + k - 1
    return pltpu.make_async_remote_copy(
        src_ref=src_ref.at[peer_idx] if scatter else src_ref, dst_ref=land_ref.at[peer_idx if receiving else me],
        send_sem=send_sems.at[sem], recv_sem=recv_sems.at[sem], device_id=peer, device_id_type=MESH_ID)


def _exchange_start_call(srcs, after, scatter, name):
    n = len(srcs)
    slot_shapes = [s.shape[1:] if scatter else s.shape for s in srcs]

    def body(*refs):
        src_refs, land_refs = refs[:n], refs[n:2 * n]
        send_sems, recv_sems = refs[2 * n + 1], refs[2 * n + 2]
        token = refs[-1]
        for i in range(n):
            for k in range(1, N_DEV):
                _exchange_copy(src_refs[i], land_refs[i], send_sems, recv_sems, i, k, scatter, False).start()
        token[...] = jnp.zeros_like(token)

    lands = [pltpu.with_memory_space_constraint(lax.empty((N_DEV,) + shp, s.dtype), pltpu.HBM)
             for shp, s in zip(slot_shapes, srcs)]
    out = pl.pallas_call(
        body,
        name=name,
        out_shape=([pltpu.SemaphoreType.DMA((n * N_PEERS,)), pltpu.SemaphoreType.DMA((n * N_PEERS,))]
                   + [pltpu.HBM(s.shape, s.dtype) for s in srcs] + [pltpu.HBM(l.shape, l.dtype) for l in lands]
                   + [jax.ShapeDtypeStruct((SUBLANES, LANES), F32)]),
        in_specs=[_HBM] * (2 * n) + [pl.BlockSpec(memory_space=pl.ANY)],
        out_specs=[_SEM, _SEM] + [_HBM] * (2 * n) + [pl.BlockSpec(memory_space=pltpu.VMEM)],
        input_output_aliases={j: 2 + j for j in range(2 * n)},
        compiler_params=pltpu.CompilerParams(has_side_effects=_SIDE_EFFECT),
    )(*[pltpu.with_memory_space_constraint(s, pltpu.HBM) for s in srcs], *lands, after)
    return out[0], out[1], list(out[2:2 + n]), list(out[2 + n:2 + 2 * n]), out[-1]


def _exchange_wait_call(started, after, scatter, name):
    send_sems, recv_sems, srcs, lands, _ = started
    n = len(srcs)

    def body(*refs):
        src_refs, land_refs = refs[:n], refs[n:2 * n]
        send_s, recv_s = refs[2 * n], refs[2 * n + 1]
        for i in range(n):
            for k in range(1, N_DEV):
                cp = _exchange_copy(src_refs[i], land_refs[i], send_s, recv_s, i, k, scatter, True)
                cp.wait_send()
                cp.wait_recv()

    out = pl.pallas_call(
        body,
        name=name,
        out_shape=[pltpu.HBM(s.shape, s.dtype) for s in srcs] + [pltpu.HBM(l.shape, l.dtype) for l in lands],
        in_specs=[_HBM] * (2 * n) + [_SEM, _SEM, pl.BlockSpec(memory_space=pl.ANY)],
        out_specs=[_HBM] * (2 * n),
        input_output_aliases={j: j for j in range(2 * n)},
        compiler_params=pltpu.CompilerParams(has_side_effects=_SIDE_EFFECT),
    )(*srcs, *lands, send_sems, recv_sems, after)
    return list(out[n:])


def _own_slot(land, own):
    x, y, c = _place()
    return lax.dynamic_update_index_in_dim(land, own, 4 * x + 2 * y + c, 0)


def _slot_sum_call(land, name):
    rows, cols = _rows_cols(land.shape[1:])
    tr = _pick_tile(rows, (256, 128, 64, 32, 16))

    def body(land_ref, o_ref):
        acc = land_ref[0].astype(F32)
        for s in range(1, N_DEV):
            acc = acc + land_ref[s].astype(F32)
        o_ref[...] = acc

    out = pl.pallas_call(
        body,
        name=name,
        grid=(rows // tr,),
        in_specs=[pl.BlockSpec((N_DEV, tr, cols), lambda i: (0, i, 0))],
        out_specs=pl.BlockSpec((tr, cols), lambda i: (i, 0)),
        out_shape=jax.ShapeDtypeStruct((rows, cols), F32),
        compiler_params=_cparams("parallel"),
    )(land.reshape(N_DEV, rows, cols))
    return out.reshape(land.shape[1:])


SMALL_BLOCK_ROWS = 16


def _pack_small(parts):
    flat = jnp.concatenate([p.reshape(-1) for p in parts])
    chunk = N_DEV * SMALL_BLOCK_ROWS * COMM_LANES
    flat = jnp.pad(flat, (0, (-flat.shape[0]) % chunk))
    return flat.reshape(N_DEV, -1, COMM_LANES)


def _unpack_small(buf, shapes):
    flat = buf.reshape(-1)
    out, off = [], 0
    for shp in shapes:
        n = math.prod(shp)
        out.append(flat[off:off + n].reshape(shp))
        off += n
    return out


SHARDED = dict(w_in=2, ssm_w_glu=2, mla_w_uq=2, mla_w_ukv=2, mla_w_o=2, hg_w_o=2, w_out=1, x_w_q=1, x_w_kv=1,
               x_w_o=2, ffn_w_gate_up=2, ffn_w_down=1)
REPLICATED = ("norm_mix", "ssm_lam_re", "ssm_lam_im", "ssm_b_re", "ssm_b_im", "ssm_c_re", "ssm_c_im", "ssm_d",
              "ssm_log_step", "mla_q_norm", "mla_kv_norm", "hg_lb", "hg_g_norm", "norm_cross", "norm_mem", "norm_ffn",
              "norm_final")


def _join_shards(stacked, axis):
    n, l, a, b = stacked.shape
    if axis == 1:
        return stacked.transpose(1, 0, 2, 3).reshape(l, n * a, b)
    return stacked.transpose(1, 2, 0, 3).reshape(l, a, n * b)


def _split_shards(full, axis):
    l, a, b = full.shape
    if axis == 1:
        return full.reshape(l, N_DEV, a // N_DEV, b).transpose(1, 0, 2, 3)
    return full.reshape(l, a, N_DEV, b // N_DEV).transpose(2, 0, 1, 3)


MIXER_WEIGHTS = ("w_in", "ssm_w_glu", "mla_w_uq", "mla_w_ukv", "mla_w_o", "hg_w_o", "w_out")
TAIL_WEIGHTS = ("x_w_q", "x_w_kv", "x_w_o", "ffn_w_gate_up", "ffn_w_down")


def _mixer_fn(l, tabs):
    def f(x, full, small, lower):
        m = _mixer_matrices(dict(zip(MIXER_WEIGHTS, full)), 0)
        return _mixer(x, tabs, m, dict(zip(REPLICATED, small)), l, lower[l].reshape(1, -1))
    return f


def _tail_fn(l, mem):
    def f(x, full, small):
        return _tail(x, mem, _tail_matrices(dict(zip(TAIL_WEIGHTS, full)), 0), dict(zip(REPLICATED, small)), l)
    return f


ADAM_LR, ADAM_B1, ADAM_B2, ADAM_EPS, ADAM_WD, ADAM_STEP = 0.001, 0.9, 0.999, 1e-08, 0.01, 10


def _adamw_update(w, g, m, v):
    m_new = ADAM_B1 * m + (1.0 - ADAM_B1) * g
    v_new = ADAM_B2 * v + (1.0 - ADAM_B2) * jnp.square(g)
    m_hat = m_new / (1.0 - ADAM_B1 ** ADAM_STEP)
    v_hat = v_new / (1.0 - ADAM_B2 ** ADAM_STEP)
    return -ADAM_LR * (m_hat / (jnp.sqrt(v_hat) + ADAM_EPS) + ADAM_WD * w), m_new, v_new


def _adamw_stacked_call(w, g, m, v, name):
    depth, rows, cols = w.shape
    tr = _pick_tile(rows, (512, 256, 128, 64, 32, 16, 8))

    def body(w_ref, g_ref, m_ref, v_ref, d_ref, nm_ref, nv_ref):
        d_ref[...], nm_ref[...], nv_ref[...] = _adamw_update(w_ref[...], g_ref[...], m_ref[...], v_ref[...])

    spec = pl.BlockSpec((None, tr, cols), lambda l, i: (l, i, 0))
    return tuple(pl.pallas_call(
        body, name=name, grid=(depth, rows // tr), in_specs=[spec] * 4, out_specs=[spec] * 3,
        out_shape=[jax.ShapeDtypeStruct(w.shape, F32)] * 3, compiler_params=_cparams("parallel", "parallel"),
    )(w, g, m, v))


def _adamw_call(w, g, m, v, name):
    shape = w.shape
    if len(shape) == 3:
        return _adamw_stacked_call(w, g, m, v, name)
    cols = shape[-1]
    rows = math.prod(shape[:-1]) if len(shape) > 1 else 1
    tr = _pick_tile(rows, (512, 256, 128, 64, 32, 16, 8))

    def body(w_ref, g_ref, m_ref, v_ref, d_ref, nm_ref, nv_ref):
        d_ref[...], nm_ref[...], nv_ref[...] = _adamw_update(w_ref[...], g_ref[...], m_ref[...], v_ref[...])

    spec = pl.BlockSpec((tr, cols), lambda i: (i, 0))
    outs = pl.pallas_call(
        body, name=name, grid=(rows // tr,), in_specs=[spec] * 4, out_specs=[spec] * 3,
        out_shape=[jax.ShapeDtypeStruct((rows, cols), F32)] * 3, compiler_params=_cparams("parallel"),
    )(*(t.reshape(rows, cols) for t in (w, g, m, v)))
    return tuple(o.reshape(shape) for o in outs)


WEIGHTS = ("norm_mix", "w_in", "ssm_lam_re", "ssm_lam_im", "ssm_b_re", "ssm_b_im", "ssm_c_re", "ssm_c_im", "ssm_d",
           "ssm_log_step", "ssm_w_glu", "mla_q_norm", "mla_kv_norm", "mla_w_uq", "mla_w_ukv", "mla_w_o", "hg_lb",
           "hg_g_norm", "hg_w_o", "w_out", "norm_cross", "norm_mem", "x_w_q", "x_w_kv", "x_w_o", "norm_ffn",
           "ffn_w_gate_up", "ffn_w_down", "norm_final")


def kernel(x, mem, positions, norm_mix, w_in, ssm_lam_re, ssm_lam_im, ssm_b_re, ssm_b_im, ssm_c_re, ssm_c_im, ssm_d, ssm_log_step, ssm_w_glu, mla_q_norm, mla_kv_norm, mla_w_uq, mla_w_ukv, mla_w_o, hg_lb, hg_g_norm, hg_w_o, w_out, norm_cross, norm_mem, x_w_q, x_w_kv, x_w_o, norm_ffn, ffn_w_gate_up, ffn_w_down, norm_final, loss_target, m_norm_mix, m_w_in, m_ssm_lam_re, m_ssm_lam_im, m_ssm_b_re, m_ssm_b_im, m_ssm_c_re, m_ssm_c_im, m_ssm_d, m_ssm_log_step, m_ssm_w_glu, m_mla_q_norm, m_mla_kv_norm, m_mla_w_uq, m_mla_w_ukv, m_mla_w_o, m_hg_lb, m_hg_g_norm, m_hg_w_o, m_w_out, m_norm_cross, m_norm_mem, m_x_w_q, m_x_w_kv, m_x_w_o, m_norm_ffn, m_ffn_w_gate_up, m_ffn_w_down, m_norm_final, v_norm_mix, v_w_in, v_ssm_lam_re, v_ssm_lam_im, v_ssm_b_re, v_ssm_b_im, v_ssm_c_re, v_ssm_c_im, v_ssm_d, v_ssm_log_step, v_ssm_w_glu, v_mla_q_norm, v_mla_kv_norm, v_mla_w_uq, v_mla_w_ukv, v_mla_w_o, v_hg_lb, v_hg_g_norm, v_hg_w_o, v_w_out, v_norm_cross, v_norm_mem, v_x_w_q, v_x_w_kv, v_x_w_o, v_norm_ffn, v_ffn_w_gate_up, v_ffn_w_down, v_norm_final):
    given = dict(locals())
    weights = {n: given[n] for n in WEIGHTS}
    small = tuple(weights[n] for n in REPLICATED)
    layer1 = MIXER_WEIGHTS + TAIL_WEIGHTS
    shards = lambda names, l: [weights[n][l:l + 1].astype(BF16) for n in names]
    join = lambda names, stacked: tuple(_join_shards(p, SHARDED[n]) for n, p in zip(names, stacked))
    split = lambda names, cts: [_split_shards(ct, SHARDED[n]) for n, ct in zip(names, cts)]
    landed = lambda names, lands, own: join(names, [_own_slot(land, o) for land, o in zip(lands, own)])
    xs, tabs = x[0], _rope_tables(positions[0])
    lower, vjp_lower = jax.vjp(_lower_bounds, hg_lb)
    me = 4 * lax.axis_index("x") + 2 * lax.axis_index("y") + lax.axis_index("c")

    got_m0 = _all_gather_call(shards(MIXER_WEIGHTS, 0), "weights_all_gather_m0")
    own_t0, own_l1 = shards(TAIL_WEIGHTS, 0), shards(layer1, 1)
    gather_t0 = _exchange_start_call(own_t0, got_m0[0], False, "weights_gather_start_t0")
    gather_l1 = _exchange_start_call(own_l1, gather_t0[4], False, "weights_gather_start_l1")
    xs = xs + gather_l1[4][0, 0]
    xa0, vjp_m0 = jax.vjp(_mixer_fn(0, tabs), xs, join(MIXER_WEIGHTS, got_m0), small, lower)
    full_t0 = landed(TAIL_WEIGHTS, _exchange_wait_call(gather_t0, xa0, False, "weights_gather_wait_t0"), own_t0)
    x1, vjp_t0 = jax.vjp(_tail_fn(0, mem[0]), xa0, full_t0, small)
    full_l1 = landed(layer1, _exchange_wait_call(gather_l1, x1, False, "weights_gather_wait_l1"), own_l1)
    xa1, vjp_m1 = jax.vjp(_mixer_fn(1, tabs), x1, full_l1[:len(MIXER_WEIGHTS)], small, lower)
    x2, vjp_t1 = jax.vjp(_tail_fn(1, mem[0]), xa1, full_l1[len(MIXER_WEIGHTS):], small)
    loss_local, vjp_loss = jax.vjp(functools.partial(_final_loss, loss_target[0]), x2, norm_final)

    def scatter_start(names, cts, dx, tag):
        gs = split(names, cts)
        started = _exchange_start_call(gs, dx, True, "grads_scatter_start_" + tag)
        return (started, gs), dx + started[4][0, 0]

    def scatter_finish(pending, after, tag):
        started, gs = pending
        lands = _exchange_wait_call(started, after, True, "grads_scatter_wait_" + tag)
        return [_slot_sum_call(_own_slot(land, lax.dynamic_index_in_dim(g, me, 0, keepdims=False)), "grads_slot_sum_" + tag)
                for land, g in zip(lands, gs)]

    dx2, d_norm_final = vjp_loss(jnp.ones((), F32))
    dxa1, dfull_t1, dsmall_t1 = vjp_t1(dx2)
    dx1, dfull_m1, dsmall_m1, dlower1 = vjp_m1(dxa1)
    pend_l1, dx1 = scatter_start(layer1, dfull_m1 + dfull_t1, dx1, "l1")
    dxa0, dfull_t0, dsmall_t0 = vjp_t0(dx1)
    pend_t0, dxa0 = scatter_start(TAIL_WEIGHTS, dfull_t0, dxa0, "t0")
    gx, dfull_m0, dsmall_m0, dlower0 = vjp_m0(dxa0)
    pend_m0, gx = scatter_start(MIXER_WEIGHTS, dfull_m0, gx, "m0")
    g_l1 = dict(zip(layer1, scatter_finish(pend_l1, gx, "l1")))
    g_t0 = dict(zip(TAIL_WEIGHTS, scatter_finish(pend_t0, gx, "t0")))
    grads = {n: jnp.concatenate([g_t0[n], g_l1[n]], axis=0) for n in TAIL_WEIGHTS}

    d_small = dict(zip(REPLICATED, (a + b + c + d for a, b, c, d in zip(dsmall_m0, dsmall_t0, dsmall_m1, dsmall_t1))))
    d_small["norm_final"] = d_small["norm_final"] + d_norm_final
    d_small["hg_lb"] = d_small["hg_lb"] + vjp_lower(dlower0 + dlower1)[0]
    shapes = [d_small[n].shape for n in REPLICATED]
    (mine,) = _reduce_scatter([_pack_small([d_small[n] for n in REPLICATED])], "small_reduce_scatter")
    (total,) = _all_gather_call([mine], "small_all_gather")
    grads.update(zip(REPLICATED, _unpack_small(total, shapes)))

    loss = lax.psum(loss_local, ("x", "y", "c"))
    adamw = lambda n: _adamw_call(weights[n], grads[n], given["m_" + n], given["v_" + n], "adamw_" + n)
    steps = {n: adamw(n) for n in TAIL_WEIGHTS + REPLICATED}
    g_m0 = scatter_finish(pend_m0, steps["ffn_w_gate_up"][0], "m0")
    grads.update({n: jnp.concatenate([g, g_l1[n]], axis=0) for n, g in zip(MIXER_WEIGHTS, g_m0)})
    steps.update({n: adamw(n) for n in MIXER_WEIGHTS})
    return (loss, gx[None], *[grads[n] for n in WEIGHTS], *[steps[n][0] for n in WEIGHTS],
            *[steps[n][1] for n in WEIGHTS], *[steps[n][2] for n in WEIGHTS])
```

```python
import functools
import math

import jax
import jax.numpy as jnp
from jax import lax
from jax.experimental import pallas as pl
from jax.experimental.pallas import tpu as pltpu

F32 = jnp.float32
BF16 = jnp.bfloat16

VMEM_LIMIT_BYTES = 48 * 1024 * 1024
LANES = 128
SUBLANES = 8


def _cparams(*sem):
    return pltpu.CompilerParams(dimension_semantics=sem, vmem_limit_bytes=VMEM_LIMIT_BYTES)


def _pick_tile(n, cands):
    for c in cands:
        if n % c == 0:
            return c
    return n


MM_VMEM_BUDGET = 38 * 1024 * 1024
MM_STEP_US = 0.35
HBM_BYTES_PER_US = 3.0e6
VREG_RMW_PER_US = 1.5e3


def _divisor_tiles(dim, cands):
    out = [t for t in cands if dim % t == 0]
    return out or [dim]


def _mm_tiles(m, n, k, sa, sb, so):
    tms = _divisor_tiles(m, (1408, 1024, 512, 256, 128, 64, 32, 16, 8))[:2]
    tns = _divisor_tiles(n, (2048, 1536, 1408, 1024, 768, 512, 384, 256, 128))
    tks = [k // d for d in (1, 2, 4, 8, 13, 16, 26, 32, 52) if k % d == 0 and (k // d) % LANES == 0] or [k]
    best = None
    for tk in tks:
        nk = k // tk
        for tm in tms:
            for tn in tns:
                vmem = 2 * (tm * tk * sa + tk * tn * sb + tm * tn * so) + (tm * tn * 4 if nk > 1 else 0)
                if vmem > MM_VMEM_BUDGET:
                    continue
                steps = (m // tm) * (n // tn) * nk
                a_reads = m * k * sa * (n // tn if nk > 1 else 1)
                b_reads = k * n * sb * (m // tm if (nk > 1 or n // tn > 1) else 1)
                cost = (steps * MM_STEP_US + (a_reads + b_reads) / HBM_BYTES_PER_US
                        + (m * n * nk / 1024 / VREG_RMW_PER_US if nk > 1 else 0.0))
                if best is None or cost < best[0]:
                    best = (cost, tm, tn, tk)
    assert best is not None, (m, n, k)
    return best[1:]


def _mm_tiles_cached_t(m, n, k, sa, sb, so):
    for tm in _divisor_tiles(m, (1024, 512)):
        if m % tm:
            break
        for tn in _divisor_tiles(n, (1024, 512, 384, 256, 128)):
            if 2 * (k * tm * sa + k * tn * sb + tm * tn * so) + tm * k * 2 <= MM_VMEM_BUDGET:
                return tm, tn
    return None


def _mm_tn_cached_call(a, b, tiles, out_dtype, name):
    k, m = a.shape
    n = b.shape[1]
    tm, tn = tiles

    def body(a_ref, b_ref, o_ref, at_ref):
        @pl.when(pl.program_id(1) == 0)
        def _():
            at_ref[...] = a_ref[...].astype(BF16).T

        o_ref[...] = lax.dot_general(at_ref[...], b_ref[...].astype(BF16), _NN_DIMS,
                                     preferred_element_type=F32).astype(out_dtype)

    return pl.pallas_call(
        body,
        name=name,
        grid=(m // tm, n // tn),
        in_specs=[pl.BlockSpec((k, tm), lambda i, j: (0, i)), pl.BlockSpec((k, tn), lambda i, j: (0, j))],
        out_specs=pl.BlockSpec((tm, tn), lambda i, j: (i, j)),
        out_shape=jax.ShapeDtypeStruct((m, n), out_dtype),
        scratch_shapes=[pltpu.VMEM((tm, k), BF16)],
        compiler_params=_cparams("parallel", "arbitrary"),
    )(a, b)


_NN_DIMS = (((1,), (0,)), ((), ()))


def _mm_call(a, b, ta, tb, add=None, out_dtype=F32, name="mm"):
    m, k = (a.shape[1], a.shape[0]) if ta else a.shape
    k2, n = (b.shape[1], b.shape[0]) if tb else b.shape
    assert k == k2, (a.shape, b.shape, ta, tb)
    sizes = (a.dtype.itemsize, b.dtype.itemsize, jnp.dtype(out_dtype).itemsize + (add.dtype.itemsize if add is not None else 0))
    if ta:
        tiles = _mm_tiles_cached_t(m, n, k, *sizes)
        if tiles is not None:
            return _mm_tn_cached_call(a, b, tiles, out_dtype, name)
    tm, tn, tk = _mm_tiles(m, n, k, *sizes)
    nk = k // tk
    a_spec = pl.BlockSpec((tk, tm), lambda i, j, kk: (kk, i)) if ta else pl.BlockSpec((tm, tk), lambda i, j, kk: (i, kk))
    b_spec = pl.BlockSpec((tn, tk), lambda i, j, kk: (j, kk)) if tb else pl.BlockSpec((tk, tn), lambda i, j, kk: (kk, j))
    o_spec = pl.BlockSpec((tm, tn), lambda i, j, kk: (i, j))
    dn = (((0 if ta else 1,), (1 if tb else 0,)), ((), ()))
    has_add = add is not None

    def body(*refs):
        a_ref, b_ref = refs[0], refs[1]
        c_ref = refs[2] if has_add else None
        o_ref = refs[3] if has_add else refs[2]
        p = lax.dot_general(a_ref[...].astype(BF16), b_ref[...].astype(BF16), dn, preferred_element_type=F32)

        def finish(r):
            if has_add:
                r = r + c_ref[...].astype(F32)
            o_ref[...] = r.astype(out_dtype)

        if nk == 1:
            finish(p)
        else:
            acc_ref = refs[-1]
            kk = pl.program_id(2)

            @pl.when(kk == 0)
            def _():
                acc_ref[...] = p

            @pl.when(kk > 0)
            def _():
                acc_ref[...] += p

            @pl.when(kk == nk - 1)
            def _():
                finish(acc_ref[...])

    in_specs = [a_spec, b_spec] + ([o_spec] if has_add else [])
    args = (a, b) + ((add,) if has_add else ())
    return pl.pallas_call(
        body,
        name=name,
        grid=(m // tm, n // tn, nk),
        in_specs=in_specs,
        out_specs=o_spec,
        out_shape=jax.ShapeDtypeStruct((m, n), out_dtype),
        scratch_shapes=[] if nk == 1 else [pltpu.VMEM((tm, tn), F32)],
        compiler_params=_cparams("parallel", "parallel", "arbitrary"),
    )(*args)


@functools.partial(jax.custom_vjp, nondiff_argnums=(2,))
def matmul(a, b, out_dtype=F32):
    return _mm_call(a, b, False, False, out_dtype=out_dtype, name="mm_fwd")


def _matmul_fwd(a, b, out_dtype):
    return matmul(a, b, out_dtype), (a, b)


def _matmul_bwd(out_dtype, res, g):
    a, b = res
    da = _mm_call(g, b, False, True, out_dtype=a.dtype, name="mm_da")
    db = _mm_call(a, g, True, False, out_dtype=b.dtype, name="mm_db")
    return da, db


matmul.defvjp(_matmul_fwd, _matmul_bwd)


@jax.custom_vjp
def matmul_add(a, b, c):
    return _mm_call(a, b, False, False, add=c, name="mm_add_fwd")


def _matmul_add_fwd(a, b, c):
    return _mm_call(a, b, False, False, add=c, name="mm_add_fwd"), (a, b)


def _matmul_add_bwd(res, g):
    a, b = res
    da = _mm_call(g, b, False, True, out_dtype=a.dtype, name="mm_da")
    db = _mm_call(a, g, True, False, out_dtype=b.dtype, name="mm_db")
    return da, db, g


matmul_add.defvjp(_matmul_add_fwd, _matmul_add_bwd)


def rowwise(f, n_rows, n_aux, tile, name, passthrough=False):
    def specs(arrs, tiled):
        out = []
        for x in arrs:
            if tiled:
                out.append(pl.BlockSpec((tile, x.shape[1]), lambda i: (i, 0)))
            else:
                out.append(pl.BlockSpec(x.shape, lambda i: (0, 0)))
        return out

    def tile_structs(args):
        rows_aux, params = args[: n_rows + n_aux], args[n_rows + n_aux:]
        return [jax.ShapeDtypeStruct((tile, x.shape[1]), x.dtype) for x in rows_aux] + [
            jax.ShapeDtypeStruct(p.shape, p.dtype) for p in params]

    def fwd_call(*args):
        s = args[0].shape[0]
        outs = jax.eval_shape(f, *tile_structs(args))
        n_in = len(args)

        def body(*refs):
            vals = [r[...] for r in refs[:n_in]]
            res = f(*vals)
            for o_ref, r in zip(refs[n_in:], res):
                o_ref[...] = r.astype(o_ref.dtype)

        return pl.pallas_call(
            body,
            name=name + "_fwd",
            grid=(s // tile,),
            in_specs=specs(args[: n_rows + n_aux], True) + specs(args[n_rows + n_aux:], False),
            out_specs=[pl.BlockSpec((tile, o.shape[1]), lambda i: (i, 0)) for o in outs],
            out_shape=[jax.ShapeDtypeStruct((s, o.shape[1]), o.dtype) for o in outs],
            compiler_params=_cparams("parallel"),
        )(*args)

    def bwd_call(args, gs):
        s = args[0].shape[0]
        rows, aux, params = args[:n_rows], args[n_rows:n_rows + n_aux], args[n_rows + n_aux:]
        n_in, n_g, n_p = len(args), len(gs), len(params)
        n_gf = n_g - 1 if passthrough else n_g

        def body(*refs):
            vals = [r[...] for r in refs[:n_in]]
            gvals = tuple(r[...] for r in refs[n_in:n_in + n_gf])
            out_refs = refs[n_in + n_g:]
            auxv = vals[n_rows:n_rows + n_aux]

            def g_(*rp):
                return tuple(f(*rp[:n_rows], *auxv, *rp[n_rows:]))

            _, vjp = jax.vjp(g_, *vals[:n_rows], *vals[n_rows + n_aux:])
            cts = list(vjp(gvals))
            if passthrough:
                cts[0] = cts[0] + refs[n_in + n_gf][...]
            for o_ref, ct in zip(out_refs[:n_rows], cts[:n_rows]):
                o_ref[...] = ct.astype(o_ref.dtype)
            if n_p:
                @pl.when(pl.program_id(0) == 0)
                def _():
                    for o_ref in out_refs[n_rows:]:
                        o_ref[...] = jnp.zeros_like(o_ref)

                for o_ref, ct in zip(out_refs[n_rows:], cts[n_rows:]):
                    o_ref[...] += ct.astype(o_ref.dtype)

        return pl.pallas_call(
            body,
            name=name + "_bwd",
            grid=(s // tile,),
            in_specs=specs(rows + aux, True) + specs(params, False) + specs(gs, True),
            out_specs=specs(rows, True) + specs(params, False),
            out_shape=[jax.ShapeDtypeStruct(x.shape, x.dtype) for x in rows + params],
            compiler_params=_cparams("arbitrary" if n_p else "parallel"),
        )(*args, *gs)

    @jax.custom_vjp
    def op(*args):
        return tuple(fwd_call(*args)) + ((args[0],) if passthrough else ())

    def op_fwd(*args):
        return op(*args), args

    def op_bwd(args, gs):
        cts = bwd_call(tuple(args), tuple(gs))
        rows_ct, par_ct = cts[:n_rows], cts[n_rows:]
        aux_ct = [jnp.zeros_like(a) for a in args[n_rows:n_rows + n_aux]]
        return tuple(rows_ct) + tuple(aux_ct) + tuple(par_ct)

    op.defvjp(op_fwd, op_bwd)
    return op


SCAN_SEGMENTS = SUBLANES


def _scan_step(ar, ai, xr, xi, br, bi):
    return ar * xr - ai * xi + br, ar * xi + ai * xr + bi


_NT_DIMS = (((1,), (1,)), ((), ()))
_TN_DIMS = (((0,), (0,)), ((), ()))


S5_GROUPS_PER_STEP = 128
S5_STATE_LANES = 512


def _s5_interleave(src_ref, scr_ref, rows):
    for k in range(SCAN_SEGMENTS):
        scr_ref[pl.ds(k, rows, stride=SCAN_SEGMENTS), :] = src_ref[k].astype(F32)


def _s5_deinterleave(val, scr_ref, dst_ref, rows):
    scr_ref[...] = val
    for k in range(SCAN_SEGMENTS):
        dst_ref[k] = scr_ref[pl.ds(k, rows, stride=SCAN_SEGMENTS), :].astype(dst_ref.dtype)


def _s5_segment_starts(a_ref, fr_ref, fi_ref, sr, si, seg_len, order):
    tn = sr.shape[1]
    pr, pi = a_ref[0:1, :], a_ref[1:2, :]
    for _ in range(seg_len.bit_length() - 1):
        pr, pi = pr * pr - pi * pi, 2.0 * pr * pi
    cr = jnp.zeros((1, tn), F32)
    ci = jnp.zeros((1, tn), F32)
    for idx, k in enumerate(order):
        if idx > 0:
            kp = order[idx - 1]
            cr, ci = (fr_ref[kp:kp + 1, :] + pr * cr - pi * ci, fi_ref[kp:kp + 1, :] + pr * ci + pi * cr)
        sr[k:k + 1, :] = cr
        si[k:k + 1, :] = ci


def _s5_pass_call(src, w_r, w_i, a, transpose_w, reverse, finals, extra, name):
    s = src.shape[0]
    nb = w_r.shape[0]
    n = nb * S5_STATE_LANES
    seg_len = s // SCAN_SEGMENTS
    ti = min(S5_GROUPS_PER_STEP, seg_len)
    nt = seg_len // ti
    tr = SCAN_SEGMENTS * ti
    tn = S5_STATE_LANES
    assert seg_len & (seg_len - 1) == 0
    order = list(range(SCAN_SEGMENTS))[::-1] if reverse else list(range(SCAN_SEGMENTS))
    dn_in = _NT_DIMS if transpose_w else _NN_DIMS
    first = finals is None
    backward = (not first) and reverse
    forward = (not first) and not reverse
    tmap3 = (lambda c, j: (0, nt - 1 - j, c)) if reverse else (lambda c, j: (0, j, c))
    tmap2 = (lambda c, j: (nt - 1 - j, c)) if reverse else (lambda c, j: (j, c))
    bf = lambda v: v.astype(BF16)

    def body(*refs):
        it = iter(refs)
        src_ref, wr_ref, wi_ref, a_ref = next(it), next(it), next(it), next(it)
        if not first:
            fr_ref, fi_ref = next(it), next(it)
        if forward:
            cdr_ref, cdi_ref = next(it), next(it)
            xr_out, xi_out, y_ref = next(it), next(it), next(it)
        if backward:
            xr_ref, xi_ref, u_ref, bdr_ref, bdi_ref = next(it), next(it), next(it), next(it), next(it)
            du_ref, dar_ref, dai_ref, dbr_ref, dbi_ref, dcr_ref, dci_ref = (next(it) for _ in range(7))
        if first:
            fr_out, fi_out = next(it), next(it)
        sr, si, in_scr, dr_scr, di_scr = next(it), next(it), next(it), next(it), next(it)
        if backward:
            accr, acci, u_scr = next(it), next(it), next(it)
        j = pl.program_id(1)

        @pl.when(j == 0)
        def _():
            if first:
                sr[...] = jnp.zeros_like(sr)
                si[...] = jnp.zeros_like(si)
            else:
                _s5_segment_starts(a_ref, fr_ref, fi_ref, sr, si, seg_len, order)
            if backward:
                for r in (accr, acci, dbr_ref, dbi_ref, dcr_ref, dci_ref):
                    r[...] = jnp.zeros_like(r)

        _s5_interleave(src_ref, in_scr, ti)
        src_b = bf(in_scr[...])
        dr_scr[...] = lax.dot_general(src_b, bf(wr_ref[0]), dn_in, preferred_element_type=F32)
        di_scr[...] = lax.dot_general(src_b, bf(wi_ref[0]), dn_in, preferred_element_type=F32)
        ar = jnp.broadcast_to(a_ref[0:1, :], (SUBLANES, tn))
        ai = jnp.broadcast_to(a_ref[1:2, :], (SUBLANES, tn))

        def step(ii, carry):
            i = (ti - 1 - ii) if reverse else ii
            rows = pl.ds(pl.multiple_of(i * SUBLANES, SUBLANES), SUBLANES)
            xr, xi = carry[0], carry[1]
            if backward:
                zr, zi = xr_ref[rows, :], xi_ref[rows, :]
                acc = (carry[2] + xr * zr + xi * zi, carry[3] + xi * zr - xr * zi)
            nr, ni = _scan_step(ar, ai, xr, xi, dr_scr[rows, :], di_scr[rows, :])
            if forward:
                xr_out[rows, :] = nr
                xi_out[rows, :] = ni
            if backward:
                dr_scr[rows, :] = nr
                di_scr[rows, :] = ni
            return (nr, ni) + (acc if backward else ())

        init = (sr[...], si[...]) + ((accr[...], acci[...]) if backward else ())
        out = lax.fori_loop(0, ti, step, init, unroll=4)
        sr[...] = out[0]
        si[...] = out[1]
        if first:
            @pl.when(j == nt - 1)
            def _():
                fr_out[...] = out[0]
                fi_out[...] = out[1]
        if forward:
            y = (lax.dot_general(bf(xr_out[...]), bf(cdr_ref[0]), _NN_DIMS, preferred_element_type=F32)
                 + lax.dot_general(bf(xi_out[...]), bf(cdi_ref[0]), _NN_DIMS, preferred_element_type=F32))
            _s5_deinterleave(y, in_scr, y_ref, ti)
        if backward:
            accr[...] = out[2]
            acci[...] = out[3]
            g_r, g_i = bf(dr_scr[...]), bf(di_scr[...])
            dcr_ref[0] += lax.dot_general(bf(xr_ref[...]), src_b, _TN_DIMS, preferred_element_type=F32)
            dci_ref[0] += lax.dot_general(bf(xi_ref[...]), src_b, _TN_DIMS, preferred_element_type=F32)
            _s5_interleave(u_ref, u_scr, ti)
            u_b = bf(u_scr[...])
            dbr_ref[0] += lax.dot_general(u_b, g_r, _TN_DIMS, preferred_element_type=F32)
            dbi_ref[0] += lax.dot_general(u_b, g_i, _TN_DIMS, preferred_element_type=F32)
            du = (lax.dot_general(g_r, bf(bdr_ref[0]), _NT_DIMS, preferred_element_type=F32)
                  + lax.dot_general(g_i, bf(bdi_ref[0]), _NT_DIMS, preferred_element_type=F32))
            _s5_deinterleave(du, u_scr, du_ref, ti)

            @pl.when(j == nt - 1)
            def _():
                dar_ref[...] = jnp.sum(out[2], axis=0, keepdims=True)
                dai_ref[...] = jnp.sum(out[3], axis=0, keepdims=True)

    view3 = lambda t: t.reshape(SCAN_SEGMENTS, seg_len, t.shape[1])
    spec3 = pl.BlockSpec((SCAN_SEGMENTS, ti, LANES), tmap3)
    wspec = lambda w: pl.BlockSpec((1,) + w.shape[1:], lambda c, j: (c, 0, 0))
    aspec = pl.BlockSpec((2, tn), lambda c, j: (0, c))
    fspec = pl.BlockSpec((SUBLANES, tn), lambda c, j: (0, c))
    xspec = pl.BlockSpec((tr, tn), tmap2)
    dspec = pl.BlockSpec((1, tn), lambda c, j: (0, c))
    f32 = lambda *shape: jax.ShapeDtypeStruct(shape, F32)
    args, in_specs = [view3(src), w_r, w_i, a], [spec3, wspec(w_r), wspec(w_i), aspec]
    if not first:
        args += list(finals)
        in_specs += [fspec, fspec]
    if forward:
        args += list(extra)
        in_specs += [wspec(extra[0]), wspec(extra[1])]
        out_specs, out_shape = [xspec, xspec, spec3], [f32(s, n), f32(s, n), f32(SCAN_SEGMENTS, seg_len, nb * LANES)]
    elif backward:
        x_r, x_i, u, bd_r, bd_i = extra
        args += [x_r, x_i, view3(u), bd_r, bd_i]
        in_specs += [xspec, xspec, spec3, wspec(bd_r), wspec(bd_i)]
        out_specs = [spec3, dspec, dspec, wspec(bd_r), wspec(bd_i), wspec(w_r), wspec(w_i)]
        out_shape = [jax.ShapeDtypeStruct((SCAN_SEGMENTS, seg_len, nb * LANES), u.dtype), f32(1, n), f32(1, n),
                     f32(*bd_r.shape), f32(*bd_i.shape),
                     f32(*w_r.shape), f32(*w_i.shape)]
    else:
        out_specs, out_shape = [fspec, fspec], [f32(SUBLANES, n), f32(SUBLANES, n)]
    scratch = ([pltpu.VMEM((SUBLANES, tn), F32)] * 2 + [pltpu.VMEM((tr, LANES), F32)] + [pltpu.VMEM((tr, tn), F32)] * 2
               + ([pltpu.VMEM((SUBLANES, tn), F32)] * 2 + [pltpu.VMEM((tr, LANES), F32)] if backward else []))
    return pl.pallas_call(
        body, name=name, grid=(nb, nt), in_specs=in_specs, out_specs=out_specs, out_shape=out_shape,
        scratch_shapes=scratch, compiler_params=_cparams("parallel", "arbitrary"),
    )(*args)


@jax.custom_vjp
def s5_core(u, a, bd_r, bd_i, cd_r, cd_i):
    return _s5_core_fwd(u, a, bd_r, bd_i, cd_r, cd_i)[0]


def _s5_core_fwd(u, a, bd_r, bd_i, cd_r, cd_i):
    fin = _s5_pass_call(u, bd_r, bd_i, a, False, False, None, None, "s5_fwd_finals")
    x_r, x_i, y = _s5_pass_call(u, bd_r, bd_i, a, False, False, fin, (cd_r, cd_i), "s5_fwd_scan")
    return y.reshape(u.shape), (u, a, bd_r, bd_i, cd_r, cd_i, x_r, x_i)


def _s5_core_bwd(res, dy):
    u, a, bd_r, bd_i, cd_r, cd_i, x_r, x_i = res
    a_conj = a * jnp.array([[1.0], [-1.0]], F32)
    fin = _s5_pass_call(dy, cd_r, cd_i, a_conj, True, True, None, None, "s5_bwd_finals")
    du, da_r, da_i, dbd_r, dbd_i, dcd_r, dcd_i = _s5_pass_call(
        dy, cd_r, cd_i, a_conj, True, True, fin, (x_r, x_i, u, bd_r, bd_i), "s5_bwd_scan")
    return du.reshape(u.shape), jnp.concatenate([da_r, da_i], axis=0), dbd_r, dbd_i, dcd_r, dcd_i


s5_core.defvjp(_s5_core_fwd, _s5_core_bwd)


_NN = (((1,), (0,)), ((), ()))
_NT = (((1,), (1,)), ((), ()))
_TN = (((0,), (0,)), ((), ()))


def _dot(a, b, dn):
    return lax.dot_general(a.astype(BF16), b.astype(BF16), dn, preferred_element_type=F32)


@jax.custom_vjp
def bdot_nn(a, b):
    return _dot(a, b, _NN)


bdot_nn.defvjp(lambda a, b: (_dot(a, b, _NN), (a, b)),
               lambda r, g: (_dot(g, r[1], _NT).astype(r[0].dtype), _dot(r[0], g, _TN).astype(r[1].dtype)))


@jax.custom_vjp
def bdot_nt(a, b):
    return _dot(a, b, _NT)


bdot_nt.defvjp(lambda a, b: (_dot(a, b, _NT), (a, b)),
               lambda r, g: (_dot(g, r[1], _NN).astype(r[0].dtype), _dot(g, r[0], _TN).astype(r[1].dtype)))


@jax.custom_vjp
def bdot_tn(a, b):
    return _dot(a, b, _TN)


bdot_tn.defvjp(lambda a, b: (_dot(a, b, _TN), (a, b)),
               lambda r, g: (_dot(r[1], g, _NT).astype(r[0].dtype), _dot(r[0], g, _NN).astype(r[1].dtype)))


def _split_hi_lo(x):
    h = x.astype(BF16)
    return h, (x - h.astype(F32)).astype(BF16)


def _exact_dot(t, x, dn):
    h, l = _split_hi_lo(x)
    d = lambda p: lax.dot_general(t, p, dn, preferred_element_type=F32)
    return d(h) + d(l)


@jax.custom_vjp
def select_dot(t, x):
    return _exact_dot(t, x, _NN)


select_dot.defvjp(lambda t, x: (_exact_dot(t, x, _NN), t),
                  lambda t, g: (jnp.zeros_like(t), _exact_dot(t, g, _TN)))


def _split_rows_impl(x, h):
    return tuple(x[i * h:(i + 1) * h] for i in range(x.shape[0] // h))


@functools.partial(jax.custom_vjp, nondiff_argnums=(1,))
def split_rows(x, h):
    return _split_rows_impl(x, h)


split_rows.defvjp(lambda x, h: (_split_rows_impl(x, h), None),
                  lambda h, r, g: (jnp.concatenate(g, axis=0),))


@jax.custom_vjp
def join_rows(parts):
    return jnp.concatenate(parts, axis=0)


def _join_rows_bwd(hs, g):
    out, off = [], 0
    for h in hs:
        out.append(g[off:off + h])
        off += h
    return (tuple(out),)


join_rows.defvjp(lambda parts: (jnp.concatenate(parts, axis=0), tuple(p.shape[0] for p in parts)), _join_rows_bwd)


def _split_lanes_impl(x, w):
    return tuple(x[:, i * w:(i + 1) * w] for i in range(x.shape[1] // w))


@functools.partial(jax.custom_vjp, nondiff_argnums=(1,))
def split_lanes(x, w):
    return _split_lanes_impl(x, w)


split_lanes.defvjp(lambda x, w: (_split_lanes_impl(x, w), None),
                   lambda w, r, g: (jnp.concatenate(g, axis=1),))


def _join_impl(parts):
    return jnp.concatenate(parts, axis=1)


@jax.custom_vjp
def join_lanes(parts):
    return _join_impl(parts)


def _join_bwd(ws, g):
    out, off = [], 0
    for w in ws:
        out.append(g[:, off:off + w])
        off += w
    return (tuple(out),)


join_lanes.defvjp(lambda parts: (_join_impl(parts), tuple(p.shape[1] for p in parts)), _join_bwd)


def _rope_impl(x, c, sa, sb, shift):
    w = x.shape[1]
    return x * c + pltpu.roll(x, w - shift, 1) * sa + pltpu.roll(x, shift, 1) * sb


@functools.partial(jax.custom_vjp, nondiff_argnums=(4,))
def rope_lanes(x, c, sa, sb, shift):
    return _rope_impl(x, c, sa, sb, shift)


def _rope_bwd(shift, r, g):
    c, sa, sb = r
    w = g.shape[1]
    dx = g * c + pltpu.roll(g * sa, shift, 1) + pltpu.roll(g * sb, w - shift, 1)
    return dx, jnp.zeros_like(c), jnp.zeros_like(sa), jnp.zeros_like(sb)


rope_lanes.defvjp(lambda x, c, sa, sb, shift: (_rope_impl(x, c, sa, sb, shift), (c, sa, sb)), _rope_bwd)


RMS_EPS = 1e-6


def _rms(x, g):
    return x * lax.rsqrt(jnp.mean(x * x, axis=-1, keepdims=True) + RMS_EPS) * g


ATTN_BLOCK = 512
MASK_VALUE = -1e30
LOG2E = math.log2(math.e)
LN2 = math.log(2.0)
V_ONES_LANE = 64


def _causal_mask(t):
    r = lax.broadcasted_iota(jnp.int32, (t, t), 0)
    c = lax.broadcasted_iota(jnp.int32, (t, t), 1)
    return c <= r


def _attn_fwd_call(q, k, v):
    s, width = q.shape
    n_heads = width // LANES
    tq = min(ATTN_BLOCK, s)
    nq = s // tq

    def body(q_ref, k_ref, v_ref, o_ref, lse_ref):
        i = pl.program_id(1)
        qb = q_ref[...].astype(BF16)
        ones_lane = lax.broadcasted_iota(jnp.int32, (tq, LANES), 1) == V_ONES_LANE

        def block(kb, carry, masked):
            m, acc = carry
            rows = pl.ds(pl.multiple_of(kb * tq, tq), tq)
            sc = lax.dot_general(qb, k_ref[rows, :].astype(BF16), _NT, preferred_element_type=F32)
            if masked:
                sc = jnp.where(_causal_mask(tq), sc, MASK_VALUE)
            m_new = jnp.maximum(m, jnp.max(sc, axis=-1, keepdims=True))
            p = jnp.exp2(sc - m_new).astype(BF16)
            vb = jnp.where(ones_lane, 1.0, v_ref[rows, :]).astype(BF16)
            acc = jnp.exp2(m - m_new) * acc + lax.dot_general(p, vb, _NN, preferred_element_type=F32)
            return m_new, acc

        init = (jnp.full((tq, 1), MASK_VALUE, F32), jnp.zeros((tq, LANES), F32))
        carry = lax.fori_loop(0, i, lambda kb, c: block(kb, c, False), init)
        m, acc = block(i, carry, True)
        l = jnp.sum(jnp.where(ones_lane, acc, 0.0), axis=-1, keepdims=True)
        o_ref[...] = jnp.where(ones_lane, 0.0, acc / l).astype(o_ref.dtype)
        lse_ref[...] = jnp.broadcast_to(m + jnp.log2(l), (tq, LANES))

    qspec = pl.BlockSpec((tq, LANES), lambda h, i: (i, h))
    kspec = pl.BlockSpec((s, LANES), lambda h, i: (0, h))
    return pl.pallas_call(
        body,
        name="mla_attn_fwd",
        grid=(n_heads, nq),
        in_specs=[qspec, kspec, kspec],
        out_specs=[qspec, qspec],
        out_shape=[jax.ShapeDtypeStruct((s, width), BF16), jax.ShapeDtypeStruct((s, width), F32)],
        compiler_params=_cparams("parallel", "parallel"),
    )(q, k, v)


def _attn_bwd_call(q, k, v, o, lse, do):
    s, width = q.shape
    n_heads = width // LANES
    tq = min(ATTN_BLOCK, s)
    nq = s // tq

    def body(q_ref, k_ref, v_ref, o_ref, lse_ref, do_ref, dq_ref, dk_ref, dv_ref, dq_acc):
        j = pl.program_id(1)

        @pl.when(j == 0)
        def _():
            dq_acc[...] = jnp.zeros_like(dq_acc)

        kb = k_ref[...].astype(BF16)
        vb = v_ref[...].astype(BF16)

        def block(i, carry, masked):
            dk, dv = carry
            rows = pl.ds(pl.multiple_of(i * tq, tq), tq)
            qi = q_ref[rows, :].astype(BF16)
            doi = do_ref[rows, :].astype(F32)
            delta = jnp.sum(doi * o_ref[rows, :].astype(F32), axis=-1, keepdims=True)
            sc = lax.dot_general(qi, kb, _NT, preferred_element_type=F32)
            if masked:
                sc = jnp.where(_causal_mask(tq), sc, MASK_VALUE)
            p = jnp.exp2(sc - lse_ref[rows, 0:1])
            dob = doi.astype(BF16)
            dv = dv + lax.dot_general(p.astype(BF16), dob, _TN, preferred_element_type=F32)
            dp = lax.dot_general(dob, vb, _NT, preferred_element_type=F32)
            ds = (p * (dp - delta)).astype(BF16)
            dq_acc[rows, :] += lax.dot_general(ds, kb, _NN, preferred_element_type=F32)
            dk = dk + lax.dot_general(ds, qi, _TN, preferred_element_type=F32)
            return dk, dv

        zero = jnp.zeros((tq, LANES), F32)
        carry = block(j, (zero, zero), True)
        dk, dv = lax.fori_loop(j + 1, nq, lambda i, c: block(i, c, False), carry)
        dk_ref[...] = (dk * LN2).astype(dk_ref.dtype)
        dv_ref[...] = dv.astype(dv_ref.dtype)

        @pl.when(j == nq - 1)
        def _():
            dq_ref[...] = (dq_acc[...] * LN2).astype(dq_ref.dtype)

    full = pl.BlockSpec((s, LANES), lambda h, j: (0, h))
    blk = pl.BlockSpec((tq, LANES), lambda h, j: (j, h))
    return pl.pallas_call(
        body,
        name="mla_attn_bwd",
        grid=(n_heads, nq),
        in_specs=[full, blk, blk, full, full, full],
        out_specs=[full, blk, blk],
        out_shape=[jax.ShapeDtypeStruct((s, width), t.dtype) for t in (q, k, v)],
        scratch_shapes=[pltpu.VMEM((s, LANES), F32)],
        compiler_params=_cparams("parallel", "arbitrary"),
    )(q, k, v, o, lse, do)


@jax.custom_vjp
def causal_attention(q, k, v):
    return _attn_fwd_call(q, k, v)[0]


def _causal_attention_fwd(q, k, v):
    o, lse = _attn_fwd_call(q, k, v)
    return o, (q, k, v, o, lse)


def _causal_attention_bwd(res, do):
    return tuple(_attn_bwd_call(*res, do))


causal_attention.defvjp(_causal_attention_fwd, _causal_attention_bwd)


HG_HEADS = 4
HG_CHUNK = 32
HG_REF_ROW = HG_CHUNK // 2 - 1
HG_TILE_ROWS = 256
HG_EXP_CLAMP = 80.0


def _hg_tile_masks(t):
    shift = HG_CHUNK.bit_length() - 1
    r = lax.broadcasted_iota(jnp.int32, (t, t), 0)
    c = lax.broadcasted_iota(jnp.int32, (t, t), 1)
    start = lax.shift_left(lax.shift_right_logical(r, shift), shift)
    causal = (c >= start) & (c <= r)
    return causal, c == start + HG_REF_ROW, c == start + (HG_CHUNK - 1)


def _hg_tile(q, fl, v, lb, st):
    t = q.shape[0]
    causal, ref_sel, last_sel = _hg_tile_masks(t)
    f = lb + (1.0 - lb) * jax.nn.sigmoid(fl)
    kk = 1.0 - f
    qs = q * jax.nn.sigmoid(q)
    b = select_dot(causal.astype(BF16), jnp.log(f))
    b_ref = select_dot(ref_sel.astype(BF16), b)
    b_last = select_dot(last_sel.astype(BF16), b)
    q_in = qs * jnp.exp(jnp.minimum(b - b_ref, HG_EXP_CLAMP))
    k_in = kk * jnp.exp(jnp.minimum(b_ref - b, HG_EXP_CLAMP))
    o = bdot_nn(jnp.where(causal, bdot_nt(q_in, k_in), 0.0), v)
    q_hat = split_rows(qs * jnp.exp(b), HG_CHUNK)
    k_hat = split_rows(kk * jnp.exp(b_last - b), HG_CHUNK)
    decay = split_rows(jnp.exp(b_last), HG_CHUNK)
    vs = split_rows(v, HG_CHUNK)
    first_row = lax.broadcasted_iota(jnp.int32, (HG_CHUNK, LANES), 0) == 0
    inter = []
    for c in range(t // HG_CHUNK):
        inter.append(bdot_nt(q_hat[c], st))
        st = st * jnp.sum(jnp.where(first_row, decay[c], 0.0), axis=0, keepdims=True) + bdot_tn(vs[c], k_hat[c])
    return o + join_rows(tuple(inter)), st


def _hg_head(q, fl, v, gate, lb, gn, st):
    o, st = _hg_tile(q, fl, v, lb, st)
    return _rms(o, gn) * (gate * jax.nn.sigmoid(gate)), st


HG_PARTS = 4


def _hg_part_slices(h, width):
    return [slice(p * width + h * LANES, p * width + (h + 1) * LANES) for p in range(HG_PARTS)]


def _hg_fwd_call(x, lb, gn):
    s = x.shape[0]
    width = x.shape[1] // HG_PARTS
    tr = min(HG_TILE_ROWS, s)
    nt = s // tr

    def body(x_ref, lb_ref, gn_ref, o_ref, sts_ref, st_ref):
        @pl.when(pl.program_id(0) == 0)
        def _():
            st_ref[...] = jnp.zeros_like(st_ref)

        for h in range(HG_HEADS):
            ln = slice(h * LANES, (h + 1) * LANES)
            st = st_ref[h]
            sts_ref[0, h] = st
            o, st_new = _hg_head(*(x_ref[:, sl] for sl in _hg_part_slices(h, width)), lb_ref[:, ln], gn_ref[...], st)
            o_ref[:, ln] = o.astype(o_ref.dtype)
            st_ref[h] = st_new

    const = lambda shape: pl.BlockSpec(shape, lambda j: (0, 0))
    return pl.pallas_call(
        body,
        name="hgrn2_fwd",
        grid=(nt,),
        in_specs=[pl.BlockSpec((tr, HG_PARTS * width), lambda j: (j, 0)), const((1, width)), const((1, LANES))],
        out_specs=[pl.BlockSpec((tr, width), lambda j: (j, 0)),
                   pl.BlockSpec((1, HG_HEADS, LANES, LANES), lambda j: (j, 0, 0, 0))],
        out_shape=[jax.ShapeDtypeStruct((s, width), BF16),
                   jax.ShapeDtypeStruct((nt, HG_HEADS, LANES, LANES), F32)],
        scratch_shapes=[pltpu.VMEM((HG_HEADS, LANES, LANES), F32)],
        compiler_params=_cparams("arbitrary"),
    )(x, lb, gn)


def _hg_bwd_call(x, lb, gn, sts, do):
    s = x.shape[0]
    width = x.shape[1] // HG_PARTS
    tr = min(HG_TILE_ROWS, s)
    nt = s // tr

    def body(x_ref, lb_ref, gn_ref, sts_ref, do_ref, dx_ref, dlb_ref, dgn_ref, dst_ref):
        @pl.when(pl.program_id(0) == 0)
        def _():
            dst_ref[...] = jnp.zeros_like(dst_ref)
            dlb_ref[...] = jnp.zeros_like(dlb_ref)
            dgn_ref[...] = jnp.zeros_like(dgn_ref)

        for h in range(HG_HEADS):
            ln = slice(h * LANES, (h + 1) * LANES)
            parts = _hg_part_slices(h, width)
            _, vjp = jax.vjp(_hg_head, *(x_ref[:, sl] for sl in parts), lb_ref[:, ln], gn_ref[...], sts_ref[0, h])
            cts = vjp((do_ref[:, ln].astype(F32), dst_ref[h]))
            for sl, ct in zip(parts, cts[:HG_PARTS]):
                dx_ref[:, sl] = ct.astype(dx_ref.dtype)
            dlb_ref[:, ln] += cts[HG_PARTS]
            dgn_ref[...] += cts[HG_PARTS + 1]
            dst_ref[h] = cts[HG_PARTS + 2]

    rev = lambda w: pl.BlockSpec((tr, w), lambda j: (nt - 1 - j, 0))
    const = lambda shape: pl.BlockSpec(shape, lambda j: (0, 0))
    return pl.pallas_call(
        body,
        name="hgrn2_bwd",
        grid=(nt,),
        in_specs=[rev(HG_PARTS * width), const((1, width)), const((1, LANES)),
                  pl.BlockSpec((1, HG_HEADS, LANES, LANES), lambda j: (nt - 1 - j, 0, 0, 0)), rev(width)],
        out_specs=[rev(HG_PARTS * width), const((1, width)), const((1, LANES))],
        out_shape=[jax.ShapeDtypeStruct(x.shape, BF16), jax.ShapeDtypeStruct((1, width), F32),
                   jax.ShapeDtypeStruct((1, LANES), F32)],
        scratch_shapes=[pltpu.VMEM((HG_HEADS, LANES, LANES), F32)],
        compiler_params=_cparams("arbitrary"),
    )(x, lb, gn, sts, do)


@jax.custom_vjp
def hgrn2_mixer(h, w, lb, gn):
    return _hg_fwd_call(_mm_call(h, w, False, False, name="hgrn2_proj"), lb, gn)[0]


def _hgrn2_mixer_fwd(h, w, lb, gn):
    x = _mm_call(h, w, False, False, name="hgrn2_proj")
    o, sts = _hg_fwd_call(x, lb, gn)
    return o, (h, w, x, lb, gn, sts)


def _hgrn2_mixer_bwd(res, do):
    h, w, x, lb, gn, sts = res
    dx, dlb, dgn = _hg_bwd_call(x, lb, gn, sts, do)
    dh = _mm_call(dx, w, False, True, out_dtype=h.dtype, name="hgrn2_proj_da")
    dw = _mm_call(h, dx, True, False, out_dtype=w.dtype, name="hgrn2_proj_db")
    return dh, dw, dlb, dgn


hgrn2_mixer.defvjp(_hgrn2_mixer_fwd, _hgrn2_mixer_bwd)


D_MODEL = 1024
DEPTH = 2
SSM_GROUPS, SSM_GROUP_CH, SSM_STATE = 32, 16, 64
SSM_WIDTH = SSM_GROUPS * SSM_GROUP_CH
MLA_HEADS, MLA_NOPE, MLA_ROPE, MLA_V = 8, 64, 32, 64
MLA_Q_RANK, MLA_KV_RANK = 512, 256
HG_WIDTH = HG_HEADS * LANES
X_HEADS, X_HEAD_DIM = 4, 128
X_WIDTH = X_HEADS * X_HEAD_DIM
D_FF = 2816
ROPE_THETA = 10000.0
IN_SPLITS = (SSM_WIDTH, MLA_Q_RANK, MLA_KV_RANK, MLA_ROPE, HG_WIDTH, HG_WIDTH, HG_WIDTH, HG_WIDTH, 3 * D_MODEL)
ROPE_LANE0 = MLA_NOPE
MLA_Q_SCALE = LOG2E / math.sqrt(MLA_NOPE + MLA_ROPE)
ROW_TILE = 512
MEM_ROW_TILE = 256


def _t_rms(x, g):
    return (_rms(x.astype(F32), g).astype(BF16),)


def _t_s5_act(y, u, d):
    return (jax.nn.gelu(y + d * u.astype(F32)).astype(BF16),)


def _t_glu(z):
    zo, zg = split_lanes(z.astype(F32), D_MODEL)
    return ((zo * jax.nn.sigmoid(zg)).astype(BF16),)


def _t_mla_rope(q, k, kr, c, sa, sb):
    rep = lambda t: jnp.concatenate([t] * MLA_HEADS, axis=1)
    half = MLA_ROPE // 2
    q_out = rope_lanes(q, rep(c), rep(sa), rep(sb), half) * MLA_Q_SCALE
    kr_out = rope_lanes(kr.astype(F32), c, sa, sb, half)
    return q_out.astype(BF16), (k + join_lanes((kr_out,) * MLA_HEADS)).astype(BF16)


def _t_merge(y_ssm, y_mla, y_hg, gates):
    g0, g1, g2 = split_lanes(gates.astype(F32), D_MODEL)
    mix = (jax.nn.sigmoid(g0) * y_ssm.astype(F32) + jax.nn.sigmoid(g1) * y_mla.astype(F32)
           + jax.nn.sigmoid(g2) * y_hg.astype(F32))
    return (mix.astype(BF16),)


def _t_xattn(q, kv):
    scale = 1.0 / math.sqrt(X_HEAD_DIM)
    heads = split_lanes(kv, X_HEAD_DIM)
    outs = []
    for qh, kh, vh in zip(split_lanes(q, X_HEAD_DIM), heads[:X_HEADS], heads[X_HEADS:]):
        sc = bdot_nt(qh, kh) * scale
        p = jnp.exp(sc - jnp.max(sc, axis=-1, keepdims=True))
        p = p / jnp.sum(p, axis=-1, keepdims=True)
        outs.append(bdot_nn(p, vh))
    return (join_lanes(tuple(outs)).astype(BF16),)


def _t_swiglu(gate_up):
    gt, up = split_lanes(gate_up.astype(F32), D_FF)
    return ((gt * jax.nn.sigmoid(gt) * up).astype(BF16),)


def _t_loss(x, tgt, g):
    e = _rms(x, g) - tgt
    return (jnp.broadcast_to(jnp.mean(e * e, axis=-1, keepdims=True), (x.shape[0], LANES)),)


rms_op = rowwise(_t_rms, 1, 0, ROW_TILE, "rmsnorm")
rms_mem_op = rowwise(_t_rms, 1, 0, MEM_ROW_TILE, "rmsnorm_mem")
rms_res_op = rowwise(_t_rms, 1, 0, ROW_TILE, "rmsnorm_res", passthrough=True)
s5_act_op = rowwise(_t_s5_act, 2, 0, ROW_TILE, "s5_act")
glu_op = rowwise(_t_glu, 1, 0, ROW_TILE, "glu")
mla_rope_op = rowwise(_t_mla_rope, 3, 3, ROW_TILE, "mla_rope")
merge_op = rowwise(_t_merge, 4, 0, ROW_TILE, "merge")
xattn_op = rowwise(_t_xattn, 1, 0, ROW_TILE, "xattn")
swiglu_op = rowwise(_t_swiglu, 1, 0, ROW_TILE, "swiglu")
loss_op = rowwise(_t_loss, 1, 1, ROW_TILE, "loss")


def _rope_tables(positions):
    half = MLA_ROPE // 2
    inv_freq = ROPE_THETA ** (-jnp.arange(half, dtype=F32) / half)
    ang = positions.astype(F32)[:, None] * inv_freq
    cos, sin = jnp.cos(ang), jnp.sin(ang)
    s = positions.shape[0]
    z = lambda w: jnp.zeros((s, w), F32)
    tail = LANES - ROPE_LANE0 - MLA_ROPE
    c = jnp.concatenate([jnp.ones((s, ROPE_LANE0), F32), cos, cos, z(tail)], axis=1)
    sa = jnp.concatenate([z(ROPE_LANE0), -sin, z(half), z(tail)], axis=1)
    sb = jnp.concatenate([z(ROPE_LANE0), z(half), sin, z(tail)], axis=1)
    return c, sa, sb


def _s5_operators(lam_re, lam_im, b_re, b_im, c_re, c_im, log_step):
    g, p, h = SSM_GROUPS, SSM_STATE, SSM_GROUP_CH
    lam = lax.complex(lam_re, lam_im)
    lam_bar = jnp.exp(lam * jnp.exp(log_step)[:, None])
    b_bar = ((lam_bar - 1.0) / lam)[..., None] * lax.complex(b_re, b_im)
    per = LANES // h
    nb = g // per
    eye = jnp.eye(per, dtype=F32)
    bd = lambda t: jnp.einsum("jgph,gk->jghkp", t.reshape(nb, per, p, h), eye).reshape(nb, per * h, per * p)
    cd = lambda t: jnp.einsum("jghp,gk->jgpkh", t.reshape(nb, per, h, p), eye).reshape(nb, per * p, per * h)
    a = jnp.stack([jnp.real(lam_bar).reshape(-1), jnp.imag(lam_bar).reshape(-1)])
    return a, bd(jnp.real(b_bar)), bd(jnp.imag(b_bar)), cd(c_re), cd(-c_im)


LATENT_WIDTH = 1536
_LATENT = {}
_off = 0
for _name, _w in (("u", SSM_WIDTH), ("q_lat", MLA_Q_RANK), ("kv_lat", MLA_KV_RANK), ("k_rope", LANES)):
    _LATENT[_name] = (_off, _off + _w)
    _off += _w


def _layer_matrices(w, l):
    return {**_mixer_matrices(w, l), **_tail_matrices(w, l)}


def _tail_matrices(w, l):
    return dict(x_q=w["x_w_q"][l], x_kv=w["x_w_kv"][l], x_o=w["x_w_o"][l], ffn_gu=w["ffn_w_gate_up"][l],
                ffn_d=w["ffn_w_down"][l])


def _mixer_matrices(w, l):
    w_in = w["w_in"][l]
    d, dt = w_in.shape[0], w_in.dtype
    z = lambda n: jnp.zeros((d, n), dt)
    r0 = SSM_WIDTH + MLA_Q_RANK + MLA_KV_RANK
    r1 = r0 + MLA_ROPE
    r2 = r1 + HG_PARTS * HG_WIDTH
    w_latent = jnp.concatenate([w_in[:, :r0], z(ROPE_LANE0), w_in[:, r0:r1],
                                z(LATENT_WIDTH - r0 - ROPE_LANE0 - MLA_ROPE)], axis=1)
    pad_heads = lambda t: jnp.pad(t, ((0, 0), (0, 0), (0, LANES - t.shape[2]))).reshape(t.shape[0], -1)
    uq = w["mla_w_uq"][l].reshape(MLA_Q_RANK, MLA_HEADS, MLA_NOPE + MLA_ROPE)
    ukv = w["mla_w_ukv"][l].reshape(MLA_KV_RANK, MLA_HEADS, MLA_NOPE + MLA_V)
    wo = w["mla_w_o"][l].reshape(MLA_HEADS, MLA_V, D_MODEL)
    return dict(
        w_latent=w_latent, w_hg=w_in[:, r1:r2], w_gates=w_in[:, r2:], glu=w["ssm_w_glu"][l],
        uq=pad_heads(uq), uk=pad_heads(ukv[:, :, :MLA_NOPE]), uv=pad_heads(ukv[:, :, MLA_NOPE:]),
        mla_o=jnp.pad(wo, ((0, 0), (0, LANES - MLA_V), (0, 0))).reshape(MLA_HEADS * LANES, D_MODEL),
        hg_o=w["hg_w_o"][l], w_out=w["w_out"][l])


def _layer(x, mem, tabs, m, sp, l, lower_bound):
    return _tail(_mixer(x, tabs, m, sp, l, lower_bound), mem, m, sp, l)


def _mixer(x, tabs, m, sp, l, lower_bound):
    row = lambda name: sp[name][l].reshape(1, -1)
    h, x = rms_res_op(x, row("norm_mix"))
    latent = matmul(h, m["w_latent"], BF16)
    seg = lambda name: latent[:, _LATENT[name][0]:_LATENT[name][1]]
    a, bd_r, bd_i, cd_r, cd_i = _s5_operators(*(sp[n][l] for n in (
        "ssm_lam_re", "ssm_lam_im", "ssm_b_re", "ssm_b_im", "ssm_c_re", "ssm_c_im", "ssm_log_step")))
    u = seg("u")
    y = s5_core(u, a, bd_r, bd_i, cd_r, cd_i)
    (ya,) = s5_act_op(y, u, row("ssm_d"))
    (y_ssm,) = glu_op(matmul(ya, m["glu"], BF16))
    (qn,) = rms_op(seg("q_lat"), row("mla_q_norm"))
    (kvn,) = rms_op(seg("kv_lat"), row("mla_kv_norm"))
    q, k = mla_rope_op(matmul(qn, m["uq"]), matmul(kvn, m["uk"]), seg("k_rope"), *tabs)
    o = causal_attention(q, k, matmul(kvn, m["uv"], BF16))
    y_mla = matmul(o, m["mla_o"], BF16)
    y_hg = matmul(hgrn2_mixer(h, m["w_hg"], lower_bound, row("hg_g_norm")), m["hg_o"], BF16)
    (merged,) = merge_op(y_ssm, y_mla, y_hg, matmul(h, m["w_gates"], BF16))
    return matmul_add(merged, m["w_out"], x)


def _tail(x, mem, m, sp, l):
    row = lambda name: sp[name][l].reshape(1, -1)
    hc, x = rms_res_op(x, row("norm_cross"))
    (mn,) = rms_mem_op(mem, row("norm_mem"))
    (ox,) = xattn_op(matmul(hc, m["x_q"], BF16), matmul(mn, m["x_kv"]))
    x = matmul_add(ox, m["x_o"], x)
    hf, x = rms_res_op(x, row("norm_ffn"))
    (act,) = swiglu_op(matmul(hf, m["ffn_gu"], BF16))
    return matmul_add(act, m["ffn_d"], x)


def _lower_bounds(hg_lb):
    lb_p = jax.nn.softmax(hg_lb, axis=0)
    return jnp.cumsum(lb_p, axis=0) - lb_p[0:1]


def _final_loss(target, x, norm_final):
    (row_loss,) = loss_op(x, target, norm_final.reshape(1, -1))
    return 0.5 * jnp.sum(row_loss[:, 0])


def _local_loss(x, mem, positions, target, w, sp):
    tabs = _rope_tables(positions)
    lower = _lower_bounds(sp["hg_lb"])
    for l in range(DEPTH):
        x = _layer(x, mem, tabs, _layer_matrices(w, l), sp, l, lower[l].reshape(1, -1))
    return _final_loss(target, x, sp["norm_final"])


N_DEV = 8
N_CHIPS = 4
COMM_LANES = 512
MESH_ID = pl.DeviceIdType.MESH
_ANY = pl.BlockSpec(memory_space=pl.ANY)
_OTHER_CHIPS = ((1, 0), (0, 1), (1, 1))


def _place():
    return lax.axis_index("x"), lax.axis_index("y"), lax.axis_index("c")


def _all_gather_call(blocks, name):
    n = len(blocks)

    def body(*refs):
        x_refs, out_refs = refs[:n], refs[n:2 * n]
        send_sems, recv_sems, local_sems = refs[2 * n:]
        x, y, c = _place()
        me, sibling = (x, y, c), (x, y, 1 - c)
        chips = [(x ^ fx, y ^ fy) for fx, fy in _OTHER_CHIPS]

        def slot(i, px, py, pc):
            return out_refs[i].at[4 * px + 2 * py + pc]

        def copy(i, k, blk, to, src=None):
            return pltpu.make_async_remote_copy(
                src_ref=slot(i, *blk) if src is None else src, dst_ref=slot(i, *blk),
                send_sem=send_sems.at[i, k], recv_sem=recv_sems.at[i, k], device_id=to, device_id_type=MESH_ID)

        mine = [pltpu.make_async_copy(x_refs[i], slot(i, *me), local_sems.at[i]) for i in range(n)]
        first = []
        for i in range(n):
            first.append(copy(i, 0, me, sibling, src=x_refs[i]))
            first += [copy(i, 1 + j, me, (*chip, c), src=x_refs[i]) for j, chip in enumerate(chips)]
        for cp in mine + first:
            cp.start()
        passed = []
        for j, chip in enumerate(chips):
            for i in range(n):
                copy(i, 1 + j, (*chip, c), me).wait_recv()
                passed.append(copy(i, 4 + j, (*chip, c), sibling))
                passed[-1].start()
        for i in range(n):
            copy(i, 0, sibling, me).wait_recv()
            for j, chip in enumerate(chips):
                copy(i, 4 + j, (*chip, 1 - c), me).wait_recv()
        for cp in first + passed:
            cp.wait_send()
        for cp in mine:
            cp.wait()

    return pl.pallas_call(
        body,
        name=name,
        out_shape=[jax.ShapeDtypeStruct((N_DEV,) + b.shape, b.dtype) for b in blocks],
        in_specs=[_ANY] * n,
        out_specs=[_ANY] * n,
        scratch_shapes=[pltpu.SemaphoreType.DMA((n, 7)), pltpu.SemaphoreType.DMA((n, 7)), pltpu.SemaphoreType.DMA((n,))],
    )(*blocks)


def _pair_exchange_call(gs, name):
    n = len(gs)

    def body(*refs):
        g_refs, got_refs = refs[:n], refs[n:2 * n]
        send_sems, recv_sems = refs[2 * n:]
        x, y, c = _place()
        sends = [pltpu.make_async_remote_copy(
            src_ref=g_refs[i].at[2 * p + (1 - c)], dst_ref=got_refs[i].at[p],
            send_sem=send_sems.at[i, p], recv_sem=recv_sems.at[i, p], device_id=(x, y, 1 - c), device_id_type=MESH_ID)
            for i in range(n) for p in range(N_CHIPS)]
        for cp in sends:
            cp.start()
        for cp in sends:
            cp.wait_recv()
        for cp in sends:
            cp.wait_send()

    return pl.pallas_call(
        body,
        name=name,
        out_shape=[jax.ShapeDtypeStruct((N_CHIPS,) + g.shape[1:], g.dtype) for g in gs],
        in_specs=[_ANY] * n,
        out_specs=[_ANY] * n,
        scratch_shapes=[pltpu.SemaphoreType.DMA((n, N_CHIPS))] * 2,
    )(*gs)


def _chip_exchange_call(parts, name):
    n = len(parts)

    def body(*refs):
        p_refs, got_refs = refs[:n], refs[n:2 * n]
        send_sems, recv_sems = refs[2 * n:]
        x, y, c = _place()
        sends = []
        for i in range(n):
            for k, (fx, fy) in enumerate(_OTHER_CHIPS):
                px, py = x ^ fx, y ^ fy
                sends.append(pltpu.make_async_remote_copy(
                    src_ref=p_refs[i].at[2 * px + py], dst_ref=got_refs[i].at[k],
                    send_sem=send_sems.at[i, k], recv_sem=recv_sems.at[i, k], device_id=(px, py, c), device_id_type=MESH_ID))
        for cp in sends:
            cp.start()
        for cp in sends:
            cp.wait_recv()
        for cp in sends:
            cp.wait_send()

    return pl.pallas_call(
        body,
        name=name,
        out_shape=[jax.ShapeDtypeStruct((3,) + p.shape[1:], p.dtype) for p in parts],
        in_specs=[_ANY] * n,
        out_specs=[_ANY] * n,
        scratch_shapes=[pltpu.SemaphoreType.DMA((n, 3))] * 2,
    )(*parts)


def _rows_cols(shape):
    return math.prod(shape[:-1]), shape[-1]


def _pair_sum_call(g, got, c_idx, name):
    rows, cols = _rows_cols(got.shape[1:])
    tr = _pick_tile(rows, (512, 256, 128, 64, 32, 16))

    def body(c_ref, a_ref, b_ref, o_ref):
        o_ref[...] = (a_ref[...].astype(F32) + b_ref[...].astype(F32)).astype(o_ref.dtype)

    spec = pl.BlockSpec((1, tr, cols), lambda p, i, c_ref: (p, i, 0))
    out = pl.pallas_call(
        body,
        name=name,
        grid_spec=pltpu.PrefetchScalarGridSpec(
            num_scalar_prefetch=1, grid=(N_CHIPS, rows // tr),
            in_specs=[pl.BlockSpec((1, tr, cols), lambda p, i, c_ref: (2 * p + c_ref[0], i, 0)), spec],
            out_specs=spec),
        out_shape=jax.ShapeDtypeStruct((N_CHIPS, rows, cols), got.dtype),
        compiler_params=_cparams("parallel", "parallel"),
    )(c_idx, g.reshape(N_DEV, rows, cols), got.reshape(N_CHIPS, rows, cols))
    return out.reshape(got.shape)


def _chip_sum_call(part, got, chip_idx, name):
    rows, cols = _rows_cols(got.shape[1:])
    tr = _pick_tile(rows, (512, 256, 128, 64, 32, 16))

    def body(p_ref, a_ref, b_ref, o_ref):
        acc = a_ref[0].astype(F32)
        for k in range(3):
            acc = acc + b_ref[k].astype(F32)
        o_ref[...] = acc

    out = pl.pallas_call(
        body,
        name=name,
        grid_spec=pltpu.PrefetchScalarGridSpec(
            num_scalar_prefetch=1, grid=(rows // tr,),
            in_specs=[pl.BlockSpec((1, tr, cols), lambda i, p_ref: (p_ref[0], i, 0)),
                      pl.BlockSpec((3, tr, cols), lambda i, p_ref: (0, i, 0))],
            out_specs=pl.BlockSpec((tr, cols), lambda i, p_ref: (i, 0))),
        out_shape=jax.ShapeDtypeStruct((rows, cols), F32),
        compiler_params=_cparams("parallel"),
    )(chip_idx, part.reshape(N_CHIPS, rows, cols), got.reshape(3, rows, cols))
    return out.reshape(got.shape[1:])


def _reduce_scatter(gs, name):
    x, y, c = _place()
    c_idx = c.astype(jnp.int32).reshape(1)
    chip_idx = (2 * x + y).astype(jnp.int32).reshape(1)
    gots = _pair_exchange_call(gs, name + "_pair")
    parts = [_pair_sum_call(g, got, c_idx, name + "_pair_sum") for g, got in zip(gs, gots)]
    gots = _chip_exchange_call(parts, name + "_chip")
    return [_chip_sum_call(p, got, chip_idx, name + "_chip_sum") for p, got in zip(parts, gots)]


_HBM = pl.BlockSpec(memory_space=pltpu.HBM)
_SEM = pl.BlockSpec(memory_space=pltpu.SEMAPHORE)
_SIDE_EFFECT = pltpu.SideEffectType.DATAFLOW_SIDE_EFFECTING
N_PEERS = N_DEV - 1


def _peer(k):
    x, y, c = _place()
    px, py, pc = x ^ ((k >> 2) & 1), y ^ ((k >> 1) & 1), c ^ (k & 1)
    return (px, py, pc), 4 * px + 2 * py + pc


def _exchange_copy(src_ref, land_ref, send_sems, recv_sems, i, k, scatter, receiving):
    x, y, c = _place()
    me = 4 * x + 2 * y + c
    peer, peer_idx = _peer(k)
    sem = i * N_PEERS + k - 1
    return pltpu.make_async_remote_copy(
        src_ref=src_ref.at[peer_idx] if scatter else src_ref, dst_ref=land_ref.at[peer_idx if receiving else me],
        send_sem=send_sems.at[sem], recv_sem=recv_sems.at[sem], device_id=peer, device_id_type=MESH_ID)


def _exchange_start_call(srcs, after, scatter, name):
    n = len(srcs)
    slot_shapes = [s.shape[1:] if scatter else s.shape for s in srcs]

    def body(*refs):
        src_refs, land_refs = refs[:n], refs[n:2 * n]
        send_sems, recv_sems = refs[2 * n + 1], refs[2 * n + 2]
        token = refs[-1]
        for i in range(n):
            for k in range(1, N_DEV):
                _exchange_copy(src_refs[i], land_refs[i], send_sems, recv_sems, i, k, scatter, False).start()
        token[...] = jnp.zeros_like(token)

    lands = [pltpu.with_memory_space_constraint(lax.empty((N_DEV,) + shp, s.dtype), pltpu.HBM)
             for shp, s in zip(slot_shapes, srcs)]
    out = pl.pallas_call(
        body,
        name=name,
        out_shape=([pltpu.SemaphoreType.DMA((n * N_PEERS,)), pltpu.SemaphoreType.DMA((n * N_PEERS,))]
                   + [pltpu.HBM(s.shape, s.dtype) for s in srcs] + [pltpu.HBM(l.shape, l.dtype) for l in lands]
                   + [jax.ShapeDtypeStruct((SUBLANES, LANES), F32)]),
        in_specs=[_HBM] * (2 * n) + [pl.BlockSpec(memory_space=pl.ANY)],
        out_specs=[_SEM, _SEM] + [_HBM] * (2 * n) + [pl.BlockSpec(memory_space=pltpu.VMEM)],
        input_output_aliases={j: 2 + j for j in range(2 * n)},
        compiler_params=pltpu.CompilerParams(has_side_effects=_SIDE_EFFECT),
    )(*[pltpu.with_memory_space_constraint(s, pltpu.HBM) for s in srcs], *lands, after)
    return out[0], out[1], list(out[2:2 + n]), list(out[2 + n:2 + 2 * n]), out[-1]


def _exchange_wait_call(started, after, scatter, name):
    send_sems, recv_sems, srcs, lands, _ = started
    n = len(srcs)

    def body(*refs):
        src_refs, land_refs = refs[:n], refs[n:2 * n]
        send_s, recv_s = refs[2 * n], refs[2 * n + 1]
        for i in range(n):
            for k in range(1, N_DEV):
                cp = _exchange_copy(src_refs[i], land_refs[i], send_s, recv_s, i, k, scatter, True)
                cp.wait_send()
                cp.wait_recv()

    out = pl.pallas_call(
        body,
        name=name,
        out_shape=[pltpu.HBM(s.shape, s.dtype) for s in srcs] + [pltpu.HBM(l.shape, l.dtype) for l in lands],
        in_specs=[_HBM] * (2 * n) + [_SEM, _SEM, pl.BlockSpec(memory_space=pl.ANY)],
        out_specs=[_HBM] * (2 * n),
        input_output_aliases={j: j for j in range(2 * n)},
        compiler_params=pltpu.CompilerParams(has_side_effects=_SIDE_EFFECT),
    )(*srcs, *lands, send_sems, recv_sems, after)
    return list(out[n:])


def _own_slot(land, own):
    x, y, c = _place()
    return lax.dynamic_update_index_in_dim(land, own, 4 * x + 2 * y + c, 0)


def _slot_sum_call(land, name):
    rows, cols = _rows_cols(land.shape[1:])
    tr = _pick_tile(rows, (256, 128, 64, 32, 16))

    def body(land_ref, o_ref):
        acc = land_ref[0].astype(F32)
        for s in range(1, N_DEV):
            acc = acc + land_ref[s].astype(F32)
        o_ref[...] = acc

    out = pl.pallas_call(
        body,
        name=name,
        grid=(rows // tr,),
        in_specs=[pl.BlockSpec((N_DEV, tr, cols), lambda i: (0, i, 0))],
        out_specs=pl.BlockSpec((tr, cols), lambda i: (i, 0)),
        out_shape=jax.ShapeDtypeStruct((rows, cols), F32),
        compiler_params=_cparams("parallel"),
    )(land.reshape(N_DEV, rows, cols))
    return out.reshape(land.shape[1:])


SMALL_BLOCK_ROWS = 16


def _pack_small(parts):
    flat = jnp.concatenate([p.reshape(-1) for p in parts])
    chunk = N_DEV * SMALL_BLOCK_ROWS * COMM_LANES
    flat = jnp.pad(flat, (0, (-flat.shape[0]) % chunk))
    return flat.reshape(N_DEV, -1, COMM_LANES)


def _unpack_small(buf, shapes):
    flat = buf.reshape(-1)
    out, off = [], 0
    for shp in shapes:
        n = math.prod(shp)
        out.append(flat[off:off + n].reshape(shp))
        off += n
    return out


SHARDED = dict(w_in=2, ssm_w_glu=2, mla_w_uq=2, mla_w_ukv=2, mla_w_o=2, hg_w_o=2, w_out=1, x_w_q=1, x_w_kv=1,
               x_w_o=2, ffn_w_gate_up=2, ffn_w_down=1)
REPLICATED = ("norm_mix", "ssm_lam_re", "ssm_lam_im", "ssm_b_re", "ssm_b_im", "ssm_c_re", "ssm_c_im", "ssm_d",
              "ssm_log_step", "mla_q_norm", "mla_kv_norm", "hg_lb", "hg_g_norm", "norm_cross", "norm_mem", "norm_ffn",
              "norm_final")


def _join_shards(stacked, axis):
    n, l, a, b = stacked.shape
    if axis == 1:
        return stacked.transpose(1, 0, 2, 3).reshape(l, n * a, b)
    return stacked.transpose(1, 2, 0, 3).reshape(l, a, n * b)


def _split_shards(full, axis):
    l, a, b = full.shape
    if axis == 1:
        return full.reshape(l, N_DEV, a // N_DEV, b).transpose(1, 0, 2, 3)
    return full.reshape(l, a, N_DEV, b // N_DEV).transpose(2, 0, 1, 3)


MIXER_WEIGHTS = ("w_in", "ssm_w_glu", "mla_w_uq", "mla_w_ukv", "mla_w_o", "hg_w_o", "w_out")
TAIL_WEIGHTS = ("x_w_q", "x_w_kv", "x_w_o", "ffn_w_gate_up", "ffn_w_down")


def _mixer_fn(l, tabs):
    def f(x, full, small, lower):
        m = _mixer_matrices(dict(zip(MIXER_WEIGHTS, full)), 0)
        return _mixer(x, tabs, m, dict(zip(REPLICATED, small)), l, lower[l].reshape(1, -1))
    return f


def _tail_fn(l, mem):
    def f(x, full, small):
        return _tail(x, mem, _tail_matrices(dict(zip(TAIL_WEIGHTS, full)), 0), dict(zip(REPLICATED, small)), l)
    return f


ADAM_LR, ADAM_B1, ADAM_B2, ADAM_EPS, ADAM_WD, ADAM_STEP = 0.001, 0.9, 0.999, 1e-08, 0.01, 10


def _adamw_update(w, g, m, v):
    m_new = ADAM_B1 * m + (1.0 - ADAM_B1) * g
    v_new = ADAM_B2 * v + (1.0 - ADAM_B2) * jnp.square(g)
    m_hat = m_new / (1.0 - ADAM_B1 ** ADAM_STEP)
    v_hat = v_new / (1.0 - ADAM_B2 ** ADAM_STEP)
    return -ADAM_LR * (m_hat / (jnp.sqrt(v_hat) + ADAM_EPS) + ADAM_WD * w), m_new, v_new


def _adamw_stacked_call(w, g, m, v, name):
    depth, rows, cols = w.shape
    tr = _pick_tile(rows, (512, 256, 128, 64, 32, 16, 8))

    def body(w_ref, g_ref, m_ref, v_ref, d_ref, nm_ref, nv_ref):
        d_ref[...], nm_ref[...], nv_ref[...] = _adamw_update(w_ref[...], g_ref[...], m_ref[...], v_ref[...])

    spec = pl.BlockSpec((None, tr, cols), lambda l, i: (l, i, 0))
    return tuple(pl.pallas_call(
        body, name=name, grid=(depth, rows // tr), in_specs=[spec] * 4, out_specs=[spec] * 3,
        out_shape=[jax.ShapeDtypeStruct(w.shape, F32)] * 3, compiler_params=_cparams("parallel", "parallel"),
    )(w, g, m, v))


def _adamw_call(w, g, m, v, name):
    shape = w.shape
    if len(shape) == 3:
        return _adamw_stacked_call(w, g, m, v, name)
    cols = shape[-1]
    rows = math.prod(shape[:-1]) if len(shape) > 1 else 1
    tr = _pick_tile(rows, (512, 256, 128, 64, 32, 16, 8))

    def body(w_ref, g_ref, m_ref, v_ref, d_ref, nm_ref, nv_ref):
        d_ref[...], nm_ref[...], nv_ref[...] = _adamw_update(w_ref[...], g_ref[...], m_ref[...], v_ref[...])

    spec = pl.BlockSpec((tr, cols), lambda i: (i, 0))
    outs = pl.pallas_call(
        body, name=name, grid=(rows // tr,), in_specs=[spec] * 4, out_specs=[spec] * 3,
        out_shape=[jax.ShapeDtypeStruct((rows, cols), F32)] * 3, compiler_params=_cparams("parallel"),
    )(*(t.reshape(rows, cols) for t in (w, g, m, v)))
    return tuple(o.reshape(shape) for o in outs)


WEIGHTS = ("norm_mix", "w_in", "ssm_lam_re", "ssm_lam_im", "ssm_b_re", "ssm_b_im", "ssm_c_re", "ssm_c_im", "ssm_d",
           "ssm_log_step", "ssm_w_glu", "mla_q_norm", "mla_kv_norm", "mla_w_uq", "mla_w_ukv", "mla_w_o", "hg_lb",
           "hg_g_norm", "hg_w_o", "w_out", "norm_cross", "norm_mem", "x_w_q", "x_w_kv", "x_w_o", "norm_ffn",
           "ffn_w_gate_up", "ffn_w_down", "norm_final")


def kernel(x, mem, positions, norm_mix, w_in, ssm_lam_re, ssm_lam_im, ssm_b_re, ssm_b_im, ssm_c_re, ssm_c_im, ssm_d, ssm_log_step, ssm_w_glu, mla_q_norm, mla_kv_norm, mla_w_uq, mla_w_ukv, mla_w_o, hg_lb, hg_g_norm, hg_w_o, w_out, norm_cross, norm_mem, x_w_q, x_w_kv, x_w_o, norm_ffn, ffn_w_gate_up, ffn_w_down, norm_final, loss_target, m_norm_mix, m_w_in, m_ssm_lam_re, m_ssm_lam_im, m_ssm_b_re, m_ssm_b_im, m_ssm_c_re, m_ssm_c_im, m_ssm_d, m_ssm_log_step, m_ssm_w_glu, m_mla_q_norm, m_mla_kv_norm, m_mla_w_uq, m_mla_w_ukv, m_mla_w_o, m_hg_lb, m_hg_g_norm, m_hg_w_o, m_w_out, m_norm_cross, m_norm_mem, m_x_w_q, m_x_w_kv, m_x_w_o, m_norm_ffn, m_ffn_w_gate_up, m_ffn_w_down, m_norm_final, v_norm_mix, v_w_in, v_ssm_lam_re, v_ssm_lam_im, v_ssm_b_re, v_ssm_b_im, v_ssm_c_re, v_ssm_c_im, v_ssm_d, v_ssm_log_step, v_ssm_w_glu, v_mla_q_norm, v_mla_kv_norm, v_mla_w_uq, v_mla_w_ukv, v_mla_w_o, v_hg_lb, v_hg_g_norm, v_hg_w_o, v_w_out, v_norm_cross, v_norm_mem, v_x_w_q, v_x_w_kv, v_x_w_o, v_norm_ffn, v_ffn_w_gate_up, v_ffn_w_down, v_norm_final):
    given = dict(locals())
    weights = {n: given[n] for n in WEIGHTS}
    small = tuple(weights[n] for n in REPLICATED)
    layer1 = MIXER_WEIGHTS + TAIL_WEIGHTS
    shards = lambda names, l: [weights[n][l:l + 1].astype(BF16) for n in names]
    join = lambda names, stacked: tuple(_join_shards(p, SHARDED[n]) for n, p in zip(names, stacked))
    split = lambda names, cts: [_split_shards(ct, SHARDED[n]) for n, ct in zip(names, cts)]
    landed = lambda names, lands, own: join(names, [_own_slot(land, o) for land, o in zip(lands, own)])
    xs, tabs = x[0], _rope_tables(positions[0])
    lower, vjp_lower = jax.vjp(_lower_bounds, hg_lb)
    me = 4 * lax.axis_index("x") + 2 * lax.axis_index("y") + lax.axis_index("c")

    got_m0 = _all_gather_call(shards(MIXER_WEIGHTS, 0), "weights_all_gather_m0")
    own_t0, own_l1 = shards(TAIL_WEIGHTS, 0), shards(layer1, 1)
    gather_t0 = _exchange_start_call(own_t0, got_m0[0], False, "weights_gather_start_t0")
    gather_l1 = _exchange_start_call(own_l1, gather_t0[4], False, "weights_gather_start_l1")
    xs = xs + gather_l1[4][0, 0]
    xa0, vjp_m0 = jax.vjp(_mixer_fn(0, tabs), xs, join(MIXER_WEIGHTS, got_m0), small, lower)
    full_t0 = landed(TAIL_WEIGHTS, _exchange_wait_call(gather_t0, xa0, False, "weights_gather_wait_t0"), own_t0)
    x1, vjp_t0 = jax.vjp(_tail_fn(0, mem[0]), xa0, full_t0, small)
    full_l1 = landed(layer1, _exchange_wait_call(gather_l1, x1, False, "weights_gather_wait_l1"), own_l1)
    xa1, vjp_m1 = jax.vjp(_mixer_fn(1, tabs), x1, full_l1[:len(MIXER_WEIGHTS)], small, lower)
    x2, vjp_t1 = jax.vjp(_tail_fn(1, mem[0]), xa1, full_l1[len(MIXER_WEIGHTS):], small)
    loss_local, vjp_loss = jax.vjp(functools.partial(_final_loss, loss_target[0]), x2, norm_final)

    def scatter_start(names, cts, dx, tag):
        gs = split(names, cts)
        started = _exchange_start_call(gs, dx, True, "grads_scatter_start_" + tag)
        return (started, gs), dx + started[4][0, 0]

    def scatter_finish(pending, after, tag):
        started, gs = pending
        lands = _exchange_wait_call(started, after, True, "grads_scatter_wait_" + tag)
        return [_slot_sum_call(_own_slot(land, lax.dynamic_index_in_dim(g, me, 0, keepdims=False)), "grads_slot_sum_" + tag)
                for land, g in zip(lands, gs)]

    dx2, d_norm_final = vjp_loss(jnp.ones((), F32))
    dxa1, dfull_t1, dsmall_t1 = vjp_t1(dx2)
    dx1, dfull_m1, dsmall_m1, dlower1 = vjp_m1(dxa1)
    pend_l1, dx1 = scatter_start(layer1, dfull_m1 + dfull_t1, dx1, "l1")
    dxa0, dfull_t0, dsmall_t0 = vjp_t0(dx1)
    pend_t0, dxa0 = scatter_start(TAIL_WEIGHTS, dfull_t0, dxa0, "t0")
    gx, dfull_m0, dsmall_m0, dlower0 = vjp_m0(dxa0)
    by_layer = {
        0: dict(zip(MIXER_WEIGHTS + TAIL_WEIGHTS,
                    _reduce_scatter(split(MIXER_WEIGHTS, dfull_m0), "grads_reduce_scatter_m0") + scatter_finish(pend_t0, gx, "t0"))),
        1: dict(zip(layer1, scatter_finish(pend_l1, gx, "l1")))}
    grads = {n: jnp.concatenate([by_layer[0][n], by_layer[1][n]], axis=0) for n in SHARDED}

    d_small = dict(zip(REPLICATED, (a + b + c + d for a, b, c, d in zip(dsmall_m0, dsmall_t0, dsmall_m1, dsmall_t1))))
    d_small["norm_final"] = d_small["norm_final"] + d_norm_final
    d_small["hg_lb"] = d_small["hg_lb"] + vjp_lower(dlower0 + dlower1)[0]
    shapes = [d_small[n].shape for n in REPLICATED]
    (mine,) = _reduce_scatter([_pack_small([d_small[n] for n in REPLICATED])], "small_reduce_scatter")
    (total,) = _all_gather_call([mine], "small_all_gather")
    grads.update(zip(REPLICATED, _unpack_small(total, shapes)))

    loss = lax.psum(loss_local, ("x", "y", "c"))
    steps = {n: _adamw_call(weights[n], grads[n], given["m_" + n], given["v_" + n], "adamw_" + n) for n in WEIGHTS}
    return (loss, gx[None], *[grads[n] for n in WEIGHTS], *[steps[n][0] for n in WEIGHTS],
            *[steps[n][1] for n in WEIGHTS], *[steps[n][2] for n in WEIGHTS])
```

```python
import functools
import math

import jax
import jax.numpy as jnp
from jax import lax
from jax.experimental import pallas as pl
from jax.experimental.pallas import tpu as pltpu

F32 = jnp.float32
BF16 = jnp.bfloat16

VMEM_LIMIT_BYTES = 48 * 1024 * 1024
LANES = 128
SUBLANES = 8


def _cparams(*sem):
    return pltpu.CompilerParams(dimension_semantics=sem, vmem_limit_bytes=VMEM_LIMIT_BYTES)


def _pick_tile(n, cands):
    for c in cands:
        if n % c == 0:
            return c
    return n


MM_VMEM_BUDGET = 38 * 1024 * 1024
MM_STEP_US = 0.35
HBM_BYTES_PER_US = 3.0e6
VREG_RMW_PER_US = 1.5e3


def _divisor_tiles(dim, cands):
    out = [t for t in cands if dim % t == 0]
    return out or [dim]


def _mm_tiles(m, n, k, sa, sb, so):
    tms = _divisor_tiles(m, (1408, 1024, 512, 256, 128, 64, 32, 16, 8))[:2]
    tns = _divisor_tiles(n, (2048, 1536, 1408, 1024, 768, 512, 384, 256, 128))
    tks = [k // d for d in (1, 2, 4, 8, 13, 16, 26, 32, 52) if k % d == 0 and (k // d) % LANES == 0] or [k]
    best = None
    for tk in tks:
        nk = k // tk
        for tm in tms:
            for tn in tns:
                vmem = 2 * (tm * tk * sa + tk * tn * sb + tm * tn * so) + (tm * tn * 4 if nk > 1 else 0)
                if vmem > MM_VMEM_BUDGET:
                    continue
                steps = (m // tm) * (n // tn) * nk
                a_reads = m * k * sa * (n // tn if nk > 1 else 1)
                b_reads = k * n * sb * (m // tm if (nk > 1 or n // tn > 1) else 1)
                cost = (steps * MM_STEP_US + (a_reads + b_reads) / HBM_BYTES_PER_US
                        + (m * n * nk / 1024 / VREG_RMW_PER_US if nk > 1 else 0.0))
                if best is None or cost < best[0]:
                    best = (cost, tm, tn, tk)
    assert best is not None, (m, n, k)
    return best[1:]


def _mm_tiles_cached_t(m, n, k, sa, sb, so):
    for tm in _divisor_tiles(m, (1024, 512)):
        if m % tm:
            break
        for tn in _divisor_tiles(n, (1024, 512, 384, 256, 128)):
            if 2 * (k * tm * sa + k * tn * sb + tm * tn * so) + tm * k * 2 <= MM_VMEM_BUDGET:
                return tm, tn
    return None


def _mm_tn_cached_call(a, b, tiles, out_dtype, name):
    k, m = a.shape
    n = b.shape[1]
    tm, tn = tiles

    def body(a_ref, b_ref, o_ref, at_ref):
        @pl.when(pl.program_id(1) == 0)
        def _():
            at_ref[...] = a_ref[...].astype(BF16).T

        o_ref[...] = lax.dot_general(at_ref[...], b_ref[...].astype(BF16), _NN_DIMS,
                                     preferred_element_type=F32).astype(out_dtype)

    return pl.pallas_call(
        body,
        name=name,
        grid=(m // tm, n // tn),
        in_specs=[pl.BlockSpec((k, tm), lambda i, j: (0, i)), pl.BlockSpec((k, tn), lambda i, j: (0, j))],
        out_specs=pl.BlockSpec((tm, tn), lambda i, j: (i, j)),
        out_shape=jax.ShapeDtypeStruct((m, n), out_dtype),
        scratch_shapes=[pltpu.VMEM((tm, k), BF16)],
        compiler_params=_cparams("parallel", "arbitrary"),
    )(a, b)


_NN_DIMS = (((1,), (0,)), ((), ()))


def _mm_call(a, b, ta, tb, add=None, out_dtype=F32, name="mm"):
    m, k = (a.shape[1], a.shape[0]) if ta else a.shape
    k2, n = (b.shape[1], b.shape[0]) if tb else b.shape
    assert k == k2, (a.shape, b.shape, ta, tb)
    sizes = (a.dtype.itemsize, b.dtype.itemsize, jnp.dtype(out_dtype).itemsize + (add.dtype.itemsize if add is not None else 0))
    if ta:
        tiles = _mm_tiles_cached_t(m, n, k, *sizes)
        if tiles is not None:
            return _mm_tn_cached_call(a, b, tiles, out_dtype, name)
    tm, tn, tk = _mm_tiles(m, n, k, *sizes)
    nk = k // tk
    a_spec = pl.BlockSpec((tk, tm), lambda i, j, kk: (kk, i)) if ta else pl.BlockSpec((tm, tk), lambda i, j, kk: (i, kk))
    b_spec = pl.BlockSpec((tn, tk), lambda i, j, kk: (j, kk)) if tb else pl.BlockSpec((tk, tn), lambda i, j, kk: (kk, j))
    o_spec = pl.BlockSpec((tm, tn), lambda i, j, kk: (i, j))
    dn = (((0 if ta else 1,), (1 if tb else 0,)), ((), ()))
    has_add = add is not None

    def body(*refs):
        a_ref, b_ref = refs[0], refs[1]
        c_ref = refs[2] if has_add else None
        o_ref = refs[3] if has_add else refs[2]
        p = lax.dot_general(a_ref[...].astype(BF16), b_ref[...].astype(BF16), dn, preferred_element_type=F32)

        def finish(r):
            if has_add:
                r = r + c_ref[...].astype(F32)
            o_ref[...] = r.astype(out_dtype)

        if nk == 1:
            finish(p)
        else:
            acc_ref = refs[-1]
            kk = pl.program_id(2)

            @pl.when(kk == 0)
            def _():
                acc_ref[...] = p

            @pl.when(kk > 0)
            def _():
                acc_ref[...] += p

            @pl.when(kk == nk - 1)
            def _():
                finish(acc_ref[...])

    in_specs = [a_spec, b_spec] + ([o_spec] if has_add else [])
    args = (a, b) + ((add,) if has_add else ())
    return pl.pallas_call(
        body,
        name=name,
        grid=(m // tm, n // tn, nk),
        in_specs=in_specs,
        out_specs=o_spec,
        out_shape=jax.ShapeDtypeStruct((m, n), out_dtype),
        scratch_shapes=[] if nk == 1 else [pltpu.VMEM((tm, tn), F32)],
        compiler_params=_cparams("parallel", "parallel", "arbitrary"),
    )(*args)


@functools.partial(jax.custom_vjp, nondiff_argnums=(2,))
def matmul(a, b, out_dtype=F32):
    return _mm_call(a, b, False, False, out_dtype=out_dtype, name="mm_fwd")


def _matmul_fwd(a, b, out_dtype):
    return matmul(a, b, out_dtype), (a, b)


def _matmul_bwd(out_dtype, res, g):
    a, b = res
    da = _mm_call(g, b, False, True, out_dtype=a.dtype, name="mm_da")
    db = _mm_call(a, g, True, False, out_dtype=b.dtype, name="mm_db")
    return da, db


matmul.defvjp(_matmul_fwd, _matmul_bwd)


@jax.custom_vjp
def matmul_add(a, b, c):
    return _mm_call(a, b, False, False, add=c, name="mm_add_fwd")


def _matmul_add_fwd(a, b, c):
    return _mm_call(a, b, False, False, add=c, name="mm_add_fwd"), (a, b)


def _matmul_add_bwd(res, g):
    a, b = res
    da = _mm_call(g, b, False, True, out_dtype=a.dtype, name="mm_da")
    db = _mm_call(a, g, True, False, out_dtype=b.dtype, name="mm_db")
    return da, db, g


matmul_add.defvjp(_matmul_add_fwd, _matmul_add_bwd)


def rowwise(f, n_rows, n_aux, tile, name, passthrough=False):
    def specs(arrs, tiled):
        out = []
        for x in arrs:
            if tiled:
                out.append(pl.BlockSpec((tile, x.shape[1]), lambda i: (i, 0)))
            else:
                out.append(pl.BlockSpec(x.shape, lambda i: (0, 0)))
        return out

    def tile_structs(args):
        rows_aux, params = args[: n_rows + n_aux], args[n_rows + n_aux:]
        return [jax.ShapeDtypeStruct((tile, x.shape[1]), x.dtype) for x in rows_aux] + [
            jax.ShapeDtypeStruct(p.shape, p.dtype) for p in params]

    def fwd_call(*args):
        s = args[0].shape[0]
        outs = jax.eval_shape(f, *tile_structs(args))
        n_in = len(args)

        def body(*refs):
            vals = [r[...] for r in refs[:n_in]]
            res = f(*vals)
            for o_ref, r in zip(refs[n_in:], res):
                o_ref[...] = r.astype(o_ref.dtype)

        return pl.pallas_call(
            body,
            name=name + "_fwd",
            grid=(s // tile,),
            in_specs=specs(args[: n_rows + n_aux], True) + specs(args[n_rows + n_aux:], False),
            out_specs=[pl.BlockSpec((tile, o.shape[1]), lambda i: (i, 0)) for o in outs],
            out_shape=[jax.ShapeDtypeStruct((s, o.shape[1]), o.dtype) for o in outs],
            compiler_params=_cparams("parallel"),
        )(*args)

    def bwd_call(args, gs):
        s = args[0].shape[0]
        rows, aux, params = args[:n_rows], args[n_rows:n_rows + n_aux], args[n_rows + n_aux:]
        n_in, n_g, n_p = len(args), len(gs), len(params)
        n_gf = n_g - 1 if passthrough else n_g

        def body(*refs):
            vals = [r[...] for r in refs[:n_in]]
            gvals = tuple(r[...] for r in refs[n_in:n_in + n_gf])
            out_refs = refs[n_in + n_g:]
            auxv = vals[n_rows:n_rows + n_aux]

            def g_(*rp):
                return tuple(f(*rp[:n_rows], *auxv, *rp[n_rows:]))

            _, vjp = jax.vjp(g_, *vals[:n_rows], *vals[n_rows + n_aux:])
            cts = list(vjp(gvals))
            if passthrough:
                cts[0] = cts[0] + refs[n_in + n_gf][...]
            for o_ref, ct in zip(out_refs[:n_rows], cts[:n_rows]):
                o_ref[...] = ct.astype(o_ref.dtype)
            if n_p:
                @pl.when(pl.program_id(0) == 0)
                def _():
                    for o_ref in out_refs[n_rows:]:
                        o_ref[...] = jnp.zeros_like(o_ref)

                for o_ref, ct in zip(out_refs[n_rows:], cts[n_rows:]):
                    o_ref[...] += ct.astype(o_ref.dtype)

        return pl.pallas_call(
            body,
            name=name + "_bwd",
            grid=(s // tile,),
            in_specs=specs(rows + aux, True) + specs(params, False) + specs(gs, True),
            out_specs=specs(rows, True) + specs(params, False),
            out_shape=[jax.ShapeDtypeStruct(x.shape, x.dtype) for x in rows + params],
            compiler_params=_cparams("arbitrary" if n_p else "parallel"),
        )(*args, *gs)

    @jax.custom_vjp
    def op(*args):
        return tuple(fwd_call(*args)) + ((args[0],) if passthrough else ())

    def op_fwd(*args):
        return op(*args), args

    def op_bwd(args, gs):
        cts = bwd_call(tuple(args), tuple(gs))
        rows_ct, par_ct = cts[:n_rows], cts[n_rows:]
        aux_ct = [jnp.zeros_like(a) for a in args[n_rows:n_rows + n_aux]]
        return tuple(rows_ct) + tuple(aux_ct) + tuple(par_ct)

    op.defvjp(op_fwd, op_bwd)
    return op


SCAN_SEGMENTS = SUBLANES


def _scan_step(ar, ai, xr, xi, br, bi):
    return ar * xr - ai * xi + br, ar * xi + ai * xr + bi


_NT_DIMS = (((1,), (1,)), ((), ()))
_TN_DIMS = (((0,), (0,)), ((), ()))


S5_GROUPS_PER_STEP = 128
S5_STATE_LANES = 512


def _s5_interleave(src_ref, scr_ref, rows):
    for k in range(SCAN_SEGMENTS):
        scr_ref[pl.ds(k, rows, stride=SCAN_SEGMENTS), :] = src_ref[k].astype(F32)


def _s5_deinterleave(val, scr_ref, dst_ref, rows):
    scr_ref[...] = val
    for k in range(SCAN_SEGMENTS):
        dst_ref[k] = scr_ref[pl.ds(k, rows, stride=SCAN_SEGMENTS), :].astype(dst_ref.dtype)


def _s5_segment_starts(a_ref, fr_ref, fi_ref, sr, si, seg_len, order):
    tn = sr.shape[1]
    pr, pi = a_ref[0:1, :], a_ref[1:2, :]
    for _ in range(seg_len.bit_length() - 1):
        pr, pi = pr * pr - pi * pi, 2.0 * pr * pi
    cr = jnp.zeros((1, tn), F32)
    ci = jnp.zeros((1, tn), F32)
    for idx, k in enumerate(order):
        if idx > 0:
            kp = order[idx - 1]
            cr, ci = (fr_ref[kp:kp + 1, :] + pr * cr - pi * ci, fi_ref[kp:kp + 1, :] + pr * ci + pi * cr)
        sr[k:k + 1, :] = cr
        si[k:k + 1, :] = ci


def _s5_pass_call(src, w_r, w_i, a, transpose_w, reverse, finals, extra, name):
    s = src.shape[0]
    nb = w_r.shape[0]
    n = nb * S5_STATE_LANES
    seg_len = s // SCAN_SEGMENTS
    ti = min(S5_GROUPS_PER_STEP, seg_len)
    nt = seg_len // ti
    tr = SCAN_SEGMENTS * ti
    tn = S5_STATE_LANES
    assert seg_len & (seg_len - 1) == 0
    order = list(range(SCAN_SEGMENTS))[::-1] if reverse else list(range(SCAN_SEGMENTS))
    dn_in = _NT_DIMS if transpose_w else _NN_DIMS
    first = finals is None
    backward = (not first) and reverse
    forward = (not first) and not reverse
    tmap3 = (lambda c, j: (0, nt - 1 - j, c)) if reverse else (lambda c, j: (0, j, c))
    tmap2 = (lambda c, j: (nt - 1 - j, c)) if reverse else (lambda c, j: (j, c))
    bf = lambda v: v.astype(BF16)

    def body(*refs):
        it = iter(refs)
        src_ref, wr_ref, wi_ref, a_ref = next(it), next(it), next(it), next(it)
        if not first:
            fr_ref, fi_ref = next(it), next(it)
        if forward:
            cdr_ref, cdi_ref = next(it), next(it)
            xr_out, xi_out, y_ref = next(it), next(it), next(it)
        if backward:
            xr_ref, xi_ref, u_ref, bdr_ref, bdi_ref = next(it), next(it), next(it), next(it), next(it)
            du_ref, dar_ref, dai_ref, dbr_ref, dbi_ref, dcr_ref, dci_ref = (next(it) for _ in range(7))
        if first:
            fr_out, fi_out = next(it), next(it)
        sr, si, in_scr, dr_scr, di_scr = next(it), next(it), next(it), next(it), next(it)
        if backward:
            accr, acci, u_scr = next(it), next(it), next(it)
        j = pl.program_id(1)

        @pl.when(j == 0)
        def _():
            if first:
                sr[...] = jnp.zeros_like(sr)
                si[...] = jnp.zeros_like(si)
            else:
                _s5_segment_starts(a_ref, fr_ref, fi_ref, sr, si, seg_len, order)
            if backward:
                for r in (accr, acci, dbr_ref, dbi_ref, dcr_ref, dci_ref):
                    r[...] = jnp.zeros_like(r)

        _s5_interleave(src_ref, in_scr, ti)
        src_b = bf(in_scr[...])
        dr_scr[...] = lax.dot_general(src_b, bf(wr_ref[0]), dn_in, preferred_element_type=F32)
        di_scr[...] = lax.dot_general(src_b, bf(wi_ref[0]), dn_in, preferred_element_type=F32)
        ar = jnp.broadcast_to(a_ref[0:1, :], (SUBLANES, tn))
        ai = jnp.broadcast_to(a_ref[1:2, :], (SUBLANES, tn))

        def step(ii, carry):
            i = (ti - 1 - ii) if reverse else ii
            rows = pl.ds(pl.multiple_of(i * SUBLANES, SUBLANES), SUBLANES)
            xr, xi = carry[0], carry[1]
            if backward:
                zr, zi = xr_ref[rows, :], xi_ref[rows, :]
                acc = (carry[2] + xr * zr + xi * zi, carry[3] + xi * zr - xr * zi)
            nr, ni = _scan_step(ar, ai, xr, xi, dr_scr[rows, :], di_scr[rows, :])
            if forward:
                xr_out[rows, :] = nr
                xi_out[rows, :] = ni
            if backward:
                dr_scr[rows, :] = nr
                di_scr[rows, :] = ni
            return (nr, ni) + (acc if backward else ())

        init = (sr[...], si[...]) + ((accr[...], acci[...]) if backward else ())
        out = lax.fori_loop(0, ti, step, init, unroll=4)
        sr[...] = out[0]
        si[...] = out[1]
        if first:
            @pl.when(j == nt - 1)
            def _():
                fr_out[...] = out[0]
                fi_out[...] = out[1]
        if forward:
            y = (lax.dot_general(bf(xr_out[...]), bf(cdr_ref[0]), _NN_DIMS, preferred_element_type=F32)
                 + lax.dot_general(bf(xi_out[...]), bf(cdi_ref[0]), _NN_DIMS, preferred_element_type=F32))
            _s5_deinterleave(y, in_scr, y_ref, ti)
        if backward:
            accr[...] = out[2]
            acci[...] = out[3]
            g_r, g_i = bf(dr_scr[...]), bf(di_scr[...])
            dcr_ref[0] += lax.dot_general(bf(xr_ref[...]), src_b, _TN_DIMS, preferred_element_type=F32)
            dci_ref[0] += lax.dot_general(bf(xi_ref[...]), src_b, _TN_DIMS, preferred_element_type=F32)
            _s5_interleave(u_ref, u_scr, ti)
            u_b = bf(u_scr[...])
            dbr_ref[0] += lax.dot_general(u_b, g_r, _TN_DIMS, preferred_element_type=F32)
            dbi_ref[0] += lax.dot_general(u_b, g_i, _TN_DIMS, preferred_element_type=F32)
            du = (lax.dot_general(g_r, bf(bdr_ref[0]), _NT_DIMS, preferred_element_type=F32)
                  + lax.dot_general(g_i, bf(bdi_ref[0]), _NT_DIMS, preferred_element_type=F32))
            _s5_deinterleave(du, u_scr, du_ref, ti)

            @pl.when(j == nt - 1)
            def _():
                dar_ref[...] = jnp.sum(out[2], axis=0, keepdims=True)
                dai_ref[...] = jnp.sum(out[3], axis=0, keepdims=True)

    view3 = lambda t: t.reshape(SCAN_SEGMENTS, seg_len, t.shape[1])
    spec3 = pl.BlockSpec((SCAN_SEGMENTS, ti, LANES), tmap3)
    wspec = lambda w: pl.BlockSpec((1,) + w.shape[1:], lambda c, j: (c, 0, 0))
    aspec = pl.BlockSpec((2, tn), lambda c, j: (0, c))
    fspec = pl.BlockSpec((SUBLANES, tn), lambda c, j: (0, c))
    xspec = pl.BlockSpec((tr, tn), tmap2)
    dspec = pl.BlockSpec((1, tn), lambda c, j: (0, c))
    f32 = lambda *shape: jax.ShapeDtypeStruct(shape, F32)
    args, in_specs = [view3(src), w_r, w_i, a], [spec3, wspec(w_r), wspec(w_i), aspec]
    if not first:
        args += list(finals)
        in_specs += [fspec, fspec]
    if forward:
        args += list(extra)
        in_specs += [wspec(extra[0]), wspec(extra[1])]
        out_specs, out_shape = [xspec, xspec, spec3], [f32(s, n), f32(s, n), f32(SCAN_SEGMENTS, seg_len, nb * LANES)]
    elif backward:
        x_r, x_i, u, bd_r, bd_i = extra
        args += [x_r, x_i, view3(u), bd_r, bd_i]
        in_specs += [xspec, xspec, spec3, wspec(bd_r), wspec(bd_i)]
        out_specs = [spec3, dspec, dspec, wspec(bd_r), wspec(bd_i), wspec(w_r), wspec(w_i)]
        out_shape = [jax.ShapeDtypeStruct((SCAN_SEGMENTS, seg_len, nb * LANES), u.dtype), f32(1, n), f32(1, n),
                     f32(*bd_r.shape), f32(*bd_i.shape),
                     f32(*w_r.shape), f32(*w_i.shape)]
    else:
        out_specs, out_shape = [fspec, fspec], [f32(SUBLANES, n), f32(SUBLANES, n)]
    scratch = ([pltpu.VMEM((SUBLANES, tn), F32)] * 2 + [pltpu.VMEM((tr, LANES), F32)] + [pltpu.VMEM((tr, tn), F32)] * 2
               + ([pltpu.VMEM((SUBLANES, tn), F32)] * 2 + [pltpu.VMEM((tr, LANES), F32)] if backward else []))
    return pl.pallas_call(
        body, name=name, grid=(nb, nt), in_specs=in_specs, out_specs=out_specs, out_shape=out_shape,
        scratch_shapes=scratch, compiler_params=_cparams("parallel", "arbitrary"),
    )(*args)


@jax.custom_vjp
def s5_core(u, a, bd_r, bd_i, cd_r, cd_i):
    return _s5_core_fwd(u, a, bd_r, bd_i, cd_r, cd_i)[0]


def _s5_core_fwd(u, a, bd_r, bd_i, cd_r, cd_i):
    fin = _s5_pass_call(u, bd_r, bd_i, a, False, False, None, None, "s5_fwd_finals")
    x_r, x_i, y = _s5_pass_call(u, bd_r, bd_i, a, False, False, fin, (cd_r, cd_i), "s5_fwd_scan")
    return y.reshape(u.shape), (u, a, bd_r, bd_i, cd_r, cd_i, x_r, x_i)


def _s5_core_bwd(res, dy):
    u, a, bd_r, bd_i, cd_r, cd_i, x_r, x_i = res
    a_conj = a * jnp.array([[1.0], [-1.0]], F32)
    fin = _s5_pass_call(dy, cd_r, cd_i, a_conj, True, True, None, None, "s5_bwd_finals")
    du, da_r, da_i, dbd_r, dbd_i, dcd_r, dcd_i = _s5_pass_call(
        dy, cd_r, cd_i, a_conj, True, True, fin, (x_r, x_i, u, bd_r, bd_i), "s5_bwd_scan")
    return du.reshape(u.shape), jnp.concatenate([da_r, da_i], axis=0), dbd_r, dbd_i, dcd_r, dcd_i


s5_core.defvjp(_s5_core_fwd, _s5_core_bwd)


_NN = (((1,), (0,)), ((), ()))
_NT = (((1,), (1,)), ((), ()))
_TN = (((0,), (0,)), ((), ()))


def _dot(a, b, dn):
    return lax.dot_general(a.astype(BF16), b.astype(BF16), dn, preferred_element_type=F32)


@jax.custom_vjp
def bdot_nn(a, b):
    return _dot(a, b, _NN)


bdot_nn.defvjp(lambda a, b: (_dot(a, b, _NN), (a, b)),
               lambda r, g: (_dot(g, r[1], _NT).astype(r[0].dtype), _dot(r[0], g, _TN).astype(r[1].dtype)))


@jax.custom_vjp
def bdot_nt(a, b):
    return _dot(a, b, _NT)


bdot_nt.defvjp(lambda a, b: (_dot(a, b, _NT), (a, b)),
               lambda r, g: (_dot(g, r[1], _NN).astype(r[0].dtype), _dot(g, r[0], _TN).astype(r[1].dtype)))


@jax.custom_vjp
def bdot_tn(a, b):
    return _dot(a, b, _TN)


bdot_tn.defvjp(lambda a, b: (_dot(a, b, _TN), (a, b)),
               lambda r, g: (_dot(r[1], g, _NT).astype(r[0].dtype), _dot(r[0], g, _NN).astype(r[1].dtype)))


def _split_hi_lo(x):
    h = x.astype(BF16)
    return h, (x - h.astype(F32)).astype(BF16)


def _exact_dot(t, x, dn):
    h, l = _split_hi_lo(x)
    d = lambda p: lax.dot_general(t, p, dn, preferred_element_type=F32)
    return d(h) + d(l)


@jax.custom_vjp
def select_dot(t, x):
    return _exact_dot(t, x, _NN)


select_dot.defvjp(lambda t, x: (_exact_dot(t, x, _NN), t),
                  lambda t, g: (jnp.zeros_like(t), _exact_dot(t, g, _TN)))


def _split_rows_impl(x, h):
    return tuple(x[i * h:(i + 1) * h] for i in range(x.shape[0] // h))


@functools.partial(jax.custom_vjp, nondiff_argnums=(1,))
def split_rows(x, h):
    return _split_rows_impl(x, h)


split_rows.defvjp(lambda x, h: (_split_rows_impl(x, h), None),
                  lambda h, r, g: (jnp.concatenate(g, axis=0),))


@jax.custom_vjp
def join_rows(parts):
    return jnp.concatenate(parts, axis=0)


def _join_rows_bwd(hs, g):
    out, off = [], 0
    for h in hs:
        out.append(g[off:off + h])
        off += h
    return (tuple(out),)


join_rows.defvjp(lambda parts: (jnp.concatenate(parts, axis=0), tuple(p.shape[0] for p in parts)), _join_rows_bwd)


def _split_lanes_impl(x, w):
    return tuple(x[:, i * w:(i + 1) * w] for i in range(x.shape[1] // w))


@functools.partial(jax.custom_vjp, nondiff_argnums=(1,))
def split_lanes(x, w):
    return _split_lanes_impl(x, w)


split_lanes.defvjp(lambda x, w: (_split_lanes_impl(x, w), None),
                   lambda w, r, g: (jnp.concatenate(g, axis=1),))


def _join_impl(parts):
    return jnp.concatenate(parts, axis=1)


@jax.custom_vjp
def join_lanes(parts):
    return _join_impl(parts)


def _join_bwd(ws, g):
    out, off = [], 0
    for w in ws:
        out.append(g[:, off:off + w])
        off += w
    return (tuple(out),)


join_lanes.defvjp(lambda parts: (_join_impl(parts), tuple(p.shape[1] for p in parts)), _join_bwd)


def _rope_impl(x, c, sa, sb, shift):
    w = x.shape[1]
    return x * c + pltpu.roll(x, w - shift, 1) * sa + pltpu.roll(x, shift, 1) * sb


@functools.partial(jax.custom_vjp, nondiff_argnums=(4,))
def rope_lanes(x, c, sa, sb, shift):
    return _rope_impl(x, c, sa, sb, shift)


def _rope_bwd(shift, r, g):
    c, sa, sb = r
    w = g.shape[1]
    dx = g * c + pltpu.roll(g * sa, shift, 1) + pltpu.roll(g * sb, w - shift, 1)
    return dx, jnp.zeros_like(c), jnp.zeros_like(sa), jnp.zeros_like(sb)


rope_lanes.defvjp(lambda x, c, sa, sb, shift: (_rope_impl(x, c, sa, sb, shift), (c, sa, sb)), _rope_bwd)


RMS_EPS = 1e-6


def _rms(x, g):
    return x * lax.rsqrt(jnp.mean(x * x, axis=-1, keepdims=True) + RMS_EPS) * g


ATTN_BLOCK = 512
MASK_VALUE = -1e30
LOG2E = math.log2(math.e)
LN2 = math.log(2.0)
V_ONES_LANE = 64


def _causal_mask(t):
    r = lax.broadcasted_iota(jnp.int32, (t, t), 0)
    c = lax.broadcasted_iota(jnp.int32, (t, t), 1)
    return c <= r


def _attn_fwd_call(q, k, v):
    s, width = q.shape
    n_heads = width // LANES
    tq = min(ATTN_BLOCK, s)
    nq = s // tq

    def body(q_ref, k_ref, v_ref, o_ref, lse_ref):
        i = pl.program_id(1)
        qb = q_ref[...].astype(BF16)
        ones_lane = lax.broadcasted_iota(jnp.int32, (tq, LANES), 1) == V_ONES_LANE

        def block(kb, carry, masked):
            m, acc = carry
            rows = pl.ds(pl.multiple_of(kb * tq, tq), tq)
            sc = lax.dot_general(qb, k_ref[rows, :].astype(BF16), _NT, preferred_element_type=F32)
            if masked:
                sc = jnp.where(_causal_mask(tq), sc, MASK_VALUE)
            m_new = jnp.maximum(m, jnp.max(sc, axis=-1, keepdims=True))
            p = jnp.exp2(sc - m_new).astype(BF16)
            vb = jnp.where(ones_lane, 1.0, v_ref[rows, :]).astype(BF16)
            acc = jnp.exp2(m - m_new) * acc + lax.dot_general(p, vb, _NN, preferred_element_type=F32)
            return m_new, acc

        init = (jnp.full((tq, 1), MASK_VALUE, F32), jnp.zeros((tq, LANES), F32))
        carry = lax.fori_loop(0, i, lambda kb, c: block(kb, c, False), init)
        m, acc = block(i, carry, True)
        l = jnp.sum(jnp.where(ones_lane, acc, 0.0), axis=-1, keepdims=True)
        o_ref[...] = jnp.where(ones_lane, 0.0, acc / l).astype(o_ref.dtype)
        lse_ref[...] = jnp.broadcast_to(m + jnp.log2(l), (tq, LANES))

    qspec = pl.BlockSpec((tq, LANES), lambda h, i: (i, h))
    kspec = pl.BlockSpec((s, LANES), lambda h, i: (0, h))
    return pl.pallas_call(
        body,
        name="mla_attn_fwd",
        grid=(n_heads, nq),
        in_specs=[qspec, kspec, kspec],
        out_specs=[qspec, qspec],
        out_shape=[jax.ShapeDtypeStruct((s, width), BF16), jax.ShapeDtypeStruct((s, width), F32)],
        compiler_params=_cparams("parallel", "parallel"),
    )(q, k, v)


def _attn_bwd_call(q, k, v, o, lse, do):
    s, width = q.shape
    n_heads = width // LANES
    tq = min(ATTN_BLOCK, s)
    nq = s // tq

    def body(q_ref, k_ref, v_ref, o_ref, lse_ref, do_ref, dq_ref, dk_ref, dv_ref, dq_acc):
        j = pl.program_id(1)

        @pl.when(j == 0)
        def _():
            dq_acc[...] = jnp.zeros_like(dq_acc)

        kb = k_ref[...].astype(BF16)
        vb = v_ref[...].astype(BF16)

        def block(i, carry, masked):
            dk, dv = carry
            rows = pl.ds(pl.multiple_of(i * tq, tq), tq)
            qi = q_ref[rows, :].astype(BF16)
            doi = do_ref[rows, :].astype(F32)
            delta = jnp.sum(doi * o_ref[rows, :].astype(F32), axis=-1, keepdims=True)
            sc = lax.dot_general(qi, kb, _NT, preferred_element_type=F32)
            if masked:
                sc = jnp.where(_causal_mask(tq), sc, MASK_VALUE)
            p = jnp.exp2(sc - lse_ref[rows, 0:1])
            dob = doi.astype(BF16)
            dv = dv + lax.dot_general(p.astype(BF16), dob, _TN, preferred_element_type=F32)
            dp = lax.dot_general(dob, vb, _NT, preferred_element_type=F32)
            ds = (p * (dp - delta)).astype(BF16)
            dq_acc[rows, :] += lax.dot_general(ds, kb, _NN, preferred_element_type=F32)
            dk = dk + lax.dot_general(ds, qi, _TN, preferred_element_type=F32)
            return dk, dv

        zero = jnp.zeros((tq, LANES), F32)
        carry = block(j, (zero, zero), True)
        dk, dv = lax.fori_loop(j + 1, nq, lambda i, c: block(i, c, False), carry)
        dk_ref[...] = (dk * LN2).astype(dk_ref.dtype)
        dv_ref[...] = dv.astype(dv_ref.dtype)

        @pl.when(j == nq - 1)
        def _():
            dq_ref[...] = (dq_acc[...] * LN2).astype(dq_ref.dtype)

    full = pl.BlockSpec((s, LANES), lambda h, j: (0, h))
    blk = pl.BlockSpec((tq, LANES), lambda h, j: (j, h))
    return pl.pallas_call(
        body,
        name="mla_attn_bwd",
        grid=(n_heads, nq),
        in_specs=[full, blk, blk, full, full, full],
        out_specs=[full, blk, blk],
        out_shape=[jax.ShapeDtypeStruct((s, width), t.dtype) for t in (q, k, v)],
        scratch_shapes=[pltpu.VMEM((s, LANES), F32)],
        compiler_params=_cparams("parallel", "arbitrary"),
    )(q, k, v, o, lse, do)


@jax.custom_vjp
def causal_attention(q, k, v):
    return _attn_fwd_call(q, k, v)[0]


def _causal_attention_fwd(q, k, v):
    o, lse = _attn_fwd_call(q, k, v)
    return o, (q, k, v, o, lse)


def _causal_attention_bwd(res, do):
    return tuple(_attn_bwd_call(*res, do))


causal_attention.defvjp(_causal_attention_fwd, _causal_attention_bwd)


HG_HEADS = 4
HG_CHUNK = 32
HG_REF_ROW = HG_CHUNK // 2 - 1
HG_TILE_ROWS = 256
HG_EXP_CLAMP = 80.0


def _hg_tile_masks(t):
    shift = HG_CHUNK.bit_length() - 1
    r = lax.broadcasted_iota(jnp.int32, (t, t), 0)
    c = lax.broadcasted_iota(jnp.int32, (t, t), 1)
    start = lax.shift_left(lax.shift_right_logical(r, shift), shift)
    causal = (c >= start) & (c <= r)
    return causal, c == start + HG_REF_ROW, c == start + (HG_CHUNK - 1)


def _hg_tile(q, fl, v, lb, st):
    t = q.shape[0]
    causal, ref_sel, last_sel = _hg_tile_masks(t)
    f = lb + (1.0 - lb) * jax.nn.sigmoid(fl)
    kk = 1.0 - f
    qs = q * jax.nn.sigmoid(q)
    b = select_dot(causal.astype(BF16), jnp.log(f))
    b_ref = select_dot(ref_sel.astype(BF16), b)
    b_last = select_dot(last_sel.astype(BF16), b)
    q_in = qs * jnp.exp(jnp.minimum(b - b_ref, HG_EXP_CLAMP))
    k_in = kk * jnp.exp(jnp.minimum(b_ref - b, HG_EXP_CLAMP))
    o = bdot_nn(jnp.where(causal, bdot_nt(q_in, k_in), 0.0), v)
    q_hat = split_rows(qs * jnp.exp(b), HG_CHUNK)
    k_hat = split_rows(kk * jnp.exp(b_last - b), HG_CHUNK)
    decay = split_rows(jnp.exp(b_last), HG_CHUNK)
    vs = split_rows(v, HG_CHUNK)
    first_row = lax.broadcasted_iota(jnp.int32, (HG_CHUNK, LANES), 0) == 0
    inter = []
    for c in range(t // HG_CHUNK):
        inter.append(bdot_nt(q_hat[c], st))
        st = st * jnp.sum(jnp.where(first_row, decay[c], 0.0), axis=0, keepdims=True) + bdot_tn(vs[c], k_hat[c])
    return o + join_rows(tuple(inter)), st


def _hg_head(q, fl, v, gate, lb, gn, st):
    o, st = _hg_tile(q, fl, v, lb, st)
    return _rms(o, gn) * (gate * jax.nn.sigmoid(gate)), st


HG_PARTS = 4


def _hg_part_slices(h, width):
    return [slice(p * width + h * LANES, p * width + (h + 1) * LANES) for p in range(HG_PARTS)]


def _hg_fwd_call(x, lb, gn):
    s = x.shape[0]
    width = x.shape[1] // HG_PARTS
    tr = min(HG_TILE_ROWS, s)
    nt = s // tr

    def body(x_ref, lb_ref, gn_ref, o_ref, sts_ref, st_ref):
        @pl.when(pl.program_id(0) == 0)
        def _():
            st_ref[...] = jnp.zeros_like(st_ref)

        for h in range(HG_HEADS):
            ln = slice(h * LANES, (h + 1) * LANES)
            st = st_ref[h]
            sts_ref[0, h] = st
            o, st_new = _hg_head(*(x_ref[:, sl] for sl in _hg_part_slices(h, width)), lb_ref[:, ln], gn_ref[...], st)
            o_ref[:, ln] = o.astype(o_ref.dtype)
            st_ref[h] = st_new

    const = lambda shape: pl.BlockSpec(shape, lambda j: (0, 0))
    return pl.pallas_call(
        body,
        name="hgrn2_fwd",
        grid=(nt,),
        in_specs=[pl.BlockSpec((tr, HG_PARTS * width), lambda j: (j, 0)), const((1, width)), const((1, LANES))],
        out_specs=[pl.BlockSpec((tr, width), lambda j: (j, 0)),
                   pl.BlockSpec((1, HG_HEADS, LANES, LANES), lambda j: (j, 0, 0, 0))],
        out_shape=[jax.ShapeDtypeStruct((s, width), BF16),
                   jax.ShapeDtypeStruct((nt, HG_HEADS, LANES, LANES), F32)],
        scratch_shapes=[pltpu.VMEM((HG_HEADS, LANES, LANES), F32)],
        compiler_params=_cparams("arbitrary"),
    )(x, lb, gn)


def _hg_bwd_call(x, lb, gn, sts, do):
    s = x.shape[0]
    width = x.shape[1] // HG_PARTS
    tr = min(HG_TILE_ROWS, s)
    nt = s // tr

    def body(x_ref, lb_ref, gn_ref, sts_ref, do_ref, dx_ref, dlb_ref, dgn_ref, dst_ref):
        @pl.when(pl.program_id(0) == 0)
        def _():
            dst_ref[...] = jnp.zeros_like(dst_ref)
            dlb_ref[...] = jnp.zeros_like(dlb_ref)
            dgn_ref[...] = jnp.zeros_like(dgn_ref)

        for h in range(HG_HEADS):
            ln = slice(h * LANES, (h + 1) * LANES)
            parts = _hg_part_slices(h, width)
            _, vjp = jax.vjp(_hg_head, *(x_ref[:, sl] for sl in parts), lb_ref[:, ln], gn_ref[...], sts_ref[0, h])
            cts = vjp((do_ref[:, ln].astype(F32), dst_ref[h]))
            for sl, ct in zip(parts, cts[:HG_PARTS]):
                dx_ref[:, sl] = ct.astype(dx_ref.dtype)
            dlb_ref[:, ln] += cts[HG_PARTS]
            dgn_ref[...] += cts[HG_PARTS + 1]
            dst_ref[h] = cts[HG_PARTS + 2]

    rev = lambda w: pl.BlockSpec((tr, w), lambda j: (nt - 1 - j, 0))
    const = lambda shape: pl.BlockSpec(shape, lambda j: (0, 0))
    return pl.pallas_call(
        body,
        name="hgrn2_bwd",
        grid=(nt,),
        in_specs=[rev(HG_PARTS * width), const((1, width)), const((1, LANES)),
                  pl.BlockSpec((1, HG_HEADS, LANES, LANES), lambda j: (nt - 1 - j, 0, 0, 0)), rev(width)],
        out_specs=[rev(HG_PARTS * width), const((1, width)), const((1, LANES))],
        out_shape=[jax.ShapeDtypeStruct(x.shape, BF16), jax.ShapeDtypeStruct((1, width), F32),
                   jax.ShapeDtypeStruct((1, LANES), F32)],
        scratch_shapes=[pltpu.VMEM((HG_HEADS, LANES, LANES), F32)],
        compiler_params=_cparams("arbitrary"),
    )(x, lb, gn, sts, do)


@jax.custom_vjp
def hgrn2_mixer(h, w, lb, gn):
    return _hg_fwd_call(_mm_call(h, w, False, False, name="hgrn2_proj"), lb, gn)[0]


def _hgrn2_mixer_fwd(h, w, lb, gn):
    x = _mm_call(h, w, False, False, name="hgrn2_proj")
    o, sts = _hg_fwd_call(x, lb, gn)
    return o, (h, w, x, lb, gn, sts)


def _hgrn2_mixer_bwd(res, do):
    h, w, x, lb, gn, sts = res
    dx, dlb, dgn = _hg_bwd_call(x, lb, gn, sts, do)
    dh = _mm_call(dx, w, False, True, out_dtype=h.dtype, name="hgrn2_proj_da")
    dw = _mm_call(h, dx, True, False, out_dtype=w.dtype, name="hgrn2_proj_db")
    return dh, dw, dlb, dgn


hgrn2_mixer.defvjp(_hgrn2_mixer_fwd, _hgrn2_mixer_bwd)


D_MODEL = 1024
DEPTH = 2
SSM_GROUPS, SSM_GROUP_CH, SSM_STATE = 32, 16, 64
SSM_WIDTH = SSM_GROUPS * SSM_GROUP_CH
MLA_HEADS, MLA_NOPE, MLA_ROPE, MLA_V = 8, 64, 32, 64
MLA_Q_RANK, MLA_KV_RANK = 512, 256
HG_WIDTH = HG_HEADS * LANES
X_HEADS, X_HEAD_DIM = 4, 128
X_WIDTH = X_HEADS * X_HEAD_DIM
D_FF = 2816
ROPE_THETA = 10000.0
IN_SPLITS = (SSM_WIDTH, MLA_Q_RANK, MLA_KV_RANK, MLA_ROPE, HG_WIDTH, HG_WIDTH, HG_WIDTH, HG_WIDTH, 3 * D_MODEL)
ROPE_LANE0 = MLA_NOPE
MLA_Q_SCALE = LOG2E / math.sqrt(MLA_NOPE + MLA_ROPE)
ROW_TILE = 512
MEM_ROW_TILE = 256


def _t_rms(x, g):
    return (_rms(x.astype(F32), g).astype(BF16),)


def _t_s5_act(y, u, d):
    return (jax.nn.gelu(y + d * u.astype(F32)).astype(BF16),)


def _t_glu(z):
    zo, zg = split_lanes(z.astype(F32), D_MODEL)
    return ((zo * jax.nn.sigmoid(zg)).astype(BF16),)


def _t_mla_rope(q, k, kr, c, sa, sb):
    rep = lambda t: jnp.concatenate([t] * MLA_HEADS, axis=1)
    half = MLA_ROPE // 2
    q_out = rope_lanes(q.astype(F32), rep(c), rep(sa), rep(sb), half) * MLA_Q_SCALE
    kr_out = rope_lanes(kr.astype(F32), c, sa, sb, half)
    return q_out.astype(BF16), (k.astype(F32) + join_lanes((kr_out,) * MLA_HEADS)).astype(BF16)


def _t_merge(y_ssm, y_mla, y_hg, gates):
    g0, g1, g2 = split_lanes(gates.astype(F32), D_MODEL)
    mix = (jax.nn.sigmoid(g0) * y_ssm.astype(F32) + jax.nn.sigmoid(g1) * y_mla.astype(F32)
           + jax.nn.sigmoid(g2) * y_hg.astype(F32))
    return (mix.astype(BF16),)


def _t_xattn(q, kv):
    scale = 1.0 / math.sqrt(X_HEAD_DIM)
    heads = split_lanes(kv, X_HEAD_DIM)
    outs = []
    for qh, kh, vh in zip(split_lanes(q, X_HEAD_DIM), heads[:X_HEADS], heads[X_HEADS:]):
        sc = bdot_nt(qh, kh) * scale
        p = jnp.exp(sc - jnp.max(sc, axis=-1, keepdims=True))
        p = p / jnp.sum(p, axis=-1, keepdims=True)
        outs.append(bdot_nn(p, vh))
    return (join_lanes(tuple(outs)).astype(BF16),)


def _t_swiglu(gate_up):
    gt, up = split_lanes(gate_up.astype(F32), D_FF)
    return ((gt * jax.nn.sigmoid(gt) * up).astype(BF16),)


def _t_loss(x, tgt, g):
    e = _rms(x, g) - tgt
    return (jnp.broadcast_to(jnp.mean(e * e, axis=-1, keepdims=True), (x.shape[0], LANES)),)


rms_op = rowwise(_t_rms, 1, 0, ROW_TILE, "rmsnorm")
rms_mem_op = rowwise(_t_rms, 1, 0, MEM_ROW_TILE, "rmsnorm_mem")
rms_res_op = rowwise(_t_rms, 1, 0, ROW_TILE, "rmsnorm_res", passthrough=True)
s5_act_op = rowwise(_t_s5_act, 2, 0, ROW_TILE, "s5_act")
glu_op = rowwise(_t_glu, 1, 0, ROW_TILE, "glu")
mla_rope_op = rowwise(_t_mla_rope, 3, 3, ROW_TILE, "mla_rope")
merge_op = rowwise(_t_merge, 4, 0, ROW_TILE, "merge")
xattn_op = rowwise(_t_xattn, 1, 0, ROW_TILE, "xattn")
swiglu_op = rowwise(_t_swiglu, 1, 0, ROW_TILE, "swiglu")
loss_op = rowwise(_t_loss, 1, 1, ROW_TILE, "loss")


def _rope_tables(positions):
    half = MLA_ROPE // 2
    inv_freq = ROPE_THETA ** (-jnp.arange(half, dtype=F32) / half)
    ang = positions.astype(F32)[:, None] * inv_freq
    cos, sin = jnp.cos(ang), jnp.sin(ang)
    s = positions.shape[0]
    z = lambda w: jnp.zeros((s, w), F32)
    tail = LANES - ROPE_LANE0 - MLA_ROPE
    c = jnp.concatenate([jnp.ones((s, ROPE_LANE0), F32), cos, cos, z(tail)], axis=1)
    sa = jnp.concatenate([z(ROPE_LANE0), -sin, z(half), z(tail)], axis=1)
    sb = jnp.concatenate([z(ROPE_LANE0), z(half), sin, z(tail)], axis=1)
    return c, sa, sb


def _s5_operators(lam_re, lam_im, b_re, b_im, c_re, c_im, log_step):
    g, p, h = SSM_GROUPS, SSM_STATE, SSM_GROUP_CH
    lam = lax.complex(lam_re, lam_im)
    lam_bar = jnp.exp(lam * jnp.exp(log_step)[:, None])
    b_bar = ((lam_bar - 1.0) / lam)[..., None] * lax.complex(b_re, b_im)
    per = LANES // h
    nb = g // per
    eye = jnp.eye(per, dtype=F32)
    bd = lambda t: jnp.einsum("jgph,gk->jghkp", t.reshape(nb, per, p, h), eye).reshape(nb, per * h, per * p)
    cd = lambda t: jnp.einsum("jghp,gk->jgpkh", t.reshape(nb, per, h, p), eye).reshape(nb, per * p, per * h)
    a = jnp.stack([jnp.real(lam_bar).reshape(-1), jnp.imag(lam_bar).reshape(-1)])
    return a, bd(jnp.real(b_bar)), bd(jnp.imag(b_bar)), cd(c_re), cd(-c_im)


LATENT_WIDTH = 1536
_LATENT = {}
_off = 0
for _name, _w in (("u", SSM_WIDTH), ("q_lat", MLA_Q_RANK), ("kv_lat", MLA_KV_RANK), ("k_rope", LANES)):
    _LATENT[_name] = (_off, _off + _w)
    _off += _w


def _layer_matrices(w, l):
    return {**_mixer_matrices(w, l), **_tail_matrices(w, l)}


def _tail_matrices(w, l):
    return dict(x_q=w["x_w_q"][l], x_kv=w["x_w_kv"][l], x_o=w["x_w_o"][l], ffn_gu=w["ffn_w_gate_up"][l],
                ffn_d=w["ffn_w_down"][l])


def _mixer_matrices(w, l):
    w_in = w["w_in"][l]
    d, dt = w_in.shape[0], w_in.dtype
    z = lambda n: jnp.zeros((d, n), dt)
    r0 = SSM_WIDTH + MLA_Q_RANK + MLA_KV_RANK
    r1 = r0 + MLA_ROPE
    r2 = r1 + HG_PARTS * HG_WIDTH
    w_latent = jnp.concatenate([w_in[:, :r0], z(ROPE_LANE0), w_in[:, r0:r1],
                                z(LATENT_WIDTH - r0 - ROPE_LANE0 - MLA_ROPE)], axis=1)
    pad_heads = lambda t: jnp.pad(t, ((0, 0), (0, 0), (0, LANES - t.shape[2]))).reshape(t.shape[0], -1)
    uq = w["mla_w_uq"][l].reshape(MLA_Q_RANK, MLA_HEADS, MLA_NOPE + MLA_ROPE)
    ukv = w["mla_w_ukv"][l].reshape(MLA_KV_RANK, MLA_HEADS, MLA_NOPE + MLA_V)
    wo = w["mla_w_o"][l].reshape(MLA_HEADS, MLA_V, D_MODEL)
    return dict(
        w_latent=w_latent, w_hg=w_in[:, r1:r2], w_gates=w_in[:, r2:], glu=w["ssm_w_glu"][l],
        uq=pad_heads(uq), uk=pad_heads(ukv[:, :, :MLA_NOPE]), uv=pad_heads(ukv[:, :, MLA_NOPE:]),
        mla_o=jnp.pad(wo, ((0, 0), (0, LANES - MLA_V), (0, 0))).reshape(MLA_HEADS * LANES, D_MODEL),
        hg_o=w["hg_w_o"][l], w_out=w["w_out"][l])


def _layer(x, mem, tabs, m, sp, l, lower_bound):
    return _tail(_mixer(x, tabs, m, sp, l, lower_bound), mem, m, sp, l)


def _mixer(x, tabs, m, sp, l, lower_bound):
    row = lambda name: sp[name][l].reshape(1, -1)
    h, x = rms_res_op(x, row("norm_mix"))
    latent = matmul(h, m["w_latent"], BF16)
    seg = lambda name: latent[:, _LATENT[name][0]:_LATENT[name][1]]
    a, bd_r, bd_i, cd_r, cd_i = _s5_operators(*(sp[n][l] for n in (
        "ssm_lam_re", "ssm_lam_im", "ssm_b_re", "ssm_b_im", "ssm_c_re", "ssm_c_im", "ssm_log_step")))
    u = seg("u")
    y = s5_core(u, a, bd_r, bd_i, cd_r, cd_i)
    (ya,) = s5_act_op(y, u, row("ssm_d"))
    (y_ssm,) = glu_op(matmul(ya, m["glu"], BF16))
    (qn,) = rms_op(seg("q_lat"), row("mla_q_norm"))
    (kvn,) = rms_op(seg("kv_lat"), row("mla_kv_norm"))
    q, k = mla_rope_op(matmul(qn, m["uq"], BF16), matmul(kvn, m["uk"], BF16), seg("k_rope"), *tabs)
    o = causal_attention(q, k, matmul(kvn, m["uv"], BF16))
    y_mla = matmul(o, m["mla_o"], BF16)
    y_hg = matmul(hgrn2_mixer(h, m["w_hg"], lower_bound, row("hg_g_norm")), m["hg_o"], BF16)
    (merged,) = merge_op(y_ssm, y_mla, y_hg, matmul(h, m["w_gates"], BF16))
    return matmul_add(merged, m["w_out"], x)


def _tail(x, mem, m, sp, l):
    row = lambda name: sp[name][l].reshape(1, -1)
    hc, x = rms_res_op(x, row("norm_cross"))
    (mn,) = rms_mem_op(mem, row("norm_mem"))
    (ox,) = xattn_op(matmul(hc, m["x_q"], BF16), matmul(mn, m["x_kv"]))
    x = matmul_add(ox, m["x_o"], x)
    hf, x = rms_res_op(x, row("norm_ffn"))
    (act,) = swiglu_op(matmul(hf, m["ffn_gu"], BF16))
    return matmul_add(act, m["ffn_d"], x)


def _lower_bounds(hg_lb):
    lb_p = jax.nn.softmax(hg_lb, axis=0)
    return jnp.cumsum(lb_p, axis=0) - lb_p[0:1]


def _final_loss(target, x, norm_final):
    (row_loss,) = loss_op(x, target, norm_final.reshape(1, -1))
    return 0.5 * jnp.sum(row_loss[:, 0])


def _local_loss(x, mem, positions, target, w, sp):
    tabs = _rope_tables(positions)
    lower = _lower_bounds(sp["hg_lb"])
    for l in range(DEPTH):
        x = _layer(x, mem, tabs, _layer_matrices(w, l), sp, l, lower[l].reshape(1, -1))
    return _final_loss(target, x, sp["norm_final"])


N_DEV = 8
N_CHIPS = 4
COMM_LANES = 512
MESH_ID = pl.DeviceIdType.MESH
_ANY = pl.BlockSpec(memory_space=pl.ANY)
_OTHER_CHIPS = ((1, 0), (0, 1), (1, 1))


def _place():
    return lax.axis_index("x"), lax.axis_index("y"), lax.axis_index("c")


def _all_gather_call(blocks, name):
    n = len(blocks)

    def body(*refs):
        x_refs, out_refs = refs[:n], refs[n:2 * n]
        send_sems, recv_sems, local_sems = refs[2 * n:]
        x, y, c = _place()
        me, sibling = (x, y, c), (x, y, 1 - c)
        chips = [(x ^ fx, y ^ fy) for fx, fy in _OTHER_CHIPS]

        def slot(i, px, py, pc):
            return out_refs[i].at[4 * px + 2 * py + pc]

        def copy(i, k, blk, to, src=None):
            return pltpu.make_async_remote_copy(
                src_ref=slot(i, *blk) if src is None else src, dst_ref=slot(i, *blk),
                send_sem=send_sems.at[i, k], recv_sem=recv_sems.at[i, k], device_id=to, device_id_type=MESH_ID)

        mine = [pltpu.make_async_copy(x_refs[i], slot(i, *me), local_sems.at[i]) for i in range(n)]
        first = []
        for i in range(n):
            first.append(copy(i, 0, me, sibling, src=x_refs[i]))
            first += [copy(i, 1 + j, me, (*chip, c), src=x_refs[i]) for j, chip in enumerate(chips)]
        for cp in mine + first:
            cp.start()
        passed = []
        for j, chip in enumerate(chips):
            for i in range(n):
                copy(i, 1 + j, (*chip, c), me).wait_recv()
                passed.append(copy(i, 4 + j, (*chip, c), sibling))
                passed[-1].start()
        for i in range(n):
            copy(i, 0, sibling, me).wait_recv()
            for j, chip in enumerate(chips):
                copy(i, 4 + j, (*chip, 1 - c), me).wait_recv()
        for cp in first + passed:
            cp.wait_send()
        for cp in mine:
            cp.wait()

    return pl.pallas_call(
        body,
        name=name,
        out_shape=[jax.ShapeDtypeStruct((N_DEV,) + b.shape, b.dtype) for b in blocks],
        in_specs=[_ANY] * n,
        out_specs=[_ANY] * n,
        scratch_shapes=[pltpu.SemaphoreType.DMA((n, 7)), pltpu.SemaphoreType.DMA((n, 7)), pltpu.SemaphoreType.DMA((n,))],
    )(*blocks)


def _pair_exchange_call(gs, name):
    n = len(gs)

    def body(*refs):
        g_refs, got_refs = refs[:n], refs[n:2 * n]
        send_sems, recv_sems = refs[2 * n:]
        x, y, c = _place()
        sends = [pltpu.make_async_remote_copy(
            src_ref=g_refs[i].at[2 * p + (1 - c)], dst_ref=got_refs[i].at[p],
            send_sem=send_sems.at[i, p], recv_sem=recv_sems.at[i, p], device_id=(x, y, 1 - c), device_id_type=MESH_ID)
            for i in range(n) for p in range(N_CHIPS)]
        for cp in sends:
            cp.start()
        for cp in sends:
            cp.wait_recv()
        for cp in sends:
            cp.wait_send()

    return pl.pallas_call(
        body,
        name=name,
        out_shape=[jax.ShapeDtypeStruct((N_CHIPS,) + g.shape[1:], g.dtype) for g in gs],
        in_specs=[_ANY] * n,
        out_specs=[_ANY] * n,
        scratch_shapes=[pltpu.SemaphoreType.DMA((n, N_CHIPS))] * 2,
    )(*gs)


def _chip_exchange_call(parts, name):
    n = len(parts)

    def body(*refs):
        p_refs, got_refs = refs[:n], refs[n:2 * n]
        send_sems, recv_sems = refs[2 * n:]
        x, y, c = _place()
        sends = []
        for i in range(n):
            for k, (fx, fy) in enumerate(_OTHER_CHIPS):
                px, py = x ^ fx, y ^ fy
                sends.append(pltpu.make_async_remote_copy(
                    src_ref=p_refs[i].at[2 * px + py], dst_ref=got_refs[i].at[k],
                    send_sem=send_sems.at[i, k], recv_sem=recv_sems.at[i, k], device_id=(px, py, c), device_id_type=MESH_ID))
        for cp in sends:
            cp.start()
        for cp in sends:
            cp.wait_recv()
        for cp in sends:
            cp.wait_send()

    return pl.pallas_call(
        body,
        name=name,
        out_shape=[jax.ShapeDtypeStruct((3,) + p.shape[1:], p.dtype) for p in parts],
        in_specs=[_ANY] * n,
        out_specs=[_ANY] * n,
        scratch_shapes=[pltpu.SemaphoreType.DMA((n, 3))] * 2,
    )(*parts)


def _rows_cols(shape):
    return math.prod(shape[:-1]), shape[-1]


def _pair_sum_call(g, got, c_idx, name):
    rows, cols = _rows_cols(got.shape[1:])
    tr = _pick_tile(rows, (512, 256, 128, 64, 32, 16))

    def body(c_ref, a_ref, b_ref, o_ref):
        o_ref[...] = (a_ref[...].astype(F32) + b_ref[...].astype(F32)).astype(o_ref.dtype)

    spec = pl.BlockSpec((1, tr, cols), lambda p, i, c_ref: (p, i, 0))
    out = pl.pallas_call(
        body,
        name=name,
        grid_spec=pltpu.PrefetchScalarGridSpec(
            num_scalar_prefetch=1, grid=(N_CHIPS, rows // tr),
            in_specs=[pl.BlockSpec((1, tr, cols), lambda p, i, c_ref: (2 * p + c_ref[0], i, 0)), spec],
            out_specs=spec),
        out_shape=jax.ShapeDtypeStruct((N_CHIPS, rows, cols), got.dtype),
        compiler_params=_cparams("parallel", "parallel"),
    )(c_idx, g.reshape(N_DEV, rows, cols), got.reshape(N_CHIPS, rows, cols))
    return out.reshape(got.shape)


def _chip_sum_call(part, got, chip_idx, name):
    rows, cols = _rows_cols(got.shape[1:])
    tr = _pick_tile(rows, (512, 256, 128, 64, 32, 16))

    def body(p_ref, a_ref, b_ref, o_ref):
        acc = a_ref[0].astype(F32)
        for k in range(3):
            acc = acc + b_ref[k].astype(F32)
        o_ref[...] = acc

    out = pl.pallas_call(
        body,
        name=name,
        grid_spec=pltpu.PrefetchScalarGridSpec(
            num_scalar_prefetch=1, grid=(rows // tr,),
            in_specs=[pl.BlockSpec((1, tr, cols), lambda i, p_ref: (p_ref[0], i, 0)),
                      pl.BlockSpec((3, tr, cols), lambda i, p_ref: (0, i, 0))],
            out_specs=pl.BlockSpec((tr, cols), lambda i, p_ref: (i, 0))),
        out_shape=jax.ShapeDtypeStruct((rows, cols), F32),
        compiler_params=_cparams("parallel"),
    )(chip_idx, part.reshape(N_CHIPS, rows, cols), got.reshape(3, rows, cols))
    return out.reshape(got.shape[1:])


def _reduce_scatter(gs, name):
    x, y, c = _place()
    c_idx = c.astype(jnp.int32).reshape(1)
    chip_idx = (2 * x + y).astype(jnp.int32).reshape(1)
    gots = _pair_exchange_call(gs, name + "_pair")
    parts = [_pair_sum_call(g, got, c_idx, name + "_pair_sum") for g, got in zip(gs, gots)]
    gots = _chip_exchange_call(parts, name + "_chip")
    return [_chip_sum_call(p, got, chip_idx, name + "_chip_sum") for p, got in zip(parts, gots)]


_HBM = pl.BlockSpec(memory_space=pltpu.HBM)
_SEM = pl.BlockSpec(memory_space=pltpu.SEMAPHORE)
_SIDE_EFFECT = pltpu.SideEffectType.DATAFLOW_SIDE_EFFECTING
N_PEERS = N_DEV - 1


def _peer(k):
    x, y, c = _place()
    px, py, pc = x ^ ((k >> 2) & 1), y ^ ((k >> 1) & 1), c ^ (k & 1)
    return (px, py, pc), 4 * px + 2 * py + pc


def _exchange_copy(src_ref, land_ref, send_sems, recv_sems, i, k, scatter, receiving):
    x, y, c = _place()
    me = 4 * x + 2 * y + c
    peer, peer_idx = _peer(k)
    sem = i * N_PEERS + k - 1
    return pltpu.make_async_remote_copy(
        src_ref=src_ref.at[peer_idx] if scatter else src_ref, dst_ref=land_ref.at[peer_idx if receiving else me],
        send_sem=send_sems.at[sem], recv_sem=recv_sems.at[sem], device_id=peer, device_id_type=MESH_ID)


def _exchange_start_call(srcs, after, scatter, name):
    n = len(srcs)
    slot_shapes = [s.shape[1:] if scatter else s.shape for s in srcs]

    def body(*refs):
        src_refs, land_refs = refs[:n], refs[n:2 * n]
        send_sems, recv_sems = refs[2 * n + 1], refs[2 * n + 2]
        token = refs[-1]
        for i in range(n):
            for k in range(1, N_DEV):
                _exchange_copy(src_refs[i], land_refs[i], send_sems, recv_sems, i, k, scatter, False).start()
        token[...] = jnp.zeros_like(token)

    lands = [pltpu.with_memory_space_constraint(lax.empty((N_DEV,) + shp, s.dtype), pltpu.HBM)
             for shp, s in zip(slot_shapes, srcs)]
    out = pl.pallas_call(
        body,
        name=name,
        out_shape=([pltpu.SemaphoreType.DMA((n * N_PEERS,)), pltpu.SemaphoreType.DMA((n * N_PEERS,))]
                   + [pltpu.HBM(s.shape, s.dtype) for s in srcs] + [pltpu.HBM(l.shape, l.dtype) for l in lands]
                   + [jax.ShapeDtypeStruct((SUBLANES, LANES), F32)]),
        in_specs=[_HBM] * (2 * n) + [pl.BlockSpec(memory_space=pl.ANY)],
        out_specs=[_SEM, _SEM] + [_HBM] * (2 * n) + [pl.BlockSpec(memory_space=pltpu.VMEM)],
        input_output_aliases={j: 2 + j for j in range(2 * n)},
        compiler_params=pltpu.CompilerParams(has_side_effects=_SIDE_EFFECT),
    )(*[pltpu.with_memory_space_constraint(s, pltpu.HBM) for s in srcs], *lands, after)
    return out[0], out[1], list(out[2:2 + n]), list(out[2 + n:2 + 2 * n]), out[-1]


def _exchange_wait_call(started, after, scatter, name):
    send_sems, recv_sems, srcs, lands, _ = started
    n = len(srcs)

    def body(*refs):
        src_refs, land_refs = refs[:n], refs[n:2 * n]
        send_s, recv_s = refs[2 * n], refs[2 * n + 1]
        for i in range(n):
            for k in range(1, N_DEV):
                cp = _exchange_copy(src_refs[i], land_refs[i], send_s, recv_s, i, k, scatter, True)
                cp.wait_send()
                cp.wait_recv()

    out = pl.pallas_call(
        body,
        name=name,
        out_shape=[pltpu.HBM(s.shape, s.dtype) for s in srcs] + [pltpu.HBM(l.shape, l.dtype) for l in lands],
        in_specs=[_HBM] * (2 * n) + [_SEM, _SEM, pl.BlockSpec(memory_space=pl.ANY)],
        out_specs=[_HBM] * (2 * n),
        input_output_aliases={j: j for j in range(2 * n)},
        compiler_params=pltpu.CompilerParams(has_side_effects=_SIDE_EFFECT),
    )(*srcs, *lands, send_sems, recv_sems, after)
    return list(out[n:])


def _own_slot(land, own):
    x, y, c = _place()
    return lax.dynamic_update_index_in_dim(land, own, 4 * x + 2 * y + c, 0)


def _slot_sum_call(land, name):
    rows, cols = _rows_cols(land.shape[1:])
    tr = _pick_tile(rows, (256, 128, 64, 32, 16))

    def body(land_ref, o_ref):
        acc = land_ref[0].astype(F32)
        for s in range(1, N_DEV):
            acc = acc + land_ref[s].astype(F32)
        o_ref[...] = acc

    out = pl.pallas_call(
        body,
        name=name,
        grid=(rows // tr,),
        in_specs=[pl.BlockSpec((N_DEV, tr, cols), lambda i: (0, i, 0))],
        out_specs=pl.BlockSpec((tr, cols), lambda i: (i, 0)),
        out_shape=jax.ShapeDtypeStruct((rows, cols), F32),
        compiler_params=_cparams("parallel"),
    )(land.reshape(N_DEV, rows, cols))
    return out.reshape(land.shape[1:])


SMALL_BLOCK_ROWS = 16


def _pack_small(parts):
    flat = jnp.concatenate([p.reshape(-1) for p in parts])
    chunk = N_DEV * SMALL_BLOCK_ROWS * COMM_LANES
    flat = jnp.pad(flat, (0, (-flat.shape[0]) % chunk))
    return flat.reshape(N_DEV, -1, COMM_LANES)


def _unpack_small(buf, shapes):
    flat = buf.reshape(-1)
    out, off = [], 0
    for shp in shapes:
        n = math.prod(shp)
        out.append(flat[off:off + n].reshape(shp))
        off += n
    return out


SHARDED = dict(w_in=2, ssm_w_glu=2, mla_w_uq=2, mla_w_ukv=2, mla_w_o=2, hg_w_o=2, w_out=1, x_w_q=1, x_w_kv=1,
               x_w_o=2, ffn_w_gate_up=2, ffn_w_down=1)
REPLICATED = ("norm_mix", "ssm_lam_re", "ssm_lam_im", "ssm_b_re", "ssm_b_im", "ssm_c_re", "ssm_c_im", "ssm_d",
              "ssm_log_step", "mla_q_norm", "mla_kv_norm", "hg_lb", "hg_g_norm", "norm_cross", "norm_mem", "norm_ffn",
              "norm_final")


def _join_shards(stacked, axis):
    n, l, a, b = stacked.shape
    if axis == 1:
        return stacked.transpose(1, 0, 2, 3).reshape(l, n * a, b)
    return stacked.transpose(1, 2, 0, 3).reshape(l, a, n * b)


def _split_shards(full, axis):
    l, a, b = full.shape
    if axis == 1:
        return full.reshape(l, N_DEV, a // N_DEV, b).transpose(1, 0, 2, 3)
    return full.reshape(l, a, N_DEV, b // N_DEV).transpose(2, 0, 1, 3)


MIXER_WEIGHTS = ("w_in", "ssm_w_glu", "mla_w_uq", "mla_w_ukv", "mla_w_o", "hg_w_o", "w_out")
TAIL_WEIGHTS = ("x_w_q", "x_w_kv", "x_w_o", "ffn_w_gate_up", "ffn_w_down")


def _mixer_fn(l, tabs):
    def f(x, full, small, lower):
        m = _mixer_matrices(dict(zip(MIXER_WEIGHTS, full)), 0)
        return _mixer(x, tabs, m, dict(zip(REPLICATED, small)), l, lower[l].reshape(1, -1))
    return f


def _tail_fn(l, mem):
    def f(x, full, small):
        return _tail(x, mem, _tail_matrices(dict(zip(TAIL_WEIGHTS, full)), 0), dict(zip(REPLICATED, small)), l)
    return f


ADAM_LR, ADAM_B1, ADAM_B2, ADAM_EPS, ADAM_WD, ADAM_STEP = 0.001, 0.9, 0.999, 1e-08, 0.01, 10


def _adamw_update(w, g, m, v):
    m_new = ADAM_B1 * m + (1.0 - ADAM_B1) * g
    v_new = ADAM_B2 * v + (1.0 - ADAM_B2) * jnp.square(g)
    m_hat = m_new / (1.0 - ADAM_B1 ** ADAM_STEP)
    v_hat = v_new / (1.0 - ADAM_B2 ** ADAM_STEP)
    return -ADAM_LR * (m_hat / (jnp.sqrt(v_hat) + ADAM_EPS) + ADAM_WD * w), m_new, v_new


def _adamw_stacked_call(w, g, m, v, name):
    depth, rows, cols = w.shape
    tr = _pick_tile(rows, (512, 256, 128, 64, 32, 16, 8))

    def body(w_ref, g_ref, m_ref, v_ref, d_ref, nm_ref, nv_ref):
        d_ref[...], nm_ref[...], nv_ref[...] = _adamw_update(w_ref[...], g_ref[...], m_ref[...], v_ref[...])

    spec = pl.BlockSpec((None, tr, cols), lambda l, i: (l, i, 0))
    return tuple(pl.pallas_call(
        body, name=name, grid=(depth, rows // tr), in_specs=[spec] * 4, out_specs=[spec] * 3,
        out_shape=[jax.ShapeDtypeStruct(w.shape, F32)] * 3, compiler_params=_cparams("parallel", "parallel"),
    )(w, g, m, v))


def _adamw_call(w, g, m, v, name):
    shape = w.shape
    if len(shape) == 3:
        return _adamw_stacked_call(w, g, m, v, name)
    cols = shape[-1]
    rows = math.prod(shape[:-1]) if len(shape) > 1 else 1
    tr = _pick_tile(rows, (512, 256, 128, 64, 32, 16, 8))

    def body(w_ref, g_ref, m_ref, v_ref, d_ref, nm_ref, nv_ref):
        d_ref[...], nm_ref[...], nv_ref[...] = _adamw_update(w_ref[...], g_ref[...], m_ref[...], v_ref[...])

    spec = pl.BlockSpec((tr, cols), lambda i: (i, 0))
    outs = pl.pallas_call(
        body, name=name, grid=(rows // tr,), in_specs=[spec] * 4, out_specs=[spec] * 3,
        out_shape=[jax.ShapeDtypeStruct((rows, cols), F32)] * 3, compiler_params=_cparams("parallel"),
    )(*(t.reshape(rows, cols) for t in (w, g, m, v)))
    return tuple(o.reshape(shape) for o in outs)


WEIGHTS = ("norm_mix", "w_in", "ssm_lam_re", "ssm_lam_im", "ssm_b_re", "ssm_b_im", "ssm_c_re", "ssm_c_im", "ssm_d",
           "ssm_log_step", "ssm_w_glu", "mla_q_norm", "mla_kv_norm", "mla_w_uq", "mla_w_ukv", "mla_w_o", "hg_lb",
           "hg_g_norm", "hg_w_o", "w_out", "norm_cross", "norm_mem", "x_w_q", "x_w_kv", "x_w_o", "norm_ffn",
           "ffn_w_gate_up", "ffn_w_down", "norm_final")


def kernel(x, mem, positions, norm_mix, w_in, ssm_lam_re, ssm_lam_im, ssm_b_re, ssm_b_im, ssm_c_re, ssm_c_im, ssm_d, ssm_log_step, ssm_w_glu, mla_q_norm, mla_kv_norm, mla_w_uq, mla_w_ukv, mla_w_o, hg_lb, hg_g_norm, hg_w_o, w_out, norm_cross, norm_mem, x_w_q, x_w_kv, x_w_o, norm_ffn, ffn_w_gate_up, ffn_w_down, norm_final, loss_target, m_norm_mix, m_w_in, m_ssm_lam_re, m_ssm_lam_im, m_ssm_b_re, m_ssm_b_im, m_ssm_c_re, m_ssm_c_im, m_ssm_d, m_ssm_log_step, m_ssm_w_glu, m_mla_q_norm, m_mla_kv_norm, m_mla_w_uq, m_mla_w_ukv, m_mla_w_o, m_hg_lb, m_hg_g_norm, m_hg_w_o, m_w_out, m_norm_cross, m_norm_mem, m_x_w_q, m_x_w_kv, m_x_w_o, m_norm_ffn, m_ffn_w_gate_up, m_ffn_w_down, m_norm_final, v_norm_mix, v_w_in, v_ssm_lam_re, v_ssm_lam_im, v_ssm_b_re, v_ssm_b_im, v_ssm_c_re, v_ssm_c_im, v_ssm_d, v_ssm_log_step, v_ssm_w_glu, v_mla_q_norm, v_mla_kv_norm, v_mla_w_uq, v_mla_w_ukv, v_mla_w_o, v_hg_lb, v_hg_g_norm, v_hg_w_o, v_w_out, v_norm_cross, v_norm_mem, v_x_w_q, v_x_w_kv, v_x_w_o, v_norm_ffn, v_ffn_w_gate_up, v_ffn_w_down, v_norm_final):
    given = dict(locals())
    weights = {n: given[n] for n in WEIGHTS}
    small = tuple(weights[n] for n in REPLICATED)
    layer1 = MIXER_WEIGHTS + TAIL_WEIGHTS
    shards = lambda names, l: [weights[n][l:l + 1].astype(BF16) for n in names]
    join = lambda names, stacked: tuple(_join_shards(p, SHARDED[n]) for n, p in zip(names, stacked))
    split = lambda names, cts: [_split_shards(ct, SHARDED[n]) for n, ct in zip(names, cts)]
    landed = lambda names, lands, own: join(names, [_own_slot(land, o) for land, o in zip(lands, own)])
    xs, tabs = x[0], _rope_tables(positions[0])
    lower, vjp_lower = jax.vjp(_lower_bounds, hg_lb)
    me = 4 * lax.axis_index("x") + 2 * lax.axis_index("y") + lax.axis_index("c")

    got_m0 = _all_gather_call(shards(MIXER_WEIGHTS, 0), "weights_all_gather_m0")
    own_t0, own_l1 = shards(TAIL_WEIGHTS, 0), shards(layer1, 1)
    gather_t0 = _exchange_start_call(own_t0, got_m0[0], False, "weights_gather_start_t0")
    gather_l1 = _exchange_start_call(own_l1, gather_t0[4], False, "weights_gather_start_l1")
    xs = xs + gather_l1[4][0, 0]
    xa0, vjp_m0 = jax.vjp(_mixer_fn(0, tabs), xs, join(MIXER_WEIGHTS, got_m0), small, lower)
    full_t0 = landed(TAIL_WEIGHTS, _exchange_wait_call(gather_t0, xa0, False, "weights_gather_wait_t0"), own_t0)
    x1, vjp_t0 = jax.vjp(_tail_fn(0, mem[0]), xa0, full_t0, small)
    full_l1 = landed(layer1, _exchange_wait_call(gather_l1, x1, False, "weights_gather_wait_l1"), own_l1)
    xa1, vjp_m1 = jax.vjp(_mixer_fn(1, tabs), x1, full_l1[:len(MIXER_WEIGHTS)], small, lower)
    x2, vjp_t1 = jax.vjp(_tail_fn(1, mem[0]), xa1, full_l1[len(MIXER_WEIGHTS):], small)
    loss_local, vjp_loss = jax.vjp(functools.partial(_final_loss, loss_target[0]), x2, norm_final)

    def scatter_start(names, cts, dx, tag):
        gs = split(names, cts)
        started = _exchange_start_call(gs, dx, True, "grads_scatter_start_" + tag)
        return (started, gs), dx + started[4][0, 0]

    def scatter_finish(pending, after, tag):
        started, gs = pending
        lands = _exchange_wait_call(started, after, True, "grads_scatter_wait_" + tag)
        return [_slot_sum_call(_own_slot(land, lax.dynamic_index_in_dim(g, me, 0, keepdims=False)), "grads_slot_sum_" + tag)
                for land, g in zip(lands, gs)]

    dx2, d_norm_final = vjp_loss(jnp.ones((), F32))
    dxa1, dfull_t1, dsmall_t1 = vjp_t1(dx2)
    dx1, dfull_m1, dsmall_m1, dlower1 = vjp_m1(dxa1)
    pend_l1, dx1 = scatter_start(layer1, dfull_m1 + dfull_t1, dx1, "l1")
    dxa0, dfull_t0, dsmall_t0 = vjp_t0(dx1)
    pend_t0, dxa0 = scatter_start(TAIL_WEIGHTS, dfull_t0, dxa0, "t0")
    gx, dfull_m0, dsmall_m0, dlower0 = vjp_m0(dxa0)
    by_layer = {
        0: dict(zip(MIXER_WEIGHTS + TAIL_WEIGHTS,
                    _reduce_scatter(split(MIXER_WEIGHTS, dfull_m0), "grads_reduce_scatter_m0") + scatter_finish(pend_t0, gx, "t0"))),
        1: dict(zip(layer1, scatter_finish(pend_l1, gx, "l1")))}
    grads = {n: jnp.concatenate([by_layer[0][n], by_layer[1][n]], axis=0) for n in SHARDED}

    d_small = dict(zip(REPLICATED, (a + b + c + d for a, b, c, d in zip(dsmall_m0, dsmall_t0, dsmall_m1, dsmall_t1))))
    d_small["norm_final"] = d_small["norm_final"] + d_norm_final
    d_small["hg_lb"] = d_small["hg_lb"] + vjp_lower(dlower0 + dlower1)[0]
    shapes = [d_small[n].shape for n in REPLICATED]
    (mine,) = _reduce_scatter([_pack_small([d_small[n] for n in REPLICATED])], "small_reduce_scatter")
    (total,) = _all_gather_call([mine], "small_all_gather")
    grads.update(zip(REPLICATED, _unpack_small(total, shapes)))

    loss = lax.psum(loss_local, ("x", "y", "c"))
    steps = {n: _adamw_call(weights[n], grads[n], given["m_" + n], given["v_" + n], "adamw_" + n) for n in WEIGHTS}
    return (loss, gx[None], *[grads[n] for n in WEIGHTS], *[steps[n][0] for n in WEIGHTS],
            *[steps[n][1] for n in WEIGHTS], *[steps[n][2] for n in WEIGHTS])
```
